```python
import jax, jax.numpy as jnp
from jax import lax
import numpy as np

D_MODEL = 1024
BATCH = 8
SEQ = 8192
DEPTH = 2

HEAD_DIM = 64
A_GROUPS = ((128, 1), (512, 4), (2048, 16))
N_GROUPS = len(A_GROUPS)
A_WIDTH = D_MODEL // 2
A_HEADS = A_WIDTH // HEAD_DIM
B_WIDTH = D_MODEL // 2
SC_WIDTH = 3
POOL_SIZES = (2, 4, 8, 16)
C_WIDTH = D_MODEL // 2
C_GROUP = C_WIDTH // len(POOL_SIZES)
D_WIDTH = D_MODEL // 2
D_CONV = 31
GATE_EVEN = A_WIDTH + B_WIDTH
GATE_ODD = C_WIDTH + D_WIDTH
EVEN_IN = 3 * N_GROUPS * A_WIDTH + 3 * B_WIDTH + GATE_EVEN
ODD_IN = C_WIDTH + 2 * D_WIDTH + GATE_ODD
ROT_DIM = HEAD_DIM // 4
ROPE_THETA = 500000.0
QBLK = 128
EPS = 1e-6
NEG = -1e30
N_EVEN = (DEPTH + 1) // 2
N_ODD = DEPTH // 2

kernel_name = "hybrid_dilated_attn_shortconv_pool_conformer"


def rms_norm(t, w):
    tf = t.astype(jnp.float32)
    tf = tf * lax.rsqrt(jnp.mean(tf * tf, axis=-1, keepdims=True) + EPS)
    return (tf * w.astype(jnp.float32)).astype(t.dtype)


def layer_norm(t, w, b):
    tf = t.astype(jnp.float32)
    mu = jnp.mean(tf, axis=-1, keepdims=True)
    var = jnp.mean(jnp.square(tf - mu), axis=-1, keepdims=True)
    y = (tf - mu) * lax.rsqrt(var + EPS)
    return (y * w.astype(jnp.float32) + b.astype(jnp.float32)).astype(t.dtype)


def rope_tables(positions):
    half = ROT_DIM // 2
    inv_freq = ROPE_THETA ** (-jnp.arange(half, dtype=jnp.float32) / half)
    ang = positions.astype(jnp.float32)[..., None] * inv_freq
    return jnp.cos(ang)[:, :, None, None, :], jnp.sin(ang)[:, :, None, None, :]


def apply_rope(t, cos, sin):
    half = ROT_DIM // 2
    t1 = t[..., :half].astype(jnp.float32)
    t2 = t[..., half:ROT_DIM].astype(jnp.float32)
    r1 = (t1 * cos - t2 * sin).astype(t.dtype)
    r2 = (t2 * cos + t1 * sin).astype(t.dtype)
    return jnp.concatenate([r1, r2, t[..., ROT_DIM:]], axis=-1)


def causal_dwconv(t, w):
    K, C = w.shape
    return lax.conv_general_dilated(
        t, w[:, None, :].astype(t.dtype), window_strides=(1,), padding=[(K - 1, 0)],
        dimension_numbers=('NWC', 'WIO', 'NWC'), feature_group_count=C)


def dilated_attention(q, k, v, window, dilation):
    Bsz, S, H, Dh = q.shape
    steps = window // dilation
    span = dilation * QBLK
    L = -(-S // span) * span
    n = L // dilation
    nb = n // QBLK

    def to_streams(t):
        t = jnp.pad(t, ((0, 0), (0, L - S), (0, 0), (0, 0)))
        t = t.reshape(Bsz, n, dilation, H, Dh).transpose(0, 2, 3, 1, 4)
        return t.reshape(Bsz, dilation, H, nb, QBLK, Dh)

    def with_prev(t):
        prev = jnp.pad(t, ((0, 0), (0, 0), (0, 0), (1, 0), (0, 0), (0, 0)))[:, :, :, :-1]
        return jnp.concatenate([prev, t], axis=-2)

    qb = to_streams(q)
    kc = with_prev(to_streams(k))
    vc = with_prev(to_streams(v))

    s = jnp.einsum('brhnqc,brhnkc->brhnqk', qb, kc).astype(jnp.float32) * (Dh ** -0.5)
    qi = jnp.arange(QBLK)[:, None] + QBLK
    kj = jnp.arange(2 * QBLK)[None, :]
    dist = qi - kj
    band = (dist >= 0) & (dist <= steps)
    blk = jnp.arange(nb)[:, None, None]
    mask = band[None] & ((blk > 0) | (kj[None] >= QBLK))
    s = jnp.where(mask, s, NEG)
    m = jnp.max(s, axis=-1, keepdims=True)
    p = jnp.exp(s - m)
    den = jnp.sum(p, axis=-1, keepdims=True)
    o = jnp.einsum('brhnqk,brhnkc->brhnqc', (p / den).astype(v.dtype), vc)
    lse = (m + jnp.log(den))[..., 0]

    o = o.reshape(Bsz, dilation, H, n, Dh).transpose(0, 3, 1, 2, 4).reshape(Bsz, L, H, Dh)[:, :S]
    lse = lse.reshape(Bsz, dilation, H, n).transpose(0, 3, 1, 2).reshape(Bsz, L, H)[:, :S]
    return o, lse


def causal_pool_minus_self(u):
    S = u.shape[1]
    cs = jnp.cumsum(u.astype(jnp.float32), axis=1)
    t = jnp.arange(S)
    outs = []
    for g, p in enumerate(POOL_SIZES):
        c = cs[:, :, g]
        prev = jnp.pad(c, ((0, 0), (p, 0), (0, 0)))[:, :S]
        cnt = jnp.minimum(t + 1, p).astype(jnp.float32)[None, :, None]
        outs.append((c - prev) / cnt - u[:, :, g].astype(jnp.float32))
    return jnp.stack(outs, axis=2).astype(u.dtype)


def even_layer(x, cos, sin, norm_w, w_in, q_norm_w, k_norm_w, conv_w, w_out):
    Bsz, S, _ = x.shape
    h = rms_norm(x, norm_w)
    proj = h @ w_in
    nA = N_GROUPS * A_WIDTH
    cuts = np.cumsum([nA, nA, nA, B_WIDTH, B_WIDTH, B_WIDTH]).tolist()
    q, k, v, bg, cg, hb, z = jnp.split(proj, cuts, axis=-1)
    shp = (Bsz, S, N_GROUPS, A_HEADS, HEAD_DIM)
    q = apply_rope(rms_norm(q.reshape(shp), q_norm_w), cos, sin)
    k = apply_rope(rms_norm(k.reshape(shp), k_norm_w), cos, sin)
    v = v.reshape(shp)
    outs, lses = [], []
    for g, (win, dil) in enumerate(A_GROUPS):
        o, l = dilated_attention(q[:, :, g], k[:, :, g], v[:, :, g], win, dil)
        outs.append(o)
        lses.append(l)
    wts = jax.nn.softmax(jnp.stack(lses, axis=0), axis=0)
    o_a = jnp.sum(wts[..., None] * jnp.stack(outs, axis=0).astype(jnp.float32), axis=0)
    o_a = o_a.astype(x.dtype).reshape(Bsz, S, A_WIDTH)
    y_b = bg * causal_dwconv(cg * hb, conv_w)
    u = jnp.concatenate([o_a, y_b], axis=-1) * jax.nn.silu(z)
    return x + u @ w_out


def odd_layer(x, norm_w, w_in, pool_w, pool_scale, dconv_w, dconv_b, ln_w, ln_b, w_out):
    Bsz, S, _ = x.shape
    h = rms_norm(x, norm_w)
    proj = h @ w_in
    cuts = np.cumsum([C_WIDTH, D_WIDTH, D_WIDTH]).tolist()
    uc, da, dg, z = jnp.split(proj, cuts, axis=-1)
    pooled = causal_pool_minus_self(uc.reshape(Bsz, S, len(POOL_SIZES), C_GROUP))
    y_c = jnp.einsum('bsgc,gcd->bsgd', pooled, pool_w).reshape(Bsz, S, C_WIDTH) * pool_scale
    gl = da * jax.nn.sigmoid(dg)
    c = causal_dwconv(gl, dconv_w) + dconv_b
    y_d = jax.nn.silu(layer_norm(c, ln_w, ln_b))
    u = jnp.concatenate([y_c, y_d], axis=-1) * jax.nn.silu(z)
    return x + u @ w_out


def _fwd_setup_inputs(seed: int = 0) -> dict:
    key = jax.random.key(seed)
    ks = jax.random.split(key, 20)
    f32 = jnp.float32
    nrm = lambda k, shape, scale: jax.random.normal(k, shape, f32) * scale
    x = jax.random.normal(ks[0], (BATCH, SEQ, D_MODEL), f32)
    offset = jax.random.randint(ks[1], (BATCH, 1), 0, 4096, dtype=jnp.int32)
    positions = offset + jnp.arange(SEQ, dtype=jnp.int32)[None, :]
    return {
        "x": x,
        "positions": positions,
        "e_norm_w": 1.0 + nrm(ks[2], (N_EVEN, D_MODEL), 0.02),
        "e_w_in": nrm(ks[3], (N_EVEN, D_MODEL, EVEN_IN), D_MODEL ** -0.5),
        "e_q_norm_w": 1.0 + nrm(ks[4], (N_EVEN, HEAD_DIM), 0.02),
        "e_k_norm_w": 1.0 + nrm(ks[5], (N_EVEN, HEAD_DIM), 0.02),
        "e_conv_w": nrm(ks[6], (N_EVEN, SC_WIDTH, B_WIDTH), SC_WIDTH ** -0.5),
        "e_w_out": nrm(ks[7], (N_EVEN, GATE_EVEN, D_MODEL), GATE_EVEN ** -0.5),
        "o_norm_w": 1.0 + nrm(ks[8], (N_ODD, D_MODEL), 0.02),
        "o_w_in": nrm(ks[9], (N_ODD, D_MODEL, ODD_IN), D_MODEL ** -0.5),
        "o_pool_w": nrm(ks[10], (N_ODD, len(POOL_SIZES), C_GROUP, C_GROUP), C_GROUP ** -0.5),
        "o_pool_scale": 1.0 + nrm(ks[11], (N_ODD, C_WIDTH), 0.02),
        "o_dconv_w": nrm(ks[12], (N_ODD, D_CONV, D_WIDTH), D_CONV ** -0.5),
        "o_dconv_b": nrm(ks[13], (N_ODD, D_WIDTH), 0.01),
        "o_ln_w": 1.0 + nrm(ks[14], (N_ODD, D_WIDTH), 0.02),
        "o_ln_b": nrm(ks[15], (N_ODD, D_WIDTH), 0.01),
        "o_w_out": nrm(ks[16], (N_ODD, GATE_ODD, D_MODEL), GATE_ODD ** -0.5),
    }


def _fwd_reference(x, positions, e_norm_w, e_w_in, e_q_norm_w, e_k_norm_w, e_conv_w, e_w_out,
              o_norm_w, o_w_in, o_pool_w, o_pool_scale, o_dconv_w, o_dconv_b, o_ln_w, o_ln_b,
              o_w_out):
    cos, sin = rope_tables(positions)
    for i in range(DEPTH):
        j = i // 2
        if i % 2 == 0:
            x = even_layer(x, cos, sin, e_norm_w[j], e_w_in[j], e_q_norm_w[j], e_k_norm_w[j],
                           e_conv_w[j], e_w_out[j])
        else:
            x = odd_layer(x, o_norm_w[j], o_w_in[j], o_pool_w[j], o_pool_scale[j], o_dconv_w[j],
                          o_dconv_b[j], o_ln_w[j], o_ln_b[j], o_w_out[j])
    return x


import jax as _jax
import jax.numpy as _jnp

TWIN_FORMAT = 'train_step'
FWD_PARAMS = ['x', 'positions', 'e_norm_w', 'e_w_in', 'e_q_norm_w', 'e_k_norm_w', 'e_conv_w', 'e_w_out', 'o_norm_w', 'o_w_in', 'o_pool_w', 'o_pool_scale', 'o_dconv_w', 'o_dconv_b', 'o_ln_w', 'o_ln_b', 'o_w_out']
TWIN_WEIGHTS = ['e_norm_w', 'e_w_in', 'e_q_norm_w', 'e_k_norm_w', 'e_conv_w', 'e_w_out', 'o_norm_w', 'o_w_in', 'o_pool_w', 'o_pool_scale', 'o_dconv_w', 'o_dconv_b', 'o_ln_w', 'o_ln_b', 'o_w_out']
TWIN_DIFF_INPUT = 'x'
TWIN_INPUTS = ['x', 'positions', 'e_norm_w', 'e_w_in', 'e_q_norm_w', 'e_k_norm_w', 'e_conv_w', 'e_w_out', 'o_norm_w', 'o_w_in', 'o_pool_w', 'o_pool_scale', 'o_dconv_w', 'o_dconv_b', 'o_ln_w', 'o_ln_b', 'o_w_out', 'loss_target', 'm_e_norm_w', 'm_e_w_in', 'm_e_q_norm_w', 'm_e_k_norm_w', 'm_e_conv_w', 'm_e_w_out', 'm_o_norm_w', 'm_o_w_in', 'm_o_pool_w', 'm_o_pool_scale', 'm_o_dconv_w', 'm_o_dconv_b', 'm_o_ln_w', 'm_o_ln_b', 'm_o_w_out', 'v_e_norm_w', 'v_e_w_in', 'v_e_q_norm_w', 'v_e_k_norm_w', 'v_e_conv_w', 'v_e_w_out', 'v_o_norm_w', 'v_o_w_in', 'v_o_pool_w', 'v_o_pool_scale', 'v_o_dconv_w', 'v_o_dconv_b', 'v_o_ln_w', 'v_o_ln_b', 'v_o_w_out']
TWIN_OUTPUTS = ['loss', 'grad_x', 'grad_e_norm_w', 'grad_e_w_in', 'grad_e_q_norm_w', 'grad_e_k_norm_w', 'grad_e_conv_w', 'grad_e_w_out', 'grad_o_norm_w', 'grad_o_w_in', 'grad_o_pool_w', 'grad_o_pool_scale', 'grad_o_dconv_w', 'grad_o_dconv_b', 'grad_o_ln_w', 'grad_o_ln_b', 'grad_o_w_out', 'delta_e_norm_w', 'delta_e_w_in', 'delta_e_q_norm_w', 'delta_e_k_norm_w', 'delta_e_conv_w', 'delta_e_w_out', 'delta_o_norm_w', 'delta_o_w_in', 'delta_o_pool_w', 'delta_o_pool_scale', 'delta_o_dconv_w', 'delta_o_dconv_b', 'delta_o_ln_w', 'delta_o_ln_b', 'delta_o_w_out', 'new_m_e_norm_w', 'new_m_e_w_in', 'new_m_e_q_norm_w', 'new_m_e_k_norm_w', 'new_m_e_conv_w', 'new_m_e_w_out', 'new_m_o_norm_w', 'new_m_o_w_in', 'new_m_o_pool_w', 'new_m_o_pool_scale', 'new_m_o_dconv_w', 'new_m_o_dconv_b', 'new_m_o_ln_w', 'new_m_o_ln_b', 'new_m_o_w_out', 'new_v_e_norm_w', 'new_v_e_w_in', 'new_v_e_q_norm_w', 'new_v_e_k_norm_w', 'new_v_e_conv_w', 'new_v_e_w_out', 'new_v_o_norm_w', 'new_v_o_w_in', 'new_v_o_pool_w', 'new_v_o_pool_scale', 'new_v_o_dconv_w', 'new_v_o_dconv_b', 'new_v_o_ln_w', 'new_v_o_ln_b', 'new_v_o_w_out']
TWIN_LEAF_KINDS = {'loss': 'loss', 'grad_x': 'grad_x', 'grad_e_norm_w': 'grad_w', 'grad_e_w_in': 'grad_w', 'grad_e_q_norm_w': 'grad_w', 'grad_e_k_norm_w': 'grad_w', 'grad_e_conv_w': 'grad_w', 'grad_e_w_out': 'grad_w', 'grad_o_norm_w': 'grad_w', 'grad_o_w_in': 'grad_w', 'grad_o_pool_w': 'grad_w', 'grad_o_pool_scale': 'grad_w', 'grad_o_dconv_w': 'grad_w', 'grad_o_dconv_b': 'grad_w', 'grad_o_ln_w': 'grad_w', 'grad_o_ln_b': 'grad_w', 'grad_o_w_out': 'grad_w', 'delta_e_norm_w': 'delta_w', 'delta_e_w_in': 'delta_w', 'delta_e_q_norm_w': 'delta_w', 'delta_e_k_norm_w': 'delta_w', 'delta_e_conv_w': 'delta_w', 'delta_e_w_out': 'delta_w', 'delta_o_norm_w': 'delta_w', 'delta_o_w_in': 'delta_w', 'delta_o_pool_w': 'delta_w', 'delta_o_pool_scale': 'delta_w', 'delta_o_dconv_w': 'delta_w', 'delta_o_dconv_b': 'delta_w', 'delta_o_ln_w': 'delta_w', 'delta_o_ln_b': 'delta_w', 'delta_o_w_out': 'delta_w', 'new_m_e_norm_w': 'new_m', 'new_m_e_w_in': 'new_m', 'new_m_e_q_norm_w': 'new_m', 'new_m_e_k_norm_w': 'new_m', 'new_m_e_conv_w': 'new_m', 'new_m_e_w_out': 'new_m', 'new_m_o_norm_w': 'new_m', 'new_m_o_w_in': 'new_m', 'new_m_o_pool_w': 'new_m', 'new_m_o_pool_scale': 'new_m', 'new_m_o_dconv_w': 'new_m', 'new_m_o_dconv_b': 'new_m', 'new_m_o_ln_w': 'new_m', 'new_m_o_ln_b': 'new_m', 'new_m_o_w_out': 'new_m', 'new_v_e_norm_w': 'new_v', 'new_v_e_w_in': 'new_v', 'new_v_e_q_norm_w': 'new_v', 'new_v_e_k_norm_w': 'new_v', 'new_v_e_conv_w': 'new_v', 'new_v_e_w_out': 'new_v', 'new_v_o_norm_w': 'new_v', 'new_v_o_w_in': 'new_v', 'new_v_o_pool_w': 'new_v', 'new_v_o_pool_scale': 'new_v', 'new_v_o_dconv_w': 'new_v', 'new_v_o_dconv_b': 'new_v', 'new_v_o_ln_w': 'new_v', 'new_v_o_ln_b': 'new_v', 'new_v_o_w_out': 'new_v'}


def _forward(args):
    return _fwd_reference(*[args[k] for k in FWD_PARAMS])


def _output_shape():
    out = _jax.eval_shape(lambda: _forward(_fwd_setup_inputs(0)))
    return out.shape, out.dtype

N_MICROBATCH = 1
ADAM_LR = 0.001
ADAM_B1 = 0.9
ADAM_B2 = 0.999
ADAM_EPS = 1e-08
ADAM_WD = 0.01
ADAM_STEP = 10
PER_EXAMPLE_BATCH_AXIS = {'x': 0, 'positions': 0, 'loss_target': 0}
SHARED_INPUTS = []
_WEIGHT_DTYPES = {'e_norm_w': _jnp.float32, 'e_w_in': _jnp.float32, 'e_q_norm_w': _jnp.float32, 'e_k_norm_w': _jnp.float32, 'e_conv_w': _jnp.float32, 'e_w_out': _jnp.float32, 'o_norm_w': _jnp.float32, 'o_w_in': _jnp.float32, 'o_pool_w': _jnp.float32, 'o_pool_scale': _jnp.float32, 'o_dconv_w': _jnp.float32, 'o_dconv_b': _jnp.float32, 'o_ln_w': _jnp.float32, 'o_ln_b': _jnp.float32, 'o_w_out': _jnp.float32}
MOMENT_SCALE = {'e_norm_w': 4.940981e+01, 'e_w_in': 6.673005e-01, 'e_q_norm_w': 6.625133e-01, 'e_k_norm_w': 6.449917e-01, 'e_conv_w': 1.250353e+01, 'e_w_out': 4.139017e-01, 'o_norm_w': 2.374201e+01, 'o_w_in': 3.281546e-01, 'o_pool_w': 1.091258e+00, 'o_pool_scale': 1.782088e+01, 'o_dconv_w': 2.159408e-01, 'o_dconv_b': 1.384975e+00, 'o_ln_w': 9.738736e+00, 'o_ln_b': 6.291983e+00, 'o_w_out': 3.129973e-01}


def _to_microbatches(a, axis):
    t = _jnp.moveaxis(a, axis, 0)
    t = t.reshape((N_MICROBATCH, t.shape[0] // N_MICROBATCH) + t.shape[1:])
    return _jnp.moveaxis(t, 1, axis + 1)


def setup_inputs(seed: int = 0) -> dict:
    inp = _fwd_setup_inputs(seed)
    key = _jax.random.fold_in(_jax.random.key(seed), 7919)
    shape, _ = _output_shape()
    out = dict(inp)
    out["loss_target"] = _jax.random.normal(_jax.random.fold_in(key, 0), shape, _jnp.float32)
    for i, name in enumerate(TWIN_WEIGHTS):
        w = inp[name].astype(_jnp.float32)
        if MOMENT_SCALE is None:
            s = _jnp.sqrt(_jnp.mean(_jnp.square(w)) + 1e-30)
        else:
            s = MOMENT_SCALE[name]
        km, kv = _jax.random.split(_jax.random.fold_in(key, i + 1))
        out[name] = w
        out["m_" + name] = s * _jax.random.normal(km, w.shape, _jnp.float32)
        out["v_" + name] = (s * s) * _jax.random.uniform(kv, w.shape, _jnp.float32, 0.5, 1.5)
    if N_MICROBATCH > 1:
        for name, axis in PER_EXAMPLE_BATCH_AXIS.items():
            out[name] = _to_microbatches(out[name], axis)
    return {'x': out['x'], 'positions': out['positions'], 'e_norm_w': out['e_norm_w'], 'e_w_in': out['e_w_in'], 'e_q_norm_w': out['e_q_norm_w'], 'e_k_norm_w': out['e_k_norm_w'], 'e_conv_w': out['e_conv_w'], 'e_w_out': out['e_w_out'], 'o_norm_w': out['o_norm_w'], 'o_w_in': out['o_w_in'], 'o_pool_w': out['o_pool_w'], 'o_pool_scale': out['o_pool_scale'], 'o_dconv_w': out['o_dconv_w'], 'o_dconv_b': out['o_dconv_b'], 'o_ln_w': out['o_ln_w'], 'o_ln_b': out['o_ln_b'], 'o_w_out': out['o_w_out'], 'loss_target': out['loss_target'], 'm_e_norm_w': out['m_e_norm_w'], 'm_e_w_in': out['m_e_w_in'], 'm_e_q_norm_w': out['m_e_q_norm_w'], 'm_e_k_norm_w': out['m_e_k_norm_w'], 'm_e_conv_w': out['m_e_conv_w'], 'm_e_w_out': out['m_e_w_out'], 'm_o_norm_w': out['m_o_norm_w'], 'm_o_w_in': out['m_o_w_in'], 'm_o_pool_w': out['m_o_pool_w'], 'm_o_pool_scale': out['m_o_pool_scale'], 'm_o_dconv_w': out['m_o_dconv_w'], 'm_o_dconv_b': out['m_o_dconv_b'], 'm_o_ln_w': out['m_o_ln_w'], 'm_o_ln_b': out['m_o_ln_b'], 'm_o_w_out': out['m_o_w_out'], 'v_e_norm_w': out['v_e_norm_w'], 'v_e_w_in': out['v_e_w_in'], 'v_e_q_norm_w': out['v_e_q_norm_w'], 'v_e_k_norm_w': out['v_e_k_norm_w'], 'v_e_conv_w': out['v_e_conv_w'], 'v_e_w_out': out['v_e_w_out'], 'v_o_norm_w': out['v_o_norm_w'], 'v_o_w_in': out['v_o_w_in'], 'v_o_pool_w': out['v_o_pool_w'], 'v_o_pool_scale': out['v_o_pool_scale'], 'v_o_dconv_w': out['v_o_dconv_w'], 'v_o_dconv_b': out['v_o_dconv_b'], 'v_o_ln_w': out['v_o_ln_w'], 'v_o_ln_b': out['v_o_ln_b'], 'v_o_w_out': out['v_o_w_out']}


def _loss(weights, diff, rest, loss_target):
    with _jax.named_scope("forward"):
        args = {**rest, TWIN_DIFF_INPUT: diff, **{k: w.astype(_WEIGHT_DTYPES[k]) for k, w in weights.items()}}
        y = _forward(args)
    with _jax.named_scope("loss_head"):
        err = _jnp.square(y.astype(_jnp.float32) - loss_target)
        return 0.5 * _jnp.sum(_jnp.mean(err, axis=-1)) if err.ndim else 0.5 * err


def _adamw(w, g, m, v):
    m = ADAM_B1 * m + (1.0 - ADAM_B1) * g
    v = ADAM_B2 * v + (1.0 - ADAM_B2) * _jnp.square(g)
    m_hat = m / (1.0 - ADAM_B1 ** ADAM_STEP)
    v_hat = v / (1.0 - ADAM_B2 ** ADAM_STEP)
    delta = -ADAM_LR * (m_hat / (_jnp.sqrt(v_hat) + ADAM_EPS) + ADAM_WD * w)
    return delta, m, v


def reference(x, positions, e_norm_w, e_w_in, e_q_norm_w, e_k_norm_w, e_conv_w, e_w_out, o_norm_w, o_w_in, o_pool_w, o_pool_scale, o_dconv_w, o_dconv_b, o_ln_w, o_ln_b, o_w_out, loss_target, m_e_norm_w, m_e_w_in, m_e_q_norm_w, m_e_k_norm_w, m_e_conv_w, m_e_w_out, m_o_norm_w, m_o_w_in, m_o_pool_w, m_o_pool_scale, m_o_dconv_w, m_o_dconv_b, m_o_ln_w, m_o_ln_b, m_o_w_out, v_e_norm_w, v_e_w_in, v_e_q_norm_w, v_e_k_norm_w, v_e_conv_w, v_e_w_out, v_o_norm_w, v_o_w_in, v_o_pool_w, v_o_pool_scale, v_o_dconv_w, v_o_dconv_b, v_o_ln_w, v_o_ln_b, v_o_w_out):
    given = dict(x=x, positions=positions, e_norm_w=e_norm_w, e_w_in=e_w_in, e_q_norm_w=e_q_norm_w, e_k_norm_w=e_k_norm_w, e_conv_w=e_conv_w, e_w_out=e_w_out, o_norm_w=o_norm_w, o_w_in=o_w_in, o_pool_w=o_pool_w, o_pool_scale=o_pool_scale, o_dconv_w=o_dconv_w, o_dconv_b=o_dconv_b, o_ln_w=o_ln_w, o_ln_b=o_ln_b, o_w_out=o_w_out, loss_target=loss_target, m_e_norm_w=m_e_norm_w, m_e_w_in=m_e_w_in, m_e_q_norm_w=m_e_q_norm_w, m_e_k_norm_w=m_e_k_norm_w, m_e_conv_w=m_e_conv_w, m_e_w_out=m_e_w_out, m_o_norm_w=m_o_norm_w, m_o_w_in=m_o_w_in, m_o_pool_w=m_o_pool_w, m_o_pool_scale=m_o_pool_scale, m_o_dconv_w=m_o_dconv_w, m_o_dconv_b=m_o_dconv_b, m_o_ln_w=m_o_ln_w, m_o_ln_b=m_o_ln_b, m_o_w_out=m_o_w_out, v_e_norm_w=v_e_norm_w, v_e_w_in=v_e_w_in, v_e_q_norm_w=v_e_q_norm_w, v_e_k_norm_w=v_e_k_norm_w, v_e_conv_w=v_e_conv_w, v_e_w_out=v_e_w_out, v_o_norm_w=v_o_norm_w, v_o_w_in=v_o_w_in, v_o_pool_w=v_o_pool_w, v_o_pool_scale=v_o_pool_scale, v_o_dconv_w=v_o_dconv_w, v_o_dconv_b=v_o_dconv_b, v_o_ln_w=v_o_ln_w, v_o_ln_b=v_o_ln_b, v_o_w_out=v_o_w_out)
    weights = {n: given[n] for n in TWIN_WEIGHTS}
    shared = {n: given[n] for n in SHARED_INPUTS}
    per_example = {n: given[n] for n in ['x', 'positions']}
    grad_fn = _jax.value_and_grad(_loss, argnums=(0, 1))

    def one_microbatch(ex, loss_target):
        ex = dict(ex)
        diff = ex.pop(TWIN_DIFF_INPUT)
        return grad_fn(weights, diff, {**shared, **ex}, loss_target)

    if N_MICROBATCH == 1:
        loss, (grad_w, grad_x) = one_microbatch(per_example, given["loss_target"])
    else:
        def body(carry, xs):
            loss_sum, grad_sum = carry
            l_k, (gw_k, gx_k) = one_microbatch(xs[0], xs[1])
            with _jax.named_scope("update"):
                return (loss_sum + l_k, _jax.tree.map(_jnp.add, grad_sum, gw_k)), gx_k

        init = (_jnp.zeros((), _jnp.float32), _jax.tree.map(_jnp.zeros_like, weights))
        (loss, grad_w), grad_x = _jax.lax.scan(body, init, (per_example, given["loss_target"]))
    with _jax.named_scope("update"):
        delta_w, new_m, new_v = {}, {}, {}
        for n in TWIN_WEIGHTS:
            delta_w[n], new_m[n], new_v[n] = _adamw(weights[n], grad_w[n], given["m_" + n], given["v_" + n])
    return (loss, grad_x, *[grad_w[n] for n in TWIN_WEIGHTS], *[delta_w[n] for n in TWIN_WEIGHTS],
            *[new_m[n] for n in TWIN_WEIGHTS], *[new_v[n] for n in TWIN_WEIGHTS])
```

```python
import functools

import numpy as np
import jax
import jax.numpy as jnp
from jax import lax
from jax.experimental import pallas as pl
from jax.experimental.pallas import tpu as pltpu

F32 = jnp.float32
BF16 = jnp.bfloat16

D_MODEL = 1024
HEAD_DIM = 64
A_WIDTH = 512
A_HEADS = 8
A_GROUPS = ((128, 1), (512, 4), (2048, 16))
QBLK = 128
ROT_DIM = 16
ROPE_THETA = 500000.0
POOL_SIZES = (2, 4, 8, 16)
D_CONV = 31
SC_WIDTH = 3
EVEN_IN = 7168
ODD_IN = 2560
EPS = 1e-6
NEG = -1e30
ADAM_LR, ADAM_B1, ADAM_B2, ADAM_EPS, ADAM_WD, ADAM_STEP = 0.001, 0.9, 0.999, 1e-08, 0.01, 10

LANES = 128
SUBLANES = 8
HALO = 32
VMEM_LIMIT = 52 * 1024 * 1024
MESH = pl.DeviceIdType.MESH

NT_DIMS = (((1,), (1,)), ((), ()))
TN_DIMS = (((0,), (0,)), ((), ()))


def _call(body, name, grid, in_specs, out_specs, out_shape, scratch=(), sem=None, aliases=None):
    return pl.pallas_call(
        body, name=name, grid=grid, in_specs=in_specs, out_specs=out_specs, out_shape=out_shape,
        scratch_shapes=list(scratch), input_output_aliases=aliases or {},
        compiler_params=pltpu.CompilerParams(dimension_semantics=sem, vmem_limit_bytes=VMEM_LIMIT))


def _sig(v):
    return jax.nn.sigmoid(v)


def _dsilu(v, s):
    return s * (1.0 + v * (1.0 - s))


def _cs8(v):
    return v.reshape(v.shape[0] // SUBLANES, SUBLANES, v.shape[1]).sum(axis=0)


def _seg_ones():
    r = lax.broadcasted_iota(jnp.int32, (LANES, LANES), 0) // HEAD_DIM
    c = lax.broadcasted_iota(jnp.int32, (LANES, LANES), 1) // HEAD_DIM
    return (r == c).astype(BF16)


def _segsum(v, ones):
    hi = v.astype(BF16)
    lo = (v - hi.astype(F32)).astype(BF16)
    return (jnp.dot(hi, ones, preferred_element_type=F32) + jnp.dot(lo, ones, preferred_element_type=F32))


def _rope_tables(pos_ref, freq_ref):
    ang = pos_ref[...].astype(F32) * freq_ref[...]
    cosv, sinv = jnp.cos(ang), jnp.sin(ang)
    lm = lax.broadcasted_iota(jnp.int32, ang.shape, 1) % HEAD_DIM
    half = ROT_DIM // 2
    c = jnp.where(lm < ROT_DIM, cosv, 1.0)
    s1 = jnp.where((lm >= half) & (lm < ROT_DIM), sinv, 0.0)
    s2 = jnp.where(lm < half, -sinv, 0.0)
    return c, s1, s2


def _freq_table():
    half = ROT_DIM // 2
    inv = ROPE_THETA ** (-np.arange(half, dtype=np.float64) / half)
    lane = np.arange(LANES) % HEAD_DIM
    f = np.where(lane < ROT_DIM, inv[lane % half], 0.0)
    return jnp.asarray(f.reshape(1, LANES), F32)


def _inproj(x, nw, w, tm, tn, name):
    S, N = x.shape[0], w.shape[1]

    def body(x_ref, nw_ref, w_ref, o_ref, ht_ref, hs):
        @pl.when(pl.program_id(1) == 0)
        def _():
            xv = x_ref[...]
            ms = jnp.mean(xv * xv, axis=-1, keepdims=True)
            h = xv * lax.rsqrt(ms + EPS) * nw_ref[...]
            hs[...] = h.astype(BF16)
            ht_ref[...] = h.T.astype(BF16)
        o_ref[...] = jnp.dot(hs[...], w_ref[...], preferred_element_type=F32)

    return _call(
        body, name, (S // tm, N // tn),
        [pl.BlockSpec((tm, D_MODEL), lambda i, j: (i, 0)),
         pl.BlockSpec((1, D_MODEL), lambda i, j: (0, 0)),
         pl.BlockSpec((D_MODEL, tn), lambda i, j: (0, j))],
        [pl.BlockSpec((tm, tn), lambda i, j: (i, j)),
         pl.BlockSpec((D_MODEL, tm), lambda i, j: (0, i))],
        [jax.ShapeDtypeStruct((S, N), F32), jax.ShapeDtypeStruct((D_MODEL, S), BF16)],
        scratch=[pltpu.VMEM((tm, D_MODEL), BF16)], sem=("parallel", "arbitrary"))(x, nw, w)


def _outproj(x, u, w, tm, name):
    S = x.shape[0]

    def body(x_ref, u_ref, w_ref, o_ref):
        o_ref[...] = x_ref[...] + jnp.dot(u_ref[...].astype(BF16), w_ref[...], preferred_element_type=F32)

    return _call(
        body, name, (S // tm,),
        [pl.BlockSpec((tm, D_MODEL), lambda i: (i, 0)),
         pl.BlockSpec((tm, D_MODEL), lambda i: (i, 0)),
         pl.BlockSpec((D_MODEL, D_MODEL), lambda i: (0, 0))],
        pl.BlockSpec((tm, D_MODEL), lambda i: (i, 0)),
        jax.ShapeDtypeStruct((S, D_MODEL), F32), sem=("parallel",))(x, u, w)


def _outproj_loss(x, u, w, tgt, tm, name):
    S = x.shape[0]

    def body(x_ref, u_ref, w_ref, t_ref, dy_ref, l_ref, acc):
        i = pl.program_id(0)

        @pl.when(i == 0)
        def _():
            acc[...] = jnp.zeros_like(acc)
        y = x_ref[...] + jnp.dot(u_ref[...].astype(BF16), w_ref[...], preferred_element_type=F32)
        diff = y - t_ref[...]
        dy_ref[...] = diff / float(D_MODEL)
        acc[...] += _cs8(diff * diff)

        @pl.when(i == pl.num_programs(0) - 1)
        def _():
            l_ref[...] = jnp.sum(acc[...], axis=0, keepdims=True)

    return _call(
        body, name, (S // tm,),
        [pl.BlockSpec((tm, D_MODEL), lambda i: (i, 0)),
         pl.BlockSpec((tm, D_MODEL), lambda i: (i, 0)),
         pl.BlockSpec((D_MODEL, D_MODEL), lambda i: (0, 0)),
         pl.BlockSpec((tm, D_MODEL), lambda i: (i, 0))],
        [pl.BlockSpec((tm, D_MODEL), lambda i: (i, 0)),
         pl.BlockSpec((1, D_MODEL), lambda i: (0, 0))],
        [jax.ShapeDtypeStruct((S, D_MODEL), F32), jax.ShapeDtypeStruct((1, D_MODEL), F32)],
        scratch=[pltpu.VMEM((SUBLANES, D_MODEL), F32)], sem=("arbitrary",))(x, u, w, tgt)


def _mm_nt(a, w, tm, name):
    S, K = a.shape
    N = w.shape[0]

    def body(a_ref, w_ref, o_ref):
        o_ref[...] = lax.dot_general(a_ref[...].astype(BF16), w_ref[...], NT_DIMS, preferred_element_type=F32)

    return _call(
        body, name, (S // tm,),
        [pl.BlockSpec((tm, K), lambda i: (i, 0)), pl.BlockSpec((N, K), lambda i: (0, 0))],
        pl.BlockSpec((tm, N), lambda i: (i, 0)),
        jax.ShapeDtypeStruct((S, N), F32), sem=("parallel",))(a, w)


def _piece_blocks(pieces, tk, axis):
    starts, counts, s = [], [], 0
    for p in pieces:
        n = p.shape[axis] // tk
        starts.append(s)
        counts.append(n)
        s += n
    return starts, counts, s


def _mm_nt_rms(pieces, w, x, nw, dres, tm, tk, name):
    S = x.shape[0]
    starts, counts, nk = _piece_blocks(pieces, tk, 1)
    npc = len(pieces)
    ni = S // tm

    def body(*refs):
        p_refs = refs[:npc]
        w_ref, x_ref, nw_ref, dr_ref, dx_ref, dnw_ref, acc, nacc = refs[npc:]
        i, k = pl.program_id(0), pl.program_id(1)

        @pl.when(k == 0)
        def _():
            acc[...] = jnp.zeros_like(acc)

        @pl.when((i == 0) & (k == 0))
        def _():
            nacc[...] = jnp.zeros_like(nacc)

        for p in range(npc):
            @pl.when((k >= starts[p]) & (k < starts[p] + counts[p]))
            def _(p=p):
                acc[...] += lax.dot_general(p_refs[p][...].astype(BF16), w_ref[...], NT_DIMS,
                                            preferred_element_type=F32)

        @pl.when(k == nk - 1)
        def _():
            xv = x_ref[...]
            rs = lax.rsqrt(jnp.mean(xv * xv, axis=-1, keepdims=True) + EPS)
            xh = xv * rs
            dh = acc[...]
            nacc[...] += _cs8(dh * xh)
            dxh = dh * nw_ref[...]
            dx_ref[...] = dr_ref[...] + rs * (dxh - xh * jnp.mean(dxh * xh, axis=-1, keepdims=True))

        @pl.when((i == ni - 1) & (k == nk - 1))
        def _():
            dnw_ref[...] = jnp.sum(nacc[...], axis=0, keepdims=True)

    def pspec(p):
        return pl.BlockSpec((tm, tk), lambda i, k: (i, jnp.clip(k - starts[p], 0, counts[p] - 1)))

    return _call(
        body, name, (ni, nk),
        [pspec(p) for p in range(npc)] +
        [pl.BlockSpec((D_MODEL, tk), lambda i, k: (0, k)),
         pl.BlockSpec((tm, D_MODEL), lambda i, k: (i, 0)),
         pl.BlockSpec((1, D_MODEL), lambda i, k: (0, 0)),
         pl.BlockSpec((tm, D_MODEL), lambda i, k: (i, 0))],
        [pl.BlockSpec((tm, D_MODEL), lambda i, k: (i, 0)),
         pl.BlockSpec((1, D_MODEL), lambda i, k: (0, 0))],
        [jax.ShapeDtypeStruct((S, D_MODEL), F32), jax.ShapeDtypeStruct((1, D_MODEL), F32)],
        scratch=[pltpu.VMEM((tm, D_MODEL), F32), pltpu.VMEM((SUBLANES, D_MODEL), F32)],
        sem=("arbitrary", "arbitrary"))(*pieces, w, x, nw, dres)


def _mm_wgrad(at, pieces, tn, tk, name):
    M, S = at.shape
    starts, counts, nj = _piece_blocks(pieces, tn, 1)
    npc = len(pieces)
    ns = S // tk

    def body(*refs):
        a_ref = refs[0]
        p_refs = refs[1:1 + npc]
        o_ref = refs[1 + npc]
        j, s = pl.program_id(0), pl.program_id(1)

        @pl.when(s == 0)
        def _():
            o_ref[...] = jnp.zeros_like(o_ref)

        for p in range(npc):
            @pl.when((j >= starts[p]) & (j < starts[p] + counts[p]))
            def _(p=p):
                o_ref[...] += jnp.dot(a_ref[...], p_refs[p][...].astype(BF16), preferred_element_type=F32)

    def pspec(p):
        def imap(j, s):
            active = (j >= starts[p]) & (j < starts[p] + counts[p])
            return (jnp.where(active, s, 0), jnp.clip(j - starts[p], 0, counts[p] - 1))
        return pl.BlockSpec((tk, tn), imap)

    return _call(
        body, name, (nj, ns),
        [pl.BlockSpec((M, tk), lambda j, s: (0, s))] + [pspec(p) for p in range(npc)],
        pl.BlockSpec((M, tn), lambda j, s: (0, j)),
        jax.ShapeDtypeStruct((M, nj * tn), F32), sem=("parallel", "arbitrary"))(at, *pieces)


def _qk_prep(proj, pos, freq, wq, wk, T):
    S = proj.shape[0]
    qk_w = 3 * A_WIDTH

    def body(q_ref, k_ref, pos_ref, f_ref, wq_ref, wk_ref, *outs):
        ones = _seg_ones()
        c, s1, s2 = _rope_tables(pos_ref, f_ref)
        for src, w_ref, base in ((q_ref, wq_ref, 0), (k_ref, wk_ref, 3)):
            wv = w_ref[...]
            for g in range(3):
                for ch in range(A_WIDTH // LANES):
                    v = src[:, g * A_WIDTH + ch * LANES: g * A_WIDTH + (ch + 1) * LANES]
                    rs = lax.rsqrt(_segsum(v * v, ones) * (1.0 / HEAD_DIM) + EPS)
                    y = v * rs * wv
                    r = y * c + pltpu.roll(y, 8, 1) * s1 + pltpu.roll(y, LANES - 8, 1) * s2
                    outs[base + g][:, ch * LANES:(ch + 1) * LANES] = r.astype(BF16)

    return _call(
        body, "qk_prep", (S // T,),
        [pl.BlockSpec((T, qk_w), lambda i: (i, 0)), pl.BlockSpec((T, qk_w), lambda i: (i, 1)),
         pl.BlockSpec((T, 1), lambda i: (i, 0)), pl.BlockSpec((1, LANES), lambda i: (0, 0)),
         pl.BlockSpec((1, LANES), lambda i: (0, 0)), pl.BlockSpec((1, LANES), lambda i: (0, 0))],
        [pl.BlockSpec((T, A_WIDTH), lambda i: (i, 0))] * 6,
        [jax.ShapeDtypeStruct((S, A_WIDTH), BF16)] * 6, sem=("parallel",))(proj, proj, pos, freq, wq, wk)


def _attn_mask(i):
    qi = lax.broadcasted_iota(jnp.int32, (QBLK, 2 * QBLK), 0) + QBLK
    kj = lax.broadcasted_iota(jnp.int32, (QBLK, 2 * QBLK), 1)
    dist = qi - kj
    return (dist >= 0) & (dist <= QBLK) & ((i > 0) | (kj >= QBLK))


def _attn_fwd(q, k, proj, g):
    S = q.shape[0]
    d = A_GROUPS[g][1]
    n = S // d
    nb = n // QBLK
    ncol = EVEN_IN // A_WIDTH
    qv, kv, pv = q.reshape(n, d * A_WIDTH), k.reshape(n, d * A_WIDTH), proj.reshape(n, d * EVEN_IN)

    def body(q_ref, kp_ref, kc_ref, vp_ref, vc_ref, o_ref, l_ref):
        i = pl.program_id(1)
        mask = _attn_mask(i)
        for h in range(A_HEADS):
            hs = slice(h * HEAD_DIM, (h + 1) * HEAD_DIM)
            kc = jnp.concatenate([kp_ref[:, hs], kc_ref[:, hs]], axis=0)
            vc = jnp.concatenate([vp_ref[:, hs], vc_ref[:, hs]], axis=0).astype(BF16)
            s = lax.dot_general(q_ref[:, hs], kc, NT_DIMS, preferred_element_type=F32) * (HEAD_DIM ** -0.5)
            s = jnp.where(mask, s, NEG)
            m = jnp.max(s, axis=-1, keepdims=True)
            p = jnp.exp(s - m)
            den = jnp.sum(p, axis=-1, keepdims=True)
            o_ref[:, hs] = jnp.dot((p / den).astype(BF16), vc, preferred_element_type=F32)
            l_ref[:, hs] = jnp.broadcast_to(m + jnp.log(den), (QBLK, HEAD_DIM))

    blk = (QBLK, A_WIDTH)
    prev = lambda r, i: (jnp.maximum(i - 1, 0), r)
    cur = lambda r, i: (i, r)
    vprev = lambda r, i: (jnp.maximum(i - 1, 0), r * ncol + 6 + g)
    vcur = lambda r, i: (i, r * ncol + 6 + g)
    o, l = _call(
        body, "attn_fwd_g%d" % g, (d, nb),
        [pl.BlockSpec(blk, cur), pl.BlockSpec(blk, prev), pl.BlockSpec(blk, cur),
         pl.BlockSpec(blk, vprev), pl.BlockSpec(blk, vcur)],
        [pl.BlockSpec(blk, cur), pl.BlockSpec(blk, cur)],
        [jax.ShapeDtypeStruct((n, d * A_WIDTH), F32)] * 2, sem=("parallel", "parallel"))(qv, kv, kv, pv, pv)
    return o.reshape(S, A_WIDTH), l.reshape(S, A_WIDTH)


def _attn_bwd(q, k, proj, do, lse, cg, g):
    S = q.shape[0]
    d = A_GROUPS[g][1]
    n = S // d
    nb = n // QBLK
    ncol = EVEN_IN // A_WIDTH
    view = lambda a: a.reshape(n, d * A_WIDTH)
    pv = proj.reshape(n, d * EVEN_IN)
    scale = HEAD_DIM ** -0.5

    def body(q_ref, kp_ref, kc_ref, vp_ref, vc_ref, do_ref, l_ref, c_ref, dq_ref, dk_ref, dv_ref, ck, cv):
        i = pl.program_id(1)

        @pl.when(i == 0)
        def _():
            ck[...] = jnp.zeros_like(ck)
            cv[...] = jnp.zeros_like(cv)

        @pl.when(i < nb)
        def _():
            mask = _attn_mask(i)
            for h in range(A_HEADS):
                hs = slice(h * HEAD_DIM, (h + 1) * HEAD_DIM)
                qh = q_ref[:, hs]
                doh = do_ref[:, hs]
                kc = jnp.concatenate([kp_ref[:, hs], kc_ref[:, hs]], axis=0)
                vc = jnp.concatenate([vp_ref[:, hs], vc_ref[:, hs]], axis=0).astype(BF16)
                s = lax.dot_general(qh, kc, NT_DIMS, preferred_element_type=F32) * scale
                p = jnp.where(mask, jnp.exp(s - l_ref[:, h * HEAD_DIM:h * HEAD_DIM + 1]), 0.0)
                dp = lax.dot_general(doh, vc, NT_DIMS, preferred_element_type=F32)
                ds = (p * (dp + c_ref[:, h * HEAD_DIM:h * HEAD_DIM + 1]) * scale).astype(BF16)
                dq_ref[:, hs] = jnp.dot(ds, kc, preferred_element_type=F32)
                dkc = lax.dot_general(ds, qh, TN_DIMS, preferred_element_type=F32)
                dvc = lax.dot_general(p.astype(BF16), doh, TN_DIMS, preferred_element_type=F32)
                dk_ref[:, hs] = ck[:, hs] + dkc[:QBLK]
                dv_ref[:, hs] = cv[:, hs] + dvc[:QBLK]
                ck[:, hs] = dkc[QBLK:]
                cv[:, hs] = dvc[QBLK:]

        @pl.when(i == nb)
        def _():
            dk_ref[...] = ck[...]
            dv_ref[...] = cv[...]

    blk = (QBLK, A_WIDTH)
    qi = lambda i: jnp.minimum(i, nb - 1)
    cur = lambda r, i: (qi(i), r)
    prev = lambda r, i: (jnp.maximum(qi(i) - 1, 0), r)
    vprev = lambda r, i: (jnp.maximum(qi(i) - 1, 0), r * ncol + 6 + g)
    vcur = lambda r, i: (qi(i), r * ncol + 6 + g)
    late = lambda r, i: (jnp.maximum(i - 1, 0), r)
    dq, dk, dv = _call(
        body, "attn_bwd_g%d" % g, (d, nb + 1),
        [pl.BlockSpec(blk, cur), pl.BlockSpec(blk, prev), pl.BlockSpec(blk, cur),
         pl.BlockSpec(blk, vprev), pl.BlockSpec(blk, vcur),
         pl.BlockSpec(blk, cur), pl.BlockSpec(blk, cur), pl.BlockSpec(blk, cur)],
        [pl.BlockSpec(blk, cur), pl.BlockSpec(blk, late), pl.BlockSpec(blk, late)],
        [jax.ShapeDtypeStruct((n, d * A_WIDTH), F32)] * 3,
        scratch=[pltpu.VMEM(blk, F32), pltpu.VMEM(blk, F32)],
        sem=("parallel", "arbitrary"))(view(q), view(k), view(k), pv, pv, view(do), view(lse), view(cg))
    return dq.reshape(S, A_WIDTH), dk.reshape(S, A_WIDTH), dv.reshape(S, A_WIDTH)


def _merge_weights(l0, l1, l2):
    mx = jnp.maximum(jnp.maximum(l0, l1), l2)
    e0, e1, e2 = jnp.exp(l0 - mx), jnp.exp(l1 - mx), jnp.exp(l2 - mx)
    den = e0 + e1 + e2
    return e0 / den, e1 / den, e2 / den


def _even_specs(T, S):
    t8 = T // SUBLANES
    last8 = S // SUBLANES - 1
    col = lambda c: pl.BlockSpec((T, A_WIDTH), lambda i: (i, c))
    prev8 = lambda c: pl.BlockSpec((SUBLANES, A_WIDTH), lambda i: (jnp.maximum(i * t8 - 1, 0), c))
    next8 = lambda c: pl.BlockSpec((SUBLANES, A_WIDTH), lambda i: (jnp.minimum((i + 1) * t8, last8), c))
    return col, prev8, next8


def _even_mixer_fwd(proj, os_, ls_, conv_w, T):
    S = proj.shape[0]
    col, prev8, _ = _even_specs(T, S)
    H = SUBLANES

    def body(bg_r, cg_r, hb_r, zl_r, zh_r, cgp_r, hbp_r, o0, o1, o2, l0, l1, l2, cw_r, u_ref, ut_ref, ext):
        i = pl.program_id(0)
        w0, w1, w2 = _merge_weights(l0[...], l1[...], l2[...])
        oa = w0 * o0[...] + w1 * o1[...] + w2 * o2[...]
        ext[0:H, :] = jnp.where(i == 0, 0.0, cgp_r[...] * hbp_r[...])
        ext[H:H + T, :] = cg_r[...] * hb_r[...]
        conv = cw_r[0:1, :] * ext[H - 2:H - 2 + T, :]
        for kk in range(1, SC_WIDTH):
            conv = conv + cw_r[kk:kk + 1, :] * ext[H - 2 + kk:H - 2 + kk + T, :]
        zl, zh = zl_r[...], zh_r[...]
        u_ref[:, 0:A_WIDTH] = oa * (zl * _sig(zl))
        u_ref[:, A_WIDTH:] = bg_r[...] * conv * (zh * _sig(zh))
        ut_ref[...] = u_ref[...].T.astype(BF16)

    full = pl.BlockSpec((T, A_WIDTH), lambda i: (i, 0))
    return _call(
        body, "even_mixer_fwd", (S // T,),
        [col(9), col(10), col(11), col(12), col(13), prev8(10), prev8(11)] + [full] * 6 +
        [pl.BlockSpec((SC_WIDTH, A_WIDTH), lambda i: (0, 0))],
        [pl.BlockSpec((T, D_MODEL), lambda i: (i, 0)), pl.BlockSpec((D_MODEL, T), lambda i: (0, i))],
        [jax.ShapeDtypeStruct((S, D_MODEL), F32), jax.ShapeDtypeStruct((D_MODEL, S), BF16)],
        scratch=[pltpu.VMEM((T + H, A_WIDTH), F32)], sem=("parallel",))(
            proj, proj, proj, proj, proj, proj, proj, *os_, *ls_, conv_w)


def _even_mixer_bwd(du, proj, os_, ls_, conv_w, T):
    S = proj.shape[0]
    nt = S // T
    col, prev8, next8 = _even_specs(T, S)
    H = SUBLANES
    t8 = T // SUBLANES
    last8 = S // SUBLANES - 1

    def body(dul_r, duh_r, bg_r, cg_r, hb_r, zl_r, zh_r, cgp_r, hbp_r, dun_r, zhn_r, bgn_r,
             o0, o1, o2, l0, l1, l2, cw_r,
             do0, do1, do2, c0, c1, c2, dr_ref, dcw_ref, ext_t, ext_d, acc):
        i = pl.program_id(0)

        @pl.when(i == 0)
        def _():
            acc[...] = jnp.zeros_like(acc)

        ones = _seg_ones()
        zl, zh = zl_r[...], zh_r[...]
        sl, sh = _sig(zl), _sig(zh)
        dul, duh = dul_r[...], duh_r[...]
        ws = _merge_weights(l0[...], l1[...], l2[...])
        oa = ws[0] * o0[...] + ws[1] * o1[...] + ws[2] * o2[...]
        doa = dul * (zl * sl)
        for ch in range(A_WIDTH // LANES):
            cs = slice(ch * LANES, (ch + 1) * LANES)
            rsum = _segsum(doa[:, cs] * oa[:, cs], ones)
            for wg, do_ref, c_ref in zip(ws, (do0, do1, do2), (c0, c1, c2)):
                do_ref[:, cs] = (wg[:, cs] * doa[:, cs]).astype(BF16)
                c_ref[:, cs] = -wg[:, cs] * rsum
        cgv, hbv, bgv = cg_r[...], hb_r[...], bg_r[...]
        ext_t[0:H, :] = jnp.where(i == 0, 0.0, cgp_r[...] * hbp_r[...])
        ext_t[H:H + T, :] = cgv * hbv
        conv = cw_r[0:1, :] * ext_t[H - 2:H - 2 + T, :]
        for kk in range(1, SC_WIDTH):
            conv = conv + cw_r[kk:kk + 1, :] * ext_t[H - 2 + kk:H - 2 + kk + T, :]
        dyb = duh * (zh * sh)
        dconv = dyb * bgv
        zn = zhn_r[...]
        ext_d[0:T, :] = dconv
        ext_d[T:T + H, :] = jnp.where(i == nt - 1, 0.0, dun_r[...] * (zn * _sig(zn)) * bgn_r[...])
        dt = cw_r[0:1, :] * ext_d[2:2 + T, :]
        for kk in range(1, SC_WIDTH):
            dt = dt + cw_r[kk:kk + 1, :] * ext_d[2 - kk:2 - kk + T, :]
        for kk in range(SC_WIDTH):
            acc[kk * SUBLANES:(kk + 1) * SUBLANES, :] += _cs8(dconv * ext_t[H - 2 + kk:H - 2 + kk + T, :])
        dr_ref[:, 0:A_WIDTH] = dyb * conv
        dr_ref[:, A_WIDTH:2 * A_WIDTH] = dt * hbv
        dr_ref[:, 2 * A_WIDTH:3 * A_WIDTH] = dt * cgv
        dr_ref[:, 3 * A_WIDTH:4 * A_WIDTH] = dul * oa * _dsilu(zl, sl)
        dr_ref[:, 4 * A_WIDTH:5 * A_WIDTH] = duh * (bgv * conv) * _dsilu(zh, sh)

        @pl.when(i == nt - 1)
        def _():
            for kk in range(SC_WIDTH):
                dcw_ref[kk:kk + 1, :] = jnp.sum(acc[kk * SUBLANES:(kk + 1) * SUBLANES, :], axis=0, keepdims=True)

    full = pl.BlockSpec((T, A_WIDTH), lambda i: (i, 0))
    dunext = pl.BlockSpec((SUBLANES, A_WIDTH), lambda i: (jnp.minimum((i + 1) * t8, last8), 1))
    outs = _call(
        body, "even_mixer_bwd", (nt,),
        [pl.BlockSpec((T, A_WIDTH), lambda i: (i, 0)), pl.BlockSpec((T, A_WIDTH), lambda i: (i, 1)),
         col(9), col(10), col(11), col(12), col(13), prev8(10), prev8(11), dunext, next8(13), next8(9)] +
        [full] * 6 + [pl.BlockSpec((SC_WIDTH, A_WIDTH), lambda i: (0, 0))],
        [full] * 6 + [pl.BlockSpec((T, 5 * A_WIDTH), lambda i: (i, 0)),
                      pl.BlockSpec((SC_WIDTH, A_WIDTH), lambda i: (0, 0))],
        [jax.ShapeDtypeStruct((S, A_WIDTH), BF16)] * 3 + [jax.ShapeDtypeStruct((S, A_WIDTH), F32)] * 3 +
        [jax.ShapeDtypeStruct((S, 5 * A_WIDTH), F32), jax.ShapeDtypeStruct((SC_WIDTH, A_WIDTH), F32)],
        scratch=[pltpu.VMEM((T + H, A_WIDTH), F32), pltpu.VMEM((T + H, A_WIDTH), F32),
                 pltpu.VMEM((SC_WIDTH * SUBLANES, A_WIDTH), F32)],
        sem=("arbitrary",))(du, du, proj, proj, proj, proj, proj, proj, proj, du, proj, proj, *os_, *ls_, conv_w)
    return outs[0:3], outs[3:6], outs[6], outs[7]


def _qk_bwd(proj, dqs, dks, pos, freq, wq, wk, T):
    S = proj.shape[0]
    nt = S // T
    qk_w = 3 * A_WIDTH

    def body(q_ref, k_ref, dq0, dq1, dq2, dk0, dk1, dk2, pos_ref, f_ref, wq_ref, wk_ref, o_ref, dw_ref, acc):
        i = pl.program_id(0)

        @pl.when(i == 0)
        def _():
            acc[...] = jnp.zeros_like(acc)

        ones = _seg_ones()
        c, s1, s2 = _rope_tables(pos_ref, f_ref)
        for t, (src, w_ref, ds) in enumerate(((q_ref, wq_ref, (dq0, dq1, dq2)), (k_ref, wk_ref, (dk0, dk1, dk2)))):
            wv = w_ref[...]
            for g in range(3):
                for ch in range(A_WIDTH // LANES):
                    cs = slice(g * A_WIDTH + ch * LANES, g * A_WIDTH + (ch + 1) * LANES)
                    v = src[:, cs]
                    dout = ds[g][:, ch * LANES:(ch + 1) * LANES]
                    rs = lax.rsqrt(_segsum(v * v, ones) * (1.0 / HEAD_DIM) + EPS)
                    xh = v * rs
                    dy = dout * c + pltpu.roll(dout * s1, LANES - 8, 1) + pltpu.roll(dout * s2, 8, 1)
                    acc[t * SUBLANES:(t + 1) * SUBLANES, :] += _cs8(dy * xh)
                    dxh = dy * wv
                    mean = _segsum(dxh * xh, ones) * (1.0 / HEAD_DIM)
                    o_ref[:, t * qk_w + g * A_WIDTH + ch * LANES: t * qk_w + g * A_WIDTH + (ch + 1) * LANES] = (
                        rs * (dxh - xh * mean))

        @pl.when(i == nt - 1)
        def _():
            for t in range(2):
                srow = jnp.sum(acc[t * SUBLANES:(t + 1) * SUBLANES, :], axis=0, keepdims=True)
                dw_ref[t:t + 1, :] = srow + pltpu.roll(srow, HEAD_DIM, 1)

    full = pl.BlockSpec((T, A_WIDTH), lambda i: (i, 0))
    return _call(
        body, "qk_bwd", (nt,),
        [pl.BlockSpec((T, qk_w), lambda i: (i, 0)), pl.BlockSpec((T, qk_w), lambda i: (i, 1))] + [full] * 6 +
        [pl.BlockSpec((T, 1), lambda i: (i, 0)), pl.BlockSpec((1, LANES), lambda i: (0, 0)),
         pl.BlockSpec((1, LANES), lambda i: (0, 0)), pl.BlockSpec((1, LANES), lambda i: (0, 0))],
        [pl.BlockSpec((T, 2 * qk_w), lambda i: (i, 0)), pl.BlockSpec((SUBLANES, LANES), lambda i: (0, 0))],
        [jax.ShapeDtypeStruct((S, 2 * qk_w), F32), jax.ShapeDtypeStruct((SUBLANES, LANES), F32)],
        scratch=[pltpu.VMEM((2 * SUBLANES, LANES), F32)], sem=("arbitrary",))(
            proj, proj, *dqs, *dks, pos, freq, wq, wk)


N_SMALL_ODD = 40


def _odd_forward_tile(i, uc_r, da_r, dg_r, ucp_r, dap_r, dgp_r, pw_r, dw_r, db_r, ext_u, ext_g, pooled_s, pm_s, T):
    H = HALO
    first = i == 0
    uc = uc_r[...]
    ext_u[0:H, :] = jnp.where(first, 0.0, ucp_r[...])
    ext_u[H:H + T, :] = uc
    ext_g[0:H, :] = jnp.where(first, 0.0, dap_r[...] * _sig(dgp_r[...]))
    ext_g[H:H + T, :] = da_r[...] * _sig(dg_r[...])
    row = i * T + lax.broadcasted_iota(jnp.int32, (T, 1), 0)
    for g, p in enumerate(POOL_SIZES):
        cs = slice(g * LANES, (g + 1) * LANES)
        win = ext_u[H:H + T, cs]
        for j in range(1, p):
            win = win + ext_u[H - j:H - j + T, cs]
        cnt = jnp.minimum(row + 1, p).astype(F32)
        pooled = win / cnt - uc[:, cs]
        pooled_s[:, cs] = pooled
        pm_s[:, cs] = jnp.dot(pooled.astype(BF16), pw_r[g].astype(BF16), preferred_element_type=F32)
    conv = db_r[...] + dw_r[0:1, :] * ext_g[H - (D_CONV - 1):H - (D_CONV - 1) + T, :]
    for kk in range(1, D_CONV):
        off = H - (D_CONV - 1) + kk
        conv = conv + dw_r[kk:kk + 1, :] * ext_g[off:off + T, :]
    return conv, row


def _odd_specs(T, S, order):
    tb = T // HALO
    col = lambda c: pl.BlockSpec((T, A_WIDTH), lambda s: (order(s), c))
    prev = lambda c: pl.BlockSpec((HALO, A_WIDTH), lambda s: (jnp.maximum(order(s) * tb - 1, 0), c))
    const2 = lambda shape: pl.BlockSpec(shape, lambda s: (0, 0))
    weights = [pl.BlockSpec((4, LANES, LANES), lambda s: (0, 0, 0)), const2((1, A_WIDTH)),
               const2((D_CONV, A_WIDTH)), const2((1, A_WIDTH)), const2((1, A_WIDTH)), const2((1, A_WIDTH))]
    return col, prev, weights


def _odd_mixer_fwd(proj, pool_w, scale, dconv_w, dconv_b, ln_w, ln_b, T):
    S = proj.shape[0]
    col, prev, wspecs = _odd_specs(T, S, lambda s: s)
    H = HALO

    def body(uc_r, da_r, dg_r, zl_r, zh_r, ucp_r, dap_r, dgp_r, pw_r, sc_r, dw_r, db_r, lw_r, lb_r,
             u_ref, ut_ref, ext_u, ext_g, pooled_s, pm_s):
        i = pl.program_id(0)
        conv, _ = _odd_forward_tile(i, uc_r, da_r, dg_r, ucp_r, dap_r, dgp_r, pw_r, dw_r, db_r,
                                    ext_u, ext_g, pooled_s, pm_s, T)
        mu = jnp.mean(conv, axis=-1, keepdims=True)
        xc = conv - mu
        yh = xc * lax.rsqrt(jnp.mean(xc * xc, axis=-1, keepdims=True) + EPS)
        ln = yh * lw_r[...] + lb_r[...]
        zl, zh = zl_r[...], zh_r[...]
        u_ref[:, 0:A_WIDTH] = pm_s[...] * sc_r[...] * (zl * _sig(zl))
        u_ref[:, A_WIDTH:] = ln * _sig(ln) * (zh * _sig(zh))
        ut_ref[...] = u_ref[...].T.astype(BF16)

    return _call(
        body, "odd_mixer_fwd", (S // T,),
        [col(0), col(1), col(2), col(3), col(4), prev(0), prev(1), prev(2)] + wspecs,
        [pl.BlockSpec((T, D_MODEL), lambda i: (i, 0)), pl.BlockSpec((D_MODEL, T), lambda i: (0, i))],
        [jax.ShapeDtypeStruct((S, D_MODEL), F32), jax.ShapeDtypeStruct((D_MODEL, S), BF16)],
        scratch=[pltpu.VMEM((T + H, A_WIDTH), F32), pltpu.VMEM((T + H, A_WIDTH), F32),
                 pltpu.VMEM((T, A_WIDTH), F32), pltpu.VMEM((T, A_WIDTH), F32)],
        sem=("parallel",))(proj, proj, proj, proj, proj, proj, proj, proj,
                           pool_w, scale, dconv_w, dconv_b, ln_w, ln_b)


def _odd_mixer_bwd(du, proj, pool_w, scale, dconv_w, dconv_b, ln_w, ln_b, T):
    S = proj.shape[0]
    nt = S // T
    order = lambda s: nt - 1 - s
    col, prev, wspecs = _odd_specs(T, S, order)
    H = HALO

    def body(dul_r, duh_r, uc_r, da_r, dg_r, zl_r, zh_r, ucp_r, dap_r, dgp_r, pw_r, sc_r, dw_r, db_r, lw_r, lb_r,
             dp_ref, dpw_ref, sm_ref, ext_u, ext_g, pooled_s, pm_s, dpl_s, ext_p, ext_c, acc):
        step = pl.program_id(0)
        i = nt - 1 - step

        @pl.when(step == 0)
        def _():
            ext_p[T:T + H, :] = jnp.zeros((H, A_WIDTH), F32)
            ext_c[T:T + H, :] = jnp.zeros((H, A_WIDTH), F32)
            acc[...] = jnp.zeros_like(acc)
            dpw_ref[...] = jnp.zeros_like(dpw_ref)

        def accum(r, v):
            acc[r * SUBLANES:(r + 1) * SUBLANES, :] += _cs8(v)

        conv, row = _odd_forward_tile(i, uc_r, da_r, dg_r, ucp_r, dap_r, dgp_r, pw_r, dw_r, db_r,
                                      ext_u, ext_g, pooled_s, pm_s, T)
        mu = jnp.mean(conv, axis=-1, keepdims=True)
        xc = conv - mu
        rstd = lax.rsqrt(jnp.mean(xc * xc, axis=-1, keepdims=True) + EPS)
        yh = xc * rstd
        ln = yh * lw_r[...] + lb_r[...]
        sln = _sig(ln)
        zl, zh = zl_r[...], zh_r[...]
        sl, sh = _sig(zl), _sig(zh)
        dul, duh = dul_r[...], duh_r[...]
        pm = pm_s[...]
        scv = sc_r[...]
        dyc = dul * (zl * sl)
        accum(34, dyc * pm)
        dpm = dyc * scv
        for g in range(len(POOL_SIZES)):
            cs = slice(g * LANES, (g + 1) * LANES)
            dpm_g = dpm[:, cs].astype(BF16)
            dpw_ref[g] += lax.dot_general(pooled_s[:, cs].astype(BF16), dpm_g, TN_DIMS, preferred_element_type=F32)
            dpl_s[:, cs] = lax.dot_general(dpm_g, pw_r[g].astype(BF16), NT_DIMS, preferred_element_type=F32)
        lane_p = lax.broadcasted_iota(jnp.int32, (1, A_WIDTH), 1) // LANES
        pvec = jnp.left_shift(2, lane_p)
        cnt = jnp.minimum(row + 1, pvec).astype(F32)
        dpl = dpl_s[...]
        ext_p[0:T, :] = dpl / cnt
        for g, p in enumerate(POOL_SIZES):
            cs = slice(g * LANES, (g + 1) * LANES)
            win = ext_p[0:T, cs]
            for j in range(1, p):
                win = win + ext_p[j:j + T, cs]
            dp_ref[:, cs] = win - dpl[:, cs]
        ext_p[T:T + H, :] = ext_p[0:H, :]
        dln = duh * (zh * sh) * _dsilu(ln, sln)
        accum(32, dln * yh)
        accum(33, dln)
        dyh = dln * lw_r[...]
        dc = rstd * (dyh - jnp.mean(dyh, axis=-1, keepdims=True) - yh * jnp.mean(dyh * yh, axis=-1, keepdims=True))
        accum(31, dc)
        ext_c[0:T, :] = dc
        base = H - (D_CONV - 1)
        dgl = dw_r[0:1, :] * ext_c[D_CONV - 1:D_CONV - 1 + T, :]
        accum(0, dc * ext_g[base:base + T, :])
        for kk in range(1, D_CONV):
            dgl = dgl + dw_r[kk:kk + 1, :] * ext_c[D_CONV - 1 - kk:D_CONV - 1 - kk + T, :]
            accum(kk, dc * ext_g[base + kk:base + kk + T, :])
        ext_c[T:T + H, :] = ext_c[0:H, :]
        dav, dgv = da_r[...], dg_r[...]
        sg = _sig(dgv)
        dp_ref[:, A_WIDTH:2 * A_WIDTH] = dgl * sg
        dp_ref[:, 2 * A_WIDTH:3 * A_WIDTH] = dgl * dav * sg * (1.0 - sg)
        dp_ref[:, 3 * A_WIDTH:4 * A_WIDTH] = dul * (pm * scv) * _dsilu(zl, sl)
        dp_ref[:, 4 * A_WIDTH:5 * A_WIDTH] = duh * (ln * sln) * _dsilu(zh, sh)

        @pl.when(step == nt - 1)
        def _():
            for r in range(N_SMALL_ODD):
                sm_ref[r:r + 1, :] = jnp.sum(acc[r * SUBLANES:(r + 1) * SUBLANES, :], axis=0, keepdims=True)

    return _call(
        body, "odd_mixer_bwd", (nt,),
        [pl.BlockSpec((T, A_WIDTH), lambda s: (order(s), 0)), pl.BlockSpec((T, A_WIDTH), lambda s: (order(s), 1)),
         col(0), col(1), col(2), col(3), col(4), prev(0), prev(1), prev(2)] + wspecs,
        [pl.BlockSpec((T, ODD_IN), lambda s: (order(s), 0)),
         pl.BlockSpec((4, LANES, LANES), lambda s: (0, 0, 0)),
         pl.BlockSpec((N_SMALL_ODD, A_WIDTH), lambda s: (0, 0))],
        [jax.ShapeDtypeStruct((S, ODD_IN), F32), jax.ShapeDtypeStruct((4, LANES, LANES), F32),
         jax.ShapeDtypeStruct((N_SMALL_ODD, A_WIDTH), F32)],
        scratch=[pltpu.VMEM((T + H, A_WIDTH), F32), pltpu.VMEM((T + H, A_WIDTH), F32),
                 pltpu.VMEM((T, A_WIDTH), F32), pltpu.VMEM((T, A_WIDTH), F32), pltpu.VMEM((T, A_WIDTH), F32),
                 pltpu.VMEM((T + H, A_WIDTH), F32), pltpu.VMEM((T + H, A_WIDTH), F32),
                 pltpu.VMEM((N_SMALL_ODD * SUBLANES, A_WIDTH), F32)],
        sem=("arbitrary",))(du, du, proj, proj, proj, proj, proj, proj, proj, proj,
                            pool_w, scale, dconv_w, dconv_b, ln_w, ln_b)


TILE_SEQ = 256
TILE_M = 512


def _local_step(x, pos, tgt, wb, p):
    T = TILE_SEQ
    freq = _freq_table()
    wq = jnp.tile(p["e_q_norm_w"], (1, LANES // HEAD_DIM))
    wk = jnp.tile(p["e_k_norm_w"], (1, LANES // HEAD_DIM))

    proj_e, ht_e = _inproj(x, p["e_norm_w"], wb["e_w_in"], TILE_M, 1792, "inproj_even")
    qk = _qk_prep(proj_e, pos, freq, wq, wk, T)
    qs, ks = qk[0:3], qk[3:6]
    os_, ls_ = [], []
    for g in range(3):
        o, l = _attn_fwd(qs[g], ks[g], proj_e, g)
        os_.append(o)
        ls_.append(l)
    u_e, ut_e = _even_mixer_fwd(proj_e, os_, ls_, p["e_conv_w"], T)
    x1 = _outproj(x, u_e, wb["e_w_out"], TILE_M, "outproj_even")
    proj_o, ht_o = _inproj(x1, p["o_norm_w"], wb["o_w_in"], TILE_M, 1280, "inproj_odd")
    odd_w = (p["o_pool_w"], p["o_pool_scale"], p["o_dconv_w"], p["o_dconv_b"], p["o_ln_w"], p["o_ln_b"])
    u_o, ut_o = _odd_mixer_fwd(proj_o, *odd_w, T)
    dy, lsum = _outproj_loss(x1, u_o, wb["o_w_out"], tgt, TILE_M, "outproj_odd_loss")

    g = {}
    g["o_w_out"] = _mm_wgrad(ut_o, [dy], 512, 512, "wgrad_o_out")
    du_o = _mm_nt(dy, wb["o_w_out"], TILE_M, "du_odd")
    dproj_o, g["o_pool_w"], small_o = _odd_mixer_bwd(du_o, proj_o, *odd_w, T)
    g["o_w_in"] = _mm_wgrad(ht_o, [dproj_o], 512, 512, "wgrad_o_in")
    dx1, g["o_norm_w"] = _mm_nt_rms([dproj_o], wb["o_w_in"], x1, p["o_norm_w"], dy, TILE_M, 512, "dx_odd")
    g["o_dconv_w"] = small_o[0:D_CONV]
    g["o_dconv_b"] = small_o[31:32]
    g["o_ln_w"] = small_o[32:33]
    g["o_ln_b"] = small_o[33:34]
    g["o_pool_scale"] = small_o[34:35]

    g["e_w_out"] = _mm_wgrad(ut_e, [dx1], 512, 512, "wgrad_e_out")
    du_e = _mm_nt(dx1, wb["e_w_out"], TILE_M, "du_even")
    dos, cgs, drest, g["e_conv_w"] = _even_mixer_bwd(du_e, proj_e, os_, ls_, p["e_conv_w"], T)
    dqs, dks, dvs = [], [], []
    for gi in range(3):
        dq, dk, dv = _attn_bwd(qs[gi], ks[gi], proj_e, dos[gi], ls_[gi], cgs[gi], gi)
        dqs.append(dq)
        dks.append(dk)
        dvs.append(dv)
    dqk, dnw = _qk_bwd(proj_e, dqs, dks, pos, freq, wq, wk, T)
    g["e_q_norm_w"] = dnw[0:1, 0:HEAD_DIM]
    g["e_k_norm_w"] = dnw[1:2, 0:HEAD_DIM]
    pieces = [dqk] + dvs + [drest]
    g["e_w_in"] = _mm_wgrad(ht_e, pieces, 512, 512, "wgrad_e_in")
    dx, g["e_norm_w"] = _mm_nt_rms(pieces, wb["e_w_in"], x, p["e_norm_w"], dx1, TILE_M, 512, "dx_even")
    return lsum, dx, g


BIG = ("e_w_in", "e_w_out", "o_w_in", "o_w_out")
SHARD_AXIS = {"e_w_in": 1, "e_w_out": 0, "o_w_in": 1, "o_w_out": 0}
N_CHIPS = 4
ANY = pl.BlockSpec(memory_space=pl.ANY)


def _place():
    x, y, c = lax.axis_index("x"), lax.axis_index("y"), lax.axis_index("c")
    chips = [(1 - x, y), (x, 1 - y), (1 - x, 1 - y)]
    return x, y, c, chips


def _sub(ref, name, block, half):
    rows, cols = ref.shape
    if SHARD_AXIS[name] == 1:
        cw, hr = cols // N_CHIPS, rows // 2
        return ref.at[pl.ds(pl.multiple_of(half * hr, hr), hr), pl.ds(pl.multiple_of(block * cw, LANES), cw)]
    rw, hc = rows // N_CHIPS, cols // 2
    return ref.at[pl.ds(pl.multiple_of(block * rw, rw), rw), pl.ds(pl.multiple_of(half * hc, LANES), hc)]


def _half_of_shard(ref, name, half):
    rows, cols = ref.shape
    if SHARD_AXIS[name] == 1:
        return ref.at[pl.ds(pl.multiple_of(half * (rows // 2), rows // 2), rows // 2), :]
    return ref.at[:, pl.ds(pl.multiple_of(half * (cols // 2), LANES), cols // 2)]


def _half_of_full(ref, name, half):
    rows, cols = ref.shape
    if SHARD_AXIS[name] == 1:
        return ref.at[pl.ds(pl.multiple_of(half * (rows // 2), rows // 2), rows // 2), :]
    return ref.at[:, pl.ds(pl.multiple_of(half * (cols // 2), LANES), cols // 2)]


def _gather_weights(shards):
    nw = len(BIG)

    def body(*refs):
        s_refs = dict(zip(BIG, refs[:nw]))
        f_refs = dict(zip(BIG, refs[nw:2 * nw]))
        send, recv, loc = refs[2 * nw:]
        x, y, c, chips = _place()
        me = 2 * x + y
        sib = (x, y, 1 - c)

        def rc(k, src, dst, to):
            return pltpu.make_async_remote_copy(src_ref=src, dst_ref=dst, send_sem=send.at[k], recv_sem=recv.at[k],
                                                device_id=to, device_id_type=MESH)

        locals_, first, fwd = [], [], []
        for wi, n in enumerate(BIG):
            lc = pltpu.make_async_copy(s_refs[n], _own(f_refs[n], n, me), loc.at[wi])
            lc.start()
            locals_.append(lc)
            for j, (cx, cy) in enumerate(chips):
                cp = rc(wi * 3 + j, _half_of_shard(s_refs[n], n, c), _sub(f_refs[n], n, me, c), (cx, cy, c))
                cp.start()
                first.append(cp)
        for wi, n in enumerate(BIG):
            for j, (cx, cy) in enumerate(chips):
                part = _sub(f_refs[n], n, 2 * cx + cy, c)
                rc(wi * 3 + j, part, part, sib).wait_recv()
                cp = rc(12 + wi * 3 + j, part, part, sib)
                cp.start()
                fwd.append(cp)
        for wi, n in enumerate(BIG):
            for j, (cx, cy) in enumerate(chips):
                part = _sub(f_refs[n], n, 2 * cx + cy, 1 - c)
                rc(12 + wi * 3 + j, part, part, sib).wait_recv()
        for cp in first + fwd:
            cp.wait_send()
        for lc in locals_:
            lc.wait()

    def _own(ref, name, block):
        rows, cols = ref.shape
        if SHARD_AXIS[name] == 1:
            cw = cols // N_CHIPS
            return ref.at[:, pl.ds(pl.multiple_of(block * cw, LANES), cw)]
        rw = rows // N_CHIPS
        return ref.at[pl.ds(pl.multiple_of(block * rw, rw), rw), :]

    outs = []
    for n, s in zip(BIG, shards):
        r, cdim = s.shape
        outs.append(jax.ShapeDtypeStruct((r, cdim * N_CHIPS) if SHARD_AXIS[n] == 1 else (r * N_CHIPS, cdim), s.dtype))
    return pl.pallas_call(
        body, name="gather_weights", in_specs=[ANY] * nw, out_specs=[ANY] * nw, out_shape=outs,
        scratch_shapes=[pltpu.SemaphoreType.DMA((24,)), pltpu.SemaphoreType.DMA((24,)), pltpu.SemaphoreType.DMA((nw,))],
    )(*shards)


def _allreduce_small(part):
    R = part.shape[0]

    def body(p_ref, o_ref, sbuf, cbuf, send, recv):
        x, y, c, chips = _place()
        me = 2 * x + y
        sib = (x, y, 1 - c)
        sbuf[c] = p_ref[...]
        mine = sbuf.at[c]
        d2d = pltpu.make_async_remote_copy(src_ref=mine, dst_ref=mine, send_sem=send.at[0], recv_sem=recv.at[0],
                                           device_id=sib, device_id_type=MESH)
        d2d.start()
        theirs = sbuf.at[1 - c]
        pltpu.make_async_remote_copy(src_ref=theirs, dst_ref=theirs, send_sem=send.at[0], recv_sem=recv.at[0],
                                     device_id=sib, device_id_type=MESH).wait_recv()
        cbuf[me] = sbuf[0] + sbuf[1]
        blk = cbuf.at[me]
        sends = [d2d]
        for j, (cx, cy) in enumerate(chips):
            cp = pltpu.make_async_remote_copy(src_ref=blk, dst_ref=blk, send_sem=send.at[1 + j], recv_sem=recv.at[1 + j],
                                              device_id=(cx, cy, c), device_id_type=MESH)
            cp.start()
            sends.append(cp)
        for j, (cx, cy) in enumerate(chips):
            got = cbuf.at[2 * cx + cy]
            pltpu.make_async_remote_copy(src_ref=got, dst_ref=got, send_sem=send.at[1 + j], recv_sem=recv.at[1 + j],
                                         device_id=(cx, cy, c), device_id_type=MESH).wait_recv()
        o_ref[...] = (cbuf[0] + cbuf[1]) + (cbuf[2] + cbuf[3])
        for cp in sends:
            cp.wait_send()

    vm = pl.BlockSpec(memory_space=pltpu.VMEM)
    return pl.pallas_call(
        body, name="allreduce_small", in_specs=[vm], out_specs=vm,
        out_shape=jax.ShapeDtypeStruct(part.shape, F32),
        scratch_shapes=[pltpu.VMEM((2, R, LANES), F32), pltpu.VMEM((N_CHIPS, R, LANES), F32),
                        pltpu.SemaphoreType.DMA((4,)), pltpu.SemaphoreType.DMA((4,))],
    )(part)


def _half_shape(shape, name):
    r, cdim = shape
    return (r // 2, cdim) if SHARD_AXIS[name] == 1 else (r, cdim // 2)


def _shard_shape(shape, name):
    r, cdim = shape
    return (r, cdim // N_CHIPS) if SHARD_AXIS[name] == 1 else (r // N_CHIPS, cdim)


def _swap_halves(grads):
    nw = len(BIG)

    def body(*refs):
        g_refs = dict(zip(BIG, refs[:nw]))
        r_refs = dict(zip(BIG, refs[nw:2 * nw]))
        send, recv = refs[2 * nw:]
        x, y, c, _ = _place()
        sib = (x, y, 1 - c)
        cps = []
        for wi, n in enumerate(BIG):
            cp = pltpu.make_async_remote_copy(src_ref=_half_of_full(g_refs[n], n, 1 - c), dst_ref=r_refs[n],
                                              send_sem=send.at[wi], recv_sem=recv.at[wi], device_id=sib, device_id_type=MESH)
            cp.start()
            cps.append(cp)
        for cp in cps:
            cp.wait()

    outs = [jax.ShapeDtypeStruct(_half_shape(g.shape, n), F32) for n, g in zip(BIG, grads)]
    return pl.pallas_call(
        body, name="swap_halves", in_specs=[ANY] * nw, out_specs=[ANY] * nw, out_shape=outs,
        scratch_shapes=[pltpu.SemaphoreType.DMA((nw,)), pltpu.SemaphoreType.DMA((nw,))],
    )(*grads)


def _add_half(cidx, g, r, name):
    rows, cols = r.shape
    tr = 256
    tc = cols if cols <= 1792 else (1792 if cols % 1792 == 0 else 1280)
    nr, nc = rows // tr, cols // tc

    def body(c_ref, g_ref, r_ref, o_ref):
        o_ref[...] = g_ref[...] + r_ref[...]

    if SHARD_AXIS[name] == 1:
        gmap = lambda i, j, c_ref: (c_ref[0] * nr + i, j)
    else:
        gmap = lambda i, j, c_ref: (i, c_ref[0] * nc + j)
    return pl.pallas_call(
        body, name="add_half_" + name,
        grid_spec=pltpu.PrefetchScalarGridSpec(
            num_scalar_prefetch=1, grid=(nr, nc),
            in_specs=[pl.BlockSpec((tr, tc), gmap), pl.BlockSpec((tr, tc), lambda i, j, c_ref: (i, j))],
            out_specs=pl.BlockSpec((tr, tc), lambda i, j, c_ref: (i, j))),
        out_shape=jax.ShapeDtypeStruct(r.shape, F32),
        compiler_params=pltpu.CompilerParams(dimension_semantics=("parallel", "parallel"), vmem_limit_bytes=VMEM_LIMIT),
    )(cidx, g, r)


def _scatter_blocks(halves):
    nw = len(BIG)

    def block_of(ref, name, block):
        rows, cols = ref.shape
        if SHARD_AXIS[name] == 1:
            cw = cols // N_CHIPS
            return ref.at[:, pl.ds(pl.multiple_of(block * cw, LANES), cw)]
        rw = rows // N_CHIPS
        return ref.at[pl.ds(pl.multiple_of(block * rw, rw), rw), :]

    def body(*refs):
        h_refs = dict(zip(BIG, refs[:nw]))
        r_refs = dict(zip(BIG, refs[nw:2 * nw]))
        send, recv = refs[2 * nw:]
        x, y, c, chips = _place()
        cps = []
        for wi, n in enumerate(BIG):
            for j, (cx, cy) in enumerate(chips):
                cp = pltpu.make_async_remote_copy(
                    src_ref=block_of(h_refs[n], n, 2 * cx + cy), dst_ref=r_refs[n].at[j],
                    send_sem=send.at[wi * 3 + j], recv_sem=recv.at[wi * 3 + j],
                    device_id=(cx, cy, c), device_id_type=MESH)
                cp.start()
                cps.append(cp)
        for cp in cps:
            cp.wait()

    outs = []
    for n, h in zip(BIG, halves):
        outs.append(jax.ShapeDtypeStruct((3,) + _shard_shape(h.shape, n), F32))
    return pl.pallas_call(
        body, name="scatter_blocks", in_specs=[ANY] * nw, out_specs=[ANY] * nw, out_shape=outs,
        scratch_shapes=[pltpu.SemaphoreType.DMA((3 * nw,)), pltpu.SemaphoreType.DMA((3 * nw,))],
    )(*halves)


def _add_blocks(bidx, h, r, name):
    _, rows, cols = r.shape
    tr = min(rows, 256)
    nr = rows // tr

    def body(b_ref, h_ref, r0, r1, r2, o_ref):
        o_ref[...] = ((h_ref[...] + r0[0]) + r1[0]) + r2[0]

    if SHARD_AXIS[name] == 1:
        hmap = lambda i, b_ref: (i, b_ref[0])
    else:
        hmap = lambda i, b_ref: (b_ref[0] * nr + i, 0)
    rspec = lambda j: pl.BlockSpec((1, tr, cols), lambda i, b_ref, j=j: (j, i, 0))
    return pl.pallas_call(
        body, name="add_blocks_" + name,
        grid_spec=pltpu.PrefetchScalarGridSpec(
            num_scalar_prefetch=1, grid=(nr,),
            in_specs=[pl.BlockSpec((tr, cols), hmap), rspec(0), rspec(1), rspec(2)],
            out_specs=pl.BlockSpec((tr, cols), lambda i, b_ref: (i, 0))),
        out_shape=jax.ShapeDtypeStruct((rows, cols), F32),
        compiler_params=pltpu.CompilerParams(dimension_semantics=("parallel",), vmem_limit_bytes=VMEM_LIMIT),
    )(bidx, h, r, r, r)


def _join_halves(parts):
    nw = len(BIG)

    def body(*refs):
        t_refs = dict(zip(BIG, refs[:nw]))
        o_refs = dict(zip(BIG, refs[nw:2 * nw]))
        send, recv, loc = refs[2 * nw:]
        x, y, c, _ = _place()
        sib = (x, y, 1 - c)
        cps, lcs = [], []
        for wi, n in enumerate(BIG):
            lc = pltpu.make_async_copy(t_refs[n], _half_of_shard(o_refs[n], n, c), loc.at[wi])
            lc.start()
            lcs.append(lc)
            cp = pltpu.make_async_remote_copy(src_ref=t_refs[n], dst_ref=_half_of_shard(o_refs[n], n, c),
                                              send_sem=send.at[wi], recv_sem=recv.at[wi], device_id=sib, device_id_type=MESH)
            cp.start()
            cps.append(cp)
        for wi, n in enumerate(BIG):
            other = _half_of_shard(o_refs[n], n, 1 - c)
            pltpu.make_async_remote_copy(src_ref=t_refs[n], dst_ref=other, send_sem=send.at[wi], recv_sem=recv.at[wi],
                                         device_id=sib, device_id_type=MESH).wait_recv()
        for cp in cps:
            cp.wait_send()
        for lc in lcs:
            lc.wait()

    outs = []
    for n, t in zip(BIG, parts):
        r, cdim = t.shape
        outs.append(jax.ShapeDtypeStruct((r * 2, cdim) if SHARD_AXIS[n] == 1 else (r, cdim * 2), F32))
    return pl.pallas_call(
        body, name="join_halves", in_specs=[ANY] * nw, out_specs=[ANY] * nw, out_shape=outs,
        scratch_shapes=[pltpu.SemaphoreType.DMA((nw,)), pltpu.SemaphoreType.DMA((nw,)), pltpu.SemaphoreType.DMA((nw,))],
    )(*parts)


def _adamw(w, g, m, v, name):
    rows, cols = w.shape
    tr = 128 if (rows % 128 == 0 and cols > LANES) else rows
    c1 = 1.0 - ADAM_B1 ** ADAM_STEP
    c2 = 1.0 - ADAM_B2 ** ADAM_STEP

    def body(w_ref, g_ref, m_ref, v_ref, d_ref, nm_ref, nv_ref):
        gv = g_ref[...]
        nm = ADAM_B1 * m_ref[...] + (1.0 - ADAM_B1) * gv
        nv = ADAM_B2 * v_ref[...] + (1.0 - ADAM_B2) * (gv * gv)
        d_ref[...] = -ADAM_LR * ((nm / c1) / (jnp.sqrt(nv / c2) + ADAM_EPS) + ADAM_WD * w_ref[...])
        nm_ref[...] = nm
        nv_ref[...] = nv

    spec = pl.BlockSpec((tr, cols), lambda i: (i, 0))
    return _call(body, "adamw_" + name, (rows // tr,), [spec] * 4, [spec] * 3,
                 [jax.ShapeDtypeStruct(w.shape, F32)] * 3, sem=("parallel",))(w, g, m, v)


SMALL = ("e_norm_w", "e_q_norm_w", "e_k_norm_w", "e_conv_w", "o_norm_w", "o_pool_w", "o_pool_scale",
         "o_dconv_w", "o_dconv_b", "o_ln_w", "o_ln_b")
SMALL_SHARDED = ("e_conv_w", "o_norm_w", "o_pool_scale", "o_dconv_w", "o_dconv_b", "o_ln_w", "o_ln_b")
WEIGHTS = ("e_norm_w", "e_w_in", "e_q_norm_w", "e_k_norm_w", "e_conv_w", "e_w_out", "o_norm_w", "o_w_in",
           "o_pool_w", "o_pool_scale", "o_dconv_w", "o_dconv_b", "o_ln_w", "o_ln_b", "o_w_out")


def _pack(arrs):
    flat = jnp.concatenate([a.reshape(-1) for a in arrs])
    rows = -(-flat.shape[0] // (LANES * SUBLANES)) * SUBLANES
    flat = jnp.pad(flat, (0, rows * LANES - flat.shape[0]))
    return flat.reshape(rows, LANES)


def _unpack(packed, shapes):
    flat = packed.reshape(-1)
    out, off = [], 0
    for s in shapes:
        n = int(np.prod(s))
        out.append(flat[off:off + n].reshape(s))
        off += n
    return out


def _gather_last(a, block, width):
    return lax.dynamic_slice_in_dim(a, block * width, width, axis=a.ndim - 1)


def kernel(x, positions, e_norm_w, e_w_in, e_q_norm_w, e_k_norm_w, e_conv_w, e_w_out, o_norm_w, o_w_in, o_pool_w, o_pool_scale, o_dconv_w, o_dconv_b, o_ln_w, o_ln_b, o_w_out, loss_target, m_e_norm_w, m_e_w_in, m_e_q_norm_w, m_e_k_norm_w, m_e_conv_w, m_e_w_out, m_o_norm_w, m_o_w_in, m_o_pool_w, m_o_pool_scale, m_o_dconv_w, m_o_dconv_b, m_o_ln_w, m_o_ln_b, m_o_w_out, v_e_norm_w, v_e_w_in, v_e_q_norm_w, v_e_k_norm_w, v_e_conv_w, v_e_w_out, v_o_norm_w, v_o_w_in, v_o_pool_w, v_o_pool_scale, v_o_dconv_w, v_o_dconv_b, v_o_ln_w, v_o_ln_b, v_o_w_out):
    given = dict(e_norm_w=e_norm_w, e_w_in=e_w_in, e_q_norm_w=e_q_norm_w, e_k_norm_w=e_k_norm_w, e_conv_w=e_conv_w,
                 e_w_out=e_w_out, o_norm_w=o_norm_w, o_w_in=o_w_in, o_pool_w=o_pool_w, o_pool_scale=o_pool_scale,
                 o_dconv_w=o_dconv_w, o_dconv_b=o_dconv_b, o_ln_w=o_ln_w, o_ln_b=o_ln_b, o_w_out=o_w_out)
    mom = dict(e_norm_w=m_e_norm_w, e_w_in=m_e_w_in, e_q_norm_w=m_e_q_norm_w, e_k_norm_w=m_e_k_norm_w,
               e_conv_w=m_e_conv_w, e_w_out=m_e_w_out, o_norm_w=m_o_norm_w, o_w_in=m_o_w_in, o_pool_w=m_o_pool_w,
               o_pool_scale=m_o_pool_scale, o_dconv_w=m_o_dconv_w, o_dconv_b=m_o_dconv_b, o_ln_w=m_o_ln_w,
               o_ln_b=m_o_ln_b, o_w_out=m_o_w_out)
    var = dict(e_norm_w=v_e_norm_w, e_w_in=v_e_w_in, e_q_norm_w=v_e_q_norm_w, e_k_norm_w=v_e_k_norm_w,
               e_conv_w=v_e_conv_w, e_w_out=v_e_w_out, o_norm_w=v_o_norm_w, o_w_in=v_o_w_in, o_pool_w=v_o_pool_w,
               o_pool_scale=v_o_pool_scale, o_dconv_w=v_o_dconv_w, o_dconv_b=v_o_dconv_b, o_ln_w=v_o_ln_w,
               o_ln_b=v_o_ln_b, o_w_out=v_o_w_out)
    S = x.shape[1]
    mx, my, mc = lax.axis_index("x"), lax.axis_index("y"), lax.axis_index("c")
    chip = 2 * mx + my
    cidx = jnp.reshape(mc, (1,)).astype(jnp.int32)
    bidx = jnp.reshape(chip, (1,)).astype(jnp.int32)

    full_b = _gather_weights([given[n][0].astype(BF16) for n in BIG])
    wb = dict(zip(BIG, full_b))
    shard_sizes = [int(np.prod(given[n].shape)) for n in SMALL_SHARDED]
    own = _pack([given[n] for n in SMALL_SHARDED])
    rows = own.shape[0]
    slots = jnp.zeros((N_CHIPS, rows, LANES), F32)
    own = jnp.where(mc == 0, own, 0.0)
    slots = lax.dynamic_update_slice(slots, own[None], (chip, 0, 0))
    gathered = _allreduce_small(slots.reshape(N_CHIPS * rows, LANES)).reshape(N_CHIPS, rows * LANES)
    p = {}
    off = 0
    for n, size in zip(SMALL_SHARDED, shard_sizes):
        sh = given[n].shape[1:]
        parts = gathered[:, off:off + size].reshape((N_CHIPS,) + sh)
        fullp = jnp.moveaxis(parts, 0, -2).reshape(sh[:-1] + (N_CHIPS * sh[-1],))
        p[n] = fullp.reshape(-1, fullp.shape[-1])
        off += size
    p["e_norm_w"] = e_norm_w
    p["e_q_norm_w"] = e_q_norm_w
    p["e_k_norm_w"] = e_k_norm_w
    p["o_pool_w"] = o_pool_w[0]

    lsum, dx, g = _local_step(x[0], positions.reshape(S, 1), loss_target[0], wb, p)
    loss = lax.psum(0.5 * jnp.sum(lsum) / float(D_MODEL), ("x", "y", "c"))

    tot = _unpack(_allreduce_small(_pack([g[n] for n in SMALL])), [g[n].shape for n in SMALL])
    gsmall = dict(zip(SMALL, tot))
    grads = {}
    for n in SMALL:
        gv = gsmall[n]
        if n in SMALL_SHARDED:
            gv = _gather_last(gv, chip, gv.shape[-1] // N_CHIPS)
        grads[n] = gv.reshape(given[n].shape)

    recv_half = _swap_halves([g[n] for n in BIG])
    halves = [_add_half(cidx, g[n], r, n) for n, r in zip(BIG, recv_half)]
    recv_blk = _scatter_blocks(halves)
    parts = [_add_blocks(bidx, h, r, n) for n, h, r in zip(BIG, halves, recv_blk)]
    for n, gs in zip(BIG, _join_halves(parts)):
        grads[n] = gs.reshape(given[n].shape)

    delta, new_m, new_v = {}, {}, {}
    for n in BIG:
        sh = given[n].shape
        d, nm, nv = _adamw(given[n][0], grads[n][0], mom[n][0], var[n][0], n)
        delta[n], new_m[n], new_v[n] = d.reshape(sh), nm.reshape(sh), nv.reshape(sh)
    shapes = [given[n].shape for n in SMALL]
    packed = [_pack([src[n] for n in SMALL]) for src in (given, grads, mom, var)]
    for dst, pk in zip((delta, new_m, new_v), _adamw(*packed, "small")):
        for n, a in zip(SMALL, _unpack(pk, shapes)):
            dst[n] = a
    return (loss, dx[None], *[grads[n] for n in WEIGHTS], *[delta[n] for n in WEIGHTS],
            *[new_m[n] for n in WEIGHTS], *[new_v[n] for n in WEIGHTS])
```

```python
import numpy as np
import jax
import jax.numpy as jnp
from jax import lax
from jax.experimental import pallas as pl
from jax.experimental.pallas import tpu as pltpu

F32 = jnp.float32
BF16 = jnp.bfloat16

D_MODEL = 1024
HEAD_DIM = 64
A_WIDTH = 512
A_HEADS = 8
A_GROUPS = ((128, 1), (512, 4), (2048, 16))
QBLK = 128
ROT_DIM = 16
ROPE_THETA = 500000.0
POOL_SIZES = (2, 4, 8, 16)
D_CONV = 31
SC_WIDTH = 3
EVEN_IN = 7168
ODD_IN = 2560
EPS = 1e-6
NEG = -1e30
ADAM_LR, ADAM_B1, ADAM_B2, ADAM_EPS, ADAM_WD, ADAM_STEP = 0.001, 0.9, 0.999, 1e-08, 0.01, 10

LANES = 128
SUBLANES = 8
HALO = 32
VMEM_LIMIT = 52 * 1024 * 1024
MESH = pl.DeviceIdType.MESH

NT_DIMS = (((1,), (1,)), ((), ()))
TN_DIMS = (((0,), (0,)), ((), ()))


def _call(body, name, grid, in_specs, out_specs, out_shape, scratch=(), sem=None, aliases=None):
    return pl.pallas_call(
        body, name=name, grid=grid, in_specs=in_specs, out_specs=out_specs, out_shape=out_shape,
        scratch_shapes=list(scratch), input_output_aliases=aliases or {},
        compiler_params=pltpu.CompilerParams(dimension_semantics=sem, vmem_limit_bytes=VMEM_LIMIT))


def _sig(v):
    return jax.nn.sigmoid(v)


def _dsilu(v, s):
    return s * (1.0 + v * (1.0 - s))


def _cs8(v):
    return v.reshape(v.shape[0] // SUBLANES, SUBLANES, v.shape[1]).sum(axis=0)


def _seg_ones():
    r = lax.broadcasted_iota(jnp.int32, (LANES, LANES), 0) // HEAD_DIM
    c = lax.broadcasted_iota(jnp.int32, (LANES, LANES), 1) // HEAD_DIM
    return (r == c).astype(BF16)


def _segsum(v, ones):
    hi = v.astype(BF16)
    lo = (v - hi.astype(F32)).astype(BF16)
    return (jnp.dot(hi, ones, preferred_element_type=F32) + jnp.dot(lo, ones, preferred_element_type=F32))


def _rope_tables(pos_ref, freq_ref):
    ang = pos_ref[...].astype(F32) * freq_ref[...]
    cosv, sinv = jnp.cos(ang), jnp.sin(ang)
    lm = lax.broadcasted_iota(jnp.int32, ang.shape, 1) % HEAD_DIM
    half = ROT_DIM // 2
    c = jnp.where(lm < ROT_DIM, cosv, 1.0)
    s1 = jnp.where((lm >= half) & (lm < ROT_DIM), sinv, 0.0)
    s2 = jnp.where(lm < half, -sinv, 0.0)
    return c, s1, s2


def _freq_table():
    half = ROT_DIM // 2
    inv = ROPE_THETA ** (-np.arange(half, dtype=np.float64) / half)
    lane = np.arange(LANES) % HEAD_DIM
    f = np.where(lane < ROT_DIM, inv[lane % half], 0.0)
    return jnp.asarray(f.reshape(1, LANES), F32)


def _inproj(x, nw, w, tm, tn, name):
    S, N = x.shape[0], w.shape[1]

    def body(x_ref, nw_ref, w_ref, o_ref, ht_ref, hs):
        @pl.when(pl.program_id(1) == 0)
        def _():
            xv = x_ref[...]
            ms = jnp.mean(xv * xv, axis=-1, keepdims=True)
            h = xv * lax.rsqrt(ms + EPS) * nw_ref[...]
            hs[...] = h.astype(BF16)
            ht_ref[...] = h.T.astype(BF16)
        o_ref[...] = jnp.dot(hs[...], w_ref[...], preferred_element_type=F32)

    return _call(
        body, name, (S // tm, N // tn),
        [pl.BlockSpec((tm, D_MODEL), lambda i, j: (i, 0)),
         pl.BlockSpec((1, D_MODEL), lambda i, j: (0, 0)),
         pl.BlockSpec((D_MODEL, tn), lambda i, j: (0, j))],
        [pl.BlockSpec((tm, tn), lambda i, j: (i, j)),
         pl.BlockSpec((D_MODEL, tm), lambda i, j: (0, i))],
        [jax.ShapeDtypeStruct((S, N), F32), jax.ShapeDtypeStruct((D_MODEL, S), BF16)],
        scratch=[pltpu.VMEM((tm, D_MODEL), BF16)], sem=("parallel", "arbitrary"))(x, nw, w)


def _outproj(x, u, w, tm, name):
    S = x.shape[0]

    def body(x_ref, u_ref, w_ref, o_ref):
        o_ref[...] = x_ref[...] + jnp.dot(u_ref[...].astype(BF16), w_ref[...], preferred_element_type=F32)

    return _call(
        body, name, (S // tm,),
        [pl.BlockSpec((tm, D_MODEL), lambda i: (i, 0)),
         pl.BlockSpec((tm, D_MODEL), lambda i: (i, 0)),
         pl.BlockSpec((D_MODEL, D_MODEL), lambda i: (0, 0))],
        pl.BlockSpec((tm, D_MODEL), lambda i: (i, 0)),
        jax.ShapeDtypeStruct((S, D_MODEL), F32), sem=("parallel",))(x, u, w)


def _outproj_loss(x, u, w, tgt, tm, name):
    S = x.shape[0]

    def body(x_ref, u_ref, w_ref, t_ref, dy_ref, l_ref, acc):
        i = pl.program_id(0)

        @pl.when(i == 0)
        def _():
            acc[...] = jnp.zeros_like(acc)
        y = x_ref[...] + jnp.dot(u_ref[...].astype(BF16), w_ref[...], preferred_element_type=F32)
        diff = y - t_ref[...]
        dy_ref[...] = diff / float(D_MODEL)
        acc[...] += _cs8(diff * diff)

        @pl.when(i == pl.num_programs(0) - 1)
        def _():
            l_ref[...] = jnp.sum(acc[...], axis=0, keepdims=True)

    return _call(
        body, name, (S // tm,),
        [pl.BlockSpec((tm, D_MODEL), lambda i: (i, 0)),
         pl.BlockSpec((tm, D_MODEL), lambda i: (i, 0)),
         pl.BlockSpec((D_MODEL, D_MODEL), lambda i: (0, 0)),
         pl.BlockSpec((tm, D_MODEL), lambda i: (i, 0))],
        [pl.BlockSpec((tm, D_MODEL), lambda i: (i, 0)),
         pl.BlockSpec((1, D_MODEL), lambda i: (0, 0))],
        [jax.ShapeDtypeStruct((S, D_MODEL), F32), jax.ShapeDtypeStruct((1, D_MODEL), F32)],
        scratch=[pltpu.VMEM((SUBLANES, D_MODEL), F32)], sem=("arbitrary",))(x, u, w, tgt)


def _mm_nt(a, w, tm, name):
    S, K = a.shape
    N = w.shape[0]

    def body(a_ref, w_ref, o_ref):
        o_ref[...] = lax.dot_general(a_ref[...].astype(BF16), w_ref[...], NT_DIMS, preferred_element_type=F32)

    return _call(
        body, name, (S // tm,),
        [pl.BlockSpec((tm, K), lambda i: (i, 0)), pl.BlockSpec((N, K), lambda i: (0, 0))],
        pl.BlockSpec((tm, N), lambda i: (i, 0)),
        jax.ShapeDtypeStruct((S, N), F32), sem=("parallel",))(a, w)


def _piece_blocks(pieces, tk, axis):
    starts, counts, s = [], [], 0
    for p in pieces:
        n = p.shape[axis] // tk
        starts.append(s)
        counts.append(n)
        s += n
    return starts, counts, s


def _mm_nt_rms(pieces, w, x, nw, dres, tm, tk, name):
    S = x.shape[0]
    starts, counts, nk = _piece_blocks(pieces, tk, 1)
    npc = len(pieces)
    ni = S // tm

    def body(*refs):
        p_refs = refs[:npc]
        w_ref, x_ref, nw_ref, dr_ref, dx_ref, dnw_ref, acc, nacc = refs[npc:]
        i, k = pl.program_id(0), pl.program_id(1)

        @pl.when(k == 0)
        def _():
            acc[...] = jnp.zeros_like(acc)

        @pl.when((i == 0) & (k == 0))
        def _():
            nacc[...] = jnp.zeros_like(nacc)

        for p in range(npc):
            @pl.when((k >= starts[p]) & (k < starts[p] + counts[p]))
            def _(p=p):
                acc[...] += lax.dot_general(p_refs[p][...].astype(BF16), w_ref[...], NT_DIMS,
                                            preferred_element_type=F32)

        @pl.when(k == nk - 1)
        def _():
            xv = x_ref[...]
            rs = lax.rsqrt(jnp.mean(xv * xv, axis=-1, keepdims=True) + EPS)
            xh = xv * rs
            dh = acc[...]
            nacc[...] += _cs8(dh * xh)
            dxh = dh * nw_ref[...]
            dx_ref[...] = dr_ref[...] + rs * (dxh - xh * jnp.mean(dxh * xh, axis=-1, keepdims=True))

        @pl.when((i == ni - 1) & (k == nk - 1))
        def _():
            dnw_ref[...] = jnp.sum(nacc[...], axis=0, keepdims=True)

    def pspec(p):
        return pl.BlockSpec((tm, tk), lambda i, k: (i, jnp.clip(k - starts[p], 0, counts[p] - 1)))

    return _call(
        body, name, (ni, nk),
        [pspec(p) for p in range(npc)] +
        [pl.BlockSpec((D_MODEL, tk), lambda i, k: (0, k)),
         pl.BlockSpec((tm, D_MODEL), lambda i, k: (i, 0)),
         pl.BlockSpec((1, D_MODEL), lambda i, k: (0, 0)),
         pl.BlockSpec((tm, D_MODEL), lambda i, k: (i, 0))],
        [pl.BlockSpec((tm, D_MODEL), lambda i, k: (i, 0)),
         pl.BlockSpec((1, D_MODEL), lambda i, k: (0, 0))],
        [jax.ShapeDtypeStruct((S, D_MODEL), F32), jax.ShapeDtypeStruct((1, D_MODEL), F32)],
        scratch=[pltpu.VMEM((tm, D_MODEL), F32), pltpu.VMEM((SUBLANES, D_MODEL), F32)],
        sem=("arbitrary", "arbitrary"))(*pieces, w, x, nw, dres)


def _mm_wgrad(at, pieces, tn, tk, name):
    M, S = at.shape
    starts, counts, nj = _piece_blocks(pieces, tn, 1)
    npc = len(pieces)
    ns = S // tk

    def body(*refs):
        a_ref = refs[0]
        p_refs = refs[1:1 + npc]
        o_ref = refs[1 + npc]
        j, s = pl.program_id(0), pl.program_id(1)

        @pl.when(s == 0)
        def _():
            o_ref[...] = jnp.zeros_like(o_ref)

        for p in range(npc):
            @pl.when((j >= starts[p]) & (j < starts[p] + counts[p]))
            def _(p=p):
                o_ref[...] += jnp.dot(a_ref[...], p_refs[p][...].astype(BF16), preferred_element_type=F32)

    def pspec(p):
        def imap(j, s):
            active = (j >= starts[p]) & (j < starts[p] + counts[p])
            return (jnp.where(active, s, 0), jnp.clip(j - starts[p], 0, counts[p] - 1))
        return pl.BlockSpec((tk, tn), imap)

    return _call(
        body, name, (nj, ns),
        [pl.BlockSpec((M, tk), lambda j, s: (0, s))] + [pspec(p) for p in range(npc)],
        pl.BlockSpec((M, tn), lambda j, s: (0, j)),
        jax.ShapeDtypeStruct((M, nj * tn), F32), sem=("parallel", "arbitrary"))(at, *pieces)


def _stream_spec(d, T):
    return pl.BlockSpec((d, T // d, A_WIDTH), lambda i: (0, i, 0))


def _stream_shape(d, S, dtype):
    return jax.ShapeDtypeStruct((d, S // d, A_WIDTH), dtype)


N_CHUNK = A_WIDTH // LANES


def _to_tokens(ref, scr, d, T):
    if d == 1:
        return ref[0].astype(F32)
    for r in range(d):
        for ch in range(N_CHUNK):
            scr.at[ch][pl.ds(r, T // d, stride=d), :] = ref[r, :, ch * LANES:(ch + 1) * LANES].astype(F32)
    return _get(scr)


def _from_tokens(out_ref, scr, d, T):
    for r in range(d):
        for ch in range(N_CHUNK):
            out_ref[r, :, ch * LANES:(ch + 1) * LANES] = scr.at[ch][pl.ds(r, T // d, stride=d), :].astype(out_ref.dtype)


def _put(scr, val):
    for ch in range(N_CHUNK):
        scr[ch] = val[:, ch * LANES:(ch + 1) * LANES]


def _get(scr):
    return jnp.concatenate([scr[ch] for ch in range(N_CHUNK)], axis=1)


def _chunked(T):
    return pltpu.VMEM((N_CHUNK, T, LANES), F32)


def _qkv_prep(proj, pos, freq, wq, wk, T):
    S = proj.shape[0]
    qk_w = 3 * A_WIDTH

    def body(q_ref, k_ref, v_ref, pos_ref, f_ref, wq_ref, wk_ref, *rest):
        outs, scr = rest[:9], rest[9]
        ones = _seg_ones()
        c, s1, s2 = _rope_tables(pos_ref, f_ref)
        for t, (src, w_ref) in enumerate(((q_ref, wq_ref), (k_ref, wk_ref), (v_ref, None))):
            for g in range(3):
                d = A_GROUPS[g][1]
                out = outs[3 * t + g]
                for ch in range(A_WIDTH // LANES):
                    cs = slice(ch * LANES, (ch + 1) * LANES)
                    v = src[:, g * A_WIDTH + ch * LANES: g * A_WIDTH + (ch + 1) * LANES]
                    if w_ref is not None:
                        rs = lax.rsqrt(_segsum(v * v, ones) * (1.0 / HEAD_DIM) + EPS)
                        y = v * rs * w_ref[...]
                        v = y * c + pltpu.roll(y, 8, 1) * s1 + pltpu.roll(y, LANES - 8, 1) * s2
                    if d == 1:
                        out[0, :, cs] = v.astype(BF16)
                    else:
                        scr[ch] = v
                if d > 1:
                    _from_tokens(out, scr, d, T)

    ds_ = [A_GROUPS[g][1] for g in range(3)] * 3
    return _call(
        body, "qkv_prep", (S // T,),
        [pl.BlockSpec((T, qk_w), lambda i: (i, 0)), pl.BlockSpec((T, qk_w), lambda i: (i, 1)),
         pl.BlockSpec((T, qk_w), lambda i: (i, 2)),
         pl.BlockSpec((T, 1), lambda i: (i, 0)), pl.BlockSpec((1, LANES), lambda i: (0, 0)),
         pl.BlockSpec((1, LANES), lambda i: (0, 0)), pl.BlockSpec((1, LANES), lambda i: (0, 0))],
        [_stream_spec(d, T) for d in ds_],
        [_stream_shape(d, S, BF16) for d in ds_],
        scratch=[_chunked(T)], sem=("parallel",))(proj, proj, proj, pos, freq, wq, wk)


def _attn_mask(i):
    qi = lax.broadcasted_iota(jnp.int32, (QBLK, 2 * QBLK), 0) + QBLK
    kj = lax.broadcasted_iota(jnp.int32, (QBLK, 2 * QBLK), 1)
    dist = qi - kj
    return (dist >= 0) & (dist <= QBLK) & ((i > 0) | (kj >= QBLK))


ATT_BLK = (None, QBLK, A_WIDTH)


def _attn_fwd(q, k, v, g):
    d, n, _ = q.shape
    nb = n // QBLK

    def body(q_ref, kp_ref, kc_ref, vp_ref, vc_ref, o_ref, l_ref):
        i = pl.program_id(1)
        mask = _attn_mask(i)
        for h in range(A_HEADS):
            hs = slice(h * HEAD_DIM, (h + 1) * HEAD_DIM)
            kc = jnp.concatenate([kp_ref[:, hs], kc_ref[:, hs]], axis=0)
            vc = jnp.concatenate([vp_ref[:, hs], vc_ref[:, hs]], axis=0)
            s = lax.dot_general(q_ref[:, hs], kc, NT_DIMS, preferred_element_type=F32) * (HEAD_DIM ** -0.5)
            s = jnp.where(mask, s, NEG)
            m = jnp.max(s, axis=-1, keepdims=True)
            p = jnp.exp(s - m)
            den = jnp.sum(p, axis=-1, keepdims=True)
            o_ref[:, hs] = jnp.dot((p / den).astype(BF16), vc, preferred_element_type=F32)
            l_ref[:, hs] = jnp.broadcast_to(m + jnp.log(den), (QBLK, HEAD_DIM))

    prev = lambda r, i: (r, jnp.maximum(i - 1, 0), 0)
    cur = lambda r, i: (r, i, 0)
    return _call(
        body, "attn_fwd_g%d" % g, (d, nb),
        [pl.BlockSpec(ATT_BLK, cur), pl.BlockSpec(ATT_BLK, prev), pl.BlockSpec(ATT_BLK, cur),
         pl.BlockSpec(ATT_BLK, prev), pl.BlockSpec(ATT_BLK, cur)],
        [pl.BlockSpec(ATT_BLK, cur), pl.BlockSpec(ATT_BLK, cur)],
        [jax.ShapeDtypeStruct((d, n, A_WIDTH), F32)] * 2, sem=("parallel", "parallel"))(q, k, k, v, v)


def _attn_bwd(q, k, v, do, lse, cg, g):
    d, n, _ = q.shape
    nb = n // QBLK
    scale = HEAD_DIM ** -0.5

    def body(q_ref, kp_ref, kc_ref, vp_ref, vc_ref, do_ref, l_ref, c_ref, dq_ref, dk_ref, dv_ref, ck, cv):
        i = pl.program_id(1)

        @pl.when(i == 0)
        def _():
            ck[...] = jnp.zeros_like(ck)
            cv[...] = jnp.zeros_like(cv)

        @pl.when(i < nb)
        def _():
            mask = _attn_mask(i)
            for h in range(A_HEADS):
                hs = slice(h * HEAD_DIM, (h + 1) * HEAD_DIM)
                qh = q_ref[:, hs]
                doh = do_ref[:, hs]
                kc = jnp.concatenate([kp_ref[:, hs], kc_ref[:, hs]], axis=0)
                vc = jnp.concatenate([vp_ref[:, hs], vc_ref[:, hs]], axis=0)
                s = lax.dot_general(qh, kc, NT_DIMS, preferred_element_type=F32) * scale
                p = jnp.where(mask, jnp.exp(s - l_ref[:, h * HEAD_DIM:h * HEAD_DIM + 1]), 0.0)
                dp = lax.dot_general(doh, vc, NT_DIMS, preferred_element_type=F32)
                ds = (p * (dp + c_ref[:, h * HEAD_DIM:h * HEAD_DIM + 1]) * scale).astype(BF16)
                dq_ref[:, hs] = jnp.dot(ds, kc, preferred_element_type=F32)
                dkc = lax.dot_general(ds, qh, TN_DIMS, preferred_element_type=F32)
                dvc = lax.dot_general(p.astype(BF16), doh, TN_DIMS, preferred_element_type=F32)
                dk_ref[:, hs] = ck[:, hs] + dkc[:QBLK]
                dv_ref[:, hs] = cv[:, hs] + dvc[:QBLK]
                ck[:, hs] = dkc[QBLK:]
                cv[:, hs] = dvc[QBLK:]

        @pl.when(i == nb)
        def _():
            dk_ref[...] = ck[...]
            dv_ref[...] = cv[...]

    qi = lambda i: jnp.minimum(i, nb - 1)
    cur = lambda r, i: (r, qi(i), 0)
    prev = lambda r, i: (r, jnp.maximum(qi(i) - 1, 0), 0)
    late = lambda r, i: (r, jnp.maximum(i - 1, 0), 0)
    return _call(
        body, "attn_bwd_g%d" % g, (d, nb + 1),
        [pl.BlockSpec(ATT_BLK, cur), pl.BlockSpec(ATT_BLK, prev), pl.BlockSpec(ATT_BLK, cur),
         pl.BlockSpec(ATT_BLK, prev), pl.BlockSpec(ATT_BLK, cur),
         pl.BlockSpec(ATT_BLK, cur), pl.BlockSpec(ATT_BLK, cur), pl.BlockSpec(ATT_BLK, cur)],
        [pl.BlockSpec(ATT_BLK, cur), pl.BlockSpec(ATT_BLK, late), pl.BlockSpec(ATT_BLK, late)],
        [jax.ShapeDtypeStruct((d, n, A_WIDTH), F32)] * 3,
        scratch=[pltpu.VMEM((QBLK, A_WIDTH), F32), pltpu.VMEM((QBLK, A_WIDTH), F32)],
        sem=("parallel", "arbitrary"))(q, k, k, v, v, do, lse, cg)


def _merge_weights(l0, l1, l2):
    mx = jnp.maximum(jnp.maximum(l0, l1), l2)
    e0, e1, e2 = jnp.exp(l0 - mx), jnp.exp(l1 - mx), jnp.exp(l2 - mx)
    den = e0 + e1 + e2
    return e0 / den, e1 / den, e2 / den


def _even_specs(T, S):
    t8 = T // SUBLANES
    last8 = S // SUBLANES - 1
    col = lambda c: pl.BlockSpec((T, A_WIDTH), lambda i: (i, c))
    prev8 = lambda c: pl.BlockSpec((SUBLANES, A_WIDTH), lambda i: (jnp.maximum(i * t8 - 1, 0), c))
    next8 = lambda c: pl.BlockSpec((SUBLANES, A_WIDTH), lambda i: (jnp.minimum((i + 1) * t8, last8), c))
    return col, prev8, next8


GROUP_D = tuple(d for _, d in A_GROUPS)


def _even_mixer_fwd(proj, os_, ls_, conv_w, T):
    S = proj.shape[0]
    col, prev8, _ = _even_specs(T, S)
    H = SUBLANES

    def body(bg_r, cg_r, hb_r, zl_r, zh_r, cgp_r, hbp_r, o0, o1, o2, l0, l1, l2, cw_r, u_ref, ut_ref, ext, *scr):
        i = pl.program_id(0)
        ls = [_to_tokens(r, scr[g], GROUP_D[g], T) for g, r in enumerate((l0, l1, l2))]
        ws = _merge_weights(*ls)
        oa = ws[0] * _to_tokens(o0, scr[0], GROUP_D[0], T)
        oa = oa + ws[1] * _to_tokens(o1, scr[1], GROUP_D[1], T)
        oa = oa + ws[2] * _to_tokens(o2, scr[2], GROUP_D[2], T)
        ext[0:H, :] = jnp.where(i == 0, 0.0, cgp_r[...] * hbp_r[...])
        ext[H:H + T, :] = cg_r[...] * hb_r[...]
        conv = cw_r[0:1, :] * ext[H - 2:H - 2 + T, :]
        for kk in range(1, SC_WIDTH):
            conv = conv + cw_r[kk:kk + 1, :] * ext[H - 2 + kk:H - 2 + kk + T, :]
        zl, zh = zl_r[...], zh_r[...]
        u_ref[:, 0:A_WIDTH] = oa * (zl * _sig(zl))
        u_ref[:, A_WIDTH:] = bg_r[...] * conv * (zh * _sig(zh))
        ut_ref[...] = u_ref[...].T.astype(BF16)

    streams = [_stream_spec(d, T) for d in GROUP_D]
    return _call(
        body, "even_mixer_fwd", (S // T,),
        [col(9), col(10), col(11), col(12), col(13), prev8(10), prev8(11)] + streams + streams +
        [pl.BlockSpec((SC_WIDTH, A_WIDTH), lambda i: (0, 0))],
        [pl.BlockSpec((T, D_MODEL), lambda i: (i, 0)), pl.BlockSpec((D_MODEL, T), lambda i: (0, i))],
        [jax.ShapeDtypeStruct((S, D_MODEL), F32), jax.ShapeDtypeStruct((D_MODEL, S), BF16)],
        scratch=[pltpu.VMEM((T + H, A_WIDTH), F32)] + [_chunked(T)] * 3, sem=("parallel",))(
            proj, proj, proj, proj, proj, proj, proj, *os_, *ls_, conv_w)


def _even_mixer_bwd(du, proj, os_, ls_, conv_w, T):
    S = proj.shape[0]
    nt = S // T
    col, prev8, next8 = _even_specs(T, S)
    H = SUBLANES
    t8 = T // SUBLANES
    last8 = S // SUBLANES - 1

    def body(dul_r, duh_r, bg_r, cg_r, hb_r, zl_r, zh_r, cgp_r, hbp_r, dun_r, zhn_r, bgn_r,
             o0, o1, o2, l0, l1, l2, cw_r,
             do0, do1, do2, c0, c1, c2, dr_ref, dcw_ref, ext_t, ext_d, acc, s_a, s_b, s_c, *ws_scr):
        i = pl.program_id(0)

        @pl.when(i == 0)
        def _():
            acc[...] = jnp.zeros_like(acc)

        ones = _seg_ones()
        zl, zh = zl_r[...], zh_r[...]
        sl, sh = _sig(zl), _sig(zh)
        dul, duh = dul_r[...], duh_r[...]
        scr = (s_a, s_b, s_c)
        ls = [_to_tokens(r, scr[g], GROUP_D[g], T) for g, r in enumerate((l0, l1, l2))]
        ws = _merge_weights(*ls)
        for g in range(3):
            _put(ws_scr[g], ws[g])
        oa = ws[0] * _to_tokens(o0, scr[0], GROUP_D[0], T)
        oa = oa + ws[1] * _to_tokens(o1, scr[1], GROUP_D[1], T)
        oa = oa + ws[2] * _to_tokens(o2, scr[2], GROUP_D[2], T)
        doa = dul * (zl * sl)
        _put(s_a, doa)
        prod = doa * oa
        for ch in range(N_CHUNK):
            s_b[ch] = _segsum(prod[:, ch * LANES:(ch + 1) * LANES], ones)
        for g, (do_ref, c_ref) in enumerate(((do0, c0), (do1, c1), (do2, c2))):
            d = GROUP_D[g]
            if d == 1:
                for ch in range(N_CHUNK):
                    cs = slice(ch * LANES, (ch + 1) * LANES)
                    do_ref[0, :, cs] = (ws_scr[g][ch] * s_a[ch]).astype(BF16)
                    c_ref[0, :, cs] = -ws_scr[g][ch] * s_b[ch]
            else:
                for ch in range(N_CHUNK):
                    s_c[ch] = ws_scr[g][ch] * s_a[ch]
                _from_tokens(do_ref, s_c, d, T)
                for ch in range(N_CHUNK):
                    s_c[ch] = -ws_scr[g][ch] * s_b[ch]
                _from_tokens(c_ref, s_c, d, T)
        cgv, hbv, bgv = cg_r[...], hb_r[...], bg_r[...]
        ext_t[0:H, :] = jnp.where(i == 0, 0.0, cgp_r[...] * hbp_r[...])
        ext_t[H:H + T, :] = cgv * hbv
        conv = cw_r[0:1, :] * ext_t[H - 2:H - 2 + T, :]
        for kk in range(1, SC_WIDTH):
            conv = conv + cw_r[kk:kk + 1, :] * ext_t[H - 2 + kk:H - 2 + kk + T, :]
        dyb = duh * (zh * sh)
        dconv = dyb * bgv
        zn = zhn_r[...]
        ext_d[0:T, :] = dconv
        ext_d[T:T + H, :] = jnp.where(i == nt - 1, 0.0, dun_r[...] * (zn * _sig(zn)) * bgn_r[...])
        dt = cw_r[0:1, :] * ext_d[2:2 + T, :]
        for kk in range(1, SC_WIDTH):
            dt = dt + cw_r[kk:kk + 1, :] * ext_d[2 - kk:2 - kk + T, :]
        for kk in range(SC_WIDTH):
            acc[kk * SUBLANES:(kk + 1) * SUBLANES, :] += _cs8(dconv * ext_t[H - 2 + kk:H - 2 + kk + T, :])
        dr_ref[:, 0:A_WIDTH] = dyb * conv
        dr_ref[:, A_WIDTH:2 * A_WIDTH] = dt * hbv
        dr_ref[:, 2 * A_WIDTH:3 * A_WIDTH] = dt * cgv
        dr_ref[:, 3 * A_WIDTH:4 * A_WIDTH] = dul * oa * _dsilu(zl, sl)
        dr_ref[:, 4 * A_WIDTH:5 * A_WIDTH] = duh * (bgv * conv) * _dsilu(zh, sh)

        @pl.when(i == nt - 1)
        def _():
            for kk in range(SC_WIDTH):
                dcw_ref[kk:kk + 1, :] = jnp.sum(acc[kk * SUBLANES:(kk + 1) * SUBLANES, :], axis=0, keepdims=True)

    streams = [_stream_spec(d, T) for d in GROUP_D]
    dunext = pl.BlockSpec((SUBLANES, A_WIDTH), lambda i: (jnp.minimum((i + 1) * t8, last8), 1))
    tile = _chunked(T)
    outs = _call(
        body, "even_mixer_bwd", (nt,),
        [pl.BlockSpec((T, A_WIDTH), lambda i: (i, 0)), pl.BlockSpec((T, A_WIDTH), lambda i: (i, 1)),
         col(9), col(10), col(11), col(12), col(13), prev8(10), prev8(11), dunext, next8(13), next8(9)] +
        streams + streams + [pl.BlockSpec((SC_WIDTH, A_WIDTH), lambda i: (0, 0))],
        streams + streams + [pl.BlockSpec((T, 5 * A_WIDTH), lambda i: (i, 0)),
                             pl.BlockSpec((SC_WIDTH, A_WIDTH), lambda i: (0, 0))],
        [_stream_shape(d, S, BF16) for d in GROUP_D] + [_stream_shape(d, S, F32) for d in GROUP_D] +
        [jax.ShapeDtypeStruct((S, 5 * A_WIDTH), F32), jax.ShapeDtypeStruct((SC_WIDTH, A_WIDTH), F32)],
        scratch=[pltpu.VMEM((T + H, A_WIDTH), F32), pltpu.VMEM((T + H, A_WIDTH), F32),
                 pltpu.VMEM((SC_WIDTH * SUBLANES, A_WIDTH), F32)] + [tile] * 6,
        sem=("arbitrary",))(du, du, proj, proj, proj, proj, proj, proj, proj, du, proj, proj, *os_, *ls_, conv_w)
    return outs[0:3], outs[3:6], outs[6], outs[7]


def _qk_bwd(proj, dqs, dks, dvs, pos, freq, wq, wk, T):
    S = proj.shape[0]
    nt = S // T
    qk_w = 3 * A_WIDTH

    def body(q_ref, k_ref, dq0, dq1, dq2, dk0, dk1, dk2, dv0, dv1, dv2, pos_ref, f_ref, wq_ref, wk_ref,
             o_ref, dw_ref, acc, scr):
        i = pl.program_id(0)

        @pl.when(i == 0)
        def _():
            acc[...] = jnp.zeros_like(acc)
            dw_ref[...] = jnp.zeros_like(dw_ref)

        ones = _seg_ones()
        c, s1, s2 = _rope_tables(pos_ref, f_ref)
        for t, (src, w_ref, ds) in enumerate(((q_ref, wq_ref, (dq0, dq1, dq2)), (k_ref, wk_ref, (dk0, dk1, dk2)))):
            wv = w_ref[...]
            for g in range(3):
                d = GROUP_D[g]
                if d > 1:
                    _to_tokens(ds[g], scr, d, T)
                for ch in range(A_WIDTH // LANES):
                    cs = slice(g * A_WIDTH + ch * LANES, g * A_WIDTH + (ch + 1) * LANES)
                    lc = slice(ch * LANES, (ch + 1) * LANES)
                    v = src[:, cs]
                    dout = ds[g][0, :, lc] if d == 1 else scr[ch]
                    rs = lax.rsqrt(_segsum(v * v, ones) * (1.0 / HEAD_DIM) + EPS)
                    xh = v * rs
                    dy = dout * c + pltpu.roll(dout * s1, LANES - 8, 1) + pltpu.roll(dout * s2, 8, 1)
                    acc[t * SUBLANES:(t + 1) * SUBLANES, :] += _cs8(dy * xh)
                    dxh = dy * wv
                    mean = _segsum(dxh * xh, ones) * (1.0 / HEAD_DIM)
                    o_ref[:, t * qk_w + g * A_WIDTH + ch * LANES: t * qk_w + g * A_WIDTH + (ch + 1) * LANES] = (
                        rs * (dxh - xh * mean))
        for g, dv in enumerate((dv0, dv1, dv2)):
            d = GROUP_D[g]
            base = 2 * qk_w + g * A_WIDTH
            o_ref[:, base:base + A_WIDTH] = _to_tokens(dv, scr, d, T)

        @pl.when(i == nt - 1)
        def _():
            for t in range(2):
                srow = jnp.sum(acc[t * SUBLANES:(t + 1) * SUBLANES, :], axis=0, keepdims=True)
                dw_ref[t:t + 1, :] = srow + pltpu.roll(srow, HEAD_DIM, 1)

    streams = [_stream_spec(d, T) for d in GROUP_D]
    return _call(
        body, "qk_bwd", (nt,),
        [pl.BlockSpec((T, qk_w), lambda i: (i, 0)), pl.BlockSpec((T, qk_w), lambda i: (i, 1))] + streams * 3 +
        [pl.BlockSpec((T, 1), lambda i: (i, 0)), pl.BlockSpec((1, LANES), lambda i: (0, 0)),
         pl.BlockSpec((1, LANES), lambda i: (0, 0)), pl.BlockSpec((1, LANES), lambda i: (0, 0))],
        [pl.BlockSpec((T, 3 * qk_w), lambda i: (i, 0)), pl.BlockSpec((SUBLANES, LANES), lambda i: (0, 0))],
        [jax.ShapeDtypeStruct((S, 3 * qk_w), F32), jax.ShapeDtypeStruct((SUBLANES, LANES), F32)],
        scratch=[pltpu.VMEM((2 * SUBLANES, LANES), F32), _chunked(T)], sem=("arbitrary",))(
            proj, proj, *dqs, *dks, *dvs, pos, freq, wq, wk)


N_SMALL_ODD = 40
SHIFT_ROWS_LESS = SUBLANES


def _fill_shifted(ext_ref, sh_ref):
    rows = ext_ref.shape[0] - SHIFT_ROWS_LESS
    for b in range(1, SUBLANES):
        sh_ref[b - 1] = ext_ref[b:b + rows, :]


def _window(ext_ref, sh_ref, off, T):
    a, b = divmod(off, SUBLANES)
    if b == 0:
        return ext_ref[off:off + T, :]
    return sh_ref[b - 1, a * SUBLANES:a * SUBLANES + T, :]


def _odd_pool_tile(i, uc_r, ucp_r, pw_r, ext_u, pooled_s, pm_s, T):
    H = HALO
    uc = uc_r[...]
    ext_u[0:H, :] = jnp.where(i == 0, 0.0, ucp_r[...])
    ext_u[H:H + T, :] = uc
    row = i * T + lax.broadcasted_iota(jnp.int32, (T, 1), 0)
    for g, p in enumerate(POOL_SIZES):
        cs = slice(g * LANES, (g + 1) * LANES)
        win = ext_u[H:H + T, cs]
        for j in range(1, p):
            win = win + ext_u[H - j:H - j + T, cs]
        cnt = jnp.minimum(row + 1, p).astype(F32)
        pooled = win / cnt - uc[:, cs]
        pooled_s[:, cs] = pooled
        pm_s[:, cs] = jnp.dot(pooled.astype(BF16), pw_r[g].astype(BF16), preferred_element_type=F32)
    return row


def _odd_glu_tile(i, da_r, dg_r, dap_r, dgp_r, ext_g, sh_g, T):
    H = HALO
    ext_g[0:H, :] = jnp.where(i == 0, 0.0, dap_r[...] * _sig(dgp_r[...]))
    ext_g[H:H + T, :] = da_r[...] * _sig(dg_r[...])
    _fill_shifted(ext_g, sh_g)


def _odd_specs(T, S, order):
    tb = T // HALO
    col = lambda c: pl.BlockSpec((T, A_WIDTH), lambda s: (order(s), c))
    prev = lambda c: pl.BlockSpec((HALO, A_WIDTH), lambda s: (jnp.maximum(order(s) * tb - 1, 0), c))
    const2 = lambda shape: pl.BlockSpec(shape, lambda s: (0, 0))
    weights = [pl.BlockSpec((4, LANES, LANES), lambda s: (0, 0, 0)), const2((1, A_WIDTH)),
               const2((D_CONV, A_WIDTH)), const2((1, A_WIDTH)), const2((1, A_WIDTH)), const2((1, A_WIDTH))]
    return col, prev, weights


def _odd_mixer_fwd(proj, pool_w, scale, dconv_w, dconv_b, ln_w, ln_b, T):
    S = proj.shape[0]
    col, prev, wspecs = _odd_specs(T, S, lambda s: s)
    H = HALO

    def body(uc_r, da_r, dg_r, zl_r, zh_r, ucp_r, dap_r, dgp_r, pw_r, sc_r, dw_r, db_r, lw_r, lb_r,
             u_ref, ut_ref, cv_ref, ext_u, ext_g, sh_g, pooled_s, pm_s):
        i = pl.program_id(0)
        _odd_pool_tile(i, uc_r, ucp_r, pw_r, ext_u, pooled_s, pm_s, T)
        _odd_glu_tile(i, da_r, dg_r, dap_r, dgp_r, ext_g, sh_g, T)
        base = H - (D_CONV - 1)
        conv = db_r[...] + dw_r[0:1, :] * _window(ext_g, sh_g, base, T)
        for kk in range(1, D_CONV):
            conv = conv + dw_r[kk:kk + 1, :] * _window(ext_g, sh_g, base + kk, T)
        cv_ref[...] = conv
        mu = jnp.mean(conv, axis=-1, keepdims=True)
        xc = conv - mu
        yh = xc * lax.rsqrt(jnp.mean(xc * xc, axis=-1, keepdims=True) + EPS)
        ln = yh * lw_r[...] + lb_r[...]
        zl, zh = zl_r[...], zh_r[...]
        u_ref[:, 0:A_WIDTH] = pm_s[...] * sc_r[...] * (zl * _sig(zl))
        u_ref[:, A_WIDTH:] = ln * _sig(ln) * (zh * _sig(zh))
        ut_ref[...] = u_ref[...].T.astype(BF16)

    return _call(
        body, "odd_mixer_fwd", (S // T,),
        [col(0), col(1), col(2), col(3), col(4), prev(0), prev(1), prev(2)] + wspecs,
        [pl.BlockSpec((T, D_MODEL), lambda i: (i, 0)), pl.BlockSpec((D_MODEL, T), lambda i: (0, i)),
         pl.BlockSpec((T, A_WIDTH), lambda i: (i, 0))],
        [jax.ShapeDtypeStruct((S, D_MODEL), F32), jax.ShapeDtypeStruct((D_MODEL, S), BF16),
         jax.ShapeDtypeStruct((S, A_WIDTH), F32)],
        scratch=[pltpu.VMEM((T + H, A_WIDTH), F32), pltpu.VMEM((T + H, A_WIDTH), F32),
                 pltpu.VMEM((SUBLANES - 1, T + H - SHIFT_ROWS_LESS, A_WIDTH), F32),
                 pltpu.VMEM((T, A_WIDTH), F32), pltpu.VMEM((T, A_WIDTH), F32)],
        sem=("parallel",))(proj, proj, proj, proj, proj, proj, proj, proj,
                           pool_w, scale, dconv_w, dconv_b, ln_w, ln_b)


def _odd_mixer_bwd(du, proj, conv, pool_w, scale, dconv_w, dconv_b, ln_w, ln_b, T):
    S = proj.shape[0]
    nt = S // T
    order = lambda s: nt - 1 - s
    col, prev, wspecs = _odd_specs(T, S, order)
    H = HALO

    def body(dul_r, duh_r, cv_r, uc_r, da_r, dg_r, zl_r, zh_r, ucp_r, dap_r, dgp_r, pw_r, sc_r, dw_r, db_r, lw_r, lb_r,
             dp_ref, dpw_ref, sm_ref, ext_u, ext_g, sh_g, pooled_s, pm_s, dpl_s, ext_p, ext_c, sh_c, acc):
        step = pl.program_id(0)
        i = nt - 1 - step

        @pl.when(step == 0)
        def _():
            ext_p[T:T + H, :] = jnp.zeros((H, A_WIDTH), F32)
            ext_c[T:T + H, :] = jnp.zeros((H, A_WIDTH), F32)
            acc[...] = jnp.zeros_like(acc)
            dpw_ref[...] = jnp.zeros_like(dpw_ref)

        def accum(r, v):
            acc[r * SUBLANES:(r + 1) * SUBLANES, :] += _cs8(v)

        row = _odd_pool_tile(i, uc_r, ucp_r, pw_r, ext_u, pooled_s, pm_s, T)
        _odd_glu_tile(i, da_r, dg_r, dap_r, dgp_r, ext_g, sh_g, T)
        conv = cv_r[...]
        mu = jnp.mean(conv, axis=-1, keepdims=True)
        xc = conv - mu
        rstd = lax.rsqrt(jnp.mean(xc * xc, axis=-1, keepdims=True) + EPS)
        yh = xc * rstd
        ln = yh * lw_r[...] + lb_r[...]
        sln = _sig(ln)
        zl, zh = zl_r[...], zh_r[...]
        sl, sh = _sig(zl), _sig(zh)
        dul, duh = dul_r[...], duh_r[...]
        pm = pm_s[...]
        scv = sc_r[...]
        dyc = dul * (zl * sl)
        accum(34, dyc * pm)
        dpm = dyc * scv
        for g in range(len(POOL_SIZES)):
            cs = slice(g * LANES, (g + 1) * LANES)
            dpm_g = dpm[:, cs].astype(BF16)
            dpw_ref[g] += lax.dot_general(pooled_s[:, cs].astype(BF16), dpm_g, TN_DIMS, preferred_element_type=F32)
            dpl_s[:, cs] = lax.dot_general(dpm_g, pw_r[g].astype(BF16), NT_DIMS, preferred_element_type=F32)
        lane_p = lax.broadcasted_iota(jnp.int32, (1, A_WIDTH), 1) // LANES
        pvec = jnp.left_shift(2, lane_p)
        cnt = jnp.minimum(row + 1, pvec).astype(F32)
        dpl = dpl_s[...]
        ext_p[0:T, :] = dpl / cnt
        for g, p in enumerate(POOL_SIZES):
            cs = slice(g * LANES, (g + 1) * LANES)
            win = ext_p[0:T, cs]
            for j in range(1, p):
                win = win + ext_p[j:j + T, cs]
            dp_ref[:, cs] = win - dpl[:, cs]
        ext_p[T:T + H, :] = ext_p[0:H, :]
        dln = duh * (zh * sh) * _dsilu(ln, sln)
        accum(32, dln * yh)
        accum(33, dln)
        dyh = dln * lw_r[...]
        dc = rstd * (dyh - jnp.mean(dyh, axis=-1, keepdims=True) - yh * jnp.mean(dyh * yh, axis=-1, keepdims=True))
        accum(31, dc)
        ext_c[0:T, :] = dc
        _fill_shifted(ext_c, sh_c)
        base = H - (D_CONV - 1)
        dgl = dw_r[0:1, :] * _window(ext_c, sh_c, D_CONV - 1, T)
        accum(0, dc * _window(ext_g, sh_g, base, T))
        for kk in range(1, D_CONV):
            dgl = dgl + dw_r[kk:kk + 1, :] * _window(ext_c, sh_c, D_CONV - 1 - kk, T)
            accum(kk, dc * _window(ext_g, sh_g, base + kk, T))
        ext_c[T:T + H, :] = ext_c[0:H, :]
        dav, dgv = da_r[...], dg_r[...]
        sg = _sig(dgv)
        dp_ref[:, A_WIDTH:2 * A_WIDTH] = dgl * sg
        dp_ref[:, 2 * A_WIDTH:3 * A_WIDTH] = dgl * dav * sg * (1.0 - sg)
        dp_ref[:, 3 * A_WIDTH:4 * A_WIDTH] = dul * (pm * scv) * _dsilu(zl, sl)
        dp_ref[:, 4 * A_WIDTH:5 * A_WIDTH] = duh * (ln * sln) * _dsilu(zh, sh)

        @pl.when(step == nt - 1)
        def _():
            for r in range(N_SMALL_ODD):
                sm_ref[r:r + 1, :] = jnp.sum(acc[r * SUBLANES:(r + 1) * SUBLANES, :], axis=0, keepdims=True)

    ext = pltpu.VMEM((T + H, A_WIDTH), F32)
    shifted = pltpu.VMEM((SUBLANES - 1, T + H - SHIFT_ROWS_LESS, A_WIDTH), F32)
    tile = pltpu.VMEM((T, A_WIDTH), F32)
    return _call(
        body, "odd_mixer_bwd", (nt,),
        [pl.BlockSpec((T, A_WIDTH), lambda s: (order(s), 0)), pl.BlockSpec((T, A_WIDTH), lambda s: (order(s), 1)),
         pl.BlockSpec((T, A_WIDTH), lambda s: (order(s), 0)),
         col(0), col(1), col(2), col(3), col(4), prev(0), prev(1), prev(2)] + wspecs,
        [pl.BlockSpec((T, ODD_IN), lambda s: (order(s), 0)),
         pl.BlockSpec((4, LANES, LANES), lambda s: (0, 0, 0)),
         pl.BlockSpec((N_SMALL_ODD, A_WIDTH), lambda s: (0, 0))],
        [jax.ShapeDtypeStruct((S, ODD_IN), F32), jax.ShapeDtypeStruct((4, LANES, LANES), F32),
         jax.ShapeDtypeStruct((N_SMALL_ODD, A_WIDTH), F32)],
        scratch=[ext, ext, shifted, tile, tile, tile, ext, ext, shifted,
                 pltpu.VMEM((N_SMALL_ODD * SUBLANES, A_WIDTH), F32)],
        sem=("arbitrary",))(du, du, conv, proj, proj, proj, proj, proj, proj, proj, proj,
                            pool_w, scale, dconv_w, dconv_b, ln_w, ln_b)


TILE_SEQ = 256
TILE_M = 512


def _local_step(x, pos, tgt, wb, p):
    T = TILE_SEQ
    freq = _freq_table()
    wq = jnp.tile(p["e_q_norm_w"], (1, LANES // HEAD_DIM))
    wk = jnp.tile(p["e_k_norm_w"], (1, LANES // HEAD_DIM))

    proj_e, ht_e = _inproj(x, p["e_norm_w"], wb["e_w_in"], TILE_M, 1792, "inproj_even")
    qkv = _qkv_prep(proj_e, pos, freq, wq, wk, T)
    qs, ks, vs = qkv[0:3], qkv[3:6], qkv[6:9]
    os_, ls_ = [], []
    for g in range(3):
        o, l = _attn_fwd(qs[g], ks[g], vs[g], g)
        os_.append(o)
        ls_.append(l)
    u_e, ut_e = _even_mixer_fwd(proj_e, os_, ls_, p["e_conv_w"], T)
    x1 = _outproj(x, u_e, wb["e_w_out"], TILE_M, "outproj_even")
    proj_o, ht_o = _inproj(x1, p["o_norm_w"], wb["o_w_in"], TILE_M, 1280, "inproj_odd")
    odd_w = (p["o_pool_w"], p["o_pool_scale"], p["o_dconv_w"], p["o_dconv_b"], p["o_ln_w"], p["o_ln_b"])
    u_o, ut_o, conv_o = _odd_mixer_fwd(proj_o, *odd_w, T)
    dy, lsum = _outproj_loss(x1, u_o, wb["o_w_out"], tgt, TILE_M, "outproj_odd_loss")

    g = {}
    g["o_w_out"] = _mm_wgrad(ut_o, [dy], 512, 512, "wgrad_o_out")
    du_o = _mm_nt(dy, wb["o_w_out"], TILE_M, "du_odd")
    dproj_o, g["o_pool_w"], small_o = _odd_mixer_bwd(du_o, proj_o, conv_o, *odd_w, T)
    g["o_w_in"] = _mm_wgrad(ht_o, [dproj_o], 512, 512, "wgrad_o_in")
    dx1, g["o_norm_w"] = _mm_nt_rms([dproj_o], wb["o_w_in"], x1, p["o_norm_w"], dy, TILE_M, 512, "dx_odd")
    g["o_dconv_w"] = small_o[0:D_CONV]
    g["o_dconv_b"] = small_o[31:32]
    g["o_ln_w"] = small_o[32:33]
    g["o_ln_b"] = small_o[33:34]
    g["o_pool_scale"] = small_o[34:35]

    g["e_w_out"] = _mm_wgrad(ut_e, [dx1], 512, 512, "wgrad_e_out")
    du_e = _mm_nt(dx1, wb["e_w_out"], TILE_M, "du_even")
    dos, cgs, drest, g["e_conv_w"] = _even_mixer_bwd(du_e, proj_e, os_, ls_, p["e_conv_w"], T)
    dqs, dks, dvs = [], [], []
    for gi in range(3):
        dq, dk, dv = _attn_bwd(qs[gi], ks[gi], vs[gi], dos[gi], ls_[gi], cgs[gi], gi)
        dqs.append(dq)
        dks.append(dk)
        dvs.append(dv)
    dqkv, dnw = _qk_bwd(proj_e, dqs, dks, dvs, pos, freq, wq, wk, T)
    g["e_q_norm_w"] = dnw[0:1, 0:HEAD_DIM]
    g["e_k_norm_w"] = dnw[1:2, 0:HEAD_DIM]
    pieces = [dqkv, drest]
    g["e_w_in"] = _mm_wgrad(ht_e, pieces, 512, 512, "wgrad_e_in")
    dx, g["e_norm_w"] = _mm_nt_rms(pieces, wb["e_w_in"], x, p["e_norm_w"], dx1, TILE_M, 512, "dx_even")
    return lsum, dx, g


BIG = ("e_w_in", "e_w_out", "o_w_in", "o_w_out")
SHARD_AXIS = {"e_w_in": 1, "e_w_out": 0, "o_w_in": 1, "o_w_out": 0}
N_CHIPS = 4
ANY = pl.BlockSpec(memory_space=pl.ANY)


def _place():
    x, y, c = lax.axis_index("x"), lax.axis_index("y"), lax.axis_index("c")
    chips = [(1 - x, y), (x, 1 - y), (1 - x, 1 - y)]
    return x, y, c, chips


def _block_of(ref, name, block):
    rows, cols = ref.shape
    if SHARD_AXIS[name] == 1:
        cw = cols // N_CHIPS
        return ref.at[:, pl.ds(pl.multiple_of(block * cw, LANES), cw)]
    rw = rows // N_CHIPS
    return ref.at[pl.ds(pl.multiple_of(block * rw, rw), rw), :]


def _half_of(ref, name, half):
    rows, cols = ref.shape
    if SHARD_AXIS[name] == 1:
        return ref.at[pl.ds(pl.multiple_of(half * (rows // 2), rows // 2), rows // 2), :]
    return ref.at[:, pl.ds(pl.multiple_of(half * (cols // 2), LANES), cols // 2)]


def _sub(ref, name, block, half):
    rows, cols = ref.shape
    if SHARD_AXIS[name] == 1:
        cw, hr = cols // N_CHIPS, rows // 2
        return ref.at[pl.ds(pl.multiple_of(half * hr, hr), hr), pl.ds(pl.multiple_of(block * cw, LANES), cw)]
    rw, hc = rows // N_CHIPS, cols // 2
    return ref.at[pl.ds(pl.multiple_of(block * rw, rw), rw), pl.ds(pl.multiple_of(half * hc, LANES), hc)]


def _gather_weights(shards):
    nw = len(BIG)

    def body(*refs):
        s_refs = dict(zip(BIG, refs[:nw]))
        f_refs = dict(zip(BIG, refs[nw:2 * nw]))
        send, recv = refs[2 * nw:]
        x, y, c, chips = _place()
        me = 2 * x + y
        sib = (x, y, 1 - c)

        def rc(k, src, dst, to):
            return pltpu.make_async_remote_copy(src_ref=src, dst_ref=dst, send_sem=send.at[k], recv_sem=recv.at[k],
                                                device_id=to, device_id_type=MESH)

        first, fwd = [], []
        for wi, n in enumerate(BIG):
            own = _block_of(f_refs[n], n, me)
            cp = rc(24 + wi, s_refs[n], own, sib)
            cp.start()
            first.append(cp)
            for j, (cx, cy) in enumerate(chips):
                cp = rc(wi * 3 + j, _half_of(s_refs[n], n, c), _sub(f_refs[n], n, me, c), (cx, cy, c))
                cp.start()
                first.append(cp)
        for wi, n in enumerate(BIG):
            for j, (cx, cy) in enumerate(chips):
                part = _sub(f_refs[n], n, 2 * cx + cy, c)
                rc(wi * 3 + j, part, part, sib).wait_recv()
                cp = rc(12 + wi * 3 + j, part, part, sib)
                cp.start()
                fwd.append(cp)
        for wi, n in enumerate(BIG):
            own = _block_of(f_refs[n], n, me)
            rc(24 + wi, s_refs[n], own, sib).wait_recv()
            for j, (cx, cy) in enumerate(chips):
                part = _sub(f_refs[n], n, 2 * cx + cy, 1 - c)
                rc(12 + wi * 3 + j, part, part, sib).wait_recv()
        for cp in first + fwd:
            cp.wait_send()

    outs = []
    for n, s in zip(BIG, shards):
        r, cdim = s.shape
        outs.append(jax.ShapeDtypeStruct((r, cdim * N_CHIPS) if SHARD_AXIS[n] == 1 else (r * N_CHIPS, cdim), s.dtype))
    return pl.pallas_call(
        body, name="gather_weights", in_specs=[ANY] * nw, out_specs=[ANY] * nw, out_shape=outs,
        scratch_shapes=[pltpu.SemaphoreType.DMA((28,)), pltpu.SemaphoreType.DMA((28,))],
    )(*shards)


def _allreduce_small(part, name):
    R = part.shape[0]

    def body(p_ref, o_ref, sbuf, cbuf, send, recv):
        x, y, c, chips = _place()
        me = 2 * x + y
        sib = (x, y, 1 - c)
        sbuf[c] = p_ref[...]
        mine = sbuf.at[c]
        d2d = pltpu.make_async_remote_copy(src_ref=mine, dst_ref=mine, send_sem=send.at[0], recv_sem=recv.at[0],
                                           device_id=sib, device_id_type=MESH)
        d2d.start()
        theirs = sbuf.at[1 - c]
        pltpu.make_async_remote_copy(src_ref=theirs, dst_ref=theirs, send_sem=send.at[0], recv_sem=recv.at[0],
                                     device_id=sib, device_id_type=MESH).wait_recv()
        cbuf[me] = sbuf[0] + sbuf[1]
        blk = cbuf.at[me]
        sends = [d2d]
        for j, (cx, cy) in enumerate(chips):
            cp = pltpu.make_async_remote_copy(src_ref=blk, dst_ref=blk, send_sem=send.at[1 + j], recv_sem=recv.at[1 + j],
                                              device_id=(cx, cy, c), device_id_type=MESH)
            cp.start()
            sends.append(cp)
        for j, (cx, cy) in enumerate(chips):
            got = cbuf.at[2 * cx + cy]
            pltpu.make_async_remote_copy(src_ref=got, dst_ref=got, send_sem=send.at[1 + j], recv_sem=recv.at[1 + j],
                                         device_id=(cx, cy, c), device_id_type=MESH).wait_recv()
        o_ref[...] = (cbuf[0] + cbuf[1]) + (cbuf[2] + cbuf[3])
        for cp in sends:
            cp.wait_send()

    vm = pl.BlockSpec(memory_space=pltpu.VMEM)
    return pl.pallas_call(
        body, name=name, in_specs=[vm], out_specs=vm,
        out_shape=jax.ShapeDtypeStruct(part.shape, F32),
        scratch_shapes=[pltpu.VMEM((2, R, LANES), F32), pltpu.VMEM((N_CHIPS, R, LANES), F32),
                        pltpu.SemaphoreType.DMA((4,)), pltpu.SemaphoreType.DMA((4,))],
    )(part)


def _half_shape(shape, name):
    r, cdim = shape
    return (r // 2, cdim) if SHARD_AXIS[name] == 1 else (r, cdim // 2)


def _shard_shape(shape, name):
    r, cdim = shape
    return (r, cdim // N_CHIPS) if SHARD_AXIS[name] == 1 else (r // N_CHIPS, cdim)


def _swap_to_sibling(srcs, name, pick_half):
    nw = len(BIG)

    def body(*refs):
        g_refs = dict(zip(BIG, refs[:nw]))
        r_refs = dict(zip(BIG, refs[nw:2 * nw]))
        send, recv = refs[2 * nw:]
        x, y, c, _ = _place()
        sib = (x, y, 1 - c)
        cps = []
        for wi, n in enumerate(BIG):
            src = _half_of(g_refs[n], n, 1 - c) if pick_half else g_refs[n]
            cp = pltpu.make_async_remote_copy(src_ref=src, dst_ref=r_refs[n], send_sem=send.at[wi],
                                              recv_sem=recv.at[wi], device_id=sib, device_id_type=MESH)
            cp.start()
            cps.append(cp)
        for cp in cps:
            cp.wait()

    outs = [jax.ShapeDtypeStruct(_half_shape(g.shape, n) if pick_half else g.shape, g.dtype)
            for n, g in zip(BIG, srcs)]
    return pl.pallas_call(
        body, name=name, in_specs=[ANY] * nw, out_specs=[ANY] * nw, out_shape=outs,
        scratch_shapes=[pltpu.SemaphoreType.DMA((nw,)), pltpu.SemaphoreType.DMA((nw,))],
    )(*srcs)


def _add_half(cidx, g, r, name):
    rows, cols = r.shape
    tr = 256
    tc = cols if cols <= 1792 else (1792 if cols % 1792 == 0 else 1280)
    nr, nc = rows // tr, cols // tc

    def body(c_ref, g_ref, r_ref, o_ref, ob_ref):
        s = g_ref[...] + r_ref[...]
        o_ref[...] = s
        ob_ref[...] = s.astype(BF16)

    if SHARD_AXIS[name] == 1:
        gmap = lambda i, j, c_ref: (c_ref[0] * nr + i, j)
    else:
        gmap = lambda i, j, c_ref: (i, c_ref[0] * nc + j)
    same = lambda i, j, c_ref: (i, j)
    return pl.pallas_call(
        body, name="add_half_" + name,
        grid_spec=pltpu.PrefetchScalarGridSpec(
            num_scalar_prefetch=1, grid=(nr, nc),
            in_specs=[pl.BlockSpec((tr, tc), gmap), pl.BlockSpec((tr, tc), same)],
            out_specs=[pl.BlockSpec((tr, tc), same), pl.BlockSpec((tr, tc), same)]),
        out_shape=[jax.ShapeDtypeStruct(r.shape, F32), jax.ShapeDtypeStruct(r.shape, BF16)],
        compiler_params=pltpu.CompilerParams(dimension_semantics=("parallel", "parallel"), vmem_limit_bytes=VMEM_LIMIT),
    )(cidx, g, r)


def _scatter_blocks(halves):
    nw = len(BIG)

    def body(*refs):
        h_refs = dict(zip(BIG, refs[:nw]))
        r_refs = dict(zip(BIG, refs[nw:2 * nw]))
        send, recv = refs[2 * nw:]
        x, y, c, chips = _place()
        cps = []
        for wi, n in enumerate(BIG):
            for j, (cx, cy) in enumerate(chips):
                cp = pltpu.make_async_remote_copy(
                    src_ref=_block_of(h_refs[n], n, 2 * cx + cy), dst_ref=r_refs[n].at[j],
                    send_sem=send.at[wi * 3 + j], recv_sem=recv.at[wi * 3 + j],
                    device_id=(cx, cy, c), device_id_type=MESH)
                cp.start()
                cps.append(cp)
        for cp in cps:
            cp.wait()

    outs = [jax.ShapeDtypeStruct((3,) + _shard_shape(h.shape, n), h.dtype) for n, h in zip(BIG, halves)]
    return pl.pallas_call(
        body, name="scatter_blocks", in_specs=[ANY] * nw, out_specs=[ANY] * nw, out_shape=outs,
        scratch_shapes=[pltpu.SemaphoreType.DMA((3 * nw,)), pltpu.SemaphoreType.DMA((3 * nw,))],
    )(*halves)


def _add_blocks(bidx, h, r, name):
    _, rows, cols = r.shape
    tr = min(rows, 256)
    nr = rows // tr

    def body(b_ref, h_ref, r0, r1, r2, o_ref):
        o_ref[...] = ((h_ref[...] + r0[0].astype(F32)) + r1[0].astype(F32)) + r2[0].astype(F32)

    if SHARD_AXIS[name] == 1:
        hmap = lambda i, b_ref: (i, b_ref[0])
    else:
        hmap = lambda i, b_ref: (b_ref[0] * nr + i, 0)
    rspec = lambda j: pl.BlockSpec((1, tr, cols), lambda i, b_ref, j=j: (j, i, 0))
    return pl.pallas_call(
        body, name="add_blocks_" + name,
        grid_spec=pltpu.PrefetchScalarGridSpec(
            num_scalar_prefetch=1, grid=(nr,),
            in_specs=[pl.BlockSpec((tr, cols), hmap), rspec(0), rspec(1), rspec(2)],
            out_specs=pl.BlockSpec((tr, cols), lambda i, b_ref: (i, 0))),
        out_shape=jax.ShapeDtypeStruct((rows, cols), F32),
        compiler_params=pltpu.CompilerParams(dimension_semantics=("parallel",), vmem_limit_bytes=VMEM_LIMIT),
    )(bidx, h, r, r, r)


def _adam_math(w, g, m, v):
    c1 = 1.0 - ADAM_B1 ** ADAM_STEP
    c2 = 1.0 - ADAM_B2 ** ADAM_STEP
    nm = ADAM_B1 * m + (1.0 - ADAM_B1) * g
    nv = ADAM_B2 * v + (1.0 - ADAM_B2) * (g * g)
    delta = -ADAM_LR * ((nm / c1) / (jnp.sqrt(nv / c2) + ADAM_EPS) + ADAM_WD * w)
    return delta, nm, nv


def _adamw(w, g, m, v, name):
    def body(w_ref, g_ref, m_ref, v_ref, d_ref, nm_ref, nv_ref):
        d_ref[...], nm_ref[...], nv_ref[...] = _adam_math(w_ref[...], g_ref[...], m_ref[...], v_ref[...])

    spec = pl.BlockSpec(w.shape, lambda i: (0, 0))
    return _call(body, "adamw_" + name, (1,), [spec] * 4, [spec] * 3,
                 [jax.ShapeDtypeStruct(w.shape, F32)] * 3, sem=("arbitrary",))(w, g, m, v)


def _adamw_halves(cidx, w, mine, theirs, m, v, name):
    rows, cols = w.shape
    hr, hc = mine.shape
    tr = 128
    if SHARD_AXIS[name] == 1:
        ni = hr // tr
        wmap = lambda hh, i, c_ref: (hh * ni + i, 0)
    else:
        ni = hr // tr
        wmap = lambda hh, i, c_ref: (i, hh)
    hmap = lambda hh, i, c_ref: (i, 0)

    def body(c_ref, w_ref, a_ref, b_ref, m_ref, v_ref, g_ref, d_ref, nm_ref, nv_ref):
        g = jnp.where(pl.program_id(0) == c_ref[0], a_ref[...], b_ref[...])
        g_ref[...] = g
        d_ref[...], nm_ref[...], nv_ref[...] = _adam_math(w_ref[...], g, m_ref[...], v_ref[...])

    wspec = pl.BlockSpec((tr, hc), wmap)
    hspec = pl.BlockSpec((tr, hc), hmap)
    return pl.pallas_call(
        body, name="adamw_" + name,
        grid_spec=pltpu.PrefetchScalarGridSpec(
            num_scalar_prefetch=1, grid=(2, ni),
            in_specs=[wspec, hspec, hspec, wspec, wspec], out_specs=[wspec] * 4),
        out_shape=[jax.ShapeDtypeStruct(w.shape, F32)] * 4,
        compiler_params=pltpu.CompilerParams(dimension_semantics=("parallel", "parallel"), vmem_limit_bytes=VMEM_LIMIT),
    )(cidx, w, mine, theirs, m, v)


SMALL = ("e_norm_w", "e_q_norm_w", "e_k_norm_w", "e_conv_w", "o_norm_w", "o_pool_w", "o_pool_scale",
         "o_dconv_w", "o_dconv_b", "o_ln_w", "o_ln_b")
SMALL_SHARDED = ("e_conv_w", "o_norm_w", "o_pool_scale", "o_dconv_w", "o_dconv_b", "o_ln_w", "o_ln_b")
WEIGHTS = ("e_norm_w", "e_w_in", "e_q_norm_w", "e_k_norm_w", "e_conv_w", "e_w_out", "o_norm_w", "o_w_in",
           "o_pool_w", "o_pool_scale", "o_dconv_w", "o_dconv_b", "o_ln_w", "o_ln_b", "o_w_out")


def _pack(arrs):
    flat = jnp.concatenate([a.reshape(-1) for a in arrs])
    rows = -(-flat.shape[0] // (LANES * SUBLANES)) * SUBLANES
    flat = jnp.pad(flat, (0, rows * LANES - flat.shape[0]))
    return flat.reshape(rows, LANES)


def _unpack(packed, shapes):
    flat = packed.reshape(-1)
    out, off = [], 0
    for s in shapes:
        n = int(np.prod(s))
        out.append(flat[off:off + n].reshape(s))
        off += n
    return out


def _gather_last(a, block, width):
    return lax.dynamic_slice_in_dim(a, block * width, width, axis=a.ndim - 1)


def kernel(x, positions, e_norm_w, e_w_in, e_q_norm_w, e_k_norm_w, e_conv_w, e_w_out, o_norm_w, o_w_in, o_pool_w, o_pool_scale, o_dconv_w, o_dconv_b, o_ln_w, o_ln_b, o_w_out, loss_target, m_e_norm_w, m_e_w_in, m_e_q_norm_w, m_e_k_norm_w, m_e_conv_w, m_e_w_out, m_o_norm_w, m_o_w_in, m_o_pool_w, m_o_pool_scale, m_o_dconv_w, m_o_dconv_b, m_o_ln_w, m_o_ln_b, m_o_w_out, v_e_norm_w, v_e_w_in, v_e_q_norm_w, v_e_k_norm_w, v_e_conv_w, v_e_w_out, v_o_norm_w, v_o_w_in, v_o_pool_w, v_o_pool_scale, v_o_dconv_w, v_o_dconv_b, v_o_ln_w, v_o_ln_b, v_o_w_out):
    given = dict(e_norm_w=e_norm_w, e_w_in=e_w_in, e_q_norm_w=e_q_norm_w, e_k_norm_w=e_k_norm_w, e_conv_w=e_conv_w,
                 e_w_out=e_w_out, o_norm_w=o_norm_w, o_w_in=o_w_in, o_pool_w=o_pool_w, o_pool_scale=o_pool_scale,
                 o_dconv_w=o_dconv_w, o_dconv_b=o_dconv_b, o_ln_w=o_ln_w, o_ln_b=o_ln_b, o_w_out=o_w_out)
    mom = dict(e_norm_w=m_e_norm_w, e_w_in=m_e_w_in, e_q_norm_w=m_e_q_norm_w, e_k_norm_w=m_e_k_norm_w,
               e_conv_w=m_e_conv_w, e_w_out=m_e_w_out, o_norm_w=m_o_norm_w, o_w_in=m_o_w_in, o_pool_w=m_o_pool_w,
               o_pool_scale=m_o_pool_scale, o_dconv_w=m_o_dconv_w, o_dconv_b=m_o_dconv_b, o_ln_w=m_o_ln_w,
               o_ln_b=m_o_ln_b, o_w_out=m_o_w_out)
    var = dict(e_norm_w=v_e_norm_w, e_w_in=v_e_w_in, e_q_norm_w=v_e_q_norm_w, e_k_norm_w=v_e_k_norm_w,
               e_conv_w=v_e_conv_w, e_w_out=v_e_w_out, o_norm_w=v_o_norm_w, o_w_in=v_o_w_in, o_pool_w=v_o_pool_w,
               o_pool_scale=v_o_pool_scale, o_dconv_w=v_o_dconv_w, o_dconv_b=v_o_dconv_b, o_ln_w=v_o_ln_w,
               o_ln_b=v_o_ln_b, o_w_out=v_o_w_out)
    S = x.shape[1]
    mx, my, mc = lax.axis_index("x"), lax.axis_index("y"), lax.axis_index("c")
    chip = 2 * mx + my
    cidx = jnp.reshape(mc, (1,)).astype(jnp.int32)
    bidx = jnp.reshape(chip, (1,)).astype(jnp.int32)

    full_b = _gather_weights([given[n][0].astype(BF16) for n in BIG])
    wb = dict(zip(BIG, full_b))
    shard_sizes = [int(np.prod(given[n].shape)) for n in SMALL_SHARDED]
    own = _pack([given[n] for n in SMALL_SHARDED])
    rows = own.shape[0]
    slots = jnp.zeros((N_CHIPS, rows, LANES), F32)
    own = jnp.where(mc == 0, own, 0.0)
    slots = lax.dynamic_update_slice(slots, own[None], (chip, 0, 0))
    gathered = _allreduce_small(slots.reshape(N_CHIPS * rows, LANES), "gather_small")
    gathered = gathered.reshape(N_CHIPS, rows * LANES)
    p = {}
    off = 0
    for n, size in zip(SMALL_SHARDED, shard_sizes):
        sh = given[n].shape[1:]
        parts = gathered[:, off:off + size].reshape((N_CHIPS,) + sh)
        fullp = jnp.moveaxis(parts, 0, -2).reshape(sh[:-1] + (N_CHIPS * sh[-1],))
        p[n] = fullp.reshape(-1, fullp.shape[-1])
        off += size
    p["e_norm_w"] = e_norm_w
    p["e_q_norm_w"] = e_q_norm_w
    p["e_k_norm_w"] = e_k_norm_w
    p["o_pool_w"] = o_pool_w[0]

    lsum, dx, g = _local_step(x[0], positions.reshape(S, 1), loss_target[0], wb, p)
    loss = lax.psum(0.5 * jnp.sum(lsum) / float(D_MODEL), ("x", "y", "c"))

    tot = _unpack(_allreduce_small(_pack([g[n] for n in SMALL]), "allreduce_small"), [g[n].shape for n in SMALL])
    gsmall = dict(zip(SMALL, tot))
    grads = {}
    for n in SMALL:
        gv = gsmall[n]
        if n in SMALL_SHARDED:
            gv = _gather_last(gv, chip, gv.shape[-1] // N_CHIPS)
        grads[n] = gv.reshape(given[n].shape)

    recv_half = _swap_to_sibling([g[n] for n in BIG], "swap_halves", True)
    halves = [_add_half(cidx, g[n], r, n) for n, r in zip(BIG, recv_half)]
    recv_blk = _scatter_blocks([hb for _, hb in halves])
    parts = [_add_blocks(bidx, h, r, n) for n, (h, _), r in zip(BIG, halves, recv_blk)]
    theirs = _swap_to_sibling(parts, "swap_reduced", False)

    delta, new_m, new_v = {}, {}, {}
    for n, mine, other in zip(BIG, parts, theirs):
        sh = given[n].shape
        gs, d, nm, nv = _adamw_halves(cidx, given[n][0], mine, other, mom[n][0], var[n][0], n)
        grads[n], delta[n], new_m[n], new_v[n] = gs.reshape(sh), d.reshape(sh), nm.reshape(sh), nv.reshape(sh)
    shapes = [given[n].shape for n in SMALL]
    packed = [_pack([src[n] for n in SMALL]) for src in (given, grads, mom, var)]
    for dst, pk in zip((delta, new_m, new_v), _adamw(*packed, "small")):
        for n, a in zip(SMALL, _unpack(pk, shapes)):
            dst[n] = a
    return (loss, dx[None], *[grads[n] for n in WEIGHTS], *[delta[n] for n in WEIGHTS],
            *[new_m[n] for n in WEIGHTS], *[new_v[n] for n in WEIGHTS])
```

```python
import numpy as np
import jax
import jax.numpy as jnp
from jax import lax
from jax.experimental import pallas as pl
from jax.experimental.pallas import tpu as pltpu

F32 = jnp.float32
BF16 = jnp.bfloat16

D_MODEL = 1024
HEAD_DIM = 64
A_WIDTH = 512
A_HEADS = 8
A_GROUPS = ((128, 1), (512, 4), (2048, 16))
QBLK = 128
ROT_DIM = 16
ROPE_THETA = 500000.0
POOL_SIZES = (2, 4, 8, 16)
D_CONV = 31
SC_WIDTH = 3
EVEN_IN = 7168
ODD_IN = 2560
EPS = 1e-6
NEG = -1e30
ADAM_LR, ADAM_B1, ADAM_B2, ADAM_EPS, ADAM_WD, ADAM_STEP = 0.001, 0.9, 0.999, 1e-08, 0.01, 10

LANES = 128
SUBLANES = 8
HALO = 32
VMEM_LIMIT = 52 * 1024 * 1024
MESH = pl.DeviceIdType.MESH
ANY = pl.BlockSpec(memory_space=pl.ANY)

NT_DIMS = (((1,), (1,)), ((), ()))
TN_DIMS = (((0,), (0,)), ((), ()))


def _call(body, name, grid, in_specs, out_specs, out_shape, scratch=(), sem=None, aliases=None):
    return pl.pallas_call(
        body, name=name, grid=grid, in_specs=in_specs, out_specs=out_specs, out_shape=out_shape,
        scratch_shapes=list(scratch), input_output_aliases=aliases or {},
        compiler_params=pltpu.CompilerParams(dimension_semantics=sem, vmem_limit_bytes=VMEM_LIMIT))


def _sig(v):
    return jax.nn.sigmoid(v)


def _dsilu(v, s):
    return s * (1.0 + v * (1.0 - s))


def _emit_u(u_ref, ut_ref, lo, hi):
    for k, v in enumerate((lo, hi)):
        u_ref[:, k * A_WIDTH:(k + 1) * A_WIDTH] = v.astype(BF16)
        ut_ref[k * A_WIDTH:(k + 1) * A_WIDTH, :] = v.T.astype(BF16)


def _cs8(v):
    return v.reshape(v.shape[0] // SUBLANES, SUBLANES, v.shape[1]).sum(axis=0)


def _seg_ones():
    r = lax.broadcasted_iota(jnp.int32, (LANES, LANES), 0) // HEAD_DIM
    c = lax.broadcasted_iota(jnp.int32, (LANES, LANES), 1) // HEAD_DIM
    return (r == c).astype(BF16)


def _segsum(v, ones):
    hi = v.astype(BF16)
    lo = (v - hi.astype(F32)).astype(BF16)
    return (jnp.dot(hi, ones, preferred_element_type=F32) + jnp.dot(lo, ones, preferred_element_type=F32))


def _rope_tables(pos_ref, freq_ref):
    ang = pos_ref[...].astype(F32) * freq_ref[...]
    cosv, sinv = jnp.cos(ang), jnp.sin(ang)
    lm = lax.broadcasted_iota(jnp.int32, ang.shape, 1) % HEAD_DIM
    half = ROT_DIM // 2
    c = jnp.where(lm < ROT_DIM, cosv, 1.0)
    s1 = jnp.where((lm >= half) & (lm < ROT_DIM), sinv, 0.0)
    s2 = jnp.where(lm < half, -sinv, 0.0)
    return c, s1, s2


def _freq_table():
    half = ROT_DIM // 2
    inv = ROPE_THETA ** (-np.arange(half, dtype=np.float64) / half)
    lane = np.arange(LANES) % HEAD_DIM
    f = np.where(lane < ROT_DIM, inv[lane % half], 0.0)
    return jnp.asarray(f.reshape(1, LANES), F32)


def _load_once(hbm_ref, vmem_ref, sem):
    @pl.when(pl.program_id(0) == 0)
    def _():
        cp = pltpu.make_async_copy(hbm_ref, vmem_ref, sem)
        cp.start()
        cp.wait()


def _inproj(x, nw, w, tm, tn, name):
    S, N = x.shape[0], w.shape[1]

    def body(x_ref, nw_ref, w_hbm, o_ref, ht_ref, w_v, sem):
        _load_once(w_hbm, w_v, sem)
        xv = x_ref[...]
        ms = jnp.mean(xv * xv, axis=-1, keepdims=True)
        h = xv * lax.rsqrt(ms + EPS) * nw_ref[...]
        ht_ref[...] = h.T.astype(BF16)
        hb = h.astype(BF16)
        for j in range(N // tn):
            o_ref[:, j * tn:(j + 1) * tn] = jnp.dot(hb, w_v[:, j * tn:(j + 1) * tn], preferred_element_type=F32)

    return _call(
        body, name, (S // tm,),
        [pl.BlockSpec((tm, D_MODEL), lambda i: (i, 0)),
         pl.BlockSpec((1, D_MODEL), lambda i: (0, 0)), ANY],
        [pl.BlockSpec((tm, N), lambda i: (i, 0)),
         pl.BlockSpec((D_MODEL, tm), lambda i: (0, i))],
        [jax.ShapeDtypeStruct((S, N), F32), jax.ShapeDtypeStruct((D_MODEL, S), BF16)],
        scratch=[pltpu.VMEM(w.shape, BF16), pltpu.SemaphoreType.DMA(())], sem=("arbitrary",))(x, nw, w)


def _outproj(x, u, w, tm, name):
    S = x.shape[0]

    def body(x_ref, u_ref, w_ref, o_ref):
        o_ref[...] = x_ref[...] + jnp.dot(u_ref[...].astype(BF16), w_ref[...], preferred_element_type=F32)

    return _call(
        body, name, (S // tm,),
        [pl.BlockSpec((tm, D_MODEL), lambda i: (i, 0)),
         pl.BlockSpec((tm, D_MODEL), lambda i: (i, 0)),
         pl.BlockSpec((D_MODEL, D_MODEL), lambda i: (0, 0))],
        pl.BlockSpec((tm, D_MODEL), lambda i: (i, 0)),
        jax.ShapeDtypeStruct((S, D_MODEL), F32), sem=("parallel",))(x, u, w)


def _outproj_loss(x, u, w, tgt, tm, name):
    S = x.shape[0]

    def body(x_ref, u_ref, w_ref, t_ref, dy_ref, l_ref, acc):
        i = pl.program_id(0)

        @pl.when(i == 0)
        def _():
            acc[...] = jnp.zeros_like(acc)
        y = x_ref[...] + jnp.dot(u_ref[...].astype(BF16), w_ref[...], preferred_element_type=F32)
        diff = y - t_ref[...]
        dy_ref[...] = diff / float(D_MODEL)
        acc[...] += _cs8(diff * diff)

        @pl.when(i == pl.num_programs(0) - 1)
        def _():
            l_ref[...] = jnp.sum(acc[...], axis=0, keepdims=True)

    return _call(
        body, name, (S // tm,),
        [pl.BlockSpec((tm, D_MODEL), lambda i: (i, 0)),
         pl.BlockSpec((tm, D_MODEL), lambda i: (i, 0)),
         pl.BlockSpec((D_MODEL, D_MODEL), lambda i: (0, 0)),
         pl.BlockSpec((tm, D_MODEL), lambda i: (i, 0))],
        [pl.BlockSpec((tm, D_MODEL), lambda i: (i, 0)),
         pl.BlockSpec((1, D_MODEL), lambda i: (0, 0))],
        [jax.ShapeDtypeStruct((S, D_MODEL), F32), jax.ShapeDtypeStruct((1, D_MODEL), F32)],
        scratch=[pltpu.VMEM((SUBLANES, D_MODEL), F32)], sem=("arbitrary",))(x, u, w, tgt)


def _mm_nt(a, w, tm, name):
    S, K = a.shape
    N = w.shape[0]

    def body(a_ref, w_ref, o_ref):
        o_ref[...] = lax.dot_general(a_ref[...].astype(BF16), w_ref[...], NT_DIMS, preferred_element_type=F32)

    return _call(
        body, name, (S // tm,),
        [pl.BlockSpec((tm, K), lambda i: (i, 0)), pl.BlockSpec((N, K), lambda i: (0, 0))],
        pl.BlockSpec((tm, N), lambda i: (i, 0)),
        jax.ShapeDtypeStruct((S, N), F32), sem=("parallel",))(a, w)


def _piece_blocks(pieces, tk, axis):
    starts, counts, s = [], [], 0
    for p in pieces:
        n = p.shape[axis] // tk
        starts.append(s)
        counts.append(n)
        s += n
    return starts, counts, s


def _mm_nt_rms(pieces, w, x, nw, dres, tm, name):
    S = x.shape[0]
    npc = len(pieces)
    ni = S // tm
    offs = np.cumsum([0] + [p.shape[1] for p in pieces]).tolist()

    def body(*refs):
        p_refs = refs[:npc]
        w_hbm, x_ref, nw_ref, dr_ref, dx_ref, dnw_ref, w_v, sem, nacc = refs[npc:]
        i = pl.program_id(0)
        _load_once(w_hbm, w_v, sem)

        @pl.when(i == 0)
        def _():
            nacc[...] = jnp.zeros_like(nacc)

        dh = None
        for p in range(npc):
            part = lax.dot_general(p_refs[p][...].astype(BF16), w_v[:, offs[p]:offs[p + 1]], NT_DIMS,
                                   preferred_element_type=F32)
            dh = part if dh is None else dh + part
        xv = x_ref[...]
        rs = lax.rsqrt(jnp.mean(xv * xv, axis=-1, keepdims=True) + EPS)
        xh = xv * rs
        nacc[...] += _cs8(dh * xh)
        dxh = dh * nw_ref[...]
        dx_ref[...] = dr_ref[...] + rs * (dxh - xh * jnp.mean(dxh * xh, axis=-1, keepdims=True))

        @pl.when(i == ni - 1)
        def _():
            dnw_ref[...] = jnp.sum(nacc[...], axis=0, keepdims=True)

    row = pl.BlockSpec((tm, D_MODEL), lambda i: (i, 0))
    return _call(
        body, name, (ni,),
        [pl.BlockSpec((tm, p.shape[1]), lambda i: (i, 0)) for p in pieces] +
        [ANY, row, pl.BlockSpec((1, D_MODEL), lambda i: (0, 0)), row],
        [row, pl.BlockSpec((1, D_MODEL), lambda i: (0, 0))],
        [jax.ShapeDtypeStruct((S, D_MODEL), F32), jax.ShapeDtypeStruct((1, D_MODEL), F32)],
        scratch=[pltpu.VMEM(w.shape, BF16), pltpu.SemaphoreType.DMA(()), pltpu.VMEM((SUBLANES, D_MODEL), F32)],
        sem=("arbitrary",))(*pieces, w, x, nw, dres)


def _mm_wgrad(at, pieces, tn, name):
    M, S = at.shape
    starts, counts, nj = _piece_blocks(pieces, tn, 1)
    npc = len(pieces)

    def body(*refs):
        a_hbm = refs[0]
        p_refs = refs[1:1 + npc]
        o_ref, a_v, sem = refs[1 + npc:]
        j = pl.program_id(0)
        _load_once(a_hbm, a_v, sem)
        for p in range(npc):
            @pl.when((j >= starts[p]) & (j < starts[p] + counts[p]))
            def _(p=p):
                o_ref[...] = jnp.dot(a_v[...], p_refs[p][...].astype(BF16), preferred_element_type=F32)

    def pspec(p):
        return pl.BlockSpec((S, tn), lambda j: (0, jnp.clip(j - starts[p], 0, counts[p] - 1)))

    return _call(
        body, name, (nj,),
        [ANY] + [pspec(p) for p in range(npc)],
        pl.BlockSpec((M, tn), lambda j: (0, j)),
        jax.ShapeDtypeStruct((M, nj * tn), F32),
        scratch=[pltpu.VMEM(at.shape, BF16), pltpu.SemaphoreType.DMA(())], sem=("arbitrary",))(at, *pieces)


def _stream_spec(d, T):
    return pl.BlockSpec((d, T // d, A_WIDTH), lambda i: (0, i, 0))


def _stream_shape(d, S, dtype):
    return jax.ShapeDtypeStruct((d, S // d, A_WIDTH), dtype)


N_CHUNK = A_WIDTH // LANES


def _to_tokens(ref, scr, d, T):
    if d == 1:
        return ref[0].astype(F32)
    for r in range(d):
        for ch in range(N_CHUNK):
            scr.at[ch][pl.ds(r, T // d, stride=d), :] = ref[r, :, ch * LANES:(ch + 1) * LANES].astype(F32)
    return _get(scr)


def _from_tokens(out_ref, scr, d, T):
    for r in range(d):
        for ch in range(N_CHUNK):
            out_ref[r, :, ch * LANES:(ch + 1) * LANES] = scr.at[ch][pl.ds(r, T // d, stride=d), :].astype(out_ref.dtype)


def _put(scr, val):
    for ch in range(N_CHUNK):
        scr[ch] = val[:, ch * LANES:(ch + 1) * LANES]


def _get(scr):
    return jnp.concatenate([scr[ch] for ch in range(N_CHUNK)], axis=1)


def _chunked(T):
    return pltpu.VMEM((N_CHUNK, T, LANES), F32)


def _qkv_prep(proj, pos, freq, wq, wk, T):
    S = proj.shape[0]
    qk_w = 3 * A_WIDTH

    def body(q_ref, k_ref, v_ref, pos_ref, f_ref, wq_ref, wk_ref, *rest):
        outs, scr = rest[:9], rest[9]
        ones = _seg_ones()
        c, s1, s2 = _rope_tables(pos_ref, f_ref)
        for t, (src, w_ref) in enumerate(((q_ref, wq_ref), (k_ref, wk_ref), (v_ref, None))):
            for g in range(3):
                d = A_GROUPS[g][1]
                out = outs[3 * t + g]
                for ch in range(A_WIDTH // LANES):
                    cs = slice(ch * LANES, (ch + 1) * LANES)
                    v = src[:, g * A_WIDTH + ch * LANES: g * A_WIDTH + (ch + 1) * LANES]
                    if w_ref is not None:
                        rs = lax.rsqrt(_segsum(v * v, ones) * (1.0 / HEAD_DIM) + EPS)
                        y = v * rs * w_ref[...]
                        v = y * c + pltpu.roll(y, 8, 1) * s1 + pltpu.roll(y, LANES - 8, 1) * s2
                    if d == 1:
                        out[0, :, cs] = v.astype(BF16)
                    else:
                        scr[ch] = v
                if d > 1:
                    _from_tokens(out, scr, d, T)

    ds_ = [A_GROUPS[g][1] for g in range(3)] * 3
    return _call(
        body, "qkv_prep", (S // T,),
        [pl.BlockSpec((T, qk_w), lambda i: (i, 0)), pl.BlockSpec((T, qk_w), lambda i: (i, 1)),
         pl.BlockSpec((T, qk_w), lambda i: (i, 2)),
         pl.BlockSpec((T, 1), lambda i: (i, 0)), pl.BlockSpec((1, LANES), lambda i: (0, 0)),
         pl.BlockSpec((1, LANES), lambda i: (0, 0)), pl.BlockSpec((1, LANES), lambda i: (0, 0))],
        [_stream_spec(d, T) for d in ds_],
        [_stream_shape(d, S, BF16) for d in ds_],
        scratch=[_chunked(T)], sem=("parallel",))(proj, proj, proj, pos, freq, wq, wk)


def _attn_mask(i):
    qi = lax.broadcasted_iota(jnp.int32, (QBLK, 2 * QBLK), 0) + QBLK
    kj = lax.broadcasted_iota(jnp.int32, (QBLK, 2 * QBLK), 1)
    dist = qi - kj
    return (dist >= 0) & (dist <= QBLK) & ((i > 0) | (kj >= QBLK))


ATT_BLK = (None, QBLK, A_WIDTH)


def _attn_fwd(q, k, v, g):
    d, n, _ = q.shape
    nb = n // QBLK

    def body(q_ref, kp_ref, kc_ref, vp_ref, vc_ref, o_ref, l_ref, s_scr, p_scr):
        i = pl.program_id(1)
        mask = _attn_mask(i)
        for h in range(A_HEADS):
            hs = slice(h * HEAD_DIM, (h + 1) * HEAD_DIM)
            kc = jnp.concatenate([kp_ref[:, hs], kc_ref[:, hs]], axis=0)
            s_scr[h] = lax.dot_general(q_ref[:, hs], kc, NT_DIMS, preferred_element_type=F32)
        for h in range(A_HEADS):
            hs = slice(h * HEAD_DIM, (h + 1) * HEAD_DIM)
            s = jnp.where(mask, s_scr[h] * (HEAD_DIM ** -0.5), NEG)
            m = jnp.max(s, axis=-1, keepdims=True)
            p = jnp.exp(s - m)
            den = jnp.sum(p, axis=-1, keepdims=True)
            p_scr[h] = (p / den).astype(BF16)
            l_ref[:, hs] = jnp.broadcast_to(m + jnp.log(den), (QBLK, HEAD_DIM))
        for h in range(A_HEADS):
            hs = slice(h * HEAD_DIM, (h + 1) * HEAD_DIM)
            vc = jnp.concatenate([vp_ref[:, hs], vc_ref[:, hs]], axis=0)
            o_ref[:, hs] = jnp.dot(p_scr[h], vc, preferred_element_type=F32)

    prev = lambda r, i: (r, jnp.maximum(i - 1, 0), 0)
    cur = lambda r, i: (r, i, 0)
    return _call(
        body, "attn_fwd_g%d" % g, (d, nb),
        [pl.BlockSpec(ATT_BLK, cur), pl.BlockSpec(ATT_BLK, prev), pl.BlockSpec(ATT_BLK, cur),
         pl.BlockSpec(ATT_BLK, prev), pl.BlockSpec(ATT_BLK, cur)],
        [pl.BlockSpec(ATT_BLK, cur), pl.BlockSpec(ATT_BLK, cur)],
        [jax.ShapeDtypeStruct((d, n, A_WIDTH), F32)] * 2,
        scratch=[pltpu.VMEM((A_HEADS, QBLK, 2 * QBLK), F32), pltpu.VMEM((A_HEADS, QBLK, 2 * QBLK), BF16)],
        sem=("parallel", "parallel"))(q, k, k, v, v)


def _attn_bwd(q, k, v, do, lse, cg, g):
    d, n, _ = q.shape
    nb = n // QBLK
    scale = HEAD_DIM ** -0.5

    def body(q_ref, kp_ref, kc_ref, vp_ref, vc_ref, do_ref, l_ref, c_ref, dq_ref, dk_ref, dv_ref, ck, cv,
             s_scr, dp_scr, p_scr, ds_scr):
        i = pl.program_id(1)

        @pl.when(i == 0)
        def _():
            ck[...] = jnp.zeros_like(ck)
            cv[...] = jnp.zeros_like(cv)

        @pl.when(i < nb)
        def _():
            mask = _attn_mask(i)
            for h in range(A_HEADS):
                hs = slice(h * HEAD_DIM, (h + 1) * HEAD_DIM)
                kc = jnp.concatenate([kp_ref[:, hs], kc_ref[:, hs]], axis=0)
                vc = jnp.concatenate([vp_ref[:, hs], vc_ref[:, hs]], axis=0)
                s_scr[h] = lax.dot_general(q_ref[:, hs], kc, NT_DIMS, preferred_element_type=F32)
                dp_scr[h] = lax.dot_general(do_ref[:, hs], vc, NT_DIMS, preferred_element_type=F32)
            for h in range(A_HEADS):
                lane = h * HEAD_DIM
                p = jnp.where(mask, jnp.exp(s_scr[h] * scale - l_ref[:, lane:lane + 1]), 0.0)
                p_scr[h] = p.astype(BF16)
                ds_scr[h] = (p * (dp_scr[h] + c_ref[:, lane:lane + 1]) * scale).astype(BF16)
            for h in range(A_HEADS):
                hs = slice(h * HEAD_DIM, (h + 1) * HEAD_DIM)
                kc = jnp.concatenate([kp_ref[:, hs], kc_ref[:, hs]], axis=0)
                ds = ds_scr[h]
                dq_ref[:, hs] = jnp.dot(ds, kc, preferred_element_type=F32)
                dkc = lax.dot_general(ds, q_ref[:, hs], TN_DIMS, preferred_element_type=F32)
                dvc = lax.dot_general(p_scr[h], do_ref[:, hs], TN_DIMS, preferred_element_type=F32)
                dk_ref[:, hs] = ck[:, hs] + dkc[:QBLK]
                dv_ref[:, hs] = cv[:, hs] + dvc[:QBLK]
                ck[:, hs] = dkc[QBLK:]
                cv[:, hs] = dvc[QBLK:]

        @pl.when(i == nb)
        def _():
            dk_ref[...] = ck[...]
            dv_ref[...] = cv[...]

    qi = lambda i: jnp.minimum(i, nb - 1)
    cur = lambda r, i: (r, qi(i), 0)
    prev = lambda r, i: (r, jnp.maximum(qi(i) - 1, 0), 0)
    late = lambda r, i: (r, jnp.maximum(i - 1, 0), 0)
    return _call(
        body, "attn_bwd_g%d" % g, (d, nb + 1),
        [pl.BlockSpec(ATT_BLK, cur), pl.BlockSpec(ATT_BLK, prev), pl.BlockSpec(ATT_BLK, cur),
         pl.BlockSpec(ATT_BLK, prev), pl.BlockSpec(ATT_BLK, cur),
         pl.BlockSpec(ATT_BLK, cur), pl.BlockSpec(ATT_BLK, cur), pl.BlockSpec(ATT_BLK, cur)],
        [pl.BlockSpec(ATT_BLK, cur), pl.BlockSpec(ATT_BLK, late), pl.BlockSpec(ATT_BLK, late)],
        [jax.ShapeDtypeStruct((d, n, A_WIDTH), F32)] * 3,
        scratch=[pltpu.VMEM((QBLK, A_WIDTH), F32), pltpu.VMEM((QBLK, A_WIDTH), F32),
                 pltpu.VMEM((A_HEADS, QBLK, 2 * QBLK), F32), pltpu.VMEM((A_HEADS, QBLK, 2 * QBLK), F32),
                 pltpu.VMEM((A_HEADS, QBLK, 2 * QBLK), BF16), pltpu.VMEM((A_HEADS, QBLK, 2 * QBLK), BF16)],
        sem=("parallel", "arbitrary"))(q, k, k, v, v, do, lse, cg)


def _merge_weights(l0, l1, l2):
    mx = jnp.maximum(jnp.maximum(l0, l1), l2)
    e0, e1, e2 = jnp.exp(l0 - mx), jnp.exp(l1 - mx), jnp.exp(l2 - mx)
    den = e0 + e1 + e2
    return e0 / den, e1 / den, e2 / den


def _even_specs(T, S):
    t8 = T // SUBLANES
    last8 = S // SUBLANES - 1
    col = lambda c: pl.BlockSpec((T, A_WIDTH), lambda i: (i, c))
    prev8 = lambda c: pl.BlockSpec((SUBLANES, A_WIDTH), lambda i: (jnp.maximum(i * t8 - 1, 0), c))
    next8 = lambda c: pl.BlockSpec((SUBLANES, A_WIDTH), lambda i: (jnp.minimum((i + 1) * t8, last8), c))
    return col, prev8, next8


GROUP_D = tuple(d for _, d in A_GROUPS)


def _even_mixer_fwd(proj, os_, ls_, conv_w, T):
    S = proj.shape[0]
    col, prev8, _ = _even_specs(T, S)
    H = SUBLANES

    def body(bg_r, cg_r, hb_r, zl_r, zh_r, cgp_r, hbp_r, o0, o1, o2, l0, l1, l2, cw_r, u_ref, ut_ref, ext, *scr):
        i = pl.program_id(0)
        ls = [_to_tokens(r, scr[g], GROUP_D[g], T) for g, r in enumerate((l0, l1, l2))]
        ws = _merge_weights(*ls)
        oa = ws[0] * _to_tokens(o0, scr[0], GROUP_D[0], T)
        oa = oa + ws[1] * _to_tokens(o1, scr[1], GROUP_D[1], T)
        oa = oa + ws[2] * _to_tokens(o2, scr[2], GROUP_D[2], T)
        ext[0:H, :] = jnp.where(i == 0, 0.0, cgp_r[...] * hbp_r[...])
        ext[H:H + T, :] = cg_r[...] * hb_r[...]
        conv = cw_r[0:1, :] * ext[H - 2:H - 2 + T, :]
        for kk in range(1, SC_WIDTH):
            conv = conv + cw_r[kk:kk + 1, :] * ext[H - 2 + kk:H - 2 + kk + T, :]
        zl, zh = zl_r[...], zh_r[...]
        _emit_u(u_ref, ut_ref, oa * (zl * _sig(zl)), bg_r[...] * conv * (zh * _sig(zh)))

    streams = [_stream_spec(d, T) for d in GROUP_D]
    return _call(
        body, "even_mixer_fwd", (S // T,),
        [col(9), col(10), col(11), col(12), col(13), prev8(10), prev8(11)] + streams + streams +
        [pl.BlockSpec((SC_WIDTH, A_WIDTH), lambda i: (0, 0))],
        [pl.BlockSpec((T, D_MODEL), lambda i: (i, 0)), pl.BlockSpec((D_MODEL, T), lambda i: (0, i))],
        [jax.ShapeDtypeStruct((S, D_MODEL), BF16), jax.ShapeDtypeStruct((D_MODEL, S), BF16)],
        scratch=[pltpu.VMEM((T + H, A_WIDTH), F32)] + [_chunked(T)] * 3, sem=("parallel",))(
            proj, proj, proj, proj, proj, proj, proj, *os_, *ls_, conv_w)


def _even_mixer_bwd(du, proj, os_, ls_, conv_w, T):
    S = proj.shape[0]
    nt = S // T
    col, prev8, next8 = _even_specs(T, S)
    H = SUBLANES
    t8 = T // SUBLANES
    last8 = S // SUBLANES - 1

    def body(dul_r, duh_r, bg_r, cg_r, hb_r, zl_r, zh_r, cgp_r, hbp_r, dun_r, zhn_r, bgn_r,
             o0, o1, o2, l0, l1, l2, cw_r,
             do0, do1, do2, c0, c1, c2, dr_ref, dcw_ref, ext_t, ext_d, acc, s_a, s_b, s_c, *ws_scr):
        i = pl.program_id(0)

        @pl.when(i == 0)
        def _():
            acc[...] = jnp.zeros_like(acc)

        ones = _seg_ones()
        zl, zh = zl_r[...], zh_r[...]
        sl, sh = _sig(zl), _sig(zh)
        dul, duh = dul_r[...], duh_r[...]
        scr = (s_a, s_b, s_c)
        ls = [_to_tokens(r, scr[g], GROUP_D[g], T) for g, r in enumerate((l0, l1, l2))]
        ws = _merge_weights(*ls)
        for g in range(3):
            _put(ws_scr[g], ws[g])
        oa = ws[0] * _to_tokens(o0, scr[0], GROUP_D[0], T)
        oa = oa + ws[1] * _to_tokens(o1, scr[1], GROUP_D[1], T)
        oa = oa + ws[2] * _to_tokens(o2, scr[2], GROUP_D[2], T)
        doa = dul * (zl * sl)
        _put(s_a, doa)
        prod = doa * oa
        for ch in range(N_CHUNK):
            s_b[ch] = _segsum(prod[:, ch * LANES:(ch + 1) * LANES], ones)
        for g, (do_ref, c_ref) in enumerate(((do0, c0), (do1, c1), (do2, c2))):
            d = GROUP_D[g]
            if d == 1:
                for ch in range(N_CHUNK):
                    cs = slice(ch * LANES, (ch + 1) * LANES)
                    do_ref[0, :, cs] = (ws_scr[g][ch] * s_a[ch]).astype(BF16)
                    c_ref[0, :, cs] = -ws_scr[g][ch] * s_b[ch]
            else:
                for ch in range(N_CHUNK):
                    s_c[ch] = ws_scr[g][ch] * s_a[ch]
                _from_tokens(do_ref, s_c, d, T)
                for ch in range(N_CHUNK):
                    s_c[ch] = -ws_scr[g][ch] * s_b[ch]
                _from_tokens(c_ref, s_c, d, T)
        cgv, hbv, bgv = cg_r[...], hb_r[...], bg_r[...]
        ext_t[0:H, :] = jnp.where(i == 0, 0.0, cgp_r[...] * hbp_r[...])
        ext_t[H:H + T, :] = cgv * hbv
        conv = cw_r[0:1, :] * ext_t[H - 2:H - 2 + T, :]
        for kk in range(1, SC_WIDTH):
            conv = conv + cw_r[kk:kk + 1, :] * ext_t[H - 2 + kk:H - 2 + kk + T, :]
        dyb = duh * (zh * sh)
        dconv = dyb * bgv
        zn = zhn_r[...]
        ext_d[0:T, :] = dconv
        ext_d[T:T + H, :] = jnp.where(i == nt - 1, 0.0, dun_r[...] * (zn * _sig(zn)) * bgn_r[...])
        dt = cw_r[0:1, :] * ext_d[2:2 + T, :]
        for kk in range(1, SC_WIDTH):
            dt = dt + cw_r[kk:kk + 1, :] * ext_d[2 - kk:2 - kk + T, :]
        for kk in range(SC_WIDTH):
            acc[kk * SUBLANES:(kk + 1) * SUBLANES, :] += _cs8(dconv * ext_t[H - 2 + kk:H - 2 + kk + T, :])
        dr_ref[:, 0:A_WIDTH] = (dyb * conv).astype(BF16)
        dr_ref[:, A_WIDTH:2 * A_WIDTH] = (dt * hbv).astype(BF16)
        dr_ref[:, 2 * A_WIDTH:3 * A_WIDTH] = (dt * cgv).astype(BF16)
        dr_ref[:, 3 * A_WIDTH:4 * A_WIDTH] = (dul * oa * _dsilu(zl, sl)).astype(BF16)
        dr_ref[:, 4 * A_WIDTH:5 * A_WIDTH] = (duh * (bgv * conv) * _dsilu(zh, sh)).astype(BF16)

        @pl.when(i == nt - 1)
        def _():
            for kk in range(SC_WIDTH):
                dcw_ref[kk:kk + 1, :] = jnp.sum(acc[kk * SUBLANES:(kk + 1) * SUBLANES, :], axis=0, keepdims=True)

    streams = [_stream_spec(d, T) for d in GROUP_D]
    dunext = pl.BlockSpec((SUBLANES, A_WIDTH), lambda i: (jnp.minimum((i + 1) * t8, last8), 1))
    tile = _chunked(T)
    outs = _call(
        body, "even_mixer_bwd", (nt,),
        [pl.BlockSpec((T, A_WIDTH), lambda i: (i, 0)), pl.BlockSpec((T, A_WIDTH), lambda i: (i, 1)),
         col(9), col(10), col(11), col(12), col(13), prev8(10), prev8(11), dunext, next8(13), next8(9)] +
        streams + streams + [pl.BlockSpec((SC_WIDTH, A_WIDTH), lambda i: (0, 0))],
        streams + streams + [pl.BlockSpec((T, 5 * A_WIDTH), lambda i: (i, 0)),
                             pl.BlockSpec((SC_WIDTH, A_WIDTH), lambda i: (0, 0))],
        [_stream_shape(d, S, BF16) for d in GROUP_D] + [_stream_shape(d, S, F32) for d in GROUP_D] +
        [jax.ShapeDtypeStruct((S, 5 * A_WIDTH), BF16), jax.ShapeDtypeStruct((SC_WIDTH, A_WIDTH), F32)],
        scratch=[pltpu.VMEM((T + H, A_WIDTH), F32), pltpu.VMEM((T + H, A_WIDTH), F32),
                 pltpu.VMEM((SC_WIDTH * SUBLANES, A_WIDTH), F32)] + [tile] * 6,
        sem=("arbitrary",))(du, du, proj, proj, proj, proj, proj, proj, proj, du, proj, proj, *os_, *ls_, conv_w)
    return outs[0:3], outs[3:6], outs[6], outs[7]


def _qk_bwd(proj, dqs, dks, dvs, pos, freq, wq, wk, T):
    S = proj.shape[0]
    nt = S // T
    qk_w = 3 * A_WIDTH

    def body(q_ref, k_ref, dq0, dq1, dq2, dk0, dk1, dk2, dv0, dv1, dv2, pos_ref, f_ref, wq_ref, wk_ref,
             o_ref, dw_ref, acc, scr):
        i = pl.program_id(0)

        @pl.when(i == 0)
        def _():
            acc[...] = jnp.zeros_like(acc)
            dw_ref[...] = jnp.zeros_like(dw_ref)

        ones = _seg_ones()
        c, s1, s2 = _rope_tables(pos_ref, f_ref)
        for t, (src, w_ref, ds) in enumerate(((q_ref, wq_ref, (dq0, dq1, dq2)), (k_ref, wk_ref, (dk0, dk1, dk2)))):
            wv = w_ref[...]
            for g in range(3):
                d = GROUP_D[g]
                if d > 1:
                    _to_tokens(ds[g], scr, d, T)
                for ch in range(A_WIDTH // LANES):
                    cs = slice(g * A_WIDTH + ch * LANES, g * A_WIDTH + (ch + 1) * LANES)
                    lc = slice(ch * LANES, (ch + 1) * LANES)
                    v = src[:, cs]
                    dout = ds[g][0, :, lc] if d == 1 else scr[ch]
                    rs = lax.rsqrt(_segsum(v * v, ones) * (1.0 / HEAD_DIM) + EPS)
                    xh = v * rs
                    dy = dout * c + pltpu.roll(dout * s1, LANES - 8, 1) + pltpu.roll(dout * s2, 8, 1)
                    acc[t * SUBLANES:(t + 1) * SUBLANES, :] += _cs8(dy * xh)
                    dxh = dy * wv
                    mean = _segsum(dxh * xh, ones) * (1.0 / HEAD_DIM)
                    o_ref[:, t * qk_w + g * A_WIDTH + ch * LANES: t * qk_w + g * A_WIDTH + (ch + 1) * LANES] = (
                        rs * (dxh - xh * mean)).astype(BF16)
        for g, dv in enumerate((dv0, dv1, dv2)):
            d = GROUP_D[g]
            base = 2 * qk_w + g * A_WIDTH
            o_ref[:, base:base + A_WIDTH] = _to_tokens(dv, scr, d, T).astype(BF16)

        @pl.when(i == nt - 1)
        def _():
            for t in range(2):
                srow = jnp.sum(acc[t * SUBLANES:(t + 1) * SUBLANES, :], axis=0, keepdims=True)
                dw_ref[t:t + 1, :] = srow + pltpu.roll(srow, HEAD_DIM, 1)

    streams = [_stream_spec(d, T) for d in GROUP_D]
    return _call(
        body, "qk_bwd", (nt,),
        [pl.BlockSpec((T, qk_w), lambda i: (i, 0)), pl.BlockSpec((T, qk_w), lambda i: (i, 1))] + streams * 3 +
        [pl.BlockSpec((T, 1), lambda i: (i, 0)), pl.BlockSpec((1, LANES), lambda i: (0, 0)),
         pl.BlockSpec((1, LANES), lambda i: (0, 0)), pl.BlockSpec((1, LANES), lambda i: (0, 0))],
        [pl.BlockSpec((T, 3 * qk_w), lambda i: (i, 0)), pl.BlockSpec((SUBLANES, LANES), lambda i: (0, 0))],
        [jax.ShapeDtypeStruct((S, 3 * qk_w), BF16), jax.ShapeDtypeStruct((SUBLANES, LANES), F32)],
        scratch=[pltpu.VMEM((2 * SUBLANES, LANES), F32), _chunked(T)], sem=("arbitrary",))(
            proj, proj, *dqs, *dks, *dvs, pos, freq, wq, wk)


N_SMALL_ODD = 40
SHIFT_ROWS_LESS = SUBLANES


def _fill_shifted(ext_ref, sh_ref):
    rows = ext_ref.shape[0] - SHIFT_ROWS_LESS
    for b in range(1, SUBLANES):
        sh_ref[b - 1] = ext_ref[b:b + rows, :]


def _window(ext_ref, sh_ref, off, T):
    a, b = divmod(off, SUBLANES)
    if b == 0:
        return ext_ref[off:off + T, :]
    return sh_ref[b - 1, a * SUBLANES:a * SUBLANES + T, :]


def _odd_pool_tile(i, uc_r, ucp_r, pw_r, ext_u, pooled_s, pm_s, T):
    H = HALO
    uc = uc_r[...]
    ext_u[0:H, :] = jnp.where(i == 0, 0.0, ucp_r[...])
    ext_u[H:H + T, :] = uc
    row = i * T + lax.broadcasted_iota(jnp.int32, (T, 1), 0)
    for g, p in enumerate(POOL_SIZES):
        cs = slice(g * LANES, (g + 1) * LANES)
        win = ext_u[H:H + T, cs]
        for j in range(1, p):
            win = win + ext_u[H - j:H - j + T, cs]
        cnt = jnp.minimum(row + 1, p).astype(F32)
        pooled = win / cnt - uc[:, cs]
        pooled_s[:, cs] = pooled
        pm_s[:, cs] = jnp.dot(pooled.astype(BF16), pw_r[g].astype(BF16), preferred_element_type=F32)
    return row


def _odd_glu_tile(i, da_r, dg_r, dap_r, dgp_r, ext_g, sh_g, T):
    H = HALO
    ext_g[0:H, :] = jnp.where(i == 0, 0.0, dap_r[...] * _sig(dgp_r[...]))
    ext_g[H:H + T, :] = da_r[...] * _sig(dg_r[...])
    _fill_shifted(ext_g, sh_g)


def _odd_specs(T, S, order):
    tb = T // HALO
    col = lambda c: pl.BlockSpec((T, A_WIDTH), lambda s: (order(s), c))
    prev = lambda c: pl.BlockSpec((HALO, A_WIDTH), lambda s: (jnp.maximum(order(s) * tb - 1, 0), c))
    const2 = lambda shape: pl.BlockSpec(shape, lambda s: (0, 0))
    weights = [pl.BlockSpec((4, LANES, LANES), lambda s: (0, 0, 0)), const2((1, A_WIDTH)),
               const2((D_CONV, A_WIDTH)), const2((1, A_WIDTH)), const2((1, A_WIDTH)), const2((1, A_WIDTH))]
    return col, prev, weights


def _odd_mixer_fwd(proj, pool_w, scale, dconv_w, dconv_b, ln_w, ln_b, T):
    S = proj.shape[0]
    col, prev, wspecs = _odd_specs(T, S, lambda s: s)
    H = HALO

    def body(uc_r, da_r, dg_r, zl_r, zh_r, ucp_r, dap_r, dgp_r, pw_r, sc_r, dw_r, db_r, lw_r, lb_r,
             u_ref, ut_ref, cv_ref, ext_u, ext_g, sh_g, pooled_s, pm_s):
        i = pl.program_id(0)
        _odd_pool_tile(i, uc_r, ucp_r, pw_r, ext_u, pooled_s, pm_s, T)
        _odd_glu_tile(i, da_r, dg_r, dap_r, dgp_r, ext_g, sh_g, T)
        base = H - (D_CONV - 1)
        conv = db_r[...] + dw_r[0:1, :] * _window(ext_g, sh_g, base, T)
        for kk in range(1, D_CONV):
            conv = conv + dw_r[kk:kk + 1, :] * _window(ext_g, sh_g, base + kk, T)
        cv_ref[...] = conv
        mu = jnp.mean(conv, axis=-1, keepdims=True)
        xc = conv - mu
        yh = xc * lax.rsqrt(jnp.mean(xc * xc, axis=-1, keepdims=True) + EPS)
        ln = yh * lw_r[...] + lb_r[...]
        zl, zh = zl_r[...], zh_r[...]
        _emit_u(u_ref, ut_ref, pm_s[...] * sc_r[...] * (zl * _sig(zl)), ln * _sig(ln) * (zh * _sig(zh)))

    return _call(
        body, "odd_mixer_fwd", (S // T,),
        [col(0), col(1), col(2), col(3), col(4), prev(0), prev(1), prev(2)] + wspecs,
        [pl.BlockSpec((T, D_MODEL), lambda i: (i, 0)), pl.BlockSpec((D_MODEL, T), lambda i: (0, i)),
         pl.BlockSpec((T, A_WIDTH), lambda i: (i, 0))],
        [jax.ShapeDtypeStruct((S, D_MODEL), BF16), jax.ShapeDtypeStruct((D_MODEL, S), BF16),
         jax.ShapeDtypeStruct((S, A_WIDTH), F32)],
        scratch=[pltpu.VMEM((T + H, A_WIDTH), F32), pltpu.VMEM((T + H, A_WIDTH), F32),
                 pltpu.VMEM((SUBLANES - 1, T + H - SHIFT_ROWS_LESS, A_WIDTH), F32),
                 pltpu.VMEM((T, A_WIDTH), F32), pltpu.VMEM((T, A_WIDTH), F32)],
        sem=("parallel",))(proj, proj, proj, proj, proj, proj, proj, proj,
                           pool_w, scale, dconv_w, dconv_b, ln_w, ln_b)


def _odd_mixer_bwd(du, proj, conv, pool_w, scale, dconv_w, dconv_b, ln_w, ln_b, T):
    S = proj.shape[0]
    nt = S // T
    order = lambda s: nt - 1 - s
    col, prev, wspecs = _odd_specs(T, S, order)
    H = HALO

    def body(dul_r, duh_r, cv_r, uc_r, da_r, dg_r, zl_r, zh_r, ucp_r, dap_r, dgp_r, pw_r, sc_r, dw_r, db_r, lw_r, lb_r,
             dp_ref, dpw_ref, sm_ref, ext_u, ext_g, sh_g, pooled_s, pm_s, dpl_s, ext_p, ext_c, sh_c, acc):
        step = pl.program_id(0)
        i = nt - 1 - step

        @pl.when(step == 0)
        def _():
            ext_p[T:T + H, :] = jnp.zeros((H, A_WIDTH), F32)
            ext_c[T:T + H, :] = jnp.zeros((H, A_WIDTH), F32)
            acc[...] = jnp.zeros_like(acc)
            dpw_ref[...] = jnp.zeros_like(dpw_ref)

        def accum(r, v):
            acc[r * SUBLANES:(r + 1) * SUBLANES, :] += _cs8(v)

        row = _odd_pool_tile(i, uc_r, ucp_r, pw_r, ext_u, pooled_s, pm_s, T)
        _odd_glu_tile(i, da_r, dg_r, dap_r, dgp_r, ext_g, sh_g, T)
        conv = cv_r[...]
        mu = jnp.mean(conv, axis=-1, keepdims=True)
        xc = conv - mu
        rstd = lax.rsqrt(jnp.mean(xc * xc, axis=-1, keepdims=True) + EPS)
        yh = xc * rstd
        ln = yh * lw_r[...] + lb_r[...]
        sln = _sig(ln)
        zl, zh = zl_r[...], zh_r[...]
        sl, sh = _sig(zl), _sig(zh)
        dul, duh = dul_r[...], duh_r[...]
        pm = pm_s[...]
        scv = sc_r[...]
        dyc = dul * (zl * sl)
        accum(34, dyc * pm)
        dpm = dyc * scv
        for g in range(len(POOL_SIZES)):
            cs = slice(g * LANES, (g + 1) * LANES)
            dpm_g = dpm[:, cs].astype(BF16)
            dpw_ref[g] += lax.dot_general(pooled_s[:, cs].astype(BF16), dpm_g, TN_DIMS, preferred_element_type=F32)
            dpl_s[:, cs] = lax.dot_general(dpm_g, pw_r[g].astype(BF16), NT_DIMS, preferred_element_type=F32)
        lane_p = lax.broadcasted_iota(jnp.int32, (1, A_WIDTH), 1) // LANES
        pvec = jnp.left_shift(2, lane_p)
        cnt = jnp.minimum(row + 1, pvec).astype(F32)
        dpl = dpl_s[...]
        ext_p[0:T, :] = dpl / cnt
        for g, p in enumerate(POOL_SIZES):
            cs = slice(g * LANES, (g + 1) * LANES)
            win = ext_p[0:T, cs]
            for j in range(1, p):
                win = win + ext_p[j:j + T, cs]
            dp_ref[:, cs] = (win - dpl[:, cs]).astype(BF16)
        ext_p[T:T + H, :] = ext_p[0:H, :]
        dln = duh * (zh * sh) * _dsilu(ln, sln)
        accum(32, dln * yh)
        accum(33, dln)
        dyh = dln * lw_r[...]
        dc = rstd * (dyh - jnp.mean(dyh, axis=-1, keepdims=True) - yh * jnp.mean(dyh * yh, axis=-1, keepdims=True))
        accum(31, dc)
        ext_c[0:T, :] = dc
        _fill_shifted(ext_c, sh_c)
        base = H - (D_CONV - 1)
        dgl = dw_r[0:1, :] * _window(ext_c, sh_c, D_CONV - 1, T)
        accum(0, dc * _window(ext_g, sh_g, base, T))
        for kk in range(1, D_CONV):
            dgl = dgl + dw_r[kk:kk + 1, :] * _window(ext_c, sh_c, D_CONV - 1 - kk, T)
            accum(kk, dc * _window(ext_g, sh_g, base + kk, T))
        ext_c[T:T + H, :] = ext_c[0:H, :]
        dav, dgv = da_r[...], dg_r[...]
        sg = _sig(dgv)
        dp_ref[:, A_WIDTH:2 * A_WIDTH] = (dgl * sg).astype(BF16)
        dp_ref[:, 2 * A_WIDTH:3 * A_WIDTH] = (dgl * dav * sg * (1.0 - sg)).astype(BF16)
        dp_ref[:, 3 * A_WIDTH:4 * A_WIDTH] = (dul * (pm * scv) * _dsilu(zl, sl)).astype(BF16)
        dp_ref[:, 4 * A_WIDTH:5 * A_WIDTH] = (duh * (ln * sln) * _dsilu(zh, sh)).astype(BF16)

        @pl.when(step == nt - 1)
        def _():
            for r in range(N_SMALL_ODD):
                sm_ref[r:r + 1, :] = jnp.sum(acc[r * SUBLANES:(r + 1) * SUBLANES, :], axis=0, keepdims=True)

    ext = pltpu.VMEM((T + H, A_WIDTH), F32)
    shifted = pltpu.VMEM((SUBLANES - 1, T + H - SHIFT_ROWS_LESS, A_WIDTH), F32)
    tile = pltpu.VMEM((T, A_WIDTH), F32)
    return _call(
        body, "odd_mixer_bwd", (nt,),
        [pl.BlockSpec((T, A_WIDTH), lambda s: (order(s), 0)), pl.BlockSpec((T, A_WIDTH), lambda s: (order(s), 1)),
         pl.BlockSpec((T, A_WIDTH), lambda s: (order(s), 0)),
         col(0), col(1), col(2), col(3), col(4), prev(0), prev(1), prev(2)] + wspecs,
        [pl.BlockSpec((T, ODD_IN), lambda s: (order(s), 0)),
         pl.BlockSpec((4, LANES, LANES), lambda s: (0, 0, 0)),
         pl.BlockSpec((N_SMALL_ODD, A_WIDTH), lambda s: (0, 0))],
        [jax.ShapeDtypeStruct((S, ODD_IN), BF16), jax.ShapeDtypeStruct((4, LANES, LANES), F32),
         jax.ShapeDtypeStruct((N_SMALL_ODD, A_WIDTH), F32)],
        scratch=[ext, ext, shifted, tile, tile, tile, ext, ext, shifted,
                 pltpu.VMEM((N_SMALL_ODD * SUBLANES, A_WIDTH), F32)],
        sem=("arbitrary",))(du, du, conv, proj, proj, proj, proj, proj, proj, proj, proj,
                            pool_w, scale, dconv_w, dconv_b, ln_w, ln_b)


TILE_SEQ = 256
TILE_M = 512
TILE_WG = 256


def _local_step(x, pos, tgt, wb, p):
    T = TILE_SEQ
    freq = _freq_table()
    wq = jnp.tile(p["e_q_norm_w"], (1, LANES // HEAD_DIM))
    wk = jnp.tile(p["e_k_norm_w"], (1, LANES // HEAD_DIM))

    proj_e, ht_e = _inproj(x, p["e_norm_w"], wb["e_w_in"], TILE_SEQ, 1792, "inproj_even")
    qkv = _qkv_prep(proj_e, pos, freq, wq, wk, T)
    qs, ks, vs = qkv[0:3], qkv[3:6], qkv[6:9]
    os_, ls_ = [], []
    for g in range(3):
        o, l = _attn_fwd(qs[g], ks[g], vs[g], g)
        os_.append(o)
        ls_.append(l)
    u_e, ut_e = _even_mixer_fwd(proj_e, os_, ls_, p["e_conv_w"], T)
    x1 = _outproj(x, u_e, wb["e_w_out"], TILE_M, "outproj_even")
    proj_o, ht_o = _inproj(x1, p["o_norm_w"], wb["o_w_in"], TILE_SEQ, 1280, "inproj_odd")
    odd_w = (p["o_pool_w"], p["o_pool_scale"], p["o_dconv_w"], p["o_dconv_b"], p["o_ln_w"], p["o_ln_b"])
    u_o, ut_o, conv_o = _odd_mixer_fwd(proj_o, *odd_w, T)
    dy, lsum = _outproj_loss(x1, u_o, wb["o_w_out"], tgt, TILE_M, "outproj_odd_loss")

    g = {}
    g["o_w_out"] = _mm_wgrad(ut_o, [dy], TILE_WG, "wgrad_o_out")
    du_o = _mm_nt(dy, wb["o_w_out"], TILE_M, "du_odd")
    dproj_o, g["o_pool_w"], small_o = _odd_mixer_bwd(du_o, proj_o, conv_o, *odd_w, T)
    g["o_w_in"] = _mm_wgrad(ht_o, [dproj_o], TILE_WG, "wgrad_o_in")
    dx1, g["o_norm_w"] = _mm_nt_rms([dproj_o], wb["o_w_in"], x1, p["o_norm_w"], dy, TILE_SEQ, "dx_odd")
    g["o_dconv_w"] = small_o[0:D_CONV]
    g["o_dconv_b"] = small_o[31:32]
    g["o_ln_w"] = small_o[32:33]
    g["o_ln_b"] = small_o[33:34]
    g["o_pool_scale"] = small_o[34:35]

    g["e_w_out"] = _mm_wgrad(ut_e, [dx1], TILE_WG, "wgrad_e_out")
    du_e = _mm_nt(dx1, wb["e_w_out"], TILE_M, "du_even")
    dos, cgs, drest, g["e_conv_w"] = _even_mixer_bwd(du_e, proj_e, os_, ls_, p["e_conv_w"], T)
    dqs, dks, dvs = [], [], []
    for gi in range(3):
        dq, dk, dv = _attn_bwd(qs[gi], ks[gi], vs[gi], dos[gi], ls_[gi], cgs[gi], gi)
        dqs.append(dq)
        dks.append(dk)
        dvs.append(dv)
    dqkv, dnw = _qk_bwd(proj_e, dqs, dks, dvs, pos, freq, wq, wk, T)
    g["e_q_norm_w"] = dnw[0:1, 0:HEAD_DIM]
    g["e_k_norm_w"] = dnw[1:2, 0:HEAD_DIM]
    pieces = [dqkv, drest]
    g["e_w_in"] = _mm_wgrad(ht_e, pieces, TILE_WG, "wgrad_e_in")
    dx, g["e_norm_w"] = _mm_nt_rms(pieces, wb["e_w_in"], x, p["e_norm_w"], dx1, TILE_SEQ, "dx_even")
    return lsum, dx, g


BIG = ("e_w_in", "e_w_out", "o_w_in", "o_w_out")
SHARD_AXIS = {"e_w_in": 1, "e_w_out": 0, "o_w_in": 1, "o_w_out": 0}
N_CHIPS = 4


def _place():
    x, y, c = lax.axis_index("x"), lax.axis_index("y"), lax.axis_index("c")
    chips = [(1 - x, y), (x, 1 - y), (1 - x, 1 - y)]
    return x, y, c, chips


def _block_of(ref, name, block):
    rows, cols = ref.shape
    if SHARD_AXIS[name] == 1:
        cw = cols // N_CHIPS
        return ref.at[:, pl.ds(pl.multiple_of(block * cw, LANES), cw)]
    rw = rows // N_CHIPS
    return ref.at[pl.ds(pl.multiple_of(block * rw, rw), rw), :]


def _half_of(ref, name, half):
    rows, cols = ref.shape
    if SHARD_AXIS[name] == 1:
        return ref.at[pl.ds(pl.multiple_of(half * (rows // 2), rows // 2), rows // 2), :]
    return ref.at[:, pl.ds(pl.multiple_of(half * (cols // 2), LANES), cols // 2)]


def _sub(ref, name, block, half):
    rows, cols = ref.shape
    if SHARD_AXIS[name] == 1:
        cw, hr = cols // N_CHIPS, rows // 2
        return ref.at[pl.ds(pl.multiple_of(half * hr, hr), hr), pl.ds(pl.multiple_of(block * cw, LANES), cw)]
    rw, hc = rows // N_CHIPS, cols // 2
    return ref.at[pl.ds(pl.multiple_of(block * rw, rw), rw), pl.ds(pl.multiple_of(half * hc, LANES), hc)]


def _gather_weights(shards):
    nw = len(BIG)

    def body(*refs):
        s_refs = dict(zip(BIG, refs[:nw]))
        f_refs = dict(zip(BIG, refs[nw:2 * nw]))
        send, recv = refs[2 * nw:]
        x, y, c, chips = _place()
        me = 2 * x + y
        sib = (x, y, 1 - c)

        def rc(k, src, dst, to):
            return pltpu.make_async_remote_copy(src_ref=src, dst_ref=dst, send_sem=send.at[k], recv_sem=recv.at[k],
                                                device_id=to, device_id_type=MESH)

        first, fwd = [], []
        for wi, n in enumerate(BIG):
            own = _block_of(f_refs[n], n, me)
            cp = rc(24 + wi, s_refs[n], own, sib)
            cp.start()
            first.append(cp)
            for j, (cx, cy) in enumerate(chips):
                cp = rc(wi * 3 + j, _half_of(s_refs[n], n, c), _sub(f_refs[n], n, me, c), (cx, cy, c))
                cp.start()
                first.append(cp)
        for wi, n in enumerate(BIG):
            for j, (cx, cy) in enumerate(chips):
                part = _sub(f_refs[n], n, 2 * cx + cy, c)
                rc(wi * 3 + j, part, part, sib).wait_recv()
                cp = rc(12 + wi * 3 + j, part, part, sib)
                cp.start()
                fwd.append(cp)
        for wi, n in enumerate(BIG):
            own = _block_of(f_refs[n], n, me)
            rc(24 + wi, s_refs[n], own, sib).wait_recv()
            for j, (cx, cy) in enumerate(chips):
                part = _sub(f_refs[n], n, 2 * cx + cy, 1 - c)
                rc(12 + wi * 3 + j, part, part, sib).wait_recv()
        for cp in first + fwd:
            cp.wait_send()

    outs = []
    for n, s in zip(BIG, shards):
        r, cdim = s.shape
        outs.append(jax.ShapeDtypeStruct((r, cdim * N_CHIPS) if SHARD_AXIS[n] == 1 else (r * N_CHIPS, cdim), s.dtype))
    return pl.pallas_call(
        body, name="gather_weights", in_specs=[ANY] * nw, out_specs=[ANY] * nw, out_shape=outs,
        scratch_shapes=[pltpu.SemaphoreType.DMA((28,)), pltpu.SemaphoreType.DMA((28,))],
    )(*shards)


def _allreduce_small(part, name):
    R = part.shape[0]

    def body(p_ref, o_ref, sbuf, cbuf, send, recv):
        x, y, c, chips = _place()
        me = 2 * x + y
        sib = (x, y, 1 - c)
        sbuf[c] = p_ref[...]
        mine = sbuf.at[c]
        d2d = pltpu.make_async_remote_copy(src_ref=mine, dst_ref=mine, send_sem=send.at[0], recv_sem=recv.at[0],
                                           device_id=sib, device_id_type=MESH)
        d2d.start()
        theirs = sbuf.at[1 - c]
        pltpu.make_async_remote_copy(src_ref=theirs, dst_ref=theirs, send_sem=send.at[0], recv_sem=recv.at[0],
                                     device_id=sib, device_id_type=MESH).wait_recv()
        cbuf[me] = sbuf[0] + sbuf[1]
        blk = cbuf.at[me]
        sends = [d2d]
        for j, (cx, cy) in enumerate(chips):
            cp = pltpu.make_async_remote_copy(src_ref=blk, dst_ref=blk, send_sem=send.at[1 + j], recv_sem=recv.at[1 + j],
                                              device_id=(cx, cy, c), device_id_type=MESH)
            cp.start()
            sends.append(cp)
        for j, (cx, cy) in enumerate(chips):
            got = cbuf.at[2 * cx + cy]
            pltpu.make_async_remote_copy(src_ref=got, dst_ref=got, send_sem=send.at[1 + j], recv_sem=recv.at[1 + j],
                                         device_id=(cx, cy, c), device_id_type=MESH).wait_recv()
        o_ref[...] = (cbuf[0] + cbuf[1]) + (cbuf[2] + cbuf[3])
        for cp in sends:
            cp.wait_send()

    vm = pl.BlockSpec(memory_space=pltpu.VMEM)
    return pl.pallas_call(
        body, name=name, in_specs=[vm], out_specs=vm,
        out_shape=jax.ShapeDtypeStruct(part.shape, F32),
        scratch_shapes=[pltpu.VMEM((2, R, LANES), F32), pltpu.VMEM((N_CHIPS, R, LANES), F32),
                        pltpu.SemaphoreType.DMA((4,)), pltpu.SemaphoreType.DMA((4,))],
    )(part)


def _half_shape(shape, name):
    r, cdim = shape
    return (r // 2, cdim) if SHARD_AXIS[name] == 1 else (r, cdim // 2)


def _shard_shape(shape, name):
    r, cdim = shape
    return (r, cdim // N_CHIPS) if SHARD_AXIS[name] == 1 else (r // N_CHIPS, cdim)


def _swap_to_sibling(srcs, name, pick_half):
    nw = len(BIG)

    def body(*refs):
        g_refs = dict(zip(BIG, refs[:nw]))
        r_refs = dict(zip(BIG, refs[nw:2 * nw]))
        send, recv = refs[2 * nw:]
        x, y, c, _ = _place()
        sib = (x, y, 1 - c)
        cps = []
        for wi, n in enumerate(BIG):
            src = _half_of(g_refs[n], n, 1 - c) if pick_half else g_refs[n]
            cp = pltpu.make_async_remote_copy(src_ref=src, dst_ref=r_refs[n], send_sem=send.at[wi],
                                              recv_sem=recv.at[wi], device_id=sib, device_id_type=MESH)
            cp.start()
            cps.append(cp)
        for cp in cps:
            cp.wait()

    outs = [jax.ShapeDtypeStruct(_half_shape(g.shape, n) if pick_half else g.shape, g.dtype)
            for n, g in zip(BIG, srcs)]
    return pl.pallas_call(
        body, name=name, in_specs=[ANY] * nw, out_specs=[ANY] * nw, out_shape=outs,
        scratch_shapes=[pltpu.SemaphoreType.DMA((nw,)), pltpu.SemaphoreType.DMA((nw,))],
    )(*srcs)


def _add_half(cidx, g, r, name):
    rows, cols = r.shape
    tr = 256
    tc = cols if cols <= 1792 else (1792 if cols % 1792 == 0 else 1280)
    nr, nc = rows // tr, cols // tc

    def body(c_ref, g_ref, r_ref, o_ref, ob_ref):
        s = g_ref[...] + r_ref[...]
        o_ref[...] = s
        ob_ref[...] = s.astype(BF16)

    if SHARD_AXIS[name] == 1:
        gmap = lambda i, j, c_ref: (c_ref[0] * nr + i, j)
    else:
        gmap = lambda i, j, c_ref: (i, c_ref[0] * nc + j)
    same = lambda i, j, c_ref: (i, j)
    return pl.pallas_call(
        body, name="add_half_" + name,
        grid_spec=pltpu.PrefetchScalarGridSpec(
            num_scalar_prefetch=1, grid=(nr, nc),
            in_specs=[pl.BlockSpec((tr, tc), gmap), pl.BlockSpec((tr, tc), same)],
            out_specs=[pl.BlockSpec((tr, tc), same), pl.BlockSpec((tr, tc), same)]),
        out_shape=[jax.ShapeDtypeStruct(r.shape, F32), jax.ShapeDtypeStruct(r.shape, BF16)],
        compiler_params=pltpu.CompilerParams(dimension_semantics=("parallel", "parallel"), vmem_limit_bytes=VMEM_LIMIT),
    )(cidx, g, r)


def _scatter_blocks(halves):
    nw = len(BIG)

    def body(*refs):
        h_refs = dict(zip(BIG, refs[:nw]))
        r_refs = dict(zip(BIG, refs[nw:2 * nw]))
        send, recv = refs[2 * nw:]
        x, y, c, chips = _place()
        cps = []
        for wi, n in enumerate(BIG):
            for j, (cx, cy) in enumerate(chips):
                cp = pltpu.make_async_remote_copy(
                    src_ref=_block_of(h_refs[n], n, 2 * cx + cy), dst_ref=r_refs[n].at[j],
                    send_sem=send.at[wi * 3 + j], recv_sem=recv.at[wi * 3 + j],
                    device_id=(cx, cy, c), device_id_type=MESH)
                cp.start()
                cps.append(cp)
        for cp in cps:
            cp.wait()

    outs = [jax.ShapeDtypeStruct((3,) + _shard_shape(h.shape, n), h.dtype) for n, h in zip(BIG, halves)]
    return pl.pallas_call(
        body, name="scatter_blocks", in_specs=[ANY] * nw, out_specs=[ANY] * nw, out_shape=outs,
        scratch_shapes=[pltpu.SemaphoreType.DMA((3 * nw,)), pltpu.SemaphoreType.DMA((3 * nw,))],
    )(*halves)


def _add_blocks(bidx, h, r, name):
    _, rows, cols = r.shape
    tr = min(rows, 256)
    nr = rows // tr

    def body(b_ref, h_ref, r0, r1, r2, o_ref):
        o_ref[...] = ((h_ref[...] + r0[0].astype(F32)) + r1[0].astype(F32)) + r2[0].astype(F32)

    if SHARD_AXIS[name] == 1:
        hmap = lambda i, b_ref: (i, b_ref[0])
    else:
        hmap = lambda i, b_ref: (b_ref[0] * nr + i, 0)
    rspec = lambda j: pl.BlockSpec((1, tr, cols), lambda i, b_ref, j=j: (j, i, 0))
    return pl.pallas_call(
        body, name="add_blocks_" + name,
        grid_spec=pltpu.PrefetchScalarGridSpec(
            num_scalar_prefetch=1, grid=(nr,),
            in_specs=[pl.BlockSpec((tr, cols), hmap), rspec(0), rspec(1), rspec(2)],
            out_specs=pl.BlockSpec((tr, cols), lambda i, b_ref: (i, 0))),
        out_shape=jax.ShapeDtypeStruct((rows, cols), F32),
        compiler_params=pltpu.CompilerParams(dimension_semantics=("parallel",), vmem_limit_bytes=VMEM_LIMIT),
    )(bidx, h, r, r, r)


def _adam_math(w, g, m, v):
    c1 = 1.0 - ADAM_B1 ** ADAM_STEP
    c2 = 1.0 - ADAM_B2 ** ADAM_STEP
    nm = ADAM_B1 * m + (1.0 - ADAM_B1) * g
    nv = ADAM_B2 * v + (1.0 - ADAM_B2) * (g * g)
    delta = -ADAM_LR * ((nm / c1) / (jnp.sqrt(nv / c2) + ADAM_EPS) + ADAM_WD * w)
    return delta, nm, nv


def _adamw(w, g, m, v, name):
    def body(w_ref, g_ref, m_ref, v_ref, d_ref, nm_ref, nv_ref):
        d_ref[...], nm_ref[...], nv_ref[...] = _adam_math(w_ref[...], g_ref[...], m_ref[...], v_ref[...])

    spec = pl.BlockSpec(w.shape, lambda i: (0, 0))
    return _call(body, "adamw_" + name, (1,), [spec] * 4, [spec] * 3,
                 [jax.ShapeDtypeStruct(w.shape, F32)] * 3, sem=("arbitrary",))(w, g, m, v)


def _adamw_halves(cidx, w, mine, theirs, m, v, name):
    rows, cols = w.shape
    hr, hc = mine.shape
    tr = 128
    if SHARD_AXIS[name] == 1:
        ni = hr // tr
        wmap = lambda hh, i, c_ref: (hh * ni + i, 0)
    else:
        ni = hr // tr
        wmap = lambda hh, i, c_ref: (i, hh)
    hmap = lambda hh, i, c_ref: (i, 0)

    def body(c_ref, w_ref, a_ref, b_ref, m_ref, v_ref, g_ref, d_ref, nm_ref, nv_ref):
        g = jnp.where(pl.program_id(0) == c_ref[0], a_ref[...], b_ref[...])
        g_ref[...] = g
        d_ref[...], nm_ref[...], nv_ref[...] = _adam_math(w_ref[...], g, m_ref[...], v_ref[...])

    wspec = pl.BlockSpec((tr, hc), wmap)
    hspec = pl.BlockSpec((tr, hc), hmap)
    return pl.pallas_call(
        body, name="adamw_" + name,
        grid_spec=pltpu.PrefetchScalarGridSpec(
            num_scalar_prefetch=1, grid=(2, ni),
            in_specs=[wspec, hspec, hspec, wspec, wspec], out_specs=[wspec] * 4),
        out_shape=[jax.ShapeDtypeStruct(w.shape, F32)] * 4,
        compiler_params=pltpu.CompilerParams(dimension_semantics=("parallel", "parallel"), vmem_limit_bytes=VMEM_LIMIT),
    )(cidx, w, mine, theirs, m, v)


SMALL = ("e_norm_w", "e_q_norm_w", "e_k_norm_w", "e_conv_w", "o_norm_w", "o_pool_w", "o_pool_scale",
         "o_dconv_w", "o_dconv_b", "o_ln_w", "o_ln_b")
SMALL_SHARDED = ("e_conv_w", "o_norm_w", "o_pool_scale", "o_dconv_w", "o_dconv_b", "o_ln_w", "o_ln_b")
WEIGHTS = ("e_norm_w", "e_w_in", "e_q_norm_w", "e_k_norm_w", "e_conv_w", "e_w_out", "o_norm_w", "o_w_in",
           "o_pool_w", "o_pool_scale", "o_dconv_w", "o_dconv_b", "o_ln_w", "o_ln_b", "o_w_out")


def _pack(arrs):
    flat = jnp.concatenate([a.reshape(-1) for a in arrs])
    rows = -(-flat.shape[0] // (LANES * SUBLANES)) * SUBLANES
    flat = jnp.pad(flat, (0, rows * LANES - flat.shape[0]))
    return flat.reshape(rows, LANES)


def _unpack(packed, shapes):
    flat = packed.reshape(-1)
    out, off = [], 0
    for s in shapes:
        n = int(np.prod(s))
        out.append(flat[off:off + n].reshape(s))
        off += n
    return out


def _gather_last(a, block, width):
    return lax.dynamic_slice_in_dim(a, block * width, width, axis=a.ndim - 1)


def kernel(x, positions, e_norm_w, e_w_in, e_q_norm_w, e_k_norm_w, e_conv_w, e_w_out, o_norm_w, o_w_in, o_pool_w, o_pool_scale, o_dconv_w, o_dconv_b, o_ln_w, o_ln_b, o_w_out, loss_target, m_e_norm_w, m_e_w_in, m_e_q_norm_w, m_e_k_norm_w, m_e_conv_w, m_e_w_out, m_o_norm_w, m_o_w_in, m_o_pool_w, m_o_pool_scale, m_o_dconv_w, m_o_dconv_b, m_o_ln_w, m_o_ln_b, m_o_w_out, v_e_norm_w, v_e_w_in, v_e_q_norm_w, v_e_k_norm_w, v_e_conv_w, v_e_w_out, v_o_norm_w, v_o_w_in, v_o_pool_w, v_o_pool_scale, v_o_dconv_w, v_o_dconv_b, v_o_ln_w, v_o_ln_b, v_o_w_out):
    given = dict(e_norm_w=e_norm_w, e_w_in=e_w_in, e_q_norm_w=e_q_norm_w, e_k_norm_w=e_k_norm_w, e_conv_w=e_conv_w,
                 e_w_out=e_w_out, o_norm_w=o_norm_w, o_w_in=o_w_in, o_pool_w=o_pool_w, o_pool_scale=o_pool_scale,
                 o_dconv_w=o_dconv_w, o_dconv_b=o_dconv_b, o_ln_w=o_ln_w, o_ln_b=o_ln_b, o_w_out=o_w_out)
    mom = dict(e_norm_w=m_e_norm_w, e_w_in=m_e_w_in, e_q_norm_w=m_e_q_norm_w, e_k_norm_w=m_e_k_norm_w,
               e_conv_w=m_e_conv_w, e_w_out=m_e_w_out, o_norm_w=m_o_norm_w, o_w_in=m_o_w_in, o_pool_w=m_o_pool_w,
               o_pool_scale=m_o_pool_scale, o_dconv_w=m_o_dconv_w, o_dconv_b=m_o_dconv_b, o_ln_w=m_o_ln_w,
               o_ln_b=m_o_ln_b, o_w_out=m_o_w_out)
    var = dict(e_norm_w=v_e_norm_w, e_w_in=v_e_w_in, e_q_norm_w=v_e_q_norm_w, e_k_norm_w=v_e_k_norm_w,
               e_conv_w=v_e_conv_w, e_w_out=v_e_w_out, o_norm_w=v_o_norm_w, o_w_in=v_o_w_in, o_pool_w=v_o_pool_w,
               o_pool_scale=v_o_pool_scale, o_dconv_w=v_o_dconv_w, o_dconv_b=v_o_dconv_b, o_ln_w=v_o_ln_w,
               o_ln_b=v_o_ln_b, o_w_out=v_o_w_out)
    S = x.shape[1]
    mx, my, mc = lax.axis_index("x"), lax.axis_index("y"), lax.axis_index("c")
    chip = 2 * mx + my
    cidx = jnp.reshape(mc, (1,)).astype(jnp.int32)
    bidx = jnp.reshape(chip, (1,)).astype(jnp.int32)

    full_b = _gather_weights([given[n][0].astype(BF16) for n in BIG])
    wb = dict(zip(BIG, full_b))
    shard_sizes = [int(np.prod(given[n].shape)) for n in SMALL_SHARDED]
    own = _pack([given[n] for n in SMALL_SHARDED])
    rows = own.shape[0]
    slots = jnp.zeros((N_CHIPS, rows, LANES), F32)
    own = jnp.where(mc == 0, own, 0.0)
    slots = lax.dynamic_update_slice(slots, own[None], (chip, 0, 0))
    gathered = _allreduce_small(slots.reshape(N_CHIPS * rows, LANES), "gather_small")
    gathered = gathered.reshape(N_CHIPS, rows * LANES)
    p = {}
    off = 0
    for n, size in zip(SMALL_SHARDED, shard_sizes):
        sh = given[n].shape[1:]
        parts = gathered[:, off:off + size].reshape((N_CHIPS,) + sh)
        fullp = jnp.moveaxis(parts, 0, -2).reshape(sh[:-1] + (N_CHIPS * sh[-1],))
        p[n] = fullp.reshape(-1, fullp.shape[-1])
        off += size
    p["e_norm_w"] = e_norm_w
    p["e_q_norm_w"] = e_q_norm_w
    p["e_k_norm_w"] = e_k_norm_w
    p["o_pool_w"] = o_pool_w[0]

    lsum, dx, g = _local_step(x[0], positions.reshape(S, 1), loss_target[0], wb, p)
    loss = lax.psum(0.5 * jnp.sum(lsum) / float(D_MODEL), ("x", "y", "c"))

    tot = _unpack(_allreduce_small(_pack([g[n] for n in SMALL]), "allreduce_small"), [g[n].shape for n in SMALL])
    gsmall = dict(zip(SMALL, tot))
    grads = {}
    for n in SMALL:
        gv = gsmall[n]
        if n in SMALL_SHARDED:
            gv = _gather_last(gv, chip, gv.shape[-1] // N_CHIPS)
        grads[n] = gv.reshape(given[n].shape)

    recv_half = _swap_to_sibling([g[n] for n in BIG], "swap_halves", True)
    halves = [_add_half(cidx, g[n], r, n) for n, r in zip(BIG, recv_half)]
    recv_blk = _scatter_blocks([hb for _, hb in halves])
    parts = [_add_blocks(bidx, h, r, n) for n, (h, _), r in zip(BIG, halves, recv_blk)]
    theirs = _swap_to_sibling(parts, "swap_reduced", False)

    delta, new_m, new_v = {}, {}, {}
    for n, mine, other in zip(BIG, parts, theirs):
        sh = given[n].shape
        gs, d, nm, nv = _adamw_halves(cidx, given[n][0], mine, other, mom[n][0], var[n][0], n)
        grads[n], delta[n], new_m[n], new_v[n] = gs.reshape(sh), d.reshape(sh), nm.reshape(sh), nv.reshape(sh)
    shapes = [given[n].shape for n in SMALL]
    packed = [_pack([src[n] for n in SMALL]) for src in (given, grads, mom, var)]
    for dst, pk in zip((delta, new_m, new_v), _adamw(*packed, "small")):
        for n, a in zip(SMALL, _unpack(pk, shapes)):
            dst[n] = a
    return (loss, dx[None], *[grads[n] for n in WEIGHTS], *[delta[n] for n in WEIGHTS],
            *[new_m[n] for n in WEIGHTS], *[new_v[n] for n in WEIGHTS])
```

```python
import numpy as np
import jax
import jax.numpy as jnp
from jax import lax
from jax.experimental import pallas as pl
from jax.experimental.pallas import tpu as pltpu

F32 = jnp.float32
BF16 = jnp.bfloat16

D_MODEL = 1024
HEAD_DIM = 64
A_WIDTH = 512
A_HEADS = 8
A_GROUPS = ((128, 1), (512, 4), (2048, 16))
QBLK = 128
ROT_DIM = 16
ROPE_THETA = 500000.0
POOL_SIZES = (2, 4, 8, 16)
D_CONV = 31
SC_WIDTH = 3
EVEN_IN = 7168
ODD_IN = 2560
EPS = 1e-6
NEG = -1e30
ADAM_LR, ADAM_B1, ADAM_B2, ADAM_EPS, ADAM_WD, ADAM_STEP = 0.001, 0.9, 0.999, 1e-08, 0.01, 10

LANES = 128
SUBLANES = 8
HALO = 32
VMEM_LIMIT = 52 * 1024 * 1024
MESH = pl.DeviceIdType.MESH
ANY = pl.BlockSpec(memory_space=pl.ANY)

NT_DIMS = (((1,), (1,)), ((), ()))
TN_DIMS = (((0,), (0,)), ((), ()))


def _call(body, name, grid, in_specs, out_specs, out_shape, scratch=(), sem=None, aliases=None):
    return pl.pallas_call(
        body, name=name, grid=grid, in_specs=in_specs, out_specs=out_specs, out_shape=out_shape,
        scratch_shapes=list(scratch), input_output_aliases=aliases or {},
        compiler_params=pltpu.CompilerParams(dimension_semantics=sem, vmem_limit_bytes=VMEM_LIMIT))


def _sig(v):
    return jax.nn.sigmoid(v)


def _dsilu(v, s):
    return s * (1.0 + v * (1.0 - s))


def _emit_u(u_ref, ut_ref, lo, hi):
    for k, v in enumerate((lo, hi)):
        u_ref[:, k * A_WIDTH:(k + 1) * A_WIDTH] = v.astype(BF16)
        ut_ref[k * A_WIDTH:(k + 1) * A_WIDTH, :] = v.T.astype(BF16)


def _cs8(v):
    return v.reshape(v.shape[0] // SUBLANES, SUBLANES, v.shape[1]).sum(axis=0)


def _seg_ones():
    r = lax.broadcasted_iota(jnp.int32, (LANES, LANES), 0) // HEAD_DIM
    c = lax.broadcasted_iota(jnp.int32, (LANES, LANES), 1) // HEAD_DIM
    return (r == c).astype(BF16)


def _segsum(v, ones):
    hi = v.astype(BF16)
    lo = (v - hi.astype(F32)).astype(BF16)
    return (jnp.dot(hi, ones, preferred_element_type=F32) + jnp.dot(lo, ones, preferred_element_type=F32))


def _rope_tables(pos_ref, freq_ref):
    ang = pos_ref[...].astype(F32) * freq_ref[...]
    cosv, sinv = jnp.cos(ang), jnp.sin(ang)
    lm = lax.broadcasted_iota(jnp.int32, ang.shape, 1) % HEAD_DIM
    half = ROT_DIM // 2
    c = jnp.where(lm < ROT_DIM, cosv, 1.0)
    s1 = jnp.where((lm >= half) & (lm < ROT_DIM), sinv, 0.0)
    s2 = jnp.where(lm < half, -sinv, 0.0)
    return c, s1, s2


def _freq_table():
    half = ROT_DIM // 2
    inv = ROPE_THETA ** (-np.arange(half, dtype=np.float64) / half)
    lane = np.arange(LANES) % HEAD_DIM
    f = np.where(lane < ROT_DIM, inv[lane % half], 0.0)
    return jnp.asarray(f.reshape(1, LANES), F32)


def _load_once(hbm_ref, vmem_ref, sem):
    @pl.when(pl.program_id(0) == 0)
    def _():
        cp = pltpu.make_async_copy(hbm_ref, vmem_ref, sem)
        cp.start()
        cp.wait()


def _inproj(x, nw, w, tm, tn, name):
    S, N = x.shape[0], w.shape[1]

    def body(x_ref, nw_ref, w_hbm, o_ref, ht_ref, w_v, sem):
        _load_once(w_hbm, w_v, sem)
        xv = x_ref[...]
        ms = jnp.mean(xv * xv, axis=-1, keepdims=True)
        h = xv * lax.rsqrt(ms + EPS) * nw_ref[...]
        ht_ref[...] = h.T.astype(BF16)
        hb = h.astype(BF16)
        for j in range(N // tn):
            o_ref[:, j * tn:(j + 1) * tn] = jnp.dot(hb, w_v[:, j * tn:(j + 1) * tn], preferred_element_type=F32)

    return _call(
        body, name, (S // tm,),
        [pl.BlockSpec((tm, D_MODEL), lambda i: (i, 0)),
         pl.BlockSpec((1, D_MODEL), lambda i: (0, 0)), ANY],
        [pl.BlockSpec((tm, N), lambda i: (i, 0)),
         pl.BlockSpec((D_MODEL, tm), lambda i: (0, i))],
        [jax.ShapeDtypeStruct((S, N), F32), jax.ShapeDtypeStruct((D_MODEL, S), BF16)],
        scratch=[pltpu.VMEM(w.shape, BF16), pltpu.SemaphoreType.DMA(())], sem=("arbitrary",))(x, nw, w)


def _outproj(x, u, w, tm, name):
    S = x.shape[0]

    def body(x_ref, u_ref, w_ref, o_ref):
        o_ref[...] = x_ref[...] + jnp.dot(u_ref[...].astype(BF16), w_ref[...], preferred_element_type=F32)

    return _call(
        body, name, (S // tm,),
        [pl.BlockSpec((tm, D_MODEL), lambda i: (i, 0)),
         pl.BlockSpec((tm, D_MODEL), lambda i: (i, 0)),
         pl.BlockSpec((D_MODEL, D_MODEL), lambda i: (0, 0))],
        pl.BlockSpec((tm, D_MODEL), lambda i: (i, 0)),
        jax.ShapeDtypeStruct((S, D_MODEL), F32), sem=("parallel",))(x, u, w)


def _outproj_loss(x, u, w, tgt, tm, name):
    S = x.shape[0]

    def body(x_ref, u_ref, w_ref, t_ref, dy_ref, l_ref, acc):
        i = pl.program_id(0)

        @pl.when(i == 0)
        def _():
            acc[...] = jnp.zeros_like(acc)
        y = x_ref[...] + jnp.dot(u_ref[...].astype(BF16), w_ref[...], preferred_element_type=F32)
        diff = y - t_ref[...]
        dy_ref[...] = diff / float(D_MODEL)
        acc[...] += _cs8(diff * diff)

        @pl.when(i == pl.num_programs(0) - 1)
        def _():
            l_ref[...] = jnp.sum(acc[...], axis=0, keepdims=True)

    return _call(
        body, name, (S // tm,),
        [pl.BlockSpec((tm, D_MODEL), lambda i: (i, 0)),
         pl.BlockSpec((tm, D_MODEL), lambda i: (i, 0)),
         pl.BlockSpec((D_MODEL, D_MODEL), lambda i: (0, 0)),
         pl.BlockSpec((tm, D_MODEL), lambda i: (i, 0))],
        [pl.BlockSpec((tm, D_MODEL), lambda i: (i, 0)),
         pl.BlockSpec((1, D_MODEL), lambda i: (0, 0))],
        [jax.ShapeDtypeStruct((S, D_MODEL), F32), jax.ShapeDtypeStruct((1, D_MODEL), F32)],
        scratch=[pltpu.VMEM((SUBLANES, D_MODEL), F32)], sem=("arbitrary",))(x, u, w, tgt)


def _mm_nt(a, w, tm, name):
    S, K = a.shape
    N = w.shape[0]

    def body(a_ref, w_ref, o_ref):
        o_ref[...] = lax.dot_general(a_ref[...].astype(BF16), w_ref[...], NT_DIMS, preferred_element_type=F32)

    return _call(
        body, name, (S // tm,),
        [pl.BlockSpec((tm, K), lambda i: (i, 0)), pl.BlockSpec((N, K), lambda i: (0, 0))],
        pl.BlockSpec((tm, N), lambda i: (i, 0)),
        jax.ShapeDtypeStruct((S, N), F32), sem=("parallel",))(a, w)


def _piece_blocks(pieces, tk, axis):
    starts, counts, s = [], [], 0
    for p in pieces:
        n = p.shape[axis] // tk
        starts.append(s)
        counts.append(n)
        s += n
    return starts, counts, s


def _mm_nt_rms(pieces, w, x, nw, dres, tm, name):
    S = x.shape[0]
    npc = len(pieces)
    ni = S // tm
    offs = np.cumsum([0] + [p.shape[1] for p in pieces]).tolist()

    def body(*refs):
        p_refs = refs[:npc]
        w_hbm, x_ref, nw_ref, dr_ref, dx_ref, dnw_ref, w_v, sem, nacc = refs[npc:]
        i = pl.program_id(0)
        _load_once(w_hbm, w_v, sem)

        @pl.when(i == 0)
        def _():
            nacc[...] = jnp.zeros_like(nacc)

        dh = None
        for p in range(npc):
            part = lax.dot_general(p_refs[p][...].astype(BF16), w_v[:, offs[p]:offs[p + 1]], NT_DIMS,
                                   preferred_element_type=F32)
            dh = part if dh is None else dh + part
        xv = x_ref[...]
        rs = lax.rsqrt(jnp.mean(xv * xv, axis=-1, keepdims=True) + EPS)
        xh = xv * rs
        nacc[...] += _cs8(dh * xh)
        dxh = dh * nw_ref[...]
        dx_ref[...] = dr_ref[...] + rs * (dxh - xh * jnp.mean(dxh * xh, axis=-1, keepdims=True))

        @pl.when(i == ni - 1)
        def _():
            dnw_ref[...] = jnp.sum(nacc[...], axis=0, keepdims=True)

    row = pl.BlockSpec((tm, D_MODEL), lambda i: (i, 0))
    return _call(
        body, name, (ni,),
        [pl.BlockSpec((tm, p.shape[1]), lambda i: (i, 0)) for p in pieces] +
        [ANY, row, pl.BlockSpec((1, D_MODEL), lambda i: (0, 0)), row],
        [row, pl.BlockSpec((1, D_MODEL), lambda i: (0, 0))],
        [jax.ShapeDtypeStruct((S, D_MODEL), F32), jax.ShapeDtypeStruct((1, D_MODEL), F32)],
        scratch=[pltpu.VMEM(w.shape, BF16), pltpu.SemaphoreType.DMA(()), pltpu.VMEM((SUBLANES, D_MODEL), F32)],
        sem=("arbitrary",))(*pieces, w, x, nw, dres)


def _mm_wgrad(at, pieces, tn, name):
    M, S = at.shape
    starts, counts, nj = _piece_blocks(pieces, tn, 1)
    npc = len(pieces)

    def body(*refs):
        a_hbm = refs[0]
        p_refs = refs[1:1 + npc]
        o_ref, a_v, sem = refs[1 + npc:]
        j = pl.program_id(0)
        _load_once(a_hbm, a_v, sem)
        for p in range(npc):
            @pl.when((j >= starts[p]) & (j < starts[p] + counts[p]))
            def _(p=p):
                o_ref[...] = jnp.dot(a_v[...], p_refs[p][...].astype(BF16), preferred_element_type=F32)

    def pspec(p):
        return pl.BlockSpec((S, tn), lambda j: (0, jnp.clip(j - starts[p], 0, counts[p] - 1)))

    return _call(
        body, name, (nj,),
        [ANY] + [pspec(p) for p in range(npc)],
        pl.BlockSpec((M, tn), lambda j: (0, j)),
        jax.ShapeDtypeStruct((M, nj * tn), F32),
        scratch=[pltpu.VMEM(at.shape, BF16), pltpu.SemaphoreType.DMA(())], sem=("arbitrary",))(at, *pieces)


def _stream_spec(d, T):
    return pl.BlockSpec((d, T // d, A_WIDTH), lambda i: (0, i, 0))


def _stream_shape(d, S, dtype):
    return jax.ShapeDtypeStruct((d, S // d, A_WIDTH), dtype)


N_CHUNK = A_WIDTH // LANES


def _to_tokens(ref, scr, d, T):
    if d == 1:
        return ref[0].astype(F32)
    for r in range(d):
        for ch in range(N_CHUNK):
            scr.at[ch][pl.ds(r, T // d, stride=d), :] = ref[r, :, ch * LANES:(ch + 1) * LANES].astype(F32)
    return _get(scr)


def _from_tokens(out_ref, scr, d, T):
    for r in range(d):
        for ch in range(N_CHUNK):
            out_ref[r, :, ch * LANES:(ch + 1) * LANES] = scr.at[ch][pl.ds(r, T // d, stride=d), :].astype(out_ref.dtype)


def _put(scr, val):
    for ch in range(N_CHUNK):
        scr[ch] = val[:, ch * LANES:(ch + 1) * LANES]


def _get(scr):
    return jnp.concatenate([scr[ch] for ch in range(N_CHUNK)], axis=1)


def _chunked(T):
    return pltpu.VMEM((N_CHUNK, T, LANES), F32)


def _qkv_prep(proj, pos, freq, wq, wk, T):
    S = proj.shape[0]
    qk_w = 3 * A_WIDTH

    def body(q_ref, k_ref, v_ref, pos_ref, f_ref, wq_ref, wk_ref, *rest):
        outs, scr = rest[:9], rest[9]
        ones = _seg_ones()
        c, s1, s2 = _rope_tables(pos_ref, f_ref)
        for t, (src, w_ref) in enumerate(((q_ref, wq_ref), (k_ref, wk_ref), (v_ref, None))):
            for g in range(3):
                d = A_GROUPS[g][1]
                out = outs[3 * t + g]
                for ch in range(A_WIDTH // LANES):
                    cs = slice(ch * LANES, (ch + 1) * LANES)
                    v = src[:, g * A_WIDTH + ch * LANES: g * A_WIDTH + (ch + 1) * LANES]
                    if w_ref is not None:
                        rs = lax.rsqrt(_segsum(v * v, ones) * (1.0 / HEAD_DIM) + EPS)
                        y = v * rs * w_ref[...]
                        v = y * c + pltpu.roll(y, 8, 1) * s1 + pltpu.roll(y, LANES - 8, 1) * s2
                    if d == 1:
                        out[0, :, cs] = v.astype(BF16)
                    else:
                        scr[ch] = v
                if d > 1:
                    _from_tokens(out, scr, d, T)

    ds_ = [A_GROUPS[g][1] for g in range(3)] * 3
    return _call(
        body, "qkv_prep", (S // T,),
        [pl.BlockSpec((T, qk_w), lambda i: (i, 0)), pl.BlockSpec((T, qk_w), lambda i: (i, 1)),
         pl.BlockSpec((T, qk_w), lambda i: (i, 2)),
         pl.BlockSpec((T, 1), lambda i: (i, 0)), pl.BlockSpec((1, LANES), lambda i: (0, 0)),
         pl.BlockSpec((1, LANES), lambda i: (0, 0)), pl.BlockSpec((1, LANES), lambda i: (0, 0))],
        [_stream_spec(d, T) for d in ds_],
        [_stream_shape(d, S, BF16) for d in ds_],
        scratch=[_chunked(T)], sem=("parallel",))(proj, proj, proj, pos, freq, wq, wk)


def _attn_mask(i):
    qi = lax.broadcasted_iota(jnp.int32, (QBLK, 2 * QBLK), 0) + QBLK
    kj = lax.broadcasted_iota(jnp.int32, (QBLK, 2 * QBLK), 1)
    dist = qi - kj
    return (dist >= 0) & (dist <= QBLK) & ((i > 0) | (kj >= QBLK))


ATT_BLK = (None, QBLK, A_WIDTH)


def _first_head_lanes():
    return lax.broadcasted_iota(jnp.int32, (1, LANES), 1) < HEAD_DIM


def _split_heads(v, first):
    zero = jnp.zeros_like(v)
    return jnp.where(first, v, zero), jnp.where(first, zero, v)


def _attn_fwd(q, k, v, g):
    d, n, _ = q.shape
    nb = n // QBLK

    def body(q_ref, kp_ref, kc_ref, vp_ref, vc_ref, o_ref, l_ref, s_scr, p_scr):
        i = pl.program_id(1)
        mask = _attn_mask(i)
        first = _first_head_lanes()
        for pr in range(A_HEADS // 2):
            ps = slice(pr * LANES, (pr + 1) * LANES)
            kc = jnp.concatenate([kp_ref[:, ps], kc_ref[:, ps]], axis=0)
            for e, qh in enumerate(_split_heads(q_ref[:, ps], first)):
                s_scr[2 * pr + e] = lax.dot_general(qh, kc, NT_DIMS, preferred_element_type=F32)
        for pr in range(A_HEADS // 2):
            lses = []
            for e in range(2):
                s = jnp.where(mask, s_scr[2 * pr + e] * (HEAD_DIM ** -0.5), NEG)
                m = jnp.max(s, axis=-1, keepdims=True)
                p = jnp.exp(s - m)
                den = jnp.sum(p, axis=-1, keepdims=True)
                p_scr[2 * pr + e] = (p / den).astype(BF16)
                lses.append(m + jnp.log(den))
            l_ref[:, pr * LANES:(pr + 1) * LANES] = jnp.where(first, lses[0], lses[1])
        for pr in range(A_HEADS // 2):
            ps = slice(pr * LANES, (pr + 1) * LANES)
            va, vb = _split_heads(jnp.concatenate([vp_ref[:, ps], vc_ref[:, ps]], axis=0), first)
            o_ref[:, ps] = (jnp.dot(p_scr[2 * pr], va, preferred_element_type=F32) +
                            jnp.dot(p_scr[2 * pr + 1], vb, preferred_element_type=F32))

    prev = lambda r, i: (r, jnp.maximum(i - 1, 0), 0)
    cur = lambda r, i: (r, i, 0)
    return _call(
        body, "attn_fwd_g%d" % g, (d, nb),
        [pl.BlockSpec(ATT_BLK, cur), pl.BlockSpec(ATT_BLK, prev), pl.BlockSpec(ATT_BLK, cur),
         pl.BlockSpec(ATT_BLK, prev), pl.BlockSpec(ATT_BLK, cur)],
        [pl.BlockSpec(ATT_BLK, cur), pl.BlockSpec(ATT_BLK, cur)],
        [jax.ShapeDtypeStruct((d, n, A_WIDTH), F32)] * 2,
        scratch=[pltpu.VMEM((A_HEADS, QBLK, 2 * QBLK), F32), pltpu.VMEM((A_HEADS, QBLK, 2 * QBLK), BF16)],
        sem=("parallel", "parallel"))(q, k, k, v, v)


def _attn_bwd(q, k, v, do, lse, cg, g):
    d, n, _ = q.shape
    nb = n // QBLK
    scale = HEAD_DIM ** -0.5

    def body(q_ref, kp_ref, kc_ref, vp_ref, vc_ref, do_ref, l_ref, c_ref, dq_ref, dk_ref, dv_ref, ck, cv,
             s_scr, dp_scr, p_scr, ds_scr):
        i = pl.program_id(1)

        @pl.when(i == 0)
        def _():
            ck[...] = jnp.zeros_like(ck)
            cv[...] = jnp.zeros_like(cv)

        @pl.when(i < nb)
        def _():
            mask = _attn_mask(i)
            first = _first_head_lanes()
            for pr in range(A_HEADS // 2):
                ps = slice(pr * LANES, (pr + 1) * LANES)
                kc = jnp.concatenate([kp_ref[:, ps], kc_ref[:, ps]], axis=0)
                vc = jnp.concatenate([vp_ref[:, ps], vc_ref[:, ps]], axis=0)
                qs = _split_heads(q_ref[:, ps], first)
                dos = _split_heads(do_ref[:, ps], first)
                for e in range(2):
                    s_scr[2 * pr + e] = lax.dot_general(qs[e], kc, NT_DIMS, preferred_element_type=F32)
                    dp_scr[2 * pr + e] = lax.dot_general(dos[e], vc, NT_DIMS, preferred_element_type=F32)
            for h in range(A_HEADS):
                lane = h * HEAD_DIM
                p = jnp.where(mask, jnp.exp(s_scr[h] * scale - l_ref[:, lane:lane + 1]), 0.0)
                p_scr[h] = p.astype(BF16)
                ds_scr[h] = (p * (dp_scr[h] + c_ref[:, lane:lane + 1]) * scale).astype(BF16)
            for pr in range(A_HEADS // 2):
                ps = slice(pr * LANES, (pr + 1) * LANES)
                ks = _split_heads(jnp.concatenate([kp_ref[:, ps], kc_ref[:, ps]], axis=0), first)
                qs = _split_heads(q_ref[:, ps], first)
                dos = _split_heads(do_ref[:, ps], first)
                dq = dkc = dvc = None
                for e in range(2):
                    ds = ds_scr[2 * pr + e]
                    a = jnp.dot(ds, ks[e], preferred_element_type=F32)
                    b = lax.dot_general(ds, qs[e], TN_DIMS, preferred_element_type=F32)
                    c = lax.dot_general(p_scr[2 * pr + e], dos[e], TN_DIMS, preferred_element_type=F32)
                    dq, dkc, dvc = (a, b, c) if e == 0 else (dq + a, dkc + b, dvc + c)
                dq_ref[:, ps] = dq
                dk_ref[:, ps] = ck[:, ps] + dkc[:QBLK]
                dv_ref[:, ps] = cv[:, ps] + dvc[:QBLK]
                ck[:, ps] = dkc[QBLK:]
                cv[:, ps] = dvc[QBLK:]

        @pl.when(i == nb)
        def _():
            dk_ref[...] = ck[...]
            dv_ref[...] = cv[...]

    qi = lambda i: jnp.minimum(i, nb - 1)
    cur = lambda r, i: (r, qi(i), 0)
    prev = lambda r, i: (r, jnp.maximum(qi(i) - 1, 0), 0)
    late = lambda r, i: (r, jnp.maximum(i - 1, 0), 0)
    return _call(
        body, "attn_bwd_g%d" % g, (d, nb + 1),
        [pl.BlockSpec(ATT_BLK, cur), pl.BlockSpec(ATT_BLK, prev), pl.BlockSpec(ATT_BLK, cur),
         pl.BlockSpec(ATT_BLK, prev), pl.BlockSpec(ATT_BLK, cur),
         pl.BlockSpec(ATT_BLK, cur), pl.BlockSpec(ATT_BLK, cur), pl.BlockSpec(ATT_BLK, cur)],
        [pl.BlockSpec(ATT_BLK, cur), pl.BlockSpec(ATT_BLK, late), pl.BlockSpec(ATT_BLK, late)],
        [jax.ShapeDtypeStruct((d, n, A_WIDTH), F32)] * 3,
        scratch=[pltpu.VMEM((QBLK, A_WIDTH), F32), pltpu.VMEM((QBLK, A_WIDTH), F32),
                 pltpu.VMEM((A_HEADS, QBLK, 2 * QBLK), F32), pltpu.VMEM((A_HEADS, QBLK, 2 * QBLK), F32),
                 pltpu.VMEM((A_HEADS, QBLK, 2 * QBLK), BF16), pltpu.VMEM((A_HEADS, QBLK, 2 * QBLK), BF16)],
        sem=("parallel", "arbitrary"))(q, k, k, v, v, do, lse, cg)


def _merge_weights(l0, l1, l2):
    mx = jnp.maximum(jnp.maximum(l0, l1), l2)
    e0, e1, e2 = jnp.exp(l0 - mx), jnp.exp(l1 - mx), jnp.exp(l2 - mx)
    den = e0 + e1 + e2
    return e0 / den, e1 / den, e2 / den


def _even_specs(T, S):
    t8 = T // SUBLANES
    last8 = S // SUBLANES - 1
    col = lambda c: pl.BlockSpec((T, A_WIDTH), lambda i: (i, c))
    prev8 = lambda c: pl.BlockSpec((SUBLANES, A_WIDTH), lambda i: (jnp.maximum(i * t8 - 1, 0), c))
    next8 = lambda c: pl.BlockSpec((SUBLANES, A_WIDTH), lambda i: (jnp.minimum((i + 1) * t8, last8), c))
    return col, prev8, next8


GROUP_D = tuple(d for _, d in A_GROUPS)


def _even_mixer_fwd(proj, os_, ls_, conv_w, T):
    S = proj.shape[0]
    col, prev8, _ = _even_specs(T, S)
    H = SUBLANES

    def body(bg_r, cg_r, hb_r, zl_r, zh_r, cgp_r, hbp_r, o0, o1, o2, l0, l1, l2, cw_r, u_ref, ut_ref, ext, *scr):
        i = pl.program_id(0)
        ls = [_to_tokens(r, scr[g], GROUP_D[g], T) for g, r in enumerate((l0, l1, l2))]
        ws = _merge_weights(*ls)
        oa = ws[0] * _to_tokens(o0, scr[0], GROUP_D[0], T)
        oa = oa + ws[1] * _to_tokens(o1, scr[1], GROUP_D[1], T)
        oa = oa + ws[2] * _to_tokens(o2, scr[2], GROUP_D[2], T)
        ext[0:H, :] = jnp.where(i == 0, 0.0, cgp_r[...] * hbp_r[...])
        ext[H:H + T, :] = cg_r[...] * hb_r[...]
        conv = cw_r[0:1, :] * ext[H - 2:H - 2 + T, :]
        for kk in range(1, SC_WIDTH):
            conv = conv + cw_r[kk:kk + 1, :] * ext[H - 2 + kk:H - 2 + kk + T, :]
        zl, zh = zl_r[...], zh_r[...]
        _emit_u(u_ref, ut_ref, oa * (zl * _sig(zl)), bg_r[...] * conv * (zh * _sig(zh)))

    streams = [_stream_spec(d, T) for d in GROUP_D]
    return _call(
        body, "even_mixer_fwd", (S // T,),
        [col(9), col(10), col(11), col(12), col(13), prev8(10), prev8(11)] + streams + streams +
        [pl.BlockSpec((SC_WIDTH, A_WIDTH), lambda i: (0, 0))],
        [pl.BlockSpec((T, D_MODEL), lambda i: (i, 0)), pl.BlockSpec((D_MODEL, T), lambda i: (0, i))],
        [jax.ShapeDtypeStruct((S, D_MODEL), BF16), jax.ShapeDtypeStruct((D_MODEL, S), BF16)],
        scratch=[pltpu.VMEM((T + H, A_WIDTH), F32)] + [_chunked(T)] * 3, sem=("parallel",))(
            proj, proj, proj, proj, proj, proj, proj, *os_, *ls_, conv_w)


def _even_mixer_bwd(du, proj, os_, ls_, conv_w, T):
    S = proj.shape[0]
    nt = S // T
    col, prev8, next8 = _even_specs(T, S)
    H = SUBLANES
    t8 = T // SUBLANES
    last8 = S // SUBLANES - 1

    def body(dul_r, duh_r, bg_r, cg_r, hb_r, zl_r, zh_r, cgp_r, hbp_r, dun_r, zhn_r, bgn_r,
             o0, o1, o2, l0, l1, l2, cw_r,
             do0, do1, do2, c0, c1, c2, dr_ref, dcw_ref, ext_t, ext_d, acc, s_a, s_b, s_c, *ws_scr):
        i = pl.program_id(0)

        @pl.when(i == 0)
        def _():
            acc[...] = jnp.zeros_like(acc)

        ones = _seg_ones()
        zl, zh = zl_r[...], zh_r[...]
        sl, sh = _sig(zl), _sig(zh)
        dul, duh = dul_r[...], duh_r[...]
        scr = (s_a, s_b, s_c)
        ls = [_to_tokens(r, scr[g], GROUP_D[g], T) for g, r in enumerate((l0, l1, l2))]
        ws = _merge_weights(*ls)
        for g in range(3):
            _put(ws_scr[g], ws[g])
        oa = ws[0] * _to_tokens(o0, scr[0], GROUP_D[0], T)
        oa = oa + ws[1] * _to_tokens(o1, scr[1], GROUP_D[1], T)
        oa = oa + ws[2] * _to_tokens(o2, scr[2], GROUP_D[2], T)
        doa = dul * (zl * sl)
        _put(s_a, doa)
        prod = doa * oa
        for ch in range(N_CHUNK):
            s_b[ch] = _segsum(prod[:, ch * LANES:(ch + 1) * LANES], ones)
        for g, (do_ref, c_ref) in enumerate(((do0, c0), (do1, c1), (do2, c2))):
            d = GROUP_D[g]
            if d == 1:
                for ch in range(N_CHUNK):
                    cs = slice(ch * LANES, (ch + 1) * LANES)
                    do_ref[0, :, cs] = (ws_scr[g][ch] * s_a[ch]).astype(BF16)
                    c_ref[0, :, cs] = -ws_scr[g][ch] * s_b[ch]
            else:
                for ch in range(N_CHUNK):
                    s_c[ch] = ws_scr[g][ch] * s_a[ch]
                _from_tokens(do_ref, s_c, d, T)
                for ch in range(N_CHUNK):
                    s_c[ch] = -ws_scr[g][ch] * s_b[ch]
                _from_tokens(c_ref, s_c, d, T)
        cgv, hbv, bgv = cg_r[...], hb_r[...], bg_r[...]
        ext_t[0:H, :] = jnp.where(i == 0, 0.0, cgp_r[...] * hbp_r[...])
        ext_t[H:H + T, :] = cgv * hbv
        conv = cw_r[0:1, :] * ext_t[H - 2:H - 2 + T, :]
        for kk in range(1, SC_WIDTH):
            conv = conv + cw_r[kk:kk + 1, :] * ext_t[H - 2 + kk:H - 2 + kk + T, :]
        dyb = duh * (zh * sh)
        dconv = dyb * bgv
        zn = zhn_r[...]
        ext_d[0:T, :] = dconv
        ext_d[T:T + H, :] = jnp.where(i == nt - 1, 0.0, dun_r[...] * (zn * _sig(zn)) * bgn_r[...])
        dt = cw_r[0:1, :] * ext_d[2:2 + T, :]
        for kk in range(1, SC_WIDTH):
            dt = dt + cw_r[kk:kk + 1, :] * ext_d[2 - kk:2 - kk + T, :]
        for kk in range(SC_WIDTH):
            acc[kk * SUBLANES:(kk + 1) * SUBLANES, :] += _cs8(dconv * ext_t[H - 2 + kk:H - 2 + kk + T, :])
        dr_ref[:, 0:A_WIDTH] = (dyb * conv).astype(BF16)
        dr_ref[:, A_WIDTH:2 * A_WIDTH] = (dt * hbv).astype(BF16)
        dr_ref[:, 2 * A_WIDTH:3 * A_WIDTH] = (dt * cgv).astype(BF16)
        dr_ref[:, 3 * A_WIDTH:4 * A_WIDTH] = (dul * oa * _dsilu(zl, sl)).astype(BF16)
        dr_ref[:, 4 * A_WIDTH:5 * A_WIDTH] = (duh * (bgv * conv) * _dsilu(zh, sh)).astype(BF16)

        @pl.when(i == nt - 1)
        def _():
            for kk in range(SC_WIDTH):
                dcw_ref[kk:kk + 1, :] = jnp.sum(acc[kk * SUBLANES:(kk + 1) * SUBLANES, :], axis=0, keepdims=True)

    streams = [_stream_spec(d, T) for d in GROUP_D]
    dunext = pl.BlockSpec((SUBLANES, A_WIDTH), lambda i: (jnp.minimum((i + 1) * t8, last8), 1))
    tile = _chunked(T)
    outs = _call(
        body, "even_mixer_bwd", (nt,),
        [pl.BlockSpec((T, A_WIDTH), lambda i: (i, 0)), pl.BlockSpec((T, A_WIDTH), lambda i: (i, 1)),
         col(9), col(10), col(11), col(12), col(13), prev8(10), prev8(11), dunext, next8(13), next8(9)] +
        streams + streams + [pl.BlockSpec((SC_WIDTH, A_WIDTH), lambda i: (0, 0))],
        streams + streams + [pl.BlockSpec((T, 5 * A_WIDTH), lambda i: (i, 0)),
                             pl.BlockSpec((SC_WIDTH, A_WIDTH), lambda i: (0, 0))],
        [_stream_shape(d, S, BF16) for d in GROUP_D] + [_stream_shape(d, S, F32) for d in GROUP_D] +
        [jax.ShapeDtypeStruct((S, 5 * A_WIDTH), BF16), jax.ShapeDtypeStruct((SC_WIDTH, A_WIDTH), F32)],
        scratch=[pltpu.VMEM((T + H, A_WIDTH), F32), pltpu.VMEM((T + H, A_WIDTH), F32),
                 pltpu.VMEM((SC_WIDTH * SUBLANES, A_WIDTH), F32)] + [tile] * 6,
        sem=("arbitrary",))(du, du, proj, proj, proj, proj, proj, proj, proj, du, proj, proj, *os_, *ls_, conv_w)
    return outs[0:3], outs[3:6], outs[6], outs[7]


def _qk_bwd(proj, dqs, dks, dvs, pos, freq, wq, wk, T):
    S = proj.shape[0]
    nt = S // T
    qk_w = 3 * A_WIDTH

    def body(q_ref, k_ref, dq0, dq1, dq2, dk0, dk1, dk2, dv0, dv1, dv2, pos_ref, f_ref, wq_ref, wk_ref,
             o_ref, dw_ref, acc, scr):
        i = pl.program_id(0)

        @pl.when(i == 0)
        def _():
            acc[...] = jnp.zeros_like(acc)
            dw_ref[...] = jnp.zeros_like(dw_ref)

        ones = _seg_ones()
        c, s1, s2 = _rope_tables(pos_ref, f_ref)
        for t, (src, w_ref, ds) in enumerate(((q_ref, wq_ref, (dq0, dq1, dq2)), (k_ref, wk_ref, (dk0, dk1, dk2)))):
            wv = w_ref[...]
            for g in range(3):
                d = GROUP_D[g]
                if d > 1:
                    _to_tokens(ds[g], scr, d, T)
                for ch in range(A_WIDTH // LANES):
                    cs = slice(g * A_WIDTH + ch * LANES, g * A_WIDTH + (ch + 1) * LANES)
                    lc = slice(ch * LANES, (ch + 1) * LANES)
                    v = src[:, cs]
                    dout = ds[g][0, :, lc] if d == 1 else scr[ch]
                    rs = lax.rsqrt(_segsum(v * v, ones) * (1.0 / HEAD_DIM) + EPS)
                    xh = v * rs
                    dy = dout * c + pltpu.roll(dout * s1, LANES - 8, 1) + pltpu.roll(dout * s2, 8, 1)
                    acc[t * SUBLANES:(t + 1) * SUBLANES, :] += _cs8(dy * xh)
                    dxh = dy * wv
                    mean = _segsum(dxh * xh, ones) * (1.0 / HEAD_DIM)
                    o_ref[:, t * qk_w + g * A_WIDTH + ch * LANES: t * qk_w + g * A_WIDTH + (ch + 1) * LANES] = (
                        rs * (dxh - xh * mean)).astype(BF16)
        for g, dv in enumerate((dv0, dv1, dv2)):
            d = GROUP_D[g]
            base = 2 * qk_w + g * A_WIDTH
            o_ref[:, base:base + A_WIDTH] = _to_tokens(dv, scr, d, T).astype(BF16)

        @pl.when(i == nt - 1)
        def _():
            for t in range(2):
                srow = jnp.sum(acc[t * SUBLANES:(t + 1) * SUBLANES, :], axis=0, keepdims=True)
                dw_ref[t:t + 1, :] = srow + pltpu.roll(srow, HEAD_DIM, 1)

    streams = [_stream_spec(d, T) for d in GROUP_D]
    return _call(
        body, "qk_bwd", (nt,),
        [pl.BlockSpec((T, qk_w), lambda i: (i, 0)), pl.BlockSpec((T, qk_w), lambda i: (i, 1))] + streams * 3 +
        [pl.BlockSpec((T, 1), lambda i: (i, 0)), pl.BlockSpec((1, LANES), lambda i: (0, 0)),
         pl.BlockSpec((1, LANES), lambda i: (0, 0)), pl.BlockSpec((1, LANES), lambda i: (0, 0))],
        [pl.BlockSpec((T, 3 * qk_w), lambda i: (i, 0)), pl.BlockSpec((SUBLANES, LANES), lambda i: (0, 0))],
        [jax.ShapeDtypeStruct((S, 3 * qk_w), BF16), jax.ShapeDtypeStruct((SUBLANES, LANES), F32)],
        scratch=[pltpu.VMEM((2 * SUBLANES, LANES), F32), _chunked(T)], sem=("arbitrary",))(
            proj, proj, *dqs, *dks, *dvs, pos, freq, wq, wk)


N_SMALL_ODD = 40
SHIFT_ROWS_LESS = SUBLANES


def _fill_shifted(ext_ref, sh_ref):
    rows = ext_ref.shape[0] - SHIFT_ROWS_LESS
    for b in range(1, SUBLANES):
        sh_ref[b - 1] = ext_ref[b:b + rows, :]


def _window(ext_ref, sh_ref, off, T):
    a, b = divmod(off, SUBLANES)
    if b == 0:
        return ext_ref[off:off + T, :]
    return sh_ref[b - 1, a * SUBLANES:a * SUBLANES + T, :]


def _odd_pool_tile(i, uc_r, ucp_r, pw_r, ext_u, pooled_s, pm_s, T):
    H = HALO
    uc = uc_r[...]
    ext_u[0:H, :] = jnp.where(i == 0, 0.0, ucp_r[...])
    ext_u[H:H + T, :] = uc
    row = i * T + lax.broadcasted_iota(jnp.int32, (T, 1), 0)
    for g, p in enumerate(POOL_SIZES):
        cs = slice(g * LANES, (g + 1) * LANES)
        win = ext_u[H:H + T, cs]
        for j in range(1, p):
            win = win + ext_u[H - j:H - j + T, cs]
        cnt = jnp.minimum(row + 1, p).astype(F32)
        pooled = win / cnt - uc[:, cs]
        pooled_s[:, cs] = pooled
        pm_s[:, cs] = jnp.dot(pooled.astype(BF16), pw_r[g].astype(BF16), preferred_element_type=F32)
    return row


def _odd_glu_tile(i, da_r, dg_r, dap_r, dgp_r, ext_g, sh_g, T):
    H = HALO
    ext_g[0:H, :] = jnp.where(i == 0, 0.0, dap_r[...] * _sig(dgp_r[...]))
    ext_g[H:H + T, :] = da_r[...] * _sig(dg_r[...])
    _fill_shifted(ext_g, sh_g)


def _odd_specs(T, S, order):
    tb = T // HALO
    col = lambda c: pl.BlockSpec((T, A_WIDTH), lambda s: (order(s), c))
    prev = lambda c: pl.BlockSpec((HALO, A_WIDTH), lambda s: (jnp.maximum(order(s) * tb - 1, 0), c))
    const2 = lambda shape: pl.BlockSpec(shape, lambda s: (0, 0))
    weights = [pl.BlockSpec((4, LANES, LANES), lambda s: (0, 0, 0)), const2((1, A_WIDTH)),
               const2((D_CONV, A_WIDTH)), const2((1, A_WIDTH)), const2((1, A_WIDTH)), const2((1, A_WIDTH))]
    return col, prev, weights


def _odd_mixer_fwd(proj, pool_w, scale, dconv_w, dconv_b, ln_w, ln_b, T):
    S = proj.shape[0]
    col, prev, wspecs = _odd_specs(T, S, lambda s: s)
    H = HALO

    def body(uc_r, da_r, dg_r, zl_r, zh_r, ucp_r, dap_r, dgp_r, pw_r, sc_r, dw_r, db_r, lw_r, lb_r,
             u_ref, ut_ref, cv_ref, ext_u, ext_g, sh_g, pooled_s, pm_s):
        i = pl.program_id(0)
        _odd_pool_tile(i, uc_r, ucp_r, pw_r, ext_u, pooled_s, pm_s, T)
        _odd_glu_tile(i, da_r, dg_r, dap_r, dgp_r, ext_g, sh_g, T)
        base = H - (D_CONV - 1)
        conv = db_r[...] + dw_r[0:1, :] * _window(ext_g, sh_g, base, T)
        for kk in range(1, D_CONV):
            conv = conv + dw_r[kk:kk + 1, :] * _window(ext_g, sh_g, base + kk, T)
        cv_ref[...] = conv
        mu = jnp.mean(conv, axis=-1, keepdims=True)
        xc = conv - mu
        yh = xc * lax.rsqrt(jnp.mean(xc * xc, axis=-1, keepdims=True) + EPS)
        ln = yh * lw_r[...] + lb_r[...]
        zl, zh = zl_r[...], zh_r[...]
        _emit_u(u_ref, ut_ref, pm_s[...] * sc_r[...] * (zl * _sig(zl)), ln * _sig(ln) * (zh * _sig(zh)))

    return _call(
        body, "odd_mixer_fwd", (S // T,),
        [col(0), col(1), col(2), col(3), col(4), prev(0), prev(1), prev(2)] + wspecs,
        [pl.BlockSpec((T, D_MODEL), lambda i: (i, 0)), pl.BlockSpec((D_MODEL, T), lambda i: (0, i)),
         pl.BlockSpec((T, A_WIDTH), lambda i: (i, 0))],
        [jax.ShapeDtypeStruct((S, D_MODEL), BF16), jax.ShapeDtypeStruct((D_MODEL, S), BF16),
         jax.ShapeDtypeStruct((S, A_WIDTH), F32)],
        scratch=[pltpu.VMEM((T + H, A_WIDTH), F32), pltpu.VMEM((T + H, A_WIDTH), F32),
                 pltpu.VMEM((SUBLANES - 1, T + H - SHIFT_ROWS_LESS, A_WIDTH), F32),
                 pltpu.VMEM((T, A_WIDTH), F32), pltpu.VMEM((T, A_WIDTH), F32)],
        sem=("parallel",))(proj, proj, proj, proj, proj, proj, proj, proj,
                           pool_w, scale, dconv_w, dconv_b, ln_w, ln_b)


def _odd_mixer_bwd(du, proj, conv, pool_w, scale, dconv_w, dconv_b, ln_w, ln_b, T):
    S = proj.shape[0]
    nt = S // T
    order = lambda s: nt - 1 - s
    col, prev, wspecs = _odd_specs(T, S, order)
    H = HALO

    def body(dul_r, duh_r, cv_r, uc_r, da_r, dg_r, zl_r, zh_r, ucp_r, dap_r, dgp_r, pw_r, sc_r, dw_r, db_r, lw_r, lb_r,
             dp_ref, dpw_ref, sm_ref, ext_u, ext_g, sh_g, pooled_s, pm_s, dpl_s, ext_p, ext_c, sh_c, acc):
        step = pl.program_id(0)
        i = nt - 1 - step

        @pl.when(step == 0)
        def _():
            ext_p[T:T + H, :] = jnp.zeros((H, A_WIDTH), F32)
            ext_c[T:T + H, :] = jnp.zeros((H, A_WIDTH), F32)
            acc[...] = jnp.zeros_like(acc)
            dpw_ref[...] = jnp.zeros_like(dpw_ref)

        def accum(r, v):
            acc[r * SUBLANES:(r + 1) * SUBLANES, :] += _cs8(v)

        row = _odd_pool_tile(i, uc_r, ucp_r, pw_r, ext_u, pooled_s, pm_s, T)
        _odd_glu_tile(i, da_r, dg_r, dap_r, dgp_r, ext_g, sh_g, T)
        conv = cv_r[...]
        mu = jnp.mean(conv, axis=-1, keepdims=True)
        xc = conv - mu
        rstd = lax.rsqrt(jnp.mean(xc * xc, axis=-1, keepdims=True) + EPS)
        yh = xc * rstd
        ln = yh * lw_r[...] + lb_r[...]
        sln = _sig(ln)
        zl, zh = zl_r[...], zh_r[...]
        sl, sh = _sig(zl), _sig(zh)
        dul, duh = dul_r[...], duh_r[...]
        pm = pm_s[...]
        scv = sc_r[...]
        dyc = dul * (zl * sl)
        accum(34, dyc * pm)
        dpm = dyc * scv
        for g in range(len(POOL_SIZES)):
            cs = slice(g * LANES, (g + 1) * LANES)
            dpm_g = dpm[:, cs].astype(BF16)
            dpw_ref[g] += lax.dot_general(pooled_s[:, cs].astype(BF16), dpm_g, TN_DIMS, preferred_element_type=F32)
            dpl_s[:, cs] = lax.dot_general(dpm_g, pw_r[g].astype(BF16), NT_DIMS, preferred_element_type=F32)
        lane_p = lax.broadcasted_iota(jnp.int32, (1, A_WIDTH), 1) // LANES
        pvec = jnp.left_shift(2, lane_p)
        cnt = jnp.minimum(row + 1, pvec).astype(F32)
        dpl = dpl_s[...]
        ext_p[0:T, :] = dpl / cnt
        for g, p in enumerate(POOL_SIZES):
            cs = slice(g * LANES, (g + 1) * LANES)
            win = ext_p[0:T, cs]
            for j in range(1, p):
                win = win + ext_p[j:j + T, cs]
            dp_ref[:, cs] = (win - dpl[:, cs]).astype(BF16)
        ext_p[T:T + H, :] = ext_p[0:H, :]
        dln = duh * (zh * sh) * _dsilu(ln, sln)
        accum(32, dln * yh)
        accum(33, dln)
        dyh = dln * lw_r[...]
        dc = rstd * (dyh - jnp.mean(dyh, axis=-1, keepdims=True) - yh * jnp.mean(dyh * yh, axis=-1, keepdims=True))
        accum(31, dc)
        ext_c[0:T, :] = dc
        _fill_shifted(ext_c, sh_c)
        base = H - (D_CONV - 1)
        dgl = dw_r[0:1, :] * _window(ext_c, sh_c, D_CONV - 1, T)
        accum(0, dc * _window(ext_g, sh_g, base, T))
        for kk in range(1, D_CONV):
            dgl = dgl + dw_r[kk:kk + 1, :] * _window(ext_c, sh_c, D_CONV - 1 - kk, T)
            accum(kk, dc * _window(ext_g, sh_g, base + kk, T))
        ext_c[T:T + H, :] = ext_c[0:H, :]
        dav, dgv = da_r[...], dg_r[...]
        sg = _sig(dgv)
        dp_ref[:, A_WIDTH:2 * A_WIDTH] = (dgl * sg).astype(BF16)
        dp_ref[:, 2 * A_WIDTH:3 * A_WIDTH] = (dgl * dav * sg * (1.0 - sg)).astype(BF16)
        dp_ref[:, 3 * A_WIDTH:4 * A_WIDTH] = (dul * (pm * scv) * _dsilu(zl, sl)).astype(BF16)
        dp_ref[:, 4 * A_WIDTH:5 * A_WIDTH] = (duh * (ln * sln) * _dsilu(zh, sh)).astype(BF16)

        @pl.when(step == nt - 1)
        def _():
            for r in range(N_SMALL_ODD):
                sm_ref[r:r + 1, :] = jnp.sum(acc[r * SUBLANES:(r + 1) * SUBLANES, :], axis=0, keepdims=True)

    ext = pltpu.VMEM((T + H, A_WIDTH), F32)
    shifted = pltpu.VMEM((SUBLANES - 1, T + H - SHIFT_ROWS_LESS, A_WIDTH), F32)
    tile = pltpu.VMEM((T, A_WIDTH), F32)
    return _call(
        body, "odd_mixer_bwd", (nt,),
        [pl.BlockSpec((T, A_WIDTH), lambda s: (order(s), 0)), pl.BlockSpec((T, A_WIDTH), lambda s: (order(s), 1)),
         pl.BlockSpec((T, A_WIDTH), lambda s: (order(s), 0)),
         col(0), col(1), col(2), col(3), col(4), prev(0), prev(1), prev(2)] + wspecs,
        [pl.BlockSpec((T, ODD_IN), lambda s: (order(s), 0)),
         pl.BlockSpec((4, LANES, LANES), lambda s: (0, 0, 0)),
         pl.BlockSpec((N_SMALL_ODD, A_WIDTH), lambda s: (0, 0))],
        [jax.ShapeDtypeStruct((S, ODD_IN), BF16), jax.ShapeDtypeStruct((4, LANES, LANES), F32),
         jax.ShapeDtypeStruct((N_SMALL_ODD, A_WIDTH), F32)],
        scratch=[ext, ext, shifted, tile, tile, tile, ext, ext, shifted,
                 pltpu.VMEM((N_SMALL_ODD * SUBLANES, A_WIDTH), F32)],
        sem=("arbitrary",))(du, du, conv, proj, proj, proj, proj, proj, proj, proj, proj,
                            pool_w, scale, dconv_w, dconv_b, ln_w, ln_b)


TILE_SEQ = 256
TILE_M = 512
TILE_WG = 256


def _local_step(x, pos, tgt, wb, p):
    T = TILE_SEQ
    freq = _freq_table()
    wq = jnp.tile(p["e_q_norm_w"], (1, LANES // HEAD_DIM))
    wk = jnp.tile(p["e_k_norm_w"], (1, LANES // HEAD_DIM))

    proj_e, ht_e = _inproj(x, p["e_norm_w"], wb["e_w_in"], TILE_SEQ, 1792, "inproj_even")
    qkv = _qkv_prep(proj_e, pos, freq, wq, wk, T)
    qs, ks, vs = qkv[0:3], qkv[3:6], qkv[6:9]
    os_, ls_ = [], []
    for g in range(3):
        o, l = _attn_fwd(qs[g], ks[g], vs[g], g)
        os_.append(o)
        ls_.append(l)
    u_e, ut_e = _even_mixer_fwd(proj_e, os_, ls_, p["e_conv_w"], T)
    x1 = _outproj(x, u_e, wb["e_w_out"], TILE_M, "outproj_even")
    proj_o, ht_o = _inproj(x1, p["o_norm_w"], wb["o_w_in"], TILE_SEQ, 1280, "inproj_odd")
    odd_w = (p["o_pool_w"], p["o_pool_scale"], p["o_dconv_w"], p["o_dconv_b"], p["o_ln_w"], p["o_ln_b"])
    u_o, ut_o, conv_o = _odd_mixer_fwd(proj_o, *odd_w, T)
    dy, lsum = _outproj_loss(x1, u_o, wb["o_w_out"], tgt, TILE_M, "outproj_odd_loss")

    g = {}
    g["o_w_out"] = _mm_wgrad(ut_o, [dy], TILE_WG, "wgrad_o_out")
    du_o = _mm_nt(dy, wb["o_w_out"], TILE_M, "du_odd")
    dproj_o, g["o_pool_w"], small_o = _odd_mixer_bwd(du_o, proj_o, conv_o, *odd_w, T)
    g["o_w_in"] = _mm_wgrad(ht_o, [dproj_o], TILE_WG, "wgrad_o_in")
    dx1, g["o_norm_w"] = _mm_nt_rms([dproj_o], wb["o_w_in"], x1, p["o_norm_w"], dy, TILE_SEQ, "dx_odd")
    g["o_dconv_w"] = small_o[0:D_CONV]
    g["o_dconv_b"] = small_o[31:32]
    g["o_ln_w"] = small_o[32:33]
    g["o_ln_b"] = small_o[33:34]
    g["o_pool_scale"] = small_o[34:35]

    g["e_w_out"] = _mm_wgrad(ut_e, [dx1], TILE_WG, "wgrad_e_out")
    du_e = _mm_nt(dx1, wb["e_w_out"], TILE_M, "du_even")
    dos, cgs, drest, g["e_conv_w"] = _even_mixer_bwd(du_e, proj_e, os_, ls_, p["e_conv_w"], T)
    dqs, dks, dvs = [], [], []
    for gi in range(3):
        dq, dk, dv = _attn_bwd(qs[gi], ks[gi], vs[gi], dos[gi], ls_[gi], cgs[gi], gi)
        dqs.append(dq)
        dks.append(dk)
        dvs.append(dv)
    dqkv, dnw = _qk_bwd(proj_e, dqs, dks, dvs, pos, freq, wq, wk, T)
    g["e_q_norm_w"] = dnw[0:1, 0:HEAD_DIM]
    g["e_k_norm_w"] = dnw[1:2, 0:HEAD_DIM]
    pieces = [dqkv, drest]
    g["e_w_in"] = _mm_wgrad(ht_e, pieces, TILE_WG, "wgrad_e_in")
    dx, g["e_norm_w"] = _mm_nt_rms(pieces, wb["e_w_in"], x, p["e_norm_w"], dx1, TILE_SEQ, "dx_even")
    return lsum, dx, g


BIG = ("e_w_in", "e_w_out", "o_w_in", "o_w_out")
SHARD_AXIS = {"e_w_in": 1, "e_w_out": 0, "o_w_in": 1, "o_w_out": 0}
N_CHIPS = 4


def _place():
    x, y, c = lax.axis_index("x"), lax.axis_index("y"), lax.axis_index("c")
    chips = [(1 - x, y), (x, 1 - y), (1 - x, 1 - y)]
    return x, y, c, chips


def _block_of(ref, name, block):
    rows, cols = ref.shape
    if SHARD_AXIS[name] == 1:
        cw = cols // N_CHIPS
        return ref.at[:, pl.ds(pl.multiple_of(block * cw, LANES), cw)]
    rw = rows // N_CHIPS
    return ref.at[pl.ds(pl.multiple_of(block * rw, rw), rw), :]


def _half_of(ref, name, half):
    rows, cols = ref.shape
    if SHARD_AXIS[name] == 1:
        return ref.at[pl.ds(pl.multiple_of(half * (rows // 2), rows // 2), rows // 2), :]
    return ref.at[:, pl.ds(pl.multiple_of(half * (cols // 2), LANES), cols // 2)]


def _sub(ref, name, block, half):
    rows, cols = ref.shape
    if SHARD_AXIS[name] == 1:
        cw, hr = cols // N_CHIPS, rows // 2
        return ref.at[pl.ds(pl.multiple_of(half * hr, hr), hr), pl.ds(pl.multiple_of(block * cw, LANES), cw)]
    rw, hc = rows // N_CHIPS, cols // 2
    return ref.at[pl.ds(pl.multiple_of(block * rw, rw), rw), pl.ds(pl.multiple_of(half * hc, LANES), hc)]


def _gather_weights(shards):
    nw = len(BIG)

    def body(*refs):
        s_refs = dict(zip(BIG, refs[:nw]))
        f_refs = dict(zip(BIG, refs[nw:2 * nw]))
        send, recv = refs[2 * nw:]
        x, y, c, chips = _place()
        me = 2 * x + y
        sib = (x, y, 1 - c)

        def rc(k, src, dst, to):
            return pltpu.make_async_remote_copy(src_ref=src, dst_ref=dst, send_sem=send.at[k], recv_sem=recv.at[k],
                                                device_id=to, device_id_type=MESH)

        first, fwd = [], []
        for wi, n in enumerate(BIG):
            own = _block_of(f_refs[n], n, me)
            cp = rc(24 + wi, s_refs[n], own, sib)
            cp.start()
            first.append(cp)
            for j, (cx, cy) in enumerate(chips):
                cp = rc(wi * 3 + j, _half_of(s_refs[n], n, c), _sub(f_refs[n], n, me, c), (cx, cy, c))
                cp.start()
                first.append(cp)
        for wi, n in enumerate(BIG):
            for j, (cx, cy) in enumerate(chips):
                part = _sub(f_refs[n], n, 2 * cx + cy, c)
                rc(wi * 3 + j, part, part, sib).wait_recv()
                cp = rc(12 + wi * 3 + j, part, part, sib)
                cp.start()
                fwd.append(cp)
        for wi, n in enumerate(BIG):
            own = _block_of(f_refs[n], n, me)
            rc(24 + wi, s_refs[n], own, sib).wait_recv()
            for j, (cx, cy) in enumerate(chips):
                part = _sub(f_refs[n], n, 2 * cx + cy, 1 - c)
                rc(12 + wi * 3 + j, part, part, sib).wait_recv()
        for cp in first + fwd:
            cp.wait_send()

    outs = []
    for n, s in zip(BIG, shards):
        r, cdim = s.shape
        outs.append(jax.ShapeDtypeStruct((r, cdim * N_CHIPS) if SHARD_AXIS[n] == 1 else (r * N_CHIPS, cdim), s.dtype))
    return pl.pallas_call(
        body, name="gather_weights", in_specs=[ANY] * nw, out_specs=[ANY] * nw, out_shape=outs,
        scratch_shapes=[pltpu.SemaphoreType.DMA((28,)), pltpu.SemaphoreType.DMA((28,))],
    )(*shards)


def _allreduce_small(part, name):
    R = part.shape[0]

    def body(p_ref, o_ref, sbuf, cbuf, send, recv):
        x, y, c, chips = _place()
        me = 2 * x + y
        sib = (x, y, 1 - c)
        sbuf[c] = p_ref[...]
        mine = sbuf.at[c]
        d2d = pltpu.make_async_remote_copy(src_ref=mine, dst_ref=mine, send_sem=send.at[0], recv_sem=recv.at[0],
                                           device_id=sib, device_id_type=MESH)
        d2d.start()
        theirs = sbuf.at[1 - c]
        pltpu.make_async_remote_copy(src_ref=theirs, dst_ref=theirs, send_sem=send.at[0], recv_sem=recv.at[0],
                                     device_id=sib, device_id_type=MESH).wait_recv()
        cbuf[me] = sbuf[0] + sbuf[1]
        blk = cbuf.at[me]
        sends = [d2d]
        for j, (cx, cy) in enumerate(chips):
            cp = pltpu.make_async_remote_copy(src_ref=blk, dst_ref=blk, send_sem=send.at[1 + j], recv_sem=recv.at[1 + j],
                                              device_id=(cx, cy, c), device_id_type=MESH)
            cp.start()
            sends.append(cp)
        for j, (cx, cy) in enumerate(chips):
            got = cbuf.at[2 * cx + cy]
            pltpu.make_async_remote_copy(src_ref=got, dst_ref=got, send_sem=send.at[1 + j], recv_sem=recv.at[1 + j],
                                         device_id=(cx, cy, c), device_id_type=MESH).wait_recv()
        o_ref[...] = (cbuf[0] + cbuf[1]) + (cbuf[2] + cbuf[3])
        for cp in sends:
            cp.wait_send()

    vm = pl.BlockSpec(memory_space=pltpu.VMEM)
    return pl.pallas_call(
        body, name=name, in_specs=[vm], out_specs=vm,
        out_shape=jax.ShapeDtypeStruct(part.shape, F32),
        scratch_shapes=[pltpu.VMEM((2, R, LANES), F32), pltpu.VMEM((N_CHIPS, R, LANES), F32),
                        pltpu.SemaphoreType.DMA((4,)), pltpu.SemaphoreType.DMA((4,))],
    )(part)


def _half_shape(shape, name):
    r, cdim = shape
    return (r // 2, cdim) if SHARD_AXIS[name] == 1 else (r, cdim // 2)


def _shard_shape(shape, name):
    r, cdim = shape
    return (r, cdim // N_CHIPS) if SHARD_AXIS[name] == 1 else (r // N_CHIPS, cdim)


def _swap_to_sibling(srcs, name, pick_half):
    nw = len(BIG)

    def body(*refs):
        g_refs = dict(zip(BIG, refs[:nw]))
        r_refs = dict(zip(BIG, refs[nw:2 * nw]))
        send, recv = refs[2 * nw:]
        x, y, c, _ = _place()
        sib = (x, y, 1 - c)
        cps = []
        for wi, n in enumerate(BIG):
            src = _half_of(g_refs[n], n, 1 - c) if pick_half else g_refs[n]
            cp = pltpu.make_async_remote_copy(src_ref=src, dst_ref=r_refs[n], send_sem=send.at[wi],
                                              recv_sem=recv.at[wi], device_id=sib, device_id_type=MESH)
            cp.start()
            cps.append(cp)
        for cp in cps:
            cp.wait()

    outs = [jax.ShapeDtypeStruct(_half_shape(g.shape, n) if pick_half else g.shape, g.dtype)
            for n, g in zip(BIG, srcs)]
    return pl.pallas_call(
        body, name=name, in_specs=[ANY] * nw, out_specs=[ANY] * nw, out_shape=outs,
        scratch_shapes=[pltpu.SemaphoreType.DMA((nw,)), pltpu.SemaphoreType.DMA((nw,))],
    )(*srcs)


def _add_half(cidx, g, r, name):
    rows, cols = r.shape
    tr = 256
    tc = cols if cols <= 1792 else (1792 if cols % 1792 == 0 else 1280)
    nr, nc = rows // tr, cols // tc

    def body(c_ref, g_ref, r_ref, o_ref, ob_ref):
        s = g_ref[...] + r_ref[...]
        o_ref[...] = s
        ob_ref[...] = s.astype(BF16)

    if SHARD_AXIS[name] == 1:
        gmap = lambda i, j, c_ref: (c_ref[0] * nr + i, j)
    else:
        gmap = lambda i, j, c_ref: (i, c_ref[0] * nc + j)
    same = lambda i, j, c_ref: (i, j)
    return pl.pallas_call(
        body, name="add_half_" + name,
        grid_spec=pltpu.PrefetchScalarGridSpec(
            num_scalar_prefetch=1, grid=(nr, nc),
            in_specs=[pl.BlockSpec((tr, tc), gmap), pl.BlockSpec((tr, tc), same)],
            out_specs=[pl.BlockSpec((tr, tc), same), pl.BlockSpec((tr, tc), same)]),
        out_shape=[jax.ShapeDtypeStruct(r.shape, F32), jax.ShapeDtypeStruct(r.shape, BF16)],
        compiler_params=pltpu.CompilerParams(dimension_semantics=("parallel", "parallel"), vmem_limit_bytes=VMEM_LIMIT),
    )(cidx, g, r)


def _scatter_blocks(halves):
    nw = len(BIG)

    def body(*refs):
        h_refs = dict(zip(BIG, refs[:nw]))
        r_refs = dict(zip(BIG, refs[nw:2 * nw]))
        send, recv = refs[2 * nw:]
        x, y, c, chips = _place()
        cps = []
        for wi, n in enumerate(BIG):
            for j, (cx, cy) in enumerate(chips):
                cp = pltpu.make_async_remote_copy(
                    src_ref=_block_of(h_refs[n], n, 2 * cx + cy), dst_ref=r_refs[n].at[j],
                    send_sem=send.at[wi * 3 + j], recv_sem=recv.at[wi * 3 + j],
                    device_id=(cx, cy, c), device_id_type=MESH)
                cp.start()
                cps.append(cp)
        for cp in cps:
            cp.wait()

    outs = [jax.ShapeDtypeStruct((3,) + _shard_shape(h.shape, n), h.dtype) for n, h in zip(BIG, halves)]
    return pl.pallas_call(
        body, name="scatter_blocks", in_specs=[ANY] * nw, out_specs=[ANY] * nw, out_shape=outs,
        scratch_shapes=[pltpu.SemaphoreType.DMA((3 * nw,)), pltpu.SemaphoreType.DMA((3 * nw,))],
    )(*halves)


def _add_blocks(bidx, h, r, name):
    _, rows, cols = r.shape
    tr = min(rows, 256)
    nr = rows // tr

    def body(b_ref, h_ref, r0, r1, r2, o_ref):
        o_ref[...] = ((h_ref[...] + r0[0].astype(F32)) + r1[0].astype(F32)) + r2[0].astype(F32)

    if SHARD_AXIS[name] == 1:
        hmap = lambda i, b_ref: (i, b_ref[0])
    else:
        hmap = lambda i, b_ref: (b_ref[0] * nr + i, 0)
    rspec = lambda j: pl.BlockSpec((1, tr, cols), lambda i, b_ref, j=j: (j, i, 0))
    return pl.pallas_call(
        body, name="add_blocks_" + name,
        grid_spec=pltpu.PrefetchScalarGridSpec(
            num_scalar_prefetch=1, grid=(nr,),
            in_specs=[pl.BlockSpec((tr, cols), hmap), rspec(0), rspec(1), rspec(2)],
            out_specs=pl.BlockSpec((tr, cols), lambda i, b_ref: (i, 0))),
        out_shape=jax.ShapeDtypeStruct((rows, cols), F32),
        compiler_params=pltpu.CompilerParams(dimension_semantics=("parallel",), vmem_limit_bytes=VMEM_LIMIT),
    )(bidx, h, r, r, r)


def _adam_math(w, g, m, v):
    c1 = 1.0 - ADAM_B1 ** ADAM_STEP
    c2 = 1.0 - ADAM_B2 ** ADAM_STEP
    nm = ADAM_B1 * m + (1.0 - ADAM_B1) * g
    nv = ADAM_B2 * v + (1.0 - ADAM_B2) * (g * g)
    delta = -ADAM_LR * ((nm / c1) / (jnp.sqrt(nv / c2) + ADAM_EPS) + ADAM_WD * w)
    return delta, nm, nv


def _adamw(w, g, m, v, name):
    def body(w_ref, g_ref, m_ref, v_ref, d_ref, nm_ref, nv_ref):
        d_ref[...], nm_ref[...], nv_ref[...] = _adam_math(w_ref[...], g_ref[...], m_ref[...], v_ref[...])

    spec = pl.BlockSpec(w.shape, lambda i: (0, 0))
    return _call(body, "adamw_" + name, (1,), [spec] * 4, [spec] * 3,
                 [jax.ShapeDtypeStruct(w.shape, F32)] * 3, sem=("arbitrary",))(w, g, m, v)


def _adamw_halves(cidx, w, mine, theirs, m, v, name):
    rows, cols = w.shape
    hr, hc = mine.shape
    tr = 128
    if SHARD_AXIS[name] == 1:
        ni = hr // tr
        wmap = lambda hh, i, c_ref: (hh * ni + i, 0)
    else:
        ni = hr // tr
        wmap = lambda hh, i, c_ref: (i, hh)
    hmap = lambda hh, i, c_ref: (i, 0)

    def body(c_ref, w_ref, a_ref, b_ref, m_ref, v_ref, g_ref, d_ref, nm_ref, nv_ref):
        g = jnp.where(pl.program_id(0) == c_ref[0], a_ref[...], b_ref[...])
        g_ref[...] = g
        d_ref[...], nm_ref[...], nv_ref[...] = _adam_math(w_ref[...], g, m_ref[...], v_ref[...])

    wspec = pl.BlockSpec((tr, hc), wmap)
    hspec = pl.BlockSpec((tr, hc), hmap)
    return pl.pallas_call(
        body, name="adamw_" + name,
        grid_spec=pltpu.PrefetchScalarGridSpec(
            num_scalar_prefetch=1, grid=(2, ni),
            in_specs=[wspec, hspec, hspec, wspec, wspec], out_specs=[wspec] * 4),
        out_shape=[jax.ShapeDtypeStruct(w.shape, F32)] * 4,
        compiler_params=pltpu.CompilerParams(dimension_semantics=("parallel", "parallel"), vmem_limit_bytes=VMEM_LIMIT),
    )(cidx, w, mine, theirs, m, v)


SMALL = ("e_norm_w", "e_q_norm_w", "e_k_norm_w", "e_conv_w", "o_norm_w", "o_pool_w", "o_pool_scale",
         "o_dconv_w", "o_dconv_b", "o_ln_w", "o_ln_b")
SMALL_SHARDED = ("e_conv_w", "o_norm_w", "o_pool_scale", "o_dconv_w", "o_dconv_b", "o_ln_w", "o_ln_b")
WEIGHTS = ("e_norm_w", "e_w_in", "e_q_norm_w", "e_k_norm_w", "e_conv_w", "e_w_out", "o_norm_w", "o_w_in",
           "o_pool_w", "o_pool_scale", "o_dconv_w", "o_dconv_b", "o_ln_w", "o_ln_b", "o_w_out")


def _pack(arrs):
    flat = jnp.concatenate([a.reshape(-1) for a in arrs])
    rows = -(-flat.shape[0] // (LANES * SUBLANES)) * SUBLANES
    flat = jnp.pad(flat, (0, rows * LANES - flat.shape[0]))
    return flat.reshape(rows, LANES)


def _unpack(packed, shapes):
    flat = packed.reshape(-1)
    out, off = [], 0
    for s in shapes:
        n = int(np.prod(s))
        out.append(flat[off:off + n].reshape(s))
        off += n
    return out


def _gather_last(a, block, width):
    return lax.dynamic_slice_in_dim(a, block * width, width, axis=a.ndim - 1)


def kernel(x, positions, e_norm_w, e_w_in, e_q_norm_w, e_k_norm_w, e_conv_w, e_w_out, o_norm_w, o_w_in, o_pool_w, o_pool_scale, o_dconv_w, o_dconv_b, o_ln_w, o_ln_b, o_w_out, loss_target, m_e_norm_w, m_e_w_in, m_e_q_norm_w, m_e_k_norm_w, m_e_conv_w, m_e_w_out, m_o_norm_w, m_o_w_in, m_o_pool_w, m_o_pool_scale, m_o_dconv_w, m_o_dconv_b, m_o_ln_w, m_o_ln_b, m_o_w_out, v_e_norm_w, v_e_w_in, v_e_q_norm_w, v_e_k_norm_w, v_e_conv_w, v_e_w_out, v_o_norm_w, v_o_w_in, v_o_pool_w, v_o_pool_scale, v_o_dconv_w, v_o_dconv_b, v_o_ln_w, v_o_ln_b, v_o_w_out):
    given = dict(e_norm_w=e_norm_w, e_w_in=e_w_in, e_q_norm_w=e_q_norm_w, e_k_norm_w=e_k_norm_w, e_conv_w=e_conv_w,
                 e_w_out=e_w_out, o_norm_w=o_norm_w, o_w_in=o_w_in, o_pool_w=o_pool_w, o_pool_scale=o_pool_scale,
                 o_dconv_w=o_dconv_w, o_dconv_b=o_dconv_b, o_ln_w=o_ln_w, o_ln_b=o_ln_b, o_w_out=o_w_out)
    mom = dict(e_norm_w=m_e_norm_w, e_w_in=m_e_w_in, e_q_norm_w=m_e_q_norm_w, e_k_norm_w=m_e_k_norm_w,
               e_conv_w=m_e_conv_w, e_w_out=m_e_w_out, o_norm_w=m_o_norm_w, o_w_in=m_o_w_in, o_pool_w=m_o_pool_w,
               o_pool_scale=m_o_pool_scale, o_dconv_w=m_o_dconv_w, o_dconv_b=m_o_dconv_b, o_ln_w=m_o_ln_w,
               o_ln_b=m_o_ln_b, o_w_out=m_o_w_out)
    var = dict(e_norm_w=v_e_norm_w, e_w_in=v_e_w_in, e_q_norm_w=v_e_q_norm_w, e_k_norm_w=v_e_k_norm_w,
               e_conv_w=v_e_conv_w, e_w_out=v_e_w_out, o_norm_w=v_o_norm_w, o_w_in=v_o_w_in, o_pool_w=v_o_pool_w,
               o_pool_scale=v_o_pool_scale, o_dconv_w=v_o_dconv_w, o_dconv_b=v_o_dconv_b, o_ln_w=v_o_ln_w,
               o_ln_b=v_o_ln_b, o_w_out=v_o_w_out)
    S = x.shape[1]
    mx, my, mc = lax.axis_index("x"), lax.axis_index("y"), lax.axis_index("c")
    chip = 2 * mx + my
    cidx = jnp.reshape(mc, (1,)).astype(jnp.int32)
    bidx = jnp.reshape(chip, (1,)).astype(jnp.int32)

    full_b = _gather_weights([given[n][0].astype(BF16) for n in BIG])
    wb = dict(zip(BIG, full_b))
    shard_sizes = [int(np.prod(given[n].shape)) for n in SMALL_SHARDED]
    own = _pack([given[n] for n in SMALL_SHARDED])
    rows = own.shape[0]
    slots = jnp.zeros((N_CHIPS, rows, LANES), F32)
    own = jnp.where(mc == 0, own, 0.0)
    slots = lax.dynamic_update_slice(slots, own[None], (chip, 0, 0))
    gathered = _allreduce_small(slots.reshape(N_CHIPS * rows, LANES), "gather_small")
    gathered = gathered.reshape(N_CHIPS, rows * LANES)
    p = {}
    off = 0
    for n, size in zip(SMALL_SHARDED, shard_sizes):
        sh = given[n].shape[1:]
        parts = gathered[:, off:off + size].reshape((N_CHIPS,) + sh)
        fullp = jnp.moveaxis(parts, 0, -2).reshape(sh[:-1] + (N_CHIPS * sh[-1],))
        p[n] = fullp.reshape(-1, fullp.shape[-1])
        off += size
    p["e_norm_w"] = e_norm_w
    p["e_q_norm_w"] = e_q_norm_w
    p["e_k_norm_w"] = e_k_norm_w
    p["o_pool_w"] = o_pool_w[0]

    lsum, dx, g = _local_step(x[0], positions.reshape(S, 1), loss_target[0], wb, p)
    loss = lax.psum(0.5 * jnp.sum(lsum) / float(D_MODEL), ("x", "y", "c"))

    tot = _unpack(_allreduce_small(_pack([g[n] for n in SMALL]), "allreduce_small"), [g[n].shape for n in SMALL])
    gsmall = dict(zip(SMALL, tot))
    grads = {}
    for n in SMALL:
        gv = gsmall[n]
        if n in SMALL_SHARDED:
            gv = _gather_last(gv, chip, gv.shape[-1] // N_CHIPS)
        grads[n] = gv.reshape(given[n].shape)

    recv_half = _swap_to_sibling([g[n] for n in BIG], "swap_halves", True)
    halves = [_add_half(cidx, g[n], r, n) for n, r in zip(BIG, recv_half)]
    recv_blk = _scatter_blocks([hb for _, hb in halves])
    parts = [_add_blocks(bidx, h, r, n) for n, (h, _), r in zip(BIG, halves, recv_blk)]
    theirs = _swap_to_sibling(parts, "swap_reduced", False)

    delta, new_m, new_v = {}, {}, {}
    for n, mine, other in zip(BIG, parts, theirs):
        sh = given[n].shape
        gs, d, nm, nv = _adamw_halves(cidx, given[n][0], mine, other, mom[n][0], var[n][0], n)
        grads[n], delta[n], new_m[n], new_v[n] = gs.reshape(sh), d.reshape(sh), nm.reshape(sh), nv.reshape(sh)
    shapes = [given[n].shape for n in SMALL]
    packed = [_pack([src[n] for n in SMALL]) for src in (given, grads, mom, var)]
    for dst, pk in zip((delta, new_m, new_v), _adamw(*packed, "small")):
        for n, a in zip(SMALL, _unpack(pk, shapes)):
            dst[n] = a
    return (loss, dx[None], *[grads[n] for n in WEIGHTS], *[delta[n] for n in WEIGHTS],
            *[new_m[n] for n in WEIGHTS], *[new_v[n] for n in WEIGHTS])
```

```python
import numpy as np
import jax
import jax.numpy as jnp
from jax import lax
from jax.experimental import pallas as pl
from jax.experimental.pallas import tpu as pltpu

F32 = jnp.float32
BF16 = jnp.bfloat16

D_MODEL = 1024
HEAD_DIM = 64
A_WIDTH = 512
A_HEADS = 8
A_GROUPS = ((128, 1), (512, 4), (2048, 16))
QBLK = 128
ROT_DIM = 16
ROPE_THETA = 500000.0
POOL_SIZES = (2, 4, 8, 16)
D_CONV = 31
SC_WIDTH = 3
EVEN_IN = 7168
ODD_IN = 2560
EPS = 1e-6
NEG = -1e30
ADAM_LR, ADAM_B1, ADAM_B2, ADAM_EPS, ADAM_WD, ADAM_STEP = 0.001, 0.9, 0.999, 1e-08, 0.01, 10

LANES = 128
SUBLANES = 8
HALO = 32
VMEM_LIMIT = 52 * 1024 * 1024
MESH = pl.DeviceIdType.MESH
ANY = pl.BlockSpec(memory_space=pl.ANY)

NT_DIMS = (((1,), (1,)), ((), ()))
TN_DIMS = (((0,), (0,)), ((), ()))


def _call(body, name, grid, in_specs, out_specs, out_shape, scratch=(), sem=None, aliases=None):
    return pl.pallas_call(
        body, name=name, grid=grid, in_specs=in_specs, out_specs=out_specs, out_shape=out_shape,
        scratch_shapes=list(scratch), input_output_aliases=aliases or {},
        compiler_params=pltpu.CompilerParams(dimension_semantics=sem, vmem_limit_bytes=VMEM_LIMIT))


def _sig(v):
    return jax.nn.sigmoid(v)


def _dsilu(v, s):
    return s * (1.0 + v * (1.0 - s))


def _emit_u(u_ref, ut_ref, lo, hi):
    for k, v in enumerate((lo, hi)):
        u_ref[:, k * A_WIDTH:(k + 1) * A_WIDTH] = v.astype(BF16)
        ut_ref[k * A_WIDTH:(k + 1) * A_WIDTH, :] = v.T.astype(BF16)


def _cs8(v):
    return v.reshape(v.shape[0] // SUBLANES, SUBLANES, v.shape[1]).sum(axis=0)


def _seg_ones():
    r = lax.broadcasted_iota(jnp.int32, (LANES, LANES), 0) // HEAD_DIM
    c = lax.broadcasted_iota(jnp.int32, (LANES, LANES), 1) // HEAD_DIM
    return (r == c).astype(BF16)


def _segsum(v, ones):
    hi = v.astype(BF16)
    lo = (v - hi.astype(F32)).astype(BF16)
    return (jnp.dot(hi, ones, preferred_element_type=F32) + jnp.dot(lo, ones, preferred_element_type=F32))


def _rope_tables(pos_ref, freq_ref):
    ang = pos_ref[...].astype(F32) * freq_ref[...]
    cosv, sinv = jnp.cos(ang), jnp.sin(ang)
    lm = lax.broadcasted_iota(jnp.int32, ang.shape, 1) % HEAD_DIM
    half = ROT_DIM // 2
    c = jnp.where(lm < ROT_DIM, cosv, 1.0)
    s1 = jnp.where((lm >= half) & (lm < ROT_DIM), sinv, 0.0)
    s2 = jnp.where(lm < half, -sinv, 0.0)
    return c, s1, s2


def _freq_table():
    half = ROT_DIM // 2
    inv = ROPE_THETA ** (-np.arange(half, dtype=np.float64) / half)
    lane = np.arange(LANES) % HEAD_DIM
    f = np.where(lane < ROT_DIM, inv[lane % half], 0.0)
    return jnp.asarray(f.reshape(1, LANES), F32)


def _load_once(hbm_ref, vmem_ref, sem):
    @pl.when(pl.program_id(0) == 0)
    def _():
        cp = pltpu.make_async_copy(hbm_ref, vmem_ref, sem)
        cp.start()
        cp.wait()


def _inproj(x, nw, w, tm, tn, name):
    S, N = x.shape[0], w.shape[1]

    def body(x_ref, nw_ref, w_hbm, o_ref, ht_ref, w_v, sem):
        _load_once(w_hbm, w_v, sem)
        xv = x_ref[...]
        ms = jnp.mean(xv * xv, axis=-1, keepdims=True)
        h = xv * lax.rsqrt(ms + EPS) * nw_ref[...]
        ht_ref[...] = h.T.astype(BF16)
        hb = h.astype(BF16)
        for j in range(N // tn):
            o_ref[:, j * tn:(j + 1) * tn] = jnp.dot(hb, w_v[:, j * tn:(j + 1) * tn], preferred_element_type=F32)

    return _call(
        body, name, (S // tm,),
        [pl.BlockSpec((tm, D_MODEL), lambda i: (i, 0)),
         pl.BlockSpec((1, D_MODEL), lambda i: (0, 0)), ANY],
        [pl.BlockSpec((tm, N), lambda i: (i, 0)),
         pl.BlockSpec((D_MODEL, tm), lambda i: (0, i))],
        [jax.ShapeDtypeStruct((S, N), F32), jax.ShapeDtypeStruct((D_MODEL, S), BF16)],
        scratch=[pltpu.VMEM(w.shape, BF16), pltpu.SemaphoreType.DMA(())], sem=("arbitrary",))(x, nw, w)


def _outproj(x, u, w, tm, name):
    S = x.shape[0]

    def body(x_ref, u_ref, w_ref, o_ref):
        o_ref[...] = x_ref[...] + jnp.dot(u_ref[...].astype(BF16), w_ref[...], preferred_element_type=F32)

    return _call(
        body, name, (S // tm,),
        [pl.BlockSpec((tm, D_MODEL), lambda i: (i, 0)),
         pl.BlockSpec((tm, D_MODEL), lambda i: (i, 0)),
         pl.BlockSpec((D_MODEL, D_MODEL), lambda i: (0, 0))],
        pl.BlockSpec((tm, D_MODEL), lambda i: (i, 0)),
        jax.ShapeDtypeStruct((S, D_MODEL), F32), sem=("parallel",))(x, u, w)


def _outproj_loss(x, u, w, tgt, tm, name):
    S = x.shape[0]

    def body(x_ref, u_ref, w_ref, t_ref, dy_ref, l_ref, acc):
        i = pl.program_id(0)

        @pl.when(i == 0)
        def _():
            acc[...] = jnp.zeros_like(acc)
        y = x_ref[...] + jnp.dot(u_ref[...].astype(BF16), w_ref[...], preferred_element_type=F32)
        diff = y - t_ref[...]
        dy_ref[...] = diff / float(D_MODEL)
        acc[...] += _cs8(diff * diff)

        @pl.when(i == pl.num_programs(0) - 1)
        def _():
            l_ref[...] = jnp.sum(acc[...], axis=0, keepdims=True)

    return _call(
        body, name, (S // tm,),
        [pl.BlockSpec((tm, D_MODEL), lambda i: (i, 0)),
         pl.BlockSpec((tm, D_MODEL), lambda i: (i, 0)),
         pl.BlockSpec((D_MODEL, D_MODEL), lambda i: (0, 0)),
         pl.BlockSpec((tm, D_MODEL), lambda i: (i, 0))],
        [pl.BlockSpec((tm, D_MODEL), lambda i: (i, 0)),
         pl.BlockSpec((1, D_MODEL), lambda i: (0, 0))],
        [jax.ShapeDtypeStruct((S, D_MODEL), F32), jax.ShapeDtypeStruct((1, D_MODEL), F32)],
        scratch=[pltpu.VMEM((SUBLANES, D_MODEL), F32)], sem=("arbitrary",))(x, u, w, tgt)


def _mm_nt(a, w, tm, name):
    S, K = a.shape
    N = w.shape[0]

    def body(a_ref, w_ref, o_ref):
        o_ref[...] = lax.dot_general(a_ref[...].astype(BF16), w_ref[...], NT_DIMS, preferred_element_type=F32)

    return _call(
        body, name, (S // tm,),
        [pl.BlockSpec((tm, K), lambda i: (i, 0)), pl.BlockSpec((N, K), lambda i: (0, 0))],
        pl.BlockSpec((tm, N), lambda i: (i, 0)),
        jax.ShapeDtypeStruct((S, N), F32), sem=("parallel",))(a, w)


def _piece_blocks(pieces, tk, axis):
    starts, counts, s = [], [], 0
    for p in pieces:
        n = p.shape[axis] // tk
        starts.append(s)
        counts.append(n)
        s += n
    return starts, counts, s


def _mm_nt_rms(pieces, w, x, nw, dres, tm, name):
    S = x.shape[0]
    npc = len(pieces)
    ni = S // tm
    offs = np.cumsum([0] + [p.shape[1] for p in pieces]).tolist()

    def body(*refs):
        p_refs = refs[:npc]
        w_hbm, x_ref, nw_ref, dr_ref, dx_ref, dnw_ref, w_v, sem, nacc = refs[npc:]
        i = pl.program_id(0)
        _load_once(w_hbm, w_v, sem)

        @pl.when(i == 0)
        def _():
            nacc[...] = jnp.zeros_like(nacc)

        dh = None
        for p in range(npc):
            part = lax.dot_general(p_refs[p][...].astype(BF16), w_v[:, offs[p]:offs[p + 1]], NT_DIMS,
                                   preferred_element_type=F32)
            dh = part if dh is None else dh + part
        xv = x_ref[...]
        rs = lax.rsqrt(jnp.mean(xv * xv, axis=-1, keepdims=True) + EPS)
        xh = xv * rs
        nacc[...] += _cs8(dh * xh)
        dxh = dh * nw_ref[...]
        dx_ref[...] = dr_ref[...] + rs * (dxh - xh * jnp.mean(dxh * xh, axis=-1, keepdims=True))

        @pl.when(i == ni - 1)
        def _():
            dnw_ref[...] = jnp.sum(nacc[...], axis=0, keepdims=True)

    row = pl.BlockSpec((tm, D_MODEL), lambda i: (i, 0))
    return _call(
        body, name, (ni,),
        [pl.BlockSpec((tm, p.shape[1]), lambda i: (i, 0)) for p in pieces] +
        [ANY, row, pl.BlockSpec((1, D_MODEL), lambda i: (0, 0)), row],
        [row, pl.BlockSpec((1, D_MODEL), lambda i: (0, 0))],
        [jax.ShapeDtypeStruct((S, D_MODEL), F32), jax.ShapeDtypeStruct((1, D_MODEL), F32)],
        scratch=[pltpu.VMEM(w.shape, BF16), pltpu.SemaphoreType.DMA(()), pltpu.VMEM((SUBLANES, D_MODEL), F32)],
        sem=("arbitrary",))(*pieces, w, x, nw, dres)


def _mm_wgrad(at, pieces, tn, name):
    M, S = at.shape
    starts, counts, nj = _piece_blocks(pieces, tn, 1)
    npc = len(pieces)

    def body(*refs):
        a_hbm = refs[0]
        p_refs = refs[1:1 + npc]
        o_ref, a_v, sem = refs[1 + npc:]
        j = pl.program_id(0)
        _load_once(a_hbm, a_v, sem)
        for p in range(npc):
            @pl.when((j >= starts[p]) & (j < starts[p] + counts[p]))
            def _(p=p):
                o_ref[...] = jnp.dot(a_v[...], p_refs[p][...].astype(BF16), preferred_element_type=F32)

    def pspec(p):
        return pl.BlockSpec((S, tn), lambda j: (0, jnp.clip(j - starts[p], 0, counts[p] - 1)))

    return _call(
        body, name, (nj,),
        [ANY] + [pspec(p) for p in range(npc)],
        pl.BlockSpec((M, tn), lambda j: (0, j)),
        jax.ShapeDtypeStruct((M, nj * tn), F32),
        scratch=[pltpu.VMEM(at.shape, BF16), pltpu.SemaphoreType.DMA(())], sem=("arbitrary",))(at, *pieces)


def _stream_spec(d, T):
    return pl.BlockSpec((d, T // d, A_WIDTH), lambda i: (0, i, 0))


def _stream_shape(d, S, dtype):
    return jax.ShapeDtypeStruct((d, S // d, A_WIDTH), dtype)


N_CHUNK = A_WIDTH // LANES


def _to_tokens(ref, scr, d, T):
    if d == 1:
        return ref[0].astype(F32)
    for r in range(d):
        for ch in range(N_CHUNK):
            scr.at[ch][pl.ds(r, T // d, stride=d), :] = ref[r, :, ch * LANES:(ch + 1) * LANES].astype(F32)
    return _get(scr)


def _from_tokens(out_ref, scr, d, T):
    for r in range(d):
        for ch in range(N_CHUNK):
            out_ref[r, :, ch * LANES:(ch + 1) * LANES] = scr.at[ch][pl.ds(r, T // d, stride=d), :].astype(out_ref.dtype)


def _put(scr, val):
    for ch in range(N_CHUNK):
        scr[ch] = val[:, ch * LANES:(ch + 1) * LANES]


def _get(scr):
    return jnp.concatenate([scr[ch] for ch in range(N_CHUNK)], axis=1)


def _chunked(T):
    return pltpu.VMEM((N_CHUNK, T, LANES), F32)


def _compact_spec(d, T):
    return pl.BlockSpec((d, T // d, LANES), lambda i: (0, i, 0))


def _compact_shape(d, S):
    return jax.ShapeDtypeStruct((d, S // d, LANES), F32)


def _compact_to_tokens(ref, scr, d, T):
    if d == 1:
        return ref[0]
    for r in range(d):
        scr[pl.ds(r, T // d, stride=d), :] = ref[r]
    return scr[...]


def _compact_from_tokens(out_ref, scr, val, d, T):
    if d == 1:
        out_ref[0] = val
        return
    scr[...] = val
    for r in range(d):
        out_ref[r] = scr[pl.ds(r, T // d, stride=d), :]


def _head_expander():
    r = lax.broadcasted_iota(jnp.int32, (LANES, A_WIDTH), 0)
    c = lax.broadcasted_iota(jnp.int32, (LANES, A_WIDTH), 1) // HEAD_DIM
    return (r == c).astype(BF16)


def _head_reducer():
    r = lax.broadcasted_iota(jnp.int32, (A_WIDTH, LANES), 0) // HEAD_DIM
    c = lax.broadcasted_iota(jnp.int32, (A_WIDTH, LANES), 1)
    return (r == c).astype(BF16)


def _qkv_prep(proj, pos, freq, wq, wk, T):
    S = proj.shape[0]
    qk_w = 3 * A_WIDTH

    def body(q_ref, k_ref, v_ref, pos_ref, f_ref, wq_ref, wk_ref, *rest):
        outs, scr = rest[:9], rest[9]
        ones = _seg_ones()
        c, s1, s2 = _rope_tables(pos_ref, f_ref)
        for t, (src, w_ref) in enumerate(((q_ref, wq_ref), (k_ref, wk_ref), (v_ref, None))):
            for g in range(3):
                d = A_GROUPS[g][1]
                out = outs[3 * t + g]
                for ch in range(A_WIDTH // LANES):
                    cs = slice(ch * LANES, (ch + 1) * LANES)
                    v = src[:, g * A_WIDTH + ch * LANES: g * A_WIDTH + (ch + 1) * LANES]
                    if w_ref is not None:
                        rs = lax.rsqrt(_segsum(v * v, ones) * (1.0 / HEAD_DIM) + EPS)
                        y = v * rs * w_ref[...]
                        v = y * c + pltpu.roll(y, 8, 1) * s1 + pltpu.roll(y, LANES - 8, 1) * s2
                    if d == 1:
                        out[0, :, cs] = v.astype(BF16)
                    else:
                        scr[ch] = v
                if d > 1:
                    _from_tokens(out, scr, d, T)

    ds_ = [A_GROUPS[g][1] for g in range(3)] * 3
    return _call(
        body, "qkv_prep", (S // T,),
        [pl.BlockSpec((T, qk_w), lambda i: (i, 0)), pl.BlockSpec((T, qk_w), lambda i: (i, 1)),
         pl.BlockSpec((T, qk_w), lambda i: (i, 2)),
         pl.BlockSpec((T, 1), lambda i: (i, 0)), pl.BlockSpec((1, LANES), lambda i: (0, 0)),
         pl.BlockSpec((1, LANES), lambda i: (0, 0)), pl.BlockSpec((1, LANES), lambda i: (0, 0))],
        [_stream_spec(d, T) for d in ds_],
        [_stream_shape(d, S, BF16) for d in ds_],
        scratch=[_chunked(T)], sem=("parallel",))(proj, proj, proj, pos, freq, wq, wk)


def _attn_mask(i):
    qi = lax.broadcasted_iota(jnp.int32, (QBLK, 2 * QBLK), 0) + QBLK
    kj = lax.broadcasted_iota(jnp.int32, (QBLK, 2 * QBLK), 1)
    dist = qi - kj
    return (dist >= 0) & (dist <= QBLK) & ((i > 0) | (kj >= QBLK))


ATT_BLK = (None, QBLK, A_WIDTH)
ATT_CBLK = (None, QBLK, LANES)


def _first_head_lanes():
    return lax.broadcasted_iota(jnp.int32, (1, LANES), 1) < HEAD_DIM


def _split_heads(v, first):
    zero = jnp.zeros_like(v)
    return jnp.where(first, v, zero), jnp.where(first, zero, v)


def _attn_fwd(q, k, v, g):
    d, n, _ = q.shape
    nb = n // QBLK

    def body(q_ref, kp_ref, kc_ref, vp_ref, vc_ref, o_ref, l_ref, s_scr, p_scr):
        i = pl.program_id(1)
        mask = _attn_mask(i)
        first = _first_head_lanes()
        for pr in range(A_HEADS // 2):
            ps = slice(pr * LANES, (pr + 1) * LANES)
            kc = jnp.concatenate([kp_ref[:, ps], kc_ref[:, ps]], axis=0)
            for e, qh in enumerate(_split_heads(q_ref[:, ps], first)):
                s_scr[2 * pr + e] = lax.dot_general(qh, kc, NT_DIMS, preferred_element_type=F32)
        lane = lax.broadcasted_iota(jnp.int32, (1, LANES), 1)
        lrow = jnp.zeros((QBLK, LANES), F32)
        for h in range(A_HEADS):
            s = jnp.where(mask, s_scr[h] * (HEAD_DIM ** -0.5), NEG)
            m = jnp.max(s, axis=-1, keepdims=True)
            p = jnp.exp(s - m)
            den = jnp.sum(p, axis=-1, keepdims=True)
            p_scr[h] = (p / den).astype(BF16)
            lrow = jnp.where(lane == h, m + jnp.log(den), lrow)
        l_ref[...] = lrow
        for pr in range(A_HEADS // 2):
            ps = slice(pr * LANES, (pr + 1) * LANES)
            va, vb = _split_heads(jnp.concatenate([vp_ref[:, ps], vc_ref[:, ps]], axis=0), first)
            o_ref[:, ps] = (jnp.dot(p_scr[2 * pr], va, preferred_element_type=F32) +
                            jnp.dot(p_scr[2 * pr + 1], vb, preferred_element_type=F32)).astype(BF16)

    prev = lambda r, i: (r, jnp.maximum(i - 1, 0), 0)
    cur = lambda r, i: (r, i, 0)
    return _call(
        body, "attn_fwd_g%d" % g, (d, nb),
        [pl.BlockSpec(ATT_BLK, cur), pl.BlockSpec(ATT_BLK, prev), pl.BlockSpec(ATT_BLK, cur),
         pl.BlockSpec(ATT_BLK, prev), pl.BlockSpec(ATT_BLK, cur)],
        [pl.BlockSpec(ATT_BLK, cur), pl.BlockSpec(ATT_CBLK, cur)],
        [jax.ShapeDtypeStruct((d, n, A_WIDTH), BF16), jax.ShapeDtypeStruct((d, n, LANES), F32)],
        scratch=[pltpu.VMEM((A_HEADS, QBLK, 2 * QBLK), F32), pltpu.VMEM((A_HEADS, QBLK, 2 * QBLK), BF16)],
        sem=("parallel", "parallel"))(q, k, k, v, v)


def _attn_bwd(q, k, v, do, lse, cg, g):
    d, n, _ = q.shape
    nb = n // QBLK
    scale = HEAD_DIM ** -0.5

    def body(q_ref, kp_ref, kc_ref, vp_ref, vc_ref, do_ref, l_ref, c_ref, dq_ref, dk_ref, dv_ref, ck, cv,
             s_scr, dp_scr, p_scr, ds_scr):
        i = pl.program_id(1)

        @pl.when(i == 0)
        def _():
            ck[...] = jnp.zeros_like(ck)
            cv[...] = jnp.zeros_like(cv)

        @pl.when(i < nb)
        def _():
            mask = _attn_mask(i)
            first = _first_head_lanes()
            for pr in range(A_HEADS // 2):
                ps = slice(pr * LANES, (pr + 1) * LANES)
                kc = jnp.concatenate([kp_ref[:, ps], kc_ref[:, ps]], axis=0)
                vc = jnp.concatenate([vp_ref[:, ps], vc_ref[:, ps]], axis=0)
                qs = _split_heads(q_ref[:, ps], first)
                dos = _split_heads(do_ref[:, ps], first)
                for e in range(2):
                    s_scr[2 * pr + e] = lax.dot_general(qs[e], kc, NT_DIMS, preferred_element_type=F32)
                    dp_scr[2 * pr + e] = lax.dot_general(dos[e], vc, NT_DIMS, preferred_element_type=F32)
            for h in range(A_HEADS):
                p = jnp.where(mask, jnp.exp(s_scr[h] * scale - l_ref[:, h:h + 1]), 0.0)
                p_scr[h] = p.astype(BF16)
                ds_scr[h] = (p * (dp_scr[h] + c_ref[:, h:h + 1]) * scale).astype(BF16)
            for pr in range(A_HEADS // 2):
                ps = slice(pr * LANES, (pr + 1) * LANES)
                ks = _split_heads(jnp.concatenate([kp_ref[:, ps], kc_ref[:, ps]], axis=0), first)
                qs = _split_heads(q_ref[:, ps], first)
                dos = _split_heads(do_ref[:, ps], first)
                dq = dkc = dvc = None
                for e in range(2):
                    ds = ds_scr[2 * pr + e]
                    a = jnp.dot(ds, ks[e], preferred_element_type=F32)
                    b = lax.dot_general(ds, qs[e], TN_DIMS, preferred_element_type=F32)
                    c = lax.dot_general(p_scr[2 * pr + e], dos[e], TN_DIMS, preferred_element_type=F32)
                    dq, dkc, dvc = (a, b, c) if e == 0 else (dq + a, dkc + b, dvc + c)
                dq_ref[:, ps] = dq.astype(BF16)
                dk_ref[:, ps] = (ck[:, ps] + dkc[:QBLK]).astype(BF16)
                dv_ref[:, ps] = (cv[:, ps] + dvc[:QBLK]).astype(BF16)
                ck[:, ps] = dkc[QBLK:]
                cv[:, ps] = dvc[QBLK:]

        @pl.when(i == nb)
        def _():
            dk_ref[...] = ck[...].astype(BF16)
            dv_ref[...] = cv[...].astype(BF16)

    qi = lambda i: jnp.minimum(i, nb - 1)
    cur = lambda r, i: (r, qi(i), 0)
    prev = lambda r, i: (r, jnp.maximum(qi(i) - 1, 0), 0)
    late = lambda r, i: (r, jnp.maximum(i - 1, 0), 0)
    return _call(
        body, "attn_bwd_g%d" % g, (d, nb + 1),
        [pl.BlockSpec(ATT_BLK, cur), pl.BlockSpec(ATT_BLK, prev), pl.BlockSpec(ATT_BLK, cur),
         pl.BlockSpec(ATT_BLK, prev), pl.BlockSpec(ATT_BLK, cur),
         pl.BlockSpec(ATT_BLK, cur), pl.BlockSpec(ATT_CBLK, cur), pl.BlockSpec(ATT_CBLK, cur)],
        [pl.BlockSpec(ATT_BLK, cur), pl.BlockSpec(ATT_BLK, late), pl.BlockSpec(ATT_BLK, late)],
        [jax.ShapeDtypeStruct((d, n, A_WIDTH), BF16)] * 3,
        scratch=[pltpu.VMEM((QBLK, A_WIDTH), F32), pltpu.VMEM((QBLK, A_WIDTH), F32),
                 pltpu.VMEM((A_HEADS, QBLK, 2 * QBLK), F32), pltpu.VMEM((A_HEADS, QBLK, 2 * QBLK), F32),
                 pltpu.VMEM((A_HEADS, QBLK, 2 * QBLK), BF16), pltpu.VMEM((A_HEADS, QBLK, 2 * QBLK), BF16)],
        sem=("parallel", "arbitrary"))(q, k, k, v, v, do, lse, cg)


def _merge_weights(l0, l1, l2):
    mx = jnp.maximum(jnp.maximum(l0, l1), l2)
    e0, e1, e2 = jnp.exp(l0 - mx), jnp.exp(l1 - mx), jnp.exp(l2 - mx)
    den = e0 + e1 + e2
    return e0 / den, e1 / den, e2 / den


def _even_specs(T, S):
    t8 = T // SUBLANES
    last8 = S // SUBLANES - 1
    col = lambda c: pl.BlockSpec((T, A_WIDTH), lambda i: (i, c))
    prev8 = lambda c: pl.BlockSpec((SUBLANES, A_WIDTH), lambda i: (jnp.maximum(i * t8 - 1, 0), c))
    next8 = lambda c: pl.BlockSpec((SUBLANES, A_WIDTH), lambda i: (jnp.minimum((i + 1) * t8, last8), c))
    return col, prev8, next8


GROUP_D = tuple(d for _, d in A_GROUPS)


def _even_mixer_fwd(proj, os_, ls_, conv_w, T):
    S = proj.shape[0]
    col, prev8, _ = _even_specs(T, S)
    H = SUBLANES

    def body(bg_r, cg_r, hb_r, zl_r, zh_r, cgp_r, hbp_r, o0, o1, o2, l0, l1, l2, cw_r, u_ref, ut_ref, ext,
             cscr, *scr):
        i = pl.program_id(0)
        ls = [_compact_to_tokens(r, cscr, GROUP_D[g], T) for g, r in enumerate((l0, l1, l2))]
        expand = _head_expander()
        ws = [_segsum(w, expand) for w in _merge_weights(*ls)]
        oa = ws[0] * _to_tokens(o0, scr[0], GROUP_D[0], T)
        oa = oa + ws[1] * _to_tokens(o1, scr[1], GROUP_D[1], T)
        oa = oa + ws[2] * _to_tokens(o2, scr[2], GROUP_D[2], T)
        ext[0:H, :] = jnp.where(i == 0, 0.0, cgp_r[...] * hbp_r[...])
        ext[H:H + T, :] = cg_r[...] * hb_r[...]
        conv = cw_r[0:1, :] * ext[H - 2:H - 2 + T, :]
        for kk in range(1, SC_WIDTH):
            conv = conv + cw_r[kk:kk + 1, :] * ext[H - 2 + kk:H - 2 + kk + T, :]
        zl, zh = zl_r[...], zh_r[...]
        _emit_u(u_ref, ut_ref, oa * (zl * _sig(zl)), bg_r[...] * conv * (zh * _sig(zh)))

    streams = [_stream_spec(d, T) for d in GROUP_D]
    compacts = [_compact_spec(d, T) for d in GROUP_D]
    return _call(
        body, "even_mixer_fwd", (S // T,),
        [col(9), col(10), col(11), col(12), col(13), prev8(10), prev8(11)] + streams + compacts +
        [pl.BlockSpec((SC_WIDTH, A_WIDTH), lambda i: (0, 0))],
        [pl.BlockSpec((T, D_MODEL), lambda i: (i, 0)), pl.BlockSpec((D_MODEL, T), lambda i: (0, i))],
        [jax.ShapeDtypeStruct((S, D_MODEL), BF16), jax.ShapeDtypeStruct((D_MODEL, S), BF16)],
        scratch=[pltpu.VMEM((T + H, A_WIDTH), F32), pltpu.VMEM((T, LANES), F32)] + [_chunked(T)] * 3,
        sem=("parallel",))(
            proj, proj, proj, proj, proj, proj, proj, *os_, *ls_, conv_w)


def _even_mixer_bwd(du, proj, os_, ls_, conv_w, T):
    S = proj.shape[0]
    nt = S // T
    col, prev8, next8 = _even_specs(T, S)
    H = SUBLANES
    t8 = T // SUBLANES
    last8 = S // SUBLANES - 1

    def body(dul_r, duh_r, bg_r, cg_r, hb_r, zl_r, zh_r, cgp_r, hbp_r, dun_r, zhn_r, bgn_r,
             o0, o1, o2, l0, l1, l2, cw_r,
             do0, do1, do2, c0, c1, c2, dr_ref, dcw_ref, ext_t, ext_d, acc, cscr, s_a, s_b, s_c):
        i = pl.program_id(0)

        @pl.when(i == 0)
        def _():
            acc[...] = jnp.zeros_like(acc)

        zl, zh = zl_r[...], zh_r[...]
        sl, sh = _sig(zl), _sig(zh)
        dul, duh = dul_r[...], duh_r[...]
        scr = (s_a, s_b, s_c)
        ls = [_compact_to_tokens(r, cscr, GROUP_D[g], T) for g, r in enumerate((l0, l1, l2))]
        wcs = _merge_weights(*ls)
        expand = _head_expander()
        ws = [_segsum(w, expand) for w in wcs]
        oa = ws[0] * _to_tokens(o0, scr[0], GROUP_D[0], T)
        oa = oa + ws[1] * _to_tokens(o1, scr[1], GROUP_D[1], T)
        oa = oa + ws[2] * _to_tokens(o2, scr[2], GROUP_D[2], T)
        doa = dul * (zl * sl)
        rsum = _segsum(doa * oa, _head_reducer())
        for g, (do_ref, c_ref) in enumerate(((do0, c0), (do1, c1), (do2, c2))):
            d = GROUP_D[g]
            _compact_from_tokens(c_ref, cscr, -wcs[g] * rsum, d, T)
            if d == 1:
                do_ref[0] = (ws[g] * doa).astype(BF16)
            else:
                _put(s_c, ws[g] * doa)
                _from_tokens(do_ref, s_c, d, T)
        cgv, hbv, bgv = cg_r[...], hb_r[...], bg_r[...]
        ext_t[0:H, :] = jnp.where(i == 0, 0.0, cgp_r[...] * hbp_r[...])
        ext_t[H:H + T, :] = cgv * hbv
        conv = cw_r[0:1, :] * ext_t[H - 2:H - 2 + T, :]
        for kk in range(1, SC_WIDTH):
            conv = conv + cw_r[kk:kk + 1, :] * ext_t[H - 2 + kk:H - 2 + kk + T, :]
        dyb = duh * (zh * sh)
        dconv = dyb * bgv
        zn = zhn_r[...]
        ext_d[0:T, :] = dconv
        ext_d[T:T + H, :] = jnp.where(i == nt - 1, 0.0, dun_r[...] * (zn * _sig(zn)) * bgn_r[...])
        dt = cw_r[0:1, :] * ext_d[2:2 + T, :]
        for kk in range(1, SC_WIDTH):
            dt = dt + cw_r[kk:kk + 1, :] * ext_d[2 - kk:2 - kk + T, :]
        for kk in range(SC_WIDTH):
            acc[kk * SUBLANES:(kk + 1) * SUBLANES, :] += _cs8(dconv * ext_t[H - 2 + kk:H - 2 + kk + T, :])
        dr_ref[:, 0:A_WIDTH] = (dyb * conv).astype(BF16)
        dr_ref[:, A_WIDTH:2 * A_WIDTH] = (dt * hbv).astype(BF16)
        dr_ref[:, 2 * A_WIDTH:3 * A_WIDTH] = (dt * cgv).astype(BF16)
        dr_ref[:, 3 * A_WIDTH:4 * A_WIDTH] = (dul * oa * _dsilu(zl, sl)).astype(BF16)
        dr_ref[:, 4 * A_WIDTH:5 * A_WIDTH] = (duh * (bgv * conv) * _dsilu(zh, sh)).astype(BF16)

        @pl.when(i == nt - 1)
        def _():
            for kk in range(SC_WIDTH):
                dcw_ref[kk:kk + 1, :] = jnp.sum(acc[kk * SUBLANES:(kk + 1) * SUBLANES, :], axis=0, keepdims=True)

    streams = [_stream_spec(d, T) for d in GROUP_D]
    dunext = pl.BlockSpec((SUBLANES, A_WIDTH), lambda i: (jnp.minimum((i + 1) * t8, last8), 1))
    compacts = [_compact_spec(d, T) for d in GROUP_D]
    outs = _call(
        body, "even_mixer_bwd", (nt,),
        [pl.BlockSpec((T, A_WIDTH), lambda i: (i, 0)), pl.BlockSpec((T, A_WIDTH), lambda i: (i, 1)),
         col(9), col(10), col(11), col(12), col(13), prev8(10), prev8(11), dunext, next8(13), next8(9)] +
        streams + compacts + [pl.BlockSpec((SC_WIDTH, A_WIDTH), lambda i: (0, 0))],
        streams + compacts + [pl.BlockSpec((T, 5 * A_WIDTH), lambda i: (i, 0)),
                              pl.BlockSpec((SC_WIDTH, A_WIDTH), lambda i: (0, 0))],
        [_stream_shape(d, S, BF16) for d in GROUP_D] + [_compact_shape(d, S) for d in GROUP_D] +
        [jax.ShapeDtypeStruct((S, 5 * A_WIDTH), BF16), jax.ShapeDtypeStruct((SC_WIDTH, A_WIDTH), F32)],
        scratch=[pltpu.VMEM((T + H, A_WIDTH), F32), pltpu.VMEM((T + H, A_WIDTH), F32),
                 pltpu.VMEM((SC_WIDTH * SUBLANES, A_WIDTH), F32), pltpu.VMEM((T, LANES), F32)] +
                [_chunked(T)] * 3,
        sem=("arbitrary",))(du, du, proj, proj, proj, proj, proj, proj, proj, du, proj, proj, *os_, *ls_, conv_w)
    return outs[0:3], outs[3:6], outs[6], outs[7]


def _qk_bwd(proj, dqs, dks, dvs, pos, freq, wq, wk, T):
    S = proj.shape[0]
    nt = S // T
    qk_w = 3 * A_WIDTH

    def body(q_ref, k_ref, dq0, dq1, dq2, dk0, dk1, dk2, dv0, dv1, dv2, pos_ref, f_ref, wq_ref, wk_ref,
             o_ref, dw_ref, acc, scr):
        i = pl.program_id(0)

        @pl.when(i == 0)
        def _():
            acc[...] = jnp.zeros_like(acc)
            dw_ref[...] = jnp.zeros_like(dw_ref)

        ones = _seg_ones()
        c, s1, s2 = _rope_tables(pos_ref, f_ref)
        for t, (src, w_ref, ds) in enumerate(((q_ref, wq_ref, (dq0, dq1, dq2)), (k_ref, wk_ref, (dk0, dk1, dk2)))):
            wv = w_ref[...]
            for g in range(3):
                d = GROUP_D[g]
                if d > 1:
                    _to_tokens(ds[g], scr, d, T)
                for ch in range(A_WIDTH // LANES):
                    cs = slice(g * A_WIDTH + ch * LANES, g * A_WIDTH + (ch + 1) * LANES)
                    lc = slice(ch * LANES, (ch + 1) * LANES)
                    v = src[:, cs]
                    dout = ds[g][0, :, lc].astype(F32) if d == 1 else scr[ch]
                    rs = lax.rsqrt(_segsum(v * v, ones) * (1.0 / HEAD_DIM) + EPS)
                    xh = v * rs
                    dy = dout * c + pltpu.roll(dout * s1, LANES - 8, 1) + pltpu.roll(dout * s2, 8, 1)
                    acc[t * SUBLANES:(t + 1) * SUBLANES, :] += _cs8(dy * xh)
                    dxh = dy * wv
                    mean = _segsum(dxh * xh, ones) * (1.0 / HEAD_DIM)
                    o_ref[:, t * qk_w + g * A_WIDTH + ch * LANES: t * qk_w + g * A_WIDTH + (ch + 1) * LANES] = (
                        rs * (dxh - xh * mean)).astype(BF16)
        for g, dv in enumerate((dv0, dv1, dv2)):
            d = GROUP_D[g]
            base = 2 * qk_w + g * A_WIDTH
            o_ref[:, base:base + A_WIDTH] = _to_tokens(dv, scr, d, T).astype(BF16)

        @pl.when(i == nt - 1)
        def _():
            for t in range(2):
                srow = jnp.sum(acc[t * SUBLANES:(t + 1) * SUBLANES, :], axis=0, keepdims=True)
                dw_ref[t:t + 1, :] = srow + pltpu.roll(srow, HEAD_DIM, 1)

    streams = [_stream_spec(d, T) for d in GROUP_D]
    return _call(
        body, "qk_bwd", (nt,),
        [pl.BlockSpec((T, qk_w), lambda i: (i, 0)), pl.BlockSpec((T, qk_w), lambda i: (i, 1))] + streams * 3 +
        [pl.BlockSpec((T, 1), lambda i: (i, 0)), pl.BlockSpec((1, LANES), lambda i: (0, 0)),
         pl.BlockSpec((1, LANES), lambda i: (0, 0)), pl.BlockSpec((1, LANES), lambda i: (0, 0))],
        [pl.BlockSpec((T, 3 * qk_w), lambda i: (i, 0)), pl.BlockSpec((SUBLANES, LANES), lambda i: (0, 0))],
        [jax.ShapeDtypeStruct((S, 3 * qk_w), BF16), jax.ShapeDtypeStruct((SUBLANES, LANES), F32)],
        scratch=[pltpu.VMEM((2 * SUBLANES, LANES), F32), _chunked(T)], sem=("arbitrary",))(
            proj, proj, *dqs, *dks, *dvs, pos, freq, wq, wk)


N_SMALL_ODD = 40
SHIFT_ROWS_LESS = SUBLANES


def _fill_shifted(ext_ref, sh_ref):
    rows = ext_ref.shape[0] - SHIFT_ROWS_LESS
    for b in range(1, SUBLANES):
        sh_ref[b - 1] = ext_ref[b:b + rows, :]


def _window(ext_ref, sh_ref, off, T):
    a, b = divmod(off, SUBLANES)
    if b == 0:
        return ext_ref[off:off + T, :]
    return sh_ref[b - 1, a * SUBLANES:a * SUBLANES + T, :]


def _odd_pool_tile(i, uc_r, ucp_r, pw_r, ext_u, pooled_s, pm_s, T):
    H = HALO
    uc = uc_r[...]
    ext_u[0:H, :] = jnp.where(i == 0, 0.0, ucp_r[...])
    ext_u[H:H + T, :] = uc
    row = i * T + lax.broadcasted_iota(jnp.int32, (T, 1), 0)
    for g, p in enumerate(POOL_SIZES):
        cs = slice(g * LANES, (g + 1) * LANES)
        win = ext_u[H:H + T, cs]
        for j in range(1, p):
            win = win + ext_u[H - j:H - j + T, cs]
        cnt = jnp.minimum(row + 1, p).astype(F32)
        pooled = win / cnt - uc[:, cs]
        pooled_s[:, cs] = pooled
        pm_s[:, cs] = jnp.dot(pooled.astype(BF16), pw_r[g].astype(BF16), preferred_element_type=F32)
    return row


def _odd_glu_tile(i, da_r, dg_r, dap_r, dgp_r, ext_g, sh_g, T):
    H = HALO
    ext_g[0:H, :] = jnp.where(i == 0, 0.0, dap_r[...] * _sig(dgp_r[...]))
    ext_g[H:H + T, :] = da_r[...] * _sig(dg_r[...])
    _fill_shifted(ext_g, sh_g)


def _odd_specs(T, S, order):
    tb = T // HALO
    col = lambda c: pl.BlockSpec((T, A_WIDTH), lambda s: (order(s), c))
    prev = lambda c: pl.BlockSpec((HALO, A_WIDTH), lambda s: (jnp.maximum(order(s) * tb - 1, 0), c))
    const2 = lambda shape: pl.BlockSpec(shape, lambda s: (0, 0))
    weights = [pl.BlockSpec((4, LANES, LANES), lambda s: (0, 0, 0)), const2((1, A_WIDTH)),
               const2((D_CONV, A_WIDTH)), const2((1, A_WIDTH)), const2((1, A_WIDTH)), const2((1, A_WIDTH))]
    return col, prev, weights


def _odd_mixer_fwd(proj, pool_w, scale, dconv_w, dconv_b, ln_w, ln_b, T):
    S = proj.shape[0]
    col, prev, wspecs = _odd_specs(T, S, lambda s: s)
    H = HALO

    def body(uc_r, da_r, dg_r, zl_r, zh_r, ucp_r, dap_r, dgp_r, pw_r, sc_r, dw_r, db_r, lw_r, lb_r,
             u_ref, ut_ref, cv_ref, ext_u, ext_g, sh_g, pooled_s, pm_s):
        i = pl.program_id(0)
        _odd_pool_tile(i, uc_r, ucp_r, pw_r, ext_u, pooled_s, pm_s, T)
        _odd_glu_tile(i, da_r, dg_r, dap_r, dgp_r, ext_g, sh_g, T)
        base = H - (D_CONV - 1)
        conv = db_r[...] + dw_r[0:1, :] * _window(ext_g, sh_g, base, T)
        for kk in range(1, D_CONV):
            conv = conv + dw_r[kk:kk + 1, :] * _window(ext_g, sh_g, base + kk, T)
        cv_ref[...] = conv
        mu = jnp.mean(conv, axis=-1, keepdims=True)
        xc = conv - mu
        yh = xc * lax.rsqrt(jnp.mean(xc * xc, axis=-1, keepdims=True) + EPS)
        ln = yh * lw_r[...] + lb_r[...]
        zl, zh = zl_r[...], zh_r[...]
        _emit_u(u_ref, ut_ref, pm_s[...] * sc_r[...] * (zl * _sig(zl)), ln * _sig(ln) * (zh * _sig(zh)))

    return _call(
        body, "odd_mixer_fwd", (S // T,),
        [col(0), col(1), col(2), col(3), col(4), prev(0), prev(1), prev(2)] + wspecs,
        [pl.BlockSpec((T, D_MODEL), lambda i: (i, 0)), pl.BlockSpec((D_MODEL, T), lambda i: (0, i)),
         pl.BlockSpec((T, A_WIDTH), lambda i: (i, 0))],
        [jax.ShapeDtypeStruct((S, D_MODEL), BF16), jax.ShapeDtypeStruct((D_MODEL, S), BF16),
         jax.ShapeDtypeStruct((S, A_WIDTH), F32)],
        scratch=[pltpu.VMEM((T + H, A_WIDTH), F32), pltpu.VMEM((T + H, A_WIDTH), F32),
                 pltpu.VMEM((SUBLANES - 1, T + H - SHIFT_ROWS_LESS, A_WIDTH), F32),
                 pltpu.VMEM((T, A_WIDTH), F32), pltpu.VMEM((T, A_WIDTH), F32)],
        sem=("parallel",))(proj, proj, proj, proj, proj, proj, proj, proj,
                           pool_w, scale, dconv_w, dconv_b, ln_w, ln_b)


def _odd_mixer_bwd(du, proj, conv, pool_w, scale, dconv_w, dconv_b, ln_w, ln_b, T):
    S = proj.shape[0]
    nt = S // T
    order = lambda s: nt - 1 - s
    col, prev, wspecs = _odd_specs(T, S, order)
    H = HALO

    def body(dul_r, duh_r, cv_r, uc_r, da_r, dg_r, zl_r, zh_r, ucp_r, dap_r, dgp_r, pw_r, sc_r, dw_r, db_r, lw_r, lb_r,
             dp_ref, dpw_ref, sm_ref, ext_u, ext_g, sh_g, pooled_s, pm_s, dpl_s, ext_p, ext_c, sh_c, acc):
        step = pl.program_id(0)
        i = nt - 1 - step

        @pl.when(step == 0)
        def _():
            ext_p[T:T + H, :] = jnp.zeros((H, A_WIDTH), F32)
            ext_c[T:T + H, :] = jnp.zeros((H, A_WIDTH), F32)
            acc[...] = jnp.zeros_like(acc)
            dpw_ref[...] = jnp.zeros_like(dpw_ref)

        def accum(r, v):
            acc[r * SUBLANES:(r + 1) * SUBLANES, :] += _cs8(v)

        row = _odd_pool_tile(i, uc_r, ucp_r, pw_r, ext_u, pooled_s, pm_s, T)
        _odd_glu_tile(i, da_r, dg_r, dap_r, dgp_r, ext_g, sh_g, T)
        conv = cv_r[...]
        mu = jnp.mean(conv, axis=-1, keepdims=True)
        xc = conv - mu
        rstd = lax.rsqrt(jnp.mean(xc * xc, axis=-1, keepdims=True) + EPS)
        yh = xc * rstd
        ln = yh * lw_r[...] + lb_r[...]
        sln = _sig(ln)
        zl, zh = zl_r[...], zh_r[...]
        sl, sh = _sig(zl), _sig(zh)
        dul, duh = dul_r[...], duh_r[...]
        pm = pm_s[...]
        scv = sc_r[...]
        dyc = dul * (zl * sl)
        accum(34, dyc * pm)
        dpm = dyc * scv
        for g in range(len(POOL_SIZES)):
            cs = slice(g * LANES, (g + 1) * LANES)
            dpm_g = dpm[:, cs].astype(BF16)
            dpw_ref[g] += lax.dot_general(pooled_s[:, cs].astype(BF16), dpm_g, TN_DIMS, preferred_element_type=F32)
            dpl_s[:, cs] = lax.dot_general(dpm_g, pw_r[g].astype(BF16), NT_DIMS, preferred_element_type=F32)
        lane_p = lax.broadcasted_iota(jnp.int32, (1, A_WIDTH), 1) // LANES
        pvec = jnp.left_shift(2, lane_p)
        cnt = jnp.minimum(row + 1, pvec).astype(F32)
        dpl = dpl_s[...]
        ext_p[0:T, :] = dpl / cnt
        for g, p in enumerate(POOL_SIZES):
            cs = slice(g * LANES, (g + 1) * LANES)
            win = ext_p[0:T, cs]
            for j in range(1, p):
                win = win + ext_p[j:j + T, cs]
            dp_ref[:, cs] = (win - dpl[:, cs]).astype(BF16)
        ext_p[T:T + H, :] = ext_p[0:H, :]
        dln = duh * (zh * sh) * _dsilu(ln, sln)
        accum(32, dln * yh)
        accum(33, dln)
        dyh = dln * lw_r[...]
        dc = rstd * (dyh - jnp.mean(dyh, axis=-1, keepdims=True) - yh * jnp.mean(dyh * yh, axis=-1, keepdims=True))
        accum(31, dc)
        ext_c[0:T, :] = dc
        _fill_shifted(ext_c, sh_c)
        base = H - (D_CONV - 1)
        dgl = dw_r[0:1, :] * _window(ext_c, sh_c, D_CONV - 1, T)
        accum(0, dc * _window(ext_g, sh_g, base, T))
        for kk in range(1, D_CONV):
            dgl = dgl + dw_r[kk:kk + 1, :] * _window(ext_c, sh_c, D_CONV - 1 - kk, T)
            accum(kk, dc * _window(ext_g, sh_g, base + kk, T))
        ext_c[T:T + H, :] = ext_c[0:H, :]
        dav, dgv = da_r[...], dg_r[...]
        sg = _sig(dgv)
        dp_ref[:, A_WIDTH:2 * A_WIDTH] = (dgl * sg).astype(BF16)
        dp_ref[:, 2 * A_WIDTH:3 * A_WIDTH] = (dgl * dav * sg * (1.0 - sg)).astype(BF16)
        dp_ref[:, 3 * A_WIDTH:4 * A_WIDTH] = (dul * (pm * scv) * _dsilu(zl, sl)).astype(BF16)
        dp_ref[:, 4 * A_WIDTH:5 * A_WIDTH] = (duh * (ln * sln) * _dsilu(zh, sh)).astype(BF16)

        @pl.when(step == nt - 1)
        def _():
            for r in range(N_SMALL_ODD):
                sm_ref[r:r + 1, :] = jnp.sum(acc[r * SUBLANES:(r + 1) * SUBLANES, :], axis=0, keepdims=True)

    ext = pltpu.VMEM((T + H, A_WIDTH), F32)
    shifted = pltpu.VMEM((SUBLANES - 1, T + H - SHIFT_ROWS_LESS, A_WIDTH), F32)
    tile = pltpu.VMEM((T, A_WIDTH), F32)
    return _call(
        body, "odd_mixer_bwd", (nt,),
        [pl.BlockSpec((T, A_WIDTH), lambda s: (order(s), 0)), pl.BlockSpec((T, A_WIDTH), lambda s: (order(s), 1)),
         pl.BlockSpec((T, A_WIDTH), lambda s: (order(s), 0)),
         col(0), col(1), col(2), col(3), col(4), prev(0), prev(1), prev(2)] + wspecs,
        [pl.BlockSpec((T, ODD_IN), lambda s: (order(s), 0)),
         pl.BlockSpec((4, LANES, LANES), lambda s: (0, 0, 0)),
         pl.BlockSpec((N_SMALL_ODD, A_WIDTH), lambda s: (0, 0))],
        [jax.ShapeDtypeStruct((S, ODD_IN), BF16), jax.ShapeDtypeStruct((4, LANES, LANES), F32),
         jax.ShapeDtypeStruct((N_SMALL_ODD, A_WIDTH), F32)],
        scratch=[ext, ext, shifted, tile, tile, tile, ext, ext, shifted,
                 pltpu.VMEM((N_SMALL_ODD * SUBLANES, A_WIDTH), F32)],
        sem=("arbitrary",))(du, du, conv, proj, proj, proj, proj, proj, proj, proj, proj,
                            pool_w, scale, dconv_w, dconv_b, ln_w, ln_b)


TILE_SEQ = 256
TILE_M = 512
TILE_WG = 256


def _local_step(x, pos, tgt, wb, p):
    T = TILE_SEQ
    freq = _freq_table()
    wq = jnp.tile(p["e_q_norm_w"], (1, LANES // HEAD_DIM))
    wk = jnp.tile(p["e_k_norm_w"], (1, LANES // HEAD_DIM))

    proj_e, ht_e = _inproj(x, p["e_norm_w"], wb["e_w_in"], TILE_SEQ, 1792, "inproj_even")
    qkv = _qkv_prep(proj_e, pos, freq, wq, wk, T)
    qs, ks, vs = qkv[0:3], qkv[3:6], qkv[6:9]
    os_, ls_ = [], []
    for g in range(3):
        o, l = _attn_fwd(qs[g], ks[g], vs[g], g)
        os_.append(o)
        ls_.append(l)
    u_e, ut_e = _even_mixer_fwd(proj_e, os_, ls_, p["e_conv_w"], T)
    x1 = _outproj(x, u_e, wb["e_w_out"], TILE_M, "outproj_even")
    proj_o, ht_o = _inproj(x1, p["o_norm_w"], wb["o_w_in"], TILE_SEQ, 1280, "inproj_odd")
    odd_w = (p["o_pool_w"], p["o_pool_scale"], p["o_dconv_w"], p["o_dconv_b"], p["o_ln_w"], p["o_ln_b"])
    u_o, ut_o, conv_o = _odd_mixer_fwd(proj_o, *odd_w, T)
    dy, lsum = _outproj_loss(x1, u_o, wb["o_w_out"], tgt, TILE_M, "outproj_odd_loss")

    g = {}
    g["o_w_out"] = _mm_wgrad(ut_o, [dy], TILE_WG, "wgrad_o_out")
    du_o = _mm_nt(dy, wb["o_w_out"], TILE_M, "du_odd")
    dproj_o, g["o_pool_w"], small_o = _odd_mixer_bwd(du_o, proj_o, conv_o, *odd_w, T)
    g["o_w_in"] = _mm_wgrad(ht_o, [dproj_o], TILE_WG, "wgrad_o_in")
    dx1, g["o_norm_w"] = _mm_nt_rms([dproj_o], wb["o_w_in"], x1, p["o_norm_w"], dy, TILE_SEQ, "dx_odd")
    g["o_dconv_w"] = small_o[0:D_CONV]
    g["o_dconv_b"] = small_o[31:32]
    g["o_ln_w"] = small_o[32:33]
    g["o_ln_b"] = small_o[33:34]
    g["o_pool_scale"] = small_o[34:35]

    g["e_w_out"] = _mm_wgrad(ut_e, [dx1], TILE_WG, "wgrad_e_out")
    du_e = _mm_nt(dx1, wb["e_w_out"], TILE_M, "du_even")
    dos, cgs, drest, g["e_conv_w"] = _even_mixer_bwd(du_e, proj_e, os_, ls_, p["e_conv_w"], T)
    dqs, dks, dvs = [], [], []
    for gi in range(3):
        dq, dk, dv = _attn_bwd(qs[gi], ks[gi], vs[gi], dos[gi], ls_[gi], cgs[gi], gi)
        dqs.append(dq)
        dks.append(dk)
        dvs.append(dv)
    dqkv, dnw = _qk_bwd(proj_e, dqs, dks, dvs, pos, freq, wq, wk, T)
    g["e_q_norm_w"] = dnw[0:1, 0:HEAD_DIM]
    g["e_k_norm_w"] = dnw[1:2, 0:HEAD_DIM]
    pieces = [dqkv, drest]
    g["e_w_in"] = _mm_wgrad(ht_e, pieces, TILE_WG, "wgrad_e_in")
    dx, g["e_norm_w"] = _mm_nt_rms(pieces, wb["e_w_in"], x, p["e_norm_w"], dx1, TILE_SEQ, "dx_even")
    return lsum, dx, g


BIG = ("e_w_in", "e_w_out", "o_w_in", "o_w_out")
SHARD_AXIS = {"e_w_in": 1, "e_w_out": 0, "o_w_in": 1, "o_w_out": 0}
N_CHIPS = 4


def _place():
    x, y, c = lax.axis_index("x"), lax.axis_index("y"), lax.axis_index("c")
    chips = [(1 - x, y), (x, 1 - y), (1 - x, 1 - y)]
    return x, y, c, chips


def _block_of(ref, name, block):
    rows, cols = ref.shape
    if SHARD_AXIS[name] == 1:
        cw = cols // N_CHIPS
        return ref.at[:, pl.ds(pl.multiple_of(block * cw, LANES), cw)]
    rw = rows // N_CHIPS
    return ref.at[pl.ds(pl.multiple_of(block * rw, rw), rw), :]


def _half_of(ref, name, half):
    rows, cols = ref.shape
    if SHARD_AXIS[name] == 1:
        return ref.at[pl.ds(pl.multiple_of(half * (rows // 2), rows // 2), rows // 2), :]
    return ref.at[:, pl.ds(pl.multiple_of(half * (cols // 2), LANES), cols // 2)]


def _sub(ref, name, block, half):
    rows, cols = ref.shape
    if SHARD_AXIS[name] == 1:
        cw, hr = cols // N_CHIPS, rows // 2
        return ref.at[pl.ds(pl.multiple_of(half * hr, hr), hr), pl.ds(pl.multiple_of(block * cw, LANES), cw)]
    rw, hc = rows // N_CHIPS, cols // 2
    return ref.at[pl.ds(pl.multiple_of(block * rw, rw), rw), pl.ds(pl.multiple_of(half * hc, LANES), hc)]


def _gather_weights(shards):
    nw = len(BIG)

    def body(*refs):
        s_refs = dict(zip(BIG, refs[:nw]))
        f_refs = dict(zip(BIG, refs[nw:2 * nw]))
        send, recv = refs[2 * nw:]
        x, y, c, chips = _place()
        me = 2 * x + y
        sib = (x, y, 1 - c)

        def rc(k, src, dst, to):
            return pltpu.make_async_remote_copy(src_ref=src, dst_ref=dst, send_sem=send.at[k], recv_sem=recv.at[k],
                                                device_id=to, device_id_type=MESH)

        first, fwd = [], []
        for wi, n in enumerate(BIG):
            own = _block_of(f_refs[n], n, me)
            cp = rc(24 + wi, s_refs[n], own, sib)
            cp.start()
            first.append(cp)
            for j, (cx, cy) in enumerate(chips):
                cp = rc(wi * 3 + j, _half_of(s_refs[n], n, c), _sub(f_refs[n], n, me, c), (cx, cy, c))
                cp.start()
                first.append(cp)
        for wi, n in enumerate(BIG):
            for j, (cx, cy) in enumerate(chips):
                part = _sub(f_refs[n], n, 2 * cx + cy, c)
                rc(wi * 3 + j, part, part, sib).wait_recv()
                cp = rc(12 + wi * 3 + j, part, part, sib)
                cp.start()
                fwd.append(cp)
        for wi, n in enumerate(BIG):
            own = _block_of(f_refs[n], n, me)
            rc(24 + wi, s_refs[n], own, sib).wait_recv()
            for j, (cx, cy) in enumerate(chips):
                part = _sub(f_refs[n], n, 2 * cx + cy, 1 - c)
                rc(12 + wi * 3 + j, part, part, sib).wait_recv()
        for cp in first + fwd:
            cp.wait_send()

    outs = []
    for n, s in zip(BIG, shards):
        r, cdim = s.shape
        outs.append(jax.ShapeDtypeStruct((r, cdim * N_CHIPS) if SHARD_AXIS[n] == 1 else (r * N_CHIPS, cdim), s.dtype))
    return pl.pallas_call(
        body, name="gather_weights", in_specs=[ANY] * nw, out_specs=[ANY] * nw, out_shape=outs,
        scratch_shapes=[pltpu.SemaphoreType.DMA((28,)), pltpu.SemaphoreType.DMA((28,))],
    )(*shards)


def _allreduce_small(part, name):
    R = part.shape[0]

    def body(p_ref, o_ref, sbuf, cbuf, send, recv):
        x, y, c, chips = _place()
        me = 2 * x + y
        sib = (x, y, 1 - c)
        sbuf[c] = p_ref[...]
        mine = sbuf.at[c]
        d2d = pltpu.make_async_remote_copy(src_ref=mine, dst_ref=mine, send_sem=send.at[0], recv_sem=recv.at[0],
                                           device_id=sib, device_id_type=MESH)
        d2d.start()
        theirs = sbuf.at[1 - c]
        pltpu.make_async_remote_copy(src_ref=theirs, dst_ref=theirs, send_sem=send.at[0], recv_sem=recv.at[0],
                                     device_id=sib, device_id_type=MESH).wait_recv()
        cbuf[me] = sbuf[0] + sbuf[1]
        blk = cbuf.at[me]
        sends = [d2d]
        for j, (cx, cy) in enumerate(chips):
            cp = pltpu.make_async_remote_copy(src_ref=blk, dst_ref=blk, send_sem=send.at[1 + j], recv_sem=recv.at[1 + j],
                                              device_id=(cx, cy, c), device_id_type=MESH)
            cp.start()
            sends.append(cp)
        for j, (cx, cy) in enumerate(chips):
            got = cbuf.at[2 * cx + cy]
            pltpu.make_async_remote_copy(src_ref=got, dst_ref=got, send_sem=send.at[1 + j], recv_sem=recv.at[1 + j],
                                         device_id=(cx, cy, c), device_id_type=MESH).wait_recv()
        o_ref[...] = (cbuf[0] + cbuf[1]) + (cbuf[2] + cbuf[3])
        for cp in sends:
            cp.wait_send()

    vm = pl.BlockSpec(memory_space=pltpu.VMEM)
    return pl.pallas_call(
        body, name=name, in_specs=[vm], out_specs=vm,
        out_shape=jax.ShapeDtypeStruct(part.shape, F32),
        scratch_shapes=[pltpu.VMEM((2, R, LANES), F32), pltpu.VMEM((N_CHIPS, R, LANES), F32),
                        pltpu.SemaphoreType.DMA((4,)), pltpu.SemaphoreType.DMA((4,))],
    )(part)


def _half_shape(shape, name):
    r, cdim = shape
    return (r // 2, cdim) if SHARD_AXIS[name] == 1 else (r, cdim // 2)


def _shard_shape(shape, name):
    r, cdim = shape
    return (r, cdim // N_CHIPS) if SHARD_AXIS[name] == 1 else (r // N_CHIPS, cdim)


def _swap_to_sibling(srcs, name, pick_half):
    nw = len(BIG)

    def body(*refs):
        g_refs = dict(zip(BIG, refs[:nw]))
        r_refs = dict(zip(BIG, refs[nw:2 * nw]))
        send, recv = refs[2 * nw:]
        x, y, c, _ = _place()
        sib = (x, y, 1 - c)
        cps = []
        for wi, n in enumerate(BIG):
            src = _half_of(g_refs[n], n, 1 - c) if pick_half else g_refs[n]
            cp = pltpu.make_async_remote_copy(src_ref=src, dst_ref=r_refs[n], send_sem=send.at[wi],
                                              recv_sem=recv.at[wi], device_id=sib, device_id_type=MESH)
            cp.start()
            cps.append(cp)
        for cp in cps:
            cp.wait()

    outs = [jax.ShapeDtypeStruct(_half_shape(g.shape, n) if pick_half else g.shape, g.dtype)
            for n, g in zip(BIG, srcs)]
    return pl.pallas_call(
        body, name=name, in_specs=[ANY] * nw, out_specs=[ANY] * nw, out_shape=outs,
        scratch_shapes=[pltpu.SemaphoreType.DMA((nw,)), pltpu.SemaphoreType.DMA((nw,))],
    )(*srcs)


def _add_half(cidx, g, r, name):
    rows, cols = r.shape
    tr = 256
    tc = cols if cols <= 1792 else (1792 if cols % 1792 == 0 else 1280)
    nr, nc = rows // tr, cols // tc

    def body(c_ref, g_ref, r_ref, o_ref, ob_ref):
        s = g_ref[...] + r_ref[...]
        o_ref[...] = s
        ob_ref[...] = s.astype(BF16)

    if SHARD_AXIS[name] == 1:
        gmap = lambda i, j, c_ref: (c_ref[0] * nr + i, j)
    else:
        gmap = lambda i, j, c_ref: (i, c_ref[0] * nc + j)
    same = lambda i, j, c_ref: (i, j)
    return pl.pallas_call(
        body, name="add_half_" + name,
        grid_spec=pltpu.PrefetchScalarGridSpec(
            num_scalar_prefetch=1, grid=(nr, nc),
            in_specs=[pl.BlockSpec((tr, tc), gmap), pl.BlockSpec((tr, tc), same)],
            out_specs=[pl.BlockSpec((tr, tc), same), pl.BlockSpec((tr, tc), same)]),
        out_shape=[jax.ShapeDtypeStruct(r.shape, F32), jax.ShapeDtypeStruct(r.shape, BF16)],
        compiler_params=pltpu.CompilerParams(dimension_semantics=("parallel", "parallel"), vmem_limit_bytes=VMEM_LIMIT),
    )(cidx, g, r)


def _scatter_blocks(halves):
    nw = len(BIG)

    def body(*refs):
        h_refs = dict(zip(BIG, refs[:nw]))
        r_refs = dict(zip(BIG, refs[nw:2 * nw]))
        send, recv = refs[2 * nw:]
        x, y, c, chips = _place()
        cps = []
        for wi, n in enumerate(BIG):
            for j, (cx, cy) in enumerate(chips):
                cp = pltpu.make_async_remote_copy(
                    src_ref=_block_of(h_refs[n], n, 2 * cx + cy), dst_ref=r_refs[n].at[j],
                    send_sem=send.at[wi * 3 + j], recv_sem=recv.at[wi * 3 + j],
                    device_id=(cx, cy, c), device_id_type=MESH)
                cp.start()
                cps.append(cp)
        for cp in cps:
            cp.wait()

    outs = [jax.ShapeDtypeStruct((3,) + _shard_shape(h.shape, n), h.dtype) for n, h in zip(BIG, halves)]
    return pl.pallas_call(
        body, name="scatter_blocks", in_specs=[ANY] * nw, out_specs=[ANY] * nw, out_shape=outs,
        scratch_shapes=[pltpu.SemaphoreType.DMA((3 * nw,)), pltpu.SemaphoreType.DMA((3 * nw,))],
    )(*halves)


def _add_blocks(bidx, h, r, name):
    _, rows, cols = r.shape
    tr = min(rows, 256)
    nr = rows // tr

    def body(b_ref, h_ref, r0, r1, r2, o_ref):
        o_ref[...] = ((h_ref[...] + r0[0].astype(F32)) + r1[0].astype(F32)) + r2[0].astype(F32)

    if SHARD_AXIS[name] == 1:
        hmap = lambda i, b_ref: (i, b_ref[0])
    else:
        hmap = lambda i, b_ref: (b_ref[0] * nr + i, 0)
    rspec = lambda j: pl.BlockSpec((1, tr, cols), lambda i, b_ref, j=j: (j, i, 0))
    return pl.pallas_call(
        body, name="add_blocks_" + name,
        grid_spec=pltpu.PrefetchScalarGridSpec(
            num_scalar_prefetch=1, grid=(nr,),
            in_specs=[pl.BlockSpec((tr, cols), hmap), rspec(0), rspec(1), rspec(2)],
            out_specs=pl.BlockSpec((tr, cols), lambda i, b_ref: (i, 0))),
        out_shape=jax.ShapeDtypeStruct((rows, cols), F32),
        compiler_params=pltpu.CompilerParams(dimension_semantics=("parallel",), vmem_limit_bytes=VMEM_LIMIT),
    )(bidx, h, r, r, r)


def _adam_math(w, g, m, v):
    c1 = 1.0 - ADAM_B1 ** ADAM_STEP
    c2 = 1.0 - ADAM_B2 ** ADAM_STEP
    nm = ADAM_B1 * m + (1.0 - ADAM_B1) * g
    nv = ADAM_B2 * v + (1.0 - ADAM_B2) * (g * g)
    delta = -ADAM_LR * ((nm / c1) / (jnp.sqrt(nv / c2) + ADAM_EPS) + ADAM_WD * w)
    return delta, nm, nv


def _adamw(w, g, m, v, name):
    def body(w_ref, g_ref, m_ref, v_ref, d_ref, nm_ref, nv_ref):
        d_ref[...], nm_ref[...], nv_ref[...] = _adam_math(w_ref[...], g_ref[...], m_ref[...], v_ref[...])

    spec = pl.BlockSpec(w.shape, lambda i: (0, 0))
    return _call(body, "adamw_" + name, (1,), [spec] * 4, [spec] * 3,
                 [jax.ShapeDtypeStruct(w.shape, F32)] * 3, sem=("arbitrary",))(w, g, m, v)


def _adamw_halves(cidx, w, mine, theirs, m, v, name):
    rows, cols = w.shape
    hr, hc = mine.shape
    tr = 128
    if SHARD_AXIS[name] == 1:
        ni = hr // tr
        wmap = lambda hh, i, c_ref: (hh * ni + i, 0)
    else:
        ni = hr // tr
        wmap = lambda hh, i, c_ref: (i, hh)
    hmap = lambda hh, i, c_ref: (i, 0)

    def body(c_ref, w_ref, a_ref, b_ref, m_ref, v_ref, g_ref, d_ref, nm_ref, nv_ref):
        g = jnp.where(pl.program_id(0) == c_ref[0], a_ref[...], b_ref[...])
        g_ref[...] = g
        d_ref[...], nm_ref[...], nv_ref[...] = _adam_math(w_ref[...], g, m_ref[...], v_ref[...])

    wspec = pl.BlockSpec((tr, hc), wmap)
    hspec = pl.BlockSpec((tr, hc), hmap)
    return pl.pallas_call(
        body, name="adamw_" + name,
        grid_spec=pltpu.PrefetchScalarGridSpec(
            num_scalar_prefetch=1, grid=(2, ni),
            in_specs=[wspec, hspec, hspec, wspec, wspec], out_specs=[wspec] * 4),
        out_shape=[jax.ShapeDtypeStruct(w.shape, F32)] * 4,
        compiler_params=pltpu.CompilerParams(dimension_semantics=("parallel", "parallel"), vmem_limit_bytes=VMEM_LIMIT),
    )(cidx, w, mine, theirs, m, v)


SMALL = ("e_norm_w", "e_q_norm_w", "e_k_norm_w", "e_conv_w", "o_norm_w", "o_pool_w", "o_pool_scale",
         "o_dconv_w", "o_dconv_b", "o_ln_w", "o_ln_b")
SMALL_SHARDED = ("e_conv_w", "o_norm_w", "o_pool_scale", "o_dconv_w", "o_dconv_b", "o_ln_w", "o_ln_b")
WEIGHTS = ("e_norm_w", "e_w_in", "e_q_norm_w", "e_k_norm_w", "e_conv_w", "e_w_out", "o_norm_w", "o_w_in",
           "o_pool_w", "o_pool_scale", "o_dconv_w", "o_dconv_b", "o_ln_w", "o_ln_b", "o_w_out")


def _pack(arrs):
    flat = jnp.concatenate([a.reshape(-1) for a in arrs])
    rows = -(-flat.shape[0] // (LANES * SUBLANES)) * SUBLANES
    flat = jnp.pad(flat, (0, rows * LANES - flat.shape[0]))
    return flat.reshape(rows, LANES)


def _unpack(packed, shapes):
    flat = packed.reshape(-1)
    out, off = [], 0
    for s in shapes:
        n = int(np.prod(s))
        out.append(flat[off:off + n].reshape(s))
        off += n
    return out


def _gather_last(a, block, width):
    return lax.dynamic_slice_in_dim(a, block * width, width, axis=a.ndim - 1)


def kernel(x, positions, e_norm_w, e_w_in, e_q_norm_w, e_k_norm_w, e_conv_w, e_w_out, o_norm_w, o_w_in, o_pool_w, o_pool_scale, o_dconv_w, o_dconv_b, o_ln_w, o_ln_b, o_w_out, loss_target, m_e_norm_w, m_e_w_in, m_e_q_norm_w, m_e_k_norm_w, m_e_conv_w, m_e_w_out, m_o_norm_w, m_o_w_in, m_o_pool_w, m_o_pool_scale, m_o_dconv_w, m_o_dconv_b, m_o_ln_w, m_o_ln_b, m_o_w_out, v_e_norm_w, v_e_w_in, v_e_q_norm_w, v_e_k_norm_w, v_e_conv_w, v_e_w_out, v_o_norm_w, v_o_w_in, v_o_pool_w, v_o_pool_scale, v_o_dconv_w, v_o_dconv_b, v_o_ln_w, v_o_ln_b, v_o_w_out):
    given = dict(e_norm_w=e_norm_w, e_w_in=e_w_in, e_q_norm_w=e_q_norm_w, e_k_norm_w=e_k_norm_w, e_conv_w=e_conv_w,
                 e_w_out=e_w_out, o_norm_w=o_norm_w, o_w_in=o_w_in, o_pool_w=o_pool_w, o_pool_scale=o_pool_scale,
                 o_dconv_w=o_dconv_w, o_dconv_b=o_dconv_b, o_ln_w=o_ln_w, o_ln_b=o_ln_b, o_w_out=o_w_out)
    mom = dict(e_norm_w=m_e_norm_w, e_w_in=m_e_w_in, e_q_norm_w=m_e_q_norm_w, e_k_norm_w=m_e_k_norm_w,
               e_conv_w=m_e_conv_w, e_w_out=m_e_w_out, o_norm_w=m_o_norm_w, o_w_in=m_o_w_in, o_pool_w=m_o_pool_w,
               o_pool_scale=m_o_pool_scale, o_dconv_w=m_o_dconv_w, o_dconv_b=m_o_dconv_b, o_ln_w=m_o_ln_w,
               o_ln_b=m_o_ln_b, o_w_out=m_o_w_out)
    var = dict(e_norm_w=v_e_norm_w, e_w_in=v_e_w_in, e_q_norm_w=v_e_q_norm_w, e_k_norm_w=v_e_k_norm_w,
               e_conv_w=v_e_conv_w, e_w_out=v_e_w_out, o_norm_w=v_o_norm_w, o_w_in=v_o_w_in, o_pool_w=v_o_pool_w,
               o_pool_scale=v_o_pool_scale, o_dconv_w=v_o_dconv_w, o_dconv_b=v_o_dconv_b, o_ln_w=v_o_ln_w,
               o_ln_b=v_o_ln_b, o_w_out=v_o_w_out)
    S = x.shape[1]
    mx, my, mc = lax.axis_index("x"), lax.axis_index("y"), lax.axis_index("c")
    chip = 2 * mx + my
    cidx = jnp.reshape(mc, (1,)).astype(jnp.int32)
    bidx = jnp.reshape(chip, (1,)).astype(jnp.int32)

    full_b = _gather_weights([given[n][0].astype(BF16) for n in BIG])
    wb = dict(zip(BIG, full_b))
    shard_sizes = [int(np.prod(given[n].shape)) for n in SMALL_SHARDED]
    own = _pack([given[n] for n in SMALL_SHARDED])
    rows = own.shape[0]
    slots = jnp.zeros((N_CHIPS, rows, LANES), F32)
    own = jnp.where(mc == 0, own, 0.0)
    slots = lax.dynamic_update_slice(slots, own[None], (chip, 0, 0))
    gathered = _allreduce_small(slots.reshape(N_CHIPS * rows, LANES), "gather_small")
    gathered = gathered.reshape(N_CHIPS, rows * LANES)
    p = {}
    off = 0
    for n, size in zip(SMALL_SHARDED, shard_sizes):
        sh = given[n].shape[1:]
        parts = gathered[:, off:off + size].reshape((N_CHIPS,) + sh)
        fullp = jnp.moveaxis(parts, 0, -2).reshape(sh[:-1] + (N_CHIPS * sh[-1],))
        p[n] = fullp.reshape(-1, fullp.shape[-1])
        off += size
    p["e_norm_w"] = e_norm_w
    p["e_q_norm_w"] = e_q_norm_w
    p["e_k_norm_w"] = e_k_norm_w
    p["o_pool_w"] = o_pool_w[0]

    lsum, dx, g = _local_step(x[0], positions.reshape(S, 1), loss_target[0], wb, p)
    loss = lax.psum(0.5 * jnp.sum(lsum) / float(D_MODEL), ("x", "y", "c"))

    tot = _unpack(_allreduce_small(_pack([g[n] for n in SMALL]), "allreduce_small"), [g[n].shape for n in SMALL])
    gsmall = dict(zip(SMALL, tot))
    grads = {}
    for n in SMALL:
        gv = gsmall[n]
        if n in SMALL_SHARDED:
            gv = _gather_last(gv, chip, gv.shape[-1] // N_CHIPS)
        grads[n] = gv.reshape(given[n].shape)

    recv_half = _swap_to_sibling([g[n] for n in BIG], "swap_halves", True)
    halves = [_add_half(cidx, g[n], r, n) for n, r in zip(BIG, recv_half)]
    recv_blk = _scatter_blocks([hb for _, hb in halves])
    parts = [_add_blocks(bidx, h, r, n) for n, (h, _), r in zip(BIG, halves, recv_blk)]
    theirs = _swap_to_sibling(parts, "swap_reduced", False)

    delta, new_m, new_v = {}, {}, {}
    for n, mine, other in zip(BIG, parts, theirs):
        sh = given[n].shape
        gs, d, nm, nv = _adamw_halves(cidx, given[n][0], mine, other, mom[n][0], var[n][0], n)
        grads[n], delta[n], new_m[n], new_v[n] = gs.reshape(sh), d.reshape(sh), nm.reshape(sh), nv.reshape(sh)
    shapes = [given[n].shape for n in SMALL]
    packed = [_pack([src[n] for n in SMALL]) for src in (given, grads, mom, var)]
    for dst, pk in zip((delta, new_m, new_v), _adamw(*packed, "small")):
        for n, a in zip(SMALL, _unpack(pk, shapes)):
            dst[n] = a
    return (loss, dx[None], *[grads[n] for n in WEIGHTS], *[delta[n] for n in WEIGHTS],
            *[new_m[n] for n in WEIGHTS], *[new_v[n] for n in WEIGHTS])
```

```python
import numpy as np
import jax
import jax.numpy as jnp
from jax import lax
from jax.experimental import pallas as pl
from jax.experimental.pallas import tpu as pltpu

F32 = jnp.float32
BF16 = jnp.bfloat16

D_MODEL = 1024
HEAD_DIM = 64
A_WIDTH = 512
A_HEADS = 8
A_GROUPS = ((128, 1), (512, 4), (2048, 16))
QBLK = 128
ROT_DIM = 16
ROPE_THETA = 500000.0
POOL_SIZES = (2, 4, 8, 16)
D_CONV = 31
SC_WIDTH = 3
EVEN_IN = 7168
ODD_IN = 2560
EPS = 1e-6
NEG = -1e30
ADAM_LR, ADAM_B1, ADAM_B2, ADAM_EPS, ADAM_WD, ADAM_STEP = 0.001, 0.9, 0.999, 1e-08, 0.01, 10

LANES = 128
SUBLANES = 8
HALO = 32
VMEM_LIMIT = 52 * 1024 * 1024
MESH = pl.DeviceIdType.MESH
ANY = pl.BlockSpec(memory_space=pl.ANY)

NT_DIMS = (((1,), (1,)), ((), ()))
TN_DIMS = (((0,), (0,)), ((), ()))


def _call(body, name, grid, in_specs, out_specs, out_shape, scratch=(), sem=None, aliases=None):
    return pl.pallas_call(
        body, name=name, grid=grid, in_specs=in_specs, out_specs=out_specs, out_shape=out_shape,
        scratch_shapes=list(scratch), input_output_aliases=aliases or {},
        compiler_params=pltpu.CompilerParams(dimension_semantics=sem, vmem_limit_bytes=VMEM_LIMIT))


def _sig(v):
    return jax.nn.sigmoid(v)


def _dsilu(v, s):
    return s * (1.0 + v * (1.0 - s))


def _emit_u(u_ref, ut_ref, lo, hi):
    for k, v in enumerate((lo, hi)):
        u_ref[:, k * A_WIDTH:(k + 1) * A_WIDTH] = v.astype(BF16)
        ut_ref[k * A_WIDTH:(k + 1) * A_WIDTH, :] = v.T.astype(BF16)


def _cs8(v):
    return v.reshape(v.shape[0] // SUBLANES, SUBLANES, v.shape[1]).sum(axis=0)


def _seg_ones():
    r = lax.broadcasted_iota(jnp.int32, (LANES, LANES), 0) // HEAD_DIM
    c = lax.broadcasted_iota(jnp.int32, (LANES, LANES), 1) // HEAD_DIM
    return (r == c).astype(BF16)


def _segsum(v, ones):
    hi = v.astype(BF16)
    lo = (v - hi.astype(F32)).astype(BF16)
    return (jnp.dot(hi, ones, preferred_element_type=F32) + jnp.dot(lo, ones, preferred_element_type=F32))


def _rope_tables(pos_ref, freq_ref):
    ang = pos_ref[...].astype(F32) * freq_ref[...]
    cosv, sinv = jnp.cos(ang), jnp.sin(ang)
    lm = lax.broadcasted_iota(jnp.int32, ang.shape, 1) % HEAD_DIM
    half = ROT_DIM // 2
    c = jnp.where(lm < ROT_DIM, cosv, 1.0)
    s1 = jnp.where((lm >= half) & (lm < ROT_DIM), sinv, 0.0)
    s2 = jnp.where(lm < half, -sinv, 0.0)
    return c, s1, s2


def _freq_table():
    half = ROT_DIM // 2
    inv = ROPE_THETA ** (-np.arange(half, dtype=np.float64) / half)
    lane = np.arange(LANES) % HEAD_DIM
    f = np.where(lane < ROT_DIM, inv[lane % half], 0.0)
    return jnp.asarray(f.reshape(1, LANES), F32)


def _load_once(hbm_ref, vmem_ref, sem):
    @pl.when(pl.program_id(0) == 0)
    def _():
        cp = pltpu.make_async_copy(hbm_ref, vmem_ref, sem)
        cp.start()
        cp.wait()


def _inproj(x, nw, w, tm, tn, name, gather_names=(), gather_shards=()):
    S, N = x.shape[0], w.shape[1]
    ni = S // tm
    ng = len(gather_names)

    def body(*refs):
        x_ref, nw_ref, w_hbm = refs[:3]
        s_refs = refs[3:3 + ng]
        o_ref, ht_ref = refs[3 + ng:5 + ng]
        f_refs = refs[5 + ng:5 + 2 * ng]
        w_v, sem = refs[5 + 2 * ng:7 + 2 * ng]
        if ng:
            i = pl.program_id(0)
            g = _Gather(gather_names, s_refs, f_refs, *refs[7 + 2 * ng:])
            pl.when(i == 0)(g.begin)
            pl.when(i == ni // 2)(g.relay)
        _load_once(w_hbm, w_v, sem)
        xv = x_ref[...]
        ms = jnp.mean(xv * xv, axis=-1, keepdims=True)
        h = xv * lax.rsqrt(ms + EPS) * nw_ref[...]
        ht_ref[...] = h.T.astype(BF16)
        hb = h.astype(BF16)
        for j in range(N // tn):
            o_ref[:, j * tn:(j + 1) * tn] = jnp.dot(hb, w_v[:, j * tn:(j + 1) * tn], preferred_element_type=F32)
        if ng:
            pl.when(i == ni - 1)(g.end)

    return _call(
        body, name, (ni,),
        [pl.BlockSpec((tm, D_MODEL), lambda i: (i, 0)),
         pl.BlockSpec((1, D_MODEL), lambda i: (0, 0)), ANY] + [ANY] * ng,
        [pl.BlockSpec((tm, N), lambda i: (i, 0)),
         pl.BlockSpec((D_MODEL, tm), lambda i: (0, i))] + [ANY] * ng,
        [jax.ShapeDtypeStruct((S, N), F32), jax.ShapeDtypeStruct((D_MODEL, S), BF16)] +
        [_full_shape(n, s) for n, s in zip(gather_names, gather_shards)],
        scratch=[pltpu.VMEM(w.shape, BF16), pltpu.SemaphoreType.DMA(())] + (_gather_sems(gather_names) if ng else []),
        sem=("arbitrary",))(x, nw, w, *gather_shards)


def _outproj(x, u, w, tm, name):
    S = x.shape[0]

    def body(x_ref, u_ref, w_ref, o_ref):
        o_ref[...] = x_ref[...] + jnp.dot(u_ref[...].astype(BF16), w_ref[...], preferred_element_type=F32)

    return _call(
        body, name, (S // tm,),
        [pl.BlockSpec((tm, D_MODEL), lambda i: (i, 0)),
         pl.BlockSpec((tm, D_MODEL), lambda i: (i, 0)),
         pl.BlockSpec((D_MODEL, D_MODEL), lambda i: (0, 0))],
        pl.BlockSpec((tm, D_MODEL), lambda i: (i, 0)),
        jax.ShapeDtypeStruct((S, D_MODEL), F32), sem=("parallel",))(x, u, w)


def _outproj_loss(x, u, w, tgt, tm, name):
    S = x.shape[0]

    def body(x_ref, u_ref, w_ref, t_ref, dy_ref, l_ref, acc):
        i = pl.program_id(0)

        @pl.when(i == 0)
        def _():
            acc[...] = jnp.zeros_like(acc)
        y = x_ref[...] + jnp.dot(u_ref[...].astype(BF16), w_ref[...], preferred_element_type=F32)
        diff = y - t_ref[...]
        dy_ref[...] = diff / float(D_MODEL)
        acc[...] += _cs8(diff * diff)

        @pl.when(i == pl.num_programs(0) - 1)
        def _():
            l_ref[...] = jnp.sum(acc[...], axis=0, keepdims=True)

    return _call(
        body, name, (S // tm,),
        [pl.BlockSpec((tm, D_MODEL), lambda i: (i, 0)),
         pl.BlockSpec((tm, D_MODEL), lambda i: (i, 0)),
         pl.BlockSpec((D_MODEL, D_MODEL), lambda i: (0, 0)),
         pl.BlockSpec((tm, D_MODEL), lambda i: (i, 0))],
        [pl.BlockSpec((tm, D_MODEL), lambda i: (i, 0)),
         pl.BlockSpec((1, D_MODEL), lambda i: (0, 0))],
        [jax.ShapeDtypeStruct((S, D_MODEL), F32), jax.ShapeDtypeStruct((1, D_MODEL), F32)],
        scratch=[pltpu.VMEM((SUBLANES, D_MODEL), F32)], sem=("arbitrary",))(x, u, w, tgt)


def _mm_nt(a, w, tm, name):
    S, K = a.shape
    N = w.shape[0]

    def body(a_ref, w_ref, o_ref):
        o_ref[...] = lax.dot_general(a_ref[...].astype(BF16), w_ref[...], NT_DIMS, preferred_element_type=F32)

    return _call(
        body, name, (S // tm,),
        [pl.BlockSpec((tm, K), lambda i: (i, 0)), pl.BlockSpec((N, K), lambda i: (0, 0))],
        pl.BlockSpec((tm, N), lambda i: (i, 0)),
        jax.ShapeDtypeStruct((S, N), F32), sem=("parallel",))(a, w)


def _piece_blocks(pieces, tk, axis):
    starts, counts, s = [], [], 0
    for p in pieces:
        n = p.shape[axis] // tk
        starts.append(s)
        counts.append(n)
        s += n
    return starts, counts, s


def _mm_nt_rms(pieces, w, x, nw, dres, tm, name, scatter_names=(), scatter_halves=()):
    S = x.shape[0]
    npc = len(pieces)
    ni = S // tm
    ns = len(scatter_names)
    offs = np.cumsum([0] + [p.shape[1] for p in pieces]).tolist()

    def body(*refs):
        p_refs = refs[:npc]
        w_hbm, x_ref, nw_ref, dr_ref = refs[npc:npc + 4]
        h_refs = refs[npc + 4:npc + 4 + ns]
        dx_ref, dnw_ref = refs[npc + 4 + ns:npc + 6 + ns]
        r_refs = refs[npc + 6 + ns:npc + 6 + 2 * ns]
        w_v, sem, nacc = refs[npc + 6 + 2 * ns:npc + 9 + 2 * ns]
        i = pl.program_id(0)
        if ns:
            send, recv = refs[npc + 9 + 2 * ns:]

            @pl.when(i == 0)
            def _():
                for cp in _scatter_copies(scatter_names, h_refs, r_refs, send, recv):
                    cp.start()
        _load_once(w_hbm, w_v, sem)

        @pl.when(i == 0)
        def _():
            nacc[...] = jnp.zeros_like(nacc)

        dh = None
        for p in range(npc):
            part = lax.dot_general(p_refs[p][...].astype(BF16), w_v[:, offs[p]:offs[p + 1]], NT_DIMS,
                                   preferred_element_type=F32)
            dh = part if dh is None else dh + part
        xv = x_ref[...]
        rs = lax.rsqrt(jnp.mean(xv * xv, axis=-1, keepdims=True) + EPS)
        xh = xv * rs
        nacc[...] += _cs8(dh * xh)
        dxh = dh * nw_ref[...]
        dx_ref[...] = dr_ref[...] + rs * (dxh - xh * jnp.mean(dxh * xh, axis=-1, keepdims=True))

        @pl.when(i == ni - 1)
        def _():
            dnw_ref[...] = jnp.sum(nacc[...], axis=0, keepdims=True)
            if ns:
                for cp in _scatter_copies(scatter_names, h_refs, r_refs, send, recv):
                    cp.wait()

    row = pl.BlockSpec((tm, D_MODEL), lambda i: (i, 0))
    outs = _call(
        body, name, (ni,),
        [pl.BlockSpec((tm, p.shape[1]), lambda i: (i, 0)) for p in pieces] +
        [ANY, row, pl.BlockSpec((1, D_MODEL), lambda i: (0, 0)), row] + [ANY] * ns,
        [row, pl.BlockSpec((1, D_MODEL), lambda i: (0, 0))] + [ANY] * ns,
        [jax.ShapeDtypeStruct((S, D_MODEL), F32), jax.ShapeDtypeStruct((1, D_MODEL), F32)] +
        [jax.ShapeDtypeStruct((3,) + _shard_shape(h.shape, n), h.dtype) for n, h in zip(scatter_names, scatter_halves)],
        scratch=[pltpu.VMEM(w.shape, BF16), pltpu.SemaphoreType.DMA(()), pltpu.VMEM((SUBLANES, D_MODEL), F32)] +
        (_scatter_sems(scatter_names) if ns else []),
        sem=("arbitrary",))(*pieces, w, x, nw, dres, *scatter_halves)
    return outs[0], outs[1], list(outs[2:])


def _mm_wgrad(at, pieces, tn, name):
    M, S = at.shape
    starts, counts, nj = _piece_blocks(pieces, tn, 1)
    npc = len(pieces)

    def body(*refs):
        a_hbm = refs[0]
        p_refs = refs[1:1 + npc]
        o_ref, a_v, sem = refs[1 + npc:]
        j = pl.program_id(0)
        _load_once(a_hbm, a_v, sem)
        for p in range(npc):
            @pl.when((j >= starts[p]) & (j < starts[p] + counts[p]))
            def _(p=p):
                o_ref[...] = jnp.dot(a_v[...], p_refs[p][...].astype(BF16), preferred_element_type=F32)

    def pspec(p):
        return pl.BlockSpec((S, tn), lambda j: (0, jnp.clip(j - starts[p], 0, counts[p] - 1)))

    return _call(
        body, name, (nj,),
        [ANY] + [pspec(p) for p in range(npc)],
        pl.BlockSpec((M, tn), lambda j: (0, j)),
        jax.ShapeDtypeStruct((M, nj * tn), F32),
        scratch=[pltpu.VMEM(at.shape, BF16), pltpu.SemaphoreType.DMA(())], sem=("arbitrary",))(at, *pieces)


def _stream_spec(d, T):
    return pl.BlockSpec((d, T // d, A_WIDTH), lambda i: (0, i, 0))


def _stream_shape(d, S, dtype):
    return jax.ShapeDtypeStruct((d, S // d, A_WIDTH), dtype)


N_CHUNK = A_WIDTH // LANES


def _to_tokens(ref, scr, d, T):
    if d == 1:
        return ref[0].astype(F32)
    for r in range(d):
        for ch in range(N_CHUNK):
            scr.at[ch][pl.ds(r, T // d, stride=d), :] = ref[r, :, ch * LANES:(ch + 1) * LANES].astype(F32)
    return _get(scr)


def _from_tokens(out_ref, scr, d, T):
    for r in range(d):
        for ch in range(N_CHUNK):
            out_ref[r, :, ch * LANES:(ch + 1) * LANES] = scr.at[ch][pl.ds(r, T // d, stride=d), :].astype(out_ref.dtype)


def _put(scr, val):
    for ch in range(N_CHUNK):
        scr[ch] = val[:, ch * LANES:(ch + 1) * LANES]


def _get(scr):
    return jnp.concatenate([scr[ch] for ch in range(N_CHUNK)], axis=1)


def _chunked(T):
    return pltpu.VMEM((N_CHUNK, T, LANES), F32)


def _compact_spec(d, T):
    return pl.BlockSpec((d, T // d, LANES), lambda i: (0, i, 0))


def _compact_shape(d, S):
    return jax.ShapeDtypeStruct((d, S // d, LANES), F32)


def _compact_to_tokens(ref, scr, d, T):
    if d == 1:
        return ref[0]
    for r in range(d):
        scr[pl.ds(r, T // d, stride=d), :] = ref[r]
    return scr[...]


def _compact_from_tokens(out_ref, scr, val, d, T):
    if d == 1:
        out_ref[0] = val
        return
    scr[...] = val
    for r in range(d):
        out_ref[r] = scr[pl.ds(r, T // d, stride=d), :]


def _head_expander():
    r = lax.broadcasted_iota(jnp.int32, (LANES, A_WIDTH), 0)
    c = lax.broadcasted_iota(jnp.int32, (LANES, A_WIDTH), 1) // HEAD_DIM
    return (r == c).astype(BF16)


def _head_reducer():
    r = lax.broadcasted_iota(jnp.int32, (A_WIDTH, LANES), 0) // HEAD_DIM
    c = lax.broadcasted_iota(jnp.int32, (A_WIDTH, LANES), 1)
    return (r == c).astype(BF16)


def _qkv_prep(proj, pos, freq, wq, wk, T):
    S = proj.shape[0]
    qk_w = 3 * A_WIDTH

    def body(q_ref, k_ref, v_ref, pos_ref, f_ref, wq_ref, wk_ref, *rest):
        outs, scr = rest[:9], rest[9]
        ones = _seg_ones()
        c, s1, s2 = _rope_tables(pos_ref, f_ref)
        for t, (src, w_ref) in enumerate(((q_ref, wq_ref), (k_ref, wk_ref), (v_ref, None))):
            for g in range(3):
                d = A_GROUPS[g][1]
                out = outs[3 * t + g]
                for ch in range(A_WIDTH // LANES):
                    cs = slice(ch * LANES, (ch + 1) * LANES)
                    v = src[:, g * A_WIDTH + ch * LANES: g * A_WIDTH + (ch + 1) * LANES]
                    if w_ref is not None:
                        rs = lax.rsqrt(_segsum(v * v, ones) * (1.0 / HEAD_DIM) + EPS)
                        y = v * rs * w_ref[...]
                        v = y * c + pltpu.roll(y, 8, 1) * s1 + pltpu.roll(y, LANES - 8, 1) * s2
                    if d == 1:
                        out[0, :, cs] = v.astype(BF16)
                    else:
                        scr[ch] = v
                if d > 1:
                    _from_tokens(out, scr, d, T)

    ds_ = [A_GROUPS[g][1] for g in range(3)] * 3
    return _call(
        body, "qkv_prep", (S // T,),
        [pl.BlockSpec((T, qk_w), lambda i: (i, 0)), pl.BlockSpec((T, qk_w), lambda i: (i, 1)),
         pl.BlockSpec((T, qk_w), lambda i: (i, 2)),
         pl.BlockSpec((T, 1), lambda i: (i, 0)), pl.BlockSpec((1, LANES), lambda i: (0, 0)),
         pl.BlockSpec((1, LANES), lambda i: (0, 0)), pl.BlockSpec((1, LANES), lambda i: (0, 0))],
        [_stream_spec(d, T) for d in ds_],
        [_stream_shape(d, S, BF16) for d in ds_],
        scratch=[_chunked(T)], sem=("parallel",))(proj, proj, proj, pos, freq, wq, wk)


def _attn_mask(i):
    qi = lax.broadcasted_iota(jnp.int32, (QBLK, 2 * QBLK), 0) + QBLK
    kj = lax.broadcasted_iota(jnp.int32, (QBLK, 2 * QBLK), 1)
    dist = qi - kj
    return (dist >= 0) & (dist <= QBLK) & ((i > 0) | (kj >= QBLK))


ATT_BLK = (None, QBLK, A_WIDTH)
ATT_CBLK = (None, QBLK, LANES)


def _first_head_lanes():
    return lax.broadcasted_iota(jnp.int32, (1, LANES), 1) < HEAD_DIM


def _split_heads(v, first):
    zero = jnp.zeros_like(v)
    return jnp.where(first, v, zero), jnp.where(first, zero, v)


def _attn_fwd(q, k, v, g):
    d, n, _ = q.shape
    nb = n // QBLK

    def body(q_ref, kp_ref, kc_ref, vp_ref, vc_ref, o_ref, l_ref, s_scr, p_scr):
        i = pl.program_id(1)
        mask = _attn_mask(i)
        first = _first_head_lanes()
        for pr in range(A_HEADS // 2):
            ps = slice(pr * LANES, (pr + 1) * LANES)
            kc = jnp.concatenate([kp_ref[:, ps], kc_ref[:, ps]], axis=0)
            for e, qh in enumerate(_split_heads(q_ref[:, ps], first)):
                s_scr[2 * pr + e] = lax.dot_general(qh, kc, NT_DIMS, preferred_element_type=F32)
        lane = lax.broadcasted_iota(jnp.int32, (1, LANES), 1)
        lrow = jnp.zeros((QBLK, LANES), F32)
        for h in range(A_HEADS):
            s = jnp.where(mask, s_scr[h] * (HEAD_DIM ** -0.5), NEG)
            m = jnp.max(s, axis=-1, keepdims=True)
            p = jnp.exp(s - m)
            den = jnp.sum(p, axis=-1, keepdims=True)
            p_scr[h] = (p / den).astype(BF16)
            lrow = jnp.where(lane == h, m + jnp.log(den), lrow)
        l_ref[...] = lrow
        for pr in range(A_HEADS // 2):
            ps = slice(pr * LANES, (pr + 1) * LANES)
            va, vb = _split_heads(jnp.concatenate([vp_ref[:, ps], vc_ref[:, ps]], axis=0), first)
            o_ref[:, ps] = (jnp.dot(p_scr[2 * pr], va, preferred_element_type=F32) +
                            jnp.dot(p_scr[2 * pr + 1], vb, preferred_element_type=F32)).astype(BF16)

    prev = lambda r, i: (r, jnp.maximum(i - 1, 0), 0)
    cur = lambda r, i: (r, i, 0)
    return _call(
        body, "attn_fwd_g%d" % g, (d, nb),
        [pl.BlockSpec(ATT_BLK, cur), pl.BlockSpec(ATT_BLK, prev), pl.BlockSpec(ATT_BLK, cur),
         pl.BlockSpec(ATT_BLK, prev), pl.BlockSpec(ATT_BLK, cur)],
        [pl.BlockSpec(ATT_BLK, cur), pl.BlockSpec(ATT_CBLK, cur)],
        [jax.ShapeDtypeStruct((d, n, A_WIDTH), BF16), jax.ShapeDtypeStruct((d, n, LANES), F32)],
        scratch=[pltpu.VMEM((A_HEADS, QBLK, 2 * QBLK), F32), pltpu.VMEM((A_HEADS, QBLK, 2 * QBLK), BF16)],
        sem=("parallel", "parallel"))(q, k, k, v, v)


def _attn_bwd(q, k, v, do, lse, cg, g):
    d, n, _ = q.shape
    nb = n // QBLK
    scale = HEAD_DIM ** -0.5

    def body(q_ref, kp_ref, kc_ref, vp_ref, vc_ref, do_ref, l_ref, c_ref, dq_ref, dk_ref, dv_ref, ck, cv,
             s_scr, dp_scr, p_scr, ds_scr):
        i = pl.program_id(1)

        @pl.when(i == 0)
        def _():
            ck[...] = jnp.zeros_like(ck)
            cv[...] = jnp.zeros_like(cv)

        @pl.when(i < nb)
        def _():
            mask = _attn_mask(i)
            first = _first_head_lanes()
            for pr in range(A_HEADS // 2):
                ps = slice(pr * LANES, (pr + 1) * LANES)
                kc = jnp.concatenate([kp_ref[:, ps], kc_ref[:, ps]], axis=0)
                vc = jnp.concatenate([vp_ref[:, ps], vc_ref[:, ps]], axis=0)
                qs = _split_heads(q_ref[:, ps], first)
                dos = _split_heads(do_ref[:, ps], first)
                for e in range(2):
                    s_scr[2 * pr + e] = lax.dot_general(qs[e], kc, NT_DIMS, preferred_element_type=F32)
                    dp_scr[2 * pr + e] = lax.dot_general(dos[e], vc, NT_DIMS, preferred_element_type=F32)
            for h in range(A_HEADS):
                p = jnp.where(mask, jnp.exp(s_scr[h] * scale - l_ref[:, h:h + 1]), 0.0)
                p_scr[h] = p.astype(BF16)
                ds_scr[h] = (p * (dp_scr[h] + c_ref[:, h:h + 1]) * scale).astype(BF16)
            for pr in range(A_HEADS // 2):
                ps = slice(pr * LANES, (pr + 1) * LANES)
                ks = _split_heads(jnp.concatenate([kp_ref[:, ps], kc_ref[:, ps]], axis=0), first)
                qs = _split_heads(q_ref[:, ps], first)
                dos = _split_heads(do_ref[:, ps], first)
                dq = dkc = dvc = None
                for e in range(2):
                    ds = ds_scr[2 * pr + e]
                    a = jnp.dot(ds, ks[e], preferred_element_type=F32)
                    b = lax.dot_general(ds, qs[e], TN_DIMS, preferred_element_type=F32)
                    c = lax.dot_general(p_scr[2 * pr + e], dos[e], TN_DIMS, preferred_element_type=F32)
                    dq, dkc, dvc = (a, b, c) if e == 0 else (dq + a, dkc + b, dvc + c)
                dq_ref[:, ps] = dq.astype(BF16)
                dk_ref[:, ps] = (ck[:, ps] + dkc[:QBLK]).astype(BF16)
                dv_ref[:, ps] = (cv[:, ps] + dvc[:QBLK]).astype(BF16)
                ck[:, ps] = dkc[QBLK:]
                cv[:, ps] = dvc[QBLK:]

        @pl.when(i == nb)
        def _():
            dk_ref[...] = ck[...].astype(BF16)
            dv_ref[...] = cv[...].astype(BF16)

    qi = lambda i: jnp.minimum(i, nb - 1)
    cur = lambda r, i: (r, qi(i), 0)
    prev = lambda r, i: (r, jnp.maximum(qi(i) - 1, 0), 0)
    late = lambda r, i: (r, jnp.maximum(i - 1, 0), 0)
    return _call(
        body, "attn_bwd_g%d" % g, (d, nb + 1),
        [pl.BlockSpec(ATT_BLK, cur), pl.BlockSpec(ATT_BLK, prev), pl.BlockSpec(ATT_BLK, cur),
         pl.BlockSpec(ATT_BLK, prev), pl.BlockSpec(ATT_BLK, cur),
         pl.BlockSpec(ATT_BLK, cur), pl.BlockSpec(ATT_CBLK, cur), pl.BlockSpec(ATT_CBLK, cur)],
        [pl.BlockSpec(ATT_BLK, cur), pl.BlockSpec(ATT_BLK, late), pl.BlockSpec(ATT_BLK, late)],
        [jax.ShapeDtypeStruct((d, n, A_WIDTH), BF16)] * 3,
        scratch=[pltpu.VMEM((QBLK, A_WIDTH), F32), pltpu.VMEM((QBLK, A_WIDTH), F32),
                 pltpu.VMEM((A_HEADS, QBLK, 2 * QBLK), F32), pltpu.VMEM((A_HEADS, QBLK, 2 * QBLK), F32),
                 pltpu.VMEM((A_HEADS, QBLK, 2 * QBLK), BF16), pltpu.VMEM((A_HEADS, QBLK, 2 * QBLK), BF16)],
        sem=("parallel", "arbitrary"))(q, k, k, v, v, do, lse, cg)


def _merge_weights(l0, l1, l2):
    mx = jnp.maximum(jnp.maximum(l0, l1), l2)
    e0, e1, e2 = jnp.exp(l0 - mx), jnp.exp(l1 - mx), jnp.exp(l2 - mx)
    den = e0 + e1 + e2
    return e0 / den, e1 / den, e2 / den


def _even_specs(T, S):
    t8 = T // SUBLANES
    last8 = S // SUBLANES - 1
    col = lambda c: pl.BlockSpec((T, A_WIDTH), lambda i: (i, c))
    prev8 = lambda c: pl.BlockSpec((SUBLANES, A_WIDTH), lambda i: (jnp.maximum(i * t8 - 1, 0), c))
    next8 = lambda c: pl.BlockSpec((SUBLANES, A_WIDTH), lambda i: (jnp.minimum((i + 1) * t8, last8), c))
    return col, prev8, next8


GROUP_D = tuple(d for _, d in A_GROUPS)


def _even_mixer_fwd(proj, os_, ls_, conv_w, T):
    S = proj.shape[0]
    col, prev8, _ = _even_specs(T, S)
    H = SUBLANES

    def body(bg_r, cg_r, hb_r, zl_r, zh_r, cgp_r, hbp_r, o0, o1, o2, l0, l1, l2, cw_r, u_ref, ut_ref, ext,
             cscr, *scr):
        i = pl.program_id(0)
        ls = [_compact_to_tokens(r, cscr, GROUP_D[g], T) for g, r in enumerate((l0, l1, l2))]
        expand = _head_expander()
        ws = [_segsum(w, expand) for w in _merge_weights(*ls)]
        oa = ws[0] * _to_tokens(o0, scr[0], GROUP_D[0], T)
        oa = oa + ws[1] * _to_tokens(o1, scr[1], GROUP_D[1], T)
        oa = oa + ws[2] * _to_tokens(o2, scr[2], GROUP_D[2], T)
        ext[0:H, :] = jnp.where(i == 0, 0.0, cgp_r[...] * hbp_r[...])
        ext[H:H + T, :] = cg_r[...] * hb_r[...]
        conv = cw_r[0:1, :] * ext[H - 2:H - 2 + T, :]
        for kk in range(1, SC_WIDTH):
            conv = conv + cw_r[kk:kk + 1, :] * ext[H - 2 + kk:H - 2 + kk + T, :]
        zl, zh = zl_r[...], zh_r[...]
        _emit_u(u_ref, ut_ref, oa * (zl * _sig(zl)), bg_r[...] * conv * (zh * _sig(zh)))

    streams = [_stream_spec(d, T) for d in GROUP_D]
    compacts = [_compact_spec(d, T) for d in GROUP_D]
    return _call(
        body, "even_mixer_fwd", (S // T,),
        [col(9), col(10), col(11), col(12), col(13), prev8(10), prev8(11)] + streams + compacts +
        [pl.BlockSpec((SC_WIDTH, A_WIDTH), lambda i: (0, 0))],
        [pl.BlockSpec((T, D_MODEL), lambda i: (i, 0)), pl.BlockSpec((D_MODEL, T), lambda i: (0, i))],
        [jax.ShapeDtypeStruct((S, D_MODEL), BF16), jax.ShapeDtypeStruct((D_MODEL, S), BF16)],
        scratch=[pltpu.VMEM((T + H, A_WIDTH), F32), pltpu.VMEM((T, LANES), F32)] + [_chunked(T)] * 3,
        sem=("parallel",))(
            proj, proj, proj, proj, proj, proj, proj, *os_, *ls_, conv_w)


def _even_mixer_bwd(du, proj, os_, ls_, conv_w, T):
    S = proj.shape[0]
    nt = S // T
    col, prev8, next8 = _even_specs(T, S)
    H = SUBLANES
    t8 = T // SUBLANES
    last8 = S // SUBLANES - 1

    def body(dul_r, duh_r, bg_r, cg_r, hb_r, zl_r, zh_r, cgp_r, hbp_r, dun_r, zhn_r, bgn_r,
             o0, o1, o2, l0, l1, l2, cw_r,
             do0, do1, do2, c0, c1, c2, dr_ref, dcw_ref, ext_t, ext_d, acc, cscr, s_a, s_b, s_c):
        i = pl.program_id(0)

        @pl.when(i == 0)
        def _():
            acc[...] = jnp.zeros_like(acc)

        zl, zh = zl_r[...], zh_r[...]
        sl, sh = _sig(zl), _sig(zh)
        dul, duh = dul_r[...], duh_r[...]
        scr = (s_a, s_b, s_c)
        ls = [_compact_to_tokens(r, cscr, GROUP_D[g], T) for g, r in enumerate((l0, l1, l2))]
        wcs = _merge_weights(*ls)
        expand = _head_expander()
        ws = [_segsum(w, expand) for w in wcs]
        oa = ws[0] * _to_tokens(o0, scr[0], GROUP_D[0], T)
        oa = oa + ws[1] * _to_tokens(o1, scr[1], GROUP_D[1], T)
        oa = oa + ws[2] * _to_tokens(o2, scr[2], GROUP_D[2], T)
        doa = dul * (zl * sl)
        rsum = _segsum(doa * oa, _head_reducer())
        for g, (do_ref, c_ref) in enumerate(((do0, c0), (do1, c1), (do2, c2))):
            d = GROUP_D[g]
            _compact_from_tokens(c_ref, cscr, -wcs[g] * rsum, d, T)
            if d == 1:
                do_ref[0] = (ws[g] * doa).astype(BF16)
            else:
                _put(s_c, ws[g] * doa)
                _from_tokens(do_ref, s_c, d, T)
        cgv, hbv, bgv = cg_r[...], hb_r[...], bg_r[...]
        ext_t[0:H, :] = jnp.where(i == 0, 0.0, cgp_r[...] * hbp_r[...])
        ext_t[H:H + T, :] = cgv * hbv
        conv = cw_r[0:1, :] * ext_t[H - 2:H - 2 + T, :]
        for kk in range(1, SC_WIDTH):
            conv = conv + cw_r[kk:kk + 1, :] * ext_t[H - 2 + kk:H - 2 + kk + T, :]
        dyb = duh * (zh * sh)
        dconv = dyb * bgv
        zn = zhn_r[...]
        ext_d[0:T, :] = dconv
        ext_d[T:T + H, :] = jnp.where(i == nt - 1, 0.0, dun_r[...] * (zn * _sig(zn)) * bgn_r[...])
        dt = cw_r[0:1, :] * ext_d[2:2 + T, :]
        for kk in range(1, SC_WIDTH):
            dt = dt + cw_r[kk:kk + 1, :] * ext_d[2 - kk:2 - kk + T, :]
        for kk in range(SC_WIDTH):
            acc[kk * SUBLANES:(kk + 1) * SUBLANES, :] += _cs8(dconv * ext_t[H - 2 + kk:H - 2 + kk + T, :])
        dr_ref[:, 0:A_WIDTH] = (dyb * conv).astype(BF16)
        dr_ref[:, A_WIDTH:2 * A_WIDTH] = (dt * hbv).astype(BF16)
        dr_ref[:, 2 * A_WIDTH:3 * A_WIDTH] = (dt * cgv).astype(BF16)
        dr_ref[:, 3 * A_WIDTH:4 * A_WIDTH] = (dul * oa * _dsilu(zl, sl)).astype(BF16)
        dr_ref[:, 4 * A_WIDTH:5 * A_WIDTH] = (duh * (bgv * conv) * _dsilu(zh, sh)).astype(BF16)

        @pl.when(i == nt - 1)
        def _():
            for kk in range(SC_WIDTH):
                dcw_ref[kk:kk + 1, :] = jnp.sum(acc[kk * SUBLANES:(kk + 1) * SUBLANES, :], axis=0, keepdims=True)

    streams = [_stream_spec(d, T) for d in GROUP_D]
    dunext = pl.BlockSpec((SUBLANES, A_WIDTH), lambda i: (jnp.minimum((i + 1) * t8, last8), 1))
    compacts = [_compact_spec(d, T) for d in GROUP_D]
    outs = _call(
        body, "even_mixer_bwd", (nt,),
        [pl.BlockSpec((T, A_WIDTH), lambda i: (i, 0)), pl.BlockSpec((T, A_WIDTH), lambda i: (i, 1)),
         col(9), col(10), col(11), col(12), col(13), prev8(10), prev8(11), dunext, next8(13), next8(9)] +
        streams + compacts + [pl.BlockSpec((SC_WIDTH, A_WIDTH), lambda i: (0, 0))],
        streams + compacts + [pl.BlockSpec((T, 5 * A_WIDTH), lambda i: (i, 0)),
                              pl.BlockSpec((SC_WIDTH, A_WIDTH), lambda i: (0, 0))],
        [_stream_shape(d, S, BF16) for d in GROUP_D] + [_compact_shape(d, S) for d in GROUP_D] +
        [jax.ShapeDtypeStruct((S, 5 * A_WIDTH), BF16), jax.ShapeDtypeStruct((SC_WIDTH, A_WIDTH), F32)],
        scratch=[pltpu.VMEM((T + H, A_WIDTH), F32), pltpu.VMEM((T + H, A_WIDTH), F32),
                 pltpu.VMEM((SC_WIDTH * SUBLANES, A_WIDTH), F32), pltpu.VMEM((T, LANES), F32)] +
                [_chunked(T)] * 3,
        sem=("arbitrary",))(du, du, proj, proj, proj, proj, proj, proj, proj, du, proj, proj, *os_, *ls_, conv_w)
    return outs[0:3], outs[3:6], outs[6], outs[7]


def _qk_bwd(proj, dqs, dks, dvs, pos, freq, wq, wk, T):
    S = proj.shape[0]
    nt = S // T
    qk_w = 3 * A_WIDTH

    def body(q_ref, k_ref, dq0, dq1, dq2, dk0, dk1, dk2, dv0, dv1, dv2, pos_ref, f_ref, wq_ref, wk_ref,
             o_ref, dw_ref, acc, scr):
        i = pl.program_id(0)

        @pl.when(i == 0)
        def _():
            acc[...] = jnp.zeros_like(acc)
            dw_ref[...] = jnp.zeros_like(dw_ref)

        ones = _seg_ones()
        c, s1, s2 = _rope_tables(pos_ref, f_ref)
        for t, (src, w_ref, ds) in enumerate(((q_ref, wq_ref, (dq0, dq1, dq2)), (k_ref, wk_ref, (dk0, dk1, dk2)))):
            wv = w_ref[...]
            for g in range(3):
                d = GROUP_D[g]
                if d > 1:
                    _to_tokens(ds[g], scr, d, T)
                for ch in range(A_WIDTH // LANES):
                    cs = slice(g * A_WIDTH + ch * LANES, g * A_WIDTH + (ch + 1) * LANES)
                    lc = slice(ch * LANES, (ch + 1) * LANES)
                    v = src[:, cs]
                    dout = ds[g][0, :, lc].astype(F32) if d == 1 else scr[ch]
                    rs = lax.rsqrt(_segsum(v * v, ones) * (1.0 / HEAD_DIM) + EPS)
                    xh = v * rs
                    dy = dout * c + pltpu.roll(dout * s1, LANES - 8, 1) + pltpu.roll(dout * s2, 8, 1)
                    acc[t * SUBLANES:(t + 1) * SUBLANES, :] += _cs8(dy * xh)
                    dxh = dy * wv
                    mean = _segsum(dxh * xh, ones) * (1.0 / HEAD_DIM)
                    o_ref[:, t * qk_w + g * A_WIDTH + ch * LANES: t * qk_w + g * A_WIDTH + (ch + 1) * LANES] = (
                        rs * (dxh - xh * mean)).astype(BF16)
        for g, dv in enumerate((dv0, dv1, dv2)):
            d = GROUP_D[g]
            base = 2 * qk_w + g * A_WIDTH
            o_ref[:, base:base + A_WIDTH] = _to_tokens(dv, scr, d, T).astype(BF16)

        @pl.when(i == nt - 1)
        def _():
            for t in range(2):
                srow = jnp.sum(acc[t * SUBLANES:(t + 1) * SUBLANES, :], axis=0, keepdims=True)
                dw_ref[t:t + 1, :] = srow + pltpu.roll(srow, HEAD_DIM, 1)

    streams = [_stream_spec(d, T) for d in GROUP_D]
    return _call(
        body, "qk_bwd", (nt,),
        [pl.BlockSpec((T, qk_w), lambda i: (i, 0)), pl.BlockSpec((T, qk_w), lambda i: (i, 1))] + streams * 3 +
        [pl.BlockSpec((T, 1), lambda i: (i, 0)), pl.BlockSpec((1, LANES), lambda i: (0, 0)),
         pl.BlockSpec((1, LANES), lambda i: (0, 0)), pl.BlockSpec((1, LANES), lambda i: (0, 0))],
        [pl.BlockSpec((T, 3 * qk_w), lambda i: (i, 0)), pl.BlockSpec((SUBLANES, LANES), lambda i: (0, 0))],
        [jax.ShapeDtypeStruct((S, 3 * qk_w), BF16), jax.ShapeDtypeStruct((SUBLANES, LANES), F32)],
        scratch=[pltpu.VMEM((2 * SUBLANES, LANES), F32), _chunked(T)], sem=("arbitrary",))(
            proj, proj, *dqs, *dks, *dvs, pos, freq, wq, wk)


N_SMALL_ODD = 40
SHIFT_ROWS_LESS = SUBLANES


def _fill_shifted(ext_ref, sh_ref):
    rows = ext_ref.shape[0] - SHIFT_ROWS_LESS
    for b in range(1, SUBLANES):
        sh_ref[b - 1] = ext_ref[b:b + rows, :]


def _window(ext_ref, sh_ref, off, T):
    a, b = divmod(off, SUBLANES)
    if b == 0:
        return ext_ref[off:off + T, :]
    return sh_ref[b - 1, a * SUBLANES:a * SUBLANES + T, :]


def _odd_pool_tile(i, uc_r, ucp_r, pw_r, ext_u, pooled_s, pm_s, T):
    H = HALO
    uc = uc_r[...]
    ext_u[0:H, :] = jnp.where(i == 0, 0.0, ucp_r[...])
    ext_u[H:H + T, :] = uc
    row = i * T + lax.broadcasted_iota(jnp.int32, (T, 1), 0)
    for g, p in enumerate(POOL_SIZES):
        cs = slice(g * LANES, (g + 1) * LANES)
        win = ext_u[H:H + T, cs]
        for j in range(1, p):
            win = win + ext_u[H - j:H - j + T, cs]
        cnt = jnp.minimum(row + 1, p).astype(F32)
        pooled = win / cnt - uc[:, cs]
        pooled_s[:, cs] = pooled
        pm_s[:, cs] = jnp.dot(pooled.astype(BF16), pw_r[g].astype(BF16), preferred_element_type=F32)
    return row


def _odd_glu_tile(i, da_r, dg_r, dap_r, dgp_r, ext_g, sh_g, T):
    H = HALO
    ext_g[0:H, :] = jnp.where(i == 0, 0.0, dap_r[...] * _sig(dgp_r[...]))
    ext_g[H:H + T, :] = da_r[...] * _sig(dg_r[...])
    _fill_shifted(ext_g, sh_g)


def _odd_specs(T, S, order):
    tb = T // HALO
    col = lambda c: pl.BlockSpec((T, A_WIDTH), lambda s: (order(s), c))
    prev = lambda c: pl.BlockSpec((HALO, A_WIDTH), lambda s: (jnp.maximum(order(s) * tb - 1, 0), c))
    const2 = lambda shape: pl.BlockSpec(shape, lambda s: (0, 0))
    weights = [pl.BlockSpec((4, LANES, LANES), lambda s: (0, 0, 0)), const2((1, A_WIDTH)),
               const2((D_CONV, A_WIDTH)), const2((1, A_WIDTH)), const2((1, A_WIDTH)), const2((1, A_WIDTH))]
    return col, prev, weights


def _odd_mixer_fwd(proj, pool_w, scale, dconv_w, dconv_b, ln_w, ln_b, T):
    S = proj.shape[0]
    col, prev, wspecs = _odd_specs(T, S, lambda s: s)
    H = HALO

    def body(uc_r, da_r, dg_r, zl_r, zh_r, ucp_r, dap_r, dgp_r, pw_r, sc_r, dw_r, db_r, lw_r, lb_r,
             u_ref, ut_ref, cv_ref, ext_u, ext_g, sh_g, pooled_s, pm_s):
        i = pl.program_id(0)
        _odd_pool_tile(i, uc_r, ucp_r, pw_r, ext_u, pooled_s, pm_s, T)
        _odd_glu_tile(i, da_r, dg_r, dap_r, dgp_r, ext_g, sh_g, T)
        base = H - (D_CONV - 1)
        conv = db_r[...] + dw_r[0:1, :] * _window(ext_g, sh_g, base, T)
        for kk in range(1, D_CONV):
            conv = conv + dw_r[kk:kk + 1, :] * _window(ext_g, sh_g, base + kk, T)
        cv_ref[...] = conv
        mu = jnp.mean(conv, axis=-1, keepdims=True)
        xc = conv - mu
        yh = xc * lax.rsqrt(jnp.mean(xc * xc, axis=-1, keepdims=True) + EPS)
        ln = yh * lw_r[...] + lb_r[...]
        zl, zh = zl_r[...], zh_r[...]
        _emit_u(u_ref, ut_ref, pm_s[...] * sc_r[...] * (zl * _sig(zl)), ln * _sig(ln) * (zh * _sig(zh)))

    return _call(
        body, "odd_mixer_fwd", (S // T,),
        [col(0), col(1), col(2), col(3), col(4), prev(0), prev(1), prev(2)] + wspecs,
        [pl.BlockSpec((T, D_MODEL), lambda i: (i, 0)), pl.BlockSpec((D_MODEL, T), lambda i: (0, i)),
         pl.BlockSpec((T, A_WIDTH), lambda i: (i, 0))],
        [jax.ShapeDtypeStruct((S, D_MODEL), BF16), jax.ShapeDtypeStruct((D_MODEL, S), BF16),
         jax.ShapeDtypeStruct((S, A_WIDTH), F32)],
        scratch=[pltpu.VMEM((T + H, A_WIDTH), F32), pltpu.VMEM((T + H, A_WIDTH), F32),
                 pltpu.VMEM((SUBLANES - 1, T + H - SHIFT_ROWS_LESS, A_WIDTH), F32),
                 pltpu.VMEM((T, A_WIDTH), F32), pltpu.VMEM((T, A_WIDTH), F32)],
        sem=("parallel",))(proj, proj, proj, proj, proj, proj, proj, proj,
                           pool_w, scale, dconv_w, dconv_b, ln_w, ln_b)


def _odd_mixer_bwd(du, proj, conv, pool_w, scale, dconv_w, dconv_b, ln_w, ln_b, T):
    S = proj.shape[0]
    nt = S // T
    order = lambda s: nt - 1 - s
    col, prev, wspecs = _odd_specs(T, S, order)
    H = HALO

    def body(dul_r, duh_r, cv_r, uc_r, da_r, dg_r, zl_r, zh_r, ucp_r, dap_r, dgp_r, pw_r, sc_r, dw_r, db_r, lw_r, lb_r,
             dp_ref, dpw_ref, sm_ref, ext_u, ext_g, sh_g, pooled_s, pm_s, dpl_s, ext_p, ext_c, sh_c, acc):
        step = pl.program_id(0)
        i = nt - 1 - step

        @pl.when(step == 0)
        def _():
            ext_p[T:T + H, :] = jnp.zeros((H, A_WIDTH), F32)
            ext_c[T:T + H, :] = jnp.zeros((H, A_WIDTH), F32)
            acc[...] = jnp.zeros_like(acc)
            dpw_ref[...] = jnp.zeros_like(dpw_ref)

        def accum(r, v):
            acc[r * SUBLANES:(r + 1) * SUBLANES, :] += _cs8(v)

        row = _odd_pool_tile(i, uc_r, ucp_r, pw_r, ext_u, pooled_s, pm_s, T)
        _odd_glu_tile(i, da_r, dg_r, dap_r, dgp_r, ext_g, sh_g, T)
        conv = cv_r[...]
        mu = jnp.mean(conv, axis=-1, keepdims=True)
        xc = conv - mu
        rstd = lax.rsqrt(jnp.mean(xc * xc, axis=-1, keepdims=True) + EPS)
        yh = xc * rstd
        ln = yh * lw_r[...] + lb_r[...]
        sln = _sig(ln)
        zl, zh = zl_r[...], zh_r[...]
        sl, sh = _sig(zl), _sig(zh)
        dul, duh = dul_r[...], duh_r[...]
        pm = pm_s[...]
        scv = sc_r[...]
        dyc = dul * (zl * sl)
        accum(34, dyc * pm)
        dpm = dyc * scv
        for g in range(len(POOL_SIZES)):
            cs = slice(g * LANES, (g + 1) * LANES)
            dpm_g = dpm[:, cs].astype(BF16)
            dpw_ref[g] += lax.dot_general(pooled_s[:, cs].astype(BF16), dpm_g, TN_DIMS, preferred_element_type=F32)
            dpl_s[:, cs] = lax.dot_general(dpm_g, pw_r[g].astype(BF16), NT_DIMS, preferred_element_type=F32)
        lane_p = lax.broadcasted_iota(jnp.int32, (1, A_WIDTH), 1) // LANES
        pvec = jnp.left_shift(2, lane_p)
        cnt = jnp.minimum(row + 1, pvec).astype(F32)
        dpl = dpl_s[...]
        ext_p[0:T, :] = dpl / cnt
        for g, p in enumerate(POOL_SIZES):
            cs = slice(g * LANES, (g + 1) * LANES)
            win = ext_p[0:T, cs]
            for j in range(1, p):
                win = win + ext_p[j:j + T, cs]
            dp_ref[:, cs] = (win - dpl[:, cs]).astype(BF16)
        ext_p[T:T + H, :] = ext_p[0:H, :]
        dln = duh * (zh * sh) * _dsilu(ln, sln)
        accum(32, dln * yh)
        accum(33, dln)
        dyh = dln * lw_r[...]
        dc = rstd * (dyh - jnp.mean(dyh, axis=-1, keepdims=True) - yh * jnp.mean(dyh * yh, axis=-1, keepdims=True))
        accum(31, dc)
        ext_c[0:T, :] = dc
        _fill_shifted(ext_c, sh_c)
        base = H - (D_CONV - 1)
        dgl = dw_r[0:1, :] * _window(ext_c, sh_c, D_CONV - 1, T)
        accum(0, dc * _window(ext_g, sh_g, base, T))
        for kk in range(1, D_CONV):
            dgl = dgl + dw_r[kk:kk + 1, :] * _window(ext_c, sh_c, D_CONV - 1 - kk, T)
            accum(kk, dc * _window(ext_g, sh_g, base + kk, T))
        ext_c[T:T + H, :] = ext_c[0:H, :]
        dav, dgv = da_r[...], dg_r[...]
        sg = _sig(dgv)
        dp_ref[:, A_WIDTH:2 * A_WIDTH] = (dgl * sg).astype(BF16)
        dp_ref[:, 2 * A_WIDTH:3 * A_WIDTH] = (dgl * dav * sg * (1.0 - sg)).astype(BF16)
        dp_ref[:, 3 * A_WIDTH:4 * A_WIDTH] = (dul * (pm * scv) * _dsilu(zl, sl)).astype(BF16)
        dp_ref[:, 4 * A_WIDTH:5 * A_WIDTH] = (duh * (ln * sln) * _dsilu(zh, sh)).astype(BF16)

        @pl.when(step == nt - 1)
        def _():
            for r in range(N_SMALL_ODD):
                sm_ref[r:r + 1, :] = jnp.sum(acc[r * SUBLANES:(r + 1) * SUBLANES, :], axis=0, keepdims=True)

    ext = pltpu.VMEM((T + H, A_WIDTH), F32)
    shifted = pltpu.VMEM((SUBLANES - 1, T + H - SHIFT_ROWS_LESS, A_WIDTH), F32)
    tile = pltpu.VMEM((T, A_WIDTH), F32)
    return _call(
        body, "odd_mixer_bwd", (nt,),
        [pl.BlockSpec((T, A_WIDTH), lambda s: (order(s), 0)), pl.BlockSpec((T, A_WIDTH), lambda s: (order(s), 1)),
         pl.BlockSpec((T, A_WIDTH), lambda s: (order(s), 0)),
         col(0), col(1), col(2), col(3), col(4), prev(0), prev(1), prev(2)] + wspecs,
        [pl.BlockSpec((T, ODD_IN), lambda s: (order(s), 0)),
         pl.BlockSpec((4, LANES, LANES), lambda s: (0, 0, 0)),
         pl.BlockSpec((N_SMALL_ODD, A_WIDTH), lambda s: (0, 0))],
        [jax.ShapeDtypeStruct((S, ODD_IN), BF16), jax.ShapeDtypeStruct((4, LANES, LANES), F32),
         jax.ShapeDtypeStruct((N_SMALL_ODD, A_WIDTH), F32)],
        scratch=[ext, ext, shifted, tile, tile, tile, ext, ext, shifted,
                 pltpu.VMEM((N_SMALL_ODD * SUBLANES, A_WIDTH), F32)],
        sem=("arbitrary",))(du, du, conv, proj, proj, proj, proj, proj, proj, proj, proj,
                            pool_w, scale, dconv_w, dconv_b, ln_w, ln_b)


TILE_SEQ = 256
TILE_M = 512
TILE_WG = 256


LATE_WEIGHTS = ("e_w_out", "o_w_in", "o_w_out")
ODD_MATS = ("o_w_in", "o_w_out")
EVEN_MATS = ("e_w_in", "e_w_out")


def _reduce_start(names, grads, cidx):
    recv = _swap_to_sibling(names, [grads[n] for n in names], "swap_halves_" + names[0][0], True)
    both = [_add_half(cidx, grads[n], r, n) for n, r in zip(names, recv)]
    return [h for h, _ in both], [hb for _, hb in both]


def _local_step(x, pos, tgt, w_e_in, shards, p, cidx, bidx):
    T = TILE_SEQ
    freq = _freq_table()
    wq = jnp.tile(p["e_q_norm_w"], (1, LANES // HEAD_DIM))
    wk = jnp.tile(p["e_k_norm_w"], (1, LANES // HEAD_DIM))

    proj_e, ht_e, *late = _inproj(x, p["e_norm_w"], w_e_in, TILE_SEQ, 1792, "inproj_even",
                                  LATE_WEIGHTS, [shards[n] for n in LATE_WEIGHTS])
    wb = dict(zip(LATE_WEIGHTS, late), e_w_in=w_e_in)
    qkv = _qkv_prep(proj_e, pos, freq, wq, wk, T)
    qs, ks, vs = qkv[0:3], qkv[3:6], qkv[6:9]
    os_, ls_ = [], []
    for g in range(3):
        o, l = _attn_fwd(qs[g], ks[g], vs[g], g)
        os_.append(o)
        ls_.append(l)
    u_e, ut_e = _even_mixer_fwd(proj_e, os_, ls_, p["e_conv_w"], T)
    x1 = _outproj(x, u_e, wb["e_w_out"], TILE_M, "outproj_even")
    proj_o, ht_o = _inproj(x1, p["o_norm_w"], wb["o_w_in"], TILE_SEQ, 1280, "inproj_odd")
    odd_w = (p["o_pool_w"], p["o_pool_scale"], p["o_dconv_w"], p["o_dconv_b"], p["o_ln_w"], p["o_ln_b"])
    u_o, ut_o, conv_o = _odd_mixer_fwd(proj_o, *odd_w, T)
    dy, lsum = _outproj_loss(x1, u_o, wb["o_w_out"], tgt, TILE_M, "outproj_odd_loss")

    g = {}
    g["o_w_out"] = _mm_wgrad(ut_o, [dy], TILE_WG, "wgrad_o_out")
    du_o = _mm_nt(dy, wb["o_w_out"], TILE_M, "du_odd")
    dproj_o, g["o_pool_w"], small_o = _odd_mixer_bwd(du_o, proj_o, conv_o, *odd_w, T)
    g["o_w_in"] = _mm_wgrad(ht_o, [dproj_o], TILE_WG, "wgrad_o_in")
    half_o, half_o16 = _reduce_start(ODD_MATS, g, cidx)
    dx1, g["o_norm_w"], blocks_o = _mm_nt_rms([dproj_o], wb["o_w_in"], x1, p["o_norm_w"], dy, TILE_SEQ, "dx_odd",
                                              ODD_MATS, half_o16)
    g["o_dconv_w"] = small_o[0:D_CONV]
    g["o_dconv_b"] = small_o[31:32]
    g["o_ln_w"] = small_o[32:33]
    g["o_ln_b"] = small_o[33:34]
    g["o_pool_scale"] = small_o[34:35]

    g["e_w_out"] = _mm_wgrad(ut_e, [dx1], TILE_WG, "wgrad_e_out")
    du_e = _mm_nt(dx1, wb["e_w_out"], TILE_M, "du_even")
    dos, cgs, drest, g["e_conv_w"] = _even_mixer_bwd(du_e, proj_e, os_, ls_, p["e_conv_w"], T)
    dqs, dks, dvs = [], [], []
    for gi in range(3):
        dq, dk, dv = _attn_bwd(qs[gi], ks[gi], vs[gi], dos[gi], ls_[gi], cgs[gi], gi)
        dqs.append(dq)
        dks.append(dk)
        dvs.append(dv)
    dqkv, dnw = _qk_bwd(proj_e, dqs, dks, dvs, pos, freq, wq, wk, T)
    g["e_q_norm_w"] = dnw[0:1, 0:HEAD_DIM]
    g["e_k_norm_w"] = dnw[1:2, 0:HEAD_DIM]
    pieces = [dqkv, drest]
    g["e_w_in"] = _mm_wgrad(ht_e, pieces, TILE_WG, "wgrad_e_in")
    half_e, half_e16 = _reduce_start(EVEN_MATS, g, cidx)
    dx, g["e_norm_w"], blocks_e = _mm_nt_rms(pieces, wb["e_w_in"], x, p["e_norm_w"], dx1, TILE_SEQ, "dx_even",
                                             EVEN_MATS, half_e16)
    parts = {}
    for names, halves, blocks in ((ODD_MATS, half_o, blocks_o), (EVEN_MATS, half_e, blocks_e)):
        for n, h, r in zip(names, halves, blocks):
            parts[n] = _add_blocks(bidx, h, r, n)
    return lsum, dx, g, parts


BIG = ("e_w_in", "e_w_out", "o_w_in", "o_w_out")
SHARD_AXIS = {"e_w_in": 1, "e_w_out": 0, "o_w_in": 1, "o_w_out": 0}
N_CHIPS = 4


def _place():
    x, y, c = lax.axis_index("x"), lax.axis_index("y"), lax.axis_index("c")
    chips = [(1 - x, y), (x, 1 - y), (1 - x, 1 - y)]
    return x, y, c, chips


def _block_of(ref, name, block):
    rows, cols = ref.shape
    if SHARD_AXIS[name] == 1:
        cw = cols // N_CHIPS
        return ref.at[:, pl.ds(pl.multiple_of(block * cw, LANES), cw)]
    rw = rows // N_CHIPS
    return ref.at[pl.ds(pl.multiple_of(block * rw, rw), rw), :]


def _half_of(ref, name, half):
    rows, cols = ref.shape
    if SHARD_AXIS[name] == 1:
        return ref.at[pl.ds(pl.multiple_of(half * (rows // 2), rows // 2), rows // 2), :]
    return ref.at[:, pl.ds(pl.multiple_of(half * (cols // 2), LANES), cols // 2)]


def _sub(ref, name, block, half):
    rows, cols = ref.shape
    if SHARD_AXIS[name] == 1:
        cw, hr = cols // N_CHIPS, rows // 2
        return ref.at[pl.ds(pl.multiple_of(half * hr, hr), hr), pl.ds(pl.multiple_of(block * cw, LANES), cw)]
    rw, hc = rows // N_CHIPS, cols // 2
    return ref.at[pl.ds(pl.multiple_of(block * rw, rw), rw), pl.ds(pl.multiple_of(half * hc, LANES), hc)]


GATHER_COPIES = 7


class _Gather:
    def __init__(self, names, s_refs, f_refs, send, recv):
        self.names, self.s, self.f, self.send, self.recv = names, s_refs, f_refs, send, recv

    def _copy(self, k, src, dst, to):
        return pltpu.make_async_remote_copy(src_ref=src, dst_ref=dst, send_sem=self.send.at[k],
                                            recv_sem=self.recv.at[k], device_id=to, device_id_type=MESH)

    def _plan(self):
        x, y, c, chips = _place()
        me, sib = 2 * x + y, (x, y, 1 - c)
        first, relay_in, relay, last_in = [], [], [], []
        for wi, n in enumerate(self.names):
            k0 = wi * GATHER_COPIES
            s, f = self.s[wi], self.f[wi]
            own = _block_of(f, n, me)
            first.append(self._copy(k0 + 3, s, own, sib))
            last_in.append(self._copy(k0 + 3, s, own, sib))
            for j, (cx, cy) in enumerate(chips):
                first.append(self._copy(k0 + j, _half_of(s, n, c), _sub(f, n, me, c), (cx, cy, c)))
                mine = _sub(f, n, 2 * cx + cy, c)
                relay_in.append(self._copy(k0 + j, mine, mine, sib))
                relay.append(self._copy(k0 + 4 + j, mine, mine, sib))
                theirs = _sub(f, n, 2 * cx + cy, 1 - c)
                last_in.append(self._copy(k0 + 4 + j, theirs, theirs, sib))
        return first, relay_in, relay, last_in

    def begin(self):
        for cp in self._plan()[0]:
            cp.start()

    def relay(self):
        _, relay_in, relay, _ = self._plan()
        for arrived, onward in zip(relay_in, relay):
            arrived.wait_recv()
            onward.start()

    def end(self):
        first, _, relay, last_in = self._plan()
        for cp in last_in:
            cp.wait_recv()
        for cp in first + relay:
            cp.wait_send()


def _full_shape(n, s):
    r, cdim = s.shape
    return jax.ShapeDtypeStruct((r, cdim * N_CHIPS) if SHARD_AXIS[n] == 1 else (r * N_CHIPS, cdim), s.dtype)


def _gather_sems(names):
    k = GATHER_COPIES * len(names)
    return [pltpu.SemaphoreType.DMA((k,)), pltpu.SemaphoreType.DMA((k,))]


def _gather_weights(names, shards):
    nw = len(names)

    def body(*refs):
        g = _Gather(names, refs[:nw], refs[nw:2 * nw], *refs[2 * nw:])
        g.begin()
        g.relay()
        g.end()

    return pl.pallas_call(
        body, name="gather_weights", in_specs=[ANY] * nw, out_specs=[ANY] * nw,
        out_shape=[_full_shape(n, s) for n, s in zip(names, shards)], scratch_shapes=_gather_sems(names),
    )(*shards)


def _scatter_copies(names, h_refs, r_refs, send, recv):
    _, _, c, chips = _place()
    cps = []
    for wi, n in enumerate(names):
        for j, (cx, cy) in enumerate(chips):
            cps.append(pltpu.make_async_remote_copy(
                src_ref=_block_of(h_refs[wi], n, 2 * cx + cy), dst_ref=r_refs[wi].at[j],
                send_sem=send.at[wi * 3 + j], recv_sem=recv.at[wi * 3 + j],
                device_id=(cx, cy, c), device_id_type=MESH))
    return cps


def _scatter_sems(names):
    return [pltpu.SemaphoreType.DMA((3 * len(names),)), pltpu.SemaphoreType.DMA((3 * len(names),))]


def _allreduce_small(part, name):
    R = part.shape[0]

    def body(p_ref, o_ref, sbuf, cbuf, send, recv):
        x, y, c, chips = _place()
        me = 2 * x + y
        sib = (x, y, 1 - c)
        sbuf[c] = p_ref[...]
        mine = sbuf.at[c]
        d2d = pltpu.make_async_remote_copy(src_ref=mine, dst_ref=mine, send_sem=send.at[0], recv_sem=recv.at[0],
                                           device_id=sib, device_id_type=MESH)
        d2d.start()
        theirs = sbuf.at[1 - c]
        pltpu.make_async_remote_copy(src_ref=theirs, dst_ref=theirs, send_sem=send.at[0], recv_sem=recv.at[0],
                                     device_id=sib, device_id_type=MESH).wait_recv()
        cbuf[me] = sbuf[0] + sbuf[1]
        blk = cbuf.at[me]
        sends = [d2d]
        for j, (cx, cy) in enumerate(chips):
            cp = pltpu.make_async_remote_copy(src_ref=blk, dst_ref=blk, send_sem=send.at[1 + j], recv_sem=recv.at[1 + j],
                                              device_id=(cx, cy, c), device_id_type=MESH)
            cp.start()
            sends.append(cp)
        for j, (cx, cy) in enumerate(chips):
            got = cbuf.at[2 * cx + cy]
            pltpu.make_async_remote_copy(src_ref=got, dst_ref=got, send_sem=send.at[1 + j], recv_sem=recv.at[1 + j],
                                         device_id=(cx, cy, c), device_id_type=MESH).wait_recv()
        o_ref[...] = (cbuf[0] + cbuf[1]) + (cbuf[2] + cbuf[3])
        for cp in sends:
            cp.wait_send()

    vm = pl.BlockSpec(memory_space=pltpu.VMEM)
    return pl.pallas_call(
        body, name=name, in_specs=[vm], out_specs=vm,
        out_shape=jax.ShapeDtypeStruct(part.shape, F32),
        scratch_shapes=[pltpu.VMEM((2, R, LANES), F32), pltpu.VMEM((N_CHIPS, R, LANES), F32),
                        pltpu.SemaphoreType.DMA((4,)), pltpu.SemaphoreType.DMA((4,))],
    )(part)


def _half_shape(shape, name):
    r, cdim = shape
    return (r // 2, cdim) if SHARD_AXIS[name] == 1 else (r, cdim // 2)


def _shard_shape(shape, name):
    r, cdim = shape
    return (r, cdim // N_CHIPS) if SHARD_AXIS[name] == 1 else (r // N_CHIPS, cdim)


def _swap_to_sibling(names, srcs, name, pick_half):
    nw = len(names)

    def body(*refs):
        g_refs, r_refs = refs[:nw], refs[nw:2 * nw]
        send, recv = refs[2 * nw:]
        x, y, c, _ = _place()
        sib = (x, y, 1 - c)
        cps = []
        for wi, n in enumerate(names):
            src = _half_of(g_refs[wi], n, 1 - c) if pick_half else g_refs[wi]
            cp = pltpu.make_async_remote_copy(src_ref=src, dst_ref=r_refs[wi], send_sem=send.at[wi],
                                              recv_sem=recv.at[wi], device_id=sib, device_id_type=MESH)
            cp.start()
            cps.append(cp)
        for cp in cps:
            cp.wait()

    outs = [jax.ShapeDtypeStruct(_half_shape(g.shape, n) if pick_half else g.shape, g.dtype)
            for n, g in zip(names, srcs)]
    return pl.pallas_call(
        body, name=name, in_specs=[ANY] * nw, out_specs=[ANY] * nw, out_shape=outs,
        scratch_shapes=[pltpu.SemaphoreType.DMA((nw,)), pltpu.SemaphoreType.DMA((nw,))],
    )(*srcs)


def _add_half(cidx, g, r, name):
    rows, cols = r.shape
    tr = 256
    tc = cols if cols <= 1792 else (1792 if cols % 1792 == 0 else 1280)
    nr, nc = rows // tr, cols // tc

    def body(c_ref, g_ref, r_ref, o_ref, ob_ref):
        s = g_ref[...] + r_ref[...]
        o_ref[...] = s
        ob_ref[...] = s.astype(BF16)

    if SHARD_AXIS[name] == 1:
        gmap = lambda i, j, c_ref: (c_ref[0] * nr + i, j)
    else:
        gmap = lambda i, j, c_ref: (i, c_ref[0] * nc + j)
    same = lambda i, j, c_ref: (i, j)
    return pl.pallas_call(
        body, name="add_half_" + name,
        grid_spec=pltpu.PrefetchScalarGridSpec(
            num_scalar_prefetch=1, grid=(nr, nc),
            in_specs=[pl.BlockSpec((tr, tc), gmap), pl.BlockSpec((tr, tc), same)],
            out_specs=[pl.BlockSpec((tr, tc), same), pl.BlockSpec((tr, tc), same)]),
        out_shape=[jax.ShapeDtypeStruct(r.shape, F32), jax.ShapeDtypeStruct(r.shape, BF16)],
        compiler_params=pltpu.CompilerParams(dimension_semantics=("parallel", "parallel"), vmem_limit_bytes=VMEM_LIMIT),
    )(cidx, g, r)


def _add_blocks(bidx, h, r, name):
    _, rows, cols = r.shape
    tr = min(rows, 256)
    nr = rows // tr

    def body(b_ref, h_ref, r0, r1, r2, o_ref):
        o_ref[...] = ((h_ref[...] + r0[0].astype(F32)) + r1[0].astype(F32)) + r2[0].astype(F32)

    if SHARD_AXIS[name] == 1:
        hmap = lambda i, b_ref: (i, b_ref[0])
    else:
        hmap = lambda i, b_ref: (b_ref[0] * nr + i, 0)
    rspec = lambda j: pl.BlockSpec((1, tr, cols), lambda i, b_ref, j=j: (j, i, 0))
    return pl.pallas_call(
        body, name="add_blocks_" + name,
        grid_spec=pltpu.PrefetchScalarGridSpec(
            num_scalar_prefetch=1, grid=(nr,),
            in_specs=[pl.BlockSpec((tr, cols), hmap), rspec(0), rspec(1), rspec(2)],
            out_specs=pl.BlockSpec((tr, cols), lambda i, b_ref: (i, 0))),
        out_shape=jax.ShapeDtypeStruct((rows, cols), F32),
        compiler_params=pltpu.CompilerParams(dimension_semantics=("parallel",), vmem_limit_bytes=VMEM_LIMIT),
    )(bidx, h, r, r, r)


def _adam_math(w, g, m, v):
    c1 = 1.0 - ADAM_B1 ** ADAM_STEP
    c2 = 1.0 - ADAM_B2 ** ADAM_STEP
    nm = ADAM_B1 * m + (1.0 - ADAM_B1) * g
    nv = ADAM_B2 * v + (1.0 - ADAM_B2) * (g * g)
    delta = -ADAM_LR * ((nm / c1) / (jnp.sqrt(nv / c2) + ADAM_EPS) + ADAM_WD * w)
    return delta, nm, nv


def _adamw(w, g, m, v, name):
    def body(w_ref, g_ref, m_ref, v_ref, d_ref, nm_ref, nv_ref):
        d_ref[...], nm_ref[...], nv_ref[...] = _adam_math(w_ref[...], g_ref[...], m_ref[...], v_ref[...])

    spec = pl.BlockSpec(w.shape, lambda i: (0, 0))
    return _call(body, "adamw_" + name, (1,), [spec] * 4, [spec] * 3,
                 [jax.ShapeDtypeStruct(w.shape, F32)] * 3, sem=("arbitrary",))(w, g, m, v)


def _adamw_halves(cidx, w, mine, theirs, m, v, name):
    rows, cols = w.shape
    hr, hc = mine.shape
    tr = 128
    if SHARD_AXIS[name] == 1:
        ni = hr // tr
        wmap = lambda hh, i, c_ref: (hh * ni + i, 0)
    else:
        ni = hr // tr
        wmap = lambda hh, i, c_ref: (i, hh)
    hmap = lambda hh, i, c_ref: (i, 0)

    def body(c_ref, w_ref, a_ref, b_ref, m_ref, v_ref, g_ref, d_ref, nm_ref, nv_ref):
        g = jnp.where(pl.program_id(0) == c_ref[0], a_ref[...], b_ref[...])
        g_ref[...] = g
        d_ref[...], nm_ref[...], nv_ref[...] = _adam_math(w_ref[...], g, m_ref[...], v_ref[...])

    wspec = pl.BlockSpec((tr, hc), wmap)
    hspec = pl.BlockSpec((tr, hc), hmap)
    return pl.pallas_call(
        body, name="adamw_" + name,
        grid_spec=pltpu.PrefetchScalarGridSpec(
            num_scalar_prefetch=1, grid=(2, ni),
            in_specs=[wspec, hspec, hspec, wspec, wspec], out_specs=[wspec] * 4),
        out_shape=[jax.ShapeDtypeStruct(w.shape, F32)] * 4,
        compiler_params=pltpu.CompilerParams(dimension_semantics=("parallel", "parallel"), vmem_limit_bytes=VMEM_LIMIT),
    )(cidx, w, mine, theirs, m, v)


SMALL = ("e_norm_w", "e_q_norm_w", "e_k_norm_w", "e_conv_w", "o_norm_w", "o_pool_w", "o_pool_scale",
         "o_dconv_w", "o_dconv_b", "o_ln_w", "o_ln_b")
SMALL_SHARDED = ("e_conv_w", "o_norm_w", "o_pool_scale", "o_dconv_w", "o_dconv_b", "o_ln_w", "o_ln_b")
WEIGHTS = ("e_norm_w", "e_w_in", "e_q_norm_w", "e_k_norm_w", "e_conv_w", "e_w_out", "o_norm_w", "o_w_in",
           "o_pool_w", "o_pool_scale", "o_dconv_w", "o_dconv_b", "o_ln_w", "o_ln_b", "o_w_out")


def _pack(arrs):
    flat = jnp.concatenate([a.reshape(-1) for a in arrs])
    rows = -(-flat.shape[0] // (LANES * SUBLANES)) * SUBLANES
    flat = jnp.pad(flat, (0, rows * LANES - flat.shape[0]))
    return flat.reshape(rows, LANES)


def _unpack(packed, shapes):
    flat = packed.reshape(-1)
    out, off = [], 0
    for s in shapes:
        n = int(np.prod(s))
        out.append(flat[off:off + n].reshape(s))
        off += n
    return out


def _gather_last(a, block, width):
    return lax.dynamic_slice_in_dim(a, block * width, width, axis=a.ndim - 1)


def kernel(x, positions, e_norm_w, e_w_in, e_q_norm_w, e_k_norm_w, e_conv_w, e_w_out, o_norm_w, o_w_in, o_pool_w, o_pool_scale, o_dconv_w, o_dconv_b, o_ln_w, o_ln_b, o_w_out, loss_target, m_e_norm_w, m_e_w_in, m_e_q_norm_w, m_e_k_norm_w, m_e_conv_w, m_e_w_out, m_o_norm_w, m_o_w_in, m_o_pool_w, m_o_pool_scale, m_o_dconv_w, m_o_dconv_b, m_o_ln_w, m_o_ln_b, m_o_w_out, v_e_norm_w, v_e_w_in, v_e_q_norm_w, v_e_k_norm_w, v_e_conv_w, v_e_w_out, v_o_norm_w, v_o_w_in, v_o_pool_w, v_o_pool_scale, v_o_dconv_w, v_o_dconv_b, v_o_ln_w, v_o_ln_b, v_o_w_out):
    given = dict(e_norm_w=e_norm_w, e_w_in=e_w_in, e_q_norm_w=e_q_norm_w, e_k_norm_w=e_k_norm_w, e_conv_w=e_conv_w,
                 e_w_out=e_w_out, o_norm_w=o_norm_w, o_w_in=o_w_in, o_pool_w=o_pool_w, o_pool_scale=o_pool_scale,
                 o_dconv_w=o_dconv_w, o_dconv_b=o_dconv_b, o_ln_w=o_ln_w, o_ln_b=o_ln_b, o_w_out=o_w_out)
    mom = dict(e_norm_w=m_e_norm_w, e_w_in=m_e_w_in, e_q_norm_w=m_e_q_norm_w, e_k_norm_w=m_e_k_norm_w,
               e_conv_w=m_e_conv_w, e_w_out=m_e_w_out, o_norm_w=m_o_norm_w, o_w_in=m_o_w_in, o_pool_w=m_o_pool_w,
               o_pool_scale=m_o_pool_scale, o_dconv_w=m_o_dconv_w, o_dconv_b=m_o_dconv_b, o_ln_w=m_o_ln_w,
               o_ln_b=m_o_ln_b, o_w_out=m_o_w_out)
    var = dict(e_norm_w=v_e_norm_w, e_w_in=v_e_w_in, e_q_norm_w=v_e_q_norm_w, e_k_norm_w=v_e_k_norm_w,
               e_conv_w=v_e_conv_w, e_w_out=v_e_w_out, o_norm_w=v_o_norm_w, o_w_in=v_o_w_in, o_pool_w=v_o_pool_w,
               o_pool_scale=v_o_pool_scale, o_dconv_w=v_o_dconv_w, o_dconv_b=v_o_dconv_b, o_ln_w=v_o_ln_w,
               o_ln_b=v_o_ln_b, o_w_out=v_o_w_out)
    S = x.shape[1]
    mx, my, mc = lax.axis_index("x"), lax.axis_index("y"), lax.axis_index("c")
    chip = 2 * mx + my
    cidx = jnp.reshape(mc, (1,)).astype(jnp.int32)
    bidx = jnp.reshape(chip, (1,)).astype(jnp.int32)

    shards = {n: given[n][0].astype(BF16) for n in BIG}
    (w_e_in,) = _gather_weights(("e_w_in",), [shards["e_w_in"]])
    shard_sizes = [int(np.prod(given[n].shape)) for n in SMALL_SHARDED]
    own = _pack([given[n] for n in SMALL_SHARDED])
    rows = own.shape[0]
    slots = jnp.zeros((N_CHIPS, rows, LANES), F32)
    own = jnp.where(mc == 0, own, 0.0)
    slots = lax.dynamic_update_slice(slots, own[None], (chip, 0, 0))
    gathered = _allreduce_small(slots.reshape(N_CHIPS * rows, LANES), "gather_small")
    gathered = gathered.reshape(N_CHIPS, rows * LANES)
    p = {}
    off = 0
    for n, size in zip(SMALL_SHARDED, shard_sizes):
        sh = given[n].shape[1:]
        parts = gathered[:, off:off + size].reshape((N_CHIPS,) + sh)
        fullp = jnp.moveaxis(parts, 0, -2).reshape(sh[:-1] + (N_CHIPS * sh[-1],))
        p[n] = fullp.reshape(-1, fullp.shape[-1])
        off += size
    p["e_norm_w"] = e_norm_w
    p["e_q_norm_w"] = e_q_norm_w
    p["e_k_norm_w"] = e_k_norm_w
    p["o_pool_w"] = o_pool_w[0]

    lsum, dx, g, parts = _local_step(x[0], positions.reshape(S, 1), loss_target[0], w_e_in, shards, p, cidx, bidx)
    loss = lax.psum(0.5 * jnp.sum(lsum) / float(D_MODEL), ("x", "y", "c"))

    tot = _unpack(_allreduce_small(_pack([g[n] for n in SMALL]), "allreduce_small"), [g[n].shape for n in SMALL])
    gsmall = dict(zip(SMALL, tot))
    grads = {}
    for n in SMALL:
        gv = gsmall[n]
        if n in SMALL_SHARDED:
            gv = _gather_last(gv, chip, gv.shape[-1] // N_CHIPS)
        grads[n] = gv.reshape(given[n].shape)

    theirs = _swap_to_sibling(BIG, [parts[n] for n in BIG], "swap_reduced", False)

    delta, new_m, new_v = {}, {}, {}
    for n, other in zip(BIG, theirs):
        sh = given[n].shape
        gs, d, nm, nv = _adamw_halves(cidx, given[n][0], parts[n], other, mom[n][0], var[n][0], n)
        grads[n], delta[n], new_m[n], new_v[n] = gs.reshape(sh), d.reshape(sh), nm.reshape(sh), nv.reshape(sh)
    shapes = [given[n].shape for n in SMALL]
    packed = [_pack([src[n] for n in SMALL]) for src in (given, grads, mom, var)]
    for dst, pk in zip((delta, new_m, new_v), _adamw(*packed, "small")):
        for n, a in zip(SMALL, _unpack(pk, shapes)):
            dst[n] = a
    return (loss, dx[None], *[grads[n] for n in WEIGHTS], *[delta[n] for n in WEIGHTS],
            *[new_m[n] for n in WEIGHTS], *[new_v[n] for n in WEIGHTS])
```

```python
import numpy as np
import jax
import jax.numpy as jnp
from jax import lax
from jax.experimental import pallas as pl
from jax.experimental.pallas import tpu as pltpu

F32 = jnp.float32
BF16 = jnp.bfloat16

D_MODEL = 1024
HEAD_DIM = 64
A_WIDTH = 512
A_HEADS = 8
A_GROUPS = ((128, 1), (512, 4), (2048, 16))
QBLK = 128
ROT_DIM = 16
ROPE_THETA = 500000.0
POOL_SIZES = (2, 4, 8, 16)
D_CONV = 31
SC_WIDTH = 3
EVEN_IN = 7168
ODD_IN = 2560
EPS = 1e-6
NEG = -1e30
ADAM_LR, ADAM_B1, ADAM_B2, ADAM_EPS, ADAM_WD, ADAM_STEP = 0.001, 0.9, 0.999, 1e-08, 0.01, 10

LANES = 128
SUBLANES = 8
HALO = 32
VMEM_LIMIT = 52 * 1024 * 1024
MESH = pl.DeviceIdType.MESH
ANY = pl.BlockSpec(memory_space=pl.ANY)

NT_DIMS = (((1,), (1,)), ((), ()))
TN_DIMS = (((0,), (0,)), ((), ()))


def _call(body, name, grid, in_specs, out_specs, out_shape, scratch=(), sem=None, aliases=None):
    return pl.pallas_call(
        body, name=name, grid=grid, in_specs=in_specs, out_specs=out_specs, out_shape=out_shape,
        scratch_shapes=list(scratch), input_output_aliases=aliases or {},
        compiler_params=pltpu.CompilerParams(dimension_semantics=sem, vmem_limit_bytes=VMEM_LIMIT))


def _sig(v):
    return jax.nn.sigmoid(v)


def _dsilu(v, s):
    return s * (1.0 + v * (1.0 - s))


def _out_projection(ut_ref, w_ref, lo, hi):
    acc = None
    for k, v in enumerate((lo, hi)):
        ut_ref[k * A_WIDTH:(k + 1) * A_WIDTH, :] = v.T.astype(BF16)
        part = jnp.dot(v.astype(BF16), w_ref[k * A_WIDTH:(k + 1) * A_WIDTH, :], preferred_element_type=F32)
        acc = part if acc is None else acc + part
    return acc


def _cs8(v):
    return v.reshape(v.shape[0] // SUBLANES, SUBLANES, v.shape[1]).sum(axis=0)


def _seg_ones():
    r = lax.broadcasted_iota(jnp.int32, (LANES, LANES), 0) // HEAD_DIM
    c = lax.broadcasted_iota(jnp.int32, (LANES, LANES), 1) // HEAD_DIM
    return (r == c).astype(BF16)


def _segsum(v, ones):
    hi = v.astype(BF16)
    lo = (v - hi.astype(F32)).astype(BF16)
    return (jnp.dot(hi, ones, preferred_element_type=F32) + jnp.dot(lo, ones, preferred_element_type=F32))


def _rope_tables(pos_ref, freq_ref):
    ang = pos_ref[...].astype(F32) * freq_ref[...]
    cosv, sinv = jnp.cos(ang), jnp.sin(ang)
    lm = lax.broadcasted_iota(jnp.int32, ang.shape, 1) % HEAD_DIM
    half = ROT_DIM // 2
    c = jnp.where(lm < ROT_DIM, cosv, 1.0)
    s1 = jnp.where((lm >= half) & (lm < ROT_DIM), sinv, 0.0)
    s2 = jnp.where(lm < half, -sinv, 0.0)
    return c, s1, s2


def _freq_table():
    half = ROT_DIM // 2
    inv = ROPE_THETA ** (-np.arange(half, dtype=np.float64) / half)
    lane = np.arange(LANES) % HEAD_DIM
    f = np.where(lane < ROT_DIM, inv[lane % half], 0.0)
    return jnp.asarray(f.reshape(1, LANES), F32)


def _load_once(hbm_ref, vmem_ref, sem):
    @pl.when(pl.program_id(0) == 0)
    def _():
        cp = pltpu.make_async_copy(hbm_ref, vmem_ref, sem)
        cp.start()
        cp.wait()


def _inproj(x, nw, w, tm, tn, name, gather_names=(), gather_shards=()):
    S, N = x.shape[0], w.shape[1]
    ni = S // tm
    ng = len(gather_names)

    def body(*refs):
        x_ref, nw_ref, w_hbm = refs[:3]
        s_refs = refs[3:3 + ng]
        o_ref, ht_ref = refs[3 + ng:5 + ng]
        f_refs = refs[5 + ng:5 + 2 * ng]
        w_v, sem = refs[5 + 2 * ng:7 + 2 * ng]
        if ng:
            i = pl.program_id(0)
            g = _Gather(gather_names, s_refs, f_refs, *refs[7 + 2 * ng:])
            pl.when(i == 0)(g.begin)
            pl.when(i == ni // 2)(g.relay)
        _load_once(w_hbm, w_v, sem)
        xv = x_ref[...]
        ms = jnp.mean(xv * xv, axis=-1, keepdims=True)
        h = xv * lax.rsqrt(ms + EPS) * nw_ref[...]
        ht_ref[...] = h.T.astype(BF16)
        hb = h.astype(BF16)
        for j in range(N // tn):
            o_ref[:, j * tn:(j + 1) * tn] = jnp.dot(hb, w_v[:, j * tn:(j + 1) * tn], preferred_element_type=F32)
        if ng:
            pl.when(i == ni - 1)(g.end)

    return _call(
        body, name, (ni,),
        [pl.BlockSpec((tm, D_MODEL), lambda i: (i, 0)),
         pl.BlockSpec((1, D_MODEL), lambda i: (0, 0)), ANY] + [ANY] * ng,
        [pl.BlockSpec((tm, N), lambda i: (i, 0)),
         pl.BlockSpec((D_MODEL, tm), lambda i: (0, i))] + [ANY] * ng,
        [jax.ShapeDtypeStruct((S, N), F32), jax.ShapeDtypeStruct((D_MODEL, S), BF16)] +
        [_full_shape(n, s) for n, s in zip(gather_names, gather_shards)],
        scratch=[pltpu.VMEM(w.shape, BF16), pltpu.SemaphoreType.DMA(())] + (_gather_sems(gather_names) if ng else []),
        sem=("arbitrary",))(x, nw, w, *gather_shards)


def _piece_blocks(pieces, tk, axis):
    starts, counts, s = [], [], 0
    for p in pieces:
        n = p.shape[axis] // tk
        starts.append(s)
        counts.append(n)
        s += n
    return starts, counts, s


def _mm_nt_rms(pieces, w, x, nw, dres, tm, name, scatter_names=(), scatter_halves=()):
    S = x.shape[0]
    npc = len(pieces)
    ni = S // tm
    ns = len(scatter_names)
    offs = np.cumsum([0] + [p.shape[1] for p in pieces]).tolist()

    def body(*refs):
        p_refs = refs[:npc]
        w_hbm, x_ref, nw_ref, dr_ref = refs[npc:npc + 4]
        h_refs = refs[npc + 4:npc + 4 + ns]
        dx_ref, dnw_ref = refs[npc + 4 + ns:npc + 6 + ns]
        r_refs = refs[npc + 6 + ns:npc + 6 + 2 * ns]
        w_v, sem, nacc = refs[npc + 6 + 2 * ns:npc + 9 + 2 * ns]
        i = pl.program_id(0)
        if ns:
            send, recv = refs[npc + 9 + 2 * ns:]

            @pl.when(i == 0)
            def _():
                for cp in _scatter_copies(scatter_names, h_refs, r_refs, send, recv):
                    cp.start()
        _load_once(w_hbm, w_v, sem)

        @pl.when(i == 0)
        def _():
            nacc[...] = jnp.zeros_like(nacc)

        dh = None
        for p in range(npc):
            part = lax.dot_general(p_refs[p][...].astype(BF16), w_v[:, offs[p]:offs[p + 1]], NT_DIMS,
                                   preferred_element_type=F32)
            dh = part if dh is None else dh + part
        xv = x_ref[...]
        rs = lax.rsqrt(jnp.mean(xv * xv, axis=-1, keepdims=True) + EPS)
        xh = xv * rs
        nacc[...] += _cs8(dh * xh)
        dxh = dh * nw_ref[...]
        dx_ref[...] = dr_ref[...] + rs * (dxh - xh * jnp.mean(dxh * xh, axis=-1, keepdims=True))

        @pl.when(i == ni - 1)
        def _():
            dnw_ref[...] = jnp.sum(nacc[...], axis=0, keepdims=True)
            if ns:
                for cp in _scatter_copies(scatter_names, h_refs, r_refs, send, recv):
                    cp.wait()

    row = pl.BlockSpec((tm, D_MODEL), lambda i: (i, 0))
    outs = _call(
        body, name, (ni,),
        [pl.BlockSpec((tm, p.shape[1]), lambda i: (i, 0)) for p in pieces] +
        [ANY, row, pl.BlockSpec((1, D_MODEL), lambda i: (0, 0)), row] + [ANY] * ns,
        [row, pl.BlockSpec((1, D_MODEL), lambda i: (0, 0))] + [ANY] * ns,
        [jax.ShapeDtypeStruct((S, D_MODEL), F32), jax.ShapeDtypeStruct((1, D_MODEL), F32)] +
        [jax.ShapeDtypeStruct((3,) + _shard_shape(h.shape, n), h.dtype) for n, h in zip(scatter_names, scatter_halves)],
        scratch=[pltpu.VMEM(w.shape, BF16), pltpu.SemaphoreType.DMA(()), pltpu.VMEM((SUBLANES, D_MODEL), F32)] +
        (_scatter_sems(scatter_names) if ns else []),
        sem=("arbitrary",))(*pieces, w, x, nw, dres, *scatter_halves)
    return outs[0], outs[1], list(outs[2:])


def _mm_wgrad(at, pieces, tn, name):
    M, S = at.shape
    starts, counts, nj = _piece_blocks(pieces, tn, 1)
    npc = len(pieces)

    def body(*refs):
        a_hbm = refs[0]
        p_refs = refs[1:1 + npc]
        o_ref, a_v, sem = refs[1 + npc:]
        j = pl.program_id(0)
        _load_once(a_hbm, a_v, sem)
        for p in range(npc):
            @pl.when((j >= starts[p]) & (j < starts[p] + counts[p]))
            def _(p=p):
                o_ref[...] = jnp.dot(a_v[...], p_refs[p][...].astype(BF16), preferred_element_type=F32)

    def pspec(p):
        return pl.BlockSpec((S, tn), lambda j: (0, jnp.clip(j - starts[p], 0, counts[p] - 1)))

    return _call(
        body, name, (nj,),
        [ANY] + [pspec(p) for p in range(npc)],
        pl.BlockSpec((M, tn), lambda j: (0, j)),
        jax.ShapeDtypeStruct((M, nj * tn), F32),
        scratch=[pltpu.VMEM(at.shape, BF16), pltpu.SemaphoreType.DMA(())], sem=("arbitrary",))(at, *pieces)


def _stream_spec(d, T):
    return pl.BlockSpec((d, T // d, A_WIDTH), lambda i: (0, i, 0))


def _stream_shape(d, S, dtype):
    return jax.ShapeDtypeStruct((d, S // d, A_WIDTH), dtype)


N_CHUNK = A_WIDTH // LANES


def _to_tokens(ref, scr, d, T):
    if d == 1:
        return ref[0].astype(F32)
    for r in range(d):
        for ch in range(N_CHUNK):
            scr.at[ch][pl.ds(r, T // d, stride=d), :] = ref[r, :, ch * LANES:(ch + 1) * LANES].astype(F32)
    return _get(scr)


def _from_tokens(out_ref, scr, d, T):
    for r in range(d):
        for ch in range(N_CHUNK):
            out_ref[r, :, ch * LANES:(ch + 1) * LANES] = scr.at[ch][pl.ds(r, T // d, stride=d), :].astype(out_ref.dtype)


def _put(scr, val):
    for ch in range(N_CHUNK):
        scr[ch] = val[:, ch * LANES:(ch + 1) * LANES]


def _get(scr):
    return jnp.concatenate([scr[ch] for ch in range(N_CHUNK)], axis=1)


def _chunked(T):
    return pltpu.VMEM((N_CHUNK, T, LANES), F32)


def _compact_spec(d, T):
    return pl.BlockSpec((d, T // d, LANES), lambda i: (0, i, 0))


def _compact_shape(d, S):
    return jax.ShapeDtypeStruct((d, S // d, LANES), F32)


def _compact_to_tokens(ref, scr, d, T):
    if d == 1:
        return ref[0]
    for r in range(d):
        scr[pl.ds(r, T // d, stride=d), :] = ref[r]
    return scr[...]


def _compact_from_tokens(out_ref, scr, val, d, T):
    if d == 1:
        out_ref[0] = val
        return
    scr[...] = val
    for r in range(d):
        out_ref[r] = scr[pl.ds(r, T // d, stride=d), :]


def _head_expander():
    r = lax.broadcasted_iota(jnp.int32, (LANES, A_WIDTH), 0)
    c = lax.broadcasted_iota(jnp.int32, (LANES, A_WIDTH), 1) // HEAD_DIM
    return (r == c).astype(BF16)


def _head_reducer():
    r = lax.broadcasted_iota(jnp.int32, (A_WIDTH, LANES), 0) // HEAD_DIM
    c = lax.broadcasted_iota(jnp.int32, (A_WIDTH, LANES), 1)
    return (r == c).astype(BF16)


def _qkv_prep(proj, pos, freq, wq, wk, T):
    S = proj.shape[0]
    qk_w = 3 * A_WIDTH

    def body(q_ref, k_ref, v_ref, pos_ref, f_ref, wq_ref, wk_ref, *rest):
        outs, scr = rest[:9], rest[9]
        ones = _seg_ones()
        c, s1, s2 = _rope_tables(pos_ref, f_ref)
        for t, (src, w_ref) in enumerate(((q_ref, wq_ref), (k_ref, wk_ref), (v_ref, None))):
            for g in range(3):
                d = A_GROUPS[g][1]
                out = outs[3 * t + g]
                for ch in range(A_WIDTH // LANES):
                    cs = slice(ch * LANES, (ch + 1) * LANES)
                    v = src[:, g * A_WIDTH + ch * LANES: g * A_WIDTH + (ch + 1) * LANES]
                    if w_ref is not None:
                        rs = lax.rsqrt(_segsum(v * v, ones) * (1.0 / HEAD_DIM) + EPS)
                        y = v * rs * w_ref[...]
                        v = y * c + pltpu.roll(y, 8, 1) * s1 + pltpu.roll(y, LANES - 8, 1) * s2
                    if d == 1:
                        out[0, :, cs] = v.astype(BF16)
                    else:
                        scr[ch] = v
                if d > 1:
                    _from_tokens(out, scr, d, T)

    ds_ = [A_GROUPS[g][1] for g in range(3)] * 3
    return _call(
        body, "qkv_prep", (S // T,),
        [pl.BlockSpec((T, qk_w), lambda i: (i, 0)), pl.BlockSpec((T, qk_w), lambda i: (i, 1)),
         pl.BlockSpec((T, qk_w), lambda i: (i, 2)),
         pl.BlockSpec((T, 1), lambda i: (i, 0)), pl.BlockSpec((1, LANES), lambda i: (0, 0)),
         pl.BlockSpec((1, LANES), lambda i: (0, 0)), pl.BlockSpec((1, LANES), lambda i: (0, 0))],
        [_stream_spec(d, T) for d in ds_],
        [_stream_shape(d, S, BF16) for d in ds_],
        scratch=[_chunked(T)], sem=("parallel",))(proj, proj, proj, pos, freq, wq, wk)


def _attn_mask(i):
    qi = lax.broadcasted_iota(jnp.int32, (QBLK, 2 * QBLK), 0) + QBLK
    kj = lax.broadcasted_iota(jnp.int32, (QBLK, 2 * QBLK), 1)
    dist = qi - kj
    return (dist >= 0) & (dist <= QBLK) & ((i > 0) | (kj >= QBLK))


ATT_BLK = (None, QBLK, A_WIDTH)
ATT_CBLK = (None, QBLK, LANES)


def _first_head_lanes():
    return lax.broadcasted_iota(jnp.int32, (1, LANES), 1) < HEAD_DIM


def _split_heads(v, first):
    zero = jnp.zeros_like(v)
    return jnp.where(first, v, zero), jnp.where(first, zero, v)


def _attn_fwd(q, k, v, g):
    d, n, _ = q.shape
    nb = n // QBLK

    def body(q_ref, kp_ref, kc_ref, vp_ref, vc_ref, o_ref, l_ref, s_scr, p_scr):
        i = pl.program_id(1)
        mask = _attn_mask(i)
        first = _first_head_lanes()
        for pr in range(A_HEADS // 2):
            ps = slice(pr * LANES, (pr + 1) * LANES)
            kc = jnp.concatenate([kp_ref[:, ps], kc_ref[:, ps]], axis=0)
            for e, qh in enumerate(_split_heads(q_ref[:, ps], first)):
                s_scr[2 * pr + e] = lax.dot_general(qh, kc, NT_DIMS, preferred_element_type=F32)
        lane = lax.broadcasted_iota(jnp.int32, (1, LANES), 1)
        lrow = jnp.zeros((QBLK, LANES), F32)
        for h in range(A_HEADS):
            s = jnp.where(mask, s_scr[h] * (HEAD_DIM ** -0.5), NEG)
            m = jnp.max(s, axis=-1, keepdims=True)
            p = jnp.exp(s - m)
            den = jnp.sum(p, axis=-1, keepdims=True)
            p_scr[h] = (p / den).astype(BF16)
            lrow = jnp.where(lane == h, m + jnp.log(den), lrow)
        l_ref[...] = lrow
        for pr in range(A_HEADS // 2):
            ps = slice(pr * LANES, (pr + 1) * LANES)
            va, vb = _split_heads(jnp.concatenate([vp_ref[:, ps], vc_ref[:, ps]], axis=0), first)
            o_ref[:, ps] = (jnp.dot(p_scr[2 * pr], va, preferred_element_type=F32) +
                            jnp.dot(p_scr[2 * pr + 1], vb, preferred_element_type=F32)).astype(BF16)

    prev = lambda r, i: (r, jnp.maximum(i - 1, 0), 0)
    cur = lambda r, i: (r, i, 0)
    return _call(
        body, "attn_fwd_g%d" % g, (d, nb),
        [pl.BlockSpec(ATT_BLK, cur), pl.BlockSpec(ATT_BLK, prev), pl.BlockSpec(ATT_BLK, cur),
         pl.BlockSpec(ATT_BLK, prev), pl.BlockSpec(ATT_BLK, cur)],
        [pl.BlockSpec(ATT_BLK, cur), pl.BlockSpec(ATT_CBLK, cur)],
        [jax.ShapeDtypeStruct((d, n, A_WIDTH), BF16), jax.ShapeDtypeStruct((d, n, LANES), F32)],
        scratch=[pltpu.VMEM((A_HEADS, QBLK, 2 * QBLK), F32), pltpu.VMEM((A_HEADS, QBLK, 2 * QBLK), BF16)],
        sem=("parallel", "parallel"))(q, k, k, v, v)


def _attn_bwd(q, k, v, do, lse, cg, g):
    d, n, _ = q.shape
    nb = n // QBLK
    scale = HEAD_DIM ** -0.5

    def body(q_ref, kp_ref, kc_ref, vp_ref, vc_ref, do_ref, l_ref, c_ref, dq_ref, dk_ref, dv_ref, ck, cv,
             s_scr, dp_scr, p_scr, ds_scr):
        i = pl.program_id(1)

        @pl.when(i == 0)
        def _():
            ck[...] = jnp.zeros_like(ck)
            cv[...] = jnp.zeros_like(cv)

        @pl.when(i < nb)
        def _():
            mask = _attn_mask(i)
            first = _first_head_lanes()
            for pr in range(A_HEADS // 2):
                ps = slice(pr * LANES, (pr + 1) * LANES)
                kc = jnp.concatenate([kp_ref[:, ps], kc_ref[:, ps]], axis=0)
                vc = jnp.concatenate([vp_ref[:, ps], vc_ref[:, ps]], axis=0)
                qs = _split_heads(q_ref[:, ps], first)
                dos = _split_heads(do_ref[:, ps], first)
                for e in range(2):
                    s_scr[2 * pr + e] = lax.dot_general(qs[e], kc, NT_DIMS, preferred_element_type=F32)
                    dp_scr[2 * pr + e] = lax.dot_general(dos[e], vc, NT_DIMS, preferred_element_type=F32)
            for h in range(A_HEADS):
                p = jnp.where(mask, jnp.exp(s_scr[h] * scale - l_ref[:, h:h + 1]), 0.0)
                p_scr[h] = p.astype(BF16)
                ds_scr[h] = (p * (dp_scr[h] + c_ref[:, h:h + 1]) * scale).astype(BF16)
            for pr in range(A_HEADS // 2):
                ps = slice(pr * LANES, (pr + 1) * LANES)
                ks = _split_heads(jnp.concatenate([kp_ref[:, ps], kc_ref[:, ps]], axis=0), first)
                qs = _split_heads(q_ref[:, ps], first)
                dos = _split_heads(do_ref[:, ps], first)
                dq = dkc = dvc = None
                for e in range(2):
                    ds = ds_scr[2 * pr + e]
                    a = jnp.dot(ds, ks[e], preferred_element_type=F32)
                    b = lax.dot_general(ds, qs[e], TN_DIMS, preferred_element_type=F32)
                    c = lax.dot_general(p_scr[2 * pr + e], dos[e], TN_DIMS, preferred_element_type=F32)
                    dq, dkc, dvc = (a, b, c) if e == 0 else (dq + a, dkc + b, dvc + c)
                dq_ref[:, ps] = dq.astype(BF16)
                dk_ref[:, ps] = (ck[:, ps] + dkc[:QBLK]).astype(BF16)
                dv_ref[:, ps] = (cv[:, ps] + dvc[:QBLK]).astype(BF16)
                ck[:, ps] = dkc[QBLK:]
                cv[:, ps] = dvc[QBLK:]

        @pl.when(i == nb)
        def _():
            dk_ref[...] = ck[...].astype(BF16)
            dv_ref[...] = cv[...].astype(BF16)

    qi = lambda i: jnp.minimum(i, nb - 1)
    cur = lambda r, i: (r, qi(i), 0)
    prev = lambda r, i: (r, jnp.maximum(qi(i) - 1, 0), 0)
    late = lambda r, i: (r, jnp.maximum(i - 1, 0), 0)
    return _call(
        body, "attn_bwd_g%d" % g, (d, nb + 1),
        [pl.BlockSpec(ATT_BLK, cur), pl.BlockSpec(ATT_BLK, prev), pl.BlockSpec(ATT_BLK, cur),
         pl.BlockSpec(ATT_BLK, prev), pl.BlockSpec(ATT_BLK, cur),
         pl.BlockSpec(ATT_BLK, cur), pl.BlockSpec(ATT_CBLK, cur), pl.BlockSpec(ATT_CBLK, cur)],
        [pl.BlockSpec(ATT_BLK, cur), pl.BlockSpec(ATT_BLK, late), pl.BlockSpec(ATT_BLK, late)],
        [jax.ShapeDtypeStruct((d, n, A_WIDTH), BF16)] * 3,
        scratch=[pltpu.VMEM((QBLK, A_WIDTH), F32), pltpu.VMEM((QBLK, A_WIDTH), F32),
                 pltpu.VMEM((A_HEADS, QBLK, 2 * QBLK), F32), pltpu.VMEM((A_HEADS, QBLK, 2 * QBLK), F32),
                 pltpu.VMEM((A_HEADS, QBLK, 2 * QBLK), BF16), pltpu.VMEM((A_HEADS, QBLK, 2 * QBLK), BF16)],
        sem=("parallel", "arbitrary"))(q, k, k, v, v, do, lse, cg)


def _merge_weights(l0, l1, l2):
    mx = jnp.maximum(jnp.maximum(l0, l1), l2)
    e0, e1, e2 = jnp.exp(l0 - mx), jnp.exp(l1 - mx), jnp.exp(l2 - mx)
    den = e0 + e1 + e2
    return e0 / den, e1 / den, e2 / den


def _even_specs(T, S):
    t8 = T // SUBLANES
    last8 = S // SUBLANES - 1
    col = lambda c: pl.BlockSpec((T, A_WIDTH), lambda i: (i, c))
    prev8 = lambda c: pl.BlockSpec((SUBLANES, A_WIDTH), lambda i: (jnp.maximum(i * t8 - 1, 0), c))
    next8 = lambda c: pl.BlockSpec((SUBLANES, A_WIDTH), lambda i: (jnp.minimum((i + 1) * t8, last8), c))
    return col, prev8, next8


GROUP_D = tuple(d for _, d in A_GROUPS)


def _even_mixer_fwd(x, proj, os_, ls_, conv_w, w_out, T):
    S = proj.shape[0]
    col, prev8, _ = _even_specs(T, S)
    H = SUBLANES

    def body(x_ref, w_ref, bg_r, cg_r, hb_r, zl_r, zh_r, cgp_r, hbp_r, o0, o1, o2, l0, l1, l2, cw_r,
             x1_ref, ut_ref, ext, cscr, *scr):
        i = pl.program_id(0)
        ls = [_compact_to_tokens(r, cscr, GROUP_D[g], T) for g, r in enumerate((l0, l1, l2))]
        expand = _head_expander()
        ws = [_segsum(w, expand) for w in _merge_weights(*ls)]
        oa = ws[0] * _to_tokens(o0, scr[0], GROUP_D[0], T)
        oa = oa + ws[1] * _to_tokens(o1, scr[1], GROUP_D[1], T)
        oa = oa + ws[2] * _to_tokens(o2, scr[2], GROUP_D[2], T)
        ext[0:H, :] = jnp.where(i == 0, 0.0, cgp_r[...] * hbp_r[...])
        ext[H:H + T, :] = cg_r[...] * hb_r[...]
        conv = cw_r[0:1, :] * ext[H - 2:H - 2 + T, :]
        for kk in range(1, SC_WIDTH):
            conv = conv + cw_r[kk:kk + 1, :] * ext[H - 2 + kk:H - 2 + kk + T, :]
        zl, zh = zl_r[...], zh_r[...]
        x1_ref[...] = x_ref[...] + _out_projection(ut_ref, w_ref, oa * (zl * _sig(zl)),
                                                   bg_r[...] * conv * (zh * _sig(zh)))

    streams = [_stream_spec(d, T) for d in GROUP_D]
    compacts = [_compact_spec(d, T) for d in GROUP_D]
    row = pl.BlockSpec((T, D_MODEL), lambda i: (i, 0))
    return _call(
        body, "even_mixer_fwd", (S // T,),
        [row, pl.BlockSpec((D_MODEL, D_MODEL), lambda i: (0, 0)),
         col(9), col(10), col(11), col(12), col(13), prev8(10), prev8(11)] + streams + compacts +
        [pl.BlockSpec((SC_WIDTH, A_WIDTH), lambda i: (0, 0))],
        [row, pl.BlockSpec((D_MODEL, T), lambda i: (0, i))],
        [jax.ShapeDtypeStruct((S, D_MODEL), F32), jax.ShapeDtypeStruct((D_MODEL, S), BF16)],
        scratch=[pltpu.VMEM((T + H, A_WIDTH), F32), pltpu.VMEM((T, LANES), F32)] + [_chunked(T)] * 3,
        sem=("parallel",))(
            x, w_out, proj, proj, proj, proj, proj, proj, proj, *os_, *ls_, conv_w)


def _even_mixer_bwd(dy, w_out, proj, os_, ls_, conv_w, T):
    S = proj.shape[0]
    nt = S // T
    col, prev8, next8 = _even_specs(T, S)
    H = SUBLANES
    t8 = T // SUBLANES
    last8 = S // SUBLANES - 1

    def body(dy_r, dyn_r, w_ref, bg_r, cg_r, hb_r, zl_r, zh_r, cgp_r, hbp_r, zhn_r, bgn_r,
             o0, o1, o2, l0, l1, l2, cw_r,
             do0, do1, do2, c0, c1, c2, dr_ref, dcw_ref, ext_t, ext_d, acc, cscr, s_a, s_b, s_c):
        i = pl.program_id(0)

        @pl.when(i == 0)
        def _():
            acc[...] = jnp.zeros_like(acc)

        zl, zh = zl_r[...], zh_r[...]
        sl, sh = _sig(zl), _sig(zh)
        du = lax.dot_general(dy_r[...].astype(BF16), w_ref[...], NT_DIMS, preferred_element_type=F32)
        dul, duh = du[:, 0:A_WIDTH], du[:, A_WIDTH:]
        dun = lax.dot_general(dyn_r[...].astype(BF16), w_ref[A_WIDTH:, :], NT_DIMS, preferred_element_type=F32)
        scr = (s_a, s_b, s_c)
        ls = [_compact_to_tokens(r, cscr, GROUP_D[g], T) for g, r in enumerate((l0, l1, l2))]
        wcs = _merge_weights(*ls)
        expand = _head_expander()
        ws = [_segsum(w, expand) for w in wcs]
        oa = ws[0] * _to_tokens(o0, scr[0], GROUP_D[0], T)
        oa = oa + ws[1] * _to_tokens(o1, scr[1], GROUP_D[1], T)
        oa = oa + ws[2] * _to_tokens(o2, scr[2], GROUP_D[2], T)
        doa = dul * (zl * sl)
        rsum = _segsum(doa * oa, _head_reducer())
        for g, (do_ref, c_ref) in enumerate(((do0, c0), (do1, c1), (do2, c2))):
            d = GROUP_D[g]
            _compact_from_tokens(c_ref, cscr, -wcs[g] * rsum, d, T)
            if d == 1:
                do_ref[0] = (ws[g] * doa).astype(BF16)
            else:
                _put(s_c, ws[g] * doa)
                _from_tokens(do_ref, s_c, d, T)
        cgv, hbv, bgv = cg_r[...], hb_r[...], bg_r[...]
        ext_t[0:H, :] = jnp.where(i == 0, 0.0, cgp_r[...] * hbp_r[...])
        ext_t[H:H + T, :] = cgv * hbv
        conv = cw_r[0:1, :] * ext_t[H - 2:H - 2 + T, :]
        for kk in range(1, SC_WIDTH):
            conv = conv + cw_r[kk:kk + 1, :] * ext_t[H - 2 + kk:H - 2 + kk + T, :]
        dyb = duh * (zh * sh)
        dconv = dyb * bgv
        zn = zhn_r[...]
        ext_d[0:T, :] = dconv
        ext_d[T:T + H, :] = jnp.where(i == nt - 1, 0.0, dun * (zn * _sig(zn)) * bgn_r[...])
        dt = cw_r[0:1, :] * ext_d[2:2 + T, :]
        for kk in range(1, SC_WIDTH):
            dt = dt + cw_r[kk:kk + 1, :] * ext_d[2 - kk:2 - kk + T, :]
        for kk in range(SC_WIDTH):
            acc[kk * SUBLANES:(kk + 1) * SUBLANES, :] += _cs8(dconv * ext_t[H - 2 + kk:H - 2 + kk + T, :])
        dr_ref[:, 0:A_WIDTH] = (dyb * conv).astype(BF16)
        dr_ref[:, A_WIDTH:2 * A_WIDTH] = (dt * hbv).astype(BF16)
        dr_ref[:, 2 * A_WIDTH:3 * A_WIDTH] = (dt * cgv).astype(BF16)
        dr_ref[:, 3 * A_WIDTH:4 * A_WIDTH] = (dul * oa * _dsilu(zl, sl)).astype(BF16)
        dr_ref[:, 4 * A_WIDTH:5 * A_WIDTH] = (duh * (bgv * conv) * _dsilu(zh, sh)).astype(BF16)

        @pl.when(i == nt - 1)
        def _():
            for kk in range(SC_WIDTH):
                dcw_ref[kk:kk + 1, :] = jnp.sum(acc[kk * SUBLANES:(kk + 1) * SUBLANES, :], axis=0, keepdims=True)

    streams = [_stream_spec(d, T) for d in GROUP_D]
    dynext = pl.BlockSpec((SUBLANES, D_MODEL), lambda i: (jnp.minimum((i + 1) * t8, last8), 0))
    compacts = [_compact_spec(d, T) for d in GROUP_D]
    outs = _call(
        body, "even_mixer_bwd", (nt,),
        [pl.BlockSpec((T, D_MODEL), lambda i: (i, 0)), dynext, pl.BlockSpec((D_MODEL, D_MODEL), lambda i: (0, 0)),
         col(9), col(10), col(11), col(12), col(13), prev8(10), prev8(11), next8(13), next8(9)] +
        streams + compacts + [pl.BlockSpec((SC_WIDTH, A_WIDTH), lambda i: (0, 0))],
        streams + compacts + [pl.BlockSpec((T, 5 * A_WIDTH), lambda i: (i, 0)),
                              pl.BlockSpec((SC_WIDTH, A_WIDTH), lambda i: (0, 0))],
        [_stream_shape(d, S, BF16) for d in GROUP_D] + [_compact_shape(d, S) for d in GROUP_D] +
        [jax.ShapeDtypeStruct((S, 5 * A_WIDTH), BF16), jax.ShapeDtypeStruct((SC_WIDTH, A_WIDTH), F32)],
        scratch=[pltpu.VMEM((T + H, A_WIDTH), F32), pltpu.VMEM((T + H, A_WIDTH), F32),
                 pltpu.VMEM((SC_WIDTH * SUBLANES, A_WIDTH), F32), pltpu.VMEM((T, LANES), F32)] +
                [_chunked(T)] * 3,
        sem=("arbitrary",))(dy, dy, w_out, proj, proj, proj, proj, proj, proj, proj, proj, proj, *os_, *ls_, conv_w)
    return outs[0:3], outs[3:6], outs[6], outs[7]


def _qk_bwd(proj, dqs, dks, dvs, pos, freq, wq, wk, T):
    S = proj.shape[0]
    nt = S // T
    qk_w = 3 * A_WIDTH

    def body(q_ref, k_ref, dq0, dq1, dq2, dk0, dk1, dk2, dv0, dv1, dv2, pos_ref, f_ref, wq_ref, wk_ref,
             o_ref, dw_ref, acc, scr):
        i = pl.program_id(0)

        @pl.when(i == 0)
        def _():
            acc[...] = jnp.zeros_like(acc)
            dw_ref[...] = jnp.zeros_like(dw_ref)

        ones = _seg_ones()
        c, s1, s2 = _rope_tables(pos_ref, f_ref)
        for t, (src, w_ref, ds) in enumerate(((q_ref, wq_ref, (dq0, dq1, dq2)), (k_ref, wk_ref, (dk0, dk1, dk2)))):
            wv = w_ref[...]
            for g in range(3):
                d = GROUP_D[g]
                if d > 1:
                    _to_tokens(ds[g], scr, d, T)
                for ch in range(A_WIDTH // LANES):
                    cs = slice(g * A_WIDTH + ch * LANES, g * A_WIDTH + (ch + 1) * LANES)
                    lc = slice(ch * LANES, (ch + 1) * LANES)
                    v = src[:, cs]
                    dout = ds[g][0, :, lc].astype(F32) if d == 1 else scr[ch]
                    rs = lax.rsqrt(_segsum(v * v, ones) * (1.0 / HEAD_DIM) + EPS)
                    xh = v * rs
                    dy = dout * c + pltpu.roll(dout * s1, LANES - 8, 1) + pltpu.roll(dout * s2, 8, 1)
                    acc[t * SUBLANES:(t + 1) * SUBLANES, :] += _cs8(dy * xh)
                    dxh = dy * wv
                    mean = _segsum(dxh * xh, ones) * (1.0 / HEAD_DIM)
                    o_ref[:, t * qk_w + g * A_WIDTH + ch * LANES: t * qk_w + g * A_WIDTH + (ch + 1) * LANES] = (
                        rs * (dxh - xh * mean)).astype(BF16)
        for g, dv in enumerate((dv0, dv1, dv2)):
            d = GROUP_D[g]
            base = 2 * qk_w + g * A_WIDTH
            o_ref[:, base:base + A_WIDTH] = _to_tokens(dv, scr, d, T).astype(BF16)

        @pl.when(i == nt - 1)
        def _():
            for t in range(2):
                srow = jnp.sum(acc[t * SUBLANES:(t + 1) * SUBLANES, :], axis=0, keepdims=True)
                dw_ref[t:t + 1, :] = srow + pltpu.roll(srow, HEAD_DIM, 1)

    streams = [_stream_spec(d, T) for d in GROUP_D]
    return _call(
        body, "qk_bwd", (nt,),
        [pl.BlockSpec((T, qk_w), lambda i: (i, 0)), pl.BlockSpec((T, qk_w), lambda i: (i, 1))] + streams * 3 +
        [pl.BlockSpec((T, 1), lambda i: (i, 0)), pl.BlockSpec((1, LANES), lambda i: (0, 0)),
         pl.BlockSpec((1, LANES), lambda i: (0, 0)), pl.BlockSpec((1, LANES), lambda i: (0, 0))],
        [pl.BlockSpec((T, 3 * qk_w), lambda i: (i, 0)), pl.BlockSpec((SUBLANES, LANES), lambda i: (0, 0))],
        [jax.ShapeDtypeStruct((S, 3 * qk_w), BF16), jax.ShapeDtypeStruct((SUBLANES, LANES), F32)],
        scratch=[pltpu.VMEM((2 * SUBLANES, LANES), F32), _chunked(T)], sem=("arbitrary",))(
            proj, proj, *dqs, *dks, *dvs, pos, freq, wq, wk)


N_SMALL_ODD = 40
SHIFT_ROWS_LESS = SUBLANES


def _fill_shifted(ext_ref, sh_ref):
    rows = ext_ref.shape[0] - SHIFT_ROWS_LESS
    for b in range(1, SUBLANES):
        sh_ref[b - 1] = ext_ref[b:b + rows, :]


def _window(ext_ref, sh_ref, off, T):
    a, b = divmod(off, SUBLANES)
    if b == 0:
        return ext_ref[off:off + T, :]
    return sh_ref[b - 1, a * SUBLANES:a * SUBLANES + T, :]


def _odd_pool_tile(i, uc_r, ucp_r, pw_r, ext_u, pooled_s, pm_s, T):
    H = HALO
    uc = uc_r[...]
    ext_u[0:H, :] = jnp.where(i == 0, 0.0, ucp_r[...])
    ext_u[H:H + T, :] = uc
    row = i * T + lax.broadcasted_iota(jnp.int32, (T, 1), 0)
    for g, p in enumerate(POOL_SIZES):
        cs = slice(g * LANES, (g + 1) * LANES)
        win = ext_u[H:H + T, cs]
        for j in range(1, p):
            win = win + ext_u[H - j:H - j + T, cs]
        cnt = jnp.minimum(row + 1, p).astype(F32)
        pooled = win / cnt - uc[:, cs]
        pooled_s[:, cs] = pooled
        pm_s[:, cs] = jnp.dot(pooled.astype(BF16), pw_r[g].astype(BF16), preferred_element_type=F32)
    return row


def _odd_glu_tile(i, da_r, dg_r, dap_r, dgp_r, ext_g, sh_g, T):
    H = HALO
    ext_g[0:H, :] = jnp.where(i == 0, 0.0, dap_r[...] * _sig(dgp_r[...]))
    ext_g[H:H + T, :] = da_r[...] * _sig(dg_r[...])
    _fill_shifted(ext_g, sh_g)


def _odd_specs(T, S, order):
    tb = T // HALO
    col = lambda c: pl.BlockSpec((T, A_WIDTH), lambda s: (order(s), c))
    prev = lambda c: pl.BlockSpec((HALO, A_WIDTH), lambda s: (jnp.maximum(order(s) * tb - 1, 0), c))
    const2 = lambda shape: pl.BlockSpec(shape, lambda s: (0, 0))
    weights = [pl.BlockSpec((4, LANES, LANES), lambda s: (0, 0, 0)), const2((1, A_WIDTH)),
               const2((D_CONV, A_WIDTH)), const2((1, A_WIDTH)), const2((1, A_WIDTH)), const2((1, A_WIDTH))]
    return col, prev, weights


def _odd_mixer_fwd(x, tgt, proj, pool_w, scale, dconv_w, dconv_b, ln_w, ln_b, w_out, T):
    S = proj.shape[0]
    nt = S // T
    col, prev, wspecs = _odd_specs(T, S, lambda s: s)
    H = HALO

    def body(x_ref, t_ref, w_ref, uc_r, da_r, dg_r, zl_r, zh_r, ucp_r, dap_r, dgp_r, pw_r, sc_r, dw_r, db_r,
             lw_r, lb_r, dy_ref, l_ref, ut_ref, cv_ref, ext_u, ext_g, sh_g, pooled_s, pm_s, lacc):
        i = pl.program_id(0)

        @pl.when(i == 0)
        def _():
            lacc[...] = jnp.zeros_like(lacc)

        _odd_pool_tile(i, uc_r, ucp_r, pw_r, ext_u, pooled_s, pm_s, T)
        _odd_glu_tile(i, da_r, dg_r, dap_r, dgp_r, ext_g, sh_g, T)
        base = H - (D_CONV - 1)
        conv = db_r[...] + dw_r[0:1, :] * _window(ext_g, sh_g, base, T)
        for kk in range(1, D_CONV):
            conv = conv + dw_r[kk:kk + 1, :] * _window(ext_g, sh_g, base + kk, T)
        cv_ref[...] = conv
        mu = jnp.mean(conv, axis=-1, keepdims=True)
        xc = conv - mu
        yh = xc * lax.rsqrt(jnp.mean(xc * xc, axis=-1, keepdims=True) + EPS)
        ln = yh * lw_r[...] + lb_r[...]
        zl, zh = zl_r[...], zh_r[...]
        y = x_ref[...] + _out_projection(ut_ref, w_ref, pm_s[...] * sc_r[...] * (zl * _sig(zl)),
                                         ln * _sig(ln) * (zh * _sig(zh)))
        diff = y - t_ref[...]
        dy_ref[...] = diff / float(D_MODEL)
        lacc[...] += _cs8(diff * diff)

        @pl.when(i == nt - 1)
        def _():
            l_ref[...] = jnp.sum(lacc[...], axis=0, keepdims=True)

    row = pl.BlockSpec((T, D_MODEL), lambda i: (i, 0))
    return _call(
        body, "odd_mixer_fwd", (nt,),
        [row, row, pl.BlockSpec((D_MODEL, D_MODEL), lambda i: (0, 0)),
         col(0), col(1), col(2), col(3), col(4), prev(0), prev(1), prev(2)] + wspecs,
        [row, pl.BlockSpec((1, D_MODEL), lambda i: (0, 0)), pl.BlockSpec((D_MODEL, T), lambda i: (0, i)),
         pl.BlockSpec((T, A_WIDTH), lambda i: (i, 0))],
        [jax.ShapeDtypeStruct((S, D_MODEL), F32), jax.ShapeDtypeStruct((1, D_MODEL), F32),
         jax.ShapeDtypeStruct((D_MODEL, S), BF16), jax.ShapeDtypeStruct((S, A_WIDTH), F32)],
        scratch=[pltpu.VMEM((T + H, A_WIDTH), F32), pltpu.VMEM((T + H, A_WIDTH), F32),
                 pltpu.VMEM((SUBLANES - 1, T + H - SHIFT_ROWS_LESS, A_WIDTH), F32),
                 pltpu.VMEM((T, A_WIDTH), F32), pltpu.VMEM((T, A_WIDTH), F32),
                 pltpu.VMEM((SUBLANES, D_MODEL), F32)],
        sem=("arbitrary",))(x, tgt, w_out, proj, proj, proj, proj, proj, proj, proj, proj,
                            pool_w, scale, dconv_w, dconv_b, ln_w, ln_b)


def _odd_mixer_bwd(dy, w_out, proj, conv, pool_w, scale, dconv_w, dconv_b, ln_w, ln_b, T):
    S = proj.shape[0]
    nt = S // T
    order = lambda s: nt - 1 - s
    col, prev, wspecs = _odd_specs(T, S, order)
    H = HALO

    def body(dy_r, w_ref, cv_r, uc_r, da_r, dg_r, zl_r, zh_r, ucp_r, dap_r, dgp_r, pw_r, sc_r, dw_r, db_r, lw_r, lb_r,
             dp_ref, dpw_ref, sm_ref, ext_u, ext_g, sh_g, pooled_s, pm_s, dpl_s, ext_p, ext_c, sh_c, acc):
        step = pl.program_id(0)
        i = nt - 1 - step

        @pl.when(step == 0)
        def _():
            ext_p[T:T + H, :] = jnp.zeros((H, A_WIDTH), F32)
            ext_c[T:T + H, :] = jnp.zeros((H, A_WIDTH), F32)
            acc[...] = jnp.zeros_like(acc)
            dpw_ref[...] = jnp.zeros_like(dpw_ref)

        def accum(r, v):
            acc[r * SUBLANES:(r + 1) * SUBLANES, :] += _cs8(v)

        row = _odd_pool_tile(i, uc_r, ucp_r, pw_r, ext_u, pooled_s, pm_s, T)
        _odd_glu_tile(i, da_r, dg_r, dap_r, dgp_r, ext_g, sh_g, T)
        conv = cv_r[...]
        mu = jnp.mean(conv, axis=-1, keepdims=True)
        xc = conv - mu
        rstd = lax.rsqrt(jnp.mean(xc * xc, axis=-1, keepdims=True) + EPS)
        yh = xc * rstd
        ln = yh * lw_r[...] + lb_r[...]
        sln = _sig(ln)
        zl, zh = zl_r[...], zh_r[...]
        sl, sh = _sig(zl), _sig(zh)
        du = lax.dot_general(dy_r[...].astype(BF16), w_ref[...], NT_DIMS, preferred_element_type=F32)
        dul, duh = du[:, 0:A_WIDTH], du[:, A_WIDTH:]
        pm = pm_s[...]
        scv = sc_r[...]
        dyc = dul * (zl * sl)
        accum(34, dyc * pm)
        dpm = dyc * scv
        for g in range(len(POOL_SIZES)):
            cs = slice(g * LANES, (g + 1) * LANES)
            dpm_g = dpm[:, cs].astype(BF16)
            dpw_ref[g] += lax.dot_general(pooled_s[:, cs].astype(BF16), dpm_g, TN_DIMS, preferred_element_type=F32)
            dpl_s[:, cs] = lax.dot_general(dpm_g, pw_r[g].astype(BF16), NT_DIMS, preferred_element_type=F32)
        lane_p = lax.broadcasted_iota(jnp.int32, (1, A_WIDTH), 1) // LANES
        pvec = jnp.left_shift(2, lane_p)
        cnt = jnp.minimum(row + 1, pvec).astype(F32)
        dpl = dpl_s[...]
        ext_p[0:T, :] = dpl / cnt
        for g, p in enumerate(POOL_SIZES):
            cs = slice(g * LANES, (g + 1) * LANES)
            win = ext_p[0:T, cs]
            for j in range(1, p):
                win = win + ext_p[j:j + T, cs]
            dp_ref[:, cs] = (win - dpl[:, cs]).astype(BF16)
        ext_p[T:T + H, :] = ext_p[0:H, :]
        dln = duh * (zh * sh) * _dsilu(ln, sln)
        accum(32, dln * yh)
        accum(33, dln)
        dyh = dln * lw_r[...]
        dc = rstd * (dyh - jnp.mean(dyh, axis=-1, keepdims=True) - yh * jnp.mean(dyh * yh, axis=-1, keepdims=True))
        accum(31, dc)
        ext_c[0:T, :] = dc
        _fill_shifted(ext_c, sh_c)
        base = H - (D_CONV - 1)
        dgl = dw_r[0:1, :] * _window(ext_c, sh_c, D_CONV - 1, T)
        accum(0, dc * _window(ext_g, sh_g, base, T))
        for kk in range(1, D_CONV):
            dgl = dgl + dw_r[kk:kk + 1, :] * _window(ext_c, sh_c, D_CONV - 1 - kk, T)
            accum(kk, dc * _window(ext_g, sh_g, base + kk, T))
        ext_c[T:T + H, :] = ext_c[0:H, :]
        dav, dgv = da_r[...], dg_r[...]
        sg = _sig(dgv)
        dp_ref[:, A_WIDTH:2 * A_WIDTH] = (dgl * sg).astype(BF16)
        dp_ref[:, 2 * A_WIDTH:3 * A_WIDTH] = (dgl * dav * sg * (1.0 - sg)).astype(BF16)
        dp_ref[:, 3 * A_WIDTH:4 * A_WIDTH] = (dul * (pm * scv) * _dsilu(zl, sl)).astype(BF16)
        dp_ref[:, 4 * A_WIDTH:5 * A_WIDTH] = (duh * (ln * sln) * _dsilu(zh, sh)).astype(BF16)

        @pl.when(step == nt - 1)
        def _():
            for r in range(N_SMALL_ODD):
                sm_ref[r:r + 1, :] = jnp.sum(acc[r * SUBLANES:(r + 1) * SUBLANES, :], axis=0, keepdims=True)

    ext = pltpu.VMEM((T + H, A_WIDTH), F32)
    shifted = pltpu.VMEM((SUBLANES - 1, T + H - SHIFT_ROWS_LESS, A_WIDTH), F32)
    tile = pltpu.VMEM((T, A_WIDTH), F32)
    return _call(
        body, "odd_mixer_bwd", (nt,),
        [pl.BlockSpec((T, D_MODEL), lambda s: (order(s), 0)), pl.BlockSpec((D_MODEL, D_MODEL), lambda s: (0, 0)),
         pl.BlockSpec((T, A_WIDTH), lambda s: (order(s), 0)),
         col(0), col(1), col(2), col(3), col(4), prev(0), prev(1), prev(2)] + wspecs,
        [pl.BlockSpec((T, ODD_IN), lambda s: (order(s), 0)),
         pl.BlockSpec((4, LANES, LANES), lambda s: (0, 0, 0)),
         pl.BlockSpec((N_SMALL_ODD, A_WIDTH), lambda s: (0, 0))],
        [jax.ShapeDtypeStruct((S, ODD_IN), BF16), jax.ShapeDtypeStruct((4, LANES, LANES), F32),
         jax.ShapeDtypeStruct((N_SMALL_ODD, A_WIDTH), F32)],
        scratch=[ext, ext, shifted, tile, tile, tile, ext, ext, shifted,
                 pltpu.VMEM((N_SMALL_ODD * SUBLANES, A_WIDTH), F32)],
        sem=("arbitrary",))(dy, w_out, conv, proj, proj, proj, proj, proj, proj, proj, proj,
                            pool_w, scale, dconv_w, dconv_b, ln_w, ln_b)


TILE_SEQ = 256
TILE_WG = 256


LATE_WEIGHTS = ("e_w_out", "o_w_in", "o_w_out")
ODD_MATS = ("o_w_in", "o_w_out")
EVEN_MATS = ("e_w_in", "e_w_out")


def _reduce_start(names, grads, cidx):
    recv = _swap_to_sibling(names, [grads[n] for n in names], "swap_halves_" + names[0][0], True)
    both = [_add_half(cidx, grads[n], r, n) for n, r in zip(names, recv)]
    return [h for h, _ in both], [hb for _, hb in both]


def _local_step(x, pos, tgt, w_e_in, shards, p, cidx, bidx):
    T = TILE_SEQ
    freq = _freq_table()
    wq = jnp.tile(p["e_q_norm_w"], (1, LANES // HEAD_DIM))
    wk = jnp.tile(p["e_k_norm_w"], (1, LANES // HEAD_DIM))

    proj_e, ht_e, *late = _inproj(x, p["e_norm_w"], w_e_in, TILE_SEQ, 1792, "inproj_even",
                                  LATE_WEIGHTS, [shards[n] for n in LATE_WEIGHTS])
    wb = dict(zip(LATE_WEIGHTS, late), e_w_in=w_e_in)
    qkv = _qkv_prep(proj_e, pos, freq, wq, wk, T)
    qs, ks, vs = qkv[0:3], qkv[3:6], qkv[6:9]
    os_, ls_ = [], []
    for g in range(3):
        o, l = _attn_fwd(qs[g], ks[g], vs[g], g)
        os_.append(o)
        ls_.append(l)
    x1, ut_e = _even_mixer_fwd(x, proj_e, os_, ls_, p["e_conv_w"], wb["e_w_out"], T)
    proj_o, ht_o = _inproj(x1, p["o_norm_w"], wb["o_w_in"], TILE_SEQ, 1280, "inproj_odd")
    odd_w = (p["o_pool_w"], p["o_pool_scale"], p["o_dconv_w"], p["o_dconv_b"], p["o_ln_w"], p["o_ln_b"])
    dy, lsum, ut_o, conv_o = _odd_mixer_fwd(x1, tgt, proj_o, *odd_w, wb["o_w_out"], T)

    g = {}
    g["o_w_out"] = _mm_wgrad(ut_o, [dy], TILE_WG, "wgrad_o_out")
    dproj_o, g["o_pool_w"], small_o = _odd_mixer_bwd(dy, wb["o_w_out"], proj_o, conv_o, *odd_w, T)
    g["o_w_in"] = _mm_wgrad(ht_o, [dproj_o], TILE_WG, "wgrad_o_in")
    half_o, half_o16 = _reduce_start(ODD_MATS, g, cidx)
    dx1, g["o_norm_w"], blocks_o = _mm_nt_rms([dproj_o], wb["o_w_in"], x1, p["o_norm_w"], dy, TILE_SEQ, "dx_odd",
                                              ODD_MATS, half_o16)
    g["o_dconv_w"] = small_o[0:D_CONV]
    g["o_dconv_b"] = small_o[31:32]
    g["o_ln_w"] = small_o[32:33]
    g["o_ln_b"] = small_o[33:34]
    g["o_pool_scale"] = small_o[34:35]

    g["e_w_out"] = _mm_wgrad(ut_e, [dx1], TILE_WG, "wgrad_e_out")
    dos, cgs, drest, g["e_conv_w"] = _even_mixer_bwd(dx1, wb["e_w_out"], proj_e, os_, ls_, p["e_conv_w"], T)
    dqs, dks, dvs = [], [], []
    for gi in range(3):
        dq, dk, dv = _attn_bwd(qs[gi], ks[gi], vs[gi], dos[gi], ls_[gi], cgs[gi], gi)
        dqs.append(dq)
        dks.append(dk)
        dvs.append(dv)
    dqkv, dnw = _qk_bwd(proj_e, dqs, dks, dvs, pos, freq, wq, wk, T)
    g["e_q_norm_w"] = dnw[0:1, 0:HEAD_DIM]
    g["e_k_norm_w"] = dnw[1:2, 0:HEAD_DIM]
    pieces = [dqkv, drest]
    g["e_w_in"] = _mm_wgrad(ht_e, pieces, TILE_WG, "wgrad_e_in")
    half_e, half_e16 = _reduce_start(EVEN_MATS, g, cidx)
    dx, g["e_norm_w"], blocks_e = _mm_nt_rms(pieces, wb["e_w_in"], x, p["e_norm_w"], dx1, TILE_SEQ, "dx_even",
                                             EVEN_MATS, half_e16)
    parts = {}
    for names, halves, blocks in ((ODD_MATS, half_o, blocks_o), (EVEN_MATS, half_e, blocks_e)):
        for n, h, r in zip(names, halves, blocks):
            parts[n] = _add_blocks(bidx, h, r, n)
    return lsum, dx, g, parts


BIG = ("e_w_in", "e_w_out", "o_w_in", "o_w_out")
SHARD_AXIS = {"e_w_in": 1, "e_w_out": 0, "o_w_in": 1, "o_w_out": 0}
N_CHIPS = 4


def _place():
    x, y, c = lax.axis_index("x"), lax.axis_index("y"), lax.axis_index("c")
    chips = [(1 - x, y), (x, 1 - y), (1 - x, 1 - y)]
    return x, y, c, chips


def _block_of(ref, name, block):
    rows, cols = ref.shape
    if SHARD_AXIS[name] == 1:
        cw = cols // N_CHIPS
        return ref.at[:, pl.ds(pl.multiple_of(block * cw, LANES), cw)]
    rw = rows // N_CHIPS
    return ref.at[pl.ds(pl.multiple_of(block * rw, rw), rw), :]


def _half_of(ref, name, half):
    rows, cols = ref.shape
    if SHARD_AXIS[name] == 1:
        return ref.at[pl.ds(pl.multiple_of(half * (rows // 2), rows // 2), rows // 2), :]
    return ref.at[:, pl.ds(pl.multiple_of(half * (cols // 2), LANES), cols // 2)]


def _sub(ref, name, block, half):
    rows, cols = ref.shape
    if SHARD_AXIS[name] == 1:
        cw, hr = cols // N_CHIPS, rows // 2
        return ref.at[pl.ds(pl.multiple_of(half * hr, hr), hr), pl.ds(pl.multiple_of(block * cw, LANES), cw)]
    rw, hc = rows // N_CHIPS, cols // 2
    return ref.at[pl.ds(pl.multiple_of(block * rw, rw), rw), pl.ds(pl.multiple_of(half * hc, LANES), hc)]


GATHER_COPIES = 7


class _Gather:
    def __init__(self, names, s_refs, f_refs, send, recv):
        self.names, self.s, self.f, self.send, self.recv = names, s_refs, f_refs, send, recv

    def _copy(self, k, src, dst, to):
        return pltpu.make_async_remote_copy(src_ref=src, dst_ref=dst, send_sem=self.send.at[k],
                                            recv_sem=self.recv.at[k], device_id=to, device_id_type=MESH)

    def _plan(self):
        x, y, c, chips = _place()
        me, sib = 2 * x + y, (x, y, 1 - c)
        first, relay_in, relay, last_in = [], [], [], []
        for wi, n in enumerate(self.names):
            k0 = wi * GATHER_COPIES
            s, f = self.s[wi], self.f[wi]
            own = _block_of(f, n, me)
            first.append(self._copy(k0 + 3, s, own, sib))
            last_in.append(self._copy(k0 + 3, s, own, sib))
            for j, (cx, cy) in enumerate(chips):
                first.append(self._copy(k0 + j, _half_of(s, n, c), _sub(f, n, me, c), (cx, cy, c)))
                mine = _sub(f, n, 2 * cx + cy, c)
                relay_in.append(self._copy(k0 + j, mine, mine, sib))
                relay.append(self._copy(k0 + 4 + j, mine, mine, sib))
                theirs = _sub(f, n, 2 * cx + cy, 1 - c)
                last_in.append(self._copy(k0 + 4 + j, theirs, theirs, sib))
        return first, relay_in, relay, last_in

    def begin(self):
        for cp in self._plan()[0]:
            cp.start()

    def relay(self):
        _, relay_in, relay, _ = self._plan()
        for arrived, onward in zip(relay_in, relay):
            arrived.wait_recv()
            onward.start()

    def end(self):
        first, _, relay, last_in = self._plan()
        for cp in last_in:
            cp.wait_recv()
        for cp in first + relay:
            cp.wait_send()


def _full_shape(n, s):
    r, cdim = s.shape
    return jax.ShapeDtypeStruct((r, cdim * N_CHIPS) if SHARD_AXIS[n] == 1 else (r * N_CHIPS, cdim), s.dtype)


def _gather_sems(names):
    k = GATHER_COPIES * len(names)
    return [pltpu.SemaphoreType.DMA((k,)), pltpu.SemaphoreType.DMA((k,))]


def _gather_weights(names, shards):
    nw = len(names)

    def body(*refs):
        g = _Gather(names, refs[:nw], refs[nw:2 * nw], *refs[2 * nw:])
        g.begin()
        g.relay()
        g.end()

    return pl.pallas_call(
        body, name="gather_weights", in_specs=[ANY] * nw, out_specs=[ANY] * nw,
        out_shape=[_full_shape(n, s) for n, s in zip(names, shards)], scratch_shapes=_gather_sems(names),
    )(*shards)


def _scatter_copies(names, h_refs, r_refs, send, recv):
    _, _, c, chips = _place()
    cps = []
    for wi, n in enumerate(names):
        for j, (cx, cy) in enumerate(chips):
            cps.append(pltpu.make_async_remote_copy(
                src_ref=_block_of(h_refs[wi], n, 2 * cx + cy), dst_ref=r_refs[wi].at[j],
                send_sem=send.at[wi * 3 + j], recv_sem=recv.at[wi * 3 + j],
                device_id=(cx, cy, c), device_id_type=MESH))
    return cps


def _scatter_sems(names):
    return [pltpu.SemaphoreType.DMA((3 * len(names),)), pltpu.SemaphoreType.DMA((3 * len(names),))]


def _allreduce_small(part, name):
    R = part.shape[0]

    def body(p_ref, o_ref, sbuf, cbuf, send, recv):
        x, y, c, chips = _place()
        me = 2 * x + y
        sib = (x, y, 1 - c)
        sbuf[c] = p_ref[...]
        mine = sbuf.at[c]
        d2d = pltpu.make_async_remote_copy(src_ref=mine, dst_ref=mine, send_sem=send.at[0], recv_sem=recv.at[0],
                                           device_id=sib, device_id_type=MESH)
        d2d.start()
        theirs = sbuf.at[1 - c]
        pltpu.make_async_remote_copy(src_ref=theirs, dst_ref=theirs, send_sem=send.at[0], recv_sem=recv.at[0],
                                     device_id=sib, device_id_type=MESH).wait_recv()
        cbuf[me] = sbuf[0] + sbuf[1]
        blk = cbuf.at[me]
        sends = [d2d]
        for j, (cx, cy) in enumerate(chips):
            cp = pltpu.make_async_remote_copy(src_ref=blk, dst_ref=blk, send_sem=send.at[1 + j], recv_sem=recv.at[1 + j],
                                              device_id=(cx, cy, c), device_id_type=MESH)
            cp.start()
            sends.append(cp)
        for j, (cx, cy) in enumerate(chips):
            got = cbuf.at[2 * cx + cy]
            pltpu.make_async_remote_copy(src_ref=got, dst_ref=got, send_sem=send.at[1 + j], recv_sem=recv.at[1 + j],
                                         device_id=(cx, cy, c), device_id_type=MESH).wait_recv()
        o_ref[...] = (cbuf[0] + cbuf[1]) + (cbuf[2] + cbuf[3])
        for cp in sends:
            cp.wait_send()

    vm = pl.BlockSpec(memory_space=pltpu.VMEM)
    return pl.pallas_call(
        body, name=name, in_specs=[vm], out_specs=vm,
        out_shape=jax.ShapeDtypeStruct(part.shape, F32),
        scratch_shapes=[pltpu.VMEM((2, R, LANES), F32), pltpu.VMEM((N_CHIPS, R, LANES), F32),
                        pltpu.SemaphoreType.DMA((4,)), pltpu.SemaphoreType.DMA((4,))],
    )(part)


def _half_shape(shape, name):
    r, cdim = shape
    return (r // 2, cdim) if SHARD_AXIS[name] == 1 else (r, cdim // 2)


def _shard_shape(shape, name):
    r, cdim = shape
    return (r, cdim // N_CHIPS) if SHARD_AXIS[name] == 1 else (r // N_CHIPS, cdim)


def _swap_to_sibling(names, srcs, name, pick_half):
    nw = len(names)

    def body(*refs):
        g_refs, r_refs = refs[:nw], refs[nw:2 * nw]
        send, recv = refs[2 * nw:]
        x, y, c, _ = _place()
        sib = (x, y, 1 - c)
        cps = []
        for wi, n in enumerate(names):
            src = _half_of(g_refs[wi], n, 1 - c) if pick_half else g_refs[wi]
            cp = pltpu.make_async_remote_copy(src_ref=src, dst_ref=r_refs[wi], send_sem=send.at[wi],
                                              recv_sem=recv.at[wi], device_id=sib, device_id_type=MESH)
            cp.start()
            cps.append(cp)
        for cp in cps:
            cp.wait()

    outs = [jax.ShapeDtypeStruct(_half_shape(g.shape, n) if pick_half else g.shape, g.dtype)
            for n, g in zip(names, srcs)]
    return pl.pallas_call(
        body, name=name, in_specs=[ANY] * nw, out_specs=[ANY] * nw, out_shape=outs,
        scratch_shapes=[pltpu.SemaphoreType.DMA((nw,)), pltpu.SemaphoreType.DMA((nw,))],
    )(*srcs)


def _add_half(cidx, g, r, name):
    rows, cols = r.shape
    tr = 256
    tc = cols if cols <= 1792 else (1792 if cols % 1792 == 0 else 1280)
    nr, nc = rows // tr, cols // tc

    def body(c_ref, g_ref, r_ref, o_ref, ob_ref):
        s = g_ref[...] + r_ref[...]
        o_ref[...] = s
        ob_ref[...] = s.astype(BF16)

    if SHARD_AXIS[name] == 1:
        gmap = lambda i, j, c_ref: (c_ref[0] * nr + i, j)
    else:
        gmap = lambda i, j, c_ref: (i, c_ref[0] * nc + j)
    same = lambda i, j, c_ref: (i, j)
    return pl.pallas_call(
        body, name="add_half_" + name,
        grid_spec=pltpu.PrefetchScalarGridSpec(
            num_scalar_prefetch=1, grid=(nr, nc),
            in_specs=[pl.BlockSpec((tr, tc), gmap), pl.BlockSpec((tr, tc), same)],
            out_specs=[pl.BlockSpec((tr, tc), same), pl.BlockSpec((tr, tc), same)]),
        out_shape=[jax.ShapeDtypeStruct(r.shape, F32), jax.ShapeDtypeStruct(r.shape, BF16)],
        compiler_params=pltpu.CompilerParams(dimension_semantics=("parallel", "parallel"), vmem_limit_bytes=VMEM_LIMIT),
    )(cidx, g, r)


def _add_blocks(bidx, h, r, name):
    _, rows, cols = r.shape
    tr = min(rows, 256)
    nr = rows // tr

    def body(b_ref, h_ref, r0, r1, r2, o_ref):
        o_ref[...] = ((h_ref[...] + r0[0].astype(F32)) + r1[0].astype(F32)) + r2[0].astype(F32)

    if SHARD_AXIS[name] == 1:
        hmap = lambda i, b_ref: (i, b_ref[0])
    else:
        hmap = lambda i, b_ref: (b_ref[0] * nr + i, 0)
    rspec = lambda j: pl.BlockSpec((1, tr, cols), lambda i, b_ref, j=j: (j, i, 0))
    return pl.pallas_call(
        body, name="add_blocks_" + name,
        grid_spec=pltpu.PrefetchScalarGridSpec(
            num_scalar_prefetch=1, grid=(nr,),
            in_specs=[pl.BlockSpec((tr, cols), hmap), rspec(0), rspec(1), rspec(2)],
            out_specs=pl.BlockSpec((tr, cols), lambda i, b_ref: (i, 0))),
        out_shape=jax.ShapeDtypeStruct((rows, cols), F32),
        compiler_params=pltpu.CompilerParams(dimension_semantics=("parallel",), vmem_limit_bytes=VMEM_LIMIT),
    )(bidx, h, r, r, r)


def _adam_math(w, g, m, v):
    c1 = 1.0 - ADAM_B1 ** ADAM_STEP
    c2 = 1.0 - ADAM_B2 ** ADAM_STEP
    nm = ADAM_B1 * m + (1.0 - ADAM_B1) * g
    nv = ADAM_B2 * v + (1.0 - ADAM_B2) * (g * g)
    delta = -ADAM_LR * ((nm / c1) / (jnp.sqrt(nv / c2) + ADAM_EPS) + ADAM_WD * w)
    return delta, nm, nv


def _adamw(w, g, m, v, name):
    def body(w_ref, g_ref, m_ref, v_ref, d_ref, nm_ref, nv_ref):
        d_ref[...], nm_ref[...], nv_ref[...] = _adam_math(w_ref[...], g_ref[...], m_ref[...], v_ref[...])

    spec = pl.BlockSpec(w.shape, lambda i: (0, 0))
    return _call(body, "adamw_" + name, (1,), [spec] * 4, [spec] * 3,
                 [jax.ShapeDtypeStruct(w.shape, F32)] * 3, sem=("arbitrary",))(w, g, m, v)


def _adamw_halves(cidx, w, mine, theirs, m, v, name):
    rows, cols = w.shape
    hr, hc = mine.shape
    tr = 128
    if SHARD_AXIS[name] == 1:
        ni = hr // tr
        wmap = lambda hh, i, c_ref: (hh * ni + i, 0)
    else:
        ni = hr // tr
        wmap = lambda hh, i, c_ref: (i, hh)
    hmap = lambda hh, i, c_ref: (i, 0)

    def body(c_ref, w_ref, a_ref, b_ref, m_ref, v_ref, g_ref, d_ref, nm_ref, nv_ref):
        g = jnp.where(pl.program_id(0) == c_ref[0], a_ref[...], b_ref[...])
        g_ref[...] = g
        d_ref[...], nm_ref[...], nv_ref[...] = _adam_math(w_ref[...], g, m_ref[...], v_ref[...])

    wspec = pl.BlockSpec((tr, hc), wmap)
    hspec = pl.BlockSpec((tr, hc), hmap)
    return pl.pallas_call(
        body, name="adamw_" + name,
        grid_spec=pltpu.PrefetchScalarGridSpec(
            num_scalar_prefetch=1, grid=(2, ni),
            in_specs=[wspec, hspec, hspec, wspec, wspec], out_specs=[wspec] * 4),
        out_shape=[jax.ShapeDtypeStruct(w.shape, F32)] * 4,
        compiler_params=pltpu.CompilerParams(dimension_semantics=("parallel", "parallel"), vmem_limit_bytes=VMEM_LIMIT),
    )(cidx, w, mine, theirs, m, v)


SMALL = ("e_norm_w", "e_q_norm_w", "e_k_norm_w", "e_conv_w", "o_norm_w", "o_pool_w", "o_pool_scale",
         "o_dconv_w", "o_dconv_b", "o_ln_w", "o_ln_b")
SMALL_SHARDED = ("e_conv_w", "o_norm_w", "o_pool_scale", "o_dconv_w", "o_dconv_b", "o_ln_w", "o_ln_b")
WEIGHTS = ("e_norm_w", "e_w_in", "e_q_norm_w", "e_k_norm_w", "e_conv_w", "e_w_out", "o_norm_w", "o_w_in",
           "o_pool_w", "o_pool_scale", "o_dconv_w", "o_dconv_b", "o_ln_w", "o_ln_b", "o_w_out")


def _pack(arrs):
    flat = jnp.concatenate([a.reshape(-1) for a in arrs])
    rows = -(-flat.shape[0] // (LANES * SUBLANES)) * SUBLANES
    flat = jnp.pad(flat, (0, rows * LANES - flat.shape[0]))
    return flat.reshape(rows, LANES)


def _unpack(packed, shapes):
    flat = packed.reshape(-1)
    out, off = [], 0
    for s in shapes:
        n = int(np.prod(s))
        out.append(flat[off:off + n].reshape(s))
        off += n
    return out


def _gather_last(a, block, width):
    return lax.dynamic_slice_in_dim(a, block * width, width, axis=a.ndim - 1)


def kernel(x, positions, e_norm_w, e_w_in, e_q_norm_w, e_k_norm_w, e_conv_w, e_w_out, o_norm_w, o_w_in, o_pool_w, o_pool_scale, o_dconv_w, o_dconv_b, o_ln_w, o_ln_b, o_w_out, loss_target, m_e_norm_w, m_e_w_in, m_e_q_norm_w, m_e_k_norm_w, m_e_conv_w, m_e_w_out, m_o_norm_w, m_o_w_in, m_o_pool_w, m_o_pool_scale, m_o_dconv_w, m_o_dconv_b, m_o_ln_w, m_o_ln_b, m_o_w_out, v_e_norm_w, v_e_w_in, v_e_q_norm_w, v_e_k_norm_w, v_e_conv_w, v_e_w_out, v_o_norm_w, v_o_w_in, v_o_pool_w, v_o_pool_scale, v_o_dconv_w, v_o_dconv_b, v_o_ln_w, v_o_ln_b, v_o_w_out):
    given = dict(e_norm_w=e_norm_w, e_w_in=e_w_in, e_q_norm_w=e_q_norm_w, e_k_norm_w=e_k_norm_w, e_conv_w=e_conv_w,
                 e_w_out=e_w_out, o_norm_w=o_norm_w, o_w_in=o_w_in, o_pool_w=o_pool_w, o_pool_scale=o_pool_scale,
                 o_dconv_w=o_dconv_w, o_dconv_b=o_dconv_b, o_ln_w=o_ln_w, o_ln_b=o_ln_b, o_w_out=o_w_out)
    mom = dict(e_norm_w=m_e_norm_w, e_w_in=m_e_w_in, e_q_norm_w=m_e_q_norm_w, e_k_norm_w=m_e_k_norm_w,
               e_conv_w=m_e_conv_w, e_w_out=m_e_w_out, o_norm_w=m_o_norm_w, o_w_in=m_o_w_in, o_pool_w=m_o_pool_w,
               o_pool_scale=m_o_pool_scale, o_dconv_w=m_o_dconv_w, o_dconv_b=m_o_dconv_b, o_ln_w=m_o_ln_w,
               o_ln_b=m_o_ln_b, o_w_out=m_o_w_out)
    var = dict(e_norm_w=v_e_norm_w, e_w_in=v_e_w_in, e_q_norm_w=v_e_q_norm_w, e_k_norm_w=v_e_k_norm_w,
               e_conv_w=v_e_conv_w, e_w_out=v_e_w_out, o_norm_w=v_o_norm_w, o_w_in=v_o_w_in, o_pool_w=v_o_pool_w,
               o_pool_scale=v_o_pool_scale, o_dconv_w=v_o_dconv_w, o_dconv_b=v_o_dconv_b, o_ln_w=v_o_ln_w,
               o_ln_b=v_o_ln_b, o_w_out=v_o_w_out)
    S = x.shape[1]
    mx, my, mc = lax.axis_index("x"), lax.axis_index("y"), lax.axis_index("c")
    chip = 2 * mx + my
    cidx = jnp.reshape(mc, (1,)).astype(jnp.int32)
    bidx = jnp.reshape(chip, (1,)).astype(jnp.int32)

    shards = {n: given[n][0].astype(BF16) for n in BIG}
    (w_e_in,) = _gather_weights(("e_w_in",), [shards["e_w_in"]])
    shard_sizes = [int(np.prod(given[n].shape)) for n in SMALL_SHARDED]
    own = _pack([given[n] for n in SMALL_SHARDED])
    rows = own.shape[0]
    slots = jnp.zeros((N_CHIPS, rows, LANES), F32)
    own = jnp.where(mc == 0, own, 0.0)
    slots = lax.dynamic_update_slice(slots, own[None], (chip, 0, 0))
    gathered = _allreduce_small(slots.reshape(N_CHIPS * rows, LANES), "gather_small")
    gathered = gathered.reshape(N_CHIPS, rows * LANES)
    p = {}
    off = 0
    for n, size in zip(SMALL_SHARDED, shard_sizes):
        sh = given[n].shape[1:]
        parts = gathered[:, off:off + size].reshape((N_CHIPS,) + sh)
        fullp = jnp.moveaxis(parts, 0, -2).reshape(sh[:-1] + (N_CHIPS * sh[-1],))
        p[n] = fullp.reshape(-1, fullp.shape[-1])
        off += size
    p["e_norm_w"] = e_norm_w
    p["e_q_norm_w"] = e_q_norm_w
    p["e_k_norm_w"] = e_k_norm_w
    p["o_pool_w"] = o_pool_w[0]

    lsum, dx, g, parts = _local_step(x[0], positions.reshape(S, 1), loss_target[0], w_e_in, shards, p, cidx, bidx)
    loss = lax.psum(0.5 * jnp.sum(lsum) / float(D_MODEL), ("x", "y", "c"))

    tot = _unpack(_allreduce_small(_pack([g[n] for n in SMALL]), "allreduce_small"), [g[n].shape for n in SMALL])
    gsmall = dict(zip(SMALL, tot))
    grads = {}
    for n in SMALL:
        gv = gsmall[n]
        if n in SMALL_SHARDED:
            gv = _gather_last(gv, chip, gv.shape[-1] // N_CHIPS)
        grads[n] = gv.reshape(given[n].shape)

    theirs = _swap_to_sibling(BIG, [parts[n] for n in BIG], "swap_reduced", False)

    delta, new_m, new_v = {}, {}, {}
    for n, other in zip(BIG, theirs):
        sh = given[n].shape
        gs, d, nm, nv = _adamw_halves(cidx, given[n][0], parts[n], other, mom[n][0], var[n][0], n)
        grads[n], delta[n], new_m[n], new_v[n] = gs.reshape(sh), d.reshape(sh), nm.reshape(sh), nv.reshape(sh)
    shapes = [given[n].shape for n in SMALL]
    packed = [_pack([src[n] for n in SMALL]) for src in (given, grads, mom, var)]
    for dst, pk in zip((delta, new_m, new_v), _adamw(*packed, "small")):
        for n, a in zip(SMALL, _unpack(pk, shapes)):
            dst[n] = a
    return (loss, dx[None], *[grads[n] for n in WEIGHTS], *[delta[n] for n in WEIGHTS],
            *[new_m[n] for n in WEIGHTS], *[new_v[n] for n in WEIGHTS])
```

```python
import numpy as np
import jax
import jax.numpy as jnp
from jax import lax
from jax.experimental import pallas as pl
from jax.experimental.pallas import tpu as pltpu

F32 = jnp.float32
BF16 = jnp.bfloat16

D_MODEL = 1024
HEAD_DIM = 64
A_WIDTH = 512
A_HEADS = 8
A_GROUPS = ((128, 1), (512, 4), (2048, 16))
QBLK = 128
ROT_DIM = 16
ROPE_THETA = 500000.0
POOL_SIZES = (2, 4, 8, 16)
D_CONV = 31
SC_WIDTH = 3
EVEN_IN = 7168
ODD_IN = 2560
EPS = 1e-6
NEG = -1e30
ADAM_LR, ADAM_B1, ADAM_B2, ADAM_EPS, ADAM_WD, ADAM_STEP = 0.001, 0.9, 0.999, 1e-08, 0.01, 10

LANES = 128
SUBLANES = 8
HALO = 32
VMEM_LIMIT = 52 * 1024 * 1024
MESH = pl.DeviceIdType.MESH
ANY = pl.BlockSpec(memory_space=pl.ANY)

NT_DIMS = (((1,), (1,)), ((), ()))
TN_DIMS = (((0,), (0,)), ((), ()))


def _call(body, name, grid, in_specs, out_specs, out_shape, scratch=(), sem=None, aliases=None):
    return pl.pallas_call(
        body, name=name, grid=grid, in_specs=in_specs, out_specs=out_specs, out_shape=out_shape,
        scratch_shapes=list(scratch), input_output_aliases=aliases or {},
        compiler_params=pltpu.CompilerParams(dimension_semantics=sem, vmem_limit_bytes=VMEM_LIMIT))


def _sig(v):
    return jax.nn.sigmoid(v)


def _dsilu(v, s):
    return s * (1.0 + v * (1.0 - s))


def _out_projection(ut_ref, w_ref, lo, hi):
    acc = None
    for k, v in enumerate((lo, hi)):
        ut_ref[k * A_WIDTH:(k + 1) * A_WIDTH, :] = v.T.astype(BF16)
        part = jnp.dot(v.astype(BF16), w_ref[k * A_WIDTH:(k + 1) * A_WIDTH, :], preferred_element_type=F32)
        acc = part if acc is None else acc + part
    return acc


def _cs8(v):
    return v.reshape(v.shape[0] // SUBLANES, SUBLANES, v.shape[1]).sum(axis=0)


def _seg_mean():
    r = lax.broadcasted_iota(jnp.int32, (LANES, LANES), 0) // HEAD_DIM
    c = lax.broadcasted_iota(jnp.int32, (LANES, LANES), 1) // HEAD_DIM
    return jnp.where(r == c, 1.0 / HEAD_DIM, 0.0).astype(BF16)


def _segsum(v, ones):
    hi = v.astype(BF16)
    lo = (v - hi.astype(F32)).astype(BF16)
    return (jnp.dot(hi, ones, preferred_element_type=F32) + jnp.dot(lo, ones, preferred_element_type=F32))


def _head_rstd(v, seg_mean):
    return lax.rsqrt(jnp.dot((v * v).astype(BF16), seg_mean, preferred_element_type=F32) + EPS)


def _rope_tables(pos_ref, freq_ref):
    ang = pos_ref[...].astype(F32) * freq_ref[...]
    cosv, sinv = jnp.cos(ang), jnp.sin(ang)
    lm = lax.broadcasted_iota(jnp.int32, ang.shape, 1) % HEAD_DIM
    half = ROT_DIM // 2
    c = jnp.where(lm < ROT_DIM, cosv, 1.0)
    s1 = jnp.where((lm >= half) & (lm < ROT_DIM), sinv, 0.0)
    s2 = jnp.where(lm < half, -sinv, 0.0)
    return c, s1, s2


def _freq_table():
    half = ROT_DIM // 2
    inv = ROPE_THETA ** (-np.arange(half, dtype=np.float64) / half)
    lane = np.arange(LANES) % HEAD_DIM
    f = np.where(lane < ROT_DIM, inv[lane % half], 0.0)
    return jnp.asarray(f.reshape(1, LANES), F32)


def _load_once(hbm_ref, vmem_ref, sem):
    @pl.when(pl.program_id(0) == 0)
    def _():
        cp = pltpu.make_async_copy(hbm_ref, vmem_ref, sem)
        cp.start()
        cp.wait()


def _rms_rows(x_ref, nw_ref):
    xv = x_ref[...]
    ms = jnp.mean(xv * xv, axis=-1, keepdims=True)
    return xv * lax.rsqrt(ms + EPS) * nw_ref[...]


def _inproj(x, nw, w, tm, tn, name):
    S, N = x.shape[0], w.shape[1]

    def body(x_ref, nw_ref, w_hbm, o_ref, ht_ref, w_v, sem):
        _load_once(w_hbm, w_v, sem)
        h = _rms_rows(x_ref, nw_ref)
        ht_ref[...] = h.T.astype(BF16)
        hb = h.astype(BF16)
        for j in range(N // tn):
            o_ref[:, j * tn:(j + 1) * tn] = jnp.dot(hb, w_v[:, j * tn:(j + 1) * tn], preferred_element_type=F32)

    return _call(
        body, name, (S // tm,),
        [pl.BlockSpec((tm, D_MODEL), lambda i: (i, 0)),
         pl.BlockSpec((1, D_MODEL), lambda i: (0, 0)), ANY],
        [pl.BlockSpec((tm, N), lambda i: (i, 0)),
         pl.BlockSpec((D_MODEL, tm), lambda i: (0, i))],
        [jax.ShapeDtypeStruct((S, N), F32), jax.ShapeDtypeStruct((D_MODEL, S), BF16)],
        scratch=[pltpu.VMEM(w.shape, BF16), pltpu.SemaphoreType.DMA(())], sem=("arbitrary",))(x, nw, w)


def _inproj_gathering(x, nw, shard, bidx, first, late_names, late_shards, tm, name):
    S = x.shape[0]
    ni = S // tm
    K, cw = shard.shape
    nl = len(late_names)
    last = N_CHIPS - 1

    def body(b_ref, x_ref, nw_ref, s_hbm, *rest):
        ls_refs = rest[:nl]
        o_ref, ht_ref, f_hbm = rest[nl:nl + 3]
        lf_refs = rest[nl + 3:2 * nl + 3]
        hs, w_blk, lsem, send1, recv1, send2, recv2 = rest[2 * nl + 3:]
        j, i = pl.program_id(0), pl.program_id(1)
        g1 = _Gather((first,), (s_hbm,), (f_hbm,), send1, recv1)
        g2 = _Gather(late_names, ls_refs, lf_refs, send2, recv2)
        _, _, _, chips = _place()

        def load_block(src):
            cp = pltpu.make_async_copy(src, w_blk, lsem)
            cp.start()
            cp.wait()

        @pl.when((j == 0) & (i == 0))
        def _():
            g1.begin()
            g2.begin()
            load_block(s_hbm)

        for r, (cx, cy) in enumerate(chips):
            @pl.when((j == r + 1) & (i == 0))
            def _(r=r, cx=cx, cy=cy):
                g1.wait_relayed(r)
                load_block(_block_of(f_hbm, first, 2 * cx + cy))

        pl.when((j == 2) & (i == 0))(g2.relay)

        rows = pl.ds(pl.multiple_of(i * tm, tm), tm)

        @pl.when(j == 0)
        def _():
            h = _rms_rows(x_ref, nw_ref)
            hs[rows, :] = h.astype(BF16)
            ht_ref[...] = h.T.astype(BF16)

        o_ref[...] = jnp.dot(hs[rows, :], w_blk[...], preferred_element_type=F32)

        pl.when((j == 0) & (i == ni - 1))(g1.relay)

        @pl.when((j == last) & (i == ni - 1))
        def _():
            g1.end_rest()
            g2.end()

    def block_of_step(j, b_ref):
        return jnp.bitwise_xor(b_ref[0], jnp.bitwise_or(jnp.left_shift(jnp.bitwise_and(j, 1), 1), jnp.right_shift(j, 1)))

    outs = pl.pallas_call(
        body, name=name,
        grid_spec=pltpu.PrefetchScalarGridSpec(
            num_scalar_prefetch=1, grid=(N_CHIPS, ni),
            in_specs=[pl.BlockSpec((tm, D_MODEL), lambda j, i, b: (jnp.where(j == 0, i, 0), 0)),
                      pl.BlockSpec((1, D_MODEL), lambda j, i, b: (0, 0)), ANY] + [ANY] * nl,
            out_specs=[pl.BlockSpec((tm, cw), lambda j, i, b: (i, block_of_step(j, b))),
                       pl.BlockSpec((D_MODEL, tm), lambda j, i, b: (0, jnp.where(j == 0, i, ni - 1))),
                       ANY] + [ANY] * nl,
            scratch_shapes=[pltpu.VMEM((S, D_MODEL), BF16), pltpu.VMEM((K, cw), BF16), pltpu.SemaphoreType.DMA(())] +
            _gather_sems((first,)) + _gather_sems(late_names)),
        out_shape=[jax.ShapeDtypeStruct((S, cw * N_CHIPS), F32), jax.ShapeDtypeStruct((D_MODEL, S), BF16),
                   _full_shape(first, shard)] + [_full_shape(n, s) for n, s in zip(late_names, late_shards)],
        compiler_params=pltpu.CompilerParams(dimension_semantics=("arbitrary", "arbitrary"),
                                             vmem_limit_bytes=VMEM_LIMIT),
    )(bidx, x, nw, shard, *late_shards)
    return outs[0], outs[1], outs[2], list(outs[3:])


def _piece_blocks(pieces, tk, axis):
    starts, counts, s = [], [], 0
    for p in pieces:
        n = p.shape[axis] // tk
        starts.append(s)
        counts.append(n)
        s += n
    return starts, counts, s


def _mm_nt_rms(pieces, w, x, nw, dres, tm, name, scatter_names=(), scatter_halves=()):
    S = x.shape[0]
    npc = len(pieces)
    ni = S // tm
    ns = len(scatter_names)
    offs = np.cumsum([0] + [p.shape[1] for p in pieces]).tolist()

    def body(*refs):
        p_refs = refs[:npc]
        w_hbm, x_ref, nw_ref, dr_ref = refs[npc:npc + 4]
        h_refs = refs[npc + 4:npc + 4 + ns]
        dx_ref, dnw_ref = refs[npc + 4 + ns:npc + 6 + ns]
        r_refs = refs[npc + 6 + ns:npc + 6 + 2 * ns]
        w_v, sem, nacc = refs[npc + 6 + 2 * ns:npc + 9 + 2 * ns]
        i = pl.program_id(0)
        if ns:
            send, recv = refs[npc + 9 + 2 * ns:]

            @pl.when(i == 0)
            def _():
                for cp in _scatter_copies(scatter_names, h_refs, r_refs, send, recv):
                    cp.start()
        _load_once(w_hbm, w_v, sem)

        @pl.when(i == 0)
        def _():
            nacc[...] = jnp.zeros_like(nacc)

        dh = None
        for p in range(npc):
            part = lax.dot_general(p_refs[p][...].astype(BF16), w_v[:, offs[p]:offs[p + 1]], NT_DIMS,
                                   preferred_element_type=F32)
            dh = part if dh is None else dh + part
        xv = x_ref[...]
        rs = lax.rsqrt(jnp.mean(xv * xv, axis=-1, keepdims=True) + EPS)
        xh = xv * rs
        nacc[...] += _cs8(dh * xh)
        dxh = dh * nw_ref[...]
        dx_ref[...] = dr_ref[...] + rs * (dxh - xh * jnp.mean(dxh * xh, axis=-1, keepdims=True))

        @pl.when(i == ni - 1)
        def _():
            dnw_ref[...] = jnp.sum(nacc[...], axis=0, keepdims=True)
            if ns:
                for cp in _scatter_copies(scatter_names, h_refs, r_refs, send, recv):
                    cp.wait()

    row = pl.BlockSpec((tm, D_MODEL), lambda i: (i, 0))
    outs = _call(
        body, name, (ni,),
        [pl.BlockSpec((tm, p.shape[1]), lambda i: (i, 0)) for p in pieces] +
        [ANY, row, pl.BlockSpec((1, D_MODEL), lambda i: (0, 0)), row] + [ANY] * ns,
        [row, pl.BlockSpec((1, D_MODEL), lambda i: (0, 0))] + [ANY] * ns,
        [jax.ShapeDtypeStruct((S, D_MODEL), F32), jax.ShapeDtypeStruct((1, D_MODEL), F32)] +
        [jax.ShapeDtypeStruct((3,) + _shard_shape(h.shape, n), h.dtype) for n, h in zip(scatter_names, scatter_halves)],
        scratch=[pltpu.VMEM(w.shape, BF16), pltpu.SemaphoreType.DMA(()), pltpu.VMEM((SUBLANES, D_MODEL), F32)] +
        (_scatter_sems(scatter_names) if ns else []),
        sem=("arbitrary",))(*pieces, w, x, nw, dres, *scatter_halves)
    return outs[0], outs[1], list(outs[2:])


def _mm_wgrad(at, pieces, tn, name):
    M, S = at.shape
    starts, counts, nj = _piece_blocks(pieces, tn, 1)
    npc = len(pieces)

    def body(*refs):
        a_hbm = refs[0]
        p_refs = refs[1:1 + npc]
        o_ref, a_v, sem = refs[1 + npc:]
        j = pl.program_id(0)
        _load_once(a_hbm, a_v, sem)
        for p in range(npc):
            @pl.when((j >= starts[p]) & (j < starts[p] + counts[p]))
            def _(p=p):
                o_ref[...] = jnp.dot(a_v[...], p_refs[p][...].astype(BF16), preferred_element_type=F32)

    def pspec(p):
        return pl.BlockSpec((S, tn), lambda j: (0, jnp.clip(j - starts[p], 0, counts[p] - 1)))

    return _call(
        body, name, (nj,),
        [ANY] + [pspec(p) for p in range(npc)],
        pl.BlockSpec((M, tn), lambda j: (0, j)),
        jax.ShapeDtypeStruct((M, nj * tn), F32),
        scratch=[pltpu.VMEM(at.shape, BF16), pltpu.SemaphoreType.DMA(())], sem=("arbitrary",))(at, *pieces)


def _stream_spec(d, T):
    return pl.BlockSpec((d, T // d, A_WIDTH), lambda i: (0, i, 0))


def _stream_shape(d, S, dtype):
    return jax.ShapeDtypeStruct((d, S // d, A_WIDTH), dtype)


N_CHUNK = A_WIDTH // LANES


def _to_tokens(ref, scr, d, T):
    if d == 1:
        return ref[0].astype(F32)
    for r in range(d):
        for ch in range(N_CHUNK):
            scr.at[ch][pl.ds(r, T // d, stride=d), :] = ref[r, :, ch * LANES:(ch + 1) * LANES].astype(F32)
    return _get(scr)


def _from_tokens(out_ref, scr, d, T):
    for r in range(d):
        for ch in range(N_CHUNK):
            out_ref[r, :, ch * LANES:(ch + 1) * LANES] = scr.at[ch][pl.ds(r, T // d, stride=d), :].astype(out_ref.dtype)


def _put(scr, val):
    for ch in range(N_CHUNK):
        scr[ch] = val[:, ch * LANES:(ch + 1) * LANES]


def _get(scr):
    return jnp.concatenate([scr[ch] for ch in range(N_CHUNK)], axis=1)


def _chunked(T):
    return pltpu.VMEM((N_CHUNK, T, LANES), F32)


def _compact_spec(d, T):
    return pl.BlockSpec((d, T // d, LANES), lambda i: (0, i, 0))


def _compact_shape(d, S):
    return jax.ShapeDtypeStruct((d, S // d, LANES), F32)


def _compact_to_tokens(ref, scr, d, T):
    if d == 1:
        return ref[0]
    for r in range(d):
        scr[pl.ds(r, T // d, stride=d), :] = ref[r]
    return scr[...]


def _compact_from_tokens(out_ref, scr, val, d, T):
    if d == 1:
        out_ref[0] = val
        return
    scr[...] = val
    for r in range(d):
        out_ref[r] = scr[pl.ds(r, T // d, stride=d), :]


def _head_expander():
    r = lax.broadcasted_iota(jnp.int32, (LANES, A_WIDTH), 0)
    c = lax.broadcasted_iota(jnp.int32, (LANES, A_WIDTH), 1) // HEAD_DIM
    return (r == c).astype(BF16)


def _head_reducer():
    r = lax.broadcasted_iota(jnp.int32, (A_WIDTH, LANES), 0) // HEAD_DIM
    c = lax.broadcasted_iota(jnp.int32, (A_WIDTH, LANES), 1)
    return (r == c).astype(BF16)


def _qkv_prep(proj, pos, freq, wq, wk, T):
    S = proj.shape[0]
    qk_w = 3 * A_WIDTH

    def body(q_ref, k_ref, v_ref, pos_ref, f_ref, wq_ref, wk_ref, *rest):
        outs, scr = rest[:9], rest[9]
        seg_mean = _seg_mean()
        c, s1, s2 = _rope_tables(pos_ref, f_ref)
        for t, (src, w_ref) in enumerate(((q_ref, wq_ref), (k_ref, wk_ref), (v_ref, None))):
            for g in range(3):
                d = A_GROUPS[g][1]
                out = outs[3 * t + g]
                for ch in range(A_WIDTH // LANES):
                    cs = slice(ch * LANES, (ch + 1) * LANES)
                    v = src[:, g * A_WIDTH + ch * LANES: g * A_WIDTH + (ch + 1) * LANES]
                    if w_ref is not None:
                        y = v * _head_rstd(v, seg_mean) * w_ref[...]
                        v = y * c + pltpu.roll(y, 8, 1) * s1 + pltpu.roll(y, LANES - 8, 1) * s2
                    if d == 1:
                        out[0, :, cs] = v.astype(BF16)
                    else:
                        scr[ch] = v
                if d > 1:
                    _from_tokens(out, scr, d, T)

    ds_ = [A_GROUPS[g][1] for g in range(3)] * 3
    return _call(
        body, "qkv_prep", (S // T,),
        [pl.BlockSpec((T, qk_w), lambda i: (i, 0)), pl.BlockSpec((T, qk_w), lambda i: (i, 1)),
         pl.BlockSpec((T, qk_w), lambda i: (i, 2)),
         pl.BlockSpec((T, 1), lambda i: (i, 0)), pl.BlockSpec((1, LANES), lambda i: (0, 0)),
         pl.BlockSpec((1, LANES), lambda i: (0, 0)), pl.BlockSpec((1, LANES), lambda i: (0, 0))],
        [_stream_spec(d, T) for d in ds_],
        [_stream_shape(d, S, BF16) for d in ds_],
        scratch=[_chunked(T)], sem=("parallel",))(proj, proj, proj, pos, freq, wq, wk)


def _attn_mask(i):
    qi = lax.broadcasted_iota(jnp.int32, (QBLK, 2 * QBLK), 0) + QBLK
    kj = lax.broadcasted_iota(jnp.int32, (QBLK, 2 * QBLK), 1)
    dist = qi - kj
    return (dist >= 0) & (dist <= QBLK) & ((i > 0) | (kj >= QBLK))


ATT_BLK = (None, QBLK, A_WIDTH)
ATT_CBLK = (None, QBLK, LANES)


def _first_head_lanes():
    return lax.broadcasted_iota(jnp.int32, (1, LANES), 1) < HEAD_DIM


def _split_heads(v, first):
    zero = jnp.zeros_like(v)
    return jnp.where(first, v, zero), jnp.where(first, zero, v)


def _attn_fwd(q, k, v, g):
    d, n, _ = q.shape
    nb = n // QBLK

    def body(q_ref, kp_ref, kc_ref, vp_ref, vc_ref, o_ref, l_ref, s_scr, p_scr):
        i = pl.program_id(1)
        mask = _attn_mask(i)
        first = _first_head_lanes()
        for pr in range(A_HEADS // 2):
            ps = slice(pr * LANES, (pr + 1) * LANES)
            kc = jnp.concatenate([kp_ref[:, ps], kc_ref[:, ps]], axis=0)
            for e, qh in enumerate(_split_heads(q_ref[:, ps], first)):
                s_scr[2 * pr + e] = lax.dot_general(qh, kc, NT_DIMS, preferred_element_type=F32)
        lane = lax.broadcasted_iota(jnp.int32, (1, LANES), 1)
        lrow = jnp.zeros((QBLK, LANES), F32)
        for h in range(A_HEADS):
            s = jnp.where(mask, s_scr[h] * (HEAD_DIM ** -0.5), NEG)
            m = jnp.max(s, axis=-1, keepdims=True)
            p = jnp.exp(s - m)
            den = jnp.sum(p, axis=-1, keepdims=True)
            p_scr[h] = (p / den).astype(BF16)
            lrow = jnp.where(lane == h, m + jnp.log(den), lrow)
        l_ref[...] = lrow
        for pr in range(A_HEADS // 2):
            ps = slice(pr * LANES, (pr + 1) * LANES)
            va, vb = _split_heads(jnp.concatenate([vp_ref[:, ps], vc_ref[:, ps]], axis=0), first)
            o_ref[:, ps] = (jnp.dot(p_scr[2 * pr], va, preferred_element_type=F32) +
                            jnp.dot(p_scr[2 * pr + 1], vb, preferred_element_type=F32)).astype(BF16)

    prev = lambda r, i: (r, jnp.maximum(i - 1, 0), 0)
    cur = lambda r, i: (r, i, 0)
    return _call(
        body, "attn_fwd_g%d" % g, (d, nb),
        [pl.BlockSpec(ATT_BLK, cur), pl.BlockSpec(ATT_BLK, prev), pl.BlockSpec(ATT_BLK, cur),
         pl.BlockSpec(ATT_BLK, prev), pl.BlockSpec(ATT_BLK, cur)],
        [pl.BlockSpec(ATT_BLK, cur), pl.BlockSpec(ATT_CBLK, cur)],
        [jax.ShapeDtypeStruct((d, n, A_WIDTH), BF16), jax.ShapeDtypeStruct((d, n, LANES), F32)],
        scratch=[pltpu.VMEM((A_HEADS, QBLK, 2 * QBLK), F32), pltpu.VMEM((A_HEADS, QBLK, 2 * QBLK), BF16)],
        sem=("parallel", "parallel"))(q, k, k, v, v)


def _attn_bwd(q, k, v, do, lse, cg, g):
    d, n, _ = q.shape
    nb = n // QBLK
    scale = HEAD_DIM ** -0.5

    def body(q_ref, kp_ref, kc_ref, vp_ref, vc_ref, do_ref, l_ref, c_ref, dq_ref, dk_ref, dv_ref, ck, cv,
             s_scr, dp_scr, p_scr, ds_scr):
        i = pl.program_id(1)

        @pl.when(i == 0)
        def _():
            ck[...] = jnp.zeros_like(ck)
            cv[...] = jnp.zeros_like(cv)

        @pl.when(i < nb)
        def _():
            mask = _attn_mask(i)
            first = _first_head_lanes()
            for pr in range(A_HEADS // 2):
                ps = slice(pr * LANES, (pr + 1) * LANES)
                kc = jnp.concatenate([kp_ref[:, ps], kc_ref[:, ps]], axis=0)
                vc = jnp.concatenate([vp_ref[:, ps], vc_ref[:, ps]], axis=0)
                qs = _split_heads(q_ref[:, ps], first)
                dos = _split_heads(do_ref[:, ps], first)
                for e in range(2):
                    s_scr[2 * pr + e] = lax.dot_general(qs[e], kc, NT_DIMS, preferred_element_type=F32)
                    dp_scr[2 * pr + e] = lax.dot_general(dos[e], vc, NT_DIMS, preferred_element_type=F32)
            for h in range(A_HEADS):
                p = jnp.where(mask, jnp.exp(s_scr[h] * scale - l_ref[:, h:h + 1]), 0.0)
                p_scr[h] = p.astype(BF16)
                ds_scr[h] = (p * (dp_scr[h] + c_ref[:, h:h + 1]) * scale).astype(BF16)
            for pr in range(A_HEADS // 2):
                ps = slice(pr * LANES, (pr + 1) * LANES)
                ks = _split_heads(jnp.concatenate([kp_ref[:, ps], kc_ref[:, ps]], axis=0), first)
                qs = _split_heads(q_ref[:, ps], first)
                dos = _split_heads(do_ref[:, ps], first)
                dq = dkc = dvc = None
                for e in range(2):
                    ds = ds_scr[2 * pr + e]
                    a = jnp.dot(ds, ks[e], preferred_element_type=F32)
                    b = lax.dot_general(ds, qs[e], TN_DIMS, preferred_element_type=F32)
                    c = lax.dot_general(p_scr[2 * pr + e], dos[e], TN_DIMS, preferred_element_type=F32)
                    dq, dkc, dvc = (a, b, c) if e == 0 else (dq + a, dkc + b, dvc + c)
                dq_ref[:, ps] = dq.astype(BF16)
                dk_ref[:, ps] = (ck[:, ps] + dkc[:QBLK]).astype(BF16)
                dv_ref[:, ps] = (cv[:, ps] + dvc[:QBLK]).astype(BF16)
                ck[:, ps] = dkc[QBLK:]
                cv[:, ps] = dvc[QBLK:]

        @pl.when(i == nb)
        def _():
            dk_ref[...] = ck[...].astype(BF16)
            dv_ref[...] = cv[...].astype(BF16)

    qi = lambda i: jnp.minimum(i, nb - 1)
    cur = lambda r, i: (r, qi(i), 0)
    prev = lambda r, i: (r, jnp.maximum(qi(i) - 1, 0), 0)
    late = lambda r, i: (r, jnp.maximum(i - 1, 0), 0)
    return _call(
        body, "attn_bwd_g%d" % g, (d, nb + 1),
        [pl.BlockSpec(ATT_BLK, cur), pl.BlockSpec(ATT_BLK, prev), pl.BlockSpec(ATT_BLK, cur),
         pl.BlockSpec(ATT_BLK, prev), pl.BlockSpec(ATT_BLK, cur),
         pl.BlockSpec(ATT_BLK, cur), pl.BlockSpec(ATT_CBLK, cur), pl.BlockSpec(ATT_CBLK, cur)],
        [pl.BlockSpec(ATT_BLK, cur), pl.BlockSpec(ATT_BLK, late), pl.BlockSpec(ATT_BLK, late)],
        [jax.ShapeDtypeStruct((d, n, A_WIDTH), BF16)] * 3,
        scratch=[pltpu.VMEM((QBLK, A_WIDTH), F32), pltpu.VMEM((QBLK, A_WIDTH), F32),
                 pltpu.VMEM((A_HEADS, QBLK, 2 * QBLK), F32), pltpu.VMEM((A_HEADS, QBLK, 2 * QBLK), F32),
                 pltpu.VMEM((A_HEADS, QBLK, 2 * QBLK), BF16), pltpu.VMEM((A_HEADS, QBLK, 2 * QBLK), BF16)],
        sem=("parallel", "arbitrary"))(q, k, k, v, v, do, lse, cg)


def _merge_weights(l0, l1, l2):
    mx = jnp.maximum(jnp.maximum(l0, l1), l2)
    e0, e1, e2 = jnp.exp(l0 - mx), jnp.exp(l1 - mx), jnp.exp(l2 - mx)
    den = e0 + e1 + e2
    return e0 / den, e1 / den, e2 / den


def _even_specs(T, S):
    t8 = T // SUBLANES
    last8 = S // SUBLANES - 1
    col = lambda c: pl.BlockSpec((T, A_WIDTH), lambda i: (i, c))
    prev8 = lambda c: pl.BlockSpec((SUBLANES, A_WIDTH), lambda i: (jnp.maximum(i * t8 - 1, 0), c))
    next8 = lambda c: pl.BlockSpec((SUBLANES, A_WIDTH), lambda i: (jnp.minimum((i + 1) * t8, last8), c))
    return col, prev8, next8


GROUP_D = tuple(d for _, d in A_GROUPS)


def _even_mixer_fwd(x, proj, os_, ls_, conv_w, w_out, T):
    S = proj.shape[0]
    col, prev8, _ = _even_specs(T, S)
    H = SUBLANES

    def body(x_ref, w_ref, bg_r, cg_r, hb_r, zl_r, zh_r, cgp_r, hbp_r, o0, o1, o2, l0, l1, l2, cw_r,
             x1_ref, ut_ref, ext, cscr, *scr):
        i = pl.program_id(0)
        ls = [_compact_to_tokens(r, cscr, GROUP_D[g], T) for g, r in enumerate((l0, l1, l2))]
        expand = _head_expander()
        ws = [_segsum(w, expand) for w in _merge_weights(*ls)]
        oa = ws[0] * _to_tokens(o0, scr[0], GROUP_D[0], T)
        oa = oa + ws[1] * _to_tokens(o1, scr[1], GROUP_D[1], T)
        oa = oa + ws[2] * _to_tokens(o2, scr[2], GROUP_D[2], T)
        ext[0:H, :] = jnp.where(i == 0, 0.0, cgp_r[...] * hbp_r[...])
        ext[H:H + T, :] = cg_r[...] * hb_r[...]
        conv = cw_r[0:1, :] * ext[H - 2:H - 2 + T, :]
        for kk in range(1, SC_WIDTH):
            conv = conv + cw_r[kk:kk + 1, :] * ext[H - 2 + kk:H - 2 + kk + T, :]
        zl, zh = zl_r[...], zh_r[...]
        x1_ref[...] = x_ref[...] + _out_projection(ut_ref, w_ref, oa * (zl * _sig(zl)),
                                                   bg_r[...] * conv * (zh * _sig(zh)))

    streams = [_stream_spec(d, T) for d in GROUP_D]
    compacts = [_compact_spec(d, T) for d in GROUP_D]
    row = pl.BlockSpec((T, D_MODEL), lambda i: (i, 0))
    return _call(
        body, "even_mixer_fwd", (S // T,),
        [row, pl.BlockSpec((D_MODEL, D_MODEL), lambda i: (0, 0)),
         col(9), col(10), col(11), col(12), col(13), prev8(10), prev8(11)] + streams + compacts +
        [pl.BlockSpec((SC_WIDTH, A_WIDTH), lambda i: (0, 0))],
        [row, pl.BlockSpec((D_MODEL, T), lambda i: (0, i))],
        [jax.ShapeDtypeStruct((S, D_MODEL), F32), jax.ShapeDtypeStruct((D_MODEL, S), BF16)],
        scratch=[pltpu.VMEM((T + H, A_WIDTH), F32), pltpu.VMEM((T, LANES), F32)] + [_chunked(T)] * 3,
        sem=("parallel",))(
            x, w_out, proj, proj, proj, proj, proj, proj, proj, *os_, *ls_, conv_w)


def _even_mixer_bwd(dy, w_out, proj, os_, ls_, conv_w, T):
    S = proj.shape[0]
    nt = S // T
    col, prev8, next8 = _even_specs(T, S)
    H = SUBLANES
    t8 = T // SUBLANES
    last8 = S // SUBLANES - 1

    def body(dy_r, dyn_r, w_ref, bg_r, cg_r, hb_r, zl_r, zh_r, cgp_r, hbp_r, zhn_r, bgn_r,
             o0, o1, o2, l0, l1, l2, cw_r,
             do0, do1, do2, c0, c1, c2, dr_ref, dcw_ref, ext_t, ext_d, acc, cscr, s_a, s_b, s_c):
        i = pl.program_id(0)

        @pl.when(i == 0)
        def _():
            acc[...] = jnp.zeros_like(acc)

        zl, zh = zl_r[...], zh_r[...]
        sl, sh = _sig(zl), _sig(zh)
        du = lax.dot_general(dy_r[...].astype(BF16), w_ref[...], NT_DIMS, preferred_element_type=F32)
        dul, duh = du[:, 0:A_WIDTH], du[:, A_WIDTH:]
        dun = lax.dot_general(dyn_r[...].astype(BF16), w_ref[A_WIDTH:, :], NT_DIMS, preferred_element_type=F32)
        scr = (s_a, s_b, s_c)
        ls = [_compact_to_tokens(r, cscr, GROUP_D[g], T) for g, r in enumerate((l0, l1, l2))]
        wcs = _merge_weights(*ls)
        expand = _head_expander()
        ws = [_segsum(w, expand) for w in wcs]
        oa = ws[0] * _to_tokens(o0, scr[0], GROUP_D[0], T)
        oa = oa + ws[1] * _to_tokens(o1, scr[1], GROUP_D[1], T)
        oa = oa + ws[2] * _to_tokens(o2, scr[2], GROUP_D[2], T)
        doa = dul * (zl * sl)
        rsum = _segsum(doa * oa, _head_reducer())
        for g, (do_ref, c_ref) in enumerate(((do0, c0), (do1, c1), (do2, c2))):
            d = GROUP_D[g]
            _compact_from_tokens(c_ref, cscr, -wcs[g] * rsum, d, T)
            if d == 1:
                do_ref[0] = (ws[g] * doa).astype(BF16)
            else:
                _put(s_c, ws[g] * doa)
                _from_tokens(do_ref, s_c, d, T)
        cgv, hbv, bgv = cg_r[...], hb_r[...], bg_r[...]
        ext_t[0:H, :] = jnp.where(i == 0, 0.0, cgp_r[...] * hbp_r[...])
        ext_t[H:H + T, :] = cgv * hbv
        conv = cw_r[0:1, :] * ext_t[H - 2:H - 2 + T, :]
        for kk in range(1, SC_WIDTH):
            conv = conv + cw_r[kk:kk + 1, :] * ext_t[H - 2 + kk:H - 2 + kk + T, :]
        dyb = duh * (zh * sh)
        dconv = dyb * bgv
        zn = zhn_r[...]
        ext_d[0:T, :] = dconv
        ext_d[T:T + H, :] = jnp.where(i == nt - 1, 0.0, dun * (zn * _sig(zn)) * bgn_r[...])
        dt = cw_r[0:1, :] * ext_d[2:2 + T, :]
        for kk in range(1, SC_WIDTH):
            dt = dt + cw_r[kk:kk + 1, :] * ext_d[2 - kk:2 - kk + T, :]
        for kk in range(SC_WIDTH):
            acc[kk * SUBLANES:(kk + 1) * SUBLANES, :] += _cs8(dconv * ext_t[H - 2 + kk:H - 2 + kk + T, :])
        dr_ref[:, 0:A_WIDTH] = (dyb * conv).astype(BF16)
        dr_ref[:, A_WIDTH:2 * A_WIDTH] = (dt * hbv).astype(BF16)
        dr_ref[:, 2 * A_WIDTH:3 * A_WIDTH] = (dt * cgv).astype(BF16)
        dr_ref[:, 3 * A_WIDTH:4 * A_WIDTH] = (dul * oa * _dsilu(zl, sl)).astype(BF16)
        dr_ref[:, 4 * A_WIDTH:5 * A_WIDTH] = (duh * (bgv * conv) * _dsilu(zh, sh)).astype(BF16)

        @pl.when(i == nt - 1)
        def _():
            for kk in range(SC_WIDTH):
                dcw_ref[kk:kk + 1, :] = jnp.sum(acc[kk * SUBLANES:(kk + 1) * SUBLANES, :], axis=0, keepdims=True)

    streams = [_stream_spec(d, T) for d in GROUP_D]
    dynext = pl.BlockSpec((SUBLANES, D_MODEL), lambda i: (jnp.minimum((i + 1) * t8, last8), 0))
    compacts = [_compact_spec(d, T) for d in GROUP_D]
    outs = _call(
        body, "even_mixer_bwd", (nt,),
        [pl.BlockSpec((T, D_MODEL), lambda i: (i, 0)), dynext, pl.BlockSpec((D_MODEL, D_MODEL), lambda i: (0, 0)),
         col(9), col(10), col(11), col(12), col(13), prev8(10), prev8(11), next8(13), next8(9)] +
        streams + compacts + [pl.BlockSpec((SC_WIDTH, A_WIDTH), lambda i: (0, 0))],
        streams + compacts + [pl.BlockSpec((T, 5 * A_WIDTH), lambda i: (i, 0)),
                              pl.BlockSpec((SC_WIDTH, A_WIDTH), lambda i: (0, 0))],
        [_stream_shape(d, S, BF16) for d in GROUP_D] + [_compact_shape(d, S) for d in GROUP_D] +
        [jax.ShapeDtypeStruct((S, 5 * A_WIDTH), BF16), jax.ShapeDtypeStruct((SC_WIDTH, A_WIDTH), F32)],
        scratch=[pltpu.VMEM((T + H, A_WIDTH), F32), pltpu.VMEM((T + H, A_WIDTH), F32),
                 pltpu.VMEM((SC_WIDTH * SUBLANES, A_WIDTH), F32), pltpu.VMEM((T, LANES), F32)] +
                [_chunked(T)] * 3,
        sem=("arbitrary",))(dy, dy, w_out, proj, proj, proj, proj, proj, proj, proj, proj, proj, *os_, *ls_, conv_w)
    return outs[0:3], outs[3:6], outs[6], outs[7]


def _qk_bwd(proj, dqs, dks, dvs, pos, freq, wq, wk, T):
    S = proj.shape[0]
    nt = S // T
    qk_w = 3 * A_WIDTH

    def body(q_ref, k_ref, dq0, dq1, dq2, dk0, dk1, dk2, dv0, dv1, dv2, pos_ref, f_ref, wq_ref, wk_ref,
             o_ref, dw_ref, acc, scr):
        i = pl.program_id(0)

        @pl.when(i == 0)
        def _():
            acc[...] = jnp.zeros_like(acc)
            dw_ref[...] = jnp.zeros_like(dw_ref)

        seg_mean = _seg_mean()
        c, s1, s2 = _rope_tables(pos_ref, f_ref)
        for t, (src, w_ref, ds) in enumerate(((q_ref, wq_ref, (dq0, dq1, dq2)), (k_ref, wk_ref, (dk0, dk1, dk2)))):
            wv = w_ref[...]
            for g in range(3):
                d = GROUP_D[g]
                if d > 1:
                    _to_tokens(ds[g], scr, d, T)
                for ch in range(A_WIDTH // LANES):
                    cs = slice(g * A_WIDTH + ch * LANES, g * A_WIDTH + (ch + 1) * LANES)
                    lc = slice(ch * LANES, (ch + 1) * LANES)
                    v = src[:, cs]
                    dout = ds[g][0, :, lc].astype(F32) if d == 1 else scr[ch]
                    rs = _head_rstd(v, seg_mean)
                    xh = v * rs
                    dy = dout * c + pltpu.roll(dout * s1, LANES - 8, 1) + pltpu.roll(dout * s2, 8, 1)
                    acc[t * SUBLANES:(t + 1) * SUBLANES, :] += _cs8(dy * xh)
                    dxh = dy * wv
                    mean = _segsum(dxh * xh, seg_mean)
                    o_ref[:, t * qk_w + g * A_WIDTH + ch * LANES: t * qk_w + g * A_WIDTH + (ch + 1) * LANES] = (
                        rs * (dxh - xh * mean)).astype(BF16)
        for g, dv in enumerate((dv0, dv1, dv2)):
            d = GROUP_D[g]
            base = 2 * qk_w + g * A_WIDTH
            o_ref[:, base:base + A_WIDTH] = _to_tokens(dv, scr, d, T).astype(BF16)

        @pl.when(i == nt - 1)
        def _():
            for t in range(2):
                srow = jnp.sum(acc[t * SUBLANES:(t + 1) * SUBLANES, :], axis=0, keepdims=True)
                dw_ref[t:t + 1, :] = srow + pltpu.roll(srow, HEAD_DIM, 1)

    streams = [_stream_spec(d, T) for d in GROUP_D]
    return _call(
        body, "qk_bwd", (nt,),
        [pl.BlockSpec((T, qk_w), lambda i: (i, 0)), pl.BlockSpec((T, qk_w), lambda i: (i, 1))] + streams * 3 +
        [pl.BlockSpec((T, 1), lambda i: (i, 0)), pl.BlockSpec((1, LANES), lambda i: (0, 0)),
         pl.BlockSpec((1, LANES), lambda i: (0, 0)), pl.BlockSpec((1, LANES), lambda i: (0, 0))],
        [pl.BlockSpec((T, 3 * qk_w), lambda i: (i, 0)), pl.BlockSpec((SUBLANES, LANES), lambda i: (0, 0))],
        [jax.ShapeDtypeStruct((S, 3 * qk_w), BF16), jax.ShapeDtypeStruct((SUBLANES, LANES), F32)],
        scratch=[pltpu.VMEM((2 * SUBLANES, LANES), F32), _chunked(T)], sem=("arbitrary",))(
            proj, proj, *dqs, *dks, *dvs, pos, freq, wq, wk)


N_SMALL_ODD = 40
SHIFT_ROWS_LESS = SUBLANES


def _fill_shifted(ext_ref, sh_ref):
    rows = ext_ref.shape[0] - SHIFT_ROWS_LESS
    for b in range(1, SUBLANES):
        sh_ref[b - 1] = ext_ref[b:b + rows, :]


def _window(ext_ref, sh_ref, off, T):
    a, b = divmod(off, SUBLANES)
    if b == 0:
        return ext_ref[off:off + T, :]
    return sh_ref[b - 1, a * SUBLANES:a * SUBLANES + T, :]


def _odd_pool_tile(i, uc_r, ucp_r, pw_r, ext_u, pooled_s, pm_s, T):
    H = HALO
    uc = uc_r[...]
    ext_u[0:H, :] = jnp.where(i == 0, 0.0, ucp_r[...])
    ext_u[H:H + T, :] = uc
    row = i * T + lax.broadcasted_iota(jnp.int32, (T, 1), 0)
    for g, p in enumerate(POOL_SIZES):
        cs = slice(g * LANES, (g + 1) * LANES)
        win = ext_u[H:H + T, cs]
        for j in range(1, p):
            win = win + ext_u[H - j:H - j + T, cs]
        cnt = jnp.minimum(row + 1, p).astype(F32)
        pooled = win / cnt - uc[:, cs]
        pooled_s[:, cs] = pooled
        pm_s[:, cs] = jnp.dot(pooled.astype(BF16), pw_r[g].astype(BF16), preferred_element_type=F32)
    return row


def _odd_glu_tile(i, da_r, dg_r, dap_r, dgp_r, ext_g, sh_g, T):
    H = HALO
    ext_g[0:H, :] = jnp.where(i == 0, 0.0, dap_r[...] * _sig(dgp_r[...]))
    ext_g[H:H + T, :] = da_r[...] * _sig(dg_r[...])
    _fill_shifted(ext_g, sh_g)


def _odd_specs(T, S, order):
    tb = T // HALO
    col = lambda c: pl.BlockSpec((T, A_WIDTH), lambda s: (order(s), c))
    prev = lambda c: pl.BlockSpec((HALO, A_WIDTH), lambda s: (jnp.maximum(order(s) * tb - 1, 0), c))
    const2 = lambda shape: pl.BlockSpec(shape, lambda s: (0, 0))
    weights = [pl.BlockSpec((4, LANES, LANES), lambda s: (0, 0, 0)), const2((1, A_WIDTH)),
               const2((D_CONV, A_WIDTH)), const2((1, A_WIDTH)), const2((1, A_WIDTH)), const2((1, A_WIDTH))]
    return col, prev, weights


def _odd_mixer_fwd(x, tgt, proj, pool_w, scale, dconv_w, dconv_b, ln_w, ln_b, w_out, T):
    S = proj.shape[0]
    nt = S // T
    col, prev, wspecs = _odd_specs(T, S, lambda s: s)
    H = HALO

    def body(x_ref, t_ref, w_ref, uc_r, da_r, dg_r, zl_r, zh_r, ucp_r, dap_r, dgp_r, pw_r, sc_r, dw_r, db_r,
             lw_r, lb_r, dy_ref, l_ref, ut_ref, cv_ref, ext_u, ext_g, sh_g, pooled_s, pm_s, lacc):
        i = pl.program_id(0)

        @pl.when(i == 0)
        def _():
            lacc[...] = jnp.zeros_like(lacc)

        _odd_pool_tile(i, uc_r, ucp_r, pw_r, ext_u, pooled_s, pm_s, T)
        _odd_glu_tile(i, da_r, dg_r, dap_r, dgp_r, ext_g, sh_g, T)
        base = H - (D_CONV - 1)
        conv = db_r[...] + dw_r[0:1, :] * _window(ext_g, sh_g, base, T)
        for kk in range(1, D_CONV):
            conv = conv + dw_r[kk:kk + 1, :] * _window(ext_g, sh_g, base + kk, T)
        cv_ref[...] = conv
        mu = jnp.mean(conv, axis=-1, keepdims=True)
        xc = conv - mu
        yh = xc * lax.rsqrt(jnp.mean(xc * xc, axis=-1, keepdims=True) + EPS)
        ln = yh * lw_r[...] + lb_r[...]
        zl, zh = zl_r[...], zh_r[...]
        y = x_ref[...] + _out_projection(ut_ref, w_ref, pm_s[...] * sc_r[...] * (zl * _sig(zl)),
                                         ln * _sig(ln) * (zh * _sig(zh)))
        diff = y - t_ref[...]
        dy_ref[...] = diff / float(D_MODEL)
        lacc[...] += _cs8(diff * diff)

        @pl.when(i == nt - 1)
        def _():
            l_ref[...] = jnp.sum(lacc[...], axis=0, keepdims=True)

    row = pl.BlockSpec((T, D_MODEL), lambda i: (i, 0))
    return _call(
        body, "odd_mixer_fwd", (nt,),
        [row, row, pl.BlockSpec((D_MODEL, D_MODEL), lambda i: (0, 0)),
         col(0), col(1), col(2), col(3), col(4), prev(0), prev(1), prev(2)] + wspecs,
        [row, pl.BlockSpec((1, D_MODEL), lambda i: (0, 0)), pl.BlockSpec((D_MODEL, T), lambda i: (0, i)),
         pl.BlockSpec((T, A_WIDTH), lambda i: (i, 0))],
        [jax.ShapeDtypeStruct((S, D_MODEL), F32), jax.ShapeDtypeStruct((1, D_MODEL), F32),
         jax.ShapeDtypeStruct((D_MODEL, S), BF16), jax.ShapeDtypeStruct((S, A_WIDTH), F32)],
        scratch=[pltpu.VMEM((T + H, A_WIDTH), F32), pltpu.VMEM((T + H, A_WIDTH), F32),
                 pltpu.VMEM((SUBLANES - 1, T + H - SHIFT_ROWS_LESS, A_WIDTH), F32),
                 pltpu.VMEM((T, A_WIDTH), F32), pltpu.VMEM((T, A_WIDTH), F32),
                 pltpu.VMEM((SUBLANES, D_MODEL), F32)],
        sem=("arbitrary",))(x, tgt, w_out, proj, proj, proj, proj, proj, proj, proj, proj,
                            pool_w, scale, dconv_w, dconv_b, ln_w, ln_b)


def _odd_mixer_bwd(dy, w_out, proj, conv, pool_w, scale, dconv_w, dconv_b, ln_w, ln_b, T):
    S = proj.shape[0]
    nt = S // T
    order = lambda s: nt - 1 - s
    col, prev, wspecs = _odd_specs(T, S, order)
    H = HALO

    def body(dy_r, w_ref, cv_r, uc_r, da_r, dg_r, zl_r, zh_r, ucp_r, dap_r, dgp_r, pw_r, sc_r, dw_r, db_r, lw_r, lb_r,
             dp_ref, dpw_ref, sm_ref, ext_u, ext_g, sh_g, pooled_s, pm_s, dpl_s, ext_p, ext_c, sh_c, acc):
        step = pl.program_id(0)
        i = nt - 1 - step

        @pl.when(step == 0)
        def _():
            ext_p[T:T + H, :] = jnp.zeros((H, A_WIDTH), F32)
            ext_c[T:T + H, :] = jnp.zeros((H, A_WIDTH), F32)
            acc[...] = jnp.zeros_like(acc)
            dpw_ref[...] = jnp.zeros_like(dpw_ref)

        def accum(r, v):
            acc[r * SUBLANES:(r + 1) * SUBLANES, :] += _cs8(v)

        row = _odd_pool_tile(i, uc_r, ucp_r, pw_r, ext_u, pooled_s, pm_s, T)
        _odd_glu_tile(i, da_r, dg_r, dap_r, dgp_r, ext_g, sh_g, T)
        conv = cv_r[...]
        mu = jnp.mean(conv, axis=-1, keepdims=True)
        xc = conv - mu
        rstd = lax.rsqrt(jnp.mean(xc * xc, axis=-1, keepdims=True) + EPS)
        yh = xc * rstd
        ln = yh * lw_r[...] + lb_r[...]
        sln = _sig(ln)
        zl, zh = zl_r[...], zh_r[...]
        sl, sh = _sig(zl), _sig(zh)
        du = lax.dot_general(dy_r[...].astype(BF16), w_ref[...], NT_DIMS, preferred_element_type=F32)
        dul, duh = du[:, 0:A_WIDTH], du[:, A_WIDTH:]
        pm = pm_s[...]
        scv = sc_r[...]
        dyc = dul * (zl * sl)
        accum(34, dyc * pm)
        dpm = dyc * scv
        for g in range(len(POOL_SIZES)):
            cs = slice(g * LANES, (g + 1) * LANES)
            dpm_g = dpm[:, cs].astype(BF16)
            dpw_ref[g] += lax.dot_general(pooled_s[:, cs].astype(BF16), dpm_g, TN_DIMS, preferred_element_type=F32)
            dpl_s[:, cs] = lax.dot_general(dpm_g, pw_r[g].astype(BF16), NT_DIMS, preferred_element_type=F32)
        lane_p = lax.broadcasted_iota(jnp.int32, (1, A_WIDTH), 1) // LANES
        pvec = jnp.left_shift(2, lane_p)
        cnt = jnp.minimum(row + 1, pvec).astype(F32)
        dpl = dpl_s[...]
        ext_p[0:T, :] = dpl / cnt
        for g, p in enumerate(POOL_SIZES):
            cs = slice(g * LANES, (g + 1) * LANES)
            win = ext_p[0:T, cs]
            for j in range(1, p):
                win = win + ext_p[j:j + T, cs]
            dp_ref[:, cs] = (win - dpl[:, cs]).astype(BF16)
        ext_p[T:T + H, :] = ext_p[0:H, :]
        dln = duh * (zh * sh) * _dsilu(ln, sln)
        accum(32, dln * yh)
        accum(33, dln)
        dyh = dln * lw_r[...]
        dc = rstd * (dyh - jnp.mean(dyh, axis=-1, keepdims=True) - yh * jnp.mean(dyh * yh, axis=-1, keepdims=True))
        accum(31, dc)
        ext_c[0:T, :] = dc
        _fill_shifted(ext_c, sh_c)
        base = H - (D_CONV - 1)
        dgl = dw_r[0:1, :] * _window(ext_c, sh_c, D_CONV - 1, T)
        accum(0, dc * _window(ext_g, sh_g, base, T))
        for kk in range(1, D_CONV):
            dgl = dgl + dw_r[kk:kk + 1, :] * _window(ext_c, sh_c, D_CONV - 1 - kk, T)
            accum(kk, dc * _window(ext_g, sh_g, base + kk, T))
        ext_c[T:T + H, :] = ext_c[0:H, :]
        dav, dgv = da_r[...], dg_r[...]
        sg = _sig(dgv)
        dp_ref[:, A_WIDTH:2 * A_WIDTH] = (dgl * sg).astype(BF16)
        dp_ref[:, 2 * A_WIDTH:3 * A_WIDTH] = (dgl * dav * sg * (1.0 - sg)).astype(BF16)
        dp_ref[:, 3 * A_WIDTH:4 * A_WIDTH] = (dul * (pm * scv) * _dsilu(zl, sl)).astype(BF16)
        dp_ref[:, 4 * A_WIDTH:5 * A_WIDTH] = (duh * (ln * sln) * _dsilu(zh, sh)).astype(BF16)

        @pl.when(step == nt - 1)
        def _():
            for r in range(N_SMALL_ODD):
                sm_ref[r:r + 1, :] = jnp.sum(acc[r * SUBLANES:(r + 1) * SUBLANES, :], axis=0, keepdims=True)

    ext = pltpu.VMEM((T + H, A_WIDTH), F32)
    shifted = pltpu.VMEM((SUBLANES - 1, T + H - SHIFT_ROWS_LESS, A_WIDTH), F32)
    tile = pltpu.VMEM((T, A_WIDTH), F32)
    return _call(
        body, "odd_mixer_bwd", (nt,),
        [pl.BlockSpec((T, D_MODEL), lambda s: (order(s), 0)), pl.BlockSpec((D_MODEL, D_MODEL), lambda s: (0, 0)),
         pl.BlockSpec((T, A_WIDTH), lambda s: (order(s), 0)),
         col(0), col(1), col(2), col(3), col(4), prev(0), prev(1), prev(2)] + wspecs,
        [pl.BlockSpec((T, ODD_IN), lambda s: (order(s), 0)),
         pl.BlockSpec((4, LANES, LANES), lambda s: (0, 0, 0)),
         pl.BlockSpec((N_SMALL_ODD, A_WIDTH), lambda s: (0, 0))],
        [jax.ShapeDtypeStruct((S, ODD_IN), BF16), jax.ShapeDtypeStruct((4, LANES, LANES), F32),
         jax.ShapeDtypeStruct((N_SMALL_ODD, A_WIDTH), F32)],
        scratch=[ext, ext, shifted, tile, tile, tile, ext, ext, shifted,
                 pltpu.VMEM((N_SMALL_ODD * SUBLANES, A_WIDTH), F32)],
        sem=("arbitrary",))(dy, w_out, conv, proj, proj, proj, proj, proj, proj, proj, proj,
                            pool_w, scale, dconv_w, dconv_b, ln_w, ln_b)


TILE_SEQ = 256
TILE_WG = 256


LATE_WEIGHTS = ("e_w_out", "o_w_in", "o_w_out")
ODD_MATS = ("o_w_in", "o_w_out")
EVEN_MATS = ("e_w_in", "e_w_out")


def _reduce_start(names, grads, cidx):
    recv = _swap_to_sibling(names, [grads[n] for n in names], "swap_halves_" + names[0][0], True)
    both = [_add_half(cidx, grads[n], r, n) for n, r in zip(names, recv)]
    return [h for h, _ in both], [hb for _, hb in both]


def _local_step(x, pos, tgt, shards, p, cidx, bidx):
    T = TILE_SEQ
    freq = _freq_table()
    wq = jnp.tile(p["e_q_norm_w"], (1, LANES // HEAD_DIM))
    wk = jnp.tile(p["e_k_norm_w"], (1, LANES // HEAD_DIM))

    proj_e, ht_e, w_e_in, late = _inproj_gathering(x, p["e_norm_w"], shards["e_w_in"], bidx, "e_w_in", LATE_WEIGHTS,
                                                   [shards[n] for n in LATE_WEIGHTS], TILE_SEQ, "inproj_even")
    wb = dict(zip(LATE_WEIGHTS, late), e_w_in=w_e_in)
    qkv = _qkv_prep(proj_e, pos, freq, wq, wk, T)
    qs, ks, vs = qkv[0:3], qkv[3:6], qkv[6:9]
    os_, ls_ = [], []
    for g in range(3):
        o, l = _attn_fwd(qs[g], ks[g], vs[g], g)
        os_.append(o)
        ls_.append(l)
    x1, ut_e = _even_mixer_fwd(x, proj_e, os_, ls_, p["e_conv_w"], wb["e_w_out"], T)
    proj_o, ht_o = _inproj(x1, p["o_norm_w"], wb["o_w_in"], TILE_SEQ, 1280, "inproj_odd")
    odd_w = (p["o_pool_w"], p["o_pool_scale"], p["o_dconv_w"], p["o_dconv_b"], p["o_ln_w"], p["o_ln_b"])
    dy, lsum, ut_o, conv_o = _odd_mixer_fwd(x1, tgt, proj_o, *odd_w, wb["o_w_out"], T)

    g = {}
    g["o_w_out"] = _mm_wgrad(ut_o, [dy], TILE_WG, "wgrad_o_out")
    dproj_o, g["o_pool_w"], small_o = _odd_mixer_bwd(dy, wb["o_w_out"], proj_o, conv_o, *odd_w, T)
    g["o_w_in"] = _mm_wgrad(ht_o, [dproj_o], TILE_WG, "wgrad_o_in")
    half_o, half_o16 = _reduce_start(ODD_MATS, g, cidx)
    dx1, g["o_norm_w"], blocks_o = _mm_nt_rms([dproj_o], wb["o_w_in"], x1, p["o_norm_w"], dy, TILE_SEQ, "dx_odd",
                                              ODD_MATS, half_o16)
    g["o_dconv_w"] = small_o[0:D_CONV]
    g["o_dconv_b"] = small_o[31:32]
    g["o_ln_w"] = small_o[32:33]
    g["o_ln_b"] = small_o[33:34]
    g["o_pool_scale"] = small_o[34:35]

    g["e_w_out"] = _mm_wgrad(ut_e, [dx1], TILE_WG, "wgrad_e_out")
    dos, cgs, drest, g["e_conv_w"] = _even_mixer_bwd(dx1, wb["e_w_out"], proj_e, os_, ls_, p["e_conv_w"], T)
    dqs, dks, dvs = [], [], []
    for gi in range(3):
        dq, dk, dv = _attn_bwd(qs[gi], ks[gi], vs[gi], dos[gi], ls_[gi], cgs[gi], gi)
        dqs.append(dq)
        dks.append(dk)
        dvs.append(dv)
    dqkv, dnw = _qk_bwd(proj_e, dqs, dks, dvs, pos, freq, wq, wk, T)
    g["e_q_norm_w"] = dnw[0:1, 0:HEAD_DIM]
    g["e_k_norm_w"] = dnw[1:2, 0:HEAD_DIM]
    pieces = [dqkv, drest]
    g["e_w_in"] = _mm_wgrad(ht_e, pieces, TILE_WG, "wgrad_e_in")
    half_e, half_e16 = _reduce_start(EVEN_MATS, g, cidx)
    dx, g["e_norm_w"], blocks_e = _mm_nt_rms(pieces, wb["e_w_in"], x, p["e_norm_w"], dx1, TILE_SEQ, "dx_even",
                                             EVEN_MATS, half_e16)
    parts = {}
    for names, halves, blocks in ((ODD_MATS, half_o, blocks_o), (EVEN_MATS, half_e, blocks_e)):
        for n, h, r in zip(names, halves, blocks):
            parts[n] = _add_blocks(bidx, h, r, n)
    return lsum, dx, g, parts


BIG = ("e_w_in", "e_w_out", "o_w_in", "o_w_out")
SHARD_AXIS = {"e_w_in": 1, "e_w_out": 0, "o_w_in": 1, "o_w_out": 0}
N_CHIPS = 4


def _place():
    x, y, c = lax.axis_index("x"), lax.axis_index("y"), lax.axis_index("c")
    chips = [(1 - x, y), (x, 1 - y), (1 - x, 1 - y)]
    return x, y, c, chips


def _block_of(ref, name, block):
    rows, cols = ref.shape
    if SHARD_AXIS[name] == 1:
        cw = cols // N_CHIPS
        return ref.at[:, pl.ds(pl.multiple_of(block * cw, LANES), cw)]
    rw = rows // N_CHIPS
    return ref.at[pl.ds(pl.multiple_of(block * rw, rw), rw), :]


def _half_of(ref, name, half):
    rows, cols = ref.shape
    if SHARD_AXIS[name] == 1:
        return ref.at[pl.ds(pl.multiple_of(half * (rows // 2), rows // 2), rows // 2), :]
    return ref.at[:, pl.ds(pl.multiple_of(half * (cols // 2), LANES), cols // 2)]


def _sub(ref, name, block, half):
    rows, cols = ref.shape
    if SHARD_AXIS[name] == 1:
        cw, hr = cols // N_CHIPS, rows // 2
        return ref.at[pl.ds(pl.multiple_of(half * hr, hr), hr), pl.ds(pl.multiple_of(block * cw, LANES), cw)]
    rw, hc = rows // N_CHIPS, cols // 2
    return ref.at[pl.ds(pl.multiple_of(block * rw, rw), rw), pl.ds(pl.multiple_of(half * hc, LANES), hc)]


GATHER_COPIES = 7


class _Gather:
    def __init__(self, names, s_refs, f_refs, send, recv):
        self.names, self.s, self.f, self.send, self.recv = names, s_refs, f_refs, send, recv

    def _copy(self, k, src, dst, to):
        return pltpu.make_async_remote_copy(src_ref=src, dst_ref=dst, send_sem=self.send.at[k],
                                            recv_sem=self.recv.at[k], device_id=to, device_id_type=MESH)

    def _plan(self):
        x, y, c, chips = _place()
        me, sib = 2 * x + y, (x, y, 1 - c)
        first, relay_in, relay, last_in = [], [], [], []
        for wi, n in enumerate(self.names):
            k0 = wi * GATHER_COPIES
            s, f = self.s[wi], self.f[wi]
            own = _block_of(f, n, me)
            first.append(self._copy(k0 + 3, s, own, sib))
            last_in.append(self._copy(k0 + 3, s, own, sib))
            for j, (cx, cy) in enumerate(chips):
                first.append(self._copy(k0 + j, _half_of(s, n, c), _sub(f, n, me, c), (cx, cy, c)))
                mine = _sub(f, n, 2 * cx + cy, c)
                relay_in.append(self._copy(k0 + j, mine, mine, sib))
                relay.append(self._copy(k0 + 4 + j, mine, mine, sib))
                theirs = _sub(f, n, 2 * cx + cy, 1 - c)
                last_in.append(self._copy(k0 + 4 + j, theirs, theirs, sib))
        return first, relay_in, relay, last_in

    def begin(self):
        for cp in self._plan()[0]:
            cp.start()

    def relay(self):
        _, relay_in, relay, _ = self._plan()
        for arrived, onward in zip(relay_in, relay):
            arrived.wait_recv()
            onward.start()

    def end(self):
        first, _, relay, last_in = self._plan()
        for cp in last_in:
            cp.wait_recv()
        for cp in first + relay:
            cp.wait_send()

    def wait_relayed(self, j):
        self._plan()[3][1 + j].wait_recv()

    def end_rest(self):
        first, _, relay, last_in = self._plan()
        last_in[0].wait_recv()
        for cp in first + relay:
            cp.wait_send()


def _full_shape(n, s):
    r, cdim = s.shape
    return jax.ShapeDtypeStruct((r, cdim * N_CHIPS) if SHARD_AXIS[n] == 1 else (r * N_CHIPS, cdim), s.dtype)


def _gather_sems(names):
    k = GATHER_COPIES * len(names)
    return [pltpu.SemaphoreType.DMA((k,)), pltpu.SemaphoreType.DMA((k,))]


def _scatter_copies(names, h_refs, r_refs, send, recv):
    _, _, c, chips = _place()
    cps = []
    for wi, n in enumerate(names):
        for j, (cx, cy) in enumerate(chips):
            cps.append(pltpu.make_async_remote_copy(
                src_ref=_block_of(h_refs[wi], n, 2 * cx + cy), dst_ref=r_refs[wi].at[j],
                send_sem=send.at[wi * 3 + j], recv_sem=recv.at[wi * 3 + j],
                device_id=(cx, cy, c), device_id_type=MESH))
    return cps


def _scatter_sems(names):
    return [pltpu.SemaphoreType.DMA((3 * len(names),)), pltpu.SemaphoreType.DMA((3 * len(names),))]


def _allreduce_small(part, name):
    R = part.shape[0]

    def body(p_ref, o_ref, sbuf, cbuf, send, recv):
        x, y, c, chips = _place()
        me = 2 * x + y
        sib = (x, y, 1 - c)
        sbuf[c] = p_ref[...]
        mine = sbuf.at[c]
        d2d = pltpu.make_async_remote_copy(src_ref=mine, dst_ref=mine, send_sem=send.at[0], recv_sem=recv.at[0],
                                           device_id=sib, device_id_type=MESH)
        d2d.start()
        theirs = sbuf.at[1 - c]
        pltpu.make_async_remote_copy(src_ref=theirs, dst_ref=theirs, send_sem=send.at[0], recv_sem=recv.at[0],
                                     device_id=sib, device_id_type=MESH).wait_recv()
        cbuf[me] = sbuf[0] + sbuf[1]
        blk = cbuf.at[me]
        sends = [d2d]
        for j, (cx, cy) in enumerate(chips):
            cp = pltpu.make_async_remote_copy(src_ref=blk, dst_ref=blk, send_sem=send.at[1 + j], recv_sem=recv.at[1 + j],
                                              device_id=(cx, cy, c), device_id_type=MESH)
            cp.start()
            sends.append(cp)
        for j, (cx, cy) in enumerate(chips):
            got = cbuf.at[2 * cx + cy]
            pltpu.make_async_remote_copy(src_ref=got, dst_ref=got, send_sem=send.at[1 + j], recv_sem=recv.at[1 + j],
                                         device_id=(cx, cy, c), device_id_type=MESH).wait_recv()
        o_ref[...] = (cbuf[0] + cbuf[1]) + (cbuf[2] + cbuf[3])
        for cp in sends:
            cp.wait_send()

    vm = pl.BlockSpec(memory_space=pltpu.VMEM)
    return pl.pallas_call(
        body, name=name, in_specs=[vm], out_specs=vm,
        out_shape=jax.ShapeDtypeStruct(part.shape, F32),
        scratch_shapes=[pltpu.VMEM((2, R, LANES), F32), pltpu.VMEM((N_CHIPS, R, LANES), F32),
                        pltpu.SemaphoreType.DMA((4,)), pltpu.SemaphoreType.DMA((4,))],
    )(part)


def _half_shape(shape, name):
    r, cdim = shape
    return (r // 2, cdim) if SHARD_AXIS[name] == 1 else (r, cdim // 2)


def _shard_shape(shape, name):
    r, cdim = shape
    return (r, cdim // N_CHIPS) if SHARD_AXIS[name] == 1 else (r // N_CHIPS, cdim)


def _swap_to_sibling(names, srcs, name, pick_half):
    nw = len(names)

    def body(*refs):
        g_refs, r_refs = refs[:nw], refs[nw:2 * nw]
        send, recv = refs[2 * nw:]
        x, y, c, _ = _place()
        sib = (x, y, 1 - c)
        cps = []
        for wi, n in enumerate(names):
            src = _half_of(g_refs[wi], n, 1 - c) if pick_half else g_refs[wi]
            cp = pltpu.make_async_remote_copy(src_ref=src, dst_ref=r_refs[wi], send_sem=send.at[wi],
                                              recv_sem=recv.at[wi], device_id=sib, device_id_type=MESH)
            cp.start()
            cps.append(cp)
        for cp in cps:
            cp.wait()

    outs = [jax.ShapeDtypeStruct(_half_shape(g.shape, n) if pick_half else g.shape, g.dtype)
            for n, g in zip(names, srcs)]
    return pl.pallas_call(
        body, name=name, in_specs=[ANY] * nw, out_specs=[ANY] * nw, out_shape=outs,
        scratch_shapes=[pltpu.SemaphoreType.DMA((nw,)), pltpu.SemaphoreType.DMA((nw,))],
    )(*srcs)


def _add_half(cidx, g, r, name):
    rows, cols = r.shape
    tr = 256
    tc = cols if cols <= 1792 else (1792 if cols % 1792 == 0 else 1280)
    nr, nc = rows // tr, cols // tc

    def body(c_ref, g_ref, r_ref, o_ref, ob_ref):
        s = g_ref[...] + r_ref[...]
        o_ref[...] = s
        ob_ref[...] = s.astype(BF16)

    if SHARD_AXIS[name] == 1:
        gmap = lambda i, j, c_ref: (c_ref[0] * nr + i, j)
    else:
        gmap = lambda i, j, c_ref: (i, c_ref[0] * nc + j)
    same = lambda i, j, c_ref: (i, j)
    return pl.pallas_call(
        body, name="add_half_" + name,
        grid_spec=pltpu.PrefetchScalarGridSpec(
            num_scalar_prefetch=1, grid=(nr, nc),
            in_specs=[pl.BlockSpec((tr, tc), gmap), pl.BlockSpec((tr, tc), same)],
            out_specs=[pl.BlockSpec((tr, tc), same), pl.BlockSpec((tr, tc), same)]),
        out_shape=[jax.ShapeDtypeStruct(r.shape, F32), jax.ShapeDtypeStruct(r.shape, BF16)],
        compiler_params=pltpu.CompilerParams(dimension_semantics=("parallel", "parallel"), vmem_limit_bytes=VMEM_LIMIT),
    )(cidx, g, r)


def _add_blocks(bidx, h, r, name):
    _, rows, cols = r.shape
    tr = min(rows, 256)
    nr = rows // tr

    def body(b_ref, h_ref, r0, r1, r2, o_ref):
        o_ref[...] = ((h_ref[...] + r0[0].astype(F32)) + r1[0].astype(F32)) + r2[0].astype(F32)

    if SHARD_AXIS[name] == 1:
        hmap = lambda i, b_ref: (i, b_ref[0])
    else:
        hmap = lambda i, b_ref: (b_ref[0] * nr + i, 0)
    rspec = lambda j: pl.BlockSpec((1, tr, cols), lambda i, b_ref, j=j: (j, i, 0))
    return pl.pallas_call(
        body, name="add_blocks_" + name,
        grid_spec=pltpu.PrefetchScalarGridSpec(
            num_scalar_prefetch=1, grid=(nr,),
            in_specs=[pl.BlockSpec((tr, cols), hmap), rspec(0), rspec(1), rspec(2)],
            out_specs=pl.BlockSpec((tr, cols), lambda i, b_ref: (i, 0))),
        out_shape=jax.ShapeDtypeStruct((rows, cols), F32),
        compiler_params=pltpu.CompilerParams(dimension_semantics=("parallel",), vmem_limit_bytes=VMEM_LIMIT),
    )(bidx, h, r, r, r)


def _adam_math(w, g, m, v):
    c1 = 1.0 - ADAM_B1 ** ADAM_STEP
    c2 = 1.0 - ADAM_B2 ** ADAM_STEP
    nm = ADAM_B1 * m + (1.0 - ADAM_B1) * g
    nv = ADAM_B2 * v + (1.0 - ADAM_B2) * (g * g)
    delta = -ADAM_LR * ((nm / c1) / (jnp.sqrt(nv / c2) + ADAM_EPS) + ADAM_WD * w)
    return delta, nm, nv


def _adamw(w, g, m, v, name):
    def body(w_ref, g_ref, m_ref, v_ref, d_ref, nm_ref, nv_ref):
        d_ref[...], nm_ref[...], nv_ref[...] = _adam_math(w_ref[...], g_ref[...], m_ref[...], v_ref[...])

    spec = pl.BlockSpec(w.shape, lambda i: (0, 0))
    return _call(body, "adamw_" + name, (1,), [spec] * 4, [spec] * 3,
                 [jax.ShapeDtypeStruct(w.shape, F32)] * 3, sem=("arbitrary",))(w, g, m, v)


def _adamw_halves(cidx, w, mine, theirs, m, v, name):
    rows, cols = w.shape
    hr, hc = mine.shape
    tr = 128
    if SHARD_AXIS[name] == 1:
        ni = hr // tr
        wmap = lambda hh, i, c_ref: (hh * ni + i, 0)
    else:
        ni = hr // tr
        wmap = lambda hh, i, c_ref: (i, hh)
    hmap = lambda hh, i, c_ref: (i, 0)

    def body(c_ref, w_ref, a_ref, b_ref, m_ref, v_ref, g_ref, d_ref, nm_ref, nv_ref):
        g = jnp.where(pl.program_id(0) == c_ref[0], a_ref[...], b_ref[...])
        g_ref[...] = g
        d_ref[...], nm_ref[...], nv_ref[...] = _adam_math(w_ref[...], g, m_ref[...], v_ref[...])

    wspec = pl.BlockSpec((tr, hc), wmap)
    hspec = pl.BlockSpec((tr, hc), hmap)
    return pl.pallas_call(
        body, name="adamw_" + name,
        grid_spec=pltpu.PrefetchScalarGridSpec(
            num_scalar_prefetch=1, grid=(2, ni),
            in_specs=[wspec, hspec, hspec, wspec, wspec], out_specs=[wspec] * 4),
        out_shape=[jax.ShapeDtypeStruct(w.shape, F32)] * 4,
        compiler_params=pltpu.CompilerParams(dimension_semantics=("parallel", "parallel"), vmem_limit_bytes=VMEM_LIMIT),
    )(cidx, w, mine, theirs, m, v)


SMALL = ("e_norm_w", "e_q_norm_w", "e_k_norm_w", "e_conv_w", "o_norm_w", "o_pool_w", "o_pool_scale",
         "o_dconv_w", "o_dconv_b", "o_ln_w", "o_ln_b")
SMALL_SHARDED = ("e_conv_w", "o_norm_w", "o_pool_scale", "o_dconv_w", "o_dconv_b", "o_ln_w", "o_ln_b")
WEIGHTS = ("e_norm_w", "e_w_in", "e_q_norm_w", "e_k_norm_w", "e_conv_w", "e_w_out", "o_norm_w", "o_w_in",
           "o_pool_w", "o_pool_scale", "o_dconv_w", "o_dconv_b", "o_ln_w", "o_ln_b", "o_w_out")


def _pack(arrs):
    flat = jnp.concatenate([a.reshape(-1) for a in arrs])
    rows = -(-flat.shape[0] // (LANES * SUBLANES)) * SUBLANES
    flat = jnp.pad(flat, (0, rows * LANES - flat.shape[0]))
    return flat.reshape(rows, LANES)


def _unpack(packed, shapes):
    flat = packed.reshape(-1)
    out, off = [], 0
    for s in shapes:
        n = int(np.prod(s))
        out.append(flat[off:off + n].reshape(s))
        off += n
    return out


def _gather_last(a, block, width):
    return lax.dynamic_slice_in_dim(a, block * width, width, axis=a.ndim - 1)


def kernel(x, positions, e_norm_w, e_w_in, e_q_norm_w, e_k_norm_w, e_conv_w, e_w_out, o_norm_w, o_w_in, o_pool_w, o_pool_scale, o_dconv_w, o_dconv_b, o_ln_w, o_ln_b, o_w_out, loss_target, m_e_norm_w, m_e_w_in, m_e_q_norm_w, m_e_k_norm_w, m_e_conv_w, m_e_w_out, m_o_norm_w, m_o_w_in, m_o_pool_w, m_o_pool_scale, m_o_dconv_w, m_o_dconv_b, m_o_ln_w, m_o_ln_b, m_o_w_out, v_e_norm_w, v_e_w_in, v_e_q_norm_w, v_e_k_norm_w, v_e_conv_w, v_e_w_out, v_o_norm_w, v_o_w_in, v_o_pool_w, v_o_pool_scale, v_o_dconv_w, v_o_dconv_b, v_o_ln_w, v_o_ln_b, v_o_w_out):
    given = dict(e_norm_w=e_norm_w, e_w_in=e_w_in, e_q_norm_w=e_q_norm_w, e_k_norm_w=e_k_norm_w, e_conv_w=e_conv_w,
                 e_w_out=e_w_out, o_norm_w=o_norm_w, o_w_in=o_w_in, o_pool_w=o_pool_w, o_pool_scale=o_pool_scale,
                 o_dconv_w=o_dconv_w, o_dconv_b=o_dconv_b, o_ln_w=o_ln_w, o_ln_b=o_ln_b, o_w_out=o_w_out)
    mom = dict(e_norm_w=m_e_norm_w, e_w_in=m_e_w_in, e_q_norm_w=m_e_q_norm_w, e_k_norm_w=m_e_k_norm_w,
               e_conv_w=m_e_conv_w, e_w_out=m_e_w_out, o_norm_w=m_o_norm_w, o_w_in=m_o_w_in, o_pool_w=m_o_pool_w,
               o_pool_scale=m_o_pool_scale, o_dconv_w=m_o_dconv_w, o_dconv_b=m_o_dconv_b, o_ln_w=m_o_ln_w,
               o_ln_b=m_o_ln_b, o_w_out=m_o_w_out)
    var = dict(e_norm_w=v_e_norm_w, e_w_in=v_e_w_in, e_q_norm_w=v_e_q_norm_w, e_k_norm_w=v_e_k_norm_w,
               e_conv_w=v_e_conv_w, e_w_out=v_e_w_out, o_norm_w=v_o_norm_w, o_w_in=v_o_w_in, o_pool_w=v_o_pool_w,
               o_pool_scale=v_o_pool_scale, o_dconv_w=v_o_dconv_w, o_dconv_b=v_o_dconv_b, o_ln_w=v_o_ln_w,
               o_ln_b=v_o_ln_b, o_w_out=v_o_w_out)
    S = x.shape[1]
    mx, my, mc = lax.axis_index("x"), lax.axis_index("y"), lax.axis_index("c")
    chip = 2 * mx + my
    cidx = jnp.reshape(mc, (1,)).astype(jnp.int32)
    bidx = jnp.reshape(chip, (1,)).astype(jnp.int32)

    shards = {n: given[n][0].astype(BF16) for n in BIG}
    shard_sizes = [int(np.prod(given[n].shape)) for n in SMALL_SHARDED]
    own = _pack([given[n] for n in SMALL_SHARDED])
    rows = own.shape[0]
    slots = jnp.zeros((N_CHIPS, rows, LANES), F32)
    own = jnp.where(mc == 0, own, 0.0)
    slots = lax.dynamic_update_slice(slots, own[None], (chip, 0, 0))
    gathered = _allreduce_small(slots.reshape(N_CHIPS * rows, LANES), "gather_small")
    gathered = gathered.reshape(N_CHIPS, rows * LANES)
    p = {}
    off = 0
    for n, size in zip(SMALL_SHARDED, shard_sizes):
        sh = given[n].shape[1:]
        parts = gathered[:, off:off + size].reshape((N_CHIPS,) + sh)
        fullp = jnp.moveaxis(parts, 0, -2).reshape(sh[:-1] + (N_CHIPS * sh[-1],))
        p[n] = fullp.reshape(-1, fullp.shape[-1])
        off += size
    p["e_norm_w"] = e_norm_w
    p["e_q_norm_w"] = e_q_norm_w
    p["e_k_norm_w"] = e_k_norm_w
    p["o_pool_w"] = o_pool_w[0]

    lsum, dx, g, parts = _local_step(x[0], positions.reshape(S, 1), loss_target[0], shards, p, cidx, bidx)
    loss = lax.psum(0.5 * jnp.sum(lsum) / float(D_MODEL), ("x", "y", "c"))

    tot = _unpack(_allreduce_small(_pack([g[n] for n in SMALL]), "allreduce_small"), [g[n].shape for n in SMALL])
    gsmall = dict(zip(SMALL, tot))
    grads = {}
    for n in SMALL:
        gv = gsmall[n]
        if n in SMALL_SHARDED:
            gv = _gather_last(gv, chip, gv.shape[-1] // N_CHIPS)
        grads[n] = gv.reshape(given[n].shape)

    theirs = _swap_to_sibling(BIG, [parts[n] for n in BIG], "swap_reduced", False)

    delta, new_m, new_v = {}, {}, {}
    for n, other in zip(BIG, theirs):
        sh = given[n].shape
        gs, d, nm, nv = _adamw_halves(cidx, given[n][0], parts[n], other, mom[n][0], var[n][0], n)
        grads[n], delta[n], new_m[n], new_v[n] = gs.reshape(sh), d.reshape(sh), nm.reshape(sh), nv.reshape(sh)
    shapes = [given[n].shape for n in SMALL]
    packed = [_pack([src[n] for n in SMALL]) for src in (given, grads, mom, var)]
    for dst, pk in zip((delta, new_m, new_v), _adamw(*packed, "small")):
        for n, a in zip(SMALL, _unpack(pk, shapes)):
            dst[n] = a
    return (loss, dx[None], *[grads[n] for n in WEIGHTS], *[delta[n] for n in WEIGHTS],
            *[new_m[n] for n in WEIGHTS], *[new_v[n] for n in WEIGHTS])
```

```python
import numpy as np
import jax
import jax.numpy as jnp
from jax import lax
from jax.experimental import pallas as pl
from jax.experimental.pallas import tpu as pltpu

F32 = jnp.float32
BF16 = jnp.bfloat16

D_MODEL = 1024
HEAD_DIM = 64
A_WIDTH = 512
A_HEADS = 8
A_GROUPS = ((128, 1), (512, 4), (2048, 16))
QBLK = 128
ROT_DIM = 16
ROPE_THETA = 500000.0
POOL_SIZES = (2, 4, 8, 16)
D_CONV = 31
SC_WIDTH = 3
EVEN_IN = 7168
ODD_IN = 2560
EPS = 1e-6
NEG = -1e30
ADAM_LR, ADAM_B1, ADAM_B2, ADAM_EPS, ADAM_WD, ADAM_STEP = 0.001, 0.9, 0.999, 1e-08, 0.01, 10

LANES = 128
SUBLANES = 8
HALO = 32
VMEM_LIMIT = 52 * 1024 * 1024
MESH = pl.DeviceIdType.MESH
ANY = pl.BlockSpec(memory_space=pl.ANY)

NT_DIMS = (((1,), (1,)), ((), ()))
TN_DIMS = (((0,), (0,)), ((), ()))


def _call(body, name, grid, in_specs, out_specs, out_shape, scratch=(), sem=None, aliases=None):
    return pl.pallas_call(
        body, name=name, grid=grid, in_specs=in_specs, out_specs=out_specs, out_shape=out_shape,
        scratch_shapes=list(scratch), input_output_aliases=aliases or {},
        compiler_params=pltpu.CompilerParams(dimension_semantics=sem, vmem_limit_bytes=VMEM_LIMIT))


def _sig(v):
    return jax.nn.sigmoid(v)


def _dsilu(v, s):
    return s * (1.0 + v * (1.0 - s))


def _out_projection(ut_ref, w_ref, lo, hi):
    acc = None
    for k, v in enumerate((lo, hi)):
        ut_ref[k * A_WIDTH:(k + 1) * A_WIDTH, :] = v.T.astype(BF16)
        part = jnp.dot(v.astype(BF16), w_ref[k * A_WIDTH:(k + 1) * A_WIDTH, :], preferred_element_type=F32)
        acc = part if acc is None else acc + part
    return acc


def _cs8(v):
    return v.reshape(v.shape[0] // SUBLANES, SUBLANES, v.shape[1]).sum(axis=0)


def _seg_mean():
    r = lax.broadcasted_iota(jnp.int32, (LANES, LANES), 0) // HEAD_DIM
    c = lax.broadcasted_iota(jnp.int32, (LANES, LANES), 1) // HEAD_DIM
    return jnp.where(r == c, 1.0 / HEAD_DIM, 0.0).astype(BF16)


def _segsum(v, ones):
    hi = v.astype(BF16)
    lo = (v - hi.astype(F32)).astype(BF16)
    return (jnp.dot(hi, ones, preferred_element_type=F32) + jnp.dot(lo, ones, preferred_element_type=F32))


def _head_rstd(v, seg_mean):
    return lax.rsqrt(jnp.dot((v * v).astype(BF16), seg_mean, preferred_element_type=F32) + EPS)


def _rope_tables(pos_ref, freq_ref):
    ang = pos_ref[...].astype(F32) * freq_ref[...]
    cosv, sinv = jnp.cos(ang), jnp.sin(ang)
    lm = lax.broadcasted_iota(jnp.int32, ang.shape, 1) % HEAD_DIM
    half = ROT_DIM // 2
    c = jnp.where(lm < ROT_DIM, cosv, 1.0)
    s1 = jnp.where((lm >= half) & (lm < ROT_DIM), sinv, 0.0)
    s2 = jnp.where(lm < half, -sinv, 0.0)
    return c, s1, s2


def _freq_table():
    half = ROT_DIM // 2
    inv = ROPE_THETA ** (-np.arange(half, dtype=np.float64) / half)
    lane = np.arange(LANES) % HEAD_DIM
    f = np.where(lane < ROT_DIM, inv[lane % half], 0.0)
    return jnp.asarray(f.reshape(1, LANES), F32)


def _load_once(hbm_ref, vmem_ref, sem):
    @pl.when(pl.program_id(0) == 0)
    def _():
        cp = pltpu.make_async_copy(hbm_ref, vmem_ref, sem)
        cp.start()
        cp.wait()


def _rms_rows(x_ref, nw_ref):
    xv = x_ref[...]
    ms = jnp.mean(xv * xv, axis=-1, keepdims=True)
    return xv * lax.rsqrt(ms + EPS) * nw_ref[...]


def _inproj(x, nw, w, tm, tn, name):
    S, N = x.shape[0], w.shape[1]

    def body(x_ref, nw_ref, w_hbm, o_ref, ht_ref, w_v, sem):
        _load_once(w_hbm, w_v, sem)
        h = _rms_rows(x_ref, nw_ref)
        ht_ref[...] = h.T.astype(BF16)
        hb = h.astype(BF16)
        for j in range(N // tn):
            o_ref[:, j * tn:(j + 1) * tn] = jnp.dot(hb, w_v[:, j * tn:(j + 1) * tn], preferred_element_type=F32)

    return _call(
        body, name, (S // tm,),
        [pl.BlockSpec((tm, D_MODEL), lambda i: (i, 0)),
         pl.BlockSpec((1, D_MODEL), lambda i: (0, 0)), ANY],
        [pl.BlockSpec((tm, N), lambda i: (i, 0)),
         pl.BlockSpec((D_MODEL, tm), lambda i: (0, i))],
        [jax.ShapeDtypeStruct((S, N), F32), jax.ShapeDtypeStruct((D_MODEL, S), BF16)],
        scratch=[pltpu.VMEM(w.shape, BF16), pltpu.SemaphoreType.DMA(())], sem=("arbitrary",))(x, nw, w)


def _inproj_gathering(x, nw, shard, bidx, first, late_names, late_shards, tm, name):
    S = x.shape[0]
    ni = S // tm
    K, cw = shard.shape
    nl = len(late_names)
    last = N_CHIPS - 1

    def body(b_ref, x_ref, nw_ref, s_hbm, *rest):
        ls_refs = rest[:nl]
        o_ref, ht_ref, f_hbm = rest[nl:nl + 3]
        lf_refs = rest[nl + 3:2 * nl + 3]
        hs, w_blk, lsem, send1, recv1, send2, recv2 = rest[2 * nl + 3:]
        j, i = pl.program_id(0), pl.program_id(1)
        g1 = _Gather((first,), (s_hbm,), (f_hbm,), send1, recv1)
        g2 = _Gather(late_names, ls_refs, lf_refs, send2, recv2)
        _, _, _, chips = _place()

        def load_block(src):
            cp = pltpu.make_async_copy(src, w_blk, lsem)
            cp.start()
            cp.wait()

        @pl.when((j == 0) & (i == 0))
        def _():
            g1.begin(relations=(0, 1))
            load_block(s_hbm)

        for r, (cx, cy) in enumerate(chips):
            @pl.when((j == r + 1) & (i == 0))
            def _(r=r, cx=cx, cy=cy):
                g1.wait_relayed(r)
                load_block(_block_of(f_hbm, first, 2 * cx + cy))

        @pl.when((j == 2) & (i == 0))
        def _():
            g1.relay(relations=(2,))
            g2.begin()

        pl.when((j == last) & (i == ni // 2))(g2.relay)

        rows = pl.ds(pl.multiple_of(i * tm, tm), tm)

        @pl.when(j == 0)
        def _():
            h = _rms_rows(x_ref, nw_ref)
            hs[rows, :] = h.astype(BF16)
            ht_ref[...] = h.T.astype(BF16)

        o_ref[...] = jnp.dot(hs[rows, :], w_blk[...], preferred_element_type=F32)

        @pl.when((j == 0) & (i == ni - 1))
        def _():
            g1.relay(relations=(0, 1))
            g1.begin(relations=(2,), sibling=False)

        @pl.when((j == last) & (i == ni - 1))
        def _():
            g1.end_rest()
            g2.end()

    def block_of_step(j, b_ref):
        return jnp.bitwise_xor(b_ref[0], jnp.bitwise_or(jnp.left_shift(jnp.bitwise_and(j, 1), 1), jnp.right_shift(j, 1)))

    outs = pl.pallas_call(
        body, name=name,
        grid_spec=pltpu.PrefetchScalarGridSpec(
            num_scalar_prefetch=1, grid=(N_CHIPS, ni),
            in_specs=[pl.BlockSpec((tm, D_MODEL), lambda j, i, b: (jnp.where(j == 0, i, 0), 0)),
                      pl.BlockSpec((1, D_MODEL), lambda j, i, b: (0, 0)), ANY] + [ANY] * nl,
            out_specs=[pl.BlockSpec((tm, cw), lambda j, i, b: (i, block_of_step(j, b))),
                       pl.BlockSpec((D_MODEL, tm), lambda j, i, b: (0, jnp.where(j == 0, i, ni - 1))),
                       ANY] + [ANY] * nl,
            scratch_shapes=[pltpu.VMEM((S, D_MODEL), BF16), pltpu.VMEM((K, cw), BF16), pltpu.SemaphoreType.DMA(())] +
            _gather_sems((first,)) + _gather_sems(late_names)),
        out_shape=[jax.ShapeDtypeStruct((S, cw * N_CHIPS), F32), jax.ShapeDtypeStruct((D_MODEL, S), BF16),
                   _full_shape(first, shard)] + [_full_shape(n, s) for n, s in zip(late_names, late_shards)],
        compiler_params=pltpu.CompilerParams(dimension_semantics=("arbitrary", "arbitrary"),
                                             vmem_limit_bytes=VMEM_LIMIT),
    )(bidx, x, nw, shard, *late_shards)
    return outs[0], outs[1], outs[2], list(outs[3:])


def _piece_blocks(pieces, tk, axis):
    starts, counts, s = [], [], 0
    for p in pieces:
        n = p.shape[axis] // tk
        starts.append(s)
        counts.append(n)
        s += n
    return starts, counts, s


def _mm_nt_rms(pieces, w, x, nw, dres, tm, name, scatter_names=(), scatter_halves=()):
    S = x.shape[0]
    npc = len(pieces)
    ni = S // tm
    ns = len(scatter_names)
    offs = np.cumsum([0] + [p.shape[1] for p in pieces]).tolist()

    def body(*refs):
        p_refs = refs[:npc]
        w_hbm, x_ref, nw_ref, dr_ref = refs[npc:npc + 4]
        h_refs = refs[npc + 4:npc + 4 + ns]
        dx_ref, dnw_ref = refs[npc + 4 + ns:npc + 6 + ns]
        r_refs = refs[npc + 6 + ns:npc + 6 + 2 * ns]
        w_v, sem, nacc = refs[npc + 6 + 2 * ns:npc + 9 + 2 * ns]
        i = pl.program_id(0)
        if ns:
            send, recv = refs[npc + 9 + 2 * ns:]

            @pl.when(i == 0)
            def _():
                for cp in _scatter_copies(scatter_names, h_refs, r_refs, send, recv):
                    cp.start()
        _load_once(w_hbm, w_v, sem)

        @pl.when(i == 0)
        def _():
            nacc[...] = jnp.zeros_like(nacc)

        dh = None
        for p in range(npc):
            part = lax.dot_general(p_refs[p][...].astype(BF16), w_v[:, offs[p]:offs[p + 1]], NT_DIMS,
                                   preferred_element_type=F32)
            dh = part if dh is None else dh + part
        xv = x_ref[...]
        rs = lax.rsqrt(jnp.mean(xv * xv, axis=-1, keepdims=True) + EPS)
        xh = xv * rs
        nacc[...] += _cs8(dh * xh)
        dxh = dh * nw_ref[...]
        dx_ref[...] = dr_ref[...] + rs * (dxh - xh * jnp.mean(dxh * xh, axis=-1, keepdims=True))

        @pl.when(i == ni - 1)
        def _():
            dnw_ref[...] = jnp.sum(nacc[...], axis=0, keepdims=True)
            if ns:
                for cp in _scatter_copies(scatter_names, h_refs, r_refs, send, recv):
                    cp.wait()

    row = pl.BlockSpec((tm, D_MODEL), lambda i: (i, 0))
    outs = _call(
        body, name, (ni,),
        [pl.BlockSpec((tm, p.shape[1]), lambda i: (i, 0)) for p in pieces] +
        [ANY, row, pl.BlockSpec((1, D_MODEL), lambda i: (0, 0)), row] + [ANY] * ns,
        [row, pl.BlockSpec((1, D_MODEL), lambda i: (0, 0))] + [ANY] * ns,
        [jax.ShapeDtypeStruct((S, D_MODEL), F32), jax.ShapeDtypeStruct((1, D_MODEL), F32)] +
        [jax.ShapeDtypeStruct((3,) + _shard_shape(h.shape, n), h.dtype) for n, h in zip(scatter_names, scatter_halves)],
        scratch=[pltpu.VMEM(w.shape, BF16), pltpu.SemaphoreType.DMA(()), pltpu.VMEM((SUBLANES, D_MODEL), F32)] +
        (_scatter_sems(scatter_names) if ns else []),
        sem=("arbitrary",))(*pieces, w, x, nw, dres, *scatter_halves)
    return outs[0], outs[1], list(outs[2:])


def _mm_wgrad(at, pieces, tn, name):
    M, S = at.shape
    starts, counts, nj = _piece_blocks(pieces, tn, 1)
    npc = len(pieces)

    def body(*refs):
        a_hbm = refs[0]
        p_refs = refs[1:1 + npc]
        o_ref, a_v, sem = refs[1 + npc:]
        j = pl.program_id(0)
        _load_once(a_hbm, a_v, sem)
        for p in range(npc):
            @pl.when((j >= starts[p]) & (j < starts[p] + counts[p]))
            def _(p=p):
                o_ref[...] = jnp.dot(a_v[...], p_refs[p][...].astype(BF16), preferred_element_type=F32)

    def pspec(p):
        return pl.BlockSpec((S, tn), lambda j: (0, jnp.clip(j - starts[p], 0, counts[p] - 1)))

    return _call(
        body, name, (nj,),
        [ANY] + [pspec(p) for p in range(npc)],
        pl.BlockSpec((M, tn), lambda j: (0, j)),
        jax.ShapeDtypeStruct((M, nj * tn), F32),
        scratch=[pltpu.VMEM(at.shape, BF16), pltpu.SemaphoreType.DMA(())], sem=("arbitrary",))(at, *pieces)


def _stream_spec(d, T):
    return pl.BlockSpec((d, T // d, A_WIDTH), lambda i: (0, i, 0))


def _stream_shape(d, S, dtype):
    return jax.ShapeDtypeStruct((d, S // d, A_WIDTH), dtype)


N_CHUNK = A_WIDTH // LANES


def _to_tokens(ref, scr, d, T):
    if d == 1:
        return ref[0].astype(F32)
    for r in range(d):
        for ch in range(N_CHUNK):
            scr.at[ch][pl.ds(r, T // d, stride=d), :] = ref[r, :, ch * LANES:(ch + 1) * LANES].astype(F32)
    return _get(scr)


def _from_tokens(out_ref, scr, d, T):
    for r in range(d):
        for ch in range(N_CHUNK):
            out_ref[r, :, ch * LANES:(ch + 1) * LANES] = scr.at[ch][pl.ds(r, T // d, stride=d), :].astype(out_ref.dtype)


def _put(scr, val):
    for ch in range(N_CHUNK):
        scr[ch] = val[:, ch * LANES:(ch + 1) * LANES]


def _get(scr):
    return jnp.concatenate([scr[ch] for ch in range(N_CHUNK)], axis=1)


def _chunked(T):
    return pltpu.VMEM((N_CHUNK, T, LANES), F32)


def _compact_spec(d, T):
    return pl.BlockSpec((d, T // d, LANES), lambda i: (0, i, 0))


def _compact_shape(d, S):
    return jax.ShapeDtypeStruct((d, S // d, LANES), F32)


def _compact_to_tokens(ref, scr, d, T):
    if d == 1:
        return ref[0]
    for r in range(d):
        scr[pl.ds(r, T // d, stride=d), :] = ref[r]
    return scr[...]


def _compact_from_tokens(out_ref, scr, val, d, T):
    if d == 1:
        out_ref[0] = val
        return
    scr[...] = val
    for r in range(d):
        out_ref[r] = scr[pl.ds(r, T // d, stride=d), :]


def _head_expander():
    r = lax.broadcasted_iota(jnp.int32, (LANES, A_WIDTH), 0)
    c = lax.broadcasted_iota(jnp.int32, (LANES, A_WIDTH), 1) // HEAD_DIM
    return (r == c).astype(BF16)


def _head_reducer():
    r = lax.broadcasted_iota(jnp.int32, (A_WIDTH, LANES), 0) // HEAD_DIM
    c = lax.broadcasted_iota(jnp.int32, (A_WIDTH, LANES), 1)
    return (r == c).astype(BF16)


def _qkv_prep(proj, pos, freq, wq, wk, T):
    S = proj.shape[0]
    qk_w = 3 * A_WIDTH

    def body(q_ref, k_ref, v_ref, pos_ref, f_ref, wq_ref, wk_ref, *rest):
        outs, scr = rest[:9], rest[9]
        seg_mean = _seg_mean()
        c, s1, s2 = _rope_tables(pos_ref, f_ref)
        for t, (src, w_ref) in enumerate(((q_ref, wq_ref), (k_ref, wk_ref), (v_ref, None))):
            for g in range(3):
                d = A_GROUPS[g][1]
                out = outs[3 * t + g]
                for ch in range(A_WIDTH // LANES):
                    cs = slice(ch * LANES, (ch + 1) * LANES)
                    v = src[:, g * A_WIDTH + ch * LANES: g * A_WIDTH + (ch + 1) * LANES]
                    if w_ref is not None:
                        y = v * _head_rstd(v, seg_mean) * w_ref[...]
                        v = y * c + pltpu.roll(y, 8, 1) * s1 + pltpu.roll(y, LANES - 8, 1) * s2
                    if d == 1:
                        out[0, :, cs] = v.astype(BF16)
                    else:
                        scr[ch] = v
                if d > 1:
                    _from_tokens(out, scr, d, T)

    ds_ = [A_GROUPS[g][1] for g in range(3)] * 3
    return _call(
        body, "qkv_prep", (S // T,),
        [pl.BlockSpec((T, qk_w), lambda i: (i, 0)), pl.BlockSpec((T, qk_w), lambda i: (i, 1)),
         pl.BlockSpec((T, qk_w), lambda i: (i, 2)),
         pl.BlockSpec((T, 1), lambda i: (i, 0)), pl.BlockSpec((1, LANES), lambda i: (0, 0)),
         pl.BlockSpec((1, LANES), lambda i: (0, 0)), pl.BlockSpec((1, LANES), lambda i: (0, 0))],
        [_stream_spec(d, T) for d in ds_],
        [_stream_shape(d, S, BF16) for d in ds_],
        scratch=[_chunked(T)], sem=("parallel",))(proj, proj, proj, pos, freq, wq, wk)


def _attn_mask(i):
    qi = lax.broadcasted_iota(jnp.int32, (QBLK, 2 * QBLK), 0) + QBLK
    kj = lax.broadcasted_iota(jnp.int32, (QBLK, 2 * QBLK), 1)
    dist = qi - kj
    return (dist >= 0) & (dist <= QBLK) & ((i > 0) | (kj >= QBLK))


ATT_BLK = (None, QBLK, A_WIDTH)
ATT_CBLK = (None, QBLK, LANES)


def _first_head_lanes():
    return lax.broadcasted_iota(jnp.int32, (1, LANES), 1) < HEAD_DIM


def _split_heads(v, first):
    zero = jnp.zeros_like(v)
    return jnp.where(first, v, zero), jnp.where(first, zero, v)


def _attn_fwd(q, k, v, g):
    d, n, _ = q.shape
    nb = n // QBLK

    def body(q_ref, kp_ref, kc_ref, vp_ref, vc_ref, o_ref, l_ref, s_scr, p_scr):
        i = pl.program_id(1)
        mask = _attn_mask(i)
        first = _first_head_lanes()
        for pr in range(A_HEADS // 2):
            ps = slice(pr * LANES, (pr + 1) * LANES)
            kc = jnp.concatenate([kp_ref[:, ps], kc_ref[:, ps]], axis=0)
            for e, qh in enumerate(_split_heads(q_ref[:, ps], first)):
                s_scr[2 * pr + e] = lax.dot_general(qh, kc, NT_DIMS, preferred_element_type=F32)
        lane = lax.broadcasted_iota(jnp.int32, (1, LANES), 1)
        lrow = jnp.zeros((QBLK, LANES), F32)
        for h in range(A_HEADS):
            s = jnp.where(mask, s_scr[h] * (HEAD_DIM ** -0.5), NEG)
            m = jnp.max(s, axis=-1, keepdims=True)
            p = jnp.exp(s - m)
            den = jnp.sum(p, axis=-1, keepdims=True)
            p_scr[h] = (p / den).astype(BF16)
            lrow = jnp.where(lane == h, m + jnp.log(den), lrow)
        l_ref[...] = lrow
        for pr in range(A_HEADS // 2):
            ps = slice(pr * LANES, (pr + 1) * LANES)
            va, vb = _split_heads(jnp.concatenate([vp_ref[:, ps], vc_ref[:, ps]], axis=0), first)
            o_ref[:, ps] = (jnp.dot(p_scr[2 * pr], va, preferred_element_type=F32) +
                            jnp.dot(p_scr[2 * pr + 1], vb, preferred_element_type=F32)).astype(BF16)

    prev = lambda r, i: (r, jnp.maximum(i - 1, 0), 0)
    cur = lambda r, i: (r, i, 0)
    return _call(
        body, "attn_fwd_g%d" % g, (d, nb),
        [pl.BlockSpec(ATT_BLK, cur), pl.BlockSpec(ATT_BLK, prev), pl.BlockSpec(ATT_BLK, cur),
         pl.BlockSpec(ATT_BLK, prev), pl.BlockSpec(ATT_BLK, cur)],
        [pl.BlockSpec(ATT_BLK, cur), pl.BlockSpec(ATT_CBLK, cur)],
        [jax.ShapeDtypeStruct((d, n, A_WIDTH), BF16), jax.ShapeDtypeStruct((d, n, LANES), F32)],
        scratch=[pltpu.VMEM((A_HEADS, QBLK, 2 * QBLK), F32), pltpu.VMEM((A_HEADS, QBLK, 2 * QBLK), BF16)],
        sem=("parallel", "parallel"))(q, k, k, v, v)


def _attn_bwd(q, k, v, do, lse, cg, g):
    d, n, _ = q.shape
    nb = n // QBLK
    scale = HEAD_DIM ** -0.5

    def body(q_ref, kp_ref, kc_ref, vp_ref, vc_ref, do_ref, l_ref, c_ref, dq_ref, dk_ref, dv_ref, ck, cv,
             s_scr, dp_scr, p_scr, ds_scr):
        i = pl.program_id(1)

        @pl.when(i == 0)
        def _():
            ck[...] = jnp.zeros_like(ck)
            cv[...] = jnp.zeros_like(cv)

        @pl.when(i < nb)
        def _():
            mask = _attn_mask(i)
            first = _first_head_lanes()
            for pr in range(A_HEADS // 2):
                ps = slice(pr * LANES, (pr + 1) * LANES)
                kc = jnp.concatenate([kp_ref[:, ps], kc_ref[:, ps]], axis=0)
                vc = jnp.concatenate([vp_ref[:, ps], vc_ref[:, ps]], axis=0)
                qs = _split_heads(q_ref[:, ps], first)
                dos = _split_heads(do_ref[:, ps], first)
                for e in range(2):
                    s_scr[2 * pr + e] = lax.dot_general(qs[e], kc, NT_DIMS, preferred_element_type=F32)
                    dp_scr[2 * pr + e] = lax.dot_general(dos[e], vc, NT_DIMS, preferred_element_type=F32)
            for h in range(A_HEADS):
                p = jnp.where(mask, jnp.exp(s_scr[h] * scale - l_ref[:, h:h + 1]), 0.0)
                p_scr[h] = p.astype(BF16)
                ds_scr[h] = (p * (dp_scr[h] + c_ref[:, h:h + 1]) * scale).astype(BF16)
            for pr in range(A_HEADS // 2):
                ps = slice(pr * LANES, (pr + 1) * LANES)
                ks = _split_heads(jnp.concatenate([kp_ref[:, ps], kc_ref[:, ps]], axis=0), first)
                qs = _split_heads(q_ref[:, ps], first)
                dos = _split_heads(do_ref[:, ps], first)
                dq = dkc = dvc = None
                for e in range(2):
                    ds = ds_scr[2 * pr + e]
                    a = jnp.dot(ds, ks[e], preferred_element_type=F32)
                    b = lax.dot_general(ds, qs[e], TN_DIMS, preferred_element_type=F32)
                    c = lax.dot_general(p_scr[2 * pr + e], dos[e], TN_DIMS, preferred_element_type=F32)
                    dq, dkc, dvc = (a, b, c) if e == 0 else (dq + a, dkc + b, dvc + c)
                dq_ref[:, ps] = dq.astype(BF16)
                dk_ref[:, ps] = (ck[:, ps] + dkc[:QBLK]).astype(BF16)
                dv_ref[:, ps] = (cv[:, ps] + dvc[:QBLK]).astype(BF16)
                ck[:, ps] = dkc[QBLK:]
                cv[:, ps] = dvc[QBLK:]

        @pl.when(i == nb)
        def _():
            dk_ref[...] = ck[...].astype(BF16)
            dv_ref[...] = cv[...].astype(BF16)

    qi = lambda i: jnp.minimum(i, nb - 1)
    cur = lambda r, i: (r, qi(i), 0)
    prev = lambda r, i: (r, jnp.maximum(qi(i) - 1, 0), 0)
    late = lambda r, i: (r, jnp.maximum(i - 1, 0), 0)
    return _call(
        body, "attn_bwd_g%d" % g, (d, nb + 1),
        [pl.BlockSpec(ATT_BLK, cur), pl.BlockSpec(ATT_BLK, prev), pl.BlockSpec(ATT_BLK, cur),
         pl.BlockSpec(ATT_BLK, prev), pl.BlockSpec(ATT_BLK, cur),
         pl.BlockSpec(ATT_BLK, cur), pl.BlockSpec(ATT_CBLK, cur), pl.BlockSpec(ATT_CBLK, cur)],
        [pl.BlockSpec(ATT_BLK, cur), pl.BlockSpec(ATT_BLK, late), pl.BlockSpec(ATT_BLK, late)],
        [jax.ShapeDtypeStruct((d, n, A_WIDTH), BF16)] * 3,
        scratch=[pltpu.VMEM((QBLK, A_WIDTH), F32), pltpu.VMEM((QBLK, A_WIDTH), F32),
                 pltpu.VMEM((A_HEADS, QBLK, 2 * QBLK), F32), pltpu.VMEM((A_HEADS, QBLK, 2 * QBLK), F32),
                 pltpu.VMEM((A_HEADS, QBLK, 2 * QBLK), BF16), pltpu.VMEM((A_HEADS, QBLK, 2 * QBLK), BF16)],
        sem=("parallel", "arbitrary"))(q, k, k, v, v, do, lse, cg)


def _merge_weights(l0, l1, l2):
    mx = jnp.maximum(jnp.maximum(l0, l1), l2)
    e0, e1, e2 = jnp.exp(l0 - mx), jnp.exp(l1 - mx), jnp.exp(l2 - mx)
    den = e0 + e1 + e2
    return e0 / den, e1 / den, e2 / den


def _even_specs(T, S):
    t8 = T // SUBLANES
    last8 = S // SUBLANES - 1
    col = lambda c: pl.BlockSpec((T, A_WIDTH), lambda i: (i, c))
    prev8 = lambda c: pl.BlockSpec((SUBLANES, A_WIDTH), lambda i: (jnp.maximum(i * t8 - 1, 0), c))
    next8 = lambda c: pl.BlockSpec((SUBLANES, A_WIDTH), lambda i: (jnp.minimum((i + 1) * t8, last8), c))
    return col, prev8, next8


GROUP_D = tuple(d for _, d in A_GROUPS)


def _even_mixer_fwd(x, proj, os_, ls_, conv_w, w_out, T):
    S = proj.shape[0]
    col, prev8, _ = _even_specs(T, S)
    H = SUBLANES

    def body(x_ref, w_ref, bg_r, cg_r, hb_r, zl_r, zh_r, cgp_r, hbp_r, o0, o1, o2, l0, l1, l2, cw_r,
             x1_ref, ut_ref, ext, cscr, *scr):
        i = pl.program_id(0)
        ls = [_compact_to_tokens(r, cscr, GROUP_D[g], T) for g, r in enumerate((l0, l1, l2))]
        expand = _head_expander()
        ws = [_segsum(w, expand) for w in _merge_weights(*ls)]
        oa = ws[0] * _to_tokens(o0, scr[0], GROUP_D[0], T)
        oa = oa + ws[1] * _to_tokens(o1, scr[1], GROUP_D[1], T)
        oa = oa + ws[2] * _to_tokens(o2, scr[2], GROUP_D[2], T)
        ext[0:H, :] = jnp.where(i == 0, 0.0, cgp_r[...] * hbp_r[...])
        ext[H:H + T, :] = cg_r[...] * hb_r[...]
        conv = cw_r[0:1, :] * ext[H - 2:H - 2 + T, :]
        for kk in range(1, SC_WIDTH):
            conv = conv + cw_r[kk:kk + 1, :] * ext[H - 2 + kk:H - 2 + kk + T, :]
        zl, zh = zl_r[...], zh_r[...]
        x1_ref[...] = x_ref[...] + _out_projection(ut_ref, w_ref, oa * (zl * _sig(zl)),
                                                   bg_r[...] * conv * (zh * _sig(zh)))

    streams = [_stream_spec(d, T) for d in GROUP_D]
    compacts = [_compact_spec(d, T) for d in GROUP_D]
    row = pl.BlockSpec((T, D_MODEL), lambda i: (i, 0))
    return _call(
        body, "even_mixer_fwd", (S // T,),
        [row, pl.BlockSpec((D_MODEL, D_MODEL), lambda i: (0, 0)),
         col(9), col(10), col(11), col(12), col(13), prev8(10), prev8(11)] + streams + compacts +
        [pl.BlockSpec((SC_WIDTH, A_WIDTH), lambda i: (0, 0))],
        [row, pl.BlockSpec((D_MODEL, T), lambda i: (0, i))],
        [jax.ShapeDtypeStruct((S, D_MODEL), F32), jax.ShapeDtypeStruct((D_MODEL, S), BF16)],
        scratch=[pltpu.VMEM((T + H, A_WIDTH), F32), pltpu.VMEM((T, LANES), F32)] + [_chunked(T)] * 3,
        sem=("parallel",))(
            x, w_out, proj, proj, proj, proj, proj, proj, proj, *os_, *ls_, conv_w)


def _even_mixer_bwd(dy, w_out, proj, os_, ls_, conv_w, T):
    S = proj.shape[0]
    nt = S // T
    col, prev8, next8 = _even_specs(T, S)
    H = SUBLANES
    t8 = T // SUBLANES
    last8 = S // SUBLANES - 1

    def body(dy_r, dyn_r, w_ref, bg_r, cg_r, hb_r, zl_r, zh_r, cgp_r, hbp_r, zhn_r, bgn_r,
             o0, o1, o2, l0, l1, l2, cw_r,
             do0, do1, do2, c0, c1, c2, dr_ref, dcw_ref, ext_t, ext_d, acc, cscr, s_a, s_b, s_c):
        i = pl.program_id(0)

        @pl.when(i == 0)
        def _():
            acc[...] = jnp.zeros_like(acc)

        zl, zh = zl_r[...], zh_r[...]
        sl, sh = _sig(zl), _sig(zh)
        du = lax.dot_general(dy_r[...].astype(BF16), w_ref[...], NT_DIMS, preferred_element_type=F32)
        dul, duh = du[:, 0:A_WIDTH], du[:, A_WIDTH:]
        dun = lax.dot_general(dyn_r[...].astype(BF16), w_ref[A_WIDTH:, :], NT_DIMS, preferred_element_type=F32)
        scr = (s_a, s_b, s_c)
        ls = [_compact_to_tokens(r, cscr, GROUP_D[g], T) for g, r in enumerate((l0, l1, l2))]
        wcs = _merge_weights(*ls)
        expand = _head_expander()
        ws = [_segsum(w, expand) for w in wcs]
        oa = ws[0] * _to_tokens(o0, scr[0], GROUP_D[0], T)
        oa = oa + ws[1] * _to_tokens(o1, scr[1], GROUP_D[1], T)
        oa = oa + ws[2] * _to_tokens(o2, scr[2], GROUP_D[2], T)
        doa = dul * (zl * sl)
        rsum = _segsum(doa * oa, _head_reducer())
        for g, (do_ref, c_ref) in enumerate(((do0, c0), (do1, c1), (do2, c2))):
            d = GROUP_D[g]
            _compact_from_tokens(c_ref, cscr, -wcs[g] * rsum, d, T)
            if d == 1:
                do_ref[0] = (ws[g] * doa).astype(BF16)
            else:
                _put(s_c, ws[g] * doa)
                _from_tokens(do_ref, s_c, d, T)
        cgv, hbv, bgv = cg_r[...], hb_r[...], bg_r[...]
        ext_t[0:H, :] = jnp.where(i == 0, 0.0, cgp_r[...] * hbp_r[...])
        ext_t[H:H + T, :] = cgv * hbv
        conv = cw_r[0:1, :] * ext_t[H - 2:H - 2 + T, :]
        for kk in range(1, SC_WIDTH):
            conv = conv + cw_r[kk:kk + 1, :] * ext_t[H - 2 + kk:H - 2 + kk + T, :]
        dyb = duh * (zh * sh)
        dconv = dyb * bgv
        zn = zhn_r[...]
        ext_d[0:T, :] = dconv
        ext_d[T:T + H, :] = jnp.where(i == nt - 1, 0.0, dun * (zn * _sig(zn)) * bgn_r[...])
        dt = cw_r[0:1, :] * ext_d[2:2 + T, :]
        for kk in range(1, SC_WIDTH):
            dt = dt + cw_r[kk:kk + 1, :] * ext_d[2 - kk:2 - kk + T, :]
        for kk in range(SC_WIDTH):
            acc[kk * SUBLANES:(kk + 1) * SUBLANES, :] += _cs8(dconv * ext_t[H - 2 + kk:H - 2 + kk + T, :])
        dr_ref[:, 0:A_WIDTH] = (dyb * conv).astype(BF16)
        dr_ref[:, A_WIDTH:2 * A_WIDTH] = (dt * hbv).astype(BF16)
        dr_ref[:, 2 * A_WIDTH:3 * A_WIDTH] = (dt * cgv).astype(BF16)
        dr_ref[:, 3 * A_WIDTH:4 * A_WIDTH] = (dul * oa * _dsilu(zl, sl)).astype(BF16)
        dr_ref[:, 4 * A_WIDTH:5 * A_WIDTH] = (duh * (bgv * conv) * _dsilu(zh, sh)).astype(BF16)

        @pl.when(i == nt - 1)
        def _():
            for kk in range(SC_WIDTH):
                dcw_ref[kk:kk + 1, :] = jnp.sum(acc[kk * SUBLANES:(kk + 1) * SUBLANES, :], axis=0, keepdims=True)

    streams = [_stream_spec(d, T) for d in GROUP_D]
    dynext = pl.BlockSpec((SUBLANES, D_MODEL), lambda i: (jnp.minimum((i + 1) * t8, last8), 0))
    compacts = [_compact_spec(d, T) for d in GROUP_D]
    outs = _call(
        body, "even_mixer_bwd", (nt,),
        [pl.BlockSpec((T, D_MODEL), lambda i: (i, 0)), dynext, pl.BlockSpec((D_MODEL, D_MODEL), lambda i: (0, 0)),
         col(9), col(10), col(11), col(12), col(13), prev8(10), prev8(11), next8(13), next8(9)] +
        streams + compacts + [pl.BlockSpec((SC_WIDTH, A_WIDTH), lambda i: (0, 0))],
        streams + compacts + [pl.BlockSpec((T, 5 * A_WIDTH), lambda i: (i, 0)),
                              pl.BlockSpec((SC_WIDTH, A_WIDTH), lambda i: (0, 0))],
        [_stream_shape(d, S, BF16) for d in GROUP_D] + [_compact_shape(d, S) for d in GROUP_D] +
        [jax.ShapeDtypeStruct((S, 5 * A_WIDTH), BF16), jax.ShapeDtypeStruct((SC_WIDTH, A_WIDTH), F32)],
        scratch=[pltpu.VMEM((T + H, A_WIDTH), F32), pltpu.VMEM((T + H, A_WIDTH), F32),
                 pltpu.VMEM((SC_WIDTH * SUBLANES, A_WIDTH), F32), pltpu.VMEM((T, LANES), F32)] +
                [_chunked(T)] * 3,
        sem=("arbitrary",))(dy, dy, w_out, proj, proj, proj, proj, proj, proj, proj, proj, proj, *os_, *ls_, conv_w)
    return outs[0:3], outs[3:6], outs[6], outs[7]


def _qk_bwd(proj, dqs, dks, dvs, pos, freq, wq, wk, T):
    S = proj.shape[0]
    nt = S // T
    qk_w = 3 * A_WIDTH

    def body(q_ref, k_ref, dq0, dq1, dq2, dk0, dk1, dk2, dv0, dv1, dv2, pos_ref, f_ref, wq_ref, wk_ref,
             o_ref, dw_ref, acc, scr):
        i = pl.program_id(0)

        @pl.when(i == 0)
        def _():
            acc[...] = jnp.zeros_like(acc)
            dw_ref[...] = jnp.zeros_like(dw_ref)

        seg_mean = _seg_mean()
        c, s1, s2 = _rope_tables(pos_ref, f_ref)
        for t, (src, w_ref, ds) in enumerate(((q_ref, wq_ref, (dq0, dq1, dq2)), (k_ref, wk_ref, (dk0, dk1, dk2)))):
            wv = w_ref[...]
            for g in range(3):
                d = GROUP_D[g]
                if d > 1:
                    _to_tokens(ds[g], scr, d, T)
                for ch in range(A_WIDTH // LANES):
                    cs = slice(g * A_WIDTH + ch * LANES, g * A_WIDTH + (ch + 1) * LANES)
                    lc = slice(ch * LANES, (ch + 1) * LANES)
                    v = src[:, cs]
                    dout = ds[g][0, :, lc].astype(F32) if d == 1 else scr[ch]
                    rs = lax.rsqrt(_segsum(v * v, seg_mean) + EPS)
                    xh = v * rs
                    dy = dout * c + pltpu.roll(dout * s1, LANES - 8, 1) + pltpu.roll(dout * s2, 8, 1)
                    acc[t * SUBLANES:(t + 1) * SUBLANES, :] += _cs8(dy * xh)
                    dxh = dy * wv
                    mean = _segsum(dxh * xh, seg_mean)
                    o_ref[:, t * qk_w + g * A_WIDTH + ch * LANES: t * qk_w + g * A_WIDTH + (ch + 1) * LANES] = (
                        rs * (dxh - xh * mean)).astype(BF16)
        for g, dv in enumerate((dv0, dv1, dv2)):
            d = GROUP_D[g]
            base = 2 * qk_w + g * A_WIDTH
            o_ref[:, base:base + A_WIDTH] = _to_tokens(dv, scr, d, T).astype(BF16)

        @pl.when(i == nt - 1)
        def _():
            for t in range(2):
                srow = jnp.sum(acc[t * SUBLANES:(t + 1) * SUBLANES, :], axis=0, keepdims=True)
                dw_ref[t:t + 1, :] = srow + pltpu.roll(srow, HEAD_DIM, 1)

    streams = [_stream_spec(d, T) for d in GROUP_D]
    return _call(
        body, "qk_bwd", (nt,),
        [pl.BlockSpec((T, qk_w), lambda i: (i, 0)), pl.BlockSpec((T, qk_w), lambda i: (i, 1))] + streams * 3 +
        [pl.BlockSpec((T, 1), lambda i: (i, 0)), pl.BlockSpec((1, LANES), lambda i: (0, 0)),
         pl.BlockSpec((1, LANES), lambda i: (0, 0)), pl.BlockSpec((1, LANES), lambda i: (0, 0))],
        [pl.BlockSpec((T, 3 * qk_w), lambda i: (i, 0)), pl.BlockSpec((SUBLANES, LANES), lambda i: (0, 0))],
        [jax.ShapeDtypeStruct((S, 3 * qk_w), BF16), jax.ShapeDtypeStruct((SUBLANES, LANES), F32)],
        scratch=[pltpu.VMEM((2 * SUBLANES, LANES), F32), _chunked(T)], sem=("arbitrary",))(
            proj, proj, *dqs, *dks, *dvs, pos, freq, wq, wk)


N_SMALL_ODD = 40
SHIFT_ROWS_LESS = SUBLANES


def _fill_shifted(ext_ref, sh_ref):
    rows = ext_ref.shape[0] - SHIFT_ROWS_LESS
    for b in range(1, SUBLANES):
        sh_ref[b - 1] = ext_ref[b:b + rows, :]


def _window(ext_ref, sh_ref, off, T):
    a, b = divmod(off, SUBLANES)
    if b == 0:
        return ext_ref[off:off + T, :]
    return sh_ref[b - 1, a * SUBLANES:a * SUBLANES + T, :]


def _odd_pool_tile(i, uc_r, ucp_r, pw_r, ext_u, pooled_s, pm_s, T):
    H = HALO
    uc = uc_r[...]
    ext_u[0:H, :] = jnp.where(i == 0, 0.0, ucp_r[...])
    ext_u[H:H + T, :] = uc
    row = i * T + lax.broadcasted_iota(jnp.int32, (T, 1), 0)
    for g, p in enumerate(POOL_SIZES):
        cs = slice(g * LANES, (g + 1) * LANES)
        win = ext_u[H:H + T, cs]
        for j in range(1, p):
            win = win + ext_u[H - j:H - j + T, cs]
        cnt = jnp.minimum(row + 1, p).astype(F32)
        pooled = win / cnt - uc[:, cs]
        pooled_s[:, cs] = pooled
        pm_s[:, cs] = jnp.dot(pooled.astype(BF16), pw_r[g].astype(BF16), preferred_element_type=F32)
    return row


def _odd_glu_tile(i, da_r, dg_r, dap_r, dgp_r, ext_g, sh_g, T):
    H = HALO
    ext_g[0:H, :] = jnp.where(i == 0, 0.0, dap_r[...] * _sig(dgp_r[...]))
    ext_g[H:H + T, :] = da_r[...] * _sig(dg_r[...])
    _fill_shifted(ext_g, sh_g)


def _odd_specs(T, S, order):
    tb = T // HALO
    col = lambda c: pl.BlockSpec((T, A_WIDTH), lambda s: (order(s), c))
    prev = lambda c: pl.BlockSpec((HALO, A_WIDTH), lambda s: (jnp.maximum(order(s) * tb - 1, 0), c))
    const2 = lambda shape: pl.BlockSpec(shape, lambda s: (0, 0))
    weights = [pl.BlockSpec((4, LANES, LANES), lambda s: (0, 0, 0)), const2((1, A_WIDTH)),
               const2((D_CONV, A_WIDTH)), const2((1, A_WIDTH)), const2((1, A_WIDTH)), const2((1, A_WIDTH))]
    return col, prev, weights


def _odd_mixer_fwd(x, tgt, proj, pool_w, scale, dconv_w, dconv_b, ln_w, ln_b, w_out, T):
    S = proj.shape[0]
    nt = S // T
    col, prev, wspecs = _odd_specs(T, S, lambda s: s)
    H = HALO

    def body(x_ref, t_ref, w_ref, uc_r, da_r, dg_r, zl_r, zh_r, ucp_r, dap_r, dgp_r, pw_r, sc_r, dw_r, db_r,
             lw_r, lb_r, dy_ref, l_ref, ut_ref, cv_ref, ext_u, ext_g, sh_g, pooled_s, pm_s, lacc):
        i = pl.program_id(0)

        @pl.when(i == 0)
        def _():
            lacc[...] = jnp.zeros_like(lacc)

        _odd_pool_tile(i, uc_r, ucp_r, pw_r, ext_u, pooled_s, pm_s, T)
        _odd_glu_tile(i, da_r, dg_r, dap_r, dgp_r, ext_g, sh_g, T)
        base = H - (D_CONV - 1)
        conv = db_r[...] + dw_r[0:1, :] * _window(ext_g, sh_g, base, T)
        for kk in range(1, D_CONV):
            conv = conv + dw_r[kk:kk + 1, :] * _window(ext_g, sh_g, base + kk, T)
        cv_ref[...] = conv
        mu = jnp.mean(conv, axis=-1, keepdims=True)
        xc = conv - mu
        yh = xc * lax.rsqrt(jnp.mean(xc * xc, axis=-1, keepdims=True) + EPS)
        ln = yh * lw_r[...] + lb_r[...]
        zl, zh = zl_r[...], zh_r[...]
        y = x_ref[...] + _out_projection(ut_ref, w_ref, pm_s[...] * sc_r[...] * (zl * _sig(zl)),
                                         ln * _sig(ln) * (zh * _sig(zh)))
        diff = y - t_ref[...]
        dy_ref[...] = diff / float(D_MODEL)
        lacc[...] += _cs8(diff * diff)

        @pl.when(i == nt - 1)
        def _():
            l_ref[...] = jnp.sum(lacc[...], axis=0, keepdims=True)

    row = pl.BlockSpec((T, D_MODEL), lambda i: (i, 0))
    return _call(
        body, "odd_mixer_fwd", (nt,),
        [row, row, pl.BlockSpec((D_MODEL, D_MODEL), lambda i: (0, 0)),
         col(0), col(1), col(2), col(3), col(4), prev(0), prev(1), prev(2)] + wspecs,
        [row, pl.BlockSpec((1, D_MODEL), lambda i: (0, 0)), pl.BlockSpec((D_MODEL, T), lambda i: (0, i)),
         pl.BlockSpec((T, A_WIDTH), lambda i: (i, 0))],
        [jax.ShapeDtypeStruct((S, D_MODEL), F32), jax.ShapeDtypeStruct((1, D_MODEL), F32),
         jax.ShapeDtypeStruct((D_MODEL, S), BF16), jax.ShapeDtypeStruct((S, A_WIDTH), F32)],
        scratch=[pltpu.VMEM((T + H, A_WIDTH), F32), pltpu.VMEM((T + H, A_WIDTH), F32),
                 pltpu.VMEM((SUBLANES - 1, T + H - SHIFT_ROWS_LESS, A_WIDTH), F32),
                 pltpu.VMEM((T, A_WIDTH), F32), pltpu.VMEM((T, A_WIDTH), F32),
                 pltpu.VMEM((SUBLANES, D_MODEL), F32)],
        sem=("arbitrary",))(x, tgt, w_out, proj, proj, proj, proj, proj, proj, proj, proj,
                            pool_w, scale, dconv_w, dconv_b, ln_w, ln_b)


def _odd_mixer_bwd(dy, w_out, proj, conv, pool_w, scale, dconv_w, dconv_b, ln_w, ln_b, T):
    S = proj.shape[0]
    nt = S // T
    order = lambda s: nt - 1 - s
    col, prev, wspecs = _odd_specs(T, S, order)
    H = HALO

    def body(dy_r, w_ref, cv_r, uc_r, da_r, dg_r, zl_r, zh_r, ucp_r, dap_r, dgp_r, pw_r, sc_r, dw_r, db_r, lw_r, lb_r,
             dp_ref, dpw_ref, sm_ref, ext_u, ext_g, sh_g, pooled_s, pm_s, dpl_s, ext_p, ext_c, sh_c, acc):
        step = pl.program_id(0)
        i = nt - 1 - step

        @pl.when(step == 0)
        def _():
            ext_p[T:T + H, :] = jnp.zeros((H, A_WIDTH), F32)
            ext_c[T:T + H, :] = jnp.zeros((H, A_WIDTH), F32)
            acc[...] = jnp.zeros_like(acc)
            dpw_ref[...] = jnp.zeros_like(dpw_ref)

        def accum(r, v):
            acc[r * SUBLANES:(r + 1) * SUBLANES, :] += _cs8(v)

        row = _odd_pool_tile(i, uc_r, ucp_r, pw_r, ext_u, pooled_s, pm_s, T)
        _odd_glu_tile(i, da_r, dg_r, dap_r, dgp_r, ext_g, sh_g, T)
        conv = cv_r[...]
        mu = jnp.mean(conv, axis=-1, keepdims=True)
        xc = conv - mu
        rstd = lax.rsqrt(jnp.mean(xc * xc, axis=-1, keepdims=True) + EPS)
        yh = xc * rstd
        ln = yh * lw_r[...] + lb_r[...]
        sln = _sig(ln)
        zl, zh = zl_r[...], zh_r[...]
        sl, sh = _sig(zl), _sig(zh)
        du = lax.dot_general(dy_r[...].astype(BF16), w_ref[...], NT_DIMS, preferred_element_type=F32)
        dul, duh = du[:, 0:A_WIDTH], du[:, A_WIDTH:]
        pm = pm_s[...]
        scv = sc_r[...]
        dyc = dul * (zl * sl)
        accum(34, dyc * pm)
        dpm = dyc * scv
        for g in range(len(POOL_SIZES)):
            cs = slice(g * LANES, (g + 1) * LANES)
            dpm_g = dpm[:, cs].astype(BF16)
            dpw_ref[g] += lax.dot_general(pooled_s[:, cs].astype(BF16), dpm_g, TN_DIMS, preferred_element_type=F32)
            dpl_s[:, cs] = lax.dot_general(dpm_g, pw_r[g].astype(BF16), NT_DIMS, preferred_element_type=F32)
        lane_p = lax.broadcasted_iota(jnp.int32, (1, A_WIDTH), 1) // LANES
        pvec = jnp.left_shift(2, lane_p)
        cnt = jnp.minimum(row + 1, pvec).astype(F32)
        dpl = dpl_s[...]
        ext_p[0:T, :] = dpl / cnt
        for g, p in enumerate(POOL_SIZES):
            cs = slice(g * LANES, (g + 1) * LANES)
            win = ext_p[0:T, cs]
            for j in range(1, p):
                win = win + ext_p[j:j + T, cs]
            dp_ref[:, cs] = (win - dpl[:, cs]).astype(BF16)
        ext_p[T:T + H, :] = ext_p[0:H, :]
        dln = duh * (zh * sh) * _dsilu(ln, sln)
        accum(32, dln * yh)
        accum(33, dln)
        dyh = dln * lw_r[...]
        dc = rstd * (dyh - jnp.mean(dyh, axis=-1, keepdims=True) - yh * jnp.mean(dyh * yh, axis=-1, keepdims=True))
        accum(31, dc)
        ext_c[0:T, :] = dc
        _fill_shifted(ext_c, sh_c)
        base = H - (D_CONV - 1)
        dgl = dw_r[0:1, :] * _window(ext_c, sh_c, D_CONV - 1, T)
        accum(0, dc * _window(ext_g, sh_g, base, T))
        for kk in range(1, D_CONV):
            dgl = dgl + dw_r[kk:kk + 1, :] * _window(ext_c, sh_c, D_CONV - 1 - kk, T)
            accum(kk, dc * _window(ext_g, sh_g, base + kk, T))
        ext_c[T:T + H, :] = ext_c[0:H, :]
        dav, dgv = da_r[...], dg_r[...]
        sg = _sig(dgv)
        dp_ref[:, A_WIDTH:2 * A_WIDTH] = (dgl * sg).astype(BF16)
        dp_ref[:, 2 * A_WIDTH:3 * A_WIDTH] = (dgl * dav * sg * (1.0 - sg)).astype(BF16)
        dp_ref[:, 3 * A_WIDTH:4 * A_WIDTH] = (dul * (pm * scv) * _dsilu(zl, sl)).astype(BF16)
        dp_ref[:, 4 * A_WIDTH:5 * A_WIDTH] = (duh * (ln * sln) * _dsilu(zh, sh)).astype(BF16)

        @pl.when(step == nt - 1)
        def _():
            for r in range(N_SMALL_ODD):
                sm_ref[r:r + 1, :] = jnp.sum(acc[r * SUBLANES:(r + 1) * SUBLANES, :], axis=0, keepdims=True)

    ext = pltpu.VMEM((T + H, A_WIDTH), F32)
    shifted = pltpu.VMEM((SUBLANES - 1, T + H - SHIFT_ROWS_LESS, A_WIDTH), F32)
    tile = pltpu.VMEM((T, A_WIDTH), F32)
    return _call(
        body, "odd_mixer_bwd", (nt,),
        [pl.BlockSpec((T, D_MODEL), lambda s: (order(s), 0)), pl.BlockSpec((D_MODEL, D_MODEL), lambda s: (0, 0)),
         pl.BlockSpec((T, A_WIDTH), lambda s: (order(s), 0)),
         col(0), col(1), col(2), col(3), col(4), prev(0), prev(1), prev(2)] + wspecs,
        [pl.BlockSpec((T, ODD_IN), lambda s: (order(s), 0)),
         pl.BlockSpec((4, LANES, LANES), lambda s: (0, 0, 0)),
         pl.BlockSpec((N_SMALL_ODD, A_WIDTH), lambda s: (0, 0))],
        [jax.ShapeDtypeStruct((S, ODD_IN), BF16), jax.ShapeDtypeStruct((4, LANES, LANES), F32),
         jax.ShapeDtypeStruct((N_SMALL_ODD, A_WIDTH), F32)],
        scratch=[ext, ext, shifted, tile, tile, tile, ext, ext, shifted,
                 pltpu.VMEM((N_SMALL_ODD * SUBLANES, A_WIDTH), F32)],
        sem=("arbitrary",))(dy, w_out, conv, proj, proj, proj, proj, proj, proj, proj, proj,
                            pool_w, scale, dconv_w, dconv_b, ln_w, ln_b)


TILE_SEQ = 256
TILE_WG = 256


LATE_WEIGHTS = ("e_w_out", "o_w_in", "o_w_out")
ODD_MATS = ("o_w_in", "o_w_out")
EVEN_MATS = ("e_w_in", "e_w_out")


def _reduce_start(names, grads, cidx):
    recv = _swap_to_sibling(names, [grads[n] for n in names], "swap_halves_" + names[0][0], True)
    both = [_add_half(cidx, grads[n], r, n) for n, r in zip(names, recv)]
    return [h for h, _ in both], [hb for _, hb in both]


def _local_step(x, pos, tgt, shards, p, cidx, bidx):
    T = TILE_SEQ
    freq = _freq_table()
    wq = jnp.tile(p["e_q_norm_w"], (1, LANES // HEAD_DIM))
    wk = jnp.tile(p["e_k_norm_w"], (1, LANES // HEAD_DIM))

    proj_e, ht_e, w_e_in, late = _inproj_gathering(x, p["e_norm_w"], shards["e_w_in"], bidx, "e_w_in", LATE_WEIGHTS,
                                                   [shards[n] for n in LATE_WEIGHTS], TILE_SEQ, "inproj_even")
    wb = dict(zip(LATE_WEIGHTS, late), e_w_in=w_e_in)
    qkv = _qkv_prep(proj_e, pos, freq, wq, wk, T)
    qs, ks, vs = qkv[0:3], qkv[3:6], qkv[6:9]
    os_, ls_ = [], []
    for g in range(3):
        o, l = _attn_fwd(qs[g], ks[g], vs[g], g)
        os_.append(o)
        ls_.append(l)
    x1, ut_e = _even_mixer_fwd(x, proj_e, os_, ls_, p["e_conv_w"], wb["e_w_out"], T)
    proj_o, ht_o = _inproj(x1, p["o_norm_w"], wb["o_w_in"], TILE_SEQ, 1280, "inproj_odd")
    odd_w = (p["o_pool_w"], p["o_pool_scale"], p["o_dconv_w"], p["o_dconv_b"], p["o_ln_w"], p["o_ln_b"])
    dy, lsum, ut_o, conv_o = _odd_mixer_fwd(x1, tgt, proj_o, *odd_w, wb["o_w_out"], T)

    g = {}
    g["o_w_out"] = _mm_wgrad(ut_o, [dy], TILE_WG, "wgrad_o_out")
    dproj_o, g["o_pool_w"], small_o = _odd_mixer_bwd(dy, wb["o_w_out"], proj_o, conv_o, *odd_w, T)
    g["o_w_in"] = _mm_wgrad(ht_o, [dproj_o], TILE_WG, "wgrad_o_in")
    half_o, half_o16 = _reduce_start(ODD_MATS, g, cidx)
    dx1, g["o_norm_w"], blocks_o = _mm_nt_rms([dproj_o], wb["o_w_in"], x1, p["o_norm_w"], dy, TILE_SEQ, "dx_odd",
                                              ODD_MATS, half_o16)
    g["o_dconv_w"] = small_o[0:D_CONV]
    g["o_dconv_b"] = small_o[31:32]
    g["o_ln_w"] = small_o[32:33]
    g["o_ln_b"] = small_o[33:34]
    g["o_pool_scale"] = small_o[34:35]

    g["e_w_out"] = _mm_wgrad(ut_e, [dx1], TILE_WG, "wgrad_e_out")
    dos, cgs, drest, g["e_conv_w"] = _even_mixer_bwd(dx1, wb["e_w_out"], proj_e, os_, ls_, p["e_conv_w"], T)
    dqs, dks, dvs = [], [], []
    for gi in range(3):
        dq, dk, dv = _attn_bwd(qs[gi], ks[gi], vs[gi], dos[gi], ls_[gi], cgs[gi], gi)
        dqs.append(dq)
        dks.append(dk)
        dvs.append(dv)
    dqkv, dnw = _qk_bwd(proj_e, dqs, dks, dvs, pos, freq, wq, wk, T)
    g["e_q_norm_w"] = dnw[0:1, 0:HEAD_DIM]
    g["e_k_norm_w"] = dnw[1:2, 0:HEAD_DIM]
    pieces = [dqkv, drest]
    g["e_w_in"] = _mm_wgrad(ht_e, pieces, TILE_WG, "wgrad_e_in")
    half_e, half_e16 = _reduce_start(EVEN_MATS, g, cidx)
    dx, g["e_norm_w"], blocks_e = _mm_nt_rms(pieces, wb["e_w_in"], x, p["e_norm_w"], dx1, TILE_SEQ, "dx_even",
                                             EVEN_MATS, half_e16)
    parts = {}
    for names, halves, blocks in ((ODD_MATS, half_o, blocks_o), (EVEN_MATS, half_e, blocks_e)):
        for n, h, r in zip(names, halves, blocks):
            parts[n] = _add_blocks(bidx, h, r, n)
    return lsum, dx, g, parts


BIG = ("e_w_in", "e_w_out", "o_w_in", "o_w_out")
SHARD_AXIS = {"e_w_in": 1, "e_w_out": 0, "o_w_in": 1, "o_w_out": 0}
N_CHIPS = 4


def _place():
    x, y, c = lax.axis_index("x"), lax.axis_index("y"), lax.axis_index("c")
    chips = [(1 - x, y), (x, 1 - y), (1 - x, 1 - y)]
    return x, y, c, chips


def _block_of(ref, name, block):
    rows, cols = ref.shape
    if SHARD_AXIS[name] == 1:
        cw = cols // N_CHIPS
        return ref.at[:, pl.ds(pl.multiple_of(block * cw, LANES), cw)]
    rw = rows // N_CHIPS
    return ref.at[pl.ds(pl.multiple_of(block * rw, rw), rw), :]


def _half_of(ref, name, half):
    rows, cols = ref.shape
    if SHARD_AXIS[name] == 1:
        return ref.at[pl.ds(pl.multiple_of(half * (rows // 2), rows // 2), rows // 2), :]
    return ref.at[:, pl.ds(pl.multiple_of(half * (cols // 2), LANES), cols // 2)]


def _sub(ref, name, block, half):
    rows, cols = ref.shape
    if SHARD_AXIS[name] == 1:
        cw, hr = cols // N_CHIPS, rows // 2
        return ref.at[pl.ds(pl.multiple_of(half * hr, hr), hr), pl.ds(pl.multiple_of(block * cw, LANES), cw)]
    rw, hc = rows // N_CHIPS, cols // 2
    return ref.at[pl.ds(pl.multiple_of(block * rw, rw), rw), pl.ds(pl.multiple_of(half * hc, LANES), hc)]


GATHER_COPIES = 7


class _Gather:
    def __init__(self, names, s_refs, f_refs, send, recv):
        self.names, self.s, self.f, self.send, self.recv = names, s_refs, f_refs, send, recv

    def _copy(self, k, src, dst, to):
        return pltpu.make_async_remote_copy(src_ref=src, dst_ref=dst, send_sem=self.send.at[k],
                                            recv_sem=self.recv.at[k], device_id=to, device_id_type=MESH)

    def _plan(self):
        x, y, c, chips = _place()
        me, sib = 2 * x + y, (x, y, 1 - c)
        first, relay_in, relay, last_in = [], [], [], []
        for wi, n in enumerate(self.names):
            k0 = wi * GATHER_COPIES
            s, f = self.s[wi], self.f[wi]
            own = _block_of(f, n, me)
            first.append(self._copy(k0 + 3, s, own, sib))
            last_in.append(self._copy(k0 + 3, s, own, sib))
            for j, (cx, cy) in enumerate(chips):
                first.append(self._copy(k0 + j, _half_of(s, n, c), _sub(f, n, me, c), (cx, cy, c)))
                mine = _sub(f, n, 2 * cx + cy, c)
                relay_in.append(self._copy(k0 + j, mine, mine, sib))
                relay.append(self._copy(k0 + 4 + j, mine, mine, sib))
                theirs = _sub(f, n, 2 * cx + cy, 1 - c)
                last_in.append(self._copy(k0 + 4 + j, theirs, theirs, sib))
        return first, relay_in, relay, last_in

    N_RELATIONS = 3

    def begin(self, relations=(0, 1, 2), sibling=True):
        first = self._plan()[0]
        for wi in range(len(self.names)):
            mine = first[wi * (1 + self.N_RELATIONS):(wi + 1) * (1 + self.N_RELATIONS)]
            if sibling:
                mine[0].start()
            for j in relations:
                mine[1 + j].start()

    def relay(self, relations=(0, 1, 2)):
        _, relay_in, relay, _ = self._plan()
        for wi in range(len(self.names)):
            for j in relations:
                relay_in[wi * self.N_RELATIONS + j].wait_recv()
                relay[wi * self.N_RELATIONS + j].start()

    def end(self):
        first, _, relay, last_in = self._plan()
        for cp in last_in:
            cp.wait_recv()
        for cp in first + relay:
            cp.wait_send()

    def wait_relayed(self, j):
        self._plan()[3][1 + j].wait_recv()

    def end_rest(self):
        first, _, relay, last_in = self._plan()
        last_in[0].wait_recv()
        for cp in first + relay:
            cp.wait_send()


def _full_shape(n, s):
    r, cdim = s.shape
    return jax.ShapeDtypeStruct((r, cdim * N_CHIPS) if SHARD_AXIS[n] == 1 else (r * N_CHIPS, cdim), s.dtype)


def _gather_sems(names):
    k = GATHER_COPIES * len(names)
    return [pltpu.SemaphoreType.DMA((k,)), pltpu.SemaphoreType.DMA((k,))]


def _scatter_copies(names, h_refs, r_refs, send, recv):
    _, _, c, chips = _place()
    cps = []
    for wi, n in enumerate(names):
        for j, (cx, cy) in enumerate(chips):
            cps.append(pltpu.make_async_remote_copy(
                src_ref=_block_of(h_refs[wi], n, 2 * cx + cy), dst_ref=r_refs[wi].at[j],
                send_sem=send.at[wi * 3 + j], recv_sem=recv.at[wi * 3 + j],
                device_id=(cx, cy, c), device_id_type=MESH))
    return cps


def _scatter_sems(names):
    return [pltpu.SemaphoreType.DMA((3 * len(names),)), pltpu.SemaphoreType.DMA((3 * len(names),))]


def _allreduce_small(part, name):
    R = part.shape[0]

    def body(p_ref, o_ref, sbuf, cbuf, send, recv):
        x, y, c, chips = _place()
        me = 2 * x + y
        sib = (x, y, 1 - c)
        sbuf[c] = p_ref[...]
        mine = sbuf.at[c]
        d2d = pltpu.make_async_remote_copy(src_ref=mine, dst_ref=mine, send_sem=send.at[0], recv_sem=recv.at[0],
                                           device_id=sib, device_id_type=MESH)
        d2d.start()
        theirs = sbuf.at[1 - c]
        pltpu.make_async_remote_copy(src_ref=theirs, dst_ref=theirs, send_sem=send.at[0], recv_sem=recv.at[0],
                                     device_id=sib, device_id_type=MESH).wait_recv()
        cbuf[me] = sbuf[0] + sbuf[1]
        blk = cbuf.at[me]
        sends = [d2d]
        for j, (cx, cy) in enumerate(chips):
            cp = pltpu.make_async_remote_copy(src_ref=blk, dst_ref=blk, send_sem=send.at[1 + j], recv_sem=recv.at[1 + j],
                                              device_id=(cx, cy, c), device_id_type=MESH)
            cp.start()
            sends.append(cp)
        for j, (cx, cy) in enumerate(chips):
            got = cbuf.at[2 * cx + cy]
            pltpu.make_async_remote_copy(src_ref=got, dst_ref=got, send_sem=send.at[1 + j], recv_sem=recv.at[1 + j],
                                         device_id=(cx, cy, c), device_id_type=MESH).wait_recv()
        o_ref[...] = (cbuf[0] + cbuf[1]) + (cbuf[2] + cbuf[3])
        for cp in sends:
            cp.wait_send()

    vm = pl.BlockSpec(memory_space=pltpu.VMEM)
    return pl.pallas_call(
        body, name=name, in_specs=[vm], out_specs=vm,
        out_shape=jax.ShapeDtypeStruct(part.shape, F32),
        scratch_shapes=[pltpu.VMEM((2, R, LANES), F32), pltpu.VMEM((N_CHIPS, R, LANES), F32),
                        pltpu.SemaphoreType.DMA((4,)), pltpu.SemaphoreType.DMA((4,))],
    )(part)


def _half_shape(shape, name):
    r, cdim = shape
    return (r // 2, cdim) if SHARD_AXIS[name] == 1 else (r, cdim // 2)


def _shard_shape(shape, name):
    r, cdim = shape
    return (r, cdim // N_CHIPS) if SHARD_AXIS[name] == 1 else (r // N_CHIPS, cdim)


def _swap_to_sibling(names, srcs, name, pick_half):
    nw = len(names)

    def body(*refs):
        g_refs, r_refs = refs[:nw], refs[nw:2 * nw]
        send, recv = refs[2 * nw:]
        x, y, c, _ = _place()
        sib = (x, y, 1 - c)
        cps = []
        for wi, n in enumerate(names):
            src = _half_of(g_refs[wi], n, 1 - c) if pick_half else g_refs[wi]
            cp = pltpu.make_async_remote_copy(src_ref=src, dst_ref=r_refs[wi], send_sem=send.at[wi],
                                              recv_sem=recv.at[wi], device_id=sib, device_id_type=MESH)
            cp.start()
            cps.append(cp)
        for cp in cps:
            cp.wait()

    outs = [jax.ShapeDtypeStruct(_half_shape(g.shape, n) if pick_half else g.shape, g.dtype)
            for n, g in zip(names, srcs)]
    return pl.pallas_call(
        body, name=name, in_specs=[ANY] * nw, out_specs=[ANY] * nw, out_shape=outs,
        scratch_shapes=[pltpu.SemaphoreType.DMA((nw,)), pltpu.SemaphoreType.DMA((nw,))],
    )(*srcs)


def _add_half(cidx, g, r, name):
    rows, cols = r.shape
    tr = 256
    tc = cols if cols <= 1792 else (1792 if cols % 1792 == 0 else 1280)
    nr, nc = rows // tr, cols // tc

    def body(c_ref, g_ref, r_ref, o_ref, ob_ref):
        s = g_ref[...] + r_ref[...]
        o_ref[...] = s
        ob_ref[...] = s.astype(BF16)

    if SHARD_AXIS[name] == 1:
        gmap = lambda i, j, c_ref: (c_ref[0] * nr + i, j)
    else:
        gmap = lambda i, j, c_ref: (i, c_ref[0] * nc + j)
    same = lambda i, j, c_ref: (i, j)
    return pl.pallas_call(
        body, name="add_half_" + name,
        grid_spec=pltpu.PrefetchScalarGridSpec(
            num_scalar_prefetch=1, grid=(nr, nc),
            in_specs=[pl.BlockSpec((tr, tc), gmap), pl.BlockSpec((tr, tc), same)],
            out_specs=[pl.BlockSpec((tr, tc), same), pl.BlockSpec((tr, tc), same)]),
        out_shape=[jax.ShapeDtypeStruct(r.shape, F32), jax.ShapeDtypeStruct(r.shape, BF16)],
        compiler_params=pltpu.CompilerParams(dimension_semantics=("parallel", "parallel"), vmem_limit_bytes=VMEM_LIMIT),
    )(cidx, g, r)


def _add_blocks(bidx, h, r, name):
    _, rows, cols = r.shape
    tr = min(rows, 256)
    nr = rows // tr

    def body(b_ref, h_ref, r0, r1, r2, o_ref):
        o_ref[...] = ((h_ref[...] + r0[0].astype(F32)) + r1[0].astype(F32)) + r2[0].astype(F32)

    if SHARD_AXIS[name] == 1:
        hmap = lambda i, b_ref: (i, b_ref[0])
    else:
        hmap = lambda i, b_ref: (b_ref[0] * nr + i, 0)
    rspec = lambda j: pl.BlockSpec((1, tr, cols), lambda i, b_ref, j=j: (j, i, 0))
    return pl.pallas_call(
        body, name="add_blocks_" + name,
        grid_spec=pltpu.PrefetchScalarGridSpec(
            num_scalar_prefetch=1, grid=(nr,),
            in_specs=[pl.BlockSpec((tr, cols), hmap), rspec(0), rspec(1), rspec(2)],
            out_specs=pl.BlockSpec((tr, cols), lambda i, b_ref: (i, 0))),
        out_shape=jax.ShapeDtypeStruct((rows, cols), F32),
        compiler_params=pltpu.CompilerParams(dimension_semantics=("parallel",), vmem_limit_bytes=VMEM_LIMIT),
    )(bidx, h, r, r, r)


def _adam_math(w, g, m, v):
    c1 = 1.0 - ADAM_B1 ** ADAM_STEP
    c2 = 1.0 - ADAM_B2 ** ADAM_STEP
    nm = ADAM_B1 * m + (1.0 - ADAM_B1) * g
    nv = ADAM_B2 * v + (1.0 - ADAM_B2) * (g * g)
    delta = -ADAM_LR * ((nm / c1) / (jnp.sqrt(nv / c2) + ADAM_EPS) + ADAM_WD * w)
    return delta, nm, nv


def _adamw(w, g, m, v, name):
    def body(w_ref, g_ref, m_ref, v_ref, d_ref, nm_ref, nv_ref):
        d_ref[...], nm_ref[...], nv_ref[...] = _adam_math(w_ref[...], g_ref[...], m_ref[...], v_ref[...])

    spec = pl.BlockSpec(w.shape, lambda i: (0, 0))
    return _call(body, "adamw_" + name, (1,), [spec] * 4, [spec] * 3,
                 [jax.ShapeDtypeStruct(w.shape, F32)] * 3, sem=("arbitrary",))(w, g, m, v)


def _adamw_halves(cidx, w, mine, theirs, m, v, name):
    rows, cols = w.shape
    hr, hc = mine.shape
    tr = 128
    if SHARD_AXIS[name] == 1:
        ni = hr // tr
        wmap = lambda hh, i, c_ref: (hh * ni + i, 0)
    else:
        ni = hr // tr
        wmap = lambda hh, i, c_ref: (i, hh)
    hmap = lambda hh, i, c_ref: (i, 0)

    def body(c_ref, w_ref, a_ref, b_ref, m_ref, v_ref, g_ref, d_ref, nm_ref, nv_ref):
        g = jnp.where(pl.program_id(0) == c_ref[0], a_ref[...], b_ref[...])
        g_ref[...] = g
        d_ref[...], nm_ref[...], nv_ref[...] = _adam_math(w_ref[...], g, m_ref[...], v_ref[...])

    wspec = pl.BlockSpec((tr, hc), wmap)
    hspec = pl.BlockSpec((tr, hc), hmap)
    return pl.pallas_call(
        body, name="adamw_" + name,
        grid_spec=pltpu.PrefetchScalarGridSpec(
            num_scalar_prefetch=1, grid=(2, ni),
            in_specs=[wspec, hspec, hspec, wspec, wspec], out_specs=[wspec] * 4),
        out_shape=[jax.ShapeDtypeStruct(w.shape, F32)] * 4,
        compiler_params=pltpu.CompilerParams(dimension_semantics=("parallel", "parallel"), vmem_limit_bytes=VMEM_LIMIT),
    )(cidx, w, mine, theirs, m, v)


SMALL = ("e_norm_w", "e_q_norm_w", "e_k_norm_w", "e_conv_w", "o_norm_w", "o_pool_w", "o_pool_scale",
         "o_dconv_w", "o_dconv_b", "o_ln_w", "o_ln_b")
SMALL_SHARDED = ("e_conv_w", "o_norm_w", "o_pool_scale", "o_dconv_w", "o_dconv_b", "o_ln_w", "o_ln_b")
WEIGHTS = ("e_norm_w", "e_w_in", "e_q_norm_w", "e_k_norm_w", "e_conv_w", "e_w_out", "o_norm_w", "o_w_in",
           "o_pool_w", "o_pool_scale", "o_dconv_w", "o_dconv_b", "o_ln_w", "o_ln_b", "o_w_out")


def _pack(arrs):
    flat = jnp.concatenate([a.reshape(-1) for a in arrs])
    rows = -(-flat.shape[0] // (LANES * SUBLANES)) * SUBLANES
    flat = jnp.pad(flat, (0, rows * LANES - flat.shape[0]))
    return flat.reshape(rows, LANES)


def _unpack(packed, shapes):
    flat = packed.reshape(-1)
    out, off = [], 0
    for s in shapes:
        n = int(np.prod(s))
        out.append(flat[off:off + n].reshape(s))
        off += n
    return out


def _gather_last(a, block, width):
    return lax.dynamic_slice_in_dim(a, block * width, width, axis=a.ndim - 1)


def kernel(x, positions, e_norm_w, e_w_in, e_q_norm_w, e_k_norm_w, e_conv_w, e_w_out, o_norm_w, o_w_in, o_pool_w, o_pool_scale, o_dconv_w, o_dconv_b, o_ln_w, o_ln_b, o_w_out, loss_target, m_e_norm_w, m_e_w_in, m_e_q_norm_w, m_e_k_norm_w, m_e_conv_w, m_e_w_out, m_o_norm_w, m_o_w_in, m_o_pool_w, m_o_pool_scale, m_o_dconv_w, m_o_dconv_b, m_o_ln_w, m_o_ln_b, m_o_w_out, v_e_norm_w, v_e_w_in, v_e_q_norm_w, v_e_k_norm_w, v_e_conv_w, v_e_w_out, v_o_norm_w, v_o_w_in, v_o_pool_w, v_o_pool_scale, v_o_dconv_w, v_o_dconv_b, v_o_ln_w, v_o_ln_b, v_o_w_out):
    given = dict(e_norm_w=e_norm_w, e_w_in=e_w_in, e_q_norm_w=e_q_norm_w, e_k_norm_w=e_k_norm_w, e_conv_w=e_conv_w,
                 e_w_out=e_w_out, o_norm_w=o_norm_w, o_w_in=o_w_in, o_pool_w=o_pool_w, o_pool_scale=o_pool_scale,
                 o_dconv_w=o_dconv_w, o_dconv_b=o_dconv_b, o_ln_w=o_ln_w, o_ln_b=o_ln_b, o_w_out=o_w_out)
    mom = dict(e_norm_w=m_e_norm_w, e_w_in=m_e_w_in, e_q_norm_w=m_e_q_norm_w, e_k_norm_w=m_e_k_norm_w,
               e_conv_w=m_e_conv_w, e_w_out=m_e_w_out, o_norm_w=m_o_norm_w, o_w_in=m_o_w_in, o_pool_w=m_o_pool_w,
               o_pool_scale=m_o_pool_scale, o_dconv_w=m_o_dconv_w, o_dconv_b=m_o_dconv_b, o_ln_w=m_o_ln_w,
               o_ln_b=m_o_ln_b, o_w_out=m_o_w_out)
    var = dict(e_norm_w=v_e_norm_w, e_w_in=v_e_w_in, e_q_norm_w=v_e_q_norm_w, e_k_norm_w=v_e_k_norm_w,
               e_conv_w=v_e_conv_w, e_w_out=v_e_w_out, o_norm_w=v_o_norm_w, o_w_in=v_o_w_in, o_pool_w=v_o_pool_w,
               o_pool_scale=v_o_pool_scale, o_dconv_w=v_o_dconv_w, o_dconv_b=v_o_dconv_b, o_ln_w=v_o_ln_w,
               o_ln_b=v_o_ln_b, o_w_out=v_o_w_out)
    S = x.shape[1]
    mx, my, mc = lax.axis_index("x"), lax.axis_index("y"), lax.axis_index("c")
    chip = 2 * mx + my
    cidx = jnp.reshape(mc, (1,)).astype(jnp.int32)
    bidx = jnp.reshape(chip, (1,)).astype(jnp.int32)

    shards = {n: given[n][0].astype(BF16) for n in BIG}
    shard_sizes = [int(np.prod(given[n].shape)) for n in SMALL_SHARDED]
    own = _pack([given[n] for n in SMALL_SHARDED])
    rows = own.shape[0]
    slots = jnp.zeros((N_CHIPS, rows, LANES), F32)
    own = jnp.where(mc == 0, own, 0.0)
    slots = lax.dynamic_update_slice(slots, own[None], (chip, 0, 0))
    gathered = _allreduce_small(slots.reshape(N_CHIPS * rows, LANES), "gather_small")
    gathered = gathered.reshape(N_CHIPS, rows * LANES)
    p = {}
    off = 0
    for n, size in zip(SMALL_SHARDED, shard_sizes):
        sh = given[n].shape[1:]
        parts = gathered[:, off:off + size].reshape((N_CHIPS,) + sh)
        fullp = jnp.moveaxis(parts, 0, -2).reshape(sh[:-1] + (N_CHIPS * sh[-1],))
        p[n] = fullp.reshape(-1, fullp.shape[-1])
        off += size
    p["e_norm_w"] = e_norm_w
    p["e_q_norm_w"] = e_q_norm_w
    p["e_k_norm_w"] = e_k_norm_w
    p["o_pool_w"] = o_pool_w[0]

    lsum, dx, g, parts = _local_step(x[0], positions.reshape(S, 1), loss_target[0], shards, p, cidx, bidx)
    loss = lax.psum(0.5 * jnp.sum(lsum) / float(D_MODEL), ("x", "y", "c"))

    tot = _unpack(_allreduce_small(_pack([g[n] for n in SMALL]), "allreduce_small"), [g[n].shape for n in SMALL])
    gsmall = dict(zip(SMALL, tot))
    grads = {}
    for n in SMALL:
        gv = gsmall[n]
        if n in SMALL_SHARDED:
            gv = _gather_last(gv, chip, gv.shape[-1] // N_CHIPS)
        grads[n] = gv.reshape(given[n].shape)

    theirs = _swap_to_sibling(BIG, [parts[n] for n in BIG], "swap_reduced", False)

    delta, new_m, new_v = {}, {}, {}
    for n, other in zip(BIG, theirs):
        sh = given[n].shape
        gs, d, nm, nv = _adamw_halves(cidx, given[n][0], parts[n], other, mom[n][0], var[n][0], n)
        grads[n], delta[n], new_m[n], new_v[n] = gs.reshape(sh), d.reshape(sh), nm.reshape(sh), nv.reshape(sh)
    shapes = [given[n].shape for n in SMALL]
    packed = [_pack([src[n] for n in SMALL]) for src in (given, grads, mom, var)]
    for dst, pk in zip((delta, new_m, new_v), _adamw(*packed, "small")):
        for n, a in zip(SMALL, _unpack(pk, shapes)):
            dst[n] = a
    return (loss, dx[None], *[grads[n] for n in WEIGHTS], *[delta[n] for n in WEIGHTS],
            *[new_m[n] for n in WEIGHTS], *[new_v[n] for n in WEIGHTS])
```

```python
import numpy as np
import jax
import jax.numpy as jnp
from jax import lax
from jax.experimental import pallas as pl
from jax.experimental.pallas import tpu as pltpu

F32 = jnp.float32
BF16 = jnp.bfloat16

D_MODEL = 1024
HEAD_DIM = 64
A_WIDTH = 512
A_HEADS = 8
A_GROUPS = ((128, 1), (512, 4), (2048, 16))
QBLK = 128
ROT_DIM = 16
ROPE_THETA = 500000.0
POOL_SIZES = (2, 4, 8, 16)
D_CONV = 31
SC_WIDTH = 3
EVEN_IN = 7168
ODD_IN = 2560
EPS = 1e-6
NEG = -1e30
ADAM_LR, ADAM_B1, ADAM_B2, ADAM_EPS, ADAM_WD, ADAM_STEP = 0.001, 0.9, 0.999, 1e-08, 0.01, 10

LANES = 128
SUBLANES = 8
HALO = 32
VMEM_LIMIT = 52 * 1024 * 1024
MESH = pl.DeviceIdType.MESH
ANY = pl.BlockSpec(memory_space=pl.ANY)

NT_DIMS = (((1,), (1,)), ((), ()))
TN_DIMS = (((0,), (0,)), ((), ()))


def _call(body, name, grid, in_specs, out_specs, out_shape, scratch=(), sem=None, aliases=None):
    return pl.pallas_call(
        body, name=name, grid=grid, in_specs=in_specs, out_specs=out_specs, out_shape=out_shape,
        scratch_shapes=list(scratch), input_output_aliases=aliases or {},
        compiler_params=pltpu.CompilerParams(dimension_semantics=sem, vmem_limit_bytes=VMEM_LIMIT))


def _sig(v):
    return jax.nn.sigmoid(v)


def _dsilu(v, s):
    return s * (1.0 + v * (1.0 - s))


def _out_projection(ut_ref, w_ref, lo, hi):
    acc = None
    for k, v in enumerate((lo, hi)):
        ut_ref[k * A_WIDTH:(k + 1) * A_WIDTH, :] = v.T.astype(BF16)
        part = jnp.dot(v.astype(BF16), w_ref[k * A_WIDTH:(k + 1) * A_WIDTH, :], preferred_element_type=F32)
        acc = part if acc is None else acc + part
    return acc


def _cs8(v):
    return v.reshape(v.shape[0] // SUBLANES, SUBLANES, v.shape[1]).sum(axis=0)


def _seg_mean():
    r = lax.broadcasted_iota(jnp.int32, (LANES, LANES), 0) // HEAD_DIM
    c = lax.broadcasted_iota(jnp.int32, (LANES, LANES), 1) // HEAD_DIM
    return jnp.where(r == c, 1.0 / HEAD_DIM, 0.0).astype(BF16)


def _segsum(v, ones):
    hi = v.astype(BF16)
    lo = (v - hi.astype(F32)).astype(BF16)
    return (jnp.dot(hi, ones, preferred_element_type=F32) + jnp.dot(lo, ones, preferred_element_type=F32))


def _head_rstd(v, seg_mean):
    return lax.rsqrt(jnp.dot((v * v).astype(BF16), seg_mean, preferred_element_type=F32) + EPS)


def _rope_tables(pos_ref, freq_ref):
    ang = pos_ref[...].astype(F32) * freq_ref[...]
    cosv, sinv = jnp.cos(ang), jnp.sin(ang)
    lm = lax.broadcasted_iota(jnp.int32, ang.shape, 1) % HEAD_DIM
    half = ROT_DIM // 2
    c = jnp.where(lm < ROT_DIM, cosv, 1.0)
    s1 = jnp.where((lm >= half) & (lm < ROT_DIM), sinv, 0.0)
    s2 = jnp.where(lm < half, -sinv, 0.0)
    return c, s1, s2


def _freq_table():
    half = ROT_DIM // 2
    inv = ROPE_THETA ** (-np.arange(half, dtype=np.float64) / half)
    lane = np.arange(LANES) % HEAD_DIM
    f = np.where(lane < ROT_DIM, inv[lane % half], 0.0)
    return jnp.asarray(f.reshape(1, LANES), F32)


def _load_once(hbm_ref, vmem_ref, sem):
    @pl.when(pl.program_id(0) == 0)
    def _():
        cp = pltpu.make_async_copy(hbm_ref, vmem_ref, sem)
        cp.start()
        cp.wait()


def _rms_rows(x_ref, nw_ref):
    xv = x_ref[...]
    ms = jnp.mean(xv * xv, axis=-1, keepdims=True)
    return xv * lax.rsqrt(ms + EPS) * nw_ref[...]


def _inproj(x, nw, w, tm, tn, name):
    S, N = x.shape[0], w.shape[1]

    def body(x_ref, nw_ref, w_hbm, o_ref, ht_ref, w_v, sem):
        _load_once(w_hbm, w_v, sem)
        h = _rms_rows(x_ref, nw_ref)
        ht_ref[...] = h.T.astype(BF16)
        hb = h.astype(BF16)
        for j in range(N // tn):
            o_ref[:, j * tn:(j + 1) * tn] = jnp.dot(hb, w_v[:, j * tn:(j + 1) * tn], preferred_element_type=F32)

    return _call(
        body, name, (S // tm,),
        [pl.BlockSpec((tm, D_MODEL), lambda i: (i, 0)),
         pl.BlockSpec((1, D_MODEL), lambda i: (0, 0)), ANY],
        [pl.BlockSpec((tm, N), lambda i: (i, 0)),
         pl.BlockSpec((D_MODEL, tm), lambda i: (0, i))],
        [jax.ShapeDtypeStruct((S, N), F32), jax.ShapeDtypeStruct((D_MODEL, S), BF16)],
        scratch=[pltpu.VMEM(w.shape, BF16), pltpu.SemaphoreType.DMA(())], sem=("arbitrary",))(x, nw, w)


def _inproj_gathering(x, nw, shard, bidx, first, late_names, late_shards, tm, name):
    S = x.shape[0]
    ni = S // tm
    K, cw = shard.shape
    nl = len(late_names)
    last = N_CHIPS - 1

    def body(b_ref, x_ref, nw_ref, s_hbm, *rest):
        ls_refs = rest[:nl]
        o_ref, ht_ref, f_hbm = rest[nl:nl + 3]
        lf_refs = rest[nl + 3:2 * nl + 3]
        hs, w_blk, lsem, send1, recv1, send2, recv2 = rest[2 * nl + 3:]
        j, i = pl.program_id(0), pl.program_id(1)
        g1 = _Gather((first,), (s_hbm,), (f_hbm,), send1, recv1)
        g2 = _Gather(late_names, ls_refs, lf_refs, send2, recv2)
        _, _, _, chips = _place()

        def load_block(src):
            cp = pltpu.make_async_copy(src, w_blk, lsem)
            cp.start()
            cp.wait()

        @pl.when((j == 0) & (i == 0))
        def _():
            g1.begin(relations=(0, 1))
            load_block(s_hbm)

        for r, (cx, cy) in enumerate(chips):
            @pl.when((j == r + 1) & (i == 0))
            def _(r=r, cx=cx, cy=cy):
                g1.wait_relayed(r)
                load_block(_block_of(f_hbm, first, 2 * cx + cy))

        @pl.when((j == 2) & (i == 0))
        def _():
            g1.relay(relations=(2,))
            g2.begin()

        pl.when((j == last) & (i == ni // 2))(g2.relay)

        rows = pl.ds(pl.multiple_of(i * tm, tm), tm)

        @pl.when(j == 0)
        def _():
            h = _rms_rows(x_ref, nw_ref)
            hs[rows, :] = h.astype(BF16)
            ht_ref[...] = h.T.astype(BF16)

        o_ref[...] = jnp.dot(hs[rows, :], w_blk[...], preferred_element_type=F32)

        @pl.when((j == 0) & (i == ni - 1))
        def _():
            g1.relay(relations=(0, 1))
            g1.begin(relations=(2,), sibling=False)

        @pl.when((j == last) & (i == ni - 1))
        def _():
            g1.end_rest()
            g2.end()

    def block_of_step(j, b_ref):
        return jnp.bitwise_xor(b_ref[0], jnp.bitwise_or(jnp.left_shift(jnp.bitwise_and(j, 1), 1), jnp.right_shift(j, 1)))

    outs = pl.pallas_call(
        body, name=name,
        grid_spec=pltpu.PrefetchScalarGridSpec(
            num_scalar_prefetch=1, grid=(N_CHIPS, ni),
            in_specs=[pl.BlockSpec((tm, D_MODEL), lambda j, i, b: (jnp.where(j == 0, i, 0), 0)),
                      pl.BlockSpec((1, D_MODEL), lambda j, i, b: (0, 0)), ANY] + [ANY] * nl,
            out_specs=[pl.BlockSpec((tm, cw), lambda j, i, b: (i, block_of_step(j, b))),
                       pl.BlockSpec((D_MODEL, tm), lambda j, i, b: (0, jnp.where(j == 0, i, ni - 1))),
                       ANY] + [ANY] * nl,
            scratch_shapes=[pltpu.VMEM((S, D_MODEL), BF16), pltpu.VMEM((K, cw), BF16), pltpu.SemaphoreType.DMA(())] +
            _gather_sems((first,)) + _gather_sems(late_names)),
        out_shape=[jax.ShapeDtypeStruct((S, cw * N_CHIPS), F32), jax.ShapeDtypeStruct((D_MODEL, S), BF16),
                   _full_shape(first, shard)] + [_full_shape(n, s) for n, s in zip(late_names, late_shards)],
        compiler_params=pltpu.CompilerParams(dimension_semantics=("arbitrary", "arbitrary"),
                                             vmem_limit_bytes=VMEM_LIMIT),
    )(bidx, x, nw, shard, *late_shards)
    return outs[0], outs[1], outs[2], list(outs[3:])


def _piece_blocks(pieces, tk, axis):
    starts, counts, s = [], [], 0
    for p in pieces:
        n = p.shape[axis] // tk
        starts.append(s)
        counts.append(n)
        s += n
    return starts, counts, s


def _mm_nt_rms(pieces, w, x, nw, dres, tm, name, scatter_names=(), scatter_halves=()):
    S = x.shape[0]
    npc = len(pieces)
    ni = S // tm
    ns = len(scatter_names)
    offs = np.cumsum([0] + [p.shape[1] for p in pieces]).tolist()

    def body(*refs):
        p_refs = refs[:npc]
        w_hbm, x_ref, nw_ref, dr_ref = refs[npc:npc + 4]
        h_refs = refs[npc + 4:npc + 4 + ns]
        dx_ref, dnw_ref = refs[npc + 4 + ns:npc + 6 + ns]
        r_refs = refs[npc + 6 + ns:npc + 6 + 2 * ns]
        w_v, sem, nacc = refs[npc + 6 + 2 * ns:npc + 9 + 2 * ns]
        i = pl.program_id(0)
        if ns:
            send, recv = refs[npc + 9 + 2 * ns:]

            @pl.when(i == 0)
            def _():
                for cp in _scatter_copies(scatter_names, h_refs, r_refs, send, recv):
                    cp.start()
        _load_once(w_hbm, w_v, sem)

        @pl.when(i == 0)
        def _():
            nacc[...] = jnp.zeros_like(nacc)

        dh = None
        for p in range(npc):
            part = lax.dot_general(p_refs[p][...].astype(BF16), w_v[:, offs[p]:offs[p + 1]], NT_DIMS,
                                   preferred_element_type=F32)
            dh = part if dh is None else dh + part
        xv = x_ref[...]
        rs = lax.rsqrt(jnp.mean(xv * xv, axis=-1, keepdims=True) + EPS)
        xh = xv * rs
        nacc[...] += _cs8(dh * xh)
        dxh = dh * nw_ref[...]
        dx_ref[...] = dr_ref[...] + rs * (dxh - xh * jnp.mean(dxh * xh, axis=-1, keepdims=True))

        @pl.when(i == ni - 1)
        def _():
            dnw_ref[...] = jnp.sum(nacc[...], axis=0, keepdims=True)
            if ns:
                for cp in _scatter_copies(scatter_names, h_refs, r_refs, send, recv):
                    cp.wait()

    row = pl.BlockSpec((tm, D_MODEL), lambda i: (i, 0))
    outs = _call(
        body, name, (ni,),
        [pl.BlockSpec((tm, p.shape[1]), lambda i: (i, 0)) for p in pieces] +
        [ANY, row, pl.BlockSpec((1, D_MODEL), lambda i: (0, 0)), row] + [ANY] * ns,
        [row, pl.BlockSpec((1, D_MODEL), lambda i: (0, 0))] + [ANY] * ns,
        [jax.ShapeDtypeStruct((S, D_MODEL), F32), jax.ShapeDtypeStruct((1, D_MODEL), F32)] +
        [jax.ShapeDtypeStruct((3,) + _shard_shape(h.shape, n), h.dtype) for n, h in zip(scatter_names, scatter_halves)],
        scratch=[pltpu.VMEM(w.shape, BF16), pltpu.SemaphoreType.DMA(()), pltpu.VMEM((SUBLANES, D_MODEL), F32)] +
        (_scatter_sems(scatter_names) if ns else []),
        sem=("arbitrary",))(*pieces, w, x, nw, dres, *scatter_halves)
    return outs[0], outs[1], list(outs[2:])


def _mm_wgrad(at, pieces, tn, name):
    M, S = at.shape
    starts, counts, nj = _piece_blocks(pieces, tn, 1)
    npc = len(pieces)

    def body(*refs):
        a_hbm = refs[0]
        p_refs = refs[1:1 + npc]
        o_ref, o16_ref, a_v, sem = refs[1 + npc:]
        j = pl.program_id(0)
        _load_once(a_hbm, a_v, sem)
        for p in range(npc):
            @pl.when((j >= starts[p]) & (j < starts[p] + counts[p]))
            def _(p=p):
                acc = jnp.dot(a_v[...], p_refs[p][...].astype(BF16), preferred_element_type=F32)
                o_ref[...] = acc
                o16_ref[...] = acc.astype(BF16)

    def pspec(p):
        return pl.BlockSpec((S, tn), lambda j: (0, jnp.clip(j - starts[p], 0, counts[p] - 1)))

    col = pl.BlockSpec((M, tn), lambda j: (0, j))
    return _call(
        body, name, (nj,),
        [ANY] + [pspec(p) for p in range(npc)], [col, col],
        [jax.ShapeDtypeStruct((M, nj * tn), F32), jax.ShapeDtypeStruct((M, nj * tn), BF16)],
        scratch=[pltpu.VMEM(at.shape, BF16), pltpu.SemaphoreType.DMA(())], sem=("arbitrary",))(at, *pieces)


def _stream_spec(d, T):
    return pl.BlockSpec((d, T // d, A_WIDTH), lambda i: (0, i, 0))


def _stream_shape(d, S, dtype):
    return jax.ShapeDtypeStruct((d, S // d, A_WIDTH), dtype)


N_CHUNK = A_WIDTH // LANES


def _to_tokens(ref, scr, d, T):
    if d == 1:
        return ref[0].astype(F32)
    for r in range(d):
        for ch in range(N_CHUNK):
            scr.at[ch][pl.ds(r, T // d, stride=d), :] = ref[r, :, ch * LANES:(ch + 1) * LANES].astype(F32)
    return _get(scr)


def _from_tokens(out_ref, scr, d, T):
    for r in range(d):
        for ch in range(N_CHUNK):
            out_ref[r, :, ch * LANES:(ch + 1) * LANES] = scr.at[ch][pl.ds(r, T // d, stride=d), :].astype(out_ref.dtype)


def _put(scr, val):
    for ch in range(N_CHUNK):
        scr[ch] = val[:, ch * LANES:(ch + 1) * LANES]


def _get(scr):
    return jnp.concatenate([scr[ch] for ch in range(N_CHUNK)], axis=1)


def _chunked(T):
    return pltpu.VMEM((N_CHUNK, T, LANES), F32)


def _compact_spec(d, T):
    return pl.BlockSpec((d, T // d, LANES), lambda i: (0, i, 0))


def _compact_shape(d, S):
    return jax.ShapeDtypeStruct((d, S // d, LANES), F32)


def _compact_to_tokens(ref, scr, d, T):
    if d == 1:
        return ref[0]
    for r in range(d):
        scr[pl.ds(r, T // d, stride=d), :] = ref[r]
    return scr[...]


def _compact_from_tokens(out_ref, scr, val, d, T):
    if d == 1:
        out_ref[0] = val
        return
    scr[...] = val
    for r in range(d):
        out_ref[r] = scr[pl.ds(r, T // d, stride=d), :]


def _head_expander():
    r = lax.broadcasted_iota(jnp.int32, (LANES, A_WIDTH), 0)
    c = lax.broadcasted_iota(jnp.int32, (LANES, A_WIDTH), 1) // HEAD_DIM
    return (r == c).astype(BF16)


def _head_reducer():
    r = lax.broadcasted_iota(jnp.int32, (A_WIDTH, LANES), 0) // HEAD_DIM
    c = lax.broadcasted_iota(jnp.int32, (A_WIDTH, LANES), 1)
    return (r == c).astype(BF16)


def _qkv_prep(proj, pos, freq, wq, wk, T):
    S = proj.shape[0]
    qk_w = 3 * A_WIDTH

    def body(q_ref, k_ref, v_ref, pos_ref, f_ref, wq_ref, wk_ref, *rest):
        outs, scr = rest[:9], rest[9]
        seg_mean = _seg_mean()
        c, s1, s2 = _rope_tables(pos_ref, f_ref)
        for t, (src, w_ref) in enumerate(((q_ref, wq_ref), (k_ref, wk_ref), (v_ref, None))):
            for g in range(3):
                d = A_GROUPS[g][1]
                out = outs[3 * t + g]
                for ch in range(A_WIDTH // LANES):
                    cs = slice(ch * LANES, (ch + 1) * LANES)
                    v = src[:, g * A_WIDTH + ch * LANES: g * A_WIDTH + (ch + 1) * LANES]
                    if w_ref is not None:
                        y = v * _head_rstd(v, seg_mean) * w_ref[...]
                        v = y * c + pltpu.roll(y, 8, 1) * s1 + pltpu.roll(y, LANES - 8, 1) * s2
                    if d == 1:
                        out[0, :, cs] = v.astype(BF16)
                    else:
                        scr[ch] = v
                if d > 1:
                    _from_tokens(out, scr, d, T)

    ds_ = [A_GROUPS[g][1] for g in range(3)] * 3
    return _call(
        body, "qkv_prep", (S // T,),
        [pl.BlockSpec((T, qk_w), lambda i: (i, 0)), pl.BlockSpec((T, qk_w), lambda i: (i, 1)),
         pl.BlockSpec((T, qk_w), lambda i: (i, 2)),
         pl.BlockSpec((T, 1), lambda i: (i, 0)), pl.BlockSpec((1, LANES), lambda i: (0, 0)),
         pl.BlockSpec((1, LANES), lambda i: (0, 0)), pl.BlockSpec((1, LANES), lambda i: (0, 0))],
        [_stream_spec(d, T) for d in ds_],
        [_stream_shape(d, S, BF16) for d in ds_],
        scratch=[_chunked(T)], sem=("parallel",))(proj, proj, proj, pos, freq, wq, wk)


def _attn_mask(i):
    qi = lax.broadcasted_iota(jnp.int32, (QBLK, 2 * QBLK), 0) + QBLK
    kj = lax.broadcasted_iota(jnp.int32, (QBLK, 2 * QBLK), 1)
    dist = qi - kj
    return (dist >= 0) & (dist <= QBLK) & ((i > 0) | (kj >= QBLK))


ATT_BLK = (None, QBLK, A_WIDTH)
ATT_CBLK = (None, QBLK, LANES)


def _first_head_lanes():
    return lax.broadcasted_iota(jnp.int32, (1, LANES), 1) < HEAD_DIM


def _split_heads(v, first):
    zero = jnp.zeros_like(v)
    return jnp.where(first, v, zero), jnp.where(first, zero, v)


def _attn_fwd(q, k, v, g):
    d, n, _ = q.shape
    nb = n // QBLK

    def body(q_ref, kp_ref, kc_ref, vp_ref, vc_ref, o_ref, l_ref, s_scr, p_scr):
        i = pl.program_id(1)
        mask = _attn_mask(i)
        first = _first_head_lanes()
        for pr in range(A_HEADS // 2):
            ps = slice(pr * LANES, (pr + 1) * LANES)
            kc = jnp.concatenate([kp_ref[:, ps], kc_ref[:, ps]], axis=0)
            for e, qh in enumerate(_split_heads(q_ref[:, ps], first)):
                s_scr[2 * pr + e] = lax.dot_general(qh, kc, NT_DIMS, preferred_element_type=F32)
        lane = lax.broadcasted_iota(jnp.int32, (1, LANES), 1)
        lrow = jnp.zeros((QBLK, LANES), F32)
        for h in range(A_HEADS):
            s = jnp.where(mask, s_scr[h] * (HEAD_DIM ** -0.5), NEG)
            m = jnp.max(s, axis=-1, keepdims=True)
            p = jnp.exp(s - m)
            den = jnp.sum(p, axis=-1, keepdims=True)
            p_scr[h] = (p / den).astype(BF16)
            lrow = jnp.where(lane == h, m + jnp.log(den), lrow)
        l_ref[...] = lrow
        for pr in range(A_HEADS // 2):
            ps = slice(pr * LANES, (pr + 1) * LANES)
            va, vb = _split_heads(jnp.concatenate([vp_ref[:, ps], vc_ref[:, ps]], axis=0), first)
            o_ref[:, ps] = (jnp.dot(p_scr[2 * pr], va, preferred_element_type=F32) +
                            jnp.dot(p_scr[2 * pr + 1], vb, preferred_element_type=F32)).astype(BF16)

    prev = lambda r, i: (r, jnp.maximum(i - 1, 0), 0)
    cur = lambda r, i: (r, i, 0)
    return _call(
        body, "attn_fwd_g%d" % g, (d, nb),
        [pl.BlockSpec(ATT_BLK, cur), pl.BlockSpec(ATT_BLK, prev), pl.BlockSpec(ATT_BLK, cur),
         pl.BlockSpec(ATT_BLK, prev), pl.BlockSpec(ATT_BLK, cur)],
        [pl.BlockSpec(ATT_BLK, cur), pl.BlockSpec(ATT_CBLK, cur)],
        [jax.ShapeDtypeStruct((d, n, A_WIDTH), BF16), jax.ShapeDtypeStruct((d, n, LANES), F32)],
        scratch=[pltpu.VMEM((A_HEADS, QBLK, 2 * QBLK), F32), pltpu.VMEM((A_HEADS, QBLK, 2 * QBLK), BF16)],
        sem=("parallel", "parallel"))(q, k, k, v, v)


def _attn_bwd(q, k, v, do, lse, cg, g):
    d, n, _ = q.shape
    nb = n // QBLK
    scale = HEAD_DIM ** -0.5

    def body(q_ref, kp_ref, kc_ref, vp_ref, vc_ref, do_ref, l_ref, c_ref, dq_ref, dk_ref, dv_ref, ck, cv,
             s_scr, dp_scr, p_scr, ds_scr):
        i = pl.program_id(1)

        @pl.when(i == 0)
        def _():
            ck[...] = jnp.zeros_like(ck)
            cv[...] = jnp.zeros_like(cv)

        @pl.when(i < nb)
        def _():
            mask = _attn_mask(i)
            first = _first_head_lanes()
            for pr in range(A_HEADS // 2):
                ps = slice(pr * LANES, (pr + 1) * LANES)
                kc = jnp.concatenate([kp_ref[:, ps], kc_ref[:, ps]], axis=0)
                vc = jnp.concatenate([vp_ref[:, ps], vc_ref[:, ps]], axis=0)
                qs = _split_heads(q_ref[:, ps], first)
                dos = _split_heads(do_ref[:, ps], first)
                for e in range(2):
                    s_scr[2 * pr + e] = lax.dot_general(qs[e], kc, NT_DIMS, preferred_element_type=F32)
                    dp_scr[2 * pr + e] = lax.dot_general(dos[e], vc, NT_DIMS, preferred_element_type=F32)
            for h in range(A_HEADS):
                p = jnp.where(mask, jnp.exp(s_scr[h] * scale - l_ref[:, h:h + 1]), 0.0)
                p_scr[h] = p.astype(BF16)
                ds_scr[h] = (p * (dp_scr[h] + c_ref[:, h:h + 1]) * scale).astype(BF16)
            for pr in range(A_HEADS // 2):
                ps = slice(pr * LANES, (pr + 1) * LANES)
                ks = _split_heads(jnp.concatenate([kp_ref[:, ps], kc_ref[:, ps]], axis=0), first)
                qs = _split_heads(q_ref[:, ps], first)
                dos = _split_heads(do_ref[:, ps], first)
                dq = dkc = dvc = None
                for e in range(2):
                    ds = ds_scr[2 * pr + e]
                    a = jnp.dot(ds, ks[e], preferred_element_type=F32)
                    b = lax.dot_general(ds, qs[e], TN_DIMS, preferred_element_type=F32)
                    c = lax.dot_general(p_scr[2 * pr + e], dos[e], TN_DIMS, preferred_element_type=F32)
                    dq, dkc, dvc = (a, b, c) if e == 0 else (dq + a, dkc + b, dvc + c)
                dq_ref[:, ps] = dq.astype(BF16)
                dk_ref[:, ps] = (ck[:, ps] + dkc[:QBLK]).astype(BF16)
                dv_ref[:, ps] = (cv[:, ps] + dvc[:QBLK]).astype(BF16)
                ck[:, ps] = dkc[QBLK:]
                cv[:, ps] = dvc[QBLK:]

        @pl.when(i == nb)
        def _():
            dk_ref[...] = ck[...].astype(BF16)
            dv_ref[...] = cv[...].astype(BF16)

    qi = lambda i: jnp.minimum(i, nb - 1)
    cur = lambda r, i: (r, qi(i), 0)
    prev = lambda r, i: (r, jnp.maximum(qi(i) - 1, 0), 0)
    late = lambda r, i: (r, jnp.maximum(i - 1, 0), 0)
    return _call(
        body, "attn_bwd_g%d" % g, (d, nb + 1),
        [pl.BlockSpec(ATT_BLK, cur), pl.BlockSpec(ATT_BLK, prev), pl.BlockSpec(ATT_BLK, cur),
         pl.BlockSpec(ATT_BLK, prev), pl.BlockSpec(ATT_BLK, cur),
         pl.BlockSpec(ATT_BLK, cur), pl.BlockSpec(ATT_CBLK, cur), pl.BlockSpec(ATT_CBLK, cur)],
        [pl.BlockSpec(ATT_BLK, cur), pl.BlockSpec(ATT_BLK, late), pl.BlockSpec(ATT_BLK, late)],
        [jax.ShapeDtypeStruct((d, n, A_WIDTH), BF16)] * 3,
        scratch=[pltpu.VMEM((QBLK, A_WIDTH), F32), pltpu.VMEM((QBLK, A_WIDTH), F32),
                 pltpu.VMEM((A_HEADS, QBLK, 2 * QBLK), F32), pltpu.VMEM((A_HEADS, QBLK, 2 * QBLK), F32),
                 pltpu.VMEM((A_HEADS, QBLK, 2 * QBLK), BF16), pltpu.VMEM((A_HEADS, QBLK, 2 * QBLK), BF16)],
        sem=("parallel", "arbitrary"))(q, k, k, v, v, do, lse, cg)


def _merge_weights(l0, l1, l2):
    mx = jnp.maximum(jnp.maximum(l0, l1), l2)
    e0, e1, e2 = jnp.exp(l0 - mx), jnp.exp(l1 - mx), jnp.exp(l2 - mx)
    den = e0 + e1 + e2
    return e0 / den, e1 / den, e2 / den


def _even_specs(T, S):
    t8 = T // SUBLANES
    last8 = S // SUBLANES - 1
    col = lambda c: pl.BlockSpec((T, A_WIDTH), lambda i: (i, c))
    prev8 = lambda c: pl.BlockSpec((SUBLANES, A_WIDTH), lambda i: (jnp.maximum(i * t8 - 1, 0), c))
    next8 = lambda c: pl.BlockSpec((SUBLANES, A_WIDTH), lambda i: (jnp.minimum((i + 1) * t8, last8), c))
    return col, prev8, next8


GROUP_D = tuple(d for _, d in A_GROUPS)


def _even_mixer_fwd(x, proj, os_, ls_, conv_w, w_out, T):
    S = proj.shape[0]
    col, prev8, _ = _even_specs(T, S)
    H = SUBLANES

    def body(x_ref, w_ref, bg_r, cg_r, hb_r, zl_r, zh_r, cgp_r, hbp_r, o0, o1, o2, l0, l1, l2, cw_r,
             x1_ref, ut_ref, ext, cscr, *scr):
        i = pl.program_id(0)
        ls = [_compact_to_tokens(r, cscr, GROUP_D[g], T) for g, r in enumerate((l0, l1, l2))]
        expand = _head_expander()
        ws = [_segsum(w, expand) for w in _merge_weights(*ls)]
        oa = ws[0] * _to_tokens(o0, scr[0], GROUP_D[0], T)
        oa = oa + ws[1] * _to_tokens(o1, scr[1], GROUP_D[1], T)
        oa = oa + ws[2] * _to_tokens(o2, scr[2], GROUP_D[2], T)
        ext[0:H, :] = jnp.where(i == 0, 0.0, cgp_r[...] * hbp_r[...])
        ext[H:H + T, :] = cg_r[...] * hb_r[...]
        conv = cw_r[0:1, :] * ext[H - 2:H - 2 + T, :]
        for kk in range(1, SC_WIDTH):
            conv = conv + cw_r[kk:kk + 1, :] * ext[H - 2 + kk:H - 2 + kk + T, :]
        zl, zh = zl_r[...], zh_r[...]
        x1_ref[...] = x_ref[...] + _out_projection(ut_ref, w_ref, oa * (zl * _sig(zl)),
                                                   bg_r[...] * conv * (zh * _sig(zh)))

    streams = [_stream_spec(d, T) for d in GROUP_D]
    compacts = [_compact_spec(d, T) for d in GROUP_D]
    row = pl.BlockSpec((T, D_MODEL), lambda i: (i, 0))
    return _call(
        body, "even_mixer_fwd", (S // T,),
        [row, pl.BlockSpec((D_MODEL, D_MODEL), lambda i: (0, 0)),
         col(9), col(10), col(11), col(12), col(13), prev8(10), prev8(11)] + streams + compacts +
        [pl.BlockSpec((SC_WIDTH, A_WIDTH), lambda i: (0, 0))],
        [row, pl.BlockSpec((D_MODEL, T), lambda i: (0, i))],
        [jax.ShapeDtypeStruct((S, D_MODEL), F32), jax.ShapeDtypeStruct((D_MODEL, S), BF16)],
        scratch=[pltpu.VMEM((T + H, A_WIDTH), F32), pltpu.VMEM((T, LANES), F32)] + [_chunked(T)] * 3,
        sem=("parallel",))(
            x, w_out, proj, proj, proj, proj, proj, proj, proj, *os_, *ls_, conv_w)


def _even_mixer_bwd(dy, w_out, proj, os_, ls_, conv_w, T):
    S = proj.shape[0]
    nt = S // T
    col, prev8, next8 = _even_specs(T, S)
    H = SUBLANES
    t8 = T // SUBLANES
    last8 = S // SUBLANES - 1

    def body(dy_r, dyn_r, w_ref, bg_r, cg_r, hb_r, zl_r, zh_r, cgp_r, hbp_r, zhn_r, bgn_r,
             o0, o1, o2, l0, l1, l2, cw_r,
             do0, do1, do2, c0, c1, c2, dr_ref, dcw_ref, ext_t, ext_d, acc, cscr, s_a, s_b, s_c):
        i = pl.program_id(0)

        @pl.when(i == 0)
        def _():
            acc[...] = jnp.zeros_like(acc)

        zl, zh = zl_r[...], zh_r[...]
        sl, sh = _sig(zl), _sig(zh)
        du = lax.dot_general(dy_r[...].astype(BF16), w_ref[...], NT_DIMS, preferred_element_type=F32)
        dul, duh = du[:, 0:A_WIDTH], du[:, A_WIDTH:]
        dun = lax.dot_general(dyn_r[...].astype(BF16), w_ref[A_WIDTH:, :], NT_DIMS, preferred_element_type=F32)
        scr = (s_a, s_b, s_c)
        ls = [_compact_to_tokens(r, cscr, GROUP_D[g], T) for g, r in enumerate((l0, l1, l2))]
        wcs = _merge_weights(*ls)
        expand = _head_expander()
        ws = [_segsum(w, expand) for w in wcs]
        oa = ws[0] * _to_tokens(o0, scr[0], GROUP_D[0], T)
        oa = oa + ws[1] * _to_tokens(o1, scr[1], GROUP_D[1], T)
        oa = oa + ws[2] * _to_tokens(o2, scr[2], GROUP_D[2], T)
        doa = dul * (zl * sl)
        rsum = _segsum(doa * oa, _head_reducer())
        for g, (do_ref, c_ref) in enumerate(((do0, c0), (do1, c1), (do2, c2))):
            d = GROUP_D[g]
            _compact_from_tokens(c_ref, cscr, -wcs[g] * rsum, d, T)
            if d == 1:
                do_ref[0] = (ws[g] * doa).astype(BF16)
            else:
                _put(s_c, ws[g] * doa)
                _from_tokens(do_ref, s_c, d, T)
        cgv, hbv, bgv = cg_r[...], hb_r[...], bg_r[...]
        ext_t[0:H, :] = jnp.where(i == 0, 0.0, cgp_r[...] * hbp_r[...])
        ext_t[H:H + T, :] = cgv * hbv
        conv = cw_r[0:1, :] * ext_t[H - 2:H - 2 + T, :]
        for kk in range(1, SC_WIDTH):
            conv = conv + cw_r[kk:kk + 1, :] * ext_t[H - 2 + kk:H - 2 + kk + T, :]
        dyb = duh * (zh * sh)
        dconv = dyb * bgv
        zn = zhn_r[...]
        ext_d[0:T, :] = dconv
        ext_d[T:T + H, :] = jnp.where(i == nt - 1, 0.0, dun * (zn * _sig(zn)) * bgn_r[...])
        dt = cw_r[0:1, :] * ext_d[2:2 + T, :]
        for kk in range(1, SC_WIDTH):
            dt = dt + cw_r[kk:kk + 1, :] * ext_d[2 - kk:2 - kk + T, :]
        for kk in range(SC_WIDTH):
            acc[kk * SUBLANES:(kk + 1) * SUBLANES, :] += _cs8(dconv * ext_t[H - 2 + kk:H - 2 + kk + T, :])
        dr_ref[:, 0:A_WIDTH] = (dyb * conv).astype(BF16)
        dr_ref[:, A_WIDTH:2 * A_WIDTH] = (dt * hbv).astype(BF16)
        dr_ref[:, 2 * A_WIDTH:3 * A_WIDTH] = (dt * cgv).astype(BF16)
        dr_ref[:, 3 * A_WIDTH:4 * A_WIDTH] = (dul * oa * _dsilu(zl, sl)).astype(BF16)
        dr_ref[:, 4 * A_WIDTH:5 * A_WIDTH] = (duh * (bgv * conv) * _dsilu(zh, sh)).astype(BF16)

        @pl.when(i == nt - 1)
        def _():
            for kk in range(SC_WIDTH):
                dcw_ref[kk:kk + 1, :] = jnp.sum(acc[kk * SUBLANES:(kk + 1) * SUBLANES, :], axis=0, keepdims=True)

    streams = [_stream_spec(d, T) for d in GROUP_D]
    dynext = pl.BlockSpec((SUBLANES, D_MODEL), lambda i: (jnp.minimum((i + 1) * t8, last8), 0))
    compacts = [_compact_spec(d, T) for d in GROUP_D]
    outs = _call(
        body, "even_mixer_bwd", (nt,),
        [pl.BlockSpec((T, D_MODEL), lambda i: (i, 0)), dynext, pl.BlockSpec((D_MODEL, D_MODEL), lambda i: (0, 0)),
         col(9), col(10), col(11), col(12), col(13), prev8(10), prev8(11), next8(13), next8(9)] +
        streams + compacts + [pl.BlockSpec((SC_WIDTH, A_WIDTH), lambda i: (0, 0))],
        streams + compacts + [pl.BlockSpec((T, 5 * A_WIDTH), lambda i: (i, 0)),
                              pl.BlockSpec((SC_WIDTH, A_WIDTH), lambda i: (0, 0))],
        [_stream_shape(d, S, BF16) for d in GROUP_D] + [_compact_shape(d, S) for d in GROUP_D] +
        [jax.ShapeDtypeStruct((S, 5 * A_WIDTH), BF16), jax.ShapeDtypeStruct((SC_WIDTH, A_WIDTH), F32)],
        scratch=[pltpu.VMEM((T + H, A_WIDTH), F32), pltpu.VMEM((T + H, A_WIDTH), F32),
                 pltpu.VMEM((SC_WIDTH * SUBLANES, A_WIDTH), F32), pltpu.VMEM((T, LANES), F32)] +
                [_chunked(T)] * 3,
        sem=("arbitrary",))(dy, dy, w_out, proj, proj, proj, proj, proj, proj, proj, proj, proj, *os_, *ls_, conv_w)
    return outs[0:3], outs[3:6], outs[6], outs[7]


def _qk_bwd(proj, dqs, dks, dvs, pos, freq, wq, wk, T):
    S = proj.shape[0]
    nt = S // T
    qk_w = 3 * A_WIDTH

    def body(q_ref, k_ref, dq0, dq1, dq2, dk0, dk1, dk2, dv0, dv1, dv2, pos_ref, f_ref, wq_ref, wk_ref,
             o_ref, dw_ref, acc, scr):
        i = pl.program_id(0)

        @pl.when(i == 0)
        def _():
            acc[...] = jnp.zeros_like(acc)
            dw_ref[...] = jnp.zeros_like(dw_ref)

        seg_mean = _seg_mean()
        c, s1, s2 = _rope_tables(pos_ref, f_ref)
        for t, (src, w_ref, ds) in enumerate(((q_ref, wq_ref, (dq0, dq1, dq2)), (k_ref, wk_ref, (dk0, dk1, dk2)))):
            wv = w_ref[...]
            for g in range(3):
                d = GROUP_D[g]
                if d > 1:
                    _to_tokens(ds[g], scr, d, T)
                for ch in range(A_WIDTH // LANES):
                    cs = slice(g * A_WIDTH + ch * LANES, g * A_WIDTH + (ch + 1) * LANES)
                    lc = slice(ch * LANES, (ch + 1) * LANES)
                    v = src[:, cs]
                    dout = ds[g][0, :, lc].astype(F32) if d == 1 else scr[ch]
                    rs = lax.rsqrt(_segsum(v * v, seg_mean) + EPS)
                    xh = v * rs
                    dy = dout * c + pltpu.roll(dout * s1, LANES - 8, 1) + pltpu.roll(dout * s2, 8, 1)
                    acc[t * SUBLANES:(t + 1) * SUBLANES, :] += _cs8(dy * xh)
                    dxh = dy * wv
                    mean = _segsum(dxh * xh, seg_mean)
                    o_ref[:, t * qk_w + g * A_WIDTH + ch * LANES: t * qk_w + g * A_WIDTH + (ch + 1) * LANES] = (
                        rs * (dxh - xh * mean)).astype(BF16)
        for g, dv in enumerate((dv0, dv1, dv2)):
            d = GROUP_D[g]
            base = 2 * qk_w + g * A_WIDTH
            o_ref[:, base:base + A_WIDTH] = _to_tokens(dv, scr, d, T).astype(BF16)

        @pl.when(i == nt - 1)
        def _():
            for t in range(2):
                srow = jnp.sum(acc[t * SUBLANES:(t + 1) * SUBLANES, :], axis=0, keepdims=True)
                dw_ref[t:t + 1, :] = srow + pltpu.roll(srow, HEAD_DIM, 1)

    streams = [_stream_spec(d, T) for d in GROUP_D]
    return _call(
        body, "qk_bwd", (nt,),
        [pl.BlockSpec((T, qk_w), lambda i: (i, 0)), pl.BlockSpec((T, qk_w), lambda i: (i, 1))] + streams * 3 +
        [pl.BlockSpec((T, 1), lambda i: (i, 0)), pl.BlockSpec((1, LANES), lambda i: (0, 0)),
         pl.BlockSpec((1, LANES), lambda i: (0, 0)), pl.BlockSpec((1, LANES), lambda i: (0, 0))],
        [pl.BlockSpec((T, 3 * qk_w), lambda i: (i, 0)), pl.BlockSpec((SUBLANES, LANES), lambda i: (0, 0))],
        [jax.ShapeDtypeStruct((S, 3 * qk_w), BF16), jax.ShapeDtypeStruct((SUBLANES, LANES), F32)],
        scratch=[pltpu.VMEM((2 * SUBLANES, LANES), F32), _chunked(T)], sem=("arbitrary",))(
            proj, proj, *dqs, *dks, *dvs, pos, freq, wq, wk)


N_SMALL_ODD = 40
SHIFT_ROWS_LESS = SUBLANES


def _fill_shifted(ext_ref, sh_ref):
    rows = ext_ref.shape[0] - SHIFT_ROWS_LESS
    for b in range(1, SUBLANES):
        sh_ref[b - 1] = ext_ref[b:b + rows, :]


def _window(ext_ref, sh_ref, off, T):
    a, b = divmod(off, SUBLANES)
    if b == 0:
        return ext_ref[off:off + T, :]
    return sh_ref[b - 1, a * SUBLANES:a * SUBLANES + T, :]


def _odd_pool_tile(i, uc_r, ucp_r, pw_r, ext_u, pooled_s, pm_s, T):
    H = HALO
    uc = uc_r[...]
    ext_u[0:H, :] = jnp.where(i == 0, 0.0, ucp_r[...])
    ext_u[H:H + T, :] = uc
    row = i * T + lax.broadcasted_iota(jnp.int32, (T, 1), 0)
    for g, p in enumerate(POOL_SIZES):
        cs = slice(g * LANES, (g + 1) * LANES)
        win = ext_u[H:H + T, cs]
        for j in range(1, p):
            win = win + ext_u[H - j:H - j + T, cs]
        cnt = jnp.minimum(row + 1, p).astype(F32)
        pooled = win / cnt - uc[:, cs]
        pooled_s[:, cs] = pooled
        pm_s[:, cs] = jnp.dot(pooled.astype(BF16), pw_r[g].astype(BF16), preferred_element_type=F32)
    return row


def _odd_glu_tile(i, da_r, dg_r, dap_r, dgp_r, ext_g, sh_g, T):
    H = HALO
    ext_g[0:H, :] = jnp.where(i == 0, 0.0, dap_r[...] * _sig(dgp_r[...]))
    ext_g[H:H + T, :] = da_r[...] * _sig(dg_r[...])
    _fill_shifted(ext_g, sh_g)


def _odd_specs(T, S, order):
    tb = T // HALO
    col = lambda c: pl.BlockSpec((T, A_WIDTH), lambda s: (order(s), c))
    prev = lambda c: pl.BlockSpec((HALO, A_WIDTH), lambda s: (jnp.maximum(order(s) * tb - 1, 0), c))
    const2 = lambda shape: pl.BlockSpec(shape, lambda s: (0, 0))
    weights = [pl.BlockSpec((4, LANES, LANES), lambda s: (0, 0, 0)), const2((1, A_WIDTH)),
               const2((D_CONV, A_WIDTH)), const2((1, A_WIDTH)), const2((1, A_WIDTH)), const2((1, A_WIDTH))]
    return col, prev, weights


def _odd_mixer_fwd(x, tgt, proj, pool_w, scale, dconv_w, dconv_b, ln_w, ln_b, w_out, T):
    S = proj.shape[0]
    nt = S // T
    col, prev, wspecs = _odd_specs(T, S, lambda s: s)
    H = HALO

    def body(x_ref, t_ref, w_ref, uc_r, da_r, dg_r, zl_r, zh_r, ucp_r, dap_r, dgp_r, pw_r, sc_r, dw_r, db_r,
             lw_r, lb_r, dy_ref, l_ref, ut_ref, cv_ref, ext_u, ext_g, sh_g, pooled_s, pm_s, lacc):
        i = pl.program_id(0)

        @pl.when(i == 0)
        def _():
            lacc[...] = jnp.zeros_like(lacc)

        _odd_pool_tile(i, uc_r, ucp_r, pw_r, ext_u, pooled_s, pm_s, T)
        _odd_glu_tile(i, da_r, dg_r, dap_r, dgp_r, ext_g, sh_g, T)
        base = H - (D_CONV - 1)
        conv = db_r[...] + dw_r[0:1, :] * _window(ext_g, sh_g, base, T)
        for kk in range(1, D_CONV):
            conv = conv + dw_r[kk:kk + 1, :] * _window(ext_g, sh_g, base + kk, T)
        cv_ref[...] = conv
        mu = jnp.mean(conv, axis=-1, keepdims=True)
        xc = conv - mu
        yh = xc * lax.rsqrt(jnp.mean(xc * xc, axis=-1, keepdims=True) + EPS)
        ln = yh * lw_r[...] + lb_r[...]
        zl, zh = zl_r[...], zh_r[...]
        y = x_ref[...] + _out_projection(ut_ref, w_ref, pm_s[...] * sc_r[...] * (zl * _sig(zl)),
                                         ln * _sig(ln) * (zh * _sig(zh)))
        diff = y - t_ref[...]
        dy_ref[...] = diff / float(D_MODEL)
        lacc[...] += _cs8(diff * diff)

        @pl.when(i == nt - 1)
        def _():
            l_ref[...] = jnp.sum(lacc[...], axis=0, keepdims=True)

    row = pl.BlockSpec((T, D_MODEL), lambda i: (i, 0))
    return _call(
        body, "odd_mixer_fwd", (nt,),
        [row, row, pl.BlockSpec((D_MODEL, D_MODEL), lambda i: (0, 0)),
         col(0), col(1), col(2), col(3), col(4), prev(0), prev(1), prev(2)] + wspecs,
        [row, pl.BlockSpec((1, D_MODEL), lambda i: (0, 0)), pl.BlockSpec((D_MODEL, T), lambda i: (0, i)),
         pl.BlockSpec((T, A_WIDTH), lambda i: (i, 0))],
        [jax.ShapeDtypeStruct((S, D_MODEL), F32), jax.ShapeDtypeStruct((1, D_MODEL), F32),
         jax.ShapeDtypeStruct((D_MODEL, S), BF16), jax.ShapeDtypeStruct((S, A_WIDTH), F32)],
        scratch=[pltpu.VMEM((T + H, A_WIDTH), F32), pltpu.VMEM((T + H, A_WIDTH), F32),
                 pltpu.VMEM((SUBLANES - 1, T + H - SHIFT_ROWS_LESS, A_WIDTH), F32),
                 pltpu.VMEM((T, A_WIDTH), F32), pltpu.VMEM((T, A_WIDTH), F32),
                 pltpu.VMEM((SUBLANES, D_MODEL), F32)],
        sem=("arbitrary",))(x, tgt, w_out, proj, proj, proj, proj, proj, proj, proj, proj,
                            pool_w, scale, dconv_w, dconv_b, ln_w, ln_b)


def _odd_mixer_bwd(dy, w_out, proj, conv, pool_w, scale, dconv_w, dconv_b, ln_w, ln_b, T):
    S = proj.shape[0]
    nt = S // T
    order = lambda s: nt - 1 - s
    col, prev, wspecs = _odd_specs(T, S, order)
    H = HALO

    def body(dy_r, w_ref, cv_r, uc_r, da_r, dg_r, zl_r, zh_r, ucp_r, dap_r, dgp_r, pw_r, sc_r, dw_r, db_r, lw_r, lb_r,
             dp_ref, dpw_ref, sm_ref, ext_u, ext_g, sh_g, pooled_s, pm_s, dpl_s, ext_p, ext_c, sh_c, acc):
        step = pl.program_id(0)
        i = nt - 1 - step

        @pl.when(step == 0)
        def _():
            ext_p[T:T + H, :] = jnp.zeros((H, A_WIDTH), F32)
            ext_c[T:T + H, :] = jnp.zeros((H, A_WIDTH), F32)
            acc[...] = jnp.zeros_like(acc)
            dpw_ref[...] = jnp.zeros_like(dpw_ref)

        def accum(r, v):
            acc[r * SUBLANES:(r + 1) * SUBLANES, :] += _cs8(v)

        row = _odd_pool_tile(i, uc_r, ucp_r, pw_r, ext_u, pooled_s, pm_s, T)
        _odd_glu_tile(i, da_r, dg_r, dap_r, dgp_r, ext_g, sh_g, T)
        conv = cv_r[...]
        mu = jnp.mean(conv, axis=-1, keepdims=True)
        xc = conv - mu
        rstd = lax.rsqrt(jnp.mean(xc * xc, axis=-1, keepdims=True) + EPS)
        yh = xc * rstd
        ln = yh * lw_r[...] + lb_r[...]
        sln = _sig(ln)
        zl, zh = zl_r[...], zh_r[...]
        sl, sh = _sig(zl), _sig(zh)
        du = lax.dot_general(dy_r[...].astype(BF16), w_ref[...], NT_DIMS, preferred_element_type=F32)
        dul, duh = du[:, 0:A_WIDTH], du[:, A_WIDTH:]
        pm = pm_s[...]
        scv = sc_r[...]
        dyc = dul * (zl * sl)
        accum(34, dyc * pm)
        dpm = dyc * scv
        for g in range(len(POOL_SIZES)):
            cs = slice(g * LANES, (g + 1) * LANES)
            dpm_g = dpm[:, cs].astype(BF16)
            dpw_ref[g] += lax.dot_general(pooled_s[:, cs].astype(BF16), dpm_g, TN_DIMS, preferred_element_type=F32)
            dpl_s[:, cs] = lax.dot_general(dpm_g, pw_r[g].astype(BF16), NT_DIMS, preferred_element_type=F32)
        lane_p = lax.broadcasted_iota(jnp.int32, (1, A_WIDTH), 1) // LANES
        pvec = jnp.left_shift(2, lane_p)
        cnt = jnp.minimum(row + 1, pvec).astype(F32)
        dpl = dpl_s[...]
        ext_p[0:T, :] = dpl / cnt
        for g, p in enumerate(POOL_SIZES):
            cs = slice(g * LANES, (g + 1) * LANES)
            win = ext_p[0:T, cs]
            for j in range(1, p):
                win = win + ext_p[j:j + T, cs]
            dp_ref[:, cs] = (win - dpl[:, cs]).astype(BF16)
        ext_p[T:T + H, :] = ext_p[0:H, :]
        dln = duh * (zh * sh) * _dsilu(ln, sln)
        accum(32, dln * yh)
        accum(33, dln)
        dyh = dln * lw_r[...]
        dc = rstd * (dyh - jnp.mean(dyh, axis=-1, keepdims=True) - yh * jnp.mean(dyh * yh, axis=-1, keepdims=True))
        accum(31, dc)
        ext_c[0:T, :] = dc
        _fill_shifted(ext_c, sh_c)
        base = H - (D_CONV - 1)
        dgl = dw_r[0:1, :] * _window(ext_c, sh_c, D_CONV - 1, T)
        accum(0, dc * _window(ext_g, sh_g, base, T))
        for kk in range(1, D_CONV):
            dgl = dgl + dw_r[kk:kk + 1, :] * _window(ext_c, sh_c, D_CONV - 1 - kk, T)
            accum(kk, dc * _window(ext_g, sh_g, base + kk, T))
        ext_c[T:T + H, :] = ext_c[0:H, :]
        dav, dgv = da_r[...], dg_r[...]
        sg = _sig(dgv)
        dp_ref[:, A_WIDTH:2 * A_WIDTH] = (dgl * sg).astype(BF16)
        dp_ref[:, 2 * A_WIDTH:3 * A_WIDTH] = (dgl * dav * sg * (1.0 - sg)).astype(BF16)
        dp_ref[:, 3 * A_WIDTH:4 * A_WIDTH] = (dul * (pm * scv) * _dsilu(zl, sl)).astype(BF16)
        dp_ref[:, 4 * A_WIDTH:5 * A_WIDTH] = (duh * (ln * sln) * _dsilu(zh, sh)).astype(BF16)

        @pl.when(step == nt - 1)
        def _():
            for r in range(N_SMALL_ODD):
                sm_ref[r:r + 1, :] = jnp.sum(acc[r * SUBLANES:(r + 1) * SUBLANES, :], axis=0, keepdims=True)

    ext = pltpu.VMEM((T + H, A_WIDTH), F32)
    shifted = pltpu.VMEM((SUBLANES - 1, T + H - SHIFT_ROWS_LESS, A_WIDTH), F32)
    tile = pltpu.VMEM((T, A_WIDTH), F32)
    return _call(
        body, "odd_mixer_bwd", (nt,),
        [pl.BlockSpec((T, D_MODEL), lambda s: (order(s), 0)), pl.BlockSpec((D_MODEL, D_MODEL), lambda s: (0, 0)),
         pl.BlockSpec((T, A_WIDTH), lambda s: (order(s), 0)),
         col(0), col(1), col(2), col(3), col(4), prev(0), prev(1), prev(2)] + wspecs,
        [pl.BlockSpec((T, ODD_IN), lambda s: (order(s), 0)),
         pl.BlockSpec((4, LANES, LANES), lambda s: (0, 0, 0)),
         pl.BlockSpec((N_SMALL_ODD, A_WIDTH), lambda s: (0, 0))],
        [jax.ShapeDtypeStruct((S, ODD_IN), BF16), jax.ShapeDtypeStruct((4, LANES, LANES), F32),
         jax.ShapeDtypeStruct((N_SMALL_ODD, A_WIDTH), F32)],
        scratch=[ext, ext, shifted, tile, tile, tile, ext, ext, shifted,
                 pltpu.VMEM((N_SMALL_ODD * SUBLANES, A_WIDTH), F32)],
        sem=("arbitrary",))(dy, w_out, conv, proj, proj, proj, proj, proj, proj, proj, proj,
                            pool_w, scale, dconv_w, dconv_b, ln_w, ln_b)


TILE_SEQ = 256
TILE_WG = 256
TILE_FIRST = 512


LATE_WEIGHTS = ("e_w_out", "o_w_in", "o_w_out")
ODD_MATS = ("o_w_in", "o_w_out")
EVEN_MATS = ("e_w_in", "e_w_out")


def _reduce_start(names, grads, grads16, cidx):
    recv = _swap_to_sibling(names, [grads16[n] for n in names], "swap_halves_" + names[0][0], True)
    both = [_add_half(cidx, grads[n], r, n) for n, r in zip(names, recv)]
    return [h for h, _ in both], [hb for _, hb in both]


def _local_step(x, pos, tgt, shards, p, cidx, bidx):
    T = TILE_SEQ
    freq = _freq_table()
    wq = jnp.tile(p["e_q_norm_w"], (1, LANES // HEAD_DIM))
    wk = jnp.tile(p["e_k_norm_w"], (1, LANES // HEAD_DIM))

    proj_e, ht_e, w_e_in, late = _inproj_gathering(x, p["e_norm_w"], shards["e_w_in"], bidx, "e_w_in", LATE_WEIGHTS,
                                                   [shards[n] for n in LATE_WEIGHTS], TILE_FIRST, "inproj_even")
    wb = dict(zip(LATE_WEIGHTS, late), e_w_in=w_e_in)
    qkv = _qkv_prep(proj_e, pos, freq, wq, wk, T)
    qs, ks, vs = qkv[0:3], qkv[3:6], qkv[6:9]
    os_, ls_ = [], []
    for g in range(3):
        o, l = _attn_fwd(qs[g], ks[g], vs[g], g)
        os_.append(o)
        ls_.append(l)
    x1, ut_e = _even_mixer_fwd(x, proj_e, os_, ls_, p["e_conv_w"], wb["e_w_out"], T)
    proj_o, ht_o = _inproj(x1, p["o_norm_w"], wb["o_w_in"], TILE_SEQ, 1280, "inproj_odd")
    odd_w = (p["o_pool_w"], p["o_pool_scale"], p["o_dconv_w"], p["o_dconv_b"], p["o_ln_w"], p["o_ln_b"])
    dy, lsum, ut_o, conv_o = _odd_mixer_fwd(x1, tgt, proj_o, *odd_w, wb["o_w_out"], T)

    g, g16 = {}, {}
    g["o_w_out"], g16["o_w_out"] = _mm_wgrad(ut_o, [dy], TILE_WG, "wgrad_o_out")
    dproj_o, g["o_pool_w"], small_o = _odd_mixer_bwd(dy, wb["o_w_out"], proj_o, conv_o, *odd_w, T)
    g["o_w_in"], g16["o_w_in"] = _mm_wgrad(ht_o, [dproj_o], TILE_WG, "wgrad_o_in")
    half_o, half_o16 = _reduce_start(ODD_MATS, g, g16, cidx)
    dx1, g["o_norm_w"], blocks_o = _mm_nt_rms([dproj_o], wb["o_w_in"], x1, p["o_norm_w"], dy, TILE_SEQ, "dx_odd",
                                              ODD_MATS, half_o16)
    g["o_dconv_w"] = small_o[0:D_CONV]
    g["o_dconv_b"] = small_o[31:32]
    g["o_ln_w"] = small_o[32:33]
    g["o_ln_b"] = small_o[33:34]
    g["o_pool_scale"] = small_o[34:35]

    g["e_w_out"], g16["e_w_out"] = _mm_wgrad(ut_e, [dx1], TILE_WG, "wgrad_e_out")
    dos, cgs, drest, g["e_conv_w"] = _even_mixer_bwd(dx1, wb["e_w_out"], proj_e, os_, ls_, p["e_conv_w"], T)
    dqs, dks, dvs = [], [], []
    for gi in range(3):
        dq, dk, dv = _attn_bwd(qs[gi], ks[gi], vs[gi], dos[gi], ls_[gi], cgs[gi], gi)
        dqs.append(dq)
        dks.append(dk)
        dvs.append(dv)
    dqkv, dnw = _qk_bwd(proj_e, dqs, dks, dvs, pos, freq, wq, wk, T)
    g["e_q_norm_w"] = dnw[0:1, 0:HEAD_DIM]
    g["e_k_norm_w"] = dnw[1:2, 0:HEAD_DIM]
    pieces = [dqkv, drest]
    g["e_w_in"], g16["e_w_in"] = _mm_wgrad(ht_e, pieces, TILE_WG, "wgrad_e_in")
    half_e, half_e16 = _reduce_start(EVEN_MATS, g, g16, cidx)
    dx, g["e_norm_w"], blocks_e = _mm_nt_rms(pieces, wb["e_w_in"], x, p["e_norm_w"], dx1, TILE_SEQ, "dx_even",
                                             EVEN_MATS, half_e16)
    parts = {}
    for names, halves, blocks in ((ODD_MATS, half_o, blocks_o), (EVEN_MATS, half_e, blocks_e)):
        for n, h, r in zip(names, halves, blocks):
            parts[n] = _add_blocks(bidx, h, r, n)
    return lsum, dx, g, parts


BIG = ("e_w_in", "e_w_out", "o_w_in", "o_w_out")
SHARD_AXIS = {"e_w_in": 1, "e_w_out": 0, "o_w_in": 1, "o_w_out": 0}
N_CHIPS = 4


def _place():
    x, y, c = lax.axis_index("x"), lax.axis_index("y"), lax.axis_index("c")
    chips = [(1 - x, y), (x, 1 - y), (1 - x, 1 - y)]
    return x, y, c, chips


def _block_of(ref, name, block):
    rows, cols = ref.shape
    if SHARD_AXIS[name] == 1:
        cw = cols // N_CHIPS
        return ref.at[:, pl.ds(pl.multiple_of(block * cw, LANES), cw)]
    rw = rows // N_CHIPS
    return ref.at[pl.ds(pl.multiple_of(block * rw, rw), rw), :]


def _half_of(ref, name, half):
    rows, cols = ref.shape
    if SHARD_AXIS[name] == 1:
        return ref.at[pl.ds(pl.multiple_of(half * (rows // 2), rows // 2), rows // 2), :]
    return ref.at[:, pl.ds(pl.multiple_of(half * (cols // 2), LANES), cols // 2)]


def _sub(ref, name, block, half):
    rows, cols = ref.shape
    if SHARD_AXIS[name] == 1:
        cw, hr = cols // N_CHIPS, rows // 2
        return ref.at[pl.ds(pl.multiple_of(half * hr, hr), hr), pl.ds(pl.multiple_of(block * cw, LANES), cw)]
    rw, hc = rows // N_CHIPS, cols // 2
    return ref.at[pl.ds(pl.multiple_of(block * rw, rw), rw), pl.ds(pl.multiple_of(half * hc, LANES), hc)]


GATHER_COPIES = 7


class _Gather:
    def __init__(self, names, s_refs, f_refs, send, recv):
        self.names, self.s, self.f, self.send, self.recv = names, s_refs, f_refs, send, recv

    def _copy(self, k, src, dst, to):
        return pltpu.make_async_remote_copy(src_ref=src, dst_ref=dst, send_sem=self.send.at[k],
                                            recv_sem=self.recv.at[k], device_id=to, device_id_type=MESH)

    def _plan(self):
        x, y, c, chips = _place()
        me, sib = 2 * x + y, (x, y, 1 - c)
        first, relay_in, relay, last_in = [], [], [], []
        for wi, n in enumerate(self.names):
            k0 = wi * GATHER_COPIES
            s, f = self.s[wi], self.f[wi]
            own = _block_of(f, n, me)
            first.append(self._copy(k0 + 3, s, own, sib))
            last_in.append(self._copy(k0 + 3, s, own, sib))
            for j, (cx, cy) in enumerate(chips):
                first.append(self._copy(k0 + j, _half_of(s, n, c), _sub(f, n, me, c), (cx, cy, c)))
                mine = _sub(f, n, 2 * cx + cy, c)
                relay_in.append(self._copy(k0 + j, mine, mine, sib))
                relay.append(self._copy(k0 + 4 + j, mine, mine, sib))
                theirs = _sub(f, n, 2 * cx + cy, 1 - c)
                last_in.append(self._copy(k0 + 4 + j, theirs, theirs, sib))
        return first, relay_in, relay, last_in

    N_RELATIONS = 3

    def begin(self, relations=(0, 1, 2), sibling=True):
        first = self._plan()[0]
        for wi in range(len(self.names)):
            mine = first[wi * (1 + self.N_RELATIONS):(wi + 1) * (1 + self.N_RELATIONS)]
            if sibling:
                mine[0].start()
            for j in relations:
                mine[1 + j].start()

    def relay(self, relations=(0, 1, 2)):
        _, relay_in, relay, _ = self._plan()
        for wi in range(len(self.names)):
            for j in relations:
                relay_in[wi * self.N_RELATIONS + j].wait_recv()
                relay[wi * self.N_RELATIONS + j].start()

    def end(self):
        first, _, relay, last_in = self._plan()
        for cp in last_in:
            cp.wait_recv()
        for cp in first + relay:
            cp.wait_send()

    def wait_relayed(self, j):
        self._plan()[3][1 + j].wait_recv()

    def end_rest(self):
        first, _, relay, last_in = self._plan()
        last_in[0].wait_recv()
        for cp in first + relay:
            cp.wait_send()


def _full_shape(n, s):
    r, cdim = s.shape
    return jax.ShapeDtypeStruct((r, cdim * N_CHIPS) if SHARD_AXIS[n] == 1 else (r * N_CHIPS, cdim), s.dtype)


def _gather_sems(names):
    k = GATHER_COPIES * len(names)
    return [pltpu.SemaphoreType.DMA((k,)), pltpu.SemaphoreType.DMA((k,))]


def _scatter_copies(names, h_refs, r_refs, send, recv):
    _, _, c, chips = _place()
    cps = []
    for wi, n in enumerate(names):
        for j, (cx, cy) in enumerate(chips):
            cps.append(pltpu.make_async_remote_copy(
                src_ref=_block_of(h_refs[wi], n, 2 * cx + cy), dst_ref=r_refs[wi].at[j],
                send_sem=send.at[wi * 3 + j], recv_sem=recv.at[wi * 3 + j],
                device_id=(cx, cy, c), device_id_type=MESH))
    return cps


def _scatter_sems(names):
    return [pltpu.SemaphoreType.DMA((3 * len(names),)), pltpu.SemaphoreType.DMA((3 * len(names),))]


def _allreduce_small(part, name):
    R = part.shape[0]

    def body(p_ref, o_ref, sbuf, cbuf, send, recv):
        x, y, c, chips = _place()
        me = 2 * x + y
        sib = (x, y, 1 - c)
        sbuf[c] = p_ref[...]
        mine = sbuf.at[c]
        d2d = pltpu.make_async_remote_copy(src_ref=mine, dst_ref=mine, send_sem=send.at[0], recv_sem=recv.at[0],
                                           device_id=sib, device_id_type=MESH)
        d2d.start()
        theirs = sbuf.at[1 - c]
        pltpu.make_async_remote_copy(src_ref=theirs, dst_ref=theirs, send_sem=send.at[0], recv_sem=recv.at[0],
                                     device_id=sib, device_id_type=MESH).wait_recv()
        cbuf[me] = sbuf[0] + sbuf[1]
        blk = cbuf.at[me]
        sends = [d2d]
        for j, (cx, cy) in enumerate(chips):
            cp = pltpu.make_async_remote_copy(src_ref=blk, dst_ref=blk, send_sem=send.at[1 + j], recv_sem=recv.at[1 + j],
                                              device_id=(cx, cy, c), device_id_type=MESH)
            cp.start()
            sends.append(cp)
        for j, (cx, cy) in enumerate(chips):
            got = cbuf.at[2 * cx + cy]
            pltpu.make_async_remote_copy(src_ref=got, dst_ref=got, send_sem=send.at[1 + j], recv_sem=recv.at[1 + j],
                                         device_id=(cx, cy, c), device_id_type=MESH).wait_recv()
        o_ref[...] = (cbuf[0] + cbuf[1]) + (cbuf[2] + cbuf[3])
        for cp in sends:
            cp.wait_send()

    vm = pl.BlockSpec(memory_space=pltpu.VMEM)
    return pl.pallas_call(
        body, name=name, in_specs=[vm], out_specs=vm,
        out_shape=jax.ShapeDtypeStruct(part.shape, F32),
        scratch_shapes=[pltpu.VMEM((2, R, LANES), F32), pltpu.VMEM((N_CHIPS, R, LANES), F32),
                        pltpu.SemaphoreType.DMA((4,)), pltpu.SemaphoreType.DMA((4,))],
    )(part)


def _half_shape(shape, name):
    r, cdim = shape
    return (r // 2, cdim) if SHARD_AXIS[name] == 1 else (r, cdim // 2)


def _shard_shape(shape, name):
    r, cdim = shape
    return (r, cdim // N_CHIPS) if SHARD_AXIS[name] == 1 else (r // N_CHIPS, cdim)


def _swap_to_sibling(names, srcs, name, pick_half):
    nw = len(names)

    def body(*refs):
        g_refs, r_refs = refs[:nw], refs[nw:2 * nw]
        send, recv = refs[2 * nw:]
        x, y, c, _ = _place()
        sib = (x, y, 1 - c)
        cps = []
        for wi, n in enumerate(names):
            src = _half_of(g_refs[wi], n, 1 - c) if pick_half else g_refs[wi]
            cp = pltpu.make_async_remote_copy(src_ref=src, dst_ref=r_refs[wi], send_sem=send.at[wi],
                                              recv_sem=recv.at[wi], device_id=sib, device_id_type=MESH)
            cp.start()
            cps.append(cp)
        for cp in cps:
            cp.wait()

    outs = [jax.ShapeDtypeStruct(_half_shape(g.shape, n) if pick_half else g.shape, g.dtype)
            for n, g in zip(names, srcs)]
    return pl.pallas_call(
        body, name=name, in_specs=[ANY] * nw, out_specs=[ANY] * nw, out_shape=outs,
        scratch_shapes=[pltpu.SemaphoreType.DMA((nw,)), pltpu.SemaphoreType.DMA((nw,))],
    )(*srcs)


def _add_half(cidx, g, r, name):
    rows, cols = r.shape
    tr = 256
    tc = cols if cols <= 1792 else (1792 if cols % 1792 == 0 else 1280)
    nr, nc = rows // tr, cols // tc

    def body(c_ref, g_ref, r_ref, o_ref, ob_ref):
        s = g_ref[...] + r_ref[...].astype(F32)
        o_ref[...] = s
        ob_ref[...] = s.astype(BF16)

    if SHARD_AXIS[name] == 1:
        gmap = lambda i, j, c_ref: (c_ref[0] * nr + i, j)
    else:
        gmap = lambda i, j, c_ref: (i, c_ref[0] * nc + j)
    same = lambda i, j, c_ref: (i, j)
    return pl.pallas_call(
        body, name="add_half_" + name,
        grid_spec=pltpu.PrefetchScalarGridSpec(
            num_scalar_prefetch=1, grid=(nr, nc),
            in_specs=[pl.BlockSpec((tr, tc), gmap), pl.BlockSpec((tr, tc), same)],
            out_specs=[pl.BlockSpec((tr, tc), same), pl.BlockSpec((tr, tc), same)]),
        out_shape=[jax.ShapeDtypeStruct(r.shape, F32), jax.ShapeDtypeStruct(r.shape, BF16)],
        compiler_params=pltpu.CompilerParams(dimension_semantics=("parallel", "parallel"), vmem_limit_bytes=VMEM_LIMIT),
    )(cidx, g, r)


def _add_blocks(bidx, h, r, name):
    _, rows, cols = r.shape
    tr = min(rows, 256)
    nr = rows // tr

    def body(b_ref, h_ref, r0, r1, r2, o_ref):
        o_ref[...] = ((h_ref[...] + r0[0].astype(F32)) + r1[0].astype(F32)) + r2[0].astype(F32)

    if SHARD_AXIS[name] == 1:
        hmap = lambda i, b_ref: (i, b_ref[0])
    else:
        hmap = lambda i, b_ref: (b_ref[0] * nr + i, 0)
    rspec = lambda j: pl.BlockSpec((1, tr, cols), lambda i, b_ref, j=j: (j, i, 0))
    return pl.pallas_call(
        body, name="add_blocks_" + name,
        grid_spec=pltpu.PrefetchScalarGridSpec(
            num_scalar_prefetch=1, grid=(nr,),
            in_specs=[pl.BlockSpec((tr, cols), hmap), rspec(0), rspec(1), rspec(2)],
            out_specs=pl.BlockSpec((tr, cols), lambda i, b_ref: (i, 0))),
        out_shape=jax.ShapeDtypeStruct((rows, cols), F32),
        compiler_params=pltpu.CompilerParams(dimension_semantics=("parallel",), vmem_limit_bytes=VMEM_LIMIT),
    )(bidx, h, r, r, r)


def _adam_math(w, g, m, v):
    c1 = 1.0 - ADAM_B1 ** ADAM_STEP
    c2 = 1.0 - ADAM_B2 ** ADAM_STEP
    nm = ADAM_B1 * m + (1.0 - ADAM_B1) * g
    nv = ADAM_B2 * v + (1.0 - ADAM_B2) * (g * g)
    delta = -ADAM_LR * ((nm / c1) / (jnp.sqrt(nv / c2) + ADAM_EPS) + ADAM_WD * w)
    return delta, nm, nv


def _adamw(w, g, m, v, name):
    def body(w_ref, g_ref, m_ref, v_ref, d_ref, nm_ref, nv_ref):
        d_ref[...], nm_ref[...], nv_ref[...] = _adam_math(w_ref[...], g_ref[...], m_ref[...], v_ref[...])

    spec = pl.BlockSpec(w.shape, lambda i: (0, 0))
    return _call(body, "adamw_" + name, (1,), [spec] * 4, [spec] * 3,
                 [jax.ShapeDtypeStruct(w.shape, F32)] * 3, sem=("arbitrary",))(w, g, m, v)


def _adamw_halves(cidx, w, mine, theirs, m, v, name):
    rows, cols = w.shape
    hr, hc = mine.shape
    tr = 128
    if SHARD_AXIS[name] == 1:
        ni = hr // tr
        wmap = lambda hh, i, c_ref: (hh * ni + i, 0)
    else:
        ni = hr // tr
        wmap = lambda hh, i, c_ref: (i, hh)
    hmap = lambda hh, i, c_ref: (i, 0)

    def body(c_ref, w_ref, a_ref, b_ref, m_ref, v_ref, g_ref, d_ref, nm_ref, nv_ref):
        g = jnp.where(pl.program_id(0) == c_ref[0], a_ref[...], b_ref[...])
        g_ref[...] = g
        d_ref[...], nm_ref[...], nv_ref[...] = _adam_math(w_ref[...], g, m_ref[...], v_ref[...])

    wspec = pl.BlockSpec((tr, hc), wmap)
    hspec = pl.BlockSpec((tr, hc), hmap)
    return pl.pallas_call(
        body, name="adamw_" + name,
        grid_spec=pltpu.PrefetchScalarGridSpec(
            num_scalar_prefetch=1, grid=(2, ni),
            in_specs=[wspec, hspec, hspec, wspec, wspec], out_specs=[wspec] * 4),
        out_shape=[jax.ShapeDtypeStruct(w.shape, F32)] * 4,
        compiler_params=pltpu.CompilerParams(dimension_semantics=("parallel", "parallel"), vmem_limit_bytes=VMEM_LIMIT),
    )(cidx, w, mine, theirs, m, v)


SMALL = ("e_norm_w", "e_q_norm_w", "e_k_norm_w", "e_conv_w", "o_norm_w", "o_pool_w", "o_pool_scale",
         "o_dconv_w", "o_dconv_b", "o_ln_w", "o_ln_b")
SMALL_SHARDED = ("e_conv_w", "o_norm_w", "o_pool_scale", "o_dconv_w", "o_dconv_b", "o_ln_w", "o_ln_b")
WEIGHTS = ("e_norm_w", "e_w_in", "e_q_norm_w", "e_k_norm_w", "e_conv_w", "e_w_out", "o_norm_w", "o_w_in",
           "o_pool_w", "o_pool_scale", "o_dconv_w", "o_dconv_b", "o_ln_w", "o_ln_b", "o_w_out")


def _pack(arrs):
    flat = jnp.concatenate([a.reshape(-1) for a in arrs])
    rows = -(-flat.shape[0] // (LANES * SUBLANES)) * SUBLANES
    flat = jnp.pad(flat, (0, rows * LANES - flat.shape[0]))
    return flat.reshape(rows, LANES)


def _unpack(packed, shapes):
    flat = packed.reshape(-1)
    out, off = [], 0
    for s in shapes:
        n = int(np.prod(s))
        out.append(flat[off:off + n].reshape(s))
        off += n
    return out


def _gather_last(a, block, width):
    return lax.dynamic_slice_in_dim(a, block * width, width, axis=a.ndim - 1)


def kernel(x, positions, e_norm_w, e_w_in, e_q_norm_w, e_k_norm_w, e_conv_w, e_w_out, o_norm_w, o_w_in, o_pool_w, o_pool_scale, o_dconv_w, o_dconv_b, o_ln_w, o_ln_b, o_w_out, loss_target, m_e_norm_w, m_e_w_in, m_e_q_norm_w, m_e_k_norm_w, m_e_conv_w, m_e_w_out, m_o_norm_w, m_o_w_in, m_o_pool_w, m_o_pool_scale, m_o_dconv_w, m_o_dconv_b, m_o_ln_w, m_o_ln_b, m_o_w_out, v_e_norm_w, v_e_w_in, v_e_q_norm_w, v_e_k_norm_w, v_e_conv_w, v_e_w_out, v_o_norm_w, v_o_w_in, v_o_pool_w, v_o_pool_scale, v_o_dconv_w, v_o_dconv_b, v_o_ln_w, v_o_ln_b, v_o_w_out):
    given = dict(e_norm_w=e_norm_w, e_w_in=e_w_in, e_q_norm_w=e_q_norm_w, e_k_norm_w=e_k_norm_w, e_conv_w=e_conv_w,
                 e_w_out=e_w_out, o_norm_w=o_norm_w, o_w_in=o_w_in, o_pool_w=o_pool_w, o_pool_scale=o_pool_scale,
                 o_dconv_w=o_dconv_w, o_dconv_b=o_dconv_b, o_ln_w=o_ln_w, o_ln_b=o_ln_b, o_w_out=o_w_out)
    mom = dict(e_norm_w=m_e_norm_w, e_w_in=m_e_w_in, e_q_norm_w=m_e_q_norm_w, e_k_norm_w=m_e_k_norm_w,
               e_conv_w=m_e_conv_w, e_w_out=m_e_w_out, o_norm_w=m_o_norm_w, o_w_in=m_o_w_in, o_pool_w=m_o_pool_w,
               o_pool_scale=m_o_pool_scale, o_dconv_w=m_o_dconv_w, o_dconv_b=m_o_dconv_b, o_ln_w=m_o_ln_w,
               o_ln_b=m_o_ln_b, o_w_out=m_o_w_out)
    var = dict(e_norm_w=v_e_norm_w, e_w_in=v_e_w_in, e_q_norm_w=v_e_q_norm_w, e_k_norm_w=v_e_k_norm_w,
               e_conv_w=v_e_conv_w, e_w_out=v_e_w_out, o_norm_w=v_o_norm_w, o_w_in=v_o_w_in, o_pool_w=v_o_pool_w,
               o_pool_scale=v_o_pool_scale, o_dconv_w=v_o_dconv_w, o_dconv_b=v_o_dconv_b, o_ln_w=v_o_ln_w,
               o_ln_b=v_o_ln_b, o_w_out=v_o_w_out)
    S = x.shape[1]
    mx, my, mc = lax.axis_index("x"), lax.axis_index("y"), lax.axis_index("c")
    chip = 2 * mx + my
    cidx = jnp.reshape(mc, (1,)).astype(jnp.int32)
    bidx = jnp.reshape(chip, (1,)).astype(jnp.int32)

    shards = {n: given[n][0].astype(BF16) for n in BIG}
    shard_sizes = [int(np.prod(given[n].shape)) for n in SMALL_SHARDED]
    own = _pack([given[n] for n in SMALL_SHARDED])
    rows = own.shape[0]
    slots = jnp.zeros((N_CHIPS, rows, LANES), F32)
    own = jnp.where(mc == 0, own, 0.0)
    slots = lax.dynamic_update_slice(slots, own[None], (chip, 0, 0))
    gathered = _allreduce_small(slots.reshape(N_CHIPS * rows, LANES), "gather_small")
    gathered = gathered.reshape(N_CHIPS, rows * LANES)
    p = {}
    off = 0
    for n, size in zip(SMALL_SHARDED, shard_sizes):
        sh = given[n].shape[1:]
        parts = gathered[:, off:off + size].reshape((N_CHIPS,) + sh)
        fullp = jnp.moveaxis(parts, 0, -2).reshape(sh[:-1] + (N_CHIPS * sh[-1],))
        p[n] = fullp.reshape(-1, fullp.shape[-1])
        off += size
    p["e_norm_w"] = e_norm_w
    p["e_q_norm_w"] = e_q_norm_w
    p["e_k_norm_w"] = e_k_norm_w
    p["o_pool_w"] = o_pool_w[0]

    lsum, dx, g, parts = _local_step(x[0], positions.reshape(S, 1), loss_target[0], shards, p, cidx, bidx)
    loss = lax.psum(0.5 * jnp.sum(lsum) / float(D_MODEL), ("x", "y", "c"))

    tot = _unpack(_allreduce_small(_pack([g[n] for n in SMALL]), "allreduce_small"), [g[n].shape for n in SMALL])
    gsmall = dict(zip(SMALL, tot))
    grads = {}
    for n in SMALL:
        gv = gsmall[n]
        if n in SMALL_SHARDED:
            gv = _gather_last(gv, chip, gv.shape[-1] // N_CHIPS)
        grads[n] = gv.reshape(given[n].shape)

    theirs = _swap_to_sibling(BIG, [parts[n] for n in BIG], "swap_reduced", False)

    delta, new_m, new_v = {}, {}, {}
    for n, other in zip(BIG, theirs):
        sh = given[n].shape
        gs, d, nm, nv = _adamw_halves(cidx, given[n][0], parts[n], other, mom[n][0], var[n][0], n)
        grads[n], delta[n], new_m[n], new_v[n] = gs.reshape(sh), d.reshape(sh), nm.reshape(sh), nv.reshape(sh)
    big_small = "o_pool_w"
    pw = [src[big_small].reshape(-1, LANES) for src in (given, grads, mom, var)]
    for dst, a in zip((delta, new_m, new_v), _adamw(*pw, "pool_w")):
        dst[big_small] = a.reshape(given[big_small].shape)
    tiny = tuple(n for n in SMALL if n != big_small)
    shapes = [given[n].shape for n in tiny]
    packed = [_pack([src[n] for n in tiny]) for src in (given, grads, mom, var)]
    for dst, pk in zip((delta, new_m, new_v), _adamw(*packed, "small")):
        for n, a in zip(tiny, _unpack(pk, shapes)):
            dst[n] = a
    return (loss, dx[None], *[grads[n] for n in WEIGHTS], *[delta[n] for n in WEIGHTS],
            *[new_m[n] for n in WEIGHTS], *[new_v[n] for n in WEIGHTS])
```

```python
import numpy as np
import jax
import jax.numpy as jnp
from jax import lax
from jax.experimental import pallas as pl
from jax.experimental.pallas import tpu as pltpu

F32 = jnp.float32
BF16 = jnp.bfloat16

D_MODEL = 1024
HEAD_DIM = 64
A_WIDTH = 512
A_HEADS = 8
A_GROUPS = ((128, 1), (512, 4), (2048, 16))
QBLK = 128
ROT_DIM = 16
ROPE_THETA = 500000.0
POOL_SIZES = (2, 4, 8, 16)
D_CONV = 31
SC_WIDTH = 3
EVEN_IN = 7168
ODD_IN = 2560
EPS = 1e-6
NEG = -1e30
ADAM_LR, ADAM_B1, ADAM_B2, ADAM_EPS, ADAM_WD, ADAM_STEP = 0.001, 0.9, 0.999, 1e-08, 0.01, 10

LANES = 128
SUBLANES = 8
HALO = 32
VMEM_LIMIT = 52 * 1024 * 1024
MESH = pl.DeviceIdType.MESH
ANY = pl.BlockSpec(memory_space=pl.ANY)

NT_DIMS = (((1,), (1,)), ((), ()))
TN_DIMS = (((0,), (0,)), ((), ()))


def _call(body, name, grid, in_specs, out_specs, out_shape, scratch=(), sem=None, aliases=None):
    return pl.pallas_call(
        body, name=name, grid=grid, in_specs=in_specs, out_specs=out_specs, out_shape=out_shape,
        scratch_shapes=list(scratch), input_output_aliases=aliases or {},
        compiler_params=pltpu.CompilerParams(dimension_semantics=sem, vmem_limit_bytes=VMEM_LIMIT))


def _sig(v):
    return jax.nn.sigmoid(v)


def _dsilu(v, s):
    return s * (1.0 + v * (1.0 - s))


def _out_projection(ut_ref, w_ref, lo, hi):
    acc = None
    for k, v in enumerate((lo, hi)):
        ut_ref[k * A_WIDTH:(k + 1) * A_WIDTH, :] = v.T.astype(BF16)
        part = jnp.dot(v.astype(BF16), w_ref[k * A_WIDTH:(k + 1) * A_WIDTH, :], preferred_element_type=F32)
        acc = part if acc is None else acc + part
    return acc


def _cs8(v):
    return v.reshape(v.shape[0] // SUBLANES, SUBLANES, v.shape[1]).sum(axis=0)


def _seg_mean():
    r = lax.broadcasted_iota(jnp.int32, (LANES, LANES), 0) // HEAD_DIM
    c = lax.broadcasted_iota(jnp.int32, (LANES, LANES), 1) // HEAD_DIM
    return jnp.where(r == c, 1.0 / HEAD_DIM, 0.0).astype(BF16)


def _segsum(v, ones):
    hi = v.astype(BF16)
    lo = (v - hi.astype(F32)).astype(BF16)
    return (jnp.dot(hi, ones, preferred_element_type=F32) + jnp.dot(lo, ones, preferred_element_type=F32))


def _head_rstd(v, seg_mean):
    return lax.rsqrt(jnp.dot((v * v).astype(BF16), seg_mean, preferred_element_type=F32) + EPS)


def _rope_tables(pos_ref, freq_ref):
    ang = pos_ref[...].astype(F32) * freq_ref[...]
    cosv, sinv = jnp.cos(ang), jnp.sin(ang)
    lm = lax.broadcasted_iota(jnp.int32, ang.shape, 1) % HEAD_DIM
    half = ROT_DIM // 2
    c = jnp.where(lm < ROT_DIM, cosv, 1.0)
    s1 = jnp.where((lm >= half) & (lm < ROT_DIM), sinv, 0.0)
    s2 = jnp.where(lm < half, -sinv, 0.0)
    return c, s1, s2


def _freq_table():
    half = ROT_DIM // 2
    inv = ROPE_THETA ** (-np.arange(half, dtype=np.float64) / half)
    lane = np.arange(LANES) % HEAD_DIM
    f = np.where(lane < ROT_DIM, inv[lane % half], 0.0)
    return jnp.asarray(f.reshape(1, LANES), F32)


def _load_once(hbm_ref, vmem_ref, sem):
    @pl.when(pl.program_id(0) == 0)
    def _():
        cp = pltpu.make_async_copy(hbm_ref, vmem_ref, sem)
        cp.start()
        cp.wait()


def _rms_rows(x_ref, nw_ref):
    xv = x_ref[...]
    ms = jnp.mean(xv * xv, axis=-1, keepdims=True)
    return xv * lax.rsqrt(ms + EPS) * nw_ref[...]


def _inproj(x, nw, w, tm, tn, name):
    S, N = x.shape[0], w.shape[1]

    def body(x_ref, nw_ref, w_hbm, o_ref, ht_ref, w_v, sem):
        _load_once(w_hbm, w_v, sem)
        h = _rms_rows(x_ref, nw_ref)
        ht_ref[...] = h.T.astype(BF16)
        hb = h.astype(BF16)
        for j in range(N // tn):
            o_ref[:, j * tn:(j + 1) * tn] = jnp.dot(hb, w_v[:, j * tn:(j + 1) * tn], preferred_element_type=F32)

    return _call(
        body, name, (S // tm,),
        [pl.BlockSpec((tm, D_MODEL), lambda i: (i, 0)),
         pl.BlockSpec((1, D_MODEL), lambda i: (0, 0)), ANY],
        [pl.BlockSpec((tm, N), lambda i: (i, 0)),
         pl.BlockSpec((D_MODEL, tm), lambda i: (0, i))],
        [jax.ShapeDtypeStruct((S, N), F32), jax.ShapeDtypeStruct((D_MODEL, S), BF16)],
        scratch=[pltpu.VMEM(w.shape, BF16), pltpu.SemaphoreType.DMA(())], sem=("arbitrary",))(x, nw, w)


def _inproj_gathering(x, nw, shard, bidx, first, late_names, late_shards, tm, name):
    S = x.shape[0]
    ni = S // tm
    K, cw = shard.shape
    nl = len(late_names)
    last = N_CHIPS - 1

    def body(b_ref, x_ref, nw_ref, s_hbm, *rest):
        ls_refs = rest[:nl]
        o_ref, ht_ref, f_hbm = rest[nl:nl + 3]
        lf_refs = rest[nl + 3:2 * nl + 3]
        hs, w_blk, lsem, send1, recv1, send2, recv2 = rest[2 * nl + 3:]
        j, i = pl.program_id(0), pl.program_id(1)
        g1 = _Gather((first,), (s_hbm,), (f_hbm,), send1, recv1)
        g2 = _Gather(late_names, ls_refs, lf_refs, send2, recv2)
        _, _, _, chips = _place()

        def load_block(src):
            cp = pltpu.make_async_copy(src, w_blk, lsem)
            cp.start()
            cp.wait()

        @pl.when((j == 0) & (i == 0))
        def _():
            g1.begin(relations=(0, 1))
            load_block(s_hbm)

        for r, (cx, cy) in enumerate(chips):
            @pl.when((j == r + 1) & (i == 0))
            def _(r=r, cx=cx, cy=cy):
                g1.wait_relayed(r)
                load_block(_block_of(f_hbm, first, 2 * cx + cy))

        @pl.when((j == 2) & (i == 0))
        def _():
            g1.relay(relations=(2,))
            g2.begin()

        pl.when((j == last) & (i == ni // 2))(g2.relay)

        rows = pl.ds(pl.multiple_of(i * tm, tm), tm)

        @pl.when(j == 0)
        def _():
            h = _rms_rows(x_ref, nw_ref)
            hs[rows, :] = h.astype(BF16)
            ht_ref[...] = h.T.astype(BF16)

        o_ref[...] = jnp.dot(hs[rows, :], w_blk[...], preferred_element_type=F32)

        @pl.when((j == 0) & (i == ni - 1))
        def _():
            g1.relay(relations=(0, 1))
            g1.begin(relations=(2,), sibling=False)

        @pl.when((j == last) & (i == ni - 1))
        def _():
            g1.end_rest()
            g2.end()

    def block_of_step(j, b_ref):
        return jnp.bitwise_xor(b_ref[0], jnp.bitwise_or(jnp.left_shift(jnp.bitwise_and(j, 1), 1), jnp.right_shift(j, 1)))

    outs = pl.pallas_call(
        body, name=name,
        grid_spec=pltpu.PrefetchScalarGridSpec(
            num_scalar_prefetch=1, grid=(N_CHIPS, ni),
            in_specs=[pl.BlockSpec((tm, D_MODEL), lambda j, i, b: (jnp.where(j == 0, i, 0), 0)),
                      pl.BlockSpec((1, D_MODEL), lambda j, i, b: (0, 0)), ANY] + [ANY] * nl,
            out_specs=[pl.BlockSpec((tm, cw), lambda j, i, b: (i, block_of_step(j, b))),
                       pl.BlockSpec((D_MODEL, tm), lambda j, i, b: (0, jnp.where(j == 0, i, ni - 1))),
                       ANY] + [ANY] * nl,
            scratch_shapes=[pltpu.VMEM((S, D_MODEL), BF16), pltpu.VMEM((K, cw), BF16), pltpu.SemaphoreType.DMA(())] +
            _gather_sems((first,)) + _gather_sems(late_names)),
        out_shape=[jax.ShapeDtypeStruct((S, cw * N_CHIPS), F32), jax.ShapeDtypeStruct((D_MODEL, S), BF16),
                   _full_shape(first, shard)] + [_full_shape(n, s) for n, s in zip(late_names, late_shards)],
        compiler_params=pltpu.CompilerParams(dimension_semantics=("arbitrary", "arbitrary"),
                                             vmem_limit_bytes=VMEM_LIMIT),
    )(bidx, x, nw, shard, *late_shards)
    return outs[0], outs[1], outs[2], list(outs[3:])


def _piece_blocks(pieces, tk, axis):
    starts, counts, s = [], [], 0
    for p in pieces:
        n = p.shape[axis] // tk
        starts.append(s)
        counts.append(n)
        s += n
    return starts, counts, s


def _mm_nt_rms(pieces, w, x, nw, dres, tm, name, scatter_names=(), scatter_halves=()):
    S = x.shape[0]
    npc = len(pieces)
    ni = S // tm
    ns = len(scatter_names)
    offs = np.cumsum([0] + [p.shape[1] for p in pieces]).tolist()

    def body(*refs):
        p_refs = refs[:npc]
        w_hbm, x_ref, nw_ref, dr_ref = refs[npc:npc + 4]
        h_refs = refs[npc + 4:npc + 4 + ns]
        dx_ref, dnw_ref = refs[npc + 4 + ns:npc + 6 + ns]
        r_refs = refs[npc + 6 + ns:npc + 6 + 2 * ns]
        w_v, sem, nacc = refs[npc + 6 + 2 * ns:npc + 9 + 2 * ns]
        i = pl.program_id(0)
        if ns:
            send, recv = refs[npc + 9 + 2 * ns:]

            @pl.when(i == 0)
            def _():
                for cp in _scatter_copies(scatter_names, h_refs, r_refs, send, recv):
                    cp.start()
        _load_once(w_hbm, w_v, sem)

        @pl.when(i == 0)
        def _():
            nacc[...] = jnp.zeros_like(nacc)

        dh = None
        for p in range(npc):
            part = lax.dot_general(p_refs[p][...].astype(BF16), w_v[:, offs[p]:offs[p + 1]], NT_DIMS,
                                   preferred_element_type=F32)
            dh = part if dh is None else dh + part
        xv = x_ref[...]
        rs = lax.rsqrt(jnp.mean(xv * xv, axis=-1, keepdims=True) + EPS)
        xh = xv * rs
        nacc[...] += _cs8(dh * xh)
        dxh = dh * nw_ref[...]
        dx_ref[...] = dr_ref[...] + rs * (dxh - xh * jnp.mean(dxh * xh, axis=-1, keepdims=True))

        @pl.when(i == ni - 1)
        def _():
            dnw_ref[...] = jnp.sum(nacc[...], axis=0, keepdims=True)
            if ns:
                for cp in _scatter_copies(scatter_names, h_refs, r_refs, send, recv):
                    cp.wait()

    row = pl.BlockSpec((tm, D_MODEL), lambda i: (i, 0))
    outs = _call(
        body, name, (ni,),
        [pl.BlockSpec((tm, p.shape[1]), lambda i: (i, 0)) for p in pieces] +
        [ANY, row, pl.BlockSpec((1, D_MODEL), lambda i: (0, 0)), row] + [ANY] * ns,
        [row, pl.BlockSpec((1, D_MODEL), lambda i: (0, 0))] + [ANY] * ns,
        [jax.ShapeDtypeStruct((S, D_MODEL), F32), jax.ShapeDtypeStruct((1, D_MODEL), F32)] +
        [jax.ShapeDtypeStruct((3,) + _shard_shape(h.shape, n), h.dtype) for n, h in zip(scatter_names, scatter_halves)],
        scratch=[pltpu.VMEM(w.shape, BF16), pltpu.SemaphoreType.DMA(()), pltpu.VMEM((SUBLANES, D_MODEL), F32)] +
        (_scatter_sems(scatter_names) if ns else []),
        sem=("arbitrary",))(*pieces, w, x, nw, dres, *scatter_halves)
    return outs[0], outs[1], list(outs[2:])


def _mm_wgrad(at, pieces, tn, name):
    M, S = at.shape
    starts, counts, nj = _piece_blocks(pieces, tn, 1)
    npc = len(pieces)

    def body(*refs):
        a_hbm = refs[0]
        p_refs = refs[1:1 + npc]
        o_ref, o16_ref, a_v, sem = refs[1 + npc:]
        j = pl.program_id(0)
        _load_once(a_hbm, a_v, sem)
        for p in range(npc):
            @pl.when((j >= starts[p]) & (j < starts[p] + counts[p]))
            def _(p=p):
                acc = jnp.dot(a_v[...], p_refs[p][...].astype(BF16), preferred_element_type=F32)
                o_ref[...] = acc
                o16_ref[...] = acc.astype(BF16)

    def pspec(p):
        return pl.BlockSpec((S, tn), lambda j: (0, jnp.clip(j - starts[p], 0, counts[p] - 1)))

    col = pl.BlockSpec((M, tn), lambda j: (0, j))
    return _call(
        body, name, (nj,),
        [ANY] + [pspec(p) for p in range(npc)], [col, col],
        [jax.ShapeDtypeStruct((M, nj * tn), F32), jax.ShapeDtypeStruct((M, nj * tn), BF16)],
        scratch=[pltpu.VMEM(at.shape, BF16), pltpu.SemaphoreType.DMA(())], sem=("arbitrary",))(at, *pieces)


def _stream_spec(d, T):
    return pl.BlockSpec((d, T // d, A_WIDTH), lambda i: (0, i, 0))


def _stream_shape(d, S, dtype):
    return jax.ShapeDtypeStruct((d, S // d, A_WIDTH), dtype)


N_CHUNK = A_WIDTH // LANES


def _to_tokens(ref, scr, d, T):
    if d == 1:
        return ref[0].astype(F32)
    for r in range(d):
        for ch in range(N_CHUNK):
            scr.at[ch][pl.ds(r, T // d, stride=d), :] = ref[r, :, ch * LANES:(ch + 1) * LANES].astype(F32)
    return _get(scr)


def _from_tokens(out_ref, scr, d, T):
    for r in range(d):
        for ch in range(N_CHUNK):
            out_ref[r, :, ch * LANES:(ch + 1) * LANES] = scr.at[ch][pl.ds(r, T // d, stride=d), :].astype(out_ref.dtype)


def _put(scr, val):
    for ch in range(N_CHUNK):
        scr[ch] = val[:, ch * LANES:(ch + 1) * LANES]


def _get(scr):
    return jnp.concatenate([scr[ch] for ch in range(N_CHUNK)], axis=1)


def _chunked(T):
    return pltpu.VMEM((N_CHUNK, T, LANES), F32)


def _compact_spec(d, T):
    return pl.BlockSpec((d, T // d, LANES), lambda i: (0, i, 0))


def _compact_shape(d, S):
    return jax.ShapeDtypeStruct((d, S // d, LANES), F32)


def _compact_to_tokens(ref, scr, d, T):
    if d == 1:
        return ref[0]
    for r in range(d):
        scr[pl.ds(r, T // d, stride=d), :] = ref[r]
    return scr[...]


def _compact_from_tokens(out_ref, scr, val, d, T):
    if d == 1:
        out_ref[0] = val
        return
    scr[...] = val
    for r in range(d):
        out_ref[r] = scr[pl.ds(r, T // d, stride=d), :]


def _head_expander():
    r = lax.broadcasted_iota(jnp.int32, (LANES, A_WIDTH), 0)
    c = lax.broadcasted_iota(jnp.int32, (LANES, A_WIDTH), 1) // HEAD_DIM
    return (r == c).astype(BF16)


def _head_reducer():
    r = lax.broadcasted_iota(jnp.int32, (A_WIDTH, LANES), 0) // HEAD_DIM
    c = lax.broadcasted_iota(jnp.int32, (A_WIDTH, LANES), 1)
    return (r == c).astype(BF16)


def _qkv_prep(proj, pos, freq, wq, wk, T):
    S = proj.shape[0]
    qk_w = 3 * A_WIDTH

    def body(q_ref, k_ref, v_ref, pos_ref, f_ref, wq_ref, wk_ref, *rest):
        outs, tabs, scr = rest[:9], rest[9:12], rest[12]
        seg_mean = _seg_mean()
        c, s1, s2 = _rope_tables(pos_ref, f_ref)
        for tab, val in zip(tabs, (c, s1, s2)):
            tab[...] = val
        for t, (src, w_ref) in enumerate(((q_ref, wq_ref), (k_ref, wk_ref), (v_ref, None))):
            for g in range(3):
                d = A_GROUPS[g][1]
                out = outs[3 * t + g]
                for ch in range(A_WIDTH // LANES):
                    cs = slice(ch * LANES, (ch + 1) * LANES)
                    v = src[:, g * A_WIDTH + ch * LANES: g * A_WIDTH + (ch + 1) * LANES]
                    if w_ref is not None:
                        y = v * _head_rstd(v, seg_mean) * w_ref[...]
                        v = y * c + pltpu.roll(y, 8, 1) * s1 + pltpu.roll(y, LANES - 8, 1) * s2
                    if d == 1:
                        out[0, :, cs] = v.astype(BF16)
                    else:
                        scr[ch] = v
                if d > 1:
                    _from_tokens(out, scr, d, T)

    ds_ = [A_GROUPS[g][1] for g in range(3)] * 3
    return _call(
        body, "qkv_prep", (S // T,),
        [pl.BlockSpec((T, qk_w), lambda i: (i, 0)), pl.BlockSpec((T, qk_w), lambda i: (i, 1)),
         pl.BlockSpec((T, qk_w), lambda i: (i, 2)),
         pl.BlockSpec((T, 1), lambda i: (i, 0)), pl.BlockSpec((1, LANES), lambda i: (0, 0)),
         pl.BlockSpec((1, LANES), lambda i: (0, 0)), pl.BlockSpec((1, LANES), lambda i: (0, 0))],
        [_stream_spec(d, T) for d in ds_] + [pl.BlockSpec((T, LANES), lambda i: (i, 0))] * 3,
        [_stream_shape(d, S, BF16) for d in ds_] + [jax.ShapeDtypeStruct((S, LANES), F32)] * 3,
        scratch=[_chunked(T)], sem=("parallel",))(proj, proj, proj, pos, freq, wq, wk)


def _attn_mask(i):
    qi = lax.broadcasted_iota(jnp.int32, (QBLK, 2 * QBLK), 0) + QBLK
    kj = lax.broadcasted_iota(jnp.int32, (QBLK, 2 * QBLK), 1)
    dist = qi - kj
    return (dist >= 0) & (dist <= QBLK) & ((i > 0) | (kj >= QBLK))


ATT_BLK = (None, QBLK, A_WIDTH)
ATT_CBLK = (None, QBLK, LANES)


def _first_head_lanes():
    return lax.broadcasted_iota(jnp.int32, (1, LANES), 1) < HEAD_DIM


def _split_heads(v, first):
    zero = jnp.zeros_like(v)
    return jnp.where(first, v, zero), jnp.where(first, zero, v)


def _attn_fwd(q, k, v, g):
    d, n, _ = q.shape
    nb = n // QBLK

    def body(q_ref, kp_ref, kc_ref, vp_ref, vc_ref, o_ref, l_ref, s_scr, p_scr):
        i = pl.program_id(1)
        mask = _attn_mask(i)
        first = _first_head_lanes()
        for pr in range(A_HEADS // 2):
            ps = slice(pr * LANES, (pr + 1) * LANES)
            kc = jnp.concatenate([kp_ref[:, ps], kc_ref[:, ps]], axis=0)
            for e, qh in enumerate(_split_heads(q_ref[:, ps], first)):
                s_scr[2 * pr + e] = lax.dot_general(qh, kc, NT_DIMS, preferred_element_type=F32)
        lane = lax.broadcasted_iota(jnp.int32, (1, LANES), 1)
        lrow = jnp.zeros((QBLK, LANES), F32)
        for h in range(A_HEADS):
            s = jnp.where(mask, s_scr[h] * (HEAD_DIM ** -0.5), NEG)
            m = jnp.max(s, axis=-1, keepdims=True)
            p = jnp.exp(s - m)
            den = jnp.sum(p, axis=-1, keepdims=True)
            p_scr[h] = (p / den).astype(BF16)
            lrow = jnp.where(lane == h, m + jnp.log(den), lrow)
        l_ref[...] = lrow
        for pr in range(A_HEADS // 2):
            ps = slice(pr * LANES, (pr + 1) * LANES)
            va, vb = _split_heads(jnp.concatenate([vp_ref[:, ps], vc_ref[:, ps]], axis=0), first)
            o_ref[:, ps] = (jnp.dot(p_scr[2 * pr], va, preferred_element_type=F32) +
                            jnp.dot(p_scr[2 * pr + 1], vb, preferred_element_type=F32)).astype(BF16)

    prev = lambda r, i: (r, jnp.maximum(i - 1, 0), 0)
    cur = lambda r, i: (r, i, 0)
    return _call(
        body, "attn_fwd_g%d" % g, (d, nb),
        [pl.BlockSpec(ATT_BLK, cur), pl.BlockSpec(ATT_BLK, prev), pl.BlockSpec(ATT_BLK, cur),
         pl.BlockSpec(ATT_BLK, prev), pl.BlockSpec(ATT_BLK, cur)],
        [pl.BlockSpec(ATT_BLK, cur), pl.BlockSpec(ATT_CBLK, cur)],
        [jax.ShapeDtypeStruct((d, n, A_WIDTH), BF16), jax.ShapeDtypeStruct((d, n, LANES), F32)],
        scratch=[pltpu.VMEM((A_HEADS, QBLK, 2 * QBLK), F32), pltpu.VMEM((A_HEADS, QBLK, 2 * QBLK), BF16)],
        sem=("parallel", "parallel"))(q, k, k, v, v)


def _attn_bwd(q, k, v, do, lse, cg, g):
    d, n, _ = q.shape
    nb = n // QBLK
    scale = HEAD_DIM ** -0.5

    def body(q_ref, kp_ref, kc_ref, vp_ref, vc_ref, do_ref, l_ref, c_ref, dq_ref, dk_ref, dv_ref, ck, cv,
             s_scr, dp_scr, p_scr, ds_scr):
        i = pl.program_id(1)

        @pl.when(i == 0)
        def _():
            ck[...] = jnp.zeros_like(ck)
            cv[...] = jnp.zeros_like(cv)

        @pl.when(i < nb)
        def _():
            mask = _attn_mask(i)
            first = _first_head_lanes()
            for pr in range(A_HEADS // 2):
                ps = slice(pr * LANES, (pr + 1) * LANES)
                kc = jnp.concatenate([kp_ref[:, ps], kc_ref[:, ps]], axis=0)
                vc = jnp.concatenate([vp_ref[:, ps], vc_ref[:, ps]], axis=0)
                qs = _split_heads(q_ref[:, ps], first)
                dos = _split_heads(do_ref[:, ps], first)
                for e in range(2):
                    s_scr[2 * pr + e] = lax.dot_general(qs[e], kc, NT_DIMS, preferred_element_type=F32)
                    dp_scr[2 * pr + e] = lax.dot_general(dos[e], vc, NT_DIMS, preferred_element_type=F32)
            for h in range(A_HEADS):
                p = jnp.where(mask, jnp.exp(s_scr[h] * scale - l_ref[:, h:h + 1]), 0.0)
                p_scr[h] = p.astype(BF16)
                ds_scr[h] = (p * (dp_scr[h] + c_ref[:, h:h + 1]) * scale).astype(BF16)
            for pr in range(A_HEADS // 2):
                ps = slice(pr * LANES, (pr + 1) * LANES)
                ks = _split_heads(jnp.concatenate([kp_ref[:, ps], kc_ref[:, ps]], axis=0), first)
                qs = _split_heads(q_ref[:, ps], first)
                dos = _split_heads(do_ref[:, ps], first)
                dq = dkc = dvc = None
                for e in range(2):
                    ds = ds_scr[2 * pr + e]
                    a = jnp.dot(ds, ks[e], preferred_element_type=F32)
                    b = lax.dot_general(ds, qs[e], TN_DIMS, preferred_element_type=F32)
                    c = lax.dot_general(p_scr[2 * pr + e], dos[e], TN_DIMS, preferred_element_type=F32)
                    dq, dkc, dvc = (a, b, c) if e == 0 else (dq + a, dkc + b, dvc + c)
                dq_ref[:, ps] = dq.astype(BF16)
                dk_ref[:, ps] = (ck[:, ps] + dkc[:QBLK]).astype(BF16)
                dv_ref[:, ps] = (cv[:, ps] + dvc[:QBLK]).astype(BF16)
                ck[:, ps] = dkc[QBLK:]
                cv[:, ps] = dvc[QBLK:]

        @pl.when(i == nb)
        def _():
            dk_ref[...] = ck[...].astype(BF16)
            dv_ref[...] = cv[...].astype(BF16)

    qi = lambda i: jnp.minimum(i, nb - 1)
    cur = lambda r, i: (r, qi(i), 0)
    prev = lambda r, i: (r, jnp.maximum(qi(i) - 1, 0), 0)
    late = lambda r, i: (r, jnp.maximum(i - 1, 0), 0)
    return _call(
        body, "attn_bwd_g%d" % g, (d, nb + 1),
        [pl.BlockSpec(ATT_BLK, cur), pl.BlockSpec(ATT_BLK, prev), pl.BlockSpec(ATT_BLK, cur),
         pl.BlockSpec(ATT_BLK, prev), pl.BlockSpec(ATT_BLK, cur),
         pl.BlockSpec(ATT_BLK, cur), pl.BlockSpec(ATT_CBLK, cur), pl.BlockSpec(ATT_CBLK, cur)],
        [pl.BlockSpec(ATT_BLK, cur), pl.BlockSpec(ATT_BLK, late), pl.BlockSpec(ATT_BLK, late)],
        [jax.ShapeDtypeStruct((d, n, A_WIDTH), BF16)] * 3,
        scratch=[pltpu.VMEM((QBLK, A_WIDTH), F32), pltpu.VMEM((QBLK, A_WIDTH), F32),
                 pltpu.VMEM((A_HEADS, QBLK, 2 * QBLK), F32), pltpu.VMEM((A_HEADS, QBLK, 2 * QBLK), F32),
                 pltpu.VMEM((A_HEADS, QBLK, 2 * QBLK), BF16), pltpu.VMEM((A_HEADS, QBLK, 2 * QBLK), BF16)],
        sem=("parallel", "arbitrary"))(q, k, k, v, v, do, lse, cg)


def _merge_weights(l0, l1, l2):
    mx = jnp.maximum(jnp.maximum(l0, l1), l2)
    e0, e1, e2 = jnp.exp(l0 - mx), jnp.exp(l1 - mx), jnp.exp(l2 - mx)
    den = e0 + e1 + e2
    return e0 / den, e1 / den, e2 / den


def _even_specs(T, S):
    t8 = T // SUBLANES
    last8 = S // SUBLANES - 1
    col = lambda c: pl.BlockSpec((T, A_WIDTH), lambda i: (i, c))
    prev8 = lambda c: pl.BlockSpec((SUBLANES, A_WIDTH), lambda i: (jnp.maximum(i * t8 - 1, 0), c))
    next8 = lambda c: pl.BlockSpec((SUBLANES, A_WIDTH), lambda i: (jnp.minimum((i + 1) * t8, last8), c))
    return col, prev8, next8


GROUP_D = tuple(d for _, d in A_GROUPS)


def _even_mixer_fwd(x, proj, os_, ls_, conv_w, w_out, T):
    S = proj.shape[0]
    col, prev8, _ = _even_specs(T, S)
    H = SUBLANES

    def body(x_ref, w_ref, bg_r, cg_r, hb_r, zl_r, zh_r, cgp_r, hbp_r, o0, o1, o2, l0, l1, l2, cw_r,
             x1_ref, ut_ref, ext, cscr, *scr):
        i = pl.program_id(0)
        ls = [_compact_to_tokens(r, cscr, GROUP_D[g], T) for g, r in enumerate((l0, l1, l2))]
        expand = _head_expander()
        ws = [_segsum(w, expand) for w in _merge_weights(*ls)]
        oa = ws[0] * _to_tokens(o0, scr[0], GROUP_D[0], T)
        oa = oa + ws[1] * _to_tokens(o1, scr[1], GROUP_D[1], T)
        oa = oa + ws[2] * _to_tokens(o2, scr[2], GROUP_D[2], T)
        ext[0:H, :] = jnp.where(i == 0, 0.0, cgp_r[...] * hbp_r[...])
        ext[H:H + T, :] = cg_r[...] * hb_r[...]
        conv = cw_r[0:1, :] * ext[H - 2:H - 2 + T, :]
        for kk in range(1, SC_WIDTH):
            conv = conv + cw_r[kk:kk + 1, :] * ext[H - 2 + kk:H - 2 + kk + T, :]
        zl, zh = zl_r[...], zh_r[...]
        x1_ref[...] = x_ref[...] + _out_projection(ut_ref, w_ref, oa * (zl * _sig(zl)),
                                                   bg_r[...] * conv * (zh * _sig(zh)))

    streams = [_stream_spec(d, T) for d in GROUP_D]
    compacts = [_compact_spec(d, T) for d in GROUP_D]
    row = pl.BlockSpec((T, D_MODEL), lambda i: (i, 0))
    return _call(
        body, "even_mixer_fwd", (S // T,),
        [row, pl.BlockSpec((D_MODEL, D_MODEL), lambda i: (0, 0)),
         col(9), col(10), col(11), col(12), col(13), prev8(10), prev8(11)] + streams + compacts +
        [pl.BlockSpec((SC_WIDTH, A_WIDTH), lambda i: (0, 0))],
        [row, pl.BlockSpec((D_MODEL, T), lambda i: (0, i))],
        [jax.ShapeDtypeStruct((S, D_MODEL), F32), jax.ShapeDtypeStruct((D_MODEL, S), BF16)],
        scratch=[pltpu.VMEM((T + H, A_WIDTH), F32), pltpu.VMEM((T, LANES), F32)] + [_chunked(T)] * 3,
        sem=("parallel",))(
            x, w_out, proj, proj, proj, proj, proj, proj, proj, *os_, *ls_, conv_w)


def _even_mixer_bwd(dy, w_out, proj, os_, ls_, conv_w, T):
    S = proj.shape[0]
    nt = S // T
    col, prev8, next8 = _even_specs(T, S)
    H = SUBLANES
    t8 = T // SUBLANES
    last8 = S // SUBLANES - 1

    def body(dy_r, dyn_r, w_ref, bg_r, cg_r, hb_r, zl_r, zh_r, cgp_r, hbp_r, zhn_r, bgn_r,
             o0, o1, o2, l0, l1, l2, cw_r,
             do0, do1, do2, c0, c1, c2, dr_ref, dcw_ref, ext_t, ext_d, acc, cscr, s_a, s_b, s_c):
        i = pl.program_id(0)

        @pl.when(i == 0)
        def _():
            acc[...] = jnp.zeros_like(acc)

        zl, zh = zl_r[...], zh_r[...]
        sl, sh = _sig(zl), _sig(zh)
        du = lax.dot_general(dy_r[...].astype(BF16), w_ref[...], NT_DIMS, preferred_element_type=F32)
        dul, duh = du[:, 0:A_WIDTH], du[:, A_WIDTH:]
        dun = lax.dot_general(dyn_r[...].astype(BF16), w_ref[A_WIDTH:, :], NT_DIMS, preferred_element_type=F32)
        scr = (s_a, s_b, s_c)
        ls = [_compact_to_tokens(r, cscr, GROUP_D[g], T) for g, r in enumerate((l0, l1, l2))]
        wcs = _merge_weights(*ls)
        expand = _head_expander()
        ws = [_segsum(w, expand) for w in wcs]
        oa = ws[0] * _to_tokens(o0, scr[0], GROUP_D[0], T)
        oa = oa + ws[1] * _to_tokens(o1, scr[1], GROUP_D[1], T)
        oa = oa + ws[2] * _to_tokens(o2, scr[2], GROUP_D[2], T)
        doa = dul * (zl * sl)
        rsum = _segsum(doa * oa, _head_reducer())
        for g, (do_ref, c_ref) in enumerate(((do0, c0), (do1, c1), (do2, c2))):
            d = GROUP_D[g]
            _compact_from_tokens(c_ref, cscr, -wcs[g] * rsum, d, T)
            if d == 1:
                do_ref[0] = (ws[g] * doa).astype(BF16)
            else:
                _put(s_c, ws[g] * doa)
                _from_tokens(do_ref, s_c, d, T)
        cgv, hbv, bgv = cg_r[...], hb_r[...], bg_r[...]
        ext_t[0:H, :] = jnp.where(i == 0, 0.0, cgp_r[...] * hbp_r[...])
        ext_t[H:H + T, :] = cgv * hbv
        conv = cw_r[0:1, :] * ext_t[H - 2:H - 2 + T, :]
        for kk in range(1, SC_WIDTH):
            conv = conv + cw_r[kk:kk + 1, :] * ext_t[H - 2 + kk:H - 2 + kk + T, :]
        dyb = duh * (zh * sh)
        dconv = dyb * bgv
        zn = zhn_r[...]
        ext_d[0:T, :] = dconv
        ext_d[T:T + H, :] = jnp.where(i == nt - 1, 0.0, dun * (zn * _sig(zn)) * bgn_r[...])
        dt = cw_r[0:1, :] * ext_d[2:2 + T, :]
        for kk in range(1, SC_WIDTH):
            dt = dt + cw_r[kk:kk + 1, :] * ext_d[2 - kk:2 - kk + T, :]
        for kk in range(SC_WIDTH):
            acc[kk * SUBLANES:(kk + 1) * SUBLANES, :] += _cs8(dconv * ext_t[H - 2 + kk:H - 2 + kk + T, :])
        dr_ref[:, 0:A_WIDTH] = (dyb * conv).astype(BF16)
        dr_ref[:, A_WIDTH:2 * A_WIDTH] = (dt * hbv).astype(BF16)
        dr_ref[:, 2 * A_WIDTH:3 * A_WIDTH] = (dt * cgv).astype(BF16)
        dr_ref[:, 3 * A_WIDTH:4 * A_WIDTH] = (dul * oa * _dsilu(zl, sl)).astype(BF16)
        dr_ref[:, 4 * A_WIDTH:5 * A_WIDTH] = (duh * (bgv * conv) * _dsilu(zh, sh)).astype(BF16)

        @pl.when(i == nt - 1)
        def _():
            for kk in range(SC_WIDTH):
                dcw_ref[kk:kk + 1, :] = jnp.sum(acc[kk * SUBLANES:(kk + 1) * SUBLANES, :], axis=0, keepdims=True)

    streams = [_stream_spec(d, T) for d in GROUP_D]
    dynext = pl.BlockSpec((SUBLANES, D_MODEL), lambda i: (jnp.minimum((i + 1) * t8, last8), 0))
    compacts = [_compact_spec(d, T) for d in GROUP_D]
    outs = _call(
        body, "even_mixer_bwd", (nt,),
        [pl.BlockSpec((T, D_MODEL), lambda i: (i, 0)), dynext, pl.BlockSpec((D_MODEL, D_MODEL), lambda i: (0, 0)),
         col(9), col(10), col(11), col(12), col(13), prev8(10), prev8(11), next8(13), next8(9)] +
        streams + compacts + [pl.BlockSpec((SC_WIDTH, A_WIDTH), lambda i: (0, 0))],
        streams + compacts + [pl.BlockSpec((T, 5 * A_WIDTH), lambda i: (i, 0)),
                              pl.BlockSpec((SC_WIDTH, A_WIDTH), lambda i: (0, 0))],
        [_stream_shape(d, S, BF16) for d in GROUP_D] + [_compact_shape(d, S) for d in GROUP_D] +
        [jax.ShapeDtypeStruct((S, 5 * A_WIDTH), BF16), jax.ShapeDtypeStruct((SC_WIDTH, A_WIDTH), F32)],
        scratch=[pltpu.VMEM((T + H, A_WIDTH), F32), pltpu.VMEM((T + H, A_WIDTH), F32),
                 pltpu.VMEM((SC_WIDTH * SUBLANES, A_WIDTH), F32), pltpu.VMEM((T, LANES), F32)] +
                [_chunked(T)] * 3,
        sem=("arbitrary",))(dy, dy, w_out, proj, proj, proj, proj, proj, proj, proj, proj, proj, *os_, *ls_, conv_w)
    return outs[0:3], outs[3:6], outs[6], outs[7]


def _qk_bwd(proj, dqs, dks, dvs, rope, wq, wk, T):
    S = proj.shape[0]
    nt = S // T
    qk_w = 3 * A_WIDTH

    def body(q_ref, k_ref, dq0, dq1, dq2, dk0, dk1, dk2, dv0, dv1, dv2, c_ref, s1_ref, s2_ref, wq_ref, wk_ref,
             o_ref, dw_ref, acc, scr):
        i = pl.program_id(0)

        @pl.when(i == 0)
        def _():
            acc[...] = jnp.zeros_like(acc)
            dw_ref[...] = jnp.zeros_like(dw_ref)

        seg_mean = _seg_mean()
        c, s1, s2 = c_ref[...], s1_ref[...], s2_ref[...]
        for t, (src, w_ref, ds) in enumerate(((q_ref, wq_ref, (dq0, dq1, dq2)), (k_ref, wk_ref, (dk0, dk1, dk2)))):
            wv = w_ref[...]
            for g in range(3):
                d = GROUP_D[g]
                if d > 1:
                    _to_tokens(ds[g], scr, d, T)
                for ch in range(A_WIDTH // LANES):
                    cs = slice(g * A_WIDTH + ch * LANES, g * A_WIDTH + (ch + 1) * LANES)
                    lc = slice(ch * LANES, (ch + 1) * LANES)
                    v = src[:, cs]
                    dout = ds[g][0, :, lc].astype(F32) if d == 1 else scr[ch]
                    rs = lax.rsqrt(_segsum(v * v, seg_mean) + EPS)
                    xh = v * rs
                    dy = dout * c + pltpu.roll(dout * s1, LANES - 8, 1) + pltpu.roll(dout * s2, 8, 1)
                    acc[t * SUBLANES:(t + 1) * SUBLANES, :] += _cs8(dy * xh)
                    dxh = dy * wv
                    mean = _segsum(dxh * xh, seg_mean)
                    o_ref[:, t * qk_w + g * A_WIDTH + ch * LANES: t * qk_w + g * A_WIDTH + (ch + 1) * LANES] = (
                        rs * (dxh - xh * mean)).astype(BF16)
        for g, dv in enumerate((dv0, dv1, dv2)):
            d = GROUP_D[g]
            base = 2 * qk_w + g * A_WIDTH
            o_ref[:, base:base + A_WIDTH] = _to_tokens(dv, scr, d, T).astype(BF16)

        @pl.when(i == nt - 1)
        def _():
            for t in range(2):
                srow = jnp.sum(acc[t * SUBLANES:(t + 1) * SUBLANES, :], axis=0, keepdims=True)
                dw_ref[t:t + 1, :] = srow + pltpu.roll(srow, HEAD_DIM, 1)

    streams = [_stream_spec(d, T) for d in GROUP_D]
    return _call(
        body, "qk_bwd", (nt,),
        [pl.BlockSpec((T, qk_w), lambda i: (i, 0)), pl.BlockSpec((T, qk_w), lambda i: (i, 1))] + streams * 3 +
        [pl.BlockSpec((T, LANES), lambda i: (i, 0))] * 3 +
        [pl.BlockSpec((1, LANES), lambda i: (0, 0)), pl.BlockSpec((1, LANES), lambda i: (0, 0))],
        [pl.BlockSpec((T, 3 * qk_w), lambda i: (i, 0)), pl.BlockSpec((SUBLANES, LANES), lambda i: (0, 0))],
        [jax.ShapeDtypeStruct((S, 3 * qk_w), BF16), jax.ShapeDtypeStruct((SUBLANES, LANES), F32)],
        scratch=[pltpu.VMEM((2 * SUBLANES, LANES), F32), _chunked(T)], sem=("arbitrary",))(
            proj, proj, *dqs, *dks, *dvs, *rope, wq, wk)


N_SMALL_ODD = 40
SHIFT_ROWS_LESS = SUBLANES


def _fill_shifted(ext_ref, sh_ref):
    rows = ext_ref.shape[0] - SHIFT_ROWS_LESS
    for b in range(1, SUBLANES):
        sh_ref[b - 1] = ext_ref[b:b + rows, :]


def _window(ext_ref, sh_ref, off, T):
    a, b = divmod(off, SUBLANES)
    if b == 0:
        return ext_ref[off:off + T, :]
    return sh_ref[b - 1, a * SUBLANES:a * SUBLANES + T, :]


def _odd_pool_tile(i, uc_r, ucp_r, pw_r, ext_u, pooled_s, pm_s, T):
    H = HALO
    uc = uc_r[...]
    ext_u[0:H, :] = jnp.where(i == 0, 0.0, ucp_r[...])
    ext_u[H:H + T, :] = uc
    row = i * T + lax.broadcasted_iota(jnp.int32, (T, 1), 0)
    for g, p in enumerate(POOL_SIZES):
        cs = slice(g * LANES, (g + 1) * LANES)
        win = ext_u[H:H + T, cs]
        for j in range(1, p):
            win = win + ext_u[H - j:H - j + T, cs]
        cnt = jnp.minimum(row + 1, p).astype(F32)
        pooled = win / cnt - uc[:, cs]
        pooled_s[:, cs] = pooled
        pm_s[:, cs] = jnp.dot(pooled.astype(BF16), pw_r[g].astype(BF16), preferred_element_type=F32)
    return row


def _odd_glu_tile(i, da_r, dg_r, dap_r, dgp_r, ext_g, sh_g, T):
    H = HALO
    ext_g[0:H, :] = jnp.where(i == 0, 0.0, dap_r[...] * _sig(dgp_r[...]))
    ext_g[H:H + T, :] = da_r[...] * _sig(dg_r[...])
    _fill_shifted(ext_g, sh_g)


def _odd_specs(T, S, order):
    tb = T // HALO
    col = lambda c: pl.BlockSpec((T, A_WIDTH), lambda s: (order(s), c))
    prev = lambda c: pl.BlockSpec((HALO, A_WIDTH), lambda s: (jnp.maximum(order(s) * tb - 1, 0), c))
    const2 = lambda shape: pl.BlockSpec(shape, lambda s: (0, 0))
    weights = [pl.BlockSpec((4, LANES, LANES), lambda s: (0, 0, 0)), const2((1, A_WIDTH)),
               const2((D_CONV, A_WIDTH)), const2((1, A_WIDTH)), const2((1, A_WIDTH)), const2((1, A_WIDTH))]
    return col, prev, weights


def _odd_mixer_fwd(x, tgt, proj, pool_w, scale, dconv_w, dconv_b, ln_w, ln_b, w_out, T):
    S = proj.shape[0]
    nt = S // T
    col, prev, wspecs = _odd_specs(T, S, lambda s: s)
    H = HALO

    def body(x_ref, t_ref, w_ref, uc_r, da_r, dg_r, zl_r, zh_r, ucp_r, dap_r, dgp_r, pw_r, sc_r, dw_r, db_r,
             lw_r, lb_r, dy_ref, l_ref, ut_ref, cv_ref, ext_u, ext_g, sh_g, pooled_s, pm_s, lacc):
        i = pl.program_id(0)

        @pl.when(i == 0)
        def _():
            lacc[...] = jnp.zeros_like(lacc)

        _odd_pool_tile(i, uc_r, ucp_r, pw_r, ext_u, pooled_s, pm_s, T)
        _odd_glu_tile(i, da_r, dg_r, dap_r, dgp_r, ext_g, sh_g, T)
        base = H - (D_CONV - 1)
        conv = db_r[...] + dw_r[0:1, :] * _window(ext_g, sh_g, base, T)
        for kk in range(1, D_CONV):
            conv = conv + dw_r[kk:kk + 1, :] * _window(ext_g, sh_g, base + kk, T)
        cv_ref[...] = conv
        mu = jnp.mean(conv, axis=-1, keepdims=True)
        xc = conv - mu
        yh = xc * lax.rsqrt(jnp.mean(xc * xc, axis=-1, keepdims=True) + EPS)
        ln = yh * lw_r[...] + lb_r[...]
        zl, zh = zl_r[...], zh_r[...]
        y = x_ref[...] + _out_projection(ut_ref, w_ref, pm_s[...] * sc_r[...] * (zl * _sig(zl)),
                                         ln * _sig(ln) * (zh * _sig(zh)))
        diff = y - t_ref[...]
        dy_ref[...] = diff / float(D_MODEL)
        lacc[...] += _cs8(diff * diff)

        @pl.when(i == nt - 1)
        def _():
            l_ref[...] = jnp.sum(lacc[...], axis=0, keepdims=True)

    row = pl.BlockSpec((T, D_MODEL), lambda i: (i, 0))
    return _call(
        body, "odd_mixer_fwd", (nt,),
        [row, row, pl.BlockSpec((D_MODEL, D_MODEL), lambda i: (0, 0)),
         col(0), col(1), col(2), col(3), col(4), prev(0), prev(1), prev(2)] + wspecs,
        [row, pl.BlockSpec((1, D_MODEL), lambda i: (0, 0)), pl.BlockSpec((D_MODEL, T), lambda i: (0, i)),
         pl.BlockSpec((T, A_WIDTH), lambda i: (i, 0))],
        [jax.ShapeDtypeStruct((S, D_MODEL), F32), jax.ShapeDtypeStruct((1, D_MODEL), F32),
         jax.ShapeDtypeStruct((D_MODEL, S), BF16), jax.ShapeDtypeStruct((S, A_WIDTH), F32)],
        scratch=[pltpu.VMEM((T + H, A_WIDTH), F32), pltpu.VMEM((T + H, A_WIDTH), F32),
                 pltpu.VMEM((SUBLANES - 1, T + H - SHIFT_ROWS_LESS, A_WIDTH), F32),
                 pltpu.VMEM((T, A_WIDTH), F32), pltpu.VMEM((T, A_WIDTH), F32),
                 pltpu.VMEM((SUBLANES, D_MODEL), F32)],
        sem=("arbitrary",))(x, tgt, w_out, proj, proj, proj, proj, proj, proj, proj, proj,
                            pool_w, scale, dconv_w, dconv_b, ln_w, ln_b)


def _odd_mixer_bwd(dy, w_out, proj, conv, pool_w, scale, dconv_w, dconv_b, ln_w, ln_b, T):
    S = proj.shape[0]
    nt = S // T
    order = lambda s: nt - 1 - s
    col, prev, wspecs = _odd_specs(T, S, order)
    H = HALO

    def body(dy_r, w_ref, cv_r, uc_r, da_r, dg_r, zl_r, zh_r, ucp_r, dap_r, dgp_r, pw_r, sc_r, dw_r, db_r, lw_r, lb_r,
             dp_ref, dpw_ref, sm_ref, ext_u, ext_g, sh_g, pooled_s, pm_s, dpl_s, ext_p, ext_c, sh_c, acc):
        step = pl.program_id(0)
        i = nt - 1 - step

        @pl.when(step == 0)
        def _():
            ext_p[T:T + H, :] = jnp.zeros((H, A_WIDTH), F32)
            ext_c[T:T + H, :] = jnp.zeros((H, A_WIDTH), F32)
            acc[...] = jnp.zeros_like(acc)
            dpw_ref[...] = jnp.zeros_like(dpw_ref)

        def accum(r, v):
            acc[r * SUBLANES:(r + 1) * SUBLANES, :] += _cs8(v)

        row = _odd_pool_tile(i, uc_r, ucp_r, pw_r, ext_u, pooled_s, pm_s, T)
        _odd_glu_tile(i, da_r, dg_r, dap_r, dgp_r, ext_g, sh_g, T)
        conv = cv_r[...]
        mu = jnp.mean(conv, axis=-1, keepdims=True)
        xc = conv - mu
        rstd = lax.rsqrt(jnp.mean(xc * xc, axis=-1, keepdims=True) + EPS)
        yh = xc * rstd
        ln = yh * lw_r[...] + lb_r[...]
        sln = _sig(ln)
        zl, zh = zl_r[...], zh_r[...]
        sl, sh = _sig(zl), _sig(zh)
        du = lax.dot_general(dy_r[...].astype(BF16), w_ref[...], NT_DIMS, preferred_element_type=F32)
        dul, duh = du[:, 0:A_WIDTH], du[:, A_WIDTH:]
        pm = pm_s[...]
        scv = sc_r[...]
        dyc = dul * (zl * sl)
        accum(34, dyc * pm)
        dpm = dyc * scv
        for g in range(len(POOL_SIZES)):
            cs = slice(g * LANES, (g + 1) * LANES)
            dpm_g = dpm[:, cs].astype(BF16)
            dpw_ref[g] += lax.dot_general(pooled_s[:, cs].astype(BF16), dpm_g, TN_DIMS, preferred_element_type=F32)
            dpl_s[:, cs] = lax.dot_general(dpm_g, pw_r[g].astype(BF16), NT_DIMS, preferred_element_type=F32)
        lane_p = lax.broadcasted_iota(jnp.int32, (1, A_WIDTH), 1) // LANES
        pvec = jnp.left_shift(2, lane_p)
        cnt = jnp.minimum(row + 1, pvec).astype(F32)
        dpl = dpl_s[...]
        ext_p[0:T, :] = dpl / cnt
        for g, p in enumerate(POOL_SIZES):
            cs = slice(g * LANES, (g + 1) * LANES)
            win = ext_p[0:T, cs]
            for j in range(1, p):
                win = win + ext_p[j:j + T, cs]
            dp_ref[:, cs] = (win - dpl[:, cs]).astype(BF16)
        ext_p[T:T + H, :] = ext_p[0:H, :]
        dln = duh * (zh * sh) * _dsilu(ln, sln)
        accum(32, dln * yh)
        accum(33, dln)
        dyh = dln * lw_r[...]
        dc = rstd * (dyh - jnp.mean(dyh, axis=-1, keepdims=True) - yh * jnp.mean(dyh * yh, axis=-1, keepdims=True))
        accum(31, dc)
        ext_c[0:T, :] = dc
        _fill_shifted(ext_c, sh_c)
        base = H - (D_CONV - 1)
        dgl = dw_r[0:1, :] * _window(ext_c, sh_c, D_CONV - 1, T)
        accum(0, dc * _window(ext_g, sh_g, base, T))
        for kk in range(1, D_CONV):
            dgl = dgl + dw_r[kk:kk + 1, :] * _window(ext_c, sh_c, D_CONV - 1 - kk, T)
            accum(kk, dc * _window(ext_g, sh_g, base + kk, T))
        ext_c[T:T + H, :] = ext_c[0:H, :]
        dav, dgv = da_r[...], dg_r[...]
        sg = _sig(dgv)
        dp_ref[:, A_WIDTH:2 * A_WIDTH] = (dgl * sg).astype(BF16)
        dp_ref[:, 2 * A_WIDTH:3 * A_WIDTH] = (dgl * dav * sg * (1.0 - sg)).astype(BF16)
        dp_ref[:, 3 * A_WIDTH:4 * A_WIDTH] = (dul * (pm * scv) * _dsilu(zl, sl)).astype(BF16)
        dp_ref[:, 4 * A_WIDTH:5 * A_WIDTH] = (duh * (ln * sln) * _dsilu(zh, sh)).astype(BF16)

        @pl.when(step == nt - 1)
        def _():
            for r in range(N_SMALL_ODD):
                sm_ref[r:r + 1, :] = jnp.sum(acc[r * SUBLANES:(r + 1) * SUBLANES, :], axis=0, keepdims=True)

    ext = pltpu.VMEM((T + H, A_WIDTH), F32)
    shifted = pltpu.VMEM((SUBLANES - 1, T + H - SHIFT_ROWS_LESS, A_WIDTH), F32)
    tile = pltpu.VMEM((T, A_WIDTH), F32)
    return _call(
        body, "odd_mixer_bwd", (nt,),
        [pl.BlockSpec((T, D_MODEL), lambda s: (order(s), 0)), pl.BlockSpec((D_MODEL, D_MODEL), lambda s: (0, 0)),
         pl.BlockSpec((T, A_WIDTH), lambda s: (order(s), 0)),
         col(0), col(1), col(2), col(3), col(4), prev(0), prev(1), prev(2)] + wspecs,
        [pl.BlockSpec((T, ODD_IN), lambda s: (order(s), 0)),
         pl.BlockSpec((4, LANES, LANES), lambda s: (0, 0, 0)),
         pl.BlockSpec((N_SMALL_ODD, A_WIDTH), lambda s: (0, 0))],
        [jax.ShapeDtypeStruct((S, ODD_IN), BF16), jax.ShapeDtypeStruct((4, LANES, LANES), F32),
         jax.ShapeDtypeStruct((N_SMALL_ODD, A_WIDTH), F32)],
        scratch=[ext, ext, shifted, tile, tile, tile, ext, ext, shifted,
                 pltpu.VMEM((N_SMALL_ODD * SUBLANES, A_WIDTH), F32)],
        sem=("arbitrary",))(dy, w_out, conv, proj, proj, proj, proj, proj, proj, proj, proj,
                            pool_w, scale, dconv_w, dconv_b, ln_w, ln_b)


TILE_SEQ = 256
TILE_WG = 256
TILE_FIRST = 512


SMALL_PACK = "small_pack"
SMALL_PACK_W = 2 * LANES
LATE_WEIGHTS = ("e_w_out", "o_w_in", "o_w_out", SMALL_PACK)
ODD_MATS = ("o_w_in", "o_w_out")
EVEN_MATS = ("e_w_in", "e_w_out")


def _reduce_start(names, grads, grads16, cidx):
    recv = _swap_to_sibling(names, [grads16[n] for n in names], "swap_halves_" + names[0][0], True)
    both = [_add_half(cidx, grads[n], r, n) for n, r in zip(names, recv)]
    return [h for h, _ in both], [hb for _, hb in both]


def _local_step(x, pos, tgt, shards, p, unpack_small, cidx, bidx):
    T = TILE_SEQ
    freq = _freq_table()
    wq = jnp.tile(p["e_q_norm_w"], (1, LANES // HEAD_DIM))
    wk = jnp.tile(p["e_k_norm_w"], (1, LANES // HEAD_DIM))

    proj_e, ht_e, w_e_in, late = _inproj_gathering(x, p["e_norm_w"], shards["e_w_in"], bidx, "e_w_in", LATE_WEIGHTS,
                                                   [shards[n] for n in LATE_WEIGHTS], TILE_FIRST, "inproj_even")
    wb = dict(zip(LATE_WEIGHTS, late), e_w_in=w_e_in)
    p = dict(p, **unpack_small(wb[SMALL_PACK]))
    qkv = _qkv_prep(proj_e, pos, freq, wq, wk, T)
    qs, ks, vs, rope = qkv[0:3], qkv[3:6], qkv[6:9], qkv[9:12]
    os_, ls_ = [], []
    for g in range(3):
        o, l = _attn_fwd(qs[g], ks[g], vs[g], g)
        os_.append(o)
        ls_.append(l)
    x1, ut_e = _even_mixer_fwd(x, proj_e, os_, ls_, p["e_conv_w"], wb["e_w_out"], T)
    proj_o, ht_o = _inproj(x1, p["o_norm_w"], wb["o_w_in"], TILE_SEQ, 1280, "inproj_odd")
    odd_w = (p["o_pool_w"], p["o_pool_scale"], p["o_dconv_w"], p["o_dconv_b"], p["o_ln_w"], p["o_ln_b"])
    dy, lsum, ut_o, conv_o = _odd_mixer_fwd(x1, tgt, proj_o, *odd_w, wb["o_w_out"], T)

    g, g16 = {}, {}
    g["o_w_out"], g16["o_w_out"] = _mm_wgrad(ut_o, [dy], TILE_WG, "wgrad_o_out")
    dproj_o, g["o_pool_w"], small_o = _odd_mixer_bwd(dy, wb["o_w_out"], proj_o, conv_o, *odd_w, T)
    g["o_w_in"], g16["o_w_in"] = _mm_wgrad(ht_o, [dproj_o], TILE_WG, "wgrad_o_in")
    half_o, half_o16 = _reduce_start(ODD_MATS, g, g16, cidx)
    dx1, g["o_norm_w"], blocks_o = _mm_nt_rms([dproj_o], wb["o_w_in"], x1, p["o_norm_w"], dy, TILE_SEQ, "dx_odd",
                                              ODD_MATS, half_o16)
    g["o_dconv_w"] = small_o[0:D_CONV]
    g["o_dconv_b"] = small_o[31:32]
    g["o_ln_w"] = small_o[32:33]
    g["o_ln_b"] = small_o[33:34]
    g["o_pool_scale"] = small_o[34:35]

    g["e_w_out"], g16["e_w_out"] = _mm_wgrad(ut_e, [dx1], TILE_WG, "wgrad_e_out")
    dos, cgs, drest, g["e_conv_w"] = _even_mixer_bwd(dx1, wb["e_w_out"], proj_e, os_, ls_, p["e_conv_w"], T)
    dqs, dks, dvs = [], [], []
    for gi in range(3):
        dq, dk, dv = _attn_bwd(qs[gi], ks[gi], vs[gi], dos[gi], ls_[gi], cgs[gi], gi)
        dqs.append(dq)
        dks.append(dk)
        dvs.append(dv)
    dqkv, dnw = _qk_bwd(proj_e, dqs, dks, dvs, rope, wq, wk, T)
    g["e_q_norm_w"] = dnw[0:1, 0:HEAD_DIM]
    g["e_k_norm_w"] = dnw[1:2, 0:HEAD_DIM]
    pieces = [dqkv, drest]
    g["e_w_in"], g16["e_w_in"] = _mm_wgrad(ht_e, pieces, TILE_WG, "wgrad_e_in")
    half_e, half_e16 = _reduce_start(EVEN_MATS, g, g16, cidx)
    dx, g["e_norm_w"], blocks_e = _mm_nt_rms(pieces, wb["e_w_in"], x, p["e_norm_w"], dx1, TILE_SEQ, "dx_even",
                                             EVEN_MATS, half_e16)
    parts = {}
    for names, halves, blocks in ((ODD_MATS, half_o, blocks_o), (EVEN_MATS, half_e, blocks_e)):
        for n, h, r in zip(names, halves, blocks):
            parts[n] = _add_blocks(bidx, h, r, n)
    return lsum, dx, g, parts


BIG = ("e_w_in", "e_w_out", "o_w_in", "o_w_out")
SHARD_AXIS = {"e_w_in": 1, "e_w_out": 0, "o_w_in": 1, "o_w_out": 0, SMALL_PACK: 0}
N_CHIPS = 4


def _place():
    x, y, c = lax.axis_index("x"), lax.axis_index("y"), lax.axis_index("c")
    chips = [(1 - x, y), (x, 1 - y), (1 - x, 1 - y)]
    return x, y, c, chips


def _block_of(ref, name, block):
    rows, cols = ref.shape
    if SHARD_AXIS[name] == 1:
        cw = cols // N_CHIPS
        return ref.at[:, pl.ds(pl.multiple_of(block * cw, LANES), cw)]
    rw = rows // N_CHIPS
    return ref.at[pl.ds(pl.multiple_of(block * rw, rw), rw), :]


def _half_of(ref, name, half):
    rows, cols = ref.shape
    if SHARD_AXIS[name] == 1:
        return ref.at[pl.ds(pl.multiple_of(half * (rows // 2), rows // 2), rows // 2), :]
    return ref.at[:, pl.ds(pl.multiple_of(half * (cols // 2), LANES), cols // 2)]


def _sub(ref, name, block, half):
    rows, cols = ref.shape
    if SHARD_AXIS[name] == 1:
        cw, hr = cols // N_CHIPS, rows // 2
        return ref.at[pl.ds(pl.multiple_of(half * hr, hr), hr), pl.ds(pl.multiple_of(block * cw, LANES), cw)]
    rw, hc = rows // N_CHIPS, cols // 2
    return ref.at[pl.ds(pl.multiple_of(block * rw, rw), rw), pl.ds(pl.multiple_of(half * hc, LANES), hc)]


GATHER_COPIES = 7


class _Gather:
    def __init__(self, names, s_refs, f_refs, send, recv):
        self.names, self.s, self.f, self.send, self.recv = names, s_refs, f_refs, send, recv

    def _copy(self, k, src, dst, to):
        return pltpu.make_async_remote_copy(src_ref=src, dst_ref=dst, send_sem=self.send.at[k],
                                            recv_sem=self.recv.at[k], device_id=to, device_id_type=MESH)

    def _plan(self):
        x, y, c, chips = _place()
        me, sib = 2 * x + y, (x, y, 1 - c)
        first, relay_in, relay, last_in = [], [], [], []
        for wi, n in enumerate(self.names):
            k0 = wi * GATHER_COPIES
            s, f = self.s[wi], self.f[wi]
            own = _block_of(f, n, me)
            first.append(self._copy(k0 + 3, s, own, sib))
            last_in.append(self._copy(k0 + 3, s, own, sib))
            for j, (cx, cy) in enumerate(chips):
                first.append(self._copy(k0 + j, _half_of(s, n, c), _sub(f, n, me, c), (cx, cy, c)))
                mine = _sub(f, n, 2 * cx + cy, c)
                relay_in.append(self._copy(k0 + j, mine, mine, sib))
                relay.append(self._copy(k0 + 4 + j, mine, mine, sib))
                theirs = _sub(f, n, 2 * cx + cy, 1 - c)
                last_in.append(self._copy(k0 + 4 + j, theirs, theirs, sib))
        return first, relay_in, relay, last_in

    N_RELATIONS = 3

    def begin(self, relations=(0, 1, 2), sibling=True):
        first = self._plan()[0]
        for wi in range(len(self.names)):
            mine = first[wi * (1 + self.N_RELATIONS):(wi + 1) * (1 + self.N_RELATIONS)]
            if sibling:
                mine[0].start()
            for j in relations:
                mine[1 + j].start()

    def relay(self, relations=(0, 1, 2)):
        _, relay_in, relay, _ = self._plan()
        for wi in range(len(self.names)):
            for j in relations:
                relay_in[wi * self.N_RELATIONS + j].wait_recv()
                relay[wi * self.N_RELATIONS + j].start()

    def end(self):
        first, _, relay, last_in = self._plan()
        for cp in last_in:
            cp.wait_recv()
        for cp in first + relay:
            cp.wait_send()

    def wait_relayed(self, j):
        self._plan()[3][1 + j].wait_recv()

    def end_rest(self):
        first, _, relay, last_in = self._plan()
        last_in[0].wait_recv()
        for cp in first + relay:
            cp.wait_send()


def _full_shape(n, s):
    r, cdim = s.shape
    return jax.ShapeDtypeStruct((r, cdim * N_CHIPS) if SHARD_AXIS[n] == 1 else (r * N_CHIPS, cdim), s.dtype)


def _gather_sems(names):
    k = GATHER_COPIES * len(names)
    return [pltpu.SemaphoreType.DMA((k,)), pltpu.SemaphoreType.DMA((k,))]


def _scatter_copies(names, h_refs, r_refs, send, recv):
    _, _, c, chips = _place()
    cps = []
    for wi, n in enumerate(names):
        for j, (cx, cy) in enumerate(chips):
            cps.append(pltpu.make_async_remote_copy(
                src_ref=_block_of(h_refs[wi], n, 2 * cx + cy), dst_ref=r_refs[wi].at[j],
                send_sem=send.at[wi * 3 + j], recv_sem=recv.at[wi * 3 + j],
                device_id=(cx, cy, c), device_id_type=MESH))
    return cps


def _scatter_sems(names):
    return [pltpu.SemaphoreType.DMA((3 * len(names),)), pltpu.SemaphoreType.DMA((3 * len(names),))]


def _allreduce_small(part, name):
    R = part.shape[0]

    def body(p_ref, o_ref, sbuf, cbuf, send, recv):
        x, y, c, chips = _place()
        me = 2 * x + y
        sib = (x, y, 1 - c)
        sbuf[c] = p_ref[...]
        mine = sbuf.at[c]
        d2d = pltpu.make_async_remote_copy(src_ref=mine, dst_ref=mine, send_sem=send.at[0], recv_sem=recv.at[0],
                                           device_id=sib, device_id_type=MESH)
        d2d.start()
        theirs = sbuf.at[1 - c]
        pltpu.make_async_remote_copy(src_ref=theirs, dst_ref=theirs, send_sem=send.at[0], recv_sem=recv.at[0],
                                     device_id=sib, device_id_type=MESH).wait_recv()
        cbuf[me] = sbuf[0] + sbuf[1]
        blk = cbuf.at[me]
        sends = [d2d]
        for j, (cx, cy) in enumerate(chips):
            cp = pltpu.make_async_remote_copy(src_ref=blk, dst_ref=blk, send_sem=send.at[1 + j], recv_sem=recv.at[1 + j],
                                              device_id=(cx, cy, c), device_id_type=MESH)
            cp.start()
            sends.append(cp)
        for j, (cx, cy) in enumerate(chips):
            got = cbuf.at[2 * cx + cy]
            pltpu.make_async_remote_copy(src_ref=got, dst_ref=got, send_sem=send.at[1 + j], recv_sem=recv.at[1 + j],
                                         device_id=(cx, cy, c), device_id_type=MESH).wait_recv()
        o_ref[...] = (cbuf[0] + cbuf[1]) + (cbuf[2] + cbuf[3])
        for cp in sends:
            cp.wait_send()

    vm = pl.BlockSpec(memory_space=pltpu.VMEM)
    return pl.pallas_call(
        body, name=name, in_specs=[vm], out_specs=vm,
        out_shape=jax.ShapeDtypeStruct(part.shape, F32),
        scratch_shapes=[pltpu.VMEM((2, R, LANES), F32), pltpu.VMEM((N_CHIPS, R, LANES), F32),
                        pltpu.SemaphoreType.DMA((4,)), pltpu.SemaphoreType.DMA((4,))],
    )(part)


def _half_shape(shape, name):
    r, cdim = shape
    return (r // 2, cdim) if SHARD_AXIS[name] == 1 else (r, cdim // 2)


def _shard_shape(shape, name):
    r, cdim = shape
    return (r, cdim // N_CHIPS) if SHARD_AXIS[name] == 1 else (r // N_CHIPS, cdim)


def _swap_to_sibling(names, srcs, name, pick_half):
    nw = len(names)

    def body(*refs):
        g_refs, r_refs = refs[:nw], refs[nw:2 * nw]
        send, recv = refs[2 * nw:]
        x, y, c, _ = _place()
        sib = (x, y, 1 - c)
        cps = []
        for wi, n in enumerate(names):
            src = _half_of(g_refs[wi], n, 1 - c) if pick_half else g_refs[wi]
            cp = pltpu.make_async_remote_copy(src_ref=src, dst_ref=r_refs[wi], send_sem=send.at[wi],
                                              recv_sem=recv.at[wi], device_id=sib, device_id_type=MESH)
            cp.start()
            cps.append(cp)
        for cp in cps:
            cp.wait()

    outs = [jax.ShapeDtypeStruct(_half_shape(g.shape, n) if pick_half else g.shape, g.dtype)
            for n, g in zip(names, srcs)]
    return pl.pallas_call(
        body, name=name, in_specs=[ANY] * nw, out_specs=[ANY] * nw, out_shape=outs,
        scratch_shapes=[pltpu.SemaphoreType.DMA((nw,)), pltpu.SemaphoreType.DMA((nw,))],
    )(*srcs)


def _add_half(cidx, g, r, name):
    rows, cols = r.shape
    tr = 256
    tc = cols if cols <= 1792 else (1792 if cols % 1792 == 0 else 1280)
    nr, nc = rows // tr, cols // tc

    def body(c_ref, g_ref, r_ref, o_ref, ob_ref):
        s = g_ref[...] + r_ref[...].astype(F32)
        o_ref[...] = s
        ob_ref[...] = s.astype(BF16)

    if SHARD_AXIS[name] == 1:
        gmap = lambda i, j, c_ref: (c_ref[0] * nr + i, j)
    else:
        gmap = lambda i, j, c_ref: (i, c_ref[0] * nc + j)
    same = lambda i, j, c_ref: (i, j)
    return pl.pallas_call(
        body, name="add_half_" + name,
        grid_spec=pltpu.PrefetchScalarGridSpec(
            num_scalar_prefetch=1, grid=(nr, nc),
            in_specs=[pl.BlockSpec((tr, tc), gmap), pl.BlockSpec((tr, tc), same)],
            out_specs=[pl.BlockSpec((tr, tc), same), pl.BlockSpec((tr, tc), same)]),
        out_shape=[jax.ShapeDtypeStruct(r.shape, F32), jax.ShapeDtypeStruct(r.shape, BF16)],
        compiler_params=pltpu.CompilerParams(dimension_semantics=("parallel", "parallel"), vmem_limit_bytes=VMEM_LIMIT),
    )(cidx, g, r)


def _add_blocks(bidx, h, r, name):
    _, rows, cols = r.shape
    tr = min(rows, 256)
    nr = rows // tr

    def body(b_ref, h_ref, r0, r1, r2, o_ref):
        o_ref[...] = ((h_ref[...] + r0[0].astype(F32)) + r1[0].astype(F32)) + r2[0].astype(F32)

    if SHARD_AXIS[name] == 1:
        hmap = lambda i, b_ref: (i, b_ref[0])
    else:
        hmap = lambda i, b_ref: (b_ref[0] * nr + i, 0)
    rspec = lambda j: pl.BlockSpec((1, tr, cols), lambda i, b_ref, j=j: (j, i, 0))
    return pl.pallas_call(
        body, name="add_blocks_" + name,
        grid_spec=pltpu.PrefetchScalarGridSpec(
            num_scalar_prefetch=1, grid=(nr,),
            in_specs=[pl.BlockSpec((tr, cols), hmap), rspec(0), rspec(1), rspec(2)],
            out_specs=pl.BlockSpec((tr, cols), lambda i, b_ref: (i, 0))),
        out_shape=jax.ShapeDtypeStruct((rows, cols), F32),
        compiler_params=pltpu.CompilerParams(dimension_semantics=("parallel",), vmem_limit_bytes=VMEM_LIMIT),
    )(bidx, h, r, r, r)


def _adam_math(w, g, m, v):
    c1 = 1.0 - ADAM_B1 ** ADAM_STEP
    c2 = 1.0 - ADAM_B2 ** ADAM_STEP
    nm = ADAM_B1 * m + (1.0 - ADAM_B1) * g
    nv = ADAM_B2 * v + (1.0 - ADAM_B2) * (g * g)
    delta = -ADAM_LR * ((nm / c1) / (jnp.sqrt(nv / c2) + ADAM_EPS) + ADAM_WD * w)
    return delta, nm, nv


def _adamw(w, g, m, v, name):
    def body(w_ref, g_ref, m_ref, v_ref, d_ref, nm_ref, nv_ref):
        d_ref[...], nm_ref[...], nv_ref[...] = _adam_math(w_ref[...], g_ref[...], m_ref[...], v_ref[...])

    spec = pl.BlockSpec(w.shape, lambda i: (0, 0))
    return _call(body, "adamw_" + name, (1,), [spec] * 4, [spec] * 3,
                 [jax.ShapeDtypeStruct(w.shape, F32)] * 3, sem=("arbitrary",))(w, g, m, v)


def _adamw_halves(cidx, w, mine, theirs, m, v, name):
    rows, cols = w.shape
    hr, hc = mine.shape
    tr = 128
    if SHARD_AXIS[name] == 1:
        ni = hr // tr
        wmap = lambda hh, i, c_ref: (hh * ni + i, 0)
    else:
        ni = hr // tr
        wmap = lambda hh, i, c_ref: (i, hh)
    hmap = lambda hh, i, c_ref: (i, 0)

    def body(c_ref, w_ref, a_ref, b_ref, m_ref, v_ref, g_ref, d_ref, nm_ref, nv_ref):
        g = jnp.where(pl.program_id(0) == c_ref[0], a_ref[...], b_ref[...])
        g_ref[...] = g
        d_ref[...], nm_ref[...], nv_ref[...] = _adam_math(w_ref[...], g, m_ref[...], v_ref[...])

    wspec = pl.BlockSpec((tr, hc), wmap)
    hspec = pl.BlockSpec((tr, hc), hmap)
    return pl.pallas_call(
        body, name="adamw_" + name,
        grid_spec=pltpu.PrefetchScalarGridSpec(
            num_scalar_prefetch=1, grid=(2, ni),
            in_specs=[wspec, hspec, hspec, wspec, wspec], out_specs=[wspec] * 4),
        out_shape=[jax.ShapeDtypeStruct(w.shape, F32)] * 4,
        compiler_params=pltpu.CompilerParams(dimension_semantics=("parallel", "parallel"), vmem_limit_bytes=VMEM_LIMIT),
    )(cidx, w, mine, theirs, m, v)


SMALL = ("e_norm_w", "e_q_norm_w", "e_k_norm_w", "e_conv_w", "o_norm_w", "o_pool_w", "o_pool_scale",
         "o_dconv_w", "o_dconv_b", "o_ln_w", "o_ln_b")
SMALL_SHARDED = ("e_conv_w", "o_norm_w", "o_pool_scale", "o_dconv_w", "o_dconv_b", "o_ln_w", "o_ln_b")
WEIGHTS = ("e_norm_w", "e_w_in", "e_q_norm_w", "e_k_norm_w", "e_conv_w", "e_w_out", "o_norm_w", "o_w_in",
           "o_pool_w", "o_pool_scale", "o_dconv_w", "o_dconv_b", "o_ln_w", "o_ln_b", "o_w_out")


def _pack(arrs):
    flat = jnp.concatenate([a.reshape(-1) for a in arrs])
    rows = -(-flat.shape[0] // (LANES * SUBLANES)) * SUBLANES
    flat = jnp.pad(flat, (0, rows * LANES - flat.shape[0]))
    return flat.reshape(rows, LANES)


def _unpack(packed, shapes):
    flat = packed.reshape(-1)
    out, off = [], 0
    for s in shapes:
        n = int(np.prod(s))
        out.append(flat[off:off + n].reshape(s))
        off += n
    return out


def _gather_last(a, block, width):
    return lax.dynamic_slice_in_dim(a, block * width, width, axis=a.ndim - 1)


def kernel(x, positions, e_norm_w, e_w_in, e_q_norm_w, e_k_norm_w, e_conv_w, e_w_out, o_norm_w, o_w_in, o_pool_w, o_pool_scale, o_dconv_w, o_dconv_b, o_ln_w, o_ln_b, o_w_out, loss_target, m_e_norm_w, m_e_w_in, m_e_q_norm_w, m_e_k_norm_w, m_e_conv_w, m_e_w_out, m_o_norm_w, m_o_w_in, m_o_pool_w, m_o_pool_scale, m_o_dconv_w, m_o_dconv_b, m_o_ln_w, m_o_ln_b, m_o_w_out, v_e_norm_w, v_e_w_in, v_e_q_norm_w, v_e_k_norm_w, v_e_conv_w, v_e_w_out, v_o_norm_w, v_o_w_in, v_o_pool_w, v_o_pool_scale, v_o_dconv_w, v_o_dconv_b, v_o_ln_w, v_o_ln_b, v_o_w_out):
    given = dict(e_norm_w=e_norm_w, e_w_in=e_w_in, e_q_norm_w=e_q_norm_w, e_k_norm_w=e_k_norm_w, e_conv_w=e_conv_w,
                 e_w_out=e_w_out, o_norm_w=o_norm_w, o_w_in=o_w_in, o_pool_w=o_pool_w, o_pool_scale=o_pool_scale,
                 o_dconv_w=o_dconv_w, o_dconv_b=o_dconv_b, o_ln_w=o_ln_w, o_ln_b=o_ln_b, o_w_out=o_w_out)
    mom = dict(e_norm_w=m_e_norm_w, e_w_in=m_e_w_in, e_q_norm_w=m_e_q_norm_w, e_k_norm_w=m_e_k_norm_w,
               e_conv_w=m_e_conv_w, e_w_out=m_e_w_out, o_norm_w=m_o_norm_w, o_w_in=m_o_w_in, o_pool_w=m_o_pool_w,
               o_pool_scale=m_o_pool_scale, o_dconv_w=m_o_dconv_w, o_dconv_b=m_o_dconv_b, o_ln_w=m_o_ln_w,
               o_ln_b=m_o_ln_b, o_w_out=m_o_w_out)
    var = dict(e_norm_w=v_e_norm_w, e_w_in=v_e_w_in, e_q_norm_w=v_e_q_norm_w, e_k_norm_w=v_e_k_norm_w,
               e_conv_w=v_e_conv_w, e_w_out=v_e_w_out, o_norm_w=v_o_norm_w, o_w_in=v_o_w_in, o_pool_w=v_o_pool_w,
               o_pool_scale=v_o_pool_scale, o_dconv_w=v_o_dconv_w, o_dconv_b=v_o_dconv_b, o_ln_w=v_o_ln_w,
               o_ln_b=v_o_ln_b, o_w_out=v_o_w_out)
    S = x.shape[1]
    mx, my, mc = lax.axis_index("x"), lax.axis_index("y"), lax.axis_index("c")
    chip = 2 * mx + my
    cidx = jnp.reshape(mc, (1,)).astype(jnp.int32)
    bidx = jnp.reshape(chip, (1,)).astype(jnp.int32)

    shards = {n: given[n][0].astype(BF16) for n in BIG}
    shard_sizes = [int(np.prod(given[n].shape)) for n in SMALL_SHARDED]
    flat = jnp.concatenate([given[n].reshape(-1) for n in SMALL_SHARDED])
    rows = -(-flat.shape[0] // (SMALL_PACK_W * SUBLANES)) * SUBLANES
    shards[SMALL_PACK] = jnp.pad(flat, (0, rows * SMALL_PACK_W - flat.shape[0])).reshape(rows, SMALL_PACK_W)

    def unpack_small(full):
        gathered = full.reshape(N_CHIPS, rows * SMALL_PACK_W)
        out, off = {}, 0
        for n, size in zip(SMALL_SHARDED, shard_sizes):
            sh = given[n].shape[1:]
            parts = gathered[:, off:off + size].reshape((N_CHIPS,) + sh)
            fullp = jnp.moveaxis(parts, 0, -2).reshape(sh[:-1] + (N_CHIPS * sh[-1],))
            out[n] = fullp.reshape(-1, fullp.shape[-1])
            off += size
        return out

    p = dict(e_norm_w=e_norm_w, e_q_norm_w=e_q_norm_w, e_k_norm_w=e_k_norm_w, o_pool_w=o_pool_w[0])
    lsum, dx, g, parts = _local_step(x[0], positions.reshape(S, 1), loss_target[0], shards, p, unpack_small,
                                     cidx, bidx)
    mine = [g[n] for n in SMALL] + [(0.5 / float(D_MODEL)) * jnp.sum(lsum, keepdims=True)]
    tot = _unpack(_allreduce_small(_pack(mine), "allreduce_small"), [a.shape for a in mine])
    loss = tot[-1].reshape(())
    gsmall = dict(zip(SMALL, tot))
    grads = {}
    for n in SMALL:
        gv = gsmall[n]
        if n in SMALL_SHARDED:
            gv = _gather_last(gv, chip, gv.shape[-1] // N_CHIPS)
        grads[n] = gv.reshape(given[n].shape)

    theirs = _swap_to_sibling(BIG, [parts[n] for n in BIG], "swap_reduced", False)

    delta, new_m, new_v = {}, {}, {}
    for n, other in zip(BIG, theirs):
        sh = given[n].shape
        gs, d, nm, nv = _adamw_halves(cidx, given[n][0], parts[n], other, mom[n][0], var[n][0], n)
        grads[n], delta[n], new_m[n], new_v[n] = gs.reshape(sh), d.reshape(sh), nm.reshape(sh), nv.reshape(sh)
    big_small = "o_pool_w"
    pw = [src[big_small].reshape(-1, LANES) for src in (given, grads, mom, var)]
    for dst, a in zip((delta, new_m, new_v), _adamw(*pw, "pool_w")):
        dst[big_small] = a.reshape(given[big_small].shape)
    tiny = tuple(n for n in SMALL if n != big_small)
    shapes = [given[n].shape for n in tiny]
    packed = [_pack([src[n] for n in tiny]) for src in (given, grads, mom, var)]
    for dst, pk in zip((delta, new_m, new_v), _adamw(*packed, "small")):
        for n, a in zip(tiny, _unpack(pk, shapes)):
            dst[n] = a
    return (loss, dx[None], *[grads[n] for n in WEIGHTS], *[delta[n] for n in WEIGHTS],
            *[new_m[n] for n in WEIGHTS], *[new_v[n] for n in WEIGHTS])
```

```python
import numpy as np
import jax
import jax.numpy as jnp
from jax import lax
from jax.experimental import pallas as pl
from jax.experimental.pallas import tpu as pltpu

F32 = jnp.float32
BF16 = jnp.bfloat16

D_MODEL = 1024
HEAD_DIM = 64
A_WIDTH = 512
A_HEADS = 8
A_GROUPS = ((128, 1), (512, 4), (2048, 16))
QBLK = 128
ROT_DIM = 16
ROPE_THETA = 500000.0
POOL_SIZES = (2, 4, 8, 16)
D_CONV = 31
SC_WIDTH = 3
EVEN_IN = 7168
ODD_IN = 2560
EPS = 1e-6
NEG = -1e30
ADAM_LR, ADAM_B1, ADAM_B2, ADAM_EPS, ADAM_WD, ADAM_STEP = 0.001, 0.9, 0.999, 1e-08, 0.01, 10

LANES = 128
SUBLANES = 8
HALO = 32
VMEM_LIMIT = 52 * 1024 * 1024
MESH = pl.DeviceIdType.MESH
ANY = pl.BlockSpec(memory_space=pl.ANY)

NT_DIMS = (((1,), (1,)), ((), ()))
TN_DIMS = (((0,), (0,)), ((), ()))


def _call(body, name, grid, in_specs, out_specs, out_shape, scratch=(), sem=None, aliases=None):
    return pl.pallas_call(
        body, name=name, grid=grid, in_specs=in_specs, out_specs=out_specs, out_shape=out_shape,
        scratch_shapes=list(scratch), input_output_aliases=aliases or {},
        compiler_params=pltpu.CompilerParams(dimension_semantics=sem, vmem_limit_bytes=VMEM_LIMIT))


def _sig(v):
    return jax.nn.sigmoid(v)


def _dsilu(v, s):
    return s * (1.0 + v * (1.0 - s))


def _out_projection(ut_ref, w_ref, lo, hi):
    acc = None
    for k, v in enumerate((lo, hi)):
        ut_ref[k * A_WIDTH:(k + 1) * A_WIDTH, :] = v.T.astype(BF16)
        part = jnp.dot(v.astype(BF16), w_ref[k * A_WIDTH:(k + 1) * A_WIDTH, :], preferred_element_type=F32)
        acc = part if acc is None else acc + part
    return acc


def _cs8(v):
    return v.reshape(v.shape[0] // SUBLANES, SUBLANES, v.shape[1]).sum(axis=0)


def _seg_mean():
    r = lax.broadcasted_iota(jnp.int32, (LANES, LANES), 0) // HEAD_DIM
    c = lax.broadcasted_iota(jnp.int32, (LANES, LANES), 1) // HEAD_DIM
    return jnp.where(r == c, 1.0 / HEAD_DIM, 0.0).astype(BF16)


def _segsum(v, ones):
    hi = v.astype(BF16)
    lo = (v - hi.astype(F32)).astype(BF16)
    return (jnp.dot(hi, ones, preferred_element_type=F32) + jnp.dot(lo, ones, preferred_element_type=F32))


def _head_rstd(v, seg_mean):
    return lax.rsqrt(jnp.dot((v * v).astype(BF16), seg_mean, preferred_element_type=F32) + EPS)


def _rope_tables(pos_ref, freq_ref):
    ang = pos_ref[...].astype(F32) * freq_ref[...]
    cosv, sinv = jnp.cos(ang), jnp.sin(ang)
    lm = lax.broadcasted_iota(jnp.int32, ang.shape, 1) % HEAD_DIM
    half = ROT_DIM // 2
    c = jnp.where(lm < ROT_DIM, cosv, 1.0)
    s1 = jnp.where((lm >= half) & (lm < ROT_DIM), sinv, 0.0)
    s2 = jnp.where(lm < half, -sinv, 0.0)
    return c, s1, s2


def _freq_table():
    half = ROT_DIM // 2
    inv = ROPE_THETA ** (-np.arange(half, dtype=np.float64) / half)
    lane = np.arange(LANES) % HEAD_DIM
    f = np.where(lane < ROT_DIM, inv[lane % half], 0.0)
    return jnp.asarray(f.reshape(1, LANES), F32)


def _load_once(hbm_ref, vmem_ref, sem):
    @pl.when(pl.program_id(0) == 0)
    def _():
        cp = pltpu.make_async_copy(hbm_ref, vmem_ref, sem)
        cp.start()
        cp.wait()


def _rms_rows(x_ref, nw_ref):
    xv = x_ref[...]
    ms = jnp.mean(xv * xv, axis=-1, keepdims=True)
    return xv * lax.rsqrt(ms + EPS) * nw_ref[...]


def _inproj(x, nw, w, tm, tn, name):
    S, N = x.shape[0], w.shape[1]

    def body(x_ref, nw_ref, w_hbm, o_ref, ht_ref, w_v, sem):
        _load_once(w_hbm, w_v, sem)
        h = _rms_rows(x_ref, nw_ref)
        ht_ref[...] = h.T.astype(BF16)
        hb = h.astype(BF16)
        for j in range(N // tn):
            o_ref[:, j * tn:(j + 1) * tn] = jnp.dot(hb, w_v[:, j * tn:(j + 1) * tn], preferred_element_type=F32)

    return _call(
        body, name, (S // tm,),
        [pl.BlockSpec((tm, D_MODEL), lambda i: (i, 0)),
         pl.BlockSpec((1, D_MODEL), lambda i: (0, 0)), ANY],
        [pl.BlockSpec((tm, N), lambda i: (i, 0)),
         pl.BlockSpec((D_MODEL, tm), lambda i: (0, i))],
        [jax.ShapeDtypeStruct((S, N), F32), jax.ShapeDtypeStruct((D_MODEL, S), BF16)],
        scratch=[pltpu.VMEM(w.shape, BF16), pltpu.SemaphoreType.DMA(())], sem=("arbitrary",))(x, nw, w)


def _inproj_gathering(x, nw, shard, bidx, first, late_names, late_shards, tm, name):
    S = x.shape[0]
    ni = S // tm
    K, cw = shard.shape
    nl = len(late_names)
    last = N_CHIPS - 1

    def body(b_ref, x_ref, nw_ref, s_hbm, *rest):
        ls_refs = rest[:nl]
        o_ref, ht_ref, f_hbm = rest[nl:nl + 3]
        lf_refs = rest[nl + 3:2 * nl + 3]
        hs, w_blk, lsem, send1, recv1, send2, recv2 = rest[2 * nl + 3:]
        j, i = pl.program_id(0), pl.program_id(1)
        g1 = _Gather((first,), (s_hbm,), (f_hbm,), send1, recv1)
        g2 = _Gather(late_names, ls_refs, lf_refs, send2, recv2)
        _, _, _, chips = _place()

        def load_block(src):
            cp = pltpu.make_async_copy(src, w_blk, lsem)
            cp.start()
            cp.wait()

        @pl.when((j == 0) & (i == 0))
        def _():
            g1.begin(relations=(0, 1))
            load_block(s_hbm)

        for r, (cx, cy) in enumerate(chips):
            @pl.when((j == r + 1) & (i == 0))
            def _(r=r, cx=cx, cy=cy):
                g1.wait_relayed(r)
                load_block(_block_of(f_hbm, first, 2 * cx + cy))

        @pl.when((j == 2) & (i == 0))
        def _():
            g1.relay(relations=(2,))
            g2.begin()

        pl.when((j == last) & (i == ni // 2))(g2.relay)

        rows = pl.ds(pl.multiple_of(i * tm, tm), tm)

        @pl.when(j == 0)
        def _():
            h = _rms_rows(x_ref, nw_ref)
            hs[rows, :] = h.astype(BF16)
            ht_ref[...] = h.T.astype(BF16)

        o_ref[...] = jnp.dot(hs[rows, :], w_blk[...], preferred_element_type=F32)

        @pl.when((j == 0) & (i == ni - 1))
        def _():
            g1.relay(relations=(0, 1))
            g1.begin(relations=(2,), sibling=False)

        @pl.when((j == last) & (i == ni - 1))
        def _():
            g1.end_rest()
            g2.end()

    def block_of_step(j, b_ref):
        return jnp.bitwise_xor(b_ref[0], jnp.bitwise_or(jnp.left_shift(jnp.bitwise_and(j, 1), 1), jnp.right_shift(j, 1)))

    outs = pl.pallas_call(
        body, name=name,
        grid_spec=pltpu.PrefetchScalarGridSpec(
            num_scalar_prefetch=1, grid=(N_CHIPS, ni),
            in_specs=[pl.BlockSpec((tm, D_MODEL), lambda j, i, b: (jnp.where(j == 0, i, 0), 0)),
                      pl.BlockSpec((1, D_MODEL), lambda j, i, b: (0, 0)), ANY] + [ANY] * nl,
            out_specs=[pl.BlockSpec((tm, cw), lambda j, i, b: (i, block_of_step(j, b))),
                       pl.BlockSpec((D_MODEL, tm), lambda j, i, b: (0, jnp.where(j == 0, i, ni - 1))),
                       ANY] + [ANY] * nl,
            scratch_shapes=[pltpu.VMEM((S, D_MODEL), BF16), pltpu.VMEM((K, cw), BF16), pltpu.SemaphoreType.DMA(())] +
            _gather_sems((first,)) + _gather_sems(late_names)),
        out_shape=[jax.ShapeDtypeStruct((S, cw * N_CHIPS), F32), jax.ShapeDtypeStruct((D_MODEL, S), BF16),
                   _full_shape(first, shard)] + [_full_shape(n, s) for n, s in zip(late_names, late_shards)],
        compiler_params=pltpu.CompilerParams(dimension_semantics=("arbitrary", "arbitrary"),
                                             vmem_limit_bytes=VMEM_LIMIT),
    )(bidx, x, nw, shard, *late_shards)
    return outs[0], outs[1], outs[2], list(outs[3:])


def _piece_blocks(pieces, tk, axis):
    starts, counts, s = [], [], 0
    for p in pieces:
        n = p.shape[axis] // tk
        starts.append(s)
        counts.append(n)
        s += n
    return starts, counts, s


def _mm_nt_rms(pieces, w, x, nw, dres, tm, name, scatter_names=(), scatter_halves=()):
    S = x.shape[0]
    npc = len(pieces)
    ni = S // tm
    ns = len(scatter_names)
    offs = np.cumsum([0] + [p.shape[1] for p in pieces]).tolist()

    def body(*refs):
        p_refs = refs[:npc]
        w_hbm, x_ref, nw_ref, dr_ref = refs[npc:npc + 4]
        h_refs = refs[npc + 4:npc + 4 + ns]
        dx_ref, dnw_ref = refs[npc + 4 + ns:npc + 6 + ns]
        r_refs = refs[npc + 6 + ns:npc + 6 + 2 * ns]
        w_v, sem, nacc = refs[npc + 6 + 2 * ns:npc + 9 + 2 * ns]
        i = pl.program_id(0)
        if ns:
            send, recv = refs[npc + 9 + 2 * ns:]

            @pl.when(i == 0)
            def _():
                for cp in _scatter_copies(scatter_names, h_refs, r_refs, send, recv):
                    cp.start()
        _load_once(w_hbm, w_v, sem)

        @pl.when(i == 0)
        def _():
            nacc[...] = jnp.zeros_like(nacc)

        dh = None
        for p in range(npc):
            part = lax.dot_general(p_refs[p][...].astype(BF16), w_v[:, offs[p]:offs[p + 1]], NT_DIMS,
                                   preferred_element_type=F32)
            dh = part if dh is None else dh + part
        xv = x_ref[...]
        rs = lax.rsqrt(jnp.mean(xv * xv, axis=-1, keepdims=True) + EPS)
        xh = xv * rs
        nacc[...] += _cs8(dh * xh)
        dxh = dh * nw_ref[...]
        dx_ref[...] = dr_ref[...] + rs * (dxh - xh * jnp.mean(dxh * xh, axis=-1, keepdims=True))

        @pl.when(i == ni - 1)
        def _():
            dnw_ref[...] = jnp.sum(nacc[...], axis=0, keepdims=True)
            if ns:
                for cp in _scatter_copies(scatter_names, h_refs, r_refs, send, recv):
                    cp.wait()

    row = pl.BlockSpec((tm, D_MODEL), lambda i: (i, 0))
    outs = _call(
        body, name, (ni,),
        [pl.BlockSpec((tm, p.shape[1]), lambda i: (i, 0)) for p in pieces] +
        [ANY, row, pl.BlockSpec((1, D_MODEL), lambda i: (0, 0)), row] + [ANY] * ns,
        [row, pl.BlockSpec((1, D_MODEL), lambda i: (0, 0))] + [ANY] * ns,
        [jax.ShapeDtypeStruct((S, D_MODEL), F32), jax.ShapeDtypeStruct((1, D_MODEL), F32)] +
        [jax.ShapeDtypeStruct((3,) + _shard_shape(h.shape, n), h.dtype) for n, h in zip(scatter_names, scatter_halves)],
        scratch=[pltpu.VMEM(w.shape, BF16), pltpu.SemaphoreType.DMA(()), pltpu.VMEM((SUBLANES, D_MODEL), F32)] +
        (_scatter_sems(scatter_names) if ns else []),
        sem=("arbitrary",))(*pieces, w, x, nw, dres, *scatter_halves)
    return outs[0], outs[1], list(outs[2:])


def _mm_wgrad(at, pieces, tn, name):
    M, S = at.shape
    starts, counts, nj = _piece_blocks(pieces, tn, 1)
    npc = len(pieces)

    def body(*refs):
        a_hbm = refs[0]
        p_refs = refs[1:1 + npc]
        o_ref, o16_ref, a_v, sem = refs[1 + npc:]
        j = pl.program_id(0)
        _load_once(a_hbm, a_v, sem)
        for p in range(npc):
            @pl.when((j >= starts[p]) & (j < starts[p] + counts[p]))
            def _(p=p):
                acc = jnp.dot(a_v[...], p_refs[p][...].astype(BF16), preferred_element_type=F32)
                o_ref[...] = acc
                o16_ref[...] = acc.astype(BF16)

    def pspec(p):
        return pl.BlockSpec((S, tn), lambda j: (0, jnp.clip(j - starts[p], 0, counts[p] - 1)))

    col = pl.BlockSpec((M, tn), lambda j: (0, j))
    return _call(
        body, name, (nj,),
        [ANY] + [pspec(p) for p in range(npc)], [col, col],
        [jax.ShapeDtypeStruct((M, nj * tn), F32), jax.ShapeDtypeStruct((M, nj * tn), BF16)],
        scratch=[pltpu.VMEM(at.shape, BF16), pltpu.SemaphoreType.DMA(())], sem=("arbitrary",))(at, *pieces)


def _stream_spec(d, T):
    return pl.BlockSpec((d, T // d, A_WIDTH), lambda i: (0, i, 0))


def _stream_shape(d, S, dtype):
    return jax.ShapeDtypeStruct((d, S // d, A_WIDTH), dtype)


N_CHUNK = A_WIDTH // LANES


def _to_tokens(ref, scr, d, T):
    if d == 1:
        return ref[0].astype(F32)
    for r in range(d):
        for ch in range(N_CHUNK):
            scr.at[ch][pl.ds(r, T // d, stride=d), :] = ref[r, :, ch * LANES:(ch + 1) * LANES].astype(F32)
    return _get(scr)


def _from_tokens(out_ref, scr, d, T):
    for r in range(d):
        for ch in range(N_CHUNK):
            out_ref[r, :, ch * LANES:(ch + 1) * LANES] = scr.at[ch][pl.ds(r, T // d, stride=d), :].astype(out_ref.dtype)


def _put(scr, val):
    for ch in range(N_CHUNK):
        scr[ch] = val[:, ch * LANES:(ch + 1) * LANES]


def _get(scr):
    return jnp.concatenate([scr[ch] for ch in range(N_CHUNK)], axis=1)


def _chunked(T):
    return pltpu.VMEM((N_CHUNK, T, LANES), F32)


def _compact_spec(d, T):
    return pl.BlockSpec((d, T // d, LANES), lambda i: (0, i, 0))


def _compact_shape(d, S):
    return jax.ShapeDtypeStruct((d, S // d, LANES), F32)


def _compact_to_tokens(ref, scr, d, T):
    if d == 1:
        return ref[0]
    for r in range(d):
        scr[pl.ds(r, T // d, stride=d), :] = ref[r]
    return scr[...]


def _compact_from_tokens(out_ref, scr, val, d, T):
    if d == 1:
        out_ref[0] = val
        return
    scr[...] = val
    for r in range(d):
        out_ref[r] = scr[pl.ds(r, T // d, stride=d), :]


def _head_expander():
    r = lax.broadcasted_iota(jnp.int32, (LANES, A_WIDTH), 0)
    c = lax.broadcasted_iota(jnp.int32, (LANES, A_WIDTH), 1) // HEAD_DIM
    return (r == c).astype(BF16)


def _head_reducer():
    r = lax.broadcasted_iota(jnp.int32, (A_WIDTH, LANES), 0) // HEAD_DIM
    c = lax.broadcasted_iota(jnp.int32, (A_WIDTH, LANES), 1)
    return (r == c).astype(BF16)


def _qkv_prep(proj, pos, freq, wq, wk, T):
    S = proj.shape[0]
    qk_w = 3 * A_WIDTH

    def body(q_ref, k_ref, v_ref, pos_ref, f_ref, wq_ref, wk_ref, *rest):
        outs, tabs, scr = rest[:9], rest[9:12], rest[12]
        seg_mean = _seg_mean()
        c, s1, s2 = _rope_tables(pos_ref, f_ref)
        for tab, val in zip(tabs, (c, s1, s2)):
            tab[...] = val
        for t, (src, w_ref) in enumerate(((q_ref, wq_ref), (k_ref, wk_ref), (v_ref, None))):
            for g in range(3):
                d = A_GROUPS[g][1]
                out = outs[3 * t + g]
                for ch in range(A_WIDTH // LANES):
                    cs = slice(ch * LANES, (ch + 1) * LANES)
                    v = src[:, g * A_WIDTH + ch * LANES: g * A_WIDTH + (ch + 1) * LANES]
                    if w_ref is not None:
                        y = v * _head_rstd(v, seg_mean) * w_ref[...]
                        v = y * c + pltpu.roll(y, 8, 1) * s1 + pltpu.roll(y, LANES - 8, 1) * s2
                    if d == 1:
                        out[0, :, cs] = v.astype(BF16)
                    else:
                        scr[ch] = v
                if d > 1:
                    _from_tokens(out, scr, d, T)

    ds_ = [A_GROUPS[g][1] for g in range(3)] * 3
    return _call(
        body, "qkv_prep", (S // T,),
        [pl.BlockSpec((T, qk_w), lambda i: (i, 0)), pl.BlockSpec((T, qk_w), lambda i: (i, 1)),
         pl.BlockSpec((T, qk_w), lambda i: (i, 2)),
         pl.BlockSpec((T, 1), lambda i: (i, 0)), pl.BlockSpec((1, LANES), lambda i: (0, 0)),
         pl.BlockSpec((1, LANES), lambda i: (0, 0)), pl.BlockSpec((1, LANES), lambda i: (0, 0))],
        [_stream_spec(d, T) for d in ds_] + [pl.BlockSpec((T, LANES), lambda i: (i, 0))] * 3,
        [_stream_shape(d, S, BF16) for d in ds_] + [jax.ShapeDtypeStruct((S, LANES), F32)] * 3,
        scratch=[_chunked(T)], sem=("parallel",))(proj, proj, proj, pos, freq, wq, wk)


def _attn_mask(i):
    qi = lax.broadcasted_iota(jnp.int32, (QBLK, 2 * QBLK), 0) + QBLK
    kj = lax.broadcasted_iota(jnp.int32, (QBLK, 2 * QBLK), 1)
    dist = qi - kj
    return (dist >= 0) & (dist <= QBLK) & ((i > 0) | (kj >= QBLK))


ATT_BLK = (None, QBLK, A_WIDTH)
ATT_CBLK = (None, QBLK, LANES)


def _first_head_lanes():
    return lax.broadcasted_iota(jnp.int32, (1, LANES), 1) < HEAD_DIM


def _split_heads(v, first):
    zero = jnp.zeros_like(v)
    return jnp.where(first, v, zero), jnp.where(first, zero, v)


def _attn_fwd(q, k, v, g):
    d, n, _ = q.shape
    nb = n // QBLK

    def body(q_ref, kp_ref, kc_ref, vp_ref, vc_ref, o_ref, l_ref, s_scr, p_scr):
        i = pl.program_id(1)
        mask = _attn_mask(i)
        first = _first_head_lanes()
        for pr in range(A_HEADS // 2):
            ps = slice(pr * LANES, (pr + 1) * LANES)
            kc = jnp.concatenate([kp_ref[:, ps], kc_ref[:, ps]], axis=0)
            for e, qh in enumerate(_split_heads(q_ref[:, ps], first)):
                s_scr[2 * pr + e] = lax.dot_general(qh, kc, NT_DIMS, preferred_element_type=F32)
        lane = lax.broadcasted_iota(jnp.int32, (1, LANES), 1)
        lrow = jnp.zeros((QBLK, LANES), F32)
        for h in range(A_HEADS):
            s = jnp.where(mask, s_scr[h] * (HEAD_DIM ** -0.5), NEG)
            m = jnp.max(s, axis=-1, keepdims=True)
            p = jnp.exp(s - m)
            den = jnp.sum(p, axis=-1, keepdims=True)
            p_scr[h] = (p / den).astype(BF16)
            lrow = jnp.where(lane == h, m + jnp.log(den), lrow)
        l_ref[...] = lrow
        for pr in range(A_HEADS // 2):
            ps = slice(pr * LANES, (pr + 1) * LANES)
            va, vb = _split_heads(jnp.concatenate([vp_ref[:, ps], vc_ref[:, ps]], axis=0), first)
            o_ref[:, ps] = (jnp.dot(p_scr[2 * pr], va, preferred_element_type=F32) +
                            jnp.dot(p_scr[2 * pr + 1], vb, preferred_element_type=F32)).astype(BF16)

    prev = lambda r, i: (r, jnp.maximum(i - 1, 0), 0)
    cur = lambda r, i: (r, i, 0)
    return _call(
        body, "attn_fwd_g%d" % g, (d, nb),
        [pl.BlockSpec(ATT_BLK, cur), pl.BlockSpec(ATT_BLK, prev), pl.BlockSpec(ATT_BLK, cur),
         pl.BlockSpec(ATT_BLK, prev), pl.BlockSpec(ATT_BLK, cur)],
        [pl.BlockSpec(ATT_BLK, cur), pl.BlockSpec(ATT_CBLK, cur)],
        [jax.ShapeDtypeStruct((d, n, A_WIDTH), BF16), jax.ShapeDtypeStruct((d, n, LANES), F32)],
        scratch=[pltpu.VMEM((A_HEADS, QBLK, 2 * QBLK), F32), pltpu.VMEM((A_HEADS, QBLK, 2 * QBLK), BF16)],
        sem=("parallel", "parallel"))(q, k, k, v, v)


def _attn_bwd(q, k, v, do, lse, cg, g):
    d, n, _ = q.shape
    nb = n // QBLK
    scale = HEAD_DIM ** -0.5

    def body(q_ref, kp_ref, kc_ref, vp_ref, vc_ref, do_ref, l_ref, c_ref, dq_ref, dk_ref, dv_ref, ck, cv,
             s_scr, dp_scr, p_scr, ds_scr):
        i = pl.program_id(1)

        @pl.when(i == 0)
        def _():
            ck[...] = jnp.zeros_like(ck)
            cv[...] = jnp.zeros_like(cv)

        @pl.when(i < nb)
        def _():
            mask = _attn_mask(i)
            first = _first_head_lanes()
            for pr in range(A_HEADS // 2):
                ps = slice(pr * LANES, (pr + 1) * LANES)
                kc = jnp.concatenate([kp_ref[:, ps], kc_ref[:, ps]], axis=0)
                vc = jnp.concatenate([vp_ref[:, ps], vc_ref[:, ps]], axis=0)
                qs = _split_heads(q_ref[:, ps], first)
                dos = _split_heads(do_ref[:, ps], first)
                for e in range(2):
                    s_scr[2 * pr + e] = lax.dot_general(qs[e], kc, NT_DIMS, preferred_element_type=F32)
                    dp_scr[2 * pr + e] = lax.dot_general(dos[e], vc, NT_DIMS, preferred_element_type=F32)
            for h in range(A_HEADS):
                p = jnp.where(mask, jnp.exp(s_scr[h] * scale - l_ref[:, h:h + 1]), 0.0)
                p_scr[h] = p.astype(BF16)
                ds_scr[h] = (p * (dp_scr[h] + c_ref[:, h:h + 1]) * scale).astype(BF16)
            for pr in range(A_HEADS // 2):
                ps = slice(pr * LANES, (pr + 1) * LANES)
                ks = _split_heads(jnp.concatenate([kp_ref[:, ps], kc_ref[:, ps]], axis=0), first)
                qs = _split_heads(q_ref[:, ps], first)
                dos = _split_heads(do_ref[:, ps], first)
                dq = dkc = dvc = None
                for e in range(2):
                    ds = ds_scr[2 * pr + e]
                    a = jnp.dot(ds, ks[e], preferred_element_type=F32)
                    b = lax.dot_general(ds, qs[e], TN_DIMS, preferred_element_type=F32)
                    c = lax.dot_general(p_scr[2 * pr + e], dos[e], TN_DIMS, preferred_element_type=F32)
                    dq, dkc, dvc = (a, b, c) if e == 0 else (dq + a, dkc + b, dvc + c)
                dq_ref[:, ps] = dq.astype(BF16)
                dk_ref[:, ps] = (ck[:, ps] + dkc[:QBLK]).astype(BF16)
                dv_ref[:, ps] = (cv[:, ps] + dvc[:QBLK]).astype(BF16)
                ck[:, ps] = dkc[QBLK:]
                cv[:, ps] = dvc[QBLK:]

        @pl.when(i == nb)
        def _():
            dk_ref[...] = ck[...].astype(BF16)
            dv_ref[...] = cv[...].astype(BF16)

    qi = lambda i: jnp.minimum(i, nb - 1)
    cur = lambda r, i: (r, qi(i), 0)
    prev = lambda r, i: (r, jnp.maximum(qi(i) - 1, 0), 0)
    late = lambda r, i: (r, jnp.maximum(i - 1, 0), 0)
    return _call(
        body, "attn_bwd_g%d" % g, (d, nb + 1),
        [pl.BlockSpec(ATT_BLK, cur), pl.BlockSpec(ATT_BLK, prev), pl.BlockSpec(ATT_BLK, cur),
         pl.BlockSpec(ATT_BLK, prev), pl.BlockSpec(ATT_BLK, cur),
         pl.BlockSpec(ATT_BLK, cur), pl.BlockSpec(ATT_CBLK, cur), pl.BlockSpec(ATT_CBLK, cur)],
        [pl.BlockSpec(ATT_BLK, cur), pl.BlockSpec(ATT_BLK, late), pl.BlockSpec(ATT_BLK, late)],
        [jax.ShapeDtypeStruct((d, n, A_WIDTH), BF16)] * 3,
        scratch=[pltpu.VMEM((QBLK, A_WIDTH), F32), pltpu.VMEM((QBLK, A_WIDTH), F32),
                 pltpu.VMEM((A_HEADS, QBLK, 2 * QBLK), F32), pltpu.VMEM((A_HEADS, QBLK, 2 * QBLK), F32),
                 pltpu.VMEM((A_HEADS, QBLK, 2 * QBLK), BF16), pltpu.VMEM((A_HEADS, QBLK, 2 * QBLK), BF16)],
        sem=("parallel", "arbitrary"))(q, k, k, v, v, do, lse, cg)


def _merge_weights(l0, l1, l2):
    mx = jnp.maximum(jnp.maximum(l0, l1), l2)
    e0, e1, e2 = jnp.exp(l0 - mx), jnp.exp(l1 - mx), jnp.exp(l2 - mx)
    den = e0 + e1 + e2
    return e0 / den, e1 / den, e2 / den


def _even_specs(T, S):
    t8 = T // SUBLANES
    last8 = S // SUBLANES - 1
    col = lambda c: pl.BlockSpec((T, A_WIDTH), lambda i: (i, c))
    prev8 = lambda c: pl.BlockSpec((SUBLANES, A_WIDTH), lambda i: (jnp.maximum(i * t8 - 1, 0), c))
    next8 = lambda c: pl.BlockSpec((SUBLANES, A_WIDTH), lambda i: (jnp.minimum((i + 1) * t8, last8), c))
    return col, prev8, next8


GROUP_D = tuple(d for _, d in A_GROUPS)


def _even_mixer_fwd(x, proj, os_, ls_, conv_w, w_out, T):
    S = proj.shape[0]
    col, prev8, _ = _even_specs(T, S)
    H = SUBLANES

    def body(x_ref, w_ref, bg_r, cg_r, hb_r, zl_r, zh_r, cgp_r, hbp_r, o0, o1, o2, l0, l1, l2, cw_r,
             x1_ref, ut_ref, ext, cscr, *scr):
        i = pl.program_id(0)
        ls = [_compact_to_tokens(r, cscr, GROUP_D[g], T) for g, r in enumerate((l0, l1, l2))]
        expand = _head_expander()
        ws = [_segsum(w, expand) for w in _merge_weights(*ls)]
        oa = ws[0] * _to_tokens(o0, scr[0], GROUP_D[0], T)
        oa = oa + ws[1] * _to_tokens(o1, scr[1], GROUP_D[1], T)
        oa = oa + ws[2] * _to_tokens(o2, scr[2], GROUP_D[2], T)
        ext[0:H, :] = jnp.where(i == 0, 0.0, cgp_r[...] * hbp_r[...])
        ext[H:H + T, :] = cg_r[...] * hb_r[...]
        conv = cw_r[0:1, :] * ext[H - 2:H - 2 + T, :]
        for kk in range(1, SC_WIDTH):
            conv = conv + cw_r[kk:kk + 1, :] * ext[H - 2 + kk:H - 2 + kk + T, :]
        zl, zh = zl_r[...], zh_r[...]
        x1_ref[...] = x_ref[...] + _out_projection(ut_ref, w_ref, oa * (zl * _sig(zl)),
                                                   bg_r[...] * conv * (zh * _sig(zh)))

    streams = [_stream_spec(d, T) for d in GROUP_D]
    compacts = [_compact_spec(d, T) for d in GROUP_D]
    row = pl.BlockSpec((T, D_MODEL), lambda i: (i, 0))
    return _call(
        body, "even_mixer_fwd", (S // T,),
        [row, pl.BlockSpec((D_MODEL, D_MODEL), lambda i: (0, 0)),
         col(9), col(10), col(11), col(12), col(13), prev8(10), prev8(11)] + streams + compacts +
        [pl.BlockSpec((SC_WIDTH, A_WIDTH), lambda i: (0, 0))],
        [row, pl.BlockSpec((D_MODEL, T), lambda i: (0, i))],
        [jax.ShapeDtypeStruct((S, D_MODEL), F32), jax.ShapeDtypeStruct((D_MODEL, S), BF16)],
        scratch=[pltpu.VMEM((T + H, A_WIDTH), F32), pltpu.VMEM((T, LANES), F32)] + [_chunked(T)] * 3,
        sem=("parallel",))(
            x, w_out, proj, proj, proj, proj, proj, proj, proj, *os_, *ls_, conv_w)


def _even_mixer_bwd(dy, w_out, proj, os_, ls_, conv_w, T):
    S = proj.shape[0]
    nt = S // T
    col, prev8, next8 = _even_specs(T, S)
    H = SUBLANES
    t8 = T // SUBLANES
    last8 = S // SUBLANES - 1

    def body(dy_r, dyn_r, w_ref, bg_r, cg_r, hb_r, zl_r, zh_r, cgp_r, hbp_r, zhn_r, bgn_r,
             o0, o1, o2, l0, l1, l2, cw_r,
             do0, do1, do2, c0, c1, c2, dr_ref, dcw_ref, ext_t, ext_d, acc, cscr, s_a, s_b, s_c):
        i = pl.program_id(0)

        @pl.when(i == 0)
        def _():
            acc[...] = jnp.zeros_like(acc)

        zl, zh = zl_r[...], zh_r[...]
        sl, sh = _sig(zl), _sig(zh)
        du = lax.dot_general(dy_r[...].astype(BF16), w_ref[...], NT_DIMS, preferred_element_type=F32)
        dul, duh = du[:, 0:A_WIDTH], du[:, A_WIDTH:]
        dun = lax.dot_general(dyn_r[...].astype(BF16), w_ref[A_WIDTH:, :], NT_DIMS, preferred_element_type=F32)
        scr = (s_a, s_b, s_c)
        ls = [_compact_to_tokens(r, cscr, GROUP_D[g], T) for g, r in enumerate((l0, l1, l2))]
        wcs = _merge_weights(*ls)
        expand = _head_expander()
        ws = [_segsum(w, expand) for w in wcs]
        oa = ws[0] * _to_tokens(o0, scr[0], GROUP_D[0], T)
        oa = oa + ws[1] * _to_tokens(o1, scr[1], GROUP_D[1], T)
        oa = oa + ws[2] * _to_tokens(o2, scr[2], GROUP_D[2], T)
        doa = dul * (zl * sl)
        rsum = _segsum(doa * oa, _head_reducer())
        for g, (do_ref, c_ref) in enumerate(((do0, c0), (do1, c1), (do2, c2))):
            d = GROUP_D[g]
            _compact_from_tokens(c_ref, cscr, -wcs[g] * rsum, d, T)
            if d == 1:
                do_ref[0] = (ws[g] * doa).astype(BF16)
            else:
                _put(s_c, ws[g] * doa)
                _from_tokens(do_ref, s_c, d, T)
        cgv, hbv, bgv = cg_r[...], hb_r[...], bg_r[...]
        ext_t[0:H, :] = jnp.where(i == 0, 0.0, cgp_r[...] * hbp_r[...])
        ext_t[H:H + T, :] = cgv * hbv
        conv = cw_r[0:1, :] * ext_t[H - 2:H - 2 + T, :]
        for kk in range(1, SC_WIDTH):
            conv = conv + cw_r[kk:kk + 1, :] * ext_t[H - 2 + kk:H - 2 + kk + T, :]
        dyb = duh * (zh * sh)
        dconv = dyb * bgv
        zn = zhn_r[...]
        ext_d[0:T, :] = dconv
        ext_d[T:T + H, :] = jnp.where(i == nt - 1, 0.0, dun * (zn * _sig(zn)) * bgn_r[...])
        dt = cw_r[0:1, :] * ext_d[2:2 + T, :]
        for kk in range(1, SC_WIDTH):
            dt = dt + cw_r[kk:kk + 1, :] * ext_d[2 - kk:2 - kk + T, :]
        for kk in range(SC_WIDTH):
            acc[kk * SUBLANES:(kk + 1) * SUBLANES, :] += _cs8(dconv * ext_t[H - 2 + kk:H - 2 + kk + T, :])
        dr_ref[:, 0:A_WIDTH] = (dyb * conv).astype(BF16)
        dr_ref[:, A_WIDTH:2 * A_WIDTH] = (dt * hbv).astype(BF16)
        dr_ref[:, 2 * A_WIDTH:3 * A_WIDTH] = (dt * cgv).astype(BF16)
        dr_ref[:, 3 * A_WIDTH:4 * A_WIDTH] = (dul * oa * _dsilu(zl, sl)).astype(BF16)
        dr_ref[:, 4 * A_WIDTH:5 * A_WIDTH] = (duh * (bgv * conv) * _dsilu(zh, sh)).astype(BF16)

        @pl.when(i == nt - 1)
        def _():
            for kk in range(SC_WIDTH):
                dcw_ref[kk:kk + 1, :] = jnp.sum(acc[kk * SUBLANES:(kk + 1) * SUBLANES, :], axis=0, keepdims=True)

    streams = [_stream_spec(d, T) for d in GROUP_D]
    dynext = pl.BlockSpec((SUBLANES, D_MODEL), lambda i: (jnp.minimum((i + 1) * t8, last8), 0))
    compacts = [_compact_spec(d, T) for d in GROUP_D]
    outs = _call(
        body, "even_mixer_bwd", (nt,),
        [pl.BlockSpec((T, D_MODEL), lambda i: (i, 0)), dynext, pl.BlockSpec((D_MODEL, D_MODEL), lambda i: (0, 0)),
         col(9), col(10), col(11), col(12), col(13), prev8(10), prev8(11), next8(13), next8(9)] +
        streams + compacts + [pl.BlockSpec((SC_WIDTH, A_WIDTH), lambda i: (0, 0))],
        streams + compacts + [pl.BlockSpec((T, 5 * A_WIDTH), lambda i: (i, 0)),
                              pl.BlockSpec((SC_WIDTH, A_WIDTH), lambda i: (0, 0))],
        [_stream_shape(d, S, BF16) for d in GROUP_D] + [_compact_shape(d, S) for d in GROUP_D] +
        [jax.ShapeDtypeStruct((S, 5 * A_WIDTH), BF16), jax.ShapeDtypeStruct((SC_WIDTH, A_WIDTH), F32)],
        scratch=[pltpu.VMEM((T + H, A_WIDTH), F32), pltpu.VMEM((T + H, A_WIDTH), F32),
                 pltpu.VMEM((SC_WIDTH * SUBLANES, A_WIDTH), F32), pltpu.VMEM((T, LANES), F32)] +
                [_chunked(T)] * 3,
        sem=("arbitrary",))(dy, dy, w_out, proj, proj, proj, proj, proj, proj, proj, proj, proj, *os_, *ls_, conv_w)
    return outs[0:3], outs[3:6], outs[6], outs[7]


def _qk_bwd(proj, dqs, dks, dvs, rope, wq, wk, T):
    S = proj.shape[0]
    nt = S // T
    qk_w = 3 * A_WIDTH

    def body(q_ref, k_ref, dq0, dq1, dq2, dk0, dk1, dk2, dv0, dv1, dv2, c_ref, s1_ref, s2_ref, wq_ref, wk_ref,
             o_ref, dw_ref, acc, scr):
        i = pl.program_id(0)

        @pl.when(i == 0)
        def _():
            acc[...] = jnp.zeros_like(acc)
            dw_ref[...] = jnp.zeros_like(dw_ref)

        seg_mean = _seg_mean()
        c, s1, s2 = c_ref[...], s1_ref[...], s2_ref[...]
        for t, (src, w_ref, ds) in enumerate(((q_ref, wq_ref, (dq0, dq1, dq2)), (k_ref, wk_ref, (dk0, dk1, dk2)))):
            wv = w_ref[...]
            for g in range(3):
                d = GROUP_D[g]
                if d > 1:
                    _to_tokens(ds[g], scr, d, T)
                for ch in range(A_WIDTH // LANES):
                    cs = slice(g * A_WIDTH + ch * LANES, g * A_WIDTH + (ch + 1) * LANES)
                    lc = slice(ch * LANES, (ch + 1) * LANES)
                    v = src[:, cs]
                    dout = ds[g][0, :, lc].astype(F32) if d == 1 else scr[ch]
                    rs = lax.rsqrt(_segsum(v * v, seg_mean) + EPS)
                    xh = v * rs
                    dy = dout * c + pltpu.roll(dout * s1, LANES - 8, 1) + pltpu.roll(dout * s2, 8, 1)
                    acc[t * SUBLANES:(t + 1) * SUBLANES, :] += _cs8(dy * xh)
                    dxh = dy * wv
                    mean = _segsum(dxh * xh, seg_mean)
                    o_ref[:, t * qk_w + g * A_WIDTH + ch * LANES: t * qk_w + g * A_WIDTH + (ch + 1) * LANES] = (
                        rs * (dxh - xh * mean)).astype(BF16)
        for g, dv in enumerate((dv0, dv1, dv2)):
            d = GROUP_D[g]
            base = 2 * qk_w + g * A_WIDTH
            o_ref[:, base:base + A_WIDTH] = _to_tokens(dv, scr, d, T).astype(BF16)

        @pl.when(i == nt - 1)
        def _():
            for t in range(2):
                srow = jnp.sum(acc[t * SUBLANES:(t + 1) * SUBLANES, :], axis=0, keepdims=True)
                dw_ref[t:t + 1, :] = srow + pltpu.roll(srow, HEAD_DIM, 1)

    streams = [_stream_spec(d, T) for d in GROUP_D]
    return _call(
        body, "qk_bwd", (nt,),
        [pl.BlockSpec((T, qk_w), lambda i: (i, 0)), pl.BlockSpec((T, qk_w), lambda i: (i, 1))] + streams * 3 +
        [pl.BlockSpec((T, LANES), lambda i: (i, 0))] * 3 +
        [pl.BlockSpec((1, LANES), lambda i: (0, 0)), pl.BlockSpec((1, LANES), lambda i: (0, 0))],
        [pl.BlockSpec((T, 3 * qk_w), lambda i: (i, 0)), pl.BlockSpec((SUBLANES, LANES), lambda i: (0, 0))],
        [jax.ShapeDtypeStruct((S, 3 * qk_w), BF16), jax.ShapeDtypeStruct((SUBLANES, LANES), F32)],
        scratch=[pltpu.VMEM((2 * SUBLANES, LANES), F32), _chunked(T)], sem=("arbitrary",))(
            proj, proj, *dqs, *dks, *dvs, *rope, wq, wk)


N_SMALL_ODD = 40
SHIFT_ROWS_LESS = SUBLANES


def _fill_shifted(ext_ref, sh_ref):
    rows = ext_ref.shape[0] - SHIFT_ROWS_LESS
    for b in range(1, SUBLANES):
        sh_ref[b - 1] = ext_ref[b:b + rows, :]


def _window(ext_ref, sh_ref, off, T):
    a, b = divmod(off, SUBLANES)
    if b == 0:
        return ext_ref[off:off + T, :]
    return sh_ref[b - 1, a * SUBLANES:a * SUBLANES + T, :]


def _odd_pool_tile(i, uc_r, ucp_r, pw_r, ext_u, pooled_s, pm_s, T):
    H = HALO
    uc = uc_r[...]
    ext_u[0:H, :] = jnp.where(i == 0, 0.0, ucp_r[...])
    ext_u[H:H + T, :] = uc
    row = i * T + lax.broadcasted_iota(jnp.int32, (T, 1), 0)
    for g, p in enumerate(POOL_SIZES):
        cs = slice(g * LANES, (g + 1) * LANES)
        win = ext_u[H:H + T, cs]
        for j in range(1, p):
            win = win + ext_u[H - j:H - j + T, cs]
        cnt = jnp.minimum(row + 1, p).astype(F32)
        pooled = win / cnt - uc[:, cs]
        pooled_s[:, cs] = pooled
        pm_s[:, cs] = jnp.dot(pooled.astype(BF16), pw_r[g].astype(BF16), preferred_element_type=F32)
    return row


def _odd_glu_tile(i, da_r, dg_r, dap_r, dgp_r, ext_g, sh_g, T):
    H = HALO
    ext_g[0:H, :] = jnp.where(i == 0, 0.0, dap_r[...] * _sig(dgp_r[...]))
    ext_g[H:H + T, :] = da_r[...] * _sig(dg_r[...])
    _fill_shifted(ext_g, sh_g)


def _odd_specs(T, S, order):
    tb = T // HALO
    col = lambda c: pl.BlockSpec((T, A_WIDTH), lambda s: (order(s), c))
    prev = lambda c: pl.BlockSpec((HALO, A_WIDTH), lambda s: (jnp.maximum(order(s) * tb - 1, 0), c))
    const2 = lambda shape: pl.BlockSpec(shape, lambda s: (0, 0))
    weights = [pl.BlockSpec((4, LANES, LANES), lambda s: (0, 0, 0)), const2((1, A_WIDTH)),
               const2((D_CONV, A_WIDTH)), const2((1, A_WIDTH)), const2((1, A_WIDTH)), const2((1, A_WIDTH))]
    return col, prev, weights


def _odd_mixer_fwd(x, tgt, proj, pool_w, scale, dconv_w, dconv_b, ln_w, ln_b, w_out, T):
    S = proj.shape[0]
    nt = S // T
    col, prev, wspecs = _odd_specs(T, S, lambda s: s)
    H = HALO

    def body(x_ref, t_ref, w_ref, uc_r, da_r, dg_r, zl_r, zh_r, ucp_r, dap_r, dgp_r, pw_r, sc_r, dw_r, db_r,
             lw_r, lb_r, dy_ref, l_ref, ut_ref, cv_ref, ext_u, ext_g, sh_g, pooled_s, pm_s, lacc):
        i = pl.program_id(0)

        @pl.when(i == 0)
        def _():
            lacc[...] = jnp.zeros_like(lacc)

        _odd_pool_tile(i, uc_r, ucp_r, pw_r, ext_u, pooled_s, pm_s, T)
        _odd_glu_tile(i, da_r, dg_r, dap_r, dgp_r, ext_g, sh_g, T)
        base = H - (D_CONV - 1)
        conv = db_r[...] + dw_r[0:1, :] * _window(ext_g, sh_g, base, T)
        for kk in range(1, D_CONV):
            conv = conv + dw_r[kk:kk + 1, :] * _window(ext_g, sh_g, base + kk, T)
        cv_ref[...] = conv
        mu = jnp.mean(conv, axis=-1, keepdims=True)
        xc = conv - mu
        yh = xc * lax.rsqrt(jnp.mean(xc * xc, axis=-1, keepdims=True) + EPS)
        ln = yh * lw_r[...] + lb_r[...]
        zl, zh = zl_r[...], zh_r[...]
        y = x_ref[...] + _out_projection(ut_ref, w_ref, pm_s[...] * sc_r[...] * (zl * _sig(zl)),
                                         ln * _sig(ln) * (zh * _sig(zh)))
        diff = y - t_ref[...]
        dy_ref[...] = diff / float(D_MODEL)
        lacc[...] += _cs8(diff * diff)

        @pl.when(i == nt - 1)
        def _():
            l_ref[...] = jnp.sum(lacc[...], axis=0, keepdims=True)

    row = pl.BlockSpec((T, D_MODEL), lambda i: (i, 0))
    return _call(
        body, "odd_mixer_fwd", (nt,),
        [row, row, pl.BlockSpec((D_MODEL, D_MODEL), lambda i: (0, 0)),
         col(0), col(1), col(2), col(3), col(4), prev(0), prev(1), prev(2)] + wspecs,
        [row, pl.BlockSpec((1, D_MODEL), lambda i: (0, 0)), pl.BlockSpec((D_MODEL, T), lambda i: (0, i)),
         pl.BlockSpec((T, A_WIDTH), lambda i: (i, 0))],
        [jax.ShapeDtypeStruct((S, D_MODEL), F32), jax.ShapeDtypeStruct((1, D_MODEL), F32),
         jax.ShapeDtypeStruct((D_MODEL, S), BF16), jax.ShapeDtypeStruct((S, A_WIDTH), F32)],
        scratch=[pltpu.VMEM((T + H, A_WIDTH), F32), pltpu.VMEM((T + H, A_WIDTH), F32),
                 pltpu.VMEM((SUBLANES - 1, T + H - SHIFT_ROWS_LESS, A_WIDTH), F32),
                 pltpu.VMEM((T, A_WIDTH), F32), pltpu.VMEM((T, A_WIDTH), F32),
                 pltpu.VMEM((SUBLANES, D_MODEL), F32)],
        sem=("arbitrary",))(x, tgt, w_out, proj, proj, proj, proj, proj, proj, proj, proj,
                            pool_w, scale, dconv_w, dconv_b, ln_w, ln_b)


def _odd_mixer_bwd(dy, w_out, proj, conv, pool_w, scale, dconv_w, dconv_b, ln_w, ln_b, T):
    S = proj.shape[0]
    nt = S // T
    order = lambda s: nt - 1 - s
    col, prev, wspecs = _odd_specs(T, S, order)
    H = HALO

    def body(dy_r, w_ref, cv_r, uc_r, da_r, dg_r, zl_r, zh_r, ucp_r, dap_r, dgp_r, pw_r, sc_r, dw_r, db_r, lw_r, lb_r,
             dp_ref, dpw_ref, sm_ref, ext_u, ext_g, sh_g, pooled_s, pm_s, dpl_s, ext_p, ext_c, sh_c, acc):
        step = pl.program_id(0)
        i = nt - 1 - step

        @pl.when(step == 0)
        def _():
            ext_p[T:T + H, :] = jnp.zeros((H, A_WIDTH), F32)
            ext_c[T:T + H, :] = jnp.zeros((H, A_WIDTH), F32)
            acc[...] = jnp.zeros_like(acc)
            dpw_ref[...] = jnp.zeros_like(dpw_ref)

        def accum(r, v):
            acc[r * SUBLANES:(r + 1) * SUBLANES, :] += _cs8(v)

        row = _odd_pool_tile(i, uc_r, ucp_r, pw_r, ext_u, pooled_s, pm_s, T)
        _odd_glu_tile(i, da_r, dg_r, dap_r, dgp_r, ext_g, sh_g, T)
        conv = cv_r[...]
        mu = jnp.mean(conv, axis=-1, keepdims=True)
        xc = conv - mu
        rstd = lax.rsqrt(jnp.mean(xc * xc, axis=-1, keepdims=True) + EPS)
        yh = xc * rstd
        ln = yh * lw_r[...] + lb_r[...]
        sln = _sig(ln)
        zl, zh = zl_r[...], zh_r[...]
        sl, sh = _sig(zl), _sig(zh)
        du = lax.dot_general(dy_r[...].astype(BF16), w_ref[...], NT_DIMS, preferred_element_type=F32)
        dul, duh = du[:, 0:A_WIDTH], du[:, A_WIDTH:]
        pm = pm_s[...]
        scv = sc_r[...]
        dyc = dul * (zl * sl)
        accum(34, dyc * pm)
        dpm = dyc * scv
        for g in range(len(POOL_SIZES)):
            cs = slice(g * LANES, (g + 1) * LANES)
            dpm_g = dpm[:, cs].astype(BF16)
            dpw_ref[g] += lax.dot_general(pooled_s[:, cs].astype(BF16), dpm_g, TN_DIMS, preferred_element_type=F32)
            dpl_s[:, cs] = lax.dot_general(dpm_g, pw_r[g].astype(BF16), NT_DIMS, preferred_element_type=F32)
        lane_p = lax.broadcasted_iota(jnp.int32, (1, A_WIDTH), 1) // LANES
        pvec = jnp.left_shift(2, lane_p)
        cnt = jnp.minimum(row + 1, pvec).astype(F32)
        dpl = dpl_s[...]
        ext_p[0:T, :] = dpl / cnt
        for g, p in enumerate(POOL_SIZES):
            cs = slice(g * LANES, (g + 1) * LANES)
            win = ext_p[0:T, cs]
            for j in range(1, p):
                win = win + ext_p[j:j + T, cs]
            dp_ref[:, cs] = (win - dpl[:, cs]).astype(BF16)
        ext_p[T:T + H, :] = ext_p[0:H, :]
        dln = duh * (zh * sh) * _dsilu(ln, sln)
        accum(32, dln * yh)
        accum(33, dln)
        dyh = dln * lw_r[...]
        dc = rstd * (dyh - jnp.mean(dyh, axis=-1, keepdims=True) - yh * jnp.mean(dyh * yh, axis=-1, keepdims=True))
        accum(31, dc)
        ext_c[0:T, :] = dc
        _fill_shifted(ext_c, sh_c)
        base = H - (D_CONV - 1)
        dgl = dw_r[0:1, :] * _window(ext_c, sh_c, D_CONV - 1, T)
        accum(0, dc * _window(ext_g, sh_g, base, T))
        for kk in range(1, D_CONV):
            dgl = dgl + dw_r[kk:kk + 1, :] * _window(ext_c, sh_c, D_CONV - 1 - kk, T)
            accum(kk, dc * _window(ext_g, sh_g, base + kk, T))
        ext_c[T:T + H, :] = ext_c[0:H, :]
        dav, dgv = da_r[...], dg_r[...]
        sg = _sig(dgv)
        dp_ref[:, A_WIDTH:2 * A_WIDTH] = (dgl * sg).astype(BF16)
        dp_ref[:, 2 * A_WIDTH:3 * A_WIDTH] = (dgl * dav * sg * (1.0 - sg)).astype(BF16)
        dp_ref[:, 3 * A_WIDTH:4 * A_WIDTH] = (dul * (pm * scv) * _dsilu(zl, sl)).astype(BF16)
        dp_ref[:, 4 * A_WIDTH:5 * A_WIDTH] = (duh * (ln * sln) * _dsilu(zh, sh)).astype(BF16)

        @pl.when(step == nt - 1)
        def _():
            for r in range(N_SMALL_ODD):
                sm_ref[r:r + 1, :] = jnp.sum(acc[r * SUBLANES:(r + 1) * SUBLANES, :], axis=0, keepdims=True)

    ext = pltpu.VMEM((T + H, A_WIDTH), F32)
    shifted = pltpu.VMEM((SUBLANES - 1, T + H - SHIFT_ROWS_LESS, A_WIDTH), F32)
    tile = pltpu.VMEM((T, A_WIDTH), F32)
    return _call(
        body, "odd_mixer_bwd", (nt,),
        [pl.BlockSpec((T, D_MODEL), lambda s: (order(s), 0)), pl.BlockSpec((D_MODEL, D_MODEL), lambda s: (0, 0)),
         pl.BlockSpec((T, A_WIDTH), lambda s: (order(s), 0)),
         col(0), col(1), col(2), col(3), col(4), prev(0), prev(1), prev(2)] + wspecs,
        [pl.BlockSpec((T, ODD_IN), lambda s: (order(s), 0)),
         pl.BlockSpec((4, LANES, LANES), lambda s: (0, 0, 0)),
         pl.BlockSpec((N_SMALL_ODD, A_WIDTH), lambda s: (0, 0))],
        [jax.ShapeDtypeStruct((S, ODD_IN), BF16), jax.ShapeDtypeStruct((4, LANES, LANES), F32),
         jax.ShapeDtypeStruct((N_SMALL_ODD, A_WIDTH), F32)],
        scratch=[ext, ext, shifted, tile, tile, tile, ext, ext, shifted,
                 pltpu.VMEM((N_SMALL_ODD * SUBLANES, A_WIDTH), F32)],
        sem=("arbitrary",))(dy, w_out, conv, proj, proj, proj, proj, proj, proj, proj, proj,
                            pool_w, scale, dconv_w, dconv_b, ln_w, ln_b)


TILE_SEQ = 256
TILE_WG = 256
TILE_FIRST = 512
TILE_MM = 512


SMALL_PACK = "small_pack"
SMALL_PACK_W = 2 * LANES
LATE_WEIGHTS = ("e_w_out", "o_w_in", "o_w_out", SMALL_PACK)
ODD_MATS = ("o_w_in", "o_w_out")
EVEN_MATS = ("e_w_in", "e_w_out")


def _reduce_start(names, grads, grads16, cidx):
    recv = _swap_to_sibling(names, [grads16[n] for n in names], "swap_halves_" + names[0][0], True)
    both = [_add_half(cidx, grads[n], r, n) for n, r in zip(names, recv)]
    return [h for h, _ in both], [hb for _, hb in both]


def _local_step(x, pos, tgt, shards, p, unpack_small, cidx, bidx):
    T = TILE_SEQ
    freq = _freq_table()
    wq = jnp.tile(p["e_q_norm_w"], (1, LANES // HEAD_DIM))
    wk = jnp.tile(p["e_k_norm_w"], (1, LANES // HEAD_DIM))

    proj_e, ht_e, w_e_in, late = _inproj_gathering(x, p["e_norm_w"], shards["e_w_in"], bidx, "e_w_in", LATE_WEIGHTS,
                                                   [shards[n] for n in LATE_WEIGHTS], TILE_FIRST, "inproj_even")
    wb = dict(zip(LATE_WEIGHTS, late), e_w_in=w_e_in)
    p = dict(p, **unpack_small(wb[SMALL_PACK]))
    qkv = _qkv_prep(proj_e, pos, freq, wq, wk, T)
    qs, ks, vs, rope = qkv[0:3], qkv[3:6], qkv[6:9], qkv[9:12]
    os_, ls_ = [], []
    for g in range(3):
        o, l = _attn_fwd(qs[g], ks[g], vs[g], g)
        os_.append(o)
        ls_.append(l)
    x1, ut_e = _even_mixer_fwd(x, proj_e, os_, ls_, p["e_conv_w"], wb["e_w_out"], T)
    proj_o, ht_o = _inproj(x1, p["o_norm_w"], wb["o_w_in"], TILE_MM, 1280, "inproj_odd")
    odd_w = (p["o_pool_w"], p["o_pool_scale"], p["o_dconv_w"], p["o_dconv_b"], p["o_ln_w"], p["o_ln_b"])
    dy, lsum, ut_o, conv_o = _odd_mixer_fwd(x1, tgt, proj_o, *odd_w, wb["o_w_out"], T)

    g, g16 = {}, {}
    g["o_w_out"], g16["o_w_out"] = _mm_wgrad(ut_o, [dy], TILE_WG, "wgrad_o_out")
    dproj_o, g["o_pool_w"], small_o = _odd_mixer_bwd(dy, wb["o_w_out"], proj_o, conv_o, *odd_w, T)
    g["o_w_in"], g16["o_w_in"] = _mm_wgrad(ht_o, [dproj_o], TILE_WG, "wgrad_o_in")
    half_o, half_o16 = _reduce_start(ODD_MATS, g, g16, cidx)
    dx1, g["o_norm_w"], blocks_o = _mm_nt_rms([dproj_o], wb["o_w_in"], x1, p["o_norm_w"], dy, TILE_MM, "dx_odd",
                                              ODD_MATS, half_o16)
    g["o_dconv_w"] = small_o[0:D_CONV]
    g["o_dconv_b"] = small_o[31:32]
    g["o_ln_w"] = small_o[32:33]
    g["o_ln_b"] = small_o[33:34]
    g["o_pool_scale"] = small_o[34:35]

    g["e_w_out"], g16["e_w_out"] = _mm_wgrad(ut_e, [dx1], TILE_WG, "wgrad_e_out")
    dos, cgs, drest, g["e_conv_w"] = _even_mixer_bwd(dx1, wb["e_w_out"], proj_e, os_, ls_, p["e_conv_w"], T)
    dqs, dks, dvs = [], [], []
    for gi in range(3):
        dq, dk, dv = _attn_bwd(qs[gi], ks[gi], vs[gi], dos[gi], ls_[gi], cgs[gi], gi)
        dqs.append(dq)
        dks.append(dk)
        dvs.append(dv)
    dqkv, dnw = _qk_bwd(proj_e, dqs, dks, dvs, rope, wq, wk, T)
    g["e_q_norm_w"] = dnw[0:1, 0:HEAD_DIM]
    g["e_k_norm_w"] = dnw[1:2, 0:HEAD_DIM]
    pieces = [dqkv, drest]
    g["e_w_in"], g16["e_w_in"] = _mm_wgrad(ht_e, pieces, TILE_WG, "wgrad_e_in")
    half_e, half_e16 = _reduce_start(EVEN_MATS, g, g16, cidx)
    dx, g["e_norm_w"], blocks_e = _mm_nt_rms(pieces, wb["e_w_in"], x, p["e_norm_w"], dx1, TILE_MM, "dx_even",
                                             EVEN_MATS, half_e16)
    parts = {}
    for names, halves, blocks in ((ODD_MATS, half_o, blocks_o), (EVEN_MATS, half_e, blocks_e)):
        for n, h, r in zip(names, halves, blocks):
            parts[n] = _add_blocks(bidx, h, r, n)
    return lsum, dx, g, parts


BIG = ("e_w_in", "e_w_out", "o_w_in", "o_w_out")
SHARD_AXIS = {"e_w_in": 1, "e_w_out": 0, "o_w_in": 1, "o_w_out": 0, SMALL_PACK: 0}
N_CHIPS = 4


def _place():
    x, y, c = lax.axis_index("x"), lax.axis_index("y"), lax.axis_index("c")
    chips = [(1 - x, y), (x, 1 - y), (1 - x, 1 - y)]
    return x, y, c, chips


def _block_of(ref, name, block):
    rows, cols = ref.shape
    if SHARD_AXIS[name] == 1:
        cw = cols // N_CHIPS
        return ref.at[:, pl.ds(pl.multiple_of(block * cw, LANES), cw)]
    rw = rows // N_CHIPS
    return ref.at[pl.ds(pl.multiple_of(block * rw, rw), rw), :]


def _half_of(ref, name, half):
    rows, cols = ref.shape
    if SHARD_AXIS[name] == 1:
        return ref.at[pl.ds(pl.multiple_of(half * (rows // 2), rows // 2), rows // 2), :]
    return ref.at[:, pl.ds(pl.multiple_of(half * (cols // 2), LANES), cols // 2)]


def _sub(ref, name, block, half):
    rows, cols = ref.shape
    if SHARD_AXIS[name] == 1:
        cw, hr = cols // N_CHIPS, rows // 2
        return ref.at[pl.ds(pl.multiple_of(half * hr, hr), hr), pl.ds(pl.multiple_of(block * cw, LANES), cw)]
    rw, hc = rows // N_CHIPS, cols // 2
    return ref.at[pl.ds(pl.multiple_of(block * rw, rw), rw), pl.ds(pl.multiple_of(half * hc, LANES), hc)]


GATHER_COPIES = 7


class _Gather:
    def __init__(self, names, s_refs, f_refs, send, recv):
        self.names, self.s, self.f, self.send, self.recv = names, s_refs, f_refs, send, recv

    def _copy(self, k, src, dst, to):
        return pltpu.make_async_remote_copy(src_ref=src, dst_ref=dst, send_sem=self.send.at[k],
                                            recv_sem=self.recv.at[k], device_id=to, device_id_type=MESH)

    def _plan(self):
        x, y, c, chips = _place()
        me, sib = 2 * x + y, (x, y, 1 - c)
        first, relay_in, relay, last_in = [], [], [], []
        for wi, n in enumerate(self.names):
            k0 = wi * GATHER_COPIES
            s, f = self.s[wi], self.f[wi]
            own = _block_of(f, n, me)
            first.append(self._copy(k0 + 3, s, own, sib))
            last_in.append(self._copy(k0 + 3, s, own, sib))
            for j, (cx, cy) in enumerate(chips):
                first.append(self._copy(k0 + j, _half_of(s, n, c), _sub(f, n, me, c), (cx, cy, c)))
                mine = _sub(f, n, 2 * cx + cy, c)
                relay_in.append(self._copy(k0 + j, mine, mine, sib))
                relay.append(self._copy(k0 + 4 + j, mine, mine, sib))
                theirs = _sub(f, n, 2 * cx + cy, 1 - c)
                last_in.append(self._copy(k0 + 4 + j, theirs, theirs, sib))
        return first, relay_in, relay, last_in

    N_RELATIONS = 3

    def begin(self, relations=(0, 1, 2), sibling=True):
        first = self._plan()[0]
        for wi in range(len(self.names)):
            mine = first[wi * (1 + self.N_RELATIONS):(wi + 1) * (1 + self.N_RELATIONS)]
            if sibling:
                mine[0].start()
            for j in relations:
                mine[1 + j].start()

    def relay(self, relations=(0, 1, 2)):
        _, relay_in, relay, _ = self._plan()
        for wi in range(len(self.names)):
            for j in relations:
                relay_in[wi * self.N_RELATIONS + j].wait_recv()
                relay[wi * self.N_RELATIONS + j].start()

    def end(self):
        first, _, relay, last_in = self._plan()
        for cp in last_in:
            cp.wait_recv()
        for cp in first + relay:
            cp.wait_send()

    def wait_relayed(self, j):
        self._plan()[3][1 + j].wait_recv()

    def end_rest(self):
        first, _, relay, last_in = self._plan()
        last_in[0].wait_recv()
        for cp in first + relay:
            cp.wait_send()


def _full_shape(n, s):
    r, cdim = s.shape
    return jax.ShapeDtypeStruct((r, cdim * N_CHIPS) if SHARD_AXIS[n] == 1 else (r * N_CHIPS, cdim), s.dtype)


def _gather_sems(names):
    k = GATHER_COPIES * len(names)
    return [pltpu.SemaphoreType.DMA((k,)), pltpu.SemaphoreType.DMA((k,))]


def _scatter_copies(names, h_refs, r_refs, send, recv):
    _, _, c, chips = _place()
    cps = []
    for wi, n in enumerate(names):
        for j, (cx, cy) in enumerate(chips):
            cps.append(pltpu.make_async_remote_copy(
                src_ref=_block_of(h_refs[wi], n, 2 * cx + cy), dst_ref=r_refs[wi].at[j],
                send_sem=send.at[wi * 3 + j], recv_sem=recv.at[wi * 3 + j],
                device_id=(cx, cy, c), device_id_type=MESH))
    return cps


def _scatter_sems(names):
    return [pltpu.SemaphoreType.DMA((3 * len(names),)), pltpu.SemaphoreType.DMA((3 * len(names),))]


class _SmallSum:
    def __init__(self, p_ref, o_ref, sbuf, cbuf, send, recv):
        self.p, self.o, self.sbuf, self.cbuf, self.send, self.recv = p_ref, o_ref, sbuf, cbuf, send, recv

    def _copy(self, k, ref, to):
        return pltpu.make_async_remote_copy(src_ref=ref, dst_ref=ref, send_sem=self.send.at[k],
                                            recv_sem=self.recv.at[k], device_id=to, device_id_type=MESH)

    def _plan(self):
        x, y, c, chips = _place()
        me, sib = 2 * x + y, (x, y, 1 - c)
        d2d_out = self._copy(0, self.sbuf.at[c], sib)
        d2d_in = self._copy(0, self.sbuf.at[1 - c], sib)
        ici_out = [self._copy(1 + j, self.cbuf.at[me], (cx, cy, c)) for j, (cx, cy) in enumerate(chips)]
        ici_in = [self._copy(1 + j, self.cbuf.at[2 * cx + cy], (cx, cy, c)) for j, (cx, cy) in enumerate(chips)]
        return c, me, d2d_out, d2d_in, ici_out, ici_in

    def begin(self):
        c, _, d2d_out, _, _, _ = self._plan()
        self.sbuf[c] = self.p[...]
        d2d_out.start()

    def middle(self):
        _, me, _, d2d_in, ici_out, _ = self._plan()
        d2d_in.wait_recv()
        self.cbuf[me] = self.sbuf[0] + self.sbuf[1]
        for cp in ici_out:
            cp.start()

    def end(self):
        _, _, d2d_out, _, ici_out, ici_in = self._plan()
        for cp in ici_in:
            cp.wait_recv()
        self.o[...] = (self.cbuf[0] + self.cbuf[1]) + (self.cbuf[2] + self.cbuf[3])
        for cp in [d2d_out] + ici_out:
            cp.wait_send()


def _small_sum_scratch(R):
    return [pltpu.VMEM((2, R, LANES), F32), pltpu.VMEM((N_CHIPS, R, LANES), F32),
            pltpu.SemaphoreType.DMA((4,)), pltpu.SemaphoreType.DMA((4,))]


def _half_shape(shape, name):
    r, cdim = shape
    return (r // 2, cdim) if SHARD_AXIS[name] == 1 else (r, cdim // 2)


def _shard_shape(shape, name):
    r, cdim = shape
    return (r, cdim // N_CHIPS) if SHARD_AXIS[name] == 1 else (r // N_CHIPS, cdim)


def _swap_to_sibling(names, srcs, name, pick_half):
    nw = len(names)

    def body(*refs):
        g_refs, r_refs = refs[:nw], refs[nw:2 * nw]
        send, recv = refs[2 * nw:]
        x, y, c, _ = _place()
        sib = (x, y, 1 - c)
        cps = []
        for wi, n in enumerate(names):
            src = _half_of(g_refs[wi], n, 1 - c) if pick_half else g_refs[wi]
            cp = pltpu.make_async_remote_copy(src_ref=src, dst_ref=r_refs[wi], send_sem=send.at[wi],
                                              recv_sem=recv.at[wi], device_id=sib, device_id_type=MESH)
            cp.start()
            cps.append(cp)
        for cp in cps:
            cp.wait()

    outs = [jax.ShapeDtypeStruct(_half_shape(g.shape, n) if pick_half else g.shape, g.dtype)
            for n, g in zip(names, srcs)]
    return pl.pallas_call(
        body, name=name, in_specs=[ANY] * nw, out_specs=[ANY] * nw, out_shape=outs,
        scratch_shapes=[pltpu.SemaphoreType.DMA((nw,)), pltpu.SemaphoreType.DMA((nw,))],
    )(*srcs)


def _add_half(cidx, g, r, name):
    rows, cols = r.shape
    tr = 256
    tc = cols if cols <= 1792 else (1792 if cols % 1792 == 0 else 1280)
    nr, nc = rows // tr, cols // tc

    def body(c_ref, g_ref, r_ref, o_ref, ob_ref):
        s = g_ref[...] + r_ref[...].astype(F32)
        o_ref[...] = s
        ob_ref[...] = s.astype(BF16)

    if SHARD_AXIS[name] == 1:
        gmap = lambda i, j, c_ref: (c_ref[0] * nr + i, j)
    else:
        gmap = lambda i, j, c_ref: (i, c_ref[0] * nc + j)
    same = lambda i, j, c_ref: (i, j)
    return pl.pallas_call(
        body, name="add_half_" + name,
        grid_spec=pltpu.PrefetchScalarGridSpec(
            num_scalar_prefetch=1, grid=(nr, nc),
            in_specs=[pl.BlockSpec((tr, tc), gmap), pl.BlockSpec((tr, tc), same)],
            out_specs=[pl.BlockSpec((tr, tc), same), pl.BlockSpec((tr, tc), same)]),
        out_shape=[jax.ShapeDtypeStruct(r.shape, F32), jax.ShapeDtypeStruct(r.shape, BF16)],
        compiler_params=pltpu.CompilerParams(dimension_semantics=("parallel", "parallel"), vmem_limit_bytes=VMEM_LIMIT),
    )(cidx, g, r)


def _add_blocks(bidx, h, r, name):
    _, rows, cols = r.shape
    tr = min(rows, 256)
    nr = rows // tr

    def body(b_ref, h_ref, r0, r1, r2, o_ref):
        o_ref[...] = ((h_ref[...] + r0[0].astype(F32)) + r1[0].astype(F32)) + r2[0].astype(F32)

    if SHARD_AXIS[name] == 1:
        hmap = lambda i, b_ref: (i, b_ref[0])
    else:
        hmap = lambda i, b_ref: (b_ref[0] * nr + i, 0)
    rspec = lambda j: pl.BlockSpec((1, tr, cols), lambda i, b_ref, j=j: (j, i, 0))
    return pl.pallas_call(
        body, name="add_blocks_" + name,
        grid_spec=pltpu.PrefetchScalarGridSpec(
            num_scalar_prefetch=1, grid=(nr,),
            in_specs=[pl.BlockSpec((tr, cols), hmap), rspec(0), rspec(1), rspec(2)],
            out_specs=pl.BlockSpec((tr, cols), lambda i, b_ref: (i, 0))),
        out_shape=jax.ShapeDtypeStruct((rows, cols), F32),
        compiler_params=pltpu.CompilerParams(dimension_semantics=("parallel",), vmem_limit_bytes=VMEM_LIMIT),
    )(bidx, h, r, r, r)


def _adam_math(w, g, m, v):
    c1 = 1.0 - ADAM_B1 ** ADAM_STEP
    c2 = 1.0 - ADAM_B2 ** ADAM_STEP
    nm = ADAM_B1 * m + (1.0 - ADAM_B1) * g
    nv = ADAM_B2 * v + (1.0 - ADAM_B2) * (g * g)
    delta = -ADAM_LR * ((nm / c1) / (jnp.sqrt(nv / c2) + ADAM_EPS) + ADAM_WD * w)
    return delta, nm, nv


def _adamw(w, g, m, v, name):
    def body(w_ref, g_ref, m_ref, v_ref, d_ref, nm_ref, nv_ref):
        d_ref[...], nm_ref[...], nv_ref[...] = _adam_math(w_ref[...], g_ref[...], m_ref[...], v_ref[...])

    spec = pl.BlockSpec(w.shape, lambda i: (0, 0))
    return _call(body, "adamw_" + name, (1,), [spec] * 4, [spec] * 3,
                 [jax.ShapeDtypeStruct(w.shape, F32)] * 3, sem=("arbitrary",))(w, g, m, v)


def _adamw_halves(cidx, w, mine, theirs, m, v, name, small=None):
    rows, cols = w.shape
    hr, hc = mine.shape
    tr = 128
    ni = hr // tr
    if SHARD_AXIS[name] == 1:
        wmap = lambda hh, i, c_ref: (hh * ni + i, 0)
    else:
        wmap = lambda hh, i, c_ref: (i, hh)
    hmap = lambda hh, i, c_ref: (i, 0)
    ride = small is not None
    n_steps = 2 * ni

    def body(c_ref, w_ref, a_ref, b_ref, m_ref, v_ref, *rest):
        if ride:
            p_ref, g_ref, d_ref, nm_ref, nv_ref, o_ref = rest[:6]
            step = pl.program_id(0) * ni + pl.program_id(1)
            total = _SmallSum(p_ref, o_ref, *rest[6:])
            pl.when(step == 0)(total.begin)
            pl.when(step == 1)(total.middle)
        else:
            g_ref, d_ref, nm_ref, nv_ref = rest
        g = jnp.where(pl.program_id(0) == c_ref[0], a_ref[...], b_ref[...])
        g_ref[...] = g
        d_ref[...], nm_ref[...], nv_ref[...] = _adam_math(w_ref[...], g, m_ref[...], v_ref[...])
        if ride:
            pl.when(step == n_steps - 1)(total.end)

    wspec = pl.BlockSpec((tr, hc), wmap)
    hspec = pl.BlockSpec((tr, hc), hmap)
    whole = [pl.BlockSpec(small.shape, lambda hh, i, c_ref: (0, 0))] if ride else []
    return pl.pallas_call(
        body, name="adamw_" + name,
        grid_spec=pltpu.PrefetchScalarGridSpec(
            num_scalar_prefetch=1, grid=(2, ni),
            in_specs=[wspec, hspec, hspec, wspec, wspec] + whole, out_specs=[wspec] * 4 + whole,
            scratch_shapes=_small_sum_scratch(small.shape[0]) if ride else []),
        out_shape=[jax.ShapeDtypeStruct(w.shape, F32)] * 4 + ([jax.ShapeDtypeStruct(small.shape, F32)] if ride else []),
        compiler_params=pltpu.CompilerParams(
            dimension_semantics=("arbitrary", "arbitrary") if ride else ("parallel", "parallel"),
            vmem_limit_bytes=VMEM_LIMIT),
    )(cidx, w, mine, theirs, m, v, *([small] if ride else []))


SMALL = ("e_norm_w", "e_q_norm_w", "e_k_norm_w", "e_conv_w", "o_norm_w", "o_pool_w", "o_pool_scale",
         "o_dconv_w", "o_dconv_b", "o_ln_w", "o_ln_b")
SMALL_SHARDED = ("e_conv_w", "o_norm_w", "o_pool_scale", "o_dconv_w", "o_dconv_b", "o_ln_w", "o_ln_b")
WEIGHTS = ("e_norm_w", "e_w_in", "e_q_norm_w", "e_k_norm_w", "e_conv_w", "e_w_out", "o_norm_w", "o_w_in",
           "o_pool_w", "o_pool_scale", "o_dconv_w", "o_dconv_b", "o_ln_w", "o_ln_b", "o_w_out")


def _pack(arrs):
    flat = jnp.concatenate([a.reshape(-1) for a in arrs])
    rows = -(-flat.shape[0] // (LANES * SUBLANES)) * SUBLANES
    flat = jnp.pad(flat, (0, rows * LANES - flat.shape[0]))
    return flat.reshape(rows, LANES)


def _unpack(packed, shapes):
    flat = packed.reshape(-1)
    out, off = [], 0
    for s in shapes:
        n = int(np.prod(s))
        out.append(flat[off:off + n].reshape(s))
        off += n
    return out


def _gather_last(a, block, width):
    return lax.dynamic_slice_in_dim(a, block * width, width, axis=a.ndim - 1)


def kernel(x, positions, e_norm_w, e_w_in, e_q_norm_w, e_k_norm_w, e_conv_w, e_w_out, o_norm_w, o_w_in, o_pool_w, o_pool_scale, o_dconv_w, o_dconv_b, o_ln_w, o_ln_b, o_w_out, loss_target, m_e_norm_w, m_e_w_in, m_e_q_norm_w, m_e_k_norm_w, m_e_conv_w, m_e_w_out, m_o_norm_w, m_o_w_in, m_o_pool_w, m_o_pool_scale, m_o_dconv_w, m_o_dconv_b, m_o_ln_w, m_o_ln_b, m_o_w_out, v_e_norm_w, v_e_w_in, v_e_q_norm_w, v_e_k_norm_w, v_e_conv_w, v_e_w_out, v_o_norm_w, v_o_w_in, v_o_pool_w, v_o_pool_scale, v_o_dconv_w, v_o_dconv_b, v_o_ln_w, v_o_ln_b, v_o_w_out):
    given = dict(e_norm_w=e_norm_w, e_w_in=e_w_in, e_q_norm_w=e_q_norm_w, e_k_norm_w=e_k_norm_w, e_conv_w=e_conv_w,
                 e_w_out=e_w_out, o_norm_w=o_norm_w, o_w_in=o_w_in, o_pool_w=o_pool_w, o_pool_scale=o_pool_scale,
                 o_dconv_w=o_dconv_w, o_dconv_b=o_dconv_b, o_ln_w=o_ln_w, o_ln_b=o_ln_b, o_w_out=o_w_out)
    mom = dict(e_norm_w=m_e_norm_w, e_w_in=m_e_w_in, e_q_norm_w=m_e_q_norm_w, e_k_norm_w=m_e_k_norm_w,
               e_conv_w=m_e_conv_w, e_w_out=m_e_w_out, o_norm_w=m_o_norm_w, o_w_in=m_o_w_in, o_pool_w=m_o_pool_w,
               o_pool_scale=m_o_pool_scale, o_dconv_w=m_o_dconv_w, o_dconv_b=m_o_dconv_b, o_ln_w=m_o_ln_w,
               o_ln_b=m_o_ln_b, o_w_out=m_o_w_out)
    var = dict(e_norm_w=v_e_norm_w, e_w_in=v_e_w_in, e_q_norm_w=v_e_q_norm_w, e_k_norm_w=v_e_k_norm_w,
               e_conv_w=v_e_conv_w, e_w_out=v_e_w_out, o_norm_w=v_o_norm_w, o_w_in=v_o_w_in, o_pool_w=v_o_pool_w,
               o_pool_scale=v_o_pool_scale, o_dconv_w=v_o_dconv_w, o_dconv_b=v_o_dconv_b, o_ln_w=v_o_ln_w,
               o_ln_b=v_o_ln_b, o_w_out=v_o_w_out)
    S = x.shape[1]
    mx, my, mc = lax.axis_index("x"), lax.axis_index("y"), lax.axis_index("c")
    chip = 2 * mx + my
    cidx = jnp.reshape(mc, (1,)).astype(jnp.int32)
    bidx = jnp.reshape(chip, (1,)).astype(jnp.int32)

    shards = {n: given[n][0].astype(BF16) for n in BIG}
    shard_sizes = [int(np.prod(given[n].shape)) for n in SMALL_SHARDED]
    flat = jnp.concatenate([given[n].reshape(-1) for n in SMALL_SHARDED])
    rows = -(-flat.shape[0] // (SMALL_PACK_W * SUBLANES)) * SUBLANES
    shards[SMALL_PACK] = jnp.pad(flat, (0, rows * SMALL_PACK_W - flat.shape[0])).reshape(rows, SMALL_PACK_W)

    def unpack_small(full):
        gathered = full.reshape(N_CHIPS, rows * SMALL_PACK_W)
        out, off = {}, 0
        for n, size in zip(SMALL_SHARDED, shard_sizes):
            sh = given[n].shape[1:]
            parts = gathered[:, off:off + size].reshape((N_CHIPS,) + sh)
            fullp = jnp.moveaxis(parts, 0, -2).reshape(sh[:-1] + (N_CHIPS * sh[-1],))
            out[n] = fullp.reshape(-1, fullp.shape[-1])
            off += size
        return out

    p = dict(e_norm_w=e_norm_w, e_q_norm_w=e_q_norm_w, e_k_norm_w=e_k_norm_w, o_pool_w=o_pool_w[0])
    lsum, dx, g, parts = _local_step(x[0], positions.reshape(S, 1), loss_target[0], shards, p, unpack_small,
                                     cidx, bidx)
    theirs = _swap_to_sibling(BIG, [parts[n] for n in BIG], "swap_reduced", False)

    mine = [g[n] for n in SMALL] + [(0.5 / float(D_MODEL)) * jnp.sum(lsum, keepdims=True)]
    grads, delta, new_m, new_v = {}, {}, {}, {}
    for n, other in zip(BIG, theirs):
        sh = given[n].shape
        outs = _adamw_halves(cidx, given[n][0], parts[n], other, mom[n][0], var[n][0], n,
                             small=_pack(mine) if n == BIG[0] else None)
        grads[n], delta[n], new_m[n], new_v[n] = [a.reshape(sh) for a in outs[:4]]
        if n == BIG[0]:
            tot = _unpack(outs[4], [a.shape for a in mine])
    loss = tot[-1].reshape(())
    for n, gv in zip(SMALL, tot):
        if n in SMALL_SHARDED:
            gv = _gather_last(gv, chip, gv.shape[-1] // N_CHIPS)
        grads[n] = gv.reshape(given[n].shape)
    big_small = "o_pool_w"
    pw = [src[big_small].reshape(-1, LANES) for src in (given, grads, mom, var)]
    for dst, a in zip((delta, new_m, new_v), _adamw(*pw, "pool_w")):
        dst[big_small] = a.reshape(given[big_small].shape)
    tiny = tuple(n for n in SMALL if n != big_small)
    shapes = [given[n].shape for n in tiny]
    packed = [_pack([src[n] for n in tiny]) for src in (given, grads, mom, var)]
    for dst, pk in zip((delta, new_m, new_v), _adamw(*packed, "small")):
        for n, a in zip(tiny, _unpack(pk, shapes)):
            dst[n] = a
    return (loss, dx[None], *[grads[n] for n in WEIGHTS], *[delta[n] for n in WEIGHTS],
            *[new_m[n] for n in WEIGHTS], *[new_v[n] for n in WEIGHTS])
```

```python
import numpy as np
import jax
import jax.numpy as jnp
from jax import lax
from jax.experimental import pallas as pl
from jax.experimental.pallas import tpu as pltpu

F32 = jnp.float32
BF16 = jnp.bfloat16

D_MODEL = 1024
HEAD_DIM = 64
A_WIDTH = 512
A_HEADS = 8
A_GROUPS = ((128, 1), (512, 4), (2048, 16))
QBLK = 128
ROT_DIM = 16
ROPE_THETA = 500000.0
POOL_SIZES = (2, 4, 8, 16)
D_CONV = 31
SC_WIDTH = 3
EVEN_IN = 7168
ODD_IN = 2560
EPS = 1e-6
NEG = -1e30
ADAM_LR, ADAM_B1, ADAM_B2, ADAM_EPS, ADAM_WD, ADAM_STEP = 0.001, 0.9, 0.999, 1e-08, 0.01, 10

LANES = 128
SUBLANES = 8
HALO = 32
VMEM_LIMIT = 52 * 1024 * 1024
MESH = pl.DeviceIdType.MESH
ANY = pl.BlockSpec(memory_space=pl.ANY)

NT_DIMS = (((1,), (1,)), ((), ()))
TN_DIMS = (((0,), (0,)), ((), ()))


def _call(body, name, grid, in_specs, out_specs, out_shape, scratch=(), sem=None, aliases=None):
    return pl.pallas_call(
        body, name=name, grid=grid, in_specs=in_specs, out_specs=out_specs, out_shape=out_shape,
        scratch_shapes=list(scratch), input_output_aliases=aliases or {},
        compiler_params=pltpu.CompilerParams(dimension_semantics=sem, vmem_limit_bytes=VMEM_LIMIT))


def _sig(v):
    return jax.nn.sigmoid(v)


def _dsilu(v, s):
    return s * (1.0 + v * (1.0 - s))


def _out_projection(ut_ref, w_ref, lo, hi):
    acc = None
    for k, v in enumerate((lo, hi)):
        ut_ref[k * A_WIDTH:(k + 1) * A_WIDTH, :] = v.T.astype(BF16)
        part = jnp.dot(v.astype(BF16), w_ref[k * A_WIDTH:(k + 1) * A_WIDTH, :], preferred_element_type=F32)
        acc = part if acc is None else acc + part
    return acc


def _cs8(v):
    return v.reshape(v.shape[0] // SUBLANES, SUBLANES, v.shape[1]).sum(axis=0)


def _seg_mean():
    r = lax.broadcasted_iota(jnp.int32, (LANES, LANES), 0) // HEAD_DIM
    c = lax.broadcasted_iota(jnp.int32, (LANES, LANES), 1) // HEAD_DIM
    return jnp.where(r == c, 1.0 / HEAD_DIM, 0.0).astype(BF16)


def _segsum(v, ones):
    hi = v.astype(BF16)
    lo = (v - hi.astype(F32)).astype(BF16)
    return (jnp.dot(hi, ones, preferred_element_type=F32) + jnp.dot(lo, ones, preferred_element_type=F32))


def _head_rstd(v, seg_mean):
    return lax.rsqrt(jnp.dot((v * v).astype(BF16), seg_mean, preferred_element_type=F32) + EPS)


def _rope_tables(pos_ref, freq_ref):
    ang = pos_ref[...].astype(F32) * freq_ref[...]
    cosv, sinv = jnp.cos(ang), jnp.sin(ang)
    lm = lax.broadcasted_iota(jnp.int32, ang.shape, 1) % HEAD_DIM
    half = ROT_DIM // 2
    c = jnp.where(lm < ROT_DIM, cosv, 1.0)
    s1 = jnp.where((lm >= half) & (lm < ROT_DIM), sinv, 0.0)
    s2 = jnp.where(lm < half, -sinv, 0.0)
    return c, s1, s2


def _freq_table():
    half = ROT_DIM // 2
    inv = ROPE_THETA ** (-np.arange(half, dtype=np.float64) / half)
    lane = np.arange(LANES) % HEAD_DIM
    f = np.where(lane < ROT_DIM, inv[lane % half], 0.0)
    return jnp.asarray(f.reshape(1, LANES), F32)


def _load_once(hbm_ref, vmem_ref, sem):
    @pl.when(pl.program_id(0) == 0)
    def _():
        cp = pltpu.make_async_copy(hbm_ref, vmem_ref, sem)
        cp.start()
        cp.wait()


def _rms_rows(x_ref, nw_ref):
    xv = x_ref[...]
    ms = jnp.mean(xv * xv, axis=-1, keepdims=True)
    return xv * lax.rsqrt(ms + EPS) * nw_ref[...]


def _inproj(x, nw, w, tm, tn, name):
    S, N = x.shape[0], w.shape[1]

    def body(x_ref, nw_ref, w_hbm, o_ref, ht_ref, w_v, sem):
        _load_once(w_hbm, w_v, sem)
        h = _rms_rows(x_ref, nw_ref)
        ht_ref[...] = h.T.astype(BF16)
        hb = h.astype(BF16)
        for j in range(N // tn):
            o_ref[:, j * tn:(j + 1) * tn] = jnp.dot(hb, w_v[:, j * tn:(j + 1) * tn], preferred_element_type=F32)

    return _call(
        body, name, (S // tm,),
        [pl.BlockSpec((tm, D_MODEL), lambda i: (i, 0)),
         pl.BlockSpec((1, D_MODEL), lambda i: (0, 0)), ANY],
        [pl.BlockSpec((tm, N), lambda i: (i, 0)),
         pl.BlockSpec((D_MODEL, tm), lambda i: (0, i))],
        [jax.ShapeDtypeStruct((S, N), F32), jax.ShapeDtypeStruct((D_MODEL, S), BF16)],
        scratch=[pltpu.VMEM(w.shape, BF16), pltpu.SemaphoreType.DMA(())], sem=("arbitrary",))(x, nw, w)


def _inproj_gathering(x, nw, shard, bidx, first, late_names, late_shards, tm, name):
    S = x.shape[0]
    ni = S // tm
    K, cw = shard.shape
    nl = len(late_names)
    last = N_CHIPS - 1

    def body(b_ref, x_ref, nw_ref, s_hbm, *rest):
        ls_refs = rest[:nl]
        o_ref, ht_ref, f_hbm = rest[nl:nl + 3]
        lf_refs = rest[nl + 3:2 * nl + 3]
        hs, w_blk, lsem, send1, recv1, send2, recv2 = rest[2 * nl + 3:]
        j, i = pl.program_id(0), pl.program_id(1)
        g1 = _Gather((first,), (s_hbm,), (f_hbm,), send1, recv1)
        g2 = _Gather(late_names, ls_refs, lf_refs, send2, recv2)
        _, _, _, chips = _place()

        def load_block(src):
            cp = pltpu.make_async_copy(src, w_blk, lsem)
            cp.start()
            cp.wait()

        @pl.when((j == 0) & (i == 0))
        def _():
            g1.begin(relations=(0, 1))
            load_block(s_hbm)

        for r, (cx, cy) in enumerate(chips):
            @pl.when((j == r + 1) & (i == 0))
            def _(r=r, cx=cx, cy=cy):
                g1.wait_relayed(r)
                load_block(_block_of(f_hbm, first, 2 * cx + cy))

        @pl.when((j == 2) & (i == 0))
        def _():
            g1.relay(relations=(2,))
            g2.begin()

        pl.when((j == last) & (i == ni // 2))(g2.relay)

        rows = pl.ds(pl.multiple_of(i * tm, tm), tm)

        @pl.when(j == 0)
        def _():
            h = _rms_rows(x_ref, nw_ref)
            hs[rows, :] = h.astype(BF16)
            ht_ref[...] = h.T.astype(BF16)

        o_ref[...] = jnp.dot(hs[rows, :], w_blk[...], preferred_element_type=F32)

        @pl.when((j == 0) & (i == ni - 1))
        def _():
            g1.relay(relations=(0, 1))
            g1.begin(relations=(2,), sibling=False)

        @pl.when((j == last) & (i == ni - 1))
        def _():
            g1.end_rest()
            g2.end()

    def block_of_step(j, b_ref):
        return jnp.bitwise_xor(b_ref[0], jnp.bitwise_or(jnp.left_shift(jnp.bitwise_and(j, 1), 1), jnp.right_shift(j, 1)))

    outs = pl.pallas_call(
        body, name=name,
        grid_spec=pltpu.PrefetchScalarGridSpec(
            num_scalar_prefetch=1, grid=(N_CHIPS, ni),
            in_specs=[pl.BlockSpec((tm, D_MODEL), lambda j, i, b: (jnp.where(j == 0, i, 0), 0)),
                      pl.BlockSpec((1, D_MODEL), lambda j, i, b: (0, 0)), ANY] + [ANY] * nl,
            out_specs=[pl.BlockSpec((tm, cw), lambda j, i, b: (i, block_of_step(j, b))),
                       pl.BlockSpec((D_MODEL, tm), lambda j, i, b: (0, jnp.where(j == 0, i, ni - 1))),
                       ANY] + [ANY] * nl,
            scratch_shapes=[pltpu.VMEM((S, D_MODEL), BF16), pltpu.VMEM((K, cw), BF16), pltpu.SemaphoreType.DMA(())] +
            _gather_sems((first,)) + _gather_sems(late_names)),
        out_shape=[jax.ShapeDtypeStruct((S, cw * N_CHIPS), F32), jax.ShapeDtypeStruct((D_MODEL, S), BF16),
                   _full_shape(first, shard)] + [_full_shape(n, s) for n, s in zip(late_names, late_shards)],
        compiler_params=pltpu.CompilerParams(dimension_semantics=("arbitrary", "arbitrary"),
                                             vmem_limit_bytes=VMEM_LIMIT),
    )(bidx, x, nw, shard, *late_shards)
    return outs[0], outs[1], outs[2], list(outs[3:])


def _piece_blocks(pieces, tk, axis):
    starts, counts, s = [], [], 0
    for p in pieces:
        n = p.shape[axis] // tk
        starts.append(s)
        counts.append(n)
        s += n
    return starts, counts, s


def _mm_nt_rms(pieces, w, x, nw, dres, tm, name, scatter_names=(), scatter_halves=()):
    S = x.shape[0]
    npc = len(pieces)
    ni = S // tm
    ns = len(scatter_names)
    offs = np.cumsum([0] + [p.shape[1] for p in pieces]).tolist()

    def body(*refs):
        p_refs = refs[:npc]
        w_hbm, x_ref, nw_ref, dr_ref = refs[npc:npc + 4]
        h_refs = refs[npc + 4:npc + 4 + ns]
        dx_ref, dnw_ref = refs[npc + 4 + ns:npc + 6 + ns]
        r_refs = refs[npc + 6 + ns:npc + 6 + 2 * ns]
        w_v, sem, nacc = refs[npc + 6 + 2 * ns:npc + 9 + 2 * ns]
        i = pl.program_id(0)
        if ns:
            send, recv = refs[npc + 9 + 2 * ns:]

            @pl.when(i == 0)
            def _():
                for cp in _scatter_copies(scatter_names, h_refs, r_refs, send, recv):
                    cp.start()
        _load_once(w_hbm, w_v, sem)

        @pl.when(i == 0)
        def _():
            nacc[...] = jnp.zeros_like(nacc)

        dh = None
        for p in range(npc):
            part = lax.dot_general(p_refs[p][...].astype(BF16), w_v[:, offs[p]:offs[p + 1]], NT_DIMS,
                                   preferred_element_type=F32)
            dh = part if dh is None else dh + part
        xv = x_ref[...]
        rs = lax.rsqrt(jnp.mean(xv * xv, axis=-1, keepdims=True) + EPS)
        xh = xv * rs
        nacc[...] += _cs8(dh * xh)
        dxh = dh * nw_ref[...]
        dx_ref[...] = dr_ref[...] + rs * (dxh - xh * jnp.mean(dxh * xh, axis=-1, keepdims=True))

        @pl.when(i == ni - 1)
        def _():
            dnw_ref[...] = jnp.sum(nacc[...], axis=0, keepdims=True)
            if ns:
                for cp in _scatter_copies(scatter_names, h_refs, r_refs, send, recv):
                    cp.wait()

    row = pl.BlockSpec((tm, D_MODEL), lambda i: (i, 0))
    outs = _call(
        body, name, (ni,),
        [pl.BlockSpec((tm, p.shape[1]), lambda i: (i, 0)) for p in pieces] +
        [ANY, row, pl.BlockSpec((1, D_MODEL), lambda i: (0, 0)), row] + [ANY] * ns,
        [row, pl.BlockSpec((1, D_MODEL), lambda i: (0, 0))] + [ANY] * ns,
        [jax.ShapeDtypeStruct((S, D_MODEL), F32), jax.ShapeDtypeStruct((1, D_MODEL), F32)] +
        [jax.ShapeDtypeStruct((3,) + _shard_shape(h.shape, n), h.dtype) for n, h in zip(scatter_names, scatter_halves)],
        scratch=[pltpu.VMEM(w.shape, BF16), pltpu.SemaphoreType.DMA(()), pltpu.VMEM((SUBLANES, D_MODEL), F32)] +
        (_scatter_sems(scatter_names) if ns else []),
        sem=("arbitrary",))(*pieces, w, x, nw, dres, *scatter_halves)
    return outs[0], outs[1], list(outs[2:])


WG_CHUNKS = 4


def _mm_wgrad(at, pieces, tn, name):
    M, S = at.shape
    starts, counts, nj = _piece_blocks(pieces, tn, 1)
    npc = len(pieces)

    def body(*refs):
        a_hbm = refs[0]
        p_refs = refs[1:1 + npc]
        o_ref, o16_ref, a_v, sems = refs[1 + npc:]
        j = pl.program_id(0)
        kc = S // WG_CHUNKS

        def chunk(k):
            cols = pl.ds(k * kc, kc)
            return pltpu.make_async_copy(a_hbm.at[:, cols], a_v.at[:, cols], sems.at[k])

        @pl.when(j == 0)
        def _():
            for k in range(WG_CHUNKS):
                chunk(k).start()

        for p in range(npc):
            @pl.when((j >= starts[p]) & (j < starts[p] + counts[p]))
            def _(p=p):
                acc = None
                for k in range(WG_CHUNKS):
                    pl.when(j == 0)(chunk(k).wait)
                    part = jnp.dot(a_v[:, k * kc:(k + 1) * kc], p_refs[p][k * kc:(k + 1) * kc, :].astype(BF16),
                                   preferred_element_type=F32)
                    acc = part if acc is None else acc + part
                o_ref[...] = acc
                o16_ref[...] = acc.astype(BF16)

    def pspec(p):
        return pl.BlockSpec((S, tn), lambda j: (0, jnp.clip(j - starts[p], 0, counts[p] - 1)))

    col = pl.BlockSpec((M, tn), lambda j: (0, j))
    return _call(
        body, name, (nj,),
        [ANY] + [pspec(p) for p in range(npc)], [col, col],
        [jax.ShapeDtypeStruct((M, nj * tn), F32), jax.ShapeDtypeStruct((M, nj * tn), BF16)],
        scratch=[pltpu.VMEM(at.shape, BF16), pltpu.SemaphoreType.DMA((WG_CHUNKS,))], sem=("arbitrary",))(at, *pieces)


def _stream_spec(d, T):
    return pl.BlockSpec((d, T // d, A_WIDTH), lambda i: (0, i, 0))


def _stream_shape(d, S, dtype):
    return jax.ShapeDtypeStruct((d, S // d, A_WIDTH), dtype)


N_CHUNK = A_WIDTH // LANES


def _to_tokens(ref, scr, d, T):
    if d == 1:
        return ref[0].astype(F32)
    for r in range(d):
        for ch in range(N_CHUNK):
            scr.at[ch][pl.ds(r, T // d, stride=d), :] = ref[r, :, ch * LANES:(ch + 1) * LANES].astype(F32)
    return _get(scr)


def _from_tokens(out_ref, scr, d, T):
    for r in range(d):
        for ch in range(N_CHUNK):
            out_ref[r, :, ch * LANES:(ch + 1) * LANES] = scr.at[ch][pl.ds(r, T // d, stride=d), :].astype(out_ref.dtype)


def _put(scr, val):
    for ch in range(N_CHUNK):
        scr[ch] = val[:, ch * LANES:(ch + 1) * LANES]


def _get(scr):
    return jnp.concatenate([scr[ch] for ch in range(N_CHUNK)], axis=1)


def _chunked(T):
    return pltpu.VMEM((N_CHUNK, T, LANES), F32)


def _compact_spec(d, T):
    return pl.BlockSpec((d, T // d, LANES), lambda i: (0, i, 0))


def _compact_shape(d, S):
    return jax.ShapeDtypeStruct((d, S // d, LANES), F32)


def _compact_to_tokens(ref, scr, d, T):
    if d == 1:
        return ref[0]
    for r in range(d):
        scr[pl.ds(r, T // d, stride=d), :] = ref[r]
    return scr[...]


def _compact_from_tokens(out_ref, scr, val, d, T):
    if d == 1:
        out_ref[0] = val
        return
    scr[...] = val
    for r in range(d):
        out_ref[r] = scr[pl.ds(r, T // d, stride=d), :]


def _head_expander():
    r = lax.broadcasted_iota(jnp.int32, (LANES, A_WIDTH), 0)
    c = lax.broadcasted_iota(jnp.int32, (LANES, A_WIDTH), 1) // HEAD_DIM
    return (r == c).astype(BF16)


def _head_reducer():
    r = lax.broadcasted_iota(jnp.int32, (A_WIDTH, LANES), 0) // HEAD_DIM
    c = lax.broadcasted_iota(jnp.int32, (A_WIDTH, LANES), 1)
    return (r == c).astype(BF16)


def _qkv_prep(proj, pos, freq, wq, wk, T):
    S = proj.shape[0]
    qk_w = 3 * A_WIDTH

    def body(q_ref, k_ref, v_ref, pos_ref, f_ref, wq_ref, wk_ref, *rest):
        outs, tabs, scr = rest[:9], rest[9:12], rest[12]
        seg_mean = _seg_mean()
        c, s1, s2 = _rope_tables(pos_ref, f_ref)
        for tab, val in zip(tabs, (c, s1, s2)):
            tab[...] = val
        for t, (src, w_ref) in enumerate(((q_ref, wq_ref), (k_ref, wk_ref), (v_ref, None))):
            for g in range(3):
                d = A_GROUPS[g][1]
                out = outs[3 * t + g]
                for ch in range(A_WIDTH // LANES):
                    cs = slice(ch * LANES, (ch + 1) * LANES)
                    v = src[:, g * A_WIDTH + ch * LANES: g * A_WIDTH + (ch + 1) * LANES]
                    if w_ref is not None:
                        y = v * _head_rstd(v, seg_mean) * w_ref[...]
                        v = y * c + pltpu.roll(y, 8, 1) * s1 + pltpu.roll(y, LANES - 8, 1) * s2
                    if d == 1:
                        out[0, :, cs] = v.astype(BF16)
                    else:
                        scr[ch] = v
                if d > 1:
                    _from_tokens(out, scr, d, T)

    ds_ = [A_GROUPS[g][1] for g in range(3)] * 3
    return _call(
        body, "qkv_prep", (S // T,),
        [pl.BlockSpec((T, qk_w), lambda i: (i, 0)), pl.BlockSpec((T, qk_w), lambda i: (i, 1)),
         pl.BlockSpec((T, qk_w), lambda i: (i, 2)),
         pl.BlockSpec((T, 1), lambda i: (i, 0)), pl.BlockSpec((1, LANES), lambda i: (0, 0)),
         pl.BlockSpec((1, LANES), lambda i: (0, 0)), pl.BlockSpec((1, LANES), lambda i: (0, 0))],
        [_stream_spec(d, T) for d in ds_] + [pl.BlockSpec((T, LANES), lambda i: (i, 0))] * 3,
        [_stream_shape(d, S, BF16) for d in ds_] + [jax.ShapeDtypeStruct((S, LANES), F32)] * 3,
        scratch=[_chunked(T)], sem=("parallel",))(proj, proj, proj, pos, freq, wq, wk)


def _attn_mask(i):
    qi = lax.broadcasted_iota(jnp.int32, (QBLK, 2 * QBLK), 0) + QBLK
    kj = lax.broadcasted_iota(jnp.int32, (QBLK, 2 * QBLK), 1)
    dist = qi - kj
    return (dist >= 0) & (dist <= QBLK) & ((i > 0) | (kj >= QBLK))


ATT_BLK = (None, QBLK, A_WIDTH)
ATT_CBLK = (None, QBLK, LANES)


def _first_head_lanes():
    return lax.broadcasted_iota(jnp.int32, (1, LANES), 1) < HEAD_DIM


def _split_heads(v, first):
    zero = jnp.zeros_like(v)
    return jnp.where(first, v, zero), jnp.where(first, zero, v)


def _attn_fwd(q, k, v, g):
    d, n, _ = q.shape
    nb = n // QBLK

    def body(q_ref, kp_ref, kc_ref, vp_ref, vc_ref, o_ref, l_ref, s_scr, p_scr):
        i = pl.program_id(1)
        mask = _attn_mask(i)
        first = _first_head_lanes()
        for pr in range(A_HEADS // 2):
            ps = slice(pr * LANES, (pr + 1) * LANES)
            kc = jnp.concatenate([kp_ref[:, ps], kc_ref[:, ps]], axis=0)
            for e, qh in enumerate(_split_heads(q_ref[:, ps], first)):
                s_scr[2 * pr + e] = lax.dot_general(qh, kc, NT_DIMS, preferred_element_type=F32)
        lane = lax.broadcasted_iota(jnp.int32, (1, LANES), 1)
        lrow = jnp.zeros((QBLK, LANES), F32)
        for h in range(A_HEADS):
            s = jnp.where(mask, s_scr[h] * (HEAD_DIM ** -0.5), NEG)
            m = jnp.max(s, axis=-1, keepdims=True)
            p = jnp.exp(s - m)
            den = jnp.sum(p, axis=-1, keepdims=True)
            p_scr[h] = (p / den).astype(BF16)
            lrow = jnp.where(lane == h, m + jnp.log(den), lrow)
        l_ref[...] = lrow
        for pr in range(A_HEADS // 2):
            ps = slice(pr * LANES, (pr + 1) * LANES)
            va, vb = _split_heads(jnp.concatenate([vp_ref[:, ps], vc_ref[:, ps]], axis=0), first)
            o_ref[:, ps] = (jnp.dot(p_scr[2 * pr], va, preferred_element_type=F32) +
                            jnp.dot(p_scr[2 * pr + 1], vb, preferred_element_type=F32)).astype(BF16)

    prev = lambda r, i: (r, jnp.maximum(i - 1, 0), 0)
    cur = lambda r, i: (r, i, 0)
    return _call(
        body, "attn_fwd_g%d" % g, (d, nb),
        [pl.BlockSpec(ATT_BLK, cur), pl.BlockSpec(ATT_BLK, prev), pl.BlockSpec(ATT_BLK, cur),
         pl.BlockSpec(ATT_BLK, prev), pl.BlockSpec(ATT_BLK, cur)],
        [pl.BlockSpec(ATT_BLK, cur), pl.BlockSpec(ATT_CBLK, cur)],
        [jax.ShapeDtypeStruct((d, n, A_WIDTH), BF16), jax.ShapeDtypeStruct((d, n, LANES), F32)],
        scratch=[pltpu.VMEM((A_HEADS, QBLK, 2 * QBLK), F32), pltpu.VMEM((A_HEADS, QBLK, 2 * QBLK), BF16)],
        sem=("parallel", "parallel"))(q, k, k, v, v)


def _attn_bwd(q, k, v, do, lse, cg, g):
    d, n, _ = q.shape
    nb = n // QBLK
    scale = HEAD_DIM ** -0.5

    def body(q_ref, kp_ref, kc_ref, vp_ref, vc_ref, do_ref, l_ref, c_ref, dq_ref, dk_ref, dv_ref, ck, cv,
             s_scr, dp_scr, p_scr, ds_scr):
        i = pl.program_id(1)

        @pl.when(i == 0)
        def _():
            ck[...] = jnp.zeros_like(ck)
            cv[...] = jnp.zeros_like(cv)

        @pl.when(i < nb)
        def _():
            mask = _attn_mask(i)
            first = _first_head_lanes()
            for pr in range(A_HEADS // 2):
                ps = slice(pr * LANES, (pr + 1) * LANES)
                kc = jnp.concatenate([kp_ref[:, ps], kc_ref[:, ps]], axis=0)
                vc = jnp.concatenate([vp_ref[:, ps], vc_ref[:, ps]], axis=0)
                qs = _split_heads(q_ref[:, ps], first)
                dos = _split_heads(do_ref[:, ps], first)
                for e in range(2):
                    s_scr[2 * pr + e] = lax.dot_general(qs[e], kc, NT_DIMS, preferred_element_type=F32)
                    dp_scr[2 * pr + e] = lax.dot_general(dos[e], vc, NT_DIMS, preferred_element_type=F32)
            for h in range(A_HEADS):
                p = jnp.where(mask, jnp.exp(s_scr[h] * scale - l_ref[:, h:h + 1]), 0.0)
                p_scr[h] = p.astype(BF16)
                ds_scr[h] = (p * (dp_scr[h] + c_ref[:, h:h + 1]) * scale).astype(BF16)
            for pr in range(A_HEADS // 2):
                ps = slice(pr * LANES, (pr + 1) * LANES)
                ks = _split_heads(jnp.concatenate([kp_ref[:, ps], kc_ref[:, ps]], axis=0), first)
                qs = _split_heads(q_ref[:, ps], first)
                dos = _split_heads(do_ref[:, ps], first)
                dq = dkc = dvc = None
                for e in range(2):
                    ds = ds_scr[2 * pr + e]
                    a = jnp.dot(ds, ks[e], preferred_element_type=F32)
                    b = lax.dot_general(ds, qs[e], TN_DIMS, preferred_element_type=F32)
                    c = lax.dot_general(p_scr[2 * pr + e], dos[e], TN_DIMS, preferred_element_type=F32)
                    dq, dkc, dvc = (a, b, c) if e == 0 else (dq + a, dkc + b, dvc + c)
                dq_ref[:, ps] = dq.astype(BF16)
                dk_ref[:, ps] = (ck[:, ps] + dkc[:QBLK]).astype(BF16)
                dv_ref[:, ps] = (cv[:, ps] + dvc[:QBLK]).astype(BF16)
                ck[:, ps] = dkc[QBLK:]
                cv[:, ps] = dvc[QBLK:]

        @pl.when(i == nb)
        def _():
            dk_ref[...] = ck[...].astype(BF16)
            dv_ref[...] = cv[...].astype(BF16)

    qi = lambda i: jnp.minimum(i, nb - 1)
    cur = lambda r, i: (r, qi(i), 0)
    prev = lambda r, i: (r, jnp.maximum(qi(i) - 1, 0), 0)
    late = lambda r, i: (r, jnp.maximum(i - 1, 0), 0)
    return _call(
        body, "attn_bwd_g%d" % g, (d, nb + 1),
        [pl.BlockSpec(ATT_BLK, cur), pl.BlockSpec(ATT_BLK, prev), pl.BlockSpec(ATT_BLK, cur),
         pl.BlockSpec(ATT_BLK, prev), pl.BlockSpec(ATT_BLK, cur),
         pl.BlockSpec(ATT_BLK, cur), pl.BlockSpec(ATT_CBLK, cur), pl.BlockSpec(ATT_CBLK, cur)],
        [pl.BlockSpec(ATT_BLK, cur), pl.BlockSpec(ATT_BLK, late), pl.BlockSpec(ATT_BLK, late)],
        [jax.ShapeDtypeStruct((d, n, A_WIDTH), BF16)] * 3,
        scratch=[pltpu.VMEM((QBLK, A_WIDTH), F32), pltpu.VMEM((QBLK, A_WIDTH), F32),
                 pltpu.VMEM((A_HEADS, QBLK, 2 * QBLK), F32), pltpu.VMEM((A_HEADS, QBLK, 2 * QBLK), F32),
                 pltpu.VMEM((A_HEADS, QBLK, 2 * QBLK), BF16), pltpu.VMEM((A_HEADS, QBLK, 2 * QBLK), BF16)],
        sem=("parallel", "arbitrary"))(q, k, k, v, v, do, lse, cg)


def _merge_weights(l0, l1, l2):
    mx = jnp.maximum(jnp.maximum(l0, l1), l2)
    e0, e1, e2 = jnp.exp(l0 - mx), jnp.exp(l1 - mx), jnp.exp(l2 - mx)
    den = e0 + e1 + e2
    return e0 / den, e1 / den, e2 / den


def _even_specs(T, S):
    t8 = T // SUBLANES
    last8 = S // SUBLANES - 1
    col = lambda c: pl.BlockSpec((T, A_WIDTH), lambda i: (i, c))
    prev8 = lambda c: pl.BlockSpec((SUBLANES, A_WIDTH), lambda i: (jnp.maximum(i * t8 - 1, 0), c))
    next8 = lambda c: pl.BlockSpec((SUBLANES, A_WIDTH), lambda i: (jnp.minimum((i + 1) * t8, last8), c))
    return col, prev8, next8


GROUP_D = tuple(d for _, d in A_GROUPS)


def _even_mixer_fwd(x, proj, os_, ls_, conv_w, w_out, T):
    S = proj.shape[0]
    col, prev8, _ = _even_specs(T, S)
    H = SUBLANES

    def body(x_ref, w_ref, bg_r, cg_r, hb_r, zl_r, zh_r, cgp_r, hbp_r, o0, o1, o2, l0, l1, l2, cw_r,
             x1_ref, ut_ref, ext, cscr, *scr):
        i = pl.program_id(0)
        ls = [_compact_to_tokens(r, cscr, GROUP_D[g], T) for g, r in enumerate((l0, l1, l2))]
        expand = _head_expander()
        ws = [_segsum(w, expand) for w in _merge_weights(*ls)]
        oa = ws[0] * _to_tokens(o0, scr[0], GROUP_D[0], T)
        oa = oa + ws[1] * _to_tokens(o1, scr[1], GROUP_D[1], T)
        oa = oa + ws[2] * _to_tokens(o2, scr[2], GROUP_D[2], T)
        ext[0:H, :] = jnp.where(i == 0, 0.0, cgp_r[...] * hbp_r[...])
        ext[H:H + T, :] = cg_r[...] * hb_r[...]
        conv = cw_r[0:1, :] * ext[H - 2:H - 2 + T, :]
        for kk in range(1, SC_WIDTH):
            conv = conv + cw_r[kk:kk + 1, :] * ext[H - 2 + kk:H - 2 + kk + T, :]
        zl, zh = zl_r[...], zh_r[...]
        x1_ref[...] = x_ref[...] + _out_projection(ut_ref, w_ref, oa * (zl * _sig(zl)),
                                                   bg_r[...] * conv * (zh * _sig(zh)))

    streams = [_stream_spec(d, T) for d in GROUP_D]
    compacts = [_compact_spec(d, T) for d in GROUP_D]
    row = pl.BlockSpec((T, D_MODEL), lambda i: (i, 0))
    return _call(
        body, "even_mixer_fwd", (S // T,),
        [row, pl.BlockSpec((D_MODEL, D_MODEL), lambda i: (0, 0)),
         col(9), col(10), col(11), col(12), col(13), prev8(10), prev8(11)] + streams + compacts +
        [pl.BlockSpec((SC_WIDTH, A_WIDTH), lambda i: (0, 0))],
        [row, pl.BlockSpec((D_MODEL, T), lambda i: (0, i))],
        [jax.ShapeDtypeStruct((S, D_MODEL), F32), jax.ShapeDtypeStruct((D_MODEL, S), BF16)],
        scratch=[pltpu.VMEM((T + H, A_WIDTH), F32), pltpu.VMEM((T, LANES), F32)] + [_chunked(T)] * 3,
        sem=("parallel",))(
            x, w_out, proj, proj, proj, proj, proj, proj, proj, *os_, *ls_, conv_w)


def _even_mixer_bwd(dy, w_out, proj, os_, ls_, conv_w, T):
    S = proj.shape[0]
    nt = S // T
    col, prev8, next8 = _even_specs(T, S)
    H = SUBLANES
    t8 = T // SUBLANES
    last8 = S // SUBLANES - 1

    def body(dy_r, dyn_r, w_ref, bg_r, cg_r, hb_r, zl_r, zh_r, cgp_r, hbp_r, zhn_r, bgn_r,
             o0, o1, o2, l0, l1, l2, cw_r,
             do0, do1, do2, c0, c1, c2, dr_ref, dcw_ref, ext_t, ext_d, acc, cscr, s_a, s_b, s_c):
        i = pl.program_id(0)

        @pl.when(i == 0)
        def _():
            acc[...] = jnp.zeros_like(acc)

        zl, zh = zl_r[...], zh_r[...]
        sl, sh = _sig(zl), _sig(zh)
        du = lax.dot_general(dy_r[...].astype(BF16), w_ref[...], NT_DIMS, preferred_element_type=F32)
        dul, duh = du[:, 0:A_WIDTH], du[:, A_WIDTH:]
        dun = lax.dot_general(dyn_r[...].astype(BF16), w_ref[A_WIDTH:, :], NT_DIMS, preferred_element_type=F32)
        scr = (s_a, s_b, s_c)
        ls = [_compact_to_tokens(r, cscr, GROUP_D[g], T) for g, r in enumerate((l0, l1, l2))]
        wcs = _merge_weights(*ls)
        expand = _head_expander()
        ws = [_segsum(w, expand) for w in wcs]
        oa = ws[0] * _to_tokens(o0, scr[0], GROUP_D[0], T)
        oa = oa + ws[1] * _to_tokens(o1, scr[1], GROUP_D[1], T)
        oa = oa + ws[2] * _to_tokens(o2, scr[2], GROUP_D[2], T)
        doa = dul * (zl * sl)
        rsum = _segsum(doa * oa, _head_reducer())
        for g, (do_ref, c_ref) in enumerate(((do0, c0), (do1, c1), (do2, c2))):
            d = GROUP_D[g]
            _compact_from_tokens(c_ref, cscr, -wcs[g] * rsum, d, T)
            if d == 1:
                do_ref[0] = (ws[g] * doa).astype(BF16)
            else:
                _put(s_c, ws[g] * doa)
                _from_tokens(do_ref, s_c, d, T)
        cgv, hbv, bgv = cg_r[...], hb_r[...], bg_r[...]
        ext_t[0:H, :] = jnp.where(i == 0, 0.0, cgp_r[...] * hbp_r[...])
        ext_t[H:H + T, :] = cgv * hbv
        conv = cw_r[0:1, :] * ext_t[H - 2:H - 2 + T, :]
        for kk in range(1, SC_WIDTH):
            conv = conv + cw_r[kk:kk + 1, :] * ext_t[H - 2 + kk:H - 2 + kk + T, :]
        dyb = duh * (zh * sh)
        dconv = dyb * bgv
        zn = zhn_r[...]
        ext_d[0:T, :] = dconv
        ext_d[T:T + H, :] = jnp.where(i == nt - 1, 0.0, dun * (zn * _sig(zn)) * bgn_r[...])
        dt = cw_r[0:1, :] * ext_d[2:2 + T, :]
        for kk in range(1, SC_WIDTH):
            dt = dt + cw_r[kk:kk + 1, :] * ext_d[2 - kk:2 - kk + T, :]
        for kk in range(SC_WIDTH):
            acc[kk * SUBLANES:(kk + 1) * SUBLANES, :] += _cs8(dconv * ext_t[H - 2 + kk:H - 2 + kk + T, :])
        dr_ref[:, 0:A_WIDTH] = (dyb * conv).astype(BF16)
        dr_ref[:, A_WIDTH:2 * A_WIDTH] = (dt * hbv).astype(BF16)
        dr_ref[:, 2 * A_WIDTH:3 * A_WIDTH] = (dt * cgv).astype(BF16)
        dr_ref[:, 3 * A_WIDTH:4 * A_WIDTH] = (dul * oa * _dsilu(zl, sl)).astype(BF16)
        dr_ref[:, 4 * A_WIDTH:5 * A_WIDTH] = (duh * (bgv * conv) * _dsilu(zh, sh)).astype(BF16)

        @pl.when(i == nt - 1)
        def _():
            for kk in range(SC_WIDTH):
                dcw_ref[kk:kk + 1, :] = jnp.sum(acc[kk * SUBLANES:(kk + 1) * SUBLANES, :], axis=0, keepdims=True)

    streams = [_stream_spec(d, T) for d in GROUP_D]
    dynext = pl.BlockSpec((SUBLANES, D_MODEL), lambda i: (jnp.minimum((i + 1) * t8, last8), 0))
    compacts = [_compact_spec(d, T) for d in GROUP_D]
    outs = _call(
        body, "even_mixer_bwd", (nt,),
        [pl.BlockSpec((T, D_MODEL), lambda i: (i, 0)), dynext, pl.BlockSpec((D_MODEL, D_MODEL), lambda i: (0, 0)),
         col(9), col(10), col(11), col(12), col(13), prev8(10), prev8(11), next8(13), next8(9)] +
        streams + compacts + [pl.BlockSpec((SC_WIDTH, A_WIDTH), lambda i: (0, 0))],
        streams + compacts + [pl.BlockSpec((T, 5 * A_WIDTH), lambda i: (i, 0)),
                              pl.BlockSpec((SC_WIDTH, A_WIDTH), lambda i: (0, 0))],
        [_stream_shape(d, S, BF16) for d in GROUP_D] + [_compact_shape(d, S) for d in GROUP_D] +
        [jax.ShapeDtypeStruct((S, 5 * A_WIDTH), BF16), jax.ShapeDtypeStruct((SC_WIDTH, A_WIDTH), F32)],
        scratch=[pltpu.VMEM((T + H, A_WIDTH), F32), pltpu.VMEM((T + H, A_WIDTH), F32),
                 pltpu.VMEM((SC_WIDTH * SUBLANES, A_WIDTH), F32), pltpu.VMEM((T, LANES), F32)] +
                [_chunked(T)] * 3,
        sem=("arbitrary",))(dy, dy, w_out, proj, proj, proj, proj, proj, proj, proj, proj, proj, *os_, *ls_, conv_w)
    return outs[0:3], outs[3:6], outs[6], outs[7]


def _qk_bwd(proj, dqs, dks, dvs, rope, wq, wk, T):
    S = proj.shape[0]
    nt = S // T
    qk_w = 3 * A_WIDTH

    def body(q_ref, k_ref, dq0, dq1, dq2, dk0, dk1, dk2, dv0, dv1, dv2, c_ref, s1_ref, s2_ref, wq_ref, wk_ref,
             o_ref, dw_ref, acc, scr):
        i = pl.program_id(0)

        @pl.when(i == 0)
        def _():
            acc[...] = jnp.zeros_like(acc)
            dw_ref[...] = jnp.zeros_like(dw_ref)

        seg_mean = _seg_mean()
        c, s1, s2 = c_ref[...], s1_ref[...], s2_ref[...]
        for t, (src, w_ref, ds) in enumerate(((q_ref, wq_ref, (dq0, dq1, dq2)), (k_ref, wk_ref, (dk0, dk1, dk2)))):
            wv = w_ref[...]
            for g in range(3):
                d = GROUP_D[g]
                if d > 1:
                    _to_tokens(ds[g], scr, d, T)
                for ch in range(A_WIDTH // LANES):
                    cs = slice(g * A_WIDTH + ch * LANES, g * A_WIDTH + (ch + 1) * LANES)
                    lc = slice(ch * LANES, (ch + 1) * LANES)
                    v = src[:, cs]
                    dout = ds[g][0, :, lc].astype(F32) if d == 1 else scr[ch]
                    rs = lax.rsqrt(_segsum(v * v, seg_mean) + EPS)
                    xh = v * rs
                    dy = dout * c + pltpu.roll(dout * s1, LANES - 8, 1) + pltpu.roll(dout * s2, 8, 1)
                    acc[t * SUBLANES:(t + 1) * SUBLANES, :] += _cs8(dy * xh)
                    dxh = dy * wv
                    mean = _segsum(dxh * xh, seg_mean)
                    o_ref[:, t * qk_w + g * A_WIDTH + ch * LANES: t * qk_w + g * A_WIDTH + (ch + 1) * LANES] = (
                        rs * (dxh - xh * mean)).astype(BF16)
        for g, dv in enumerate((dv0, dv1, dv2)):
            d = GROUP_D[g]
            base = 2 * qk_w + g * A_WIDTH
            o_ref[:, base:base + A_WIDTH] = _to_tokens(dv, scr, d, T).astype(BF16)

        @pl.when(i == nt - 1)
        def _():
            for t in range(2):
                srow = jnp.sum(acc[t * SUBLANES:(t + 1) * SUBLANES, :], axis=0, keepdims=True)
                dw_ref[t:t + 1, :] = srow + pltpu.roll(srow, HEAD_DIM, 1)

    streams = [_stream_spec(d, T) for d in GROUP_D]
    return _call(
        body, "qk_bwd", (nt,),
        [pl.BlockSpec((T, qk_w), lambda i: (i, 0)), pl.BlockSpec((T, qk_w), lambda i: (i, 1))] + streams * 3 +
        [pl.BlockSpec((T, LANES), lambda i: (i, 0))] * 3 +
        [pl.BlockSpec((1, LANES), lambda i: (0, 0)), pl.BlockSpec((1, LANES), lambda i: (0, 0))],
        [pl.BlockSpec((T, 3 * qk_w), lambda i: (i, 0)), pl.BlockSpec((SUBLANES, LANES), lambda i: (0, 0))],
        [jax.ShapeDtypeStruct((S, 3 * qk_w), BF16), jax.ShapeDtypeStruct((SUBLANES, LANES), F32)],
        scratch=[pltpu.VMEM((2 * SUBLANES, LANES), F32), _chunked(T)], sem=("arbitrary",))(
            proj, proj, *dqs, *dks, *dvs, *rope, wq, wk)


N_SMALL_ODD = 40
SHIFT_ROWS_LESS = SUBLANES


def _fill_shifted(ext_ref, sh_ref):
    rows = ext_ref.shape[0] - SHIFT_ROWS_LESS
    for b in range(1, SUBLANES):
        sh_ref[b - 1] = ext_ref[b:b + rows, :]


def _window(ext_ref, sh_ref, off, T):
    a, b = divmod(off, SUBLANES)
    if b == 0:
        return ext_ref[off:off + T, :]
    return sh_ref[b - 1, a * SUBLANES:a * SUBLANES + T, :]


def _odd_pool_tile(i, uc_r, ucp_r, pw_r, ext_u, pooled_s, pm_s, T):
    H = HALO
    uc = uc_r[...]
    ext_u[0:H, :] = jnp.where(i == 0, 0.0, ucp_r[...])
    ext_u[H:H + T, :] = uc
    row = i * T + lax.broadcasted_iota(jnp.int32, (T, 1), 0)
    for g, p in enumerate(POOL_SIZES):
        cs = slice(g * LANES, (g + 1) * LANES)
        win = ext_u[H:H + T, cs]
        for j in range(1, p):
            win = win + ext_u[H - j:H - j + T, cs]
        cnt = jnp.minimum(row + 1, p).astype(F32)
        pooled = win / cnt - uc[:, cs]
        pooled_s[:, cs] = pooled
        pm_s[:, cs] = jnp.dot(pooled.astype(BF16), pw_r[g].astype(BF16), preferred_element_type=F32)
    return row


def _odd_glu_tile(i, da_r, dg_r, dap_r, dgp_r, ext_g, sh_g, T):
    H = HALO
    ext_g[0:H, :] = jnp.where(i == 0, 0.0, dap_r[...] * _sig(dgp_r[...]))
    ext_g[H:H + T, :] = da_r[...] * _sig(dg_r[...])
    _fill_shifted(ext_g, sh_g)


def _odd_specs(T, S, order):
    tb = T // HALO
    col = lambda c: pl.BlockSpec((T, A_WIDTH), lambda s: (order(s), c))
    prev = lambda c: pl.BlockSpec((HALO, A_WIDTH), lambda s: (jnp.maximum(order(s) * tb - 1, 0), c))
    const2 = lambda shape: pl.BlockSpec(shape, lambda s: (0, 0))
    weights = [pl.BlockSpec((4, LANES, LANES), lambda s: (0, 0, 0)), const2((1, A_WIDTH)),
               const2((D_CONV, A_WIDTH)), const2((1, A_WIDTH)), const2((1, A_WIDTH)), const2((1, A_WIDTH))]
    return col, prev, weights


def _odd_mixer_fwd(x, tgt, proj, pool_w, scale, dconv_w, dconv_b, ln_w, ln_b, w_out, T):
    S = proj.shape[0]
    nt = S // T
    col, prev, wspecs = _odd_specs(T, S, lambda s: s)
    H = HALO

    def body(x_ref, t_ref, w_ref, uc_r, da_r, dg_r, zl_r, zh_r, ucp_r, dap_r, dgp_r, pw_r, sc_r, dw_r, db_r,
             lw_r, lb_r, dy_ref, l_ref, ut_ref, cv_ref, ext_u, ext_g, sh_g, pooled_s, pm_s, lacc):
        i = pl.program_id(0)

        @pl.when(i == 0)
        def _():
            lacc[...] = jnp.zeros_like(lacc)

        _odd_pool_tile(i, uc_r, ucp_r, pw_r, ext_u, pooled_s, pm_s, T)
        _odd_glu_tile(i, da_r, dg_r, dap_r, dgp_r, ext_g, sh_g, T)
        base = H - (D_CONV - 1)
        conv = db_r[...] + dw_r[0:1, :] * _window(ext_g, sh_g, base, T)
        for kk in range(1, D_CONV):
            conv = conv + dw_r[kk:kk + 1, :] * _window(ext_g, sh_g, base + kk, T)
        cv_ref[...] = conv
        mu = jnp.mean(conv, axis=-1, keepdims=True)
        xc = conv - mu
        yh = xc * lax.rsqrt(jnp.mean(xc * xc, axis=-1, keepdims=True) + EPS)
        ln = yh * lw_r[...] + lb_r[...]
        zl, zh = zl_r[...], zh_r[...]
        y = x_ref[...] + _out_projection(ut_ref, w_ref, pm_s[...] * sc_r[...] * (zl * _sig(zl)),
                                         ln * _sig(ln) * (zh * _sig(zh)))
        diff = y - t_ref[...]
        dy_ref[...] = diff / float(D_MODEL)
        lacc[...] += _cs8(diff * diff)

        @pl.when(i == nt - 1)
        def _():
            l_ref[...] = jnp.sum(lacc[...], axis=0, keepdims=True)

    row = pl.BlockSpec((T, D_MODEL), lambda i: (i, 0))
    return _call(
        body, "odd_mixer_fwd", (nt,),
        [row, row, pl.BlockSpec((D_MODEL, D_MODEL), lambda i: (0, 0)),
         col(0), col(1), col(2), col(3), col(4), prev(0), prev(1), prev(2)] + wspecs,
        [row, pl.BlockSpec((1, D_MODEL), lambda i: (0, 0)), pl.BlockSpec((D_MODEL, T), lambda i: (0, i)),
         pl.BlockSpec((T, A_WIDTH), lambda i: (i, 0))],
        [jax.ShapeDtypeStruct((S, D_MODEL), F32), jax.ShapeDtypeStruct((1, D_MODEL), F32),
         jax.ShapeDtypeStruct((D_MODEL, S), BF16), jax.ShapeDtypeStruct((S, A_WIDTH), F32)],
        scratch=[pltpu.VMEM((T + H, A_WIDTH), F32), pltpu.VMEM((T + H, A_WIDTH), F32),
                 pltpu.VMEM((SUBLANES - 1, T + H - SHIFT_ROWS_LESS, A_WIDTH), F32),
                 pltpu.VMEM((T, A_WIDTH), F32), pltpu.VMEM((T, A_WIDTH), F32),
                 pltpu.VMEM((SUBLANES, D_MODEL), F32)],
        sem=("arbitrary",))(x, tgt, w_out, proj, proj, proj, proj, proj, proj, proj, proj,
                            pool_w, scale, dconv_w, dconv_b, ln_w, ln_b)


def _odd_mixer_bwd(dy, w_out, proj, conv, pool_w, scale, dconv_w, dconv_b, ln_w, ln_b, T):
    S = proj.shape[0]
    nt = S // T
    order = lambda s: nt - 1 - s
    col, prev, wspecs = _odd_specs(T, S, order)
    H = HALO

    def body(dy_r, w_ref, cv_r, uc_r, da_r, dg_r, zl_r, zh_r, ucp_r, dap_r, dgp_r, pw_r, sc_r, dw_r, db_r, lw_r, lb_r,
             dp_ref, dpw_ref, sm_ref, ext_u, ext_g, sh_g, pooled_s, pm_s, dpl_s, ext_p, ext_c, sh_c, acc):
        step = pl.program_id(0)
        i = nt - 1 - step

        @pl.when(step == 0)
        def _():
            ext_p[T:T + H, :] = jnp.zeros((H, A_WIDTH), F32)
            ext_c[T:T + H, :] = jnp.zeros((H, A_WIDTH), F32)
            acc[...] = jnp.zeros_like(acc)
            dpw_ref[...] = jnp.zeros_like(dpw_ref)

        def accum(r, v):
            acc[r * SUBLANES:(r + 1) * SUBLANES, :] += _cs8(v)

        row = _odd_pool_tile(i, uc_r, ucp_r, pw_r, ext_u, pooled_s, pm_s, T)
        _odd_glu_tile(i, da_r, dg_r, dap_r, dgp_r, ext_g, sh_g, T)
        conv = cv_r[...]
        mu = jnp.mean(conv, axis=-1, keepdims=True)
        xc = conv - mu
        rstd = lax.rsqrt(jnp.mean(xc * xc, axis=-1, keepdims=True) + EPS)
        yh = xc * rstd
        ln = yh * lw_r[...] + lb_r[...]
        sln = _sig(ln)
        zl, zh = zl_r[...], zh_r[...]
        sl, sh = _sig(zl), _sig(zh)
        du = lax.dot_general(dy_r[...].astype(BF16), w_ref[...], NT_DIMS, preferred_element_type=F32)
        dul, duh = du[:, 0:A_WIDTH], du[:, A_WIDTH:]
        pm = pm_s[...]
        scv = sc_r[...]
        dyc = dul * (zl * sl)
        accum(34, dyc * pm)
        dpm = dyc * scv
        for g in range(len(POOL_SIZES)):
            cs = slice(g * LANES, (g + 1) * LANES)
            dpm_g = dpm[:, cs].astype(BF16)
            dpw_ref[g] += lax.dot_general(pooled_s[:, cs].astype(BF16), dpm_g, TN_DIMS, preferred_element_type=F32)
            dpl_s[:, cs] = lax.dot_general(dpm_g, pw_r[g].astype(BF16), NT_DIMS, preferred_element_type=F32)
        lane_p = lax.broadcasted_iota(jnp.int32, (1, A_WIDTH), 1) // LANES
        pvec = jnp.left_shift(2, lane_p)
        cnt = jnp.minimum(row + 1, pvec).astype(F32)
        dpl = dpl_s[...]
        ext_p[0:T, :] = dpl / cnt
        for g, p in enumerate(POOL_SIZES):
            cs = slice(g * LANES, (g + 1) * LANES)
            win = ext_p[0:T, cs]
            for j in range(1, p):
                win = win + ext_p[j:j + T, cs]
            dp_ref[:, cs] = (win - dpl[:, cs]).astype(BF16)
        ext_p[T:T + H, :] = ext_p[0:H, :]
        dln = duh * (zh * sh) * _dsilu(ln, sln)
        accum(32, dln * yh)
        accum(33, dln)
        dyh = dln * lw_r[...]
        dc = rstd * (dyh - jnp.mean(dyh, axis=-1, keepdims=True) - yh * jnp.mean(dyh * yh, axis=-1, keepdims=True))
        accum(31, dc)
        ext_c[0:T, :] = dc
        _fill_shifted(ext_c, sh_c)
        base = H - (D_CONV - 1)
        dgl = dw_r[0:1, :] * _window(ext_c, sh_c, D_CONV - 1, T)
        accum(0, dc * _window(ext_g, sh_g, base, T))
        for kk in range(1, D_CONV):
            dgl = dgl + dw_r[kk:kk + 1, :] * _window(ext_c, sh_c, D_CONV - 1 - kk, T)
            accum(kk, dc * _window(ext_g, sh_g, base + kk, T))
        ext_c[T:T + H, :] = ext_c[0:H, :]
        dav, dgv = da_r[...], dg_r[...]
        sg = _sig(dgv)
        dp_ref[:, A_WIDTH:2 * A_WIDTH] = (dgl * sg).astype(BF16)
        dp_ref[:, 2 * A_WIDTH:3 * A_WIDTH] = (dgl * dav * sg * (1.0 - sg)).astype(BF16)
        dp_ref[:, 3 * A_WIDTH:4 * A_WIDTH] = (dul * (pm * scv) * _dsilu(zl, sl)).astype(BF16)
        dp_ref[:, 4 * A_WIDTH:5 * A_WIDTH] = (duh * (ln * sln) * _dsilu(zh, sh)).astype(BF16)

        @pl.when(step == nt - 1)
        def _():
            for r in range(N_SMALL_ODD):
                sm_ref[r:r + 1, :] = jnp.sum(acc[r * SUBLANES:(r + 1) * SUBLANES, :], axis=0, keepdims=True)

    ext = pltpu.VMEM((T + H, A_WIDTH), F32)
    shifted = pltpu.VMEM((SUBLANES - 1, T + H - SHIFT_ROWS_LESS, A_WIDTH), F32)
    tile = pltpu.VMEM((T, A_WIDTH), F32)
    return _call(
        body, "odd_mixer_bwd", (nt,),
        [pl.BlockSpec((T, D_MODEL), lambda s: (order(s), 0)), pl.BlockSpec((D_MODEL, D_MODEL), lambda s: (0, 0)),
         pl.BlockSpec((T, A_WIDTH), lambda s: (order(s), 0)),
         col(0), col(1), col(2), col(3), col(4), prev(0), prev(1), prev(2)] + wspecs,
        [pl.BlockSpec((T, ODD_IN), lambda s: (order(s), 0)),
         pl.BlockSpec((4, LANES, LANES), lambda s: (0, 0, 0)),
         pl.BlockSpec((N_SMALL_ODD, A_WIDTH), lambda s: (0, 0))],
        [jax.ShapeDtypeStruct((S, ODD_IN), BF16), jax.ShapeDtypeStruct((4, LANES, LANES), F32),
         jax.ShapeDtypeStruct((N_SMALL_ODD, A_WIDTH), F32)],
        scratch=[ext, ext, shifted, tile, tile, tile, ext, ext, shifted,
                 pltpu.VMEM((N_SMALL_ODD * SUBLANES, A_WIDTH), F32)],
        sem=("arbitrary",))(dy, w_out, conv, proj, proj, proj, proj, proj, proj, proj, proj,
                            pool_w, scale, dconv_w, dconv_b, ln_w, ln_b)


TILE_SEQ = 256
TILE_WG = 256
TILE_FIRST = 512
TILE_MM = 512


SMALL_PACK = "small_pack"
SMALL_PACK_W = 2 * LANES
LATE_WEIGHTS = ("e_w_out", "o_w_in", "o_w_out", SMALL_PACK)
ODD_MATS = ("o_w_in", "o_w_out")
EVEN_MATS = ("e_w_in", "e_w_out")


def _reduce_start(names, grads, grads16, cidx):
    recv = _swap_to_sibling(names, [grads16[n] for n in names], "swap_halves_" + names[0][0], True)
    both = [_add_half(cidx, grads[n], r, n) for n, r in zip(names, recv)]
    return [h for h, _ in both], [hb for _, hb in both]


def _local_step(x, pos, tgt, shards, p, unpack_small, cidx, bidx):
    T = TILE_SEQ
    freq = _freq_table()
    wq = jnp.tile(p["e_q_norm_w"], (1, LANES // HEAD_DIM))
    wk = jnp.tile(p["e_k_norm_w"], (1, LANES // HEAD_DIM))

    proj_e, ht_e, w_e_in, late = _inproj_gathering(x, p["e_norm_w"], shards["e_w_in"], bidx, "e_w_in", LATE_WEIGHTS,
                                                   [shards[n] for n in LATE_WEIGHTS], TILE_FIRST, "inproj_even")
    wb = dict(zip(LATE_WEIGHTS, late), e_w_in=w_e_in)
    p = dict(p, **unpack_small(wb[SMALL_PACK]))
    qkv = _qkv_prep(proj_e, pos, freq, wq, wk, T)
    qs, ks, vs, rope = qkv[0:3], qkv[3:6], qkv[6:9], qkv[9:12]
    os_, ls_ = [], []
    for g in range(3):
        o, l = _attn_fwd(qs[g], ks[g], vs[g], g)
        os_.append(o)
        ls_.append(l)
    x1, ut_e = _even_mixer_fwd(x, proj_e, os_, ls_, p["e_conv_w"], wb["e_w_out"], T)
    proj_o, ht_o = _inproj(x1, p["o_norm_w"], wb["o_w_in"], TILE_MM, 1280, "inproj_odd")
    odd_w = (p["o_pool_w"], p["o_pool_scale"], p["o_dconv_w"], p["o_dconv_b"], p["o_ln_w"], p["o_ln_b"])
    dy, lsum, ut_o, conv_o = _odd_mixer_fwd(x1, tgt, proj_o, *odd_w, wb["o_w_out"], T)

    g, g16 = {}, {}
    g["o_w_out"], g16["o_w_out"] = _mm_wgrad(ut_o, [dy], TILE_WG, "wgrad_o_out")
    dproj_o, g["o_pool_w"], small_o = _odd_mixer_bwd(dy, wb["o_w_out"], proj_o, conv_o, *odd_w, T)
    g["o_w_in"], g16["o_w_in"] = _mm_wgrad(ht_o, [dproj_o], TILE_WG, "wgrad_o_in")
    half_o, half_o16 = _reduce_start(ODD_MATS, g, g16, cidx)
    dx1, g["o_norm_w"], blocks_o = _mm_nt_rms([dproj_o], wb["o_w_in"], x1, p["o_norm_w"], dy, TILE_MM, "dx_odd",
                                              ODD_MATS, half_o16)
    g["o_dconv_w"] = small_o[0:D_CONV]
    g["o_dconv_b"] = small_o[31:32]
    g["o_ln_w"] = small_o[32:33]
    g["o_ln_b"] = small_o[33:34]
    g["o_pool_scale"] = small_o[34:35]

    g["e_w_out"], g16["e_w_out"] = _mm_wgrad(ut_e, [dx1], TILE_WG, "wgrad_e_out")
    dos, cgs, drest, g["e_conv_w"] = _even_mixer_bwd(dx1, wb["e_w_out"], proj_e, os_, ls_, p["e_conv_w"], T)
    dqs, dks, dvs = [], [], []
    for gi in range(3):
        dq, dk, dv = _attn_bwd(qs[gi], ks[gi], vs[gi], dos[gi], ls_[gi], cgs[gi], gi)
        dqs.append(dq)
        dks.append(dk)
        dvs.append(dv)
    dqkv, dnw = _qk_bwd(proj_e, dqs, dks, dvs, rope, wq, wk, T)
    g["e_q_norm_w"] = dnw[0:1, 0:HEAD_DIM]
    g["e_k_norm_w"] = dnw[1:2, 0:HEAD_DIM]
    pieces = [dqkv, drest]
    g["e_w_in"], g16["e_w_in"] = _mm_wgrad(ht_e, pieces, TILE_WG, "wgrad_e_in")
    half_e, half_e16 = _reduce_start(EVEN_MATS, g, g16, cidx)
    dx, g["e_norm_w"], blocks_e = _mm_nt_rms(pieces, wb["e_w_in"], x, p["e_norm_w"], dx1, TILE_MM, "dx_even",
                                             EVEN_MATS, half_e16)
    parts = {}
    for names, halves, blocks in ((ODD_MATS, half_o, blocks_o), (EVEN_MATS, half_e, blocks_e)):
        for n, h, r in zip(names, halves, blocks):
            parts[n] = _add_blocks(bidx, h, r, n)
    return lsum, dx, g, parts


BIG = ("e_w_in", "e_w_out", "o_w_in", "o_w_out")
SHARD_AXIS = {"e_w_in": 1, "e_w_out": 0, "o_w_in": 1, "o_w_out": 0, SMALL_PACK: 0}
N_CHIPS = 4


def _place():
    x, y, c = lax.axis_index("x"), lax.axis_index("y"), lax.axis_index("c")
    chips = [(1 - x, y), (x, 1 - y), (1 - x, 1 - y)]
    return x, y, c, chips


def _block_of(ref, name, block):
    rows, cols = ref.shape
    if SHARD_AXIS[name] == 1:
        cw = cols // N_CHIPS
        return ref.at[:, pl.ds(pl.multiple_of(block * cw, LANES), cw)]
    rw = rows // N_CHIPS
    return ref.at[pl.ds(pl.multiple_of(block * rw, rw), rw), :]


def _half_of(ref, name, half):
    rows, cols = ref.shape
    if SHARD_AXIS[name] == 1:
        return ref.at[pl.ds(pl.multiple_of(half * (rows // 2), rows // 2), rows // 2), :]
    return ref.at[:, pl.ds(pl.multiple_of(half * (cols // 2), LANES), cols // 2)]


def _sub(ref, name, block, half):
    rows, cols = ref.shape
    if SHARD_AXIS[name] == 1:
        cw, hr = cols // N_CHIPS, rows // 2
        return ref.at[pl.ds(pl.multiple_of(half * hr, hr), hr), pl.ds(pl.multiple_of(block * cw, LANES), cw)]
    rw, hc = rows // N_CHIPS, cols // 2
    return ref.at[pl.ds(pl.multiple_of(block * rw, rw), rw), pl.ds(pl.multiple_of(half * hc, LANES), hc)]


GATHER_COPIES = 7


class _Gather:
    def __init__(self, names, s_refs, f_refs, send, recv):
        self.names, self.s, self.f, self.send, self.recv = names, s_refs, f_refs, send, recv

    def _copy(self, k, src, dst, to):
        return pltpu.make_async_remote_copy(src_ref=src, dst_ref=dst, send_sem=self.send.at[k],
                                            recv_sem=self.recv.at[k], device_id=to, device_id_type=MESH)

    def _plan(self):
        x, y, c, chips = _place()
        me, sib = 2 * x + y, (x, y, 1 - c)
        first, relay_in, relay, last_in = [], [], [], []
        for wi, n in enumerate(self.names):
            k0 = wi * GATHER_COPIES
            s, f = self.s[wi], self.f[wi]
            own = _block_of(f, n, me)
            first.append(self._copy(k0 + 3, s, own, sib))
            last_in.append(self._copy(k0 + 3, s, own, sib))
            for j, (cx, cy) in enumerate(chips):
                first.append(self._copy(k0 + j, _half_of(s, n, c), _sub(f, n, me, c), (cx, cy, c)))
                mine = _sub(f, n, 2 * cx + cy, c)
                relay_in.append(self._copy(k0 + j, mine, mine, sib))
                relay.append(self._copy(k0 + 4 + j, mine, mine, sib))
                theirs = _sub(f, n, 2 * cx + cy, 1 - c)
                last_in.append(self._copy(k0 + 4 + j, theirs, theirs, sib))
        return first, relay_in, relay, last_in

    N_RELATIONS = 3

    def begin(self, relations=(0, 1, 2), sibling=True):
        first = self._plan()[0]
        for wi in range(len(self.names)):
            mine = first[wi * (1 + self.N_RELATIONS):(wi + 1) * (1 + self.N_RELATIONS)]
            if sibling:
                mine[0].start()
            for j in relations:
                mine[1 + j].start()

    def relay(self, relations=(0, 1, 2)):
        _, relay_in, relay, _ = self._plan()
        for wi in range(len(self.names)):
            for j in relations:
                relay_in[wi * self.N_RELATIONS + j].wait_recv()
                relay[wi * self.N_RELATIONS + j].start()

    def end(self):
        first, _, relay, last_in = self._plan()
        for cp in last_in:
            cp.wait_recv()
        for cp in first + relay:
            cp.wait_send()

    def wait_relayed(self, j):
        self._plan()[3][1 + j].wait_recv()

    def end_rest(self):
        first, _, relay, last_in = self._plan()
        last_in[0].wait_recv()
        for cp in first + relay:
            cp.wait_send()


def _full_shape(n, s):
    r, cdim = s.shape
    return jax.ShapeDtypeStruct((r, cdim * N_CHIPS) if SHARD_AXIS[n] == 1 else (r * N_CHIPS, cdim), s.dtype)


def _gather_sems(names):
    k = GATHER_COPIES * len(names)
    return [pltpu.SemaphoreType.DMA((k,)), pltpu.SemaphoreType.DMA((k,))]


def _scatter_copies(names, h_refs, r_refs, send, recv):
    _, _, c, chips = _place()
    cps = []
    for wi, n in enumerate(names):
        for j, (cx, cy) in enumerate(chips):
            cps.append(pltpu.make_async_remote_copy(
                src_ref=_block_of(h_refs[wi], n, 2 * cx + cy), dst_ref=r_refs[wi].at[j],
                send_sem=send.at[wi * 3 + j], recv_sem=recv.at[wi * 3 + j],
                device_id=(cx, cy, c), device_id_type=MESH))
    return cps


def _scatter_sems(names):
    return [pltpu.SemaphoreType.DMA((3 * len(names),)), pltpu.SemaphoreType.DMA((3 * len(names),))]


class _SmallSum:
    def __init__(self, p_ref, o_ref, sbuf, cbuf, send, recv):
        self.p, self.o, self.sbuf, self.cbuf, self.send, self.recv = p_ref, o_ref, sbuf, cbuf, send, recv

    def _copy(self, k, ref, to):
        return pltpu.make_async_remote_copy(src_ref=ref, dst_ref=ref, send_sem=self.send.at[k],
                                            recv_sem=self.recv.at[k], device_id=to, device_id_type=MESH)

    def _plan(self):
        x, y, c, chips = _place()
        me, sib = 2 * x + y, (x, y, 1 - c)
        d2d_out = self._copy(0, self.sbuf.at[c], sib)
        d2d_in = self._copy(0, self.sbuf.at[1 - c], sib)
        ici_out = [self._copy(1 + j, self.cbuf.at[me], (cx, cy, c)) for j, (cx, cy) in enumerate(chips)]
        ici_in = [self._copy(1 + j, self.cbuf.at[2 * cx + cy], (cx, cy, c)) for j, (cx, cy) in enumerate(chips)]
        return c, me, d2d_out, d2d_in, ici_out, ici_in

    def begin(self):
        c, _, d2d_out, _, _, _ = self._plan()
        self.sbuf[c] = self.p[...]
        d2d_out.start()

    def middle(self):
        _, me, _, d2d_in, ici_out, _ = self._plan()
        d2d_in.wait_recv()
        self.cbuf[me] = self.sbuf[0] + self.sbuf[1]
        for cp in ici_out:
            cp.start()

    def end(self):
        _, _, d2d_out, _, ici_out, ici_in = self._plan()
        for cp in ici_in:
            cp.wait_recv()
        self.o[...] = (self.cbuf[0] + self.cbuf[1]) + (self.cbuf[2] + self.cbuf[3])
        for cp in [d2d_out] + ici_out:
            cp.wait_send()


def _small_sum_scratch(R):
    return [pltpu.VMEM((2, R, LANES), F32), pltpu.VMEM((N_CHIPS, R, LANES), F32),
            pltpu.SemaphoreType.DMA((4,)), pltpu.SemaphoreType.DMA((4,))]


def _half_shape(shape, name):
    r, cdim = shape
    return (r // 2, cdim) if SHARD_AXIS[name] == 1 else (r, cdim // 2)


def _shard_shape(shape, name):
    r, cdim = shape
    return (r, cdim // N_CHIPS) if SHARD_AXIS[name] == 1 else (r // N_CHIPS, cdim)


def _swap_to_sibling(names, srcs, name, pick_half, small=None):
    nw = len(names)
    ns = 0 if small is None else 1
    vm = pl.BlockSpec(memory_space=pltpu.VMEM)

    def body(*refs):
        g_refs = refs[:nw]
        r_refs = refs[nw + ns:2 * nw + ns]
        send, recv = refs[2 * nw + 2 * ns:2 * nw + 2 * ns + 2]
        x, y, c, _ = _place()
        sib = (x, y, 1 - c)
        cps = []
        for wi, n in enumerate(names):
            src = _half_of(g_refs[wi], n, 1 - c) if pick_half else g_refs[wi]
            cp = pltpu.make_async_remote_copy(src_ref=src, dst_ref=r_refs[wi], send_sem=send.at[wi],
                                              recv_sem=recv.at[wi], device_id=sib, device_id_type=MESH)
            cp.start()
            cps.append(cp)
        if ns:
            total = _SmallSum(refs[nw], refs[2 * nw + ns], *refs[2 * nw + 2 * ns + 2:])
            total.begin()
            total.middle()
            total.end()
        for cp in cps:
            cp.wait()

    outs = [jax.ShapeDtypeStruct(_half_shape(g.shape, n) if pick_half else g.shape, g.dtype)
            for n, g in zip(names, srcs)]
    return pl.pallas_call(
        body, name=name, in_specs=[ANY] * nw + [vm] * ns, out_specs=[ANY] * nw + [vm] * ns,
        out_shape=outs + ([jax.ShapeDtypeStruct(small.shape, F32)] if ns else []),
        scratch_shapes=[pltpu.SemaphoreType.DMA((nw,)), pltpu.SemaphoreType.DMA((nw,))] +
        (_small_sum_scratch(small.shape[0]) if ns else []),
    )(*srcs, *([small] if ns else []))


def _add_half(cidx, g, r, name):
    rows, cols = r.shape
    tr = 256
    tc = cols if cols <= 1792 else (1792 if cols % 1792 == 0 else 1280)
    nr, nc = rows // tr, cols // tc

    def body(c_ref, g_ref, r_ref, o_ref, ob_ref):
        s = g_ref[...] + r_ref[...].astype(F32)
        o_ref[...] = s
        ob_ref[...] = s.astype(BF16)

    if SHARD_AXIS[name] == 1:
        gmap = lambda i, j, c_ref: (c_ref[0] * nr + i, j)
    else:
        gmap = lambda i, j, c_ref: (i, c_ref[0] * nc + j)
    same = lambda i, j, c_ref: (i, j)
    return pl.pallas_call(
        body, name="add_half_" + name,
        grid_spec=pltpu.PrefetchScalarGridSpec(
            num_scalar_prefetch=1, grid=(nr, nc),
            in_specs=[pl.BlockSpec((tr, tc), gmap), pl.BlockSpec((tr, tc), same)],
            out_specs=[pl.BlockSpec((tr, tc), same), pl.BlockSpec((tr, tc), same)]),
        out_shape=[jax.ShapeDtypeStruct(r.shape, F32), jax.ShapeDtypeStruct(r.shape, BF16)],
        compiler_params=pltpu.CompilerParams(dimension_semantics=("parallel", "parallel"), vmem_limit_bytes=VMEM_LIMIT),
    )(cidx, g, r)


def _add_blocks(bidx, h, r, name):
    _, rows, cols = r.shape
    tr = min(rows, 256)
    nr = rows // tr

    def body(b_ref, h_ref, r0, r1, r2, o_ref):
        o_ref[...] = ((h_ref[...] + r0[0].astype(F32)) + r1[0].astype(F32)) + r2[0].astype(F32)

    if SHARD_AXIS[name] == 1:
        hmap = lambda i, b_ref: (i, b_ref[0])
    else:
        hmap = lambda i, b_ref: (b_ref[0] * nr + i, 0)
    rspec = lambda j: pl.BlockSpec((1, tr, cols), lambda i, b_ref, j=j: (j, i, 0))
    return pl.pallas_call(
        body, name="add_blocks_" + name,
        grid_spec=pltpu.PrefetchScalarGridSpec(
            num_scalar_prefetch=1, grid=(nr,),
            in_specs=[pl.BlockSpec((tr, cols), hmap), rspec(0), rspec(1), rspec(2)],
            out_specs=pl.BlockSpec((tr, cols), lambda i, b_ref: (i, 0))),
        out_shape=jax.ShapeDtypeStruct((rows, cols), F32),
        compiler_params=pltpu.CompilerParams(dimension_semantics=("parallel",), vmem_limit_bytes=VMEM_LIMIT),
    )(bidx, h, r, r, r)


def _adam_math(w, g, m, v):
    c1 = 1.0 - ADAM_B1 ** ADAM_STEP
    c2 = 1.0 - ADAM_B2 ** ADAM_STEP
    nm = ADAM_B1 * m + (1.0 - ADAM_B1) * g
    nv = ADAM_B2 * v + (1.0 - ADAM_B2) * (g * g)
    delta = -ADAM_LR * ((nm / c1) / (jnp.sqrt(nv / c2) + ADAM_EPS) + ADAM_WD * w)
    return delta, nm, nv


def _adamw(w, g, m, v, name):
    def body(w_ref, g_ref, m_ref, v_ref, d_ref, nm_ref, nv_ref):
        d_ref[...], nm_ref[...], nv_ref[...] = _adam_math(w_ref[...], g_ref[...], m_ref[...], v_ref[...])

    spec = pl.BlockSpec(w.shape, lambda i: (0, 0))
    return _call(body, "adamw_" + name, (1,), [spec] * 4, [spec] * 3,
                 [jax.ShapeDtypeStruct(w.shape, F32)] * 3, sem=("arbitrary",))(w, g, m, v)


def _adamw_halves(cidx, w, mine, theirs, m, v, name):
    hr, hc = mine.shape
    tr = 128
    ni = hr // tr
    if SHARD_AXIS[name] == 1:
        wmap = lambda hh, i, c_ref: (hh * ni + i, 0)
    else:
        wmap = lambda hh, i, c_ref: (i, hh)
    hmap = lambda hh, i, c_ref: (i, 0)

    def body(c_ref, w_ref, a_ref, b_ref, m_ref, v_ref, g_ref, d_ref, nm_ref, nv_ref):
        g = jnp.where(pl.program_id(0) == c_ref[0], a_ref[...], b_ref[...])
        g_ref[...] = g
        d_ref[...], nm_ref[...], nv_ref[...] = _adam_math(w_ref[...], g, m_ref[...], v_ref[...])

    wspec = pl.BlockSpec((tr, hc), wmap)
    hspec = pl.BlockSpec((tr, hc), hmap)
    return pl.pallas_call(
        body, name="adamw_" + name,
        grid_spec=pltpu.PrefetchScalarGridSpec(
            num_scalar_prefetch=1, grid=(2, ni),
            in_specs=[wspec, hspec, hspec, wspec, wspec], out_specs=[wspec] * 4),
        out_shape=[jax.ShapeDtypeStruct(w.shape, F32)] * 4,
        compiler_params=pltpu.CompilerParams(dimension_semantics=("parallel", "parallel"), vmem_limit_bytes=VMEM_LIMIT),
    )(cidx, w, mine, theirs, m, v)


SMALL = ("e_norm_w", "e_q_norm_w", "e_k_norm_w", "e_conv_w", "o_norm_w", "o_pool_w", "o_pool_scale",
         "o_dconv_w", "o_dconv_b", "o_ln_w", "o_ln_b")
SMALL_SHARDED = ("e_conv_w", "o_norm_w", "o_pool_scale", "o_dconv_w", "o_dconv_b", "o_ln_w", "o_ln_b")
WEIGHTS = ("e_norm_w", "e_w_in", "e_q_norm_w", "e_k_norm_w", "e_conv_w", "e_w_out", "o_norm_w", "o_w_in",
           "o_pool_w", "o_pool_scale", "o_dconv_w", "o_dconv_b", "o_ln_w", "o_ln_b", "o_w_out")


def _pack(arrs):
    flat = jnp.concatenate([a.reshape(-1) for a in arrs])
    rows = -(-flat.shape[0] // (LANES * SUBLANES)) * SUBLANES
    flat = jnp.pad(flat, (0, rows * LANES - flat.shape[0]))
    return flat.reshape(rows, LANES)


def _unpack(packed, shapes):
    flat = packed.reshape(-1)
    out, off = [], 0
    for s in shapes:
        n = int(np.prod(s))
        out.append(flat[off:off + n].reshape(s))
        off += n
    return out


def _gather_last(a, block, width):
    return lax.dynamic_slice_in_dim(a, block * width, width, axis=a.ndim - 1)


def kernel(x, positions, e_norm_w, e_w_in, e_q_norm_w, e_k_norm_w, e_conv_w, e_w_out, o_norm_w, o_w_in, o_pool_w, o_pool_scale, o_dconv_w, o_dconv_b, o_ln_w, o_ln_b, o_w_out, loss_target, m_e_norm_w, m_e_w_in, m_e_q_norm_w, m_e_k_norm_w, m_e_conv_w, m_e_w_out, m_o_norm_w, m_o_w_in, m_o_pool_w, m_o_pool_scale, m_o_dconv_w, m_o_dconv_b, m_o_ln_w, m_o_ln_b, m_o_w_out, v_e_norm_w, v_e_w_in, v_e_q_norm_w, v_e_k_norm_w, v_e_conv_w, v_e_w_out, v_o_norm_w, v_o_w_in, v_o_pool_w, v_o_pool_scale, v_o_dconv_w, v_o_dconv_b, v_o_ln_w, v_o_ln_b, v_o_w_out):
    given = dict(e_norm_w=e_norm_w, e_w_in=e_w_in, e_q_norm_w=e_q_norm_w, e_k_norm_w=e_k_norm_w, e_conv_w=e_conv_w,
                 e_w_out=e_w_out, o_norm_w=o_norm_w, o_w_in=o_w_in, o_pool_w=o_pool_w, o_pool_scale=o_pool_scale,
                 o_dconv_w=o_dconv_w, o_dconv_b=o_dconv_b, o_ln_w=o_ln_w, o_ln_b=o_ln_b, o_w_out=o_w_out)
    mom = dict(e_norm_w=m_e_norm_w, e_w_in=m_e_w_in, e_q_norm_w=m_e_q_norm_w, e_k_norm_w=m_e_k_norm_w,
               e_conv_w=m_e_conv_w, e_w_out=m_e_w_out, o_norm_w=m_o_norm_w, o_w_in=m_o_w_in, o_pool_w=m_o_pool_w,
               o_pool_scale=m_o_pool_scale, o_dconv_w=m_o_dconv_w, o_dconv_b=m_o_dconv_b, o_ln_w=m_o_ln_w,
               o_ln_b=m_o_ln_b, o_w_out=m_o_w_out)
    var = dict(e_norm_w=v_e_norm_w, e_w_in=v_e_w_in, e_q_norm_w=v_e_q_norm_w, e_k_norm_w=v_e_k_norm_w,
               e_conv_w=v_e_conv_w, e_w_out=v_e_w_out, o_norm_w=v_o_norm_w, o_w_in=v_o_w_in, o_pool_w=v_o_pool_w,
               o_pool_scale=v_o_pool_scale, o_dconv_w=v_o_dconv_w, o_dconv_b=v_o_dconv_b, o_ln_w=v_o_ln_w,
               o_ln_b=v_o_ln_b, o_w_out=v_o_w_out)
    S = x.shape[1]
    mx, my, mc = lax.axis_index("x"), lax.axis_index("y"), lax.axis_index("c")
    chip = 2 * mx + my
    cidx = jnp.reshape(mc, (1,)).astype(jnp.int32)
    bidx = jnp.reshape(chip, (1,)).astype(jnp.int32)

    shards = {n: given[n][0].astype(BF16) for n in BIG}
    shard_sizes = [int(np.prod(given[n].shape)) for n in SMALL_SHARDED]
    flat = jnp.concatenate([given[n].reshape(-1) for n in SMALL_SHARDED])
    rows = -(-flat.shape[0] // (SMALL_PACK_W * SUBLANES)) * SUBLANES
    shards[SMALL_PACK] = jnp.pad(flat, (0, rows * SMALL_PACK_W - flat.shape[0])).reshape(rows, SMALL_PACK_W)

    def unpack_small(full):
        gathered = full.reshape(N_CHIPS, rows * SMALL_PACK_W)
        out, off = {}, 0
        for n, size in zip(SMALL_SHARDED, shard_sizes):
            sh = given[n].shape[1:]
            parts = gathered[:, off:off + size].reshape((N_CHIPS,) + sh)
            fullp = jnp.moveaxis(parts, 0, -2).reshape(sh[:-1] + (N_CHIPS * sh[-1],))
            out[n] = fullp.reshape(-1, fullp.shape[-1])
            off += size
        return out

    p = dict(e_norm_w=e_norm_w, e_q_norm_w=e_q_norm_w, e_k_norm_w=e_k_norm_w, o_pool_w=o_pool_w[0])
    lsum, dx, g, parts = _local_step(x[0], positions.reshape(S, 1), loss_target[0], shards, p, unpack_small,
                                     cidx, bidx)
    mine = [g[n] for n in SMALL] + [(0.5 / float(D_MODEL)) * jnp.sum(lsum, keepdims=True)]
    *theirs, tot = _swap_to_sibling(BIG, [parts[n] for n in BIG], "swap_reduced", False, small=_pack(mine))
    tot = _unpack(tot, [a.shape for a in mine])
    loss = tot[-1].reshape(())

    grads, delta, new_m, new_v = {}, {}, {}, {}
    for n, other in zip(BIG, theirs):
        sh = given[n].shape
        outs = _adamw_halves(cidx, given[n][0], parts[n], other, mom[n][0], var[n][0], n)
        grads[n], delta[n], new_m[n], new_v[n] = [a.reshape(sh) for a in outs]
    for n, gv in zip(SMALL, tot):
        if n in SMALL_SHARDED:
            gv = _gather_last(gv, chip, gv.shape[-1] // N_CHIPS)
        grads[n] = gv.reshape(given[n].shape)
    big_small = "o_pool_w"
    pw = [src[big_small].reshape(-1, LANES) for src in (given, grads, mom, var)]
    for dst, a in zip((delta, new_m, new_v), _adamw(*pw, "pool_w")):
        dst[big_small] = a.reshape(given[big_small].shape)
    tiny = tuple(n for n in SMALL if n != big_small)
    shapes = [given[n].shape for n in tiny]
    packed = [_pack([src[n] for n in tiny]) for src in (given, grads, mom, var)]
    for dst, pk in zip((delta, new_m, new_v), _adamw(*packed, "small")):
        for n, a in zip(tiny, _unpack(pk, shapes)):
            dst[n] = a
    return (loss, dx[None], *[grads[n] for n in WEIGHTS], *[delta[n] for n in WEIGHTS],
            *[new_m[n] for n in WEIGHTS], *[new_v[n] for n in WEIGHTS])
```

```python
import numpy as np
import jax
import jax.numpy as jnp
from jax import lax
from jax.experimental import pallas as pl
from jax.experimental.pallas import tpu as pltpu

F32 = jnp.float32
BF16 = jnp.bfloat16

D_MODEL = 1024
HEAD_DIM = 64
A_WIDTH = 512
A_HEADS = 8
A_GROUPS = ((128, 1), (512, 4), (2048, 16))
QBLK = 128
ROT_DIM = 16
ROPE_THETA = 500000.0
POOL_SIZES = (2, 4, 8, 16)
D_CONV = 31
SC_WIDTH = 3
EVEN_IN = 7168
ODD_IN = 2560
EPS = 1e-6
NEG = -1e30
ADAM_LR, ADAM_B1, ADAM_B2, ADAM_EPS, ADAM_WD, ADAM_STEP = 0.001, 0.9, 0.999, 1e-08, 0.01, 10

LANES = 128
SUBLANES = 8
HALO = 32
VMEM_LIMIT = 52 * 1024 * 1024
MESH = pl.DeviceIdType.MESH
ANY = pl.BlockSpec(memory_space=pl.ANY)

NT_DIMS = (((1,), (1,)), ((), ()))
TN_DIMS = (((0,), (0,)), ((), ()))


def _call(body, name, grid, in_specs, out_specs, out_shape, scratch=(), sem=None, aliases=None):
    return pl.pallas_call(
        body, name=name, grid=grid, in_specs=in_specs, out_specs=out_specs, out_shape=out_shape,
        scratch_shapes=list(scratch), input_output_aliases=aliases or {},
        compiler_params=pltpu.CompilerParams(dimension_semantics=sem, vmem_limit_bytes=VMEM_LIMIT))


def _sig(v):
    return jax.nn.sigmoid(v)


def _dsilu(v, s):
    return s * (1.0 + v * (1.0 - s))


def _out_projection(ut_ref, w_ref, lo, hi):
    acc = None
    for k, v in enumerate((lo, hi)):
        ut_ref[k * A_WIDTH:(k + 1) * A_WIDTH, :] = v.T.astype(BF16)
        part = jnp.dot(v.astype(BF16), w_ref[k * A_WIDTH:(k + 1) * A_WIDTH, :], preferred_element_type=F32)
        acc = part if acc is None else acc + part
    return acc


def _cs8(v):
    return v.reshape(v.shape[0] // SUBLANES, SUBLANES, v.shape[1]).sum(axis=0)


def _seg_mean():
    r = lax.broadcasted_iota(jnp.int32, (LANES, LANES), 0) // HEAD_DIM
    c = lax.broadcasted_iota(jnp.int32, (LANES, LANES), 1) // HEAD_DIM
    return jnp.where(r == c, 1.0 / HEAD_DIM, 0.0).astype(BF16)


def _segsum(v, ones):
    hi = v.astype(BF16)
    lo = (v - hi.astype(F32)).astype(BF16)
    return (jnp.dot(hi, ones, preferred_element_type=F32) + jnp.dot(lo, ones, preferred_element_type=F32))


def _head_rstd(v, seg_mean):
    return lax.rsqrt(jnp.dot((v * v).astype(BF16), seg_mean, preferred_element_type=F32) + EPS)


def _rope_tables(pos_ref, freq_ref):
    ang = pos_ref[...].astype(F32) * freq_ref[...]
    cosv, sinv = jnp.cos(ang), jnp.sin(ang)
    lm = lax.broadcasted_iota(jnp.int32, ang.shape, 1) % HEAD_DIM
    half = ROT_DIM // 2
    c = jnp.where(lm < ROT_DIM, cosv, 1.0)
    s1 = jnp.where((lm >= half) & (lm < ROT_DIM), sinv, 0.0)
    s2 = jnp.where(lm < half, -sinv, 0.0)
    return c, s1, s2


def _freq_table():
    half = ROT_DIM // 2
    inv = ROPE_THETA ** (-np.arange(half, dtype=np.float64) / half)
    lane = np.arange(LANES) % HEAD_DIM
    f = np.where(lane < ROT_DIM, inv[lane % half], 0.0)
    return jnp.asarray(f.reshape(1, LANES), F32)


def _load_once(hbm_ref, vmem_ref, sem):
    @pl.when(pl.program_id(0) == 0)
    def _():
        cp = pltpu.make_async_copy(hbm_ref, vmem_ref, sem)
        cp.start()
        cp.wait()


def _rms_rows(x_ref, nw_ref):
    xv = x_ref[...]
    ms = jnp.mean(xv * xv, axis=-1, keepdims=True)
    return xv * lax.rsqrt(ms + EPS) * nw_ref[...]


def _inproj(x, nw, w, tm, tn, name):
    S, N = x.shape[0], w.shape[1]

    def body(x_ref, nw_ref, w_hbm, o_ref, ht_ref, w_v, sem):
        _load_once(w_hbm, w_v, sem)
        h = _rms_rows(x_ref, nw_ref)
        ht_ref[...] = h.T.astype(BF16)
        hb = h.astype(BF16)
        for j in range(N // tn):
            o_ref[:, j * tn:(j + 1) * tn] = jnp.dot(hb, w_v[:, j * tn:(j + 1) * tn], preferred_element_type=F32)

    return _call(
        body, name, (S // tm,),
        [pl.BlockSpec((tm, D_MODEL), lambda i: (i, 0)),
         pl.BlockSpec((1, D_MODEL), lambda i: (0, 0)), ANY],
        [pl.BlockSpec((tm, N), lambda i: (i, 0)),
         pl.BlockSpec((D_MODEL, tm), lambda i: (0, i))],
        [jax.ShapeDtypeStruct((S, N), F32), jax.ShapeDtypeStruct((D_MODEL, S), BF16)],
        scratch=[pltpu.VMEM(w.shape, BF16), pltpu.SemaphoreType.DMA(())], sem=("arbitrary",))(x, nw, w)


def _inproj_gathering(x, nw, shard, bidx, first, late_names, late_shards, tm, name):
    S = x.shape[0]
    ni = S // tm
    K, cw = shard.shape
    nl = len(late_names)
    last = N_CHIPS - 1

    def body(b_ref, x_ref, nw_ref, s_hbm, *rest):
        ls_refs = rest[:nl]
        o_ref, ht_ref, f_hbm = rest[nl:nl + 3]
        lf_refs = rest[nl + 3:2 * nl + 3]
        hs, w_blk, lsem, send1, recv1, send2, recv2 = rest[2 * nl + 3:]
        j, i = pl.program_id(0), pl.program_id(1)
        g1 = _Gather((first,), (s_hbm,), (f_hbm,), send1, recv1)
        g2 = _Gather(late_names, ls_refs, lf_refs, send2, recv2)
        _, _, _, chips = _place()

        def load_block(src):
            cp = pltpu.make_async_copy(src, w_blk, lsem)
            cp.start()
            cp.wait()

        @pl.when((j == 0) & (i == 0))
        def _():
            g1.begin(relations=(0, 1))
            load_block(s_hbm)

        for r, (cx, cy) in enumerate(chips):
            @pl.when((j == r + 1) & (i == 0))
            def _(r=r, cx=cx, cy=cy):
                g1.wait_relayed(r)
                load_block(_block_of(f_hbm, first, 2 * cx + cy))

        @pl.when((j == 2) & (i == 0))
        def _():
            g1.relay(relations=(2,))
            g2.begin()

        pl.when((j == last) & (i == ni // 2))(g2.relay)

        rows = pl.ds(pl.multiple_of(i * tm, tm), tm)

        @pl.when(j == 0)
        def _():
            h = _rms_rows(x_ref, nw_ref)
            hs[rows, :] = h.astype(BF16)
            ht_ref[...] = h.T.astype(BF16)

        o_ref[...] = jnp.dot(hs[rows, :], w_blk[...], preferred_element_type=F32)

        @pl.when((j == 0) & (i == ni - 1))
        def _():
            g1.relay(relations=(0, 1))
            g1.begin(relations=(2,), sibling=False)

        @pl.when((j == last) & (i == ni - 1))
        def _():
            g1.end_rest()
            g2.end()

    def block_of_step(j, b_ref):
        return jnp.bitwise_xor(b_ref[0], jnp.bitwise_or(jnp.left_shift(jnp.bitwise_and(j, 1), 1), jnp.right_shift(j, 1)))

    outs = pl.pallas_call(
        body, name=name,
        grid_spec=pltpu.PrefetchScalarGridSpec(
            num_scalar_prefetch=1, grid=(N_CHIPS, ni),
            in_specs=[pl.BlockSpec((tm, D_MODEL), lambda j, i, b: (jnp.where(j == 0, i, 0), 0)),
                      pl.BlockSpec((1, D_MODEL), lambda j, i, b: (0, 0)), ANY] + [ANY] * nl,
            out_specs=[pl.BlockSpec((tm, cw), lambda j, i, b: (i, block_of_step(j, b))),
                       pl.BlockSpec((D_MODEL, tm), lambda j, i, b: (0, jnp.where(j == 0, i, ni - 1))),
                       ANY] + [ANY] * nl,
            scratch_shapes=[pltpu.VMEM((S, D_MODEL), BF16), pltpu.VMEM((K, cw), BF16), pltpu.SemaphoreType.DMA(())] +
            _gather_sems((first,)) + _gather_sems(late_names)),
        out_shape=[jax.ShapeDtypeStruct((S, cw * N_CHIPS), F32), jax.ShapeDtypeStruct((D_MODEL, S), BF16),
                   _full_shape(first, shard)] + [_full_shape(n, s) for n, s in zip(late_names, late_shards)],
        compiler_params=pltpu.CompilerParams(dimension_semantics=("arbitrary", "arbitrary"),
                                             vmem_limit_bytes=VMEM_LIMIT),
    )(bidx, x, nw, shard, *late_shards)
    return outs[0], outs[1], outs[2], list(outs[3:])


def _piece_blocks(pieces, tk, axis):
    starts, counts, s = [], [], 0
    for p in pieces:
        n = p.shape[axis] // tk
        starts.append(s)
        counts.append(n)
        s += n
    return starts, counts, s


def _mm_nt_rms(pieces, w, x, nw, dres, tm, name, scatter_names=(), scatter_halves=()):
    S = x.shape[0]
    npc = len(pieces)
    ni = S // tm
    ns = len(scatter_names)
    offs = np.cumsum([0] + [p.shape[1] for p in pieces]).tolist()

    def body(*refs):
        p_refs = refs[:npc]
        w_hbm, x_ref, nw_ref, dr_ref = refs[npc:npc + 4]
        h_refs = refs[npc + 4:npc + 4 + ns]
        dx_ref, dnw_ref = refs[npc + 4 + ns:npc + 6 + ns]
        r_refs = refs[npc + 6 + ns:npc + 6 + 2 * ns]
        w_v, sem, nacc = refs[npc + 6 + 2 * ns:npc + 9 + 2 * ns]
        i = pl.program_id(0)
        if ns:
            send, recv = refs[npc + 9 + 2 * ns:]

            @pl.when(i == 0)
            def _():
                for cp in _scatter_copies(scatter_names, h_refs, r_refs, send, recv):
                    cp.start()
        _load_once(w_hbm, w_v, sem)

        @pl.when(i == 0)
        def _():
            nacc[...] = jnp.zeros_like(nacc)

        dh = None
        for p in range(npc):
            part = lax.dot_general(p_refs[p][...].astype(BF16), w_v[:, offs[p]:offs[p + 1]], NT_DIMS,
                                   preferred_element_type=F32)
            dh = part if dh is None else dh + part
        xv = x_ref[...]
        rs = lax.rsqrt(jnp.mean(xv * xv, axis=-1, keepdims=True) + EPS)
        xh = xv * rs
        nacc[...] += _cs8(dh * xh)
        dxh = dh * nw_ref[...]
        dx_ref[...] = dr_ref[...] + rs * (dxh - xh * jnp.mean(dxh * xh, axis=-1, keepdims=True))

        @pl.when(i == ni - 1)
        def _():
            dnw_ref[...] = jnp.sum(nacc[...], axis=0, keepdims=True)
            if ns:
                for cp in _scatter_copies(scatter_names, h_refs, r_refs, send, recv):
                    cp.wait()

    row = pl.BlockSpec((tm, D_MODEL), lambda i: (i, 0))
    outs = _call(
        body, name, (ni,),
        [pl.BlockSpec((tm, p.shape[1]), lambda i: (i, 0)) for p in pieces] +
        [ANY, row, pl.BlockSpec((1, D_MODEL), lambda i: (0, 0)), row] + [ANY] * ns,
        [row, pl.BlockSpec((1, D_MODEL), lambda i: (0, 0))] + [ANY] * ns,
        [jax.ShapeDtypeStruct((S, D_MODEL), F32), jax.ShapeDtypeStruct((1, D_MODEL), F32)] +
        [jax.ShapeDtypeStruct((3,) + _shard_shape(h.shape, n), h.dtype) for n, h in zip(scatter_names, scatter_halves)],
        scratch=[pltpu.VMEM(w.shape, BF16), pltpu.SemaphoreType.DMA(()), pltpu.VMEM((SUBLANES, D_MODEL), F32)] +
        (_scatter_sems(scatter_names) if ns else []),
        sem=("arbitrary",))(*pieces, w, x, nw, dres, *scatter_halves)
    return outs[0], outs[1], list(outs[2:])


def _mm_wgrad(at, pieces, tn, name):
    M, S = at.shape
    starts, counts, nj = _piece_blocks(pieces, tn, 1)
    npc = len(pieces)

    def body(*refs):
        a_hbm = refs[0]
        p_refs = refs[1:1 + npc]
        o_ref, o16_ref, a_v, sem = refs[1 + npc:]
        j = pl.program_id(0)
        _load_once(a_hbm, a_v, sem)
        for p in range(npc):
            @pl.when((j >= starts[p]) & (j < starts[p] + counts[p]))
            def _(p=p):
                acc = jnp.dot(a_v[...], p_refs[p][...].astype(BF16), preferred_element_type=F32)
                o_ref[...] = acc
                o16_ref[...] = acc.astype(BF16)

    def pspec(p):
        return pl.BlockSpec((S, tn), lambda j: (0, jnp.clip(j - starts[p], 0, counts[p] - 1)))

    col = pl.BlockSpec((M, tn), lambda j: (0, j))
    return _call(
        body, name, (nj,),
        [ANY] + [pspec(p) for p in range(npc)], [col, col],
        [jax.ShapeDtypeStruct((M, nj * tn), F32), jax.ShapeDtypeStruct((M, nj * tn), BF16)],
        scratch=[pltpu.VMEM(at.shape, BF16), pltpu.SemaphoreType.DMA(())], sem=("arbitrary",))(at, *pieces)


def _stream_spec(d, T):
    return pl.BlockSpec((d, T // d, A_WIDTH), lambda i: (0, i, 0))


def _stream_shape(d, S, dtype):
    return jax.ShapeDtypeStruct((d, S // d, A_WIDTH), dtype)


N_CHUNK = A_WIDTH // LANES


def _to_tokens(ref, scr, d, T):
    if d == 1:
        return ref[0].astype(F32)
    for r in range(d):
        for ch in range(N_CHUNK):
            scr.at[ch][pl.ds(r, T // d, stride=d), :] = ref[r, :, ch * LANES:(ch + 1) * LANES].astype(F32)
    return _get(scr)


def _from_tokens(out_ref, scr, d, T):
    for r in range(d):
        for ch in range(N_CHUNK):
            out_ref[r, :, ch * LANES:(ch + 1) * LANES] = scr.at[ch][pl.ds(r, T // d, stride=d), :].astype(out_ref.dtype)


def _put(scr, val):
    for ch in range(N_CHUNK):
        scr[ch] = val[:, ch * LANES:(ch + 1) * LANES]


def _get(scr):
    return jnp.concatenate([scr[ch] for ch in range(N_CHUNK)], axis=1)


def _chunked(T):
    return pltpu.VMEM((N_CHUNK, T, LANES), F32)


def _compact_spec(d, T):
    return pl.BlockSpec((d, T // d, LANES), lambda i: (0, i, 0))


def _compact_shape(d, S):
    return jax.ShapeDtypeStruct((d, S // d, LANES), F32)


def _compact_to_tokens(ref, scr, d, T):
    if d == 1:
        return ref[0]
    for r in range(d):
        scr[pl.ds(r, T // d, stride=d), :] = ref[r]
    return scr[...]


def _compact_from_tokens(out_ref, scr, val, d, T):
    if d == 1:
        out_ref[0] = val
        return
    scr[...] = val
    for r in range(d):
        out_ref[r] = scr[pl.ds(r, T // d, stride=d), :]


def _head_expander():
    r = lax.broadcasted_iota(jnp.int32, (LANES, A_WIDTH), 0)
    c = lax.broadcasted_iota(jnp.int32, (LANES, A_WIDTH), 1) // HEAD_DIM
    return (r == c).astype(BF16)


def _head_reducer():
    r = lax.broadcasted_iota(jnp.int32, (A_WIDTH, LANES), 0) // HEAD_DIM
    c = lax.broadcasted_iota(jnp.int32, (A_WIDTH, LANES), 1)
    return (r == c).astype(BF16)


def _qkv_prep(proj, pos, freq, wq, wk, T):
    S = proj.shape[0]
    qk_w = 3 * A_WIDTH

    def body(q_ref, k_ref, v_ref, pos_ref, f_ref, wq_ref, wk_ref, *rest):
        outs, tabs, scr = rest[:9], rest[9:12], rest[12]
        seg_mean = _seg_mean()
        c, s1, s2 = _rope_tables(pos_ref, f_ref)
        for tab, val in zip(tabs, (c, s1, s2)):
            tab[...] = val
        for t, (src, w_ref) in enumerate(((q_ref, wq_ref), (k_ref, wk_ref), (v_ref, None))):
            for g in range(3):
                d = A_GROUPS[g][1]
                out = outs[3 * t + g]
                for ch in range(A_WIDTH // LANES):
                    cs = slice(ch * LANES, (ch + 1) * LANES)
                    v = src[:, g * A_WIDTH + ch * LANES: g * A_WIDTH + (ch + 1) * LANES]
                    if w_ref is not None:
                        y = v * _head_rstd(v, seg_mean) * w_ref[...]
                        v = y * c + pltpu.roll(y, 8, 1) * s1 + pltpu.roll(y, LANES - 8, 1) * s2
                    if d == 1:
                        out[0, :, cs] = v.astype(BF16)
                    else:
                        scr[ch] = v
                if d > 1:
                    _from_tokens(out, scr, d, T)

    ds_ = [A_GROUPS[g][1] for g in range(3)] * 3
    return _call(
        body, "qkv_prep", (S // T,),
        [pl.BlockSpec((T, qk_w), lambda i: (i, 0)), pl.BlockSpec((T, qk_w), lambda i: (i, 1)),
         pl.BlockSpec((T, qk_w), lambda i: (i, 2)),
         pl.BlockSpec((T, 1), lambda i: (i, 0)), pl.BlockSpec((1, LANES), lambda i: (0, 0)),
         pl.BlockSpec((1, LANES), lambda i: (0, 0)), pl.BlockSpec((1, LANES), lambda i: (0, 0))],
        [_stream_spec(d, T) for d in ds_] + [pl.BlockSpec((T, LANES), lambda i: (i, 0))] * 3,
        [_stream_shape(d, S, BF16) for d in ds_] + [jax.ShapeDtypeStruct((S, LANES), F32)] * 3,
        scratch=[_chunked(T)], sem=("parallel",))(proj, proj, proj, pos, freq, wq, wk)


def _attn_mask(i):
    qi = lax.broadcasted_iota(jnp.int32, (QBLK, 2 * QBLK), 0) + QBLK
    kj = lax.broadcasted_iota(jnp.int32, (QBLK, 2 * QBLK), 1)
    dist = qi - kj
    return (dist >= 0) & (dist <= QBLK) & ((i > 0) | (kj >= QBLK))


ATT_BLK = (None, QBLK, A_WIDTH)
ATT_CBLK = (None, QBLK, LANES)


def _first_head_lanes():
    return lax.broadcasted_iota(jnp.int32, (1, LANES), 1) < HEAD_DIM


def _split_heads(v, first):
    zero = jnp.zeros_like(v)
    return jnp.where(first, v, zero), jnp.where(first, zero, v)


def _attn_fwd(q, k, v, g):
    d, n, _ = q.shape
    nb = n // QBLK

    def body(q_ref, kp_ref, kc_ref, vp_ref, vc_ref, o_ref, l_ref, s_scr, p_scr):
        i = pl.program_id(1)
        mask = _attn_mask(i)
        first = _first_head_lanes()
        for pr in range(A_HEADS // 2):
            ps = slice(pr * LANES, (pr + 1) * LANES)
            kc = jnp.concatenate([kp_ref[:, ps], kc_ref[:, ps]], axis=0)
            for e, qh in enumerate(_split_heads(q_ref[:, ps], first)):
                s_scr[2 * pr + e] = lax.dot_general(qh, kc, NT_DIMS, preferred_element_type=F32)
        lane = lax.broadcasted_iota(jnp.int32, (1, LANES), 1)
        lrow = jnp.zeros((QBLK, LANES), F32)
        for h in range(A_HEADS):
            s = jnp.where(mask, s_scr[h] * (HEAD_DIM ** -0.5), NEG)
            m = jnp.max(s, axis=-1, keepdims=True)
            p = jnp.exp(s - m)
            den = jnp.sum(p, axis=-1, keepdims=True)
            p_scr[h] = (p / den).astype(BF16)
            lrow = jnp.where(lane == h, m + jnp.log(den), lrow)
        l_ref[...] = lrow
        for pr in range(A_HEADS // 2):
            ps = slice(pr * LANES, (pr + 1) * LANES)
            va, vb = _split_heads(jnp.concatenate([vp_ref[:, ps], vc_ref[:, ps]], axis=0), first)
            o_ref[:, ps] = (jnp.dot(p_scr[2 * pr], va, preferred_element_type=F32) +
                            jnp.dot(p_scr[2 * pr + 1], vb, preferred_element_type=F32)).astype(BF16)

    prev = lambda r, i: (r, jnp.maximum(i - 1, 0), 0)
    cur = lambda r, i: (r, i, 0)
    return _call(
        body, "attn_fwd_g%d" % g, (d, nb),
        [pl.BlockSpec(ATT_BLK, cur), pl.BlockSpec(ATT_BLK, prev), pl.BlockSpec(ATT_BLK, cur),
         pl.BlockSpec(ATT_BLK, prev), pl.BlockSpec(ATT_BLK, cur)],
        [pl.BlockSpec(ATT_BLK, cur), pl.BlockSpec(ATT_CBLK, cur)],
        [jax.ShapeDtypeStruct((d, n, A_WIDTH), BF16), jax.ShapeDtypeStruct((d, n, LANES), F32)],
        scratch=[pltpu.VMEM((A_HEADS, QBLK, 2 * QBLK), F32), pltpu.VMEM((A_HEADS, QBLK, 2 * QBLK), BF16)],
        sem=("parallel", "parallel"))(q, k, k, v, v)


def _attn_bwd(q, k, v, do, lse, cg, g):
    d, n, _ = q.shape
    nb = n // QBLK
    scale = HEAD_DIM ** -0.5

    def body(q_ref, kp_ref, kc_ref, vp_ref, vc_ref, do_ref, l_ref, c_ref, dq_ref, dk_ref, dv_ref, ck, cv,
             s_scr, dp_scr, p_scr, ds_scr):
        i = pl.program_id(1)

        @pl.when(i == 0)
        def _():
            ck[...] = jnp.zeros_like(ck)
            cv[...] = jnp.zeros_like(cv)

        @pl.when(i < nb)
        def _():
            mask = _attn_mask(i)
            first = _first_head_lanes()
            for pr in range(A_HEADS // 2):
                ps = slice(pr * LANES, (pr + 1) * LANES)
                kc = jnp.concatenate([kp_ref[:, ps], kc_ref[:, ps]], axis=0)
                vc = jnp.concatenate([vp_ref[:, ps], vc_ref[:, ps]], axis=0)
                qs = _split_heads(q_ref[:, ps], first)
                dos = _split_heads(do_ref[:, ps], first)
                for e in range(2):
                    s_scr[2 * pr + e] = lax.dot_general(qs[e], kc, NT_DIMS, preferred_element_type=F32)
                    dp_scr[2 * pr + e] = lax.dot_general(dos[e], vc, NT_DIMS, preferred_element_type=F32)
            for h in range(A_HEADS):
                p = jnp.where(mask, jnp.exp(s_scr[h] * scale - l_ref[:, h:h + 1]), 0.0)
                p_scr[h] = p.astype(BF16)
                ds_scr[h] = (p * (dp_scr[h] + c_ref[:, h:h + 1]) * scale).astype(BF16)
            for pr in range(A_HEADS // 2):
                ps = slice(pr * LANES, (pr + 1) * LANES)
                ks = _split_heads(jnp.concatenate([kp_ref[:, ps], kc_ref[:, ps]], axis=0), first)
                qs = _split_heads(q_ref[:, ps], first)
                dos = _split_heads(do_ref[:, ps], first)
                dq = dkc = dvc = None
                for e in range(2):
                    ds = ds_scr[2 * pr + e]
                    a = jnp.dot(ds, ks[e], preferred_element_type=F32)
                    b = lax.dot_general(ds, qs[e], TN_DIMS, preferred_element_type=F32)
                    c = lax.dot_general(p_scr[2 * pr + e], dos[e], TN_DIMS, preferred_element_type=F32)
                    dq, dkc, dvc = (a, b, c) if e == 0 else (dq + a, dkc + b, dvc + c)
                dq_ref[:, ps] = dq.astype(BF16)
                dk_ref[:, ps] = (ck[:, ps] + dkc[:QBLK]).astype(BF16)
                dv_ref[:, ps] = (cv[:, ps] + dvc[:QBLK]).astype(BF16)
                ck[:, ps] = dkc[QBLK:]
                cv[:, ps] = dvc[QBLK:]

        @pl.when(i == nb)
        def _():
            dk_ref[...] = ck[...].astype(BF16)
            dv_ref[...] = cv[...].astype(BF16)

    qi = lambda i: jnp.minimum(i, nb - 1)
    cur = lambda r, i: (r, qi(i), 0)
    prev = lambda r, i: (r, jnp.maximum(qi(i) - 1, 0), 0)
    late = lambda r, i: (r, jnp.maximum(i - 1, 0), 0)
    return _call(
        body, "attn_bwd_g%d" % g, (d, nb + 1),
        [pl.BlockSpec(ATT_BLK, cur), pl.BlockSpec(ATT_BLK, prev), pl.BlockSpec(ATT_BLK, cur),
         pl.BlockSpec(ATT_BLK, prev), pl.BlockSpec(ATT_BLK, cur),
         pl.BlockSpec(ATT_BLK, cur), pl.BlockSpec(ATT_CBLK, cur), pl.BlockSpec(ATT_CBLK, cur)],
        [pl.BlockSpec(ATT_BLK, cur), pl.BlockSpec(ATT_BLK, late), pl.BlockSpec(ATT_BLK, late)],
        [jax.ShapeDtypeStruct((d, n, A_WIDTH), BF16)] * 3,
        scratch=[pltpu.VMEM((QBLK, A_WIDTH), F32), pltpu.VMEM((QBLK, A_WIDTH), F32),
                 pltpu.VMEM((A_HEADS, QBLK, 2 * QBLK), F32), pltpu.VMEM((A_HEADS, QBLK, 2 * QBLK), F32),
                 pltpu.VMEM((A_HEADS, QBLK, 2 * QBLK), BF16), pltpu.VMEM((A_HEADS, QBLK, 2 * QBLK), BF16)],
        sem=("parallel", "arbitrary"))(q, k, k, v, v, do, lse, cg)


def _merge_weights(l0, l1, l2):
    mx = jnp.maximum(jnp.maximum(l0, l1), l2)
    e0, e1, e2 = jnp.exp(l0 - mx), jnp.exp(l1 - mx), jnp.exp(l2 - mx)
    den = e0 + e1 + e2
    return e0 / den, e1 / den, e2 / den


def _even_specs(T, S):
    t8 = T // SUBLANES
    last8 = S // SUBLANES - 1
    col = lambda c: pl.BlockSpec((T, A_WIDTH), lambda i: (i, c))
    prev8 = lambda c: pl.BlockSpec((SUBLANES, A_WIDTH), lambda i: (jnp.maximum(i * t8 - 1, 0), c))
    next8 = lambda c: pl.BlockSpec((SUBLANES, A_WIDTH), lambda i: (jnp.minimum((i + 1) * t8, last8), c))
    return col, prev8, next8


GROUP_D = tuple(d for _, d in A_GROUPS)


def _even_mixer_fwd(x, proj, os_, ls_, conv_w, w_out, T):
    S = proj.shape[0]
    col, prev8, _ = _even_specs(T, S)
    H = SUBLANES

    def body(x_ref, w_ref, bg_r, cg_r, hb_r, zl_r, zh_r, cgp_r, hbp_r, o0, o1, o2, l0, l1, l2, cw_r,
             x1_ref, ut_ref, ext, cscr, *scr):
        i = pl.program_id(0)
        ls = [_compact_to_tokens(r, cscr, GROUP_D[g], T) for g, r in enumerate((l0, l1, l2))]
        expand = _head_expander()
        ws = [_segsum(w, expand) for w in _merge_weights(*ls)]
        oa = ws[0] * _to_tokens(o0, scr[0], GROUP_D[0], T)
        oa = oa + ws[1] * _to_tokens(o1, scr[1], GROUP_D[1], T)
        oa = oa + ws[2] * _to_tokens(o2, scr[2], GROUP_D[2], T)
        ext[0:H, :] = jnp.where(i == 0, 0.0, cgp_r[...] * hbp_r[...])
        ext[H:H + T, :] = cg_r[...] * hb_r[...]
        conv = cw_r[0:1, :] * ext[H - 2:H - 2 + T, :]
        for kk in range(1, SC_WIDTH):
            conv = conv + cw_r[kk:kk + 1, :] * ext[H - 2 + kk:H - 2 + kk + T, :]
        zl, zh = zl_r[...], zh_r[...]
        x1_ref[...] = x_ref[...] + _out_projection(ut_ref, w_ref, oa * (zl * _sig(zl)),
                                                   bg_r[...] * conv * (zh * _sig(zh)))

    streams = [_stream_spec(d, T) for d in GROUP_D]
    compacts = [_compact_spec(d, T) for d in GROUP_D]
    row = pl.BlockSpec((T, D_MODEL), lambda i: (i, 0))
    return _call(
        body, "even_mixer_fwd", (S // T,),
        [row, pl.BlockSpec((D_MODEL, D_MODEL), lambda i: (0, 0)),
         col(9), col(10), col(11), col(12), col(13), prev8(10), prev8(11)] + streams + compacts +
        [pl.BlockSpec((SC_WIDTH, A_WIDTH), lambda i: (0, 0))],
        [row, pl.BlockSpec((D_MODEL, T), lambda i: (0, i))],
        [jax.ShapeDtypeStruct((S, D_MODEL), F32), jax.ShapeDtypeStruct((D_MODEL, S), BF16)],
        scratch=[pltpu.VMEM((T + H, A_WIDTH), F32), pltpu.VMEM((T, LANES), F32)] + [_chunked(T)] * 3,
        sem=("parallel",))(
            x, w_out, proj, proj, proj, proj, proj, proj, proj, *os_, *ls_, conv_w)


def _even_mixer_bwd(dy, w_out, proj, os_, ls_, conv_w, T):
    S = proj.shape[0]
    nt = S // T
    col, prev8, next8 = _even_specs(T, S)
    H = SUBLANES
    t8 = T // SUBLANES
    last8 = S // SUBLANES - 1

    def body(dy_r, dyn_r, w_ref, bg_r, cg_r, hb_r, zl_r, zh_r, cgp_r, hbp_r, zhn_r, bgn_r,
             o0, o1, o2, l0, l1, l2, cw_r,
             do0, do1, do2, c0, c1, c2, dr_ref, dcw_ref, ext_t, ext_d, acc, cscr, s_a, s_b, s_c):
        i = pl.program_id(0)

        @pl.when(i == 0)
        def _():
            acc[...] = jnp.zeros_like(acc)

        zl, zh = zl_r[...], zh_r[...]
        sl, sh = _sig(zl), _sig(zh)
        du = lax.dot_general(dy_r[...].astype(BF16), w_ref[...], NT_DIMS, preferred_element_type=F32)
        dul, duh = du[:, 0:A_WIDTH], du[:, A_WIDTH:]
        dun = lax.dot_general(dyn_r[...].astype(BF16), w_ref[A_WIDTH:, :], NT_DIMS, preferred_element_type=F32)
        scr = (s_a, s_b, s_c)
        ls = [_compact_to_tokens(r, cscr, GROUP_D[g], T) for g, r in enumerate((l0, l1, l2))]
        wcs = _merge_weights(*ls)
        expand = _head_expander()
        ws = [_segsum(w, expand) for w in wcs]
        oa = ws[0] * _to_tokens(o0, scr[0], GROUP_D[0], T)
        oa = oa + ws[1] * _to_tokens(o1, scr[1], GROUP_D[1], T)
        oa = oa + ws[2] * _to_tokens(o2, scr[2], GROUP_D[2], T)
        doa = dul * (zl * sl)
        rsum = _segsum(doa * oa, _head_reducer())
        for g, (do_ref, c_ref) in enumerate(((do0, c0), (do1, c1), (do2, c2))):
            d = GROUP_D[g]
            _compact_from_tokens(c_ref, cscr, -wcs[g] * rsum, d, T)
            if d == 1:
                do_ref[0] = (ws[g] * doa).astype(BF16)
            else:
                _put(s_c, ws[g] * doa)
                _from_tokens(do_ref, s_c, d, T)
        cgv, hbv, bgv = cg_r[...], hb_r[...], bg_r[...]
        ext_t[0:H, :] = jnp.where(i == 0, 0.0, cgp_r[...] * hbp_r[...])
        ext_t[H:H + T, :] = cgv * hbv
        conv = cw_r[0:1, :] * ext_t[H - 2:H - 2 + T, :]
        for kk in range(1, SC_WIDTH):
            conv = conv + cw_r[kk:kk + 1, :] * ext_t[H - 2 + kk:H - 2 + kk + T, :]
        dyb = duh * (zh * sh)
        dconv = dyb * bgv
        zn = zhn_r[...]
        ext_d[0:T, :] = dconv
        ext_d[T:T + H, :] = jnp.where(i == nt - 1, 0.0, dun * (zn * _sig(zn)) * bgn_r[...])
        dt = cw_r[0:1, :] * ext_d[2:2 + T, :]
        for kk in range(1, SC_WIDTH):
            dt = dt + cw_r[kk:kk + 1, :] * ext_d[2 - kk:2 - kk + T, :]
        for kk in range(SC_WIDTH):
            acc[kk * SUBLANES:(kk + 1) * SUBLANES, :] += _cs8(dconv * ext_t[H - 2 + kk:H - 2 + kk + T, :])
        dr_ref[:, 0:A_WIDTH] = (dyb * conv).astype(BF16)
        dr_ref[:, A_WIDTH:2 * A_WIDTH] = (dt * hbv).astype(BF16)
        dr_ref[:, 2 * A_WIDTH:3 * A_WIDTH] = (dt * cgv).astype(BF16)
        dr_ref[:, 3 * A_WIDTH:4 * A_WIDTH] = (dul * oa * _dsilu(zl, sl)).astype(BF16)
        dr_ref[:, 4 * A_WIDTH:5 * A_WIDTH] = (duh * (bgv * conv) * _dsilu(zh, sh)).astype(BF16)

        @pl.when(i == nt - 1)
        def _():
            for kk in range(SC_WIDTH):
                dcw_ref[kk:kk + 1, :] = jnp.sum(acc[kk * SUBLANES:(kk + 1) * SUBLANES, :], axis=0, keepdims=True)

    streams = [_stream_spec(d, T) for d in GROUP_D]
    dynext = pl.BlockSpec((SUBLANES, D_MODEL), lambda i: (jnp.minimum((i + 1) * t8, last8), 0))
    compacts = [_compact_spec(d, T) for d in GROUP_D]
    outs = _call(
        body, "even_mixer_bwd", (nt,),
        [pl.BlockSpec((T, D_MODEL), lambda i: (i, 0)), dynext, pl.BlockSpec((D_MODEL, D_MODEL), lambda i: (0, 0)),
         col(9), col(10), col(11), col(12), col(13), prev8(10), prev8(11), next8(13), next8(9)] +
        streams + compacts + [pl.BlockSpec((SC_WIDTH, A_WIDTH), lambda i: (0, 0))],
        streams + compacts + [pl.BlockSpec((T, 5 * A_WIDTH), lambda i: (i, 0)),
                              pl.BlockSpec((SC_WIDTH, A_WIDTH), lambda i: (0, 0))],
        [_stream_shape(d, S, BF16) for d in GROUP_D] + [_compact_shape(d, S) for d in GROUP_D] +
        [jax.ShapeDtypeStruct((S, 5 * A_WIDTH), BF16), jax.ShapeDtypeStruct((SC_WIDTH, A_WIDTH), F32)],
        scratch=[pltpu.VMEM((T + H, A_WIDTH), F32), pltpu.VMEM((T + H, A_WIDTH), F32),
                 pltpu.VMEM((SC_WIDTH * SUBLANES, A_WIDTH), F32), pltpu.VMEM((T, LANES), F32)] +
                [_chunked(T)] * 3,
        sem=("arbitrary",))(dy, dy, w_out, proj, proj, proj, proj, proj, proj, proj, proj, proj, *os_, *ls_, conv_w)
    return outs[0:3], outs[3:6], outs[6], outs[7]


def _qk_bwd(proj, dqs, dks, dvs, rope, wq, wk, T):
    S = proj.shape[0]
    nt = S // T
    qk_w = 3 * A_WIDTH

    def body(q_ref, k_ref, dq0, dq1, dq2, dk0, dk1, dk2, dv0, dv1, dv2, c_ref, s1_ref, s2_ref, wq_ref, wk_ref,
             o_ref, dw_ref, acc, scr):
        i = pl.program_id(0)

        @pl.when(i == 0)
        def _():
            acc[...] = jnp.zeros_like(acc)
            dw_ref[...] = jnp.zeros_like(dw_ref)

        seg_mean = _seg_mean()
        c, s1, s2 = c_ref[...], s1_ref[...], s2_ref[...]
        for t, (src, w_ref, ds) in enumerate(((q_ref, wq_ref, (dq0, dq1, dq2)), (k_ref, wk_ref, (dk0, dk1, dk2)))):
            wv = w_ref[...]
            for g in range(3):
                d = GROUP_D[g]
                if d > 1:
                    _to_tokens(ds[g], scr, d, T)
                for ch in range(A_WIDTH // LANES):
                    cs = slice(g * A_WIDTH + ch * LANES, g * A_WIDTH + (ch + 1) * LANES)
                    lc = slice(ch * LANES, (ch + 1) * LANES)
                    v = src[:, cs]
                    dout = ds[g][0, :, lc].astype(F32) if d == 1 else scr[ch]
                    rs = lax.rsqrt(_segsum(v * v, seg_mean) + EPS)
                    xh = v * rs
                    dy = dout * c + pltpu.roll(dout * s1, LANES - 8, 1) + pltpu.roll(dout * s2, 8, 1)
                    acc[t * SUBLANES:(t + 1) * SUBLANES, :] += _cs8(dy * xh)
                    dxh = dy * wv
                    mean = _segsum(dxh * xh, seg_mean)
                    o_ref[:, t * qk_w + g * A_WIDTH + ch * LANES: t * qk_w + g * A_WIDTH + (ch + 1) * LANES] = (
                        rs * (dxh - xh * mean)).astype(BF16)
        for g, dv in enumerate((dv0, dv1, dv2)):
            d = GROUP_D[g]
            base = 2 * qk_w + g * A_WIDTH
            o_ref[:, base:base + A_WIDTH] = _to_tokens(dv, scr, d, T).astype(BF16)

        @pl.when(i == nt - 1)
        def _():
            for t in range(2):
                srow = jnp.sum(acc[t * SUBLANES:(t + 1) * SUBLANES, :], axis=0, keepdims=True)
                dw_ref[t:t + 1, :] = srow + pltpu.roll(srow, HEAD_DIM, 1)

    streams = [_stream_spec(d, T) for d in GROUP_D]
    return _call(
        body, "qk_bwd", (nt,),
        [pl.BlockSpec((T, qk_w), lambda i: (i, 0)), pl.BlockSpec((T, qk_w), lambda i: (i, 1))] + streams * 3 +
        [pl.BlockSpec((T, LANES), lambda i: (i, 0))] * 3 +
        [pl.BlockSpec((1, LANES), lambda i: (0, 0)), pl.BlockSpec((1, LANES), lambda i: (0, 0))],
        [pl.BlockSpec((T, 3 * qk_w), lambda i: (i, 0)), pl.BlockSpec((SUBLANES, LANES), lambda i: (0, 0))],
        [jax.ShapeDtypeStruct((S, 3 * qk_w), BF16), jax.ShapeDtypeStruct((SUBLANES, LANES), F32)],
        scratch=[pltpu.VMEM((2 * SUBLANES, LANES), F32), _chunked(T)], sem=("arbitrary",))(
            proj, proj, *dqs, *dks, *dvs, *rope, wq, wk)


N_SMALL_ODD = 40
SHIFT_ROWS_LESS = SUBLANES


def _fill_shifted(ext_ref, sh_ref):
    rows = ext_ref.shape[0] - SHIFT_ROWS_LESS
    for b in range(1, SUBLANES):
        sh_ref[b - 1] = ext_ref[b:b + rows, :]


def _window(ext_ref, sh_ref, off, T):
    a, b = divmod(off, SUBLANES)
    if b == 0:
        return ext_ref[off:off + T, :]
    return sh_ref[b - 1, a * SUBLANES:a * SUBLANES + T, :]


def _odd_pool_tile(i, uc_r, ucp_r, pw_r, ext_u, pooled_s, pm_s, T):
    H = HALO
    uc = uc_r[...]
    ext_u[0:H, :] = jnp.where(i == 0, 0.0, ucp_r[...])
    ext_u[H:H + T, :] = uc
    row = i * T + lax.broadcasted_iota(jnp.int32, (T, 1), 0)
    for g, p in enumerate(POOL_SIZES):
        cs = slice(g * LANES, (g + 1) * LANES)
        win = ext_u[H:H + T, cs]
        for j in range(1, p):
            win = win + ext_u[H - j:H - j + T, cs]
        cnt = jnp.minimum(row + 1, p).astype(F32)
        pooled = win / cnt - uc[:, cs]
        pooled_s[:, cs] = pooled
        pm_s[:, cs] = jnp.dot(pooled.astype(BF16), pw_r[g].astype(BF16), preferred_element_type=F32)
    return row


def _odd_glu_tile(i, da_r, dg_r, dap_r, dgp_r, ext_g, sh_g, T):
    H = HALO
    ext_g[0:H, :] = jnp.where(i == 0, 0.0, dap_r[...] * _sig(dgp_r[...]))
    ext_g[H:H + T, :] = da_r[...] * _sig(dg_r[...])
    _fill_shifted(ext_g, sh_g)


def _odd_specs(T, S, order):
    tb = T // HALO
    col = lambda c: pl.BlockSpec((T, A_WIDTH), lambda s: (order(s), c))
    prev = lambda c: pl.BlockSpec((HALO, A_WIDTH), lambda s: (jnp.maximum(order(s) * tb - 1, 0), c))
    const2 = lambda shape: pl.BlockSpec(shape, lambda s: (0, 0))
    weights = [pl.BlockSpec((4, LANES, LANES), lambda s: (0, 0, 0)), const2((1, A_WIDTH)),
               const2((D_CONV, A_WIDTH)), const2((1, A_WIDTH)), const2((1, A_WIDTH)), const2((1, A_WIDTH))]
    return col, prev, weights


def _odd_mixer_fwd(x, tgt, proj, pool_w, scale, dconv_w, dconv_b, ln_w, ln_b, w_out, T):
    S = proj.shape[0]
    nt = S // T
    col, prev, wspecs = _odd_specs(T, S, lambda s: s)
    H = HALO

    def body(x_ref, t_ref, w_ref, uc_r, da_r, dg_r, zl_r, zh_r, ucp_r, dap_r, dgp_r, pw_r, sc_r, dw_r, db_r,
             lw_r, lb_r, dy_ref, l_ref, ut_ref, cv_ref, ext_u, ext_g, sh_g, pooled_s, pm_s, lacc):
        i = pl.program_id(0)

        @pl.when(i == 0)
        def _():
            lacc[...] = jnp.zeros_like(lacc)

        _odd_pool_tile(i, uc_r, ucp_r, pw_r, ext_u, pooled_s, pm_s, T)
        _odd_glu_tile(i, da_r, dg_r, dap_r, dgp_r, ext_g, sh_g, T)
        base = H - (D_CONV - 1)
        conv = db_r[...] + dw_r[0:1, :] * _window(ext_g, sh_g, base, T)
        for kk in range(1, D_CONV):
            conv = conv + dw_r[kk:kk + 1, :] * _window(ext_g, sh_g, base + kk, T)
        cv_ref[...] = conv
        mu = jnp.mean(conv, axis=-1, keepdims=True)
        xc = conv - mu
        yh = xc * lax.rsqrt(jnp.mean(xc * xc, axis=-1, keepdims=True) + EPS)
        ln = yh * lw_r[...] + lb_r[...]
        zl, zh = zl_r[...], zh_r[...]
        y = x_ref[...] + _out_projection(ut_ref, w_ref, pm_s[...] * sc_r[...] * (zl * _sig(zl)),
                                         ln * _sig(ln) * (zh * _sig(zh)))
        diff = y - t_ref[...]
        dy_ref[...] = diff / float(D_MODEL)
        lacc[...] += _cs8(diff * diff)

        @pl.when(i == nt - 1)
        def _():
            l_ref[...] = jnp.sum(lacc[...], axis=0, keepdims=True)

    row = pl.BlockSpec((T, D_MODEL), lambda i: (i, 0))
    return _call(
        body, "odd_mixer_fwd", (nt,),
        [row, row, pl.BlockSpec((D_MODEL, D_MODEL), lambda i: (0, 0)),
         col(0), col(1), col(2), col(3), col(4), prev(0), prev(1), prev(2)] + wspecs,
        [row, pl.BlockSpec((1, D_MODEL), lambda i: (0, 0)), pl.BlockSpec((D_MODEL, T), lambda i: (0, i)),
         pl.BlockSpec((T, A_WIDTH), lambda i: (i, 0))],
        [jax.ShapeDtypeStruct((S, D_MODEL), F32), jax.ShapeDtypeStruct((1, D_MODEL), F32),
         jax.ShapeDtypeStruct((D_MODEL, S), BF16), jax.ShapeDtypeStruct((S, A_WIDTH), F32)],
        scratch=[pltpu.VMEM((T + H, A_WIDTH), F32), pltpu.VMEM((T + H, A_WIDTH), F32),
                 pltpu.VMEM((SUBLANES - 1, T + H - SHIFT_ROWS_LESS, A_WIDTH), F32),
                 pltpu.VMEM((T, A_WIDTH), F32), pltpu.VMEM((T, A_WIDTH), F32),
                 pltpu.VMEM((SUBLANES, D_MODEL), F32)],
        sem=("arbitrary",))(x, tgt, w_out, proj, proj, proj, proj, proj, proj, proj, proj,
                            pool_w, scale, dconv_w, dconv_b, ln_w, ln_b)


def _odd_mixer_bwd(dy, w_out, proj, conv, pool_w, scale, dconv_w, dconv_b, ln_w, ln_b, T):
    S = proj.shape[0]
    nt = S // T
    order = lambda s: nt - 1 - s
    col, prev, wspecs = _odd_specs(T, S, order)
    H = HALO

    def body(dy_r, w_ref, cv_r, uc_r, da_r, dg_r, zl_r, zh_r, ucp_r, dap_r, dgp_r, pw_r, sc_r, dw_r, db_r, lw_r, lb_r,
             dp_ref, dpw_ref, sm_ref, ext_u, ext_g, sh_g, pooled_s, pm_s, dpl_s, ext_p, ext_c, sh_c, acc):
        step = pl.program_id(0)
        i = nt - 1 - step

        @pl.when(step == 0)
        def _():
            ext_p[T:T + H, :] = jnp.zeros((H, A_WIDTH), F32)
            ext_c[T:T + H, :] = jnp.zeros((H, A_WIDTH), F32)
            acc[...] = jnp.zeros_like(acc)
            dpw_ref[...] = jnp.zeros_like(dpw_ref)

        def accum(r, v):
            acc[r * SUBLANES:(r + 1) * SUBLANES, :] += _cs8(v)

        row = _odd_pool_tile(i, uc_r, ucp_r, pw_r, ext_u, pooled_s, pm_s, T)
        _odd_glu_tile(i, da_r, dg_r, dap_r, dgp_r, ext_g, sh_g, T)
        conv = cv_r[...]
        mu = jnp.mean(conv, axis=-1, keepdims=True)
        xc = conv - mu
        rstd = lax.rsqrt(jnp.mean(xc * xc, axis=-1, keepdims=True) + EPS)
        yh = xc * rstd
        ln = yh * lw_r[...] + lb_r[...]
        sln = _sig(ln)
        zl, zh = zl_r[...], zh_r[...]
        sl, sh = _sig(zl), _sig(zh)
        du = lax.dot_general(dy_r[...].astype(BF16), w_ref[...], NT_DIMS, preferred_element_type=F32)
        dul, duh = du[:, 0:A_WIDTH], du[:, A_WIDTH:]
        pm = pm_s[...]
        scv = sc_r[...]
        dyc = dul * (zl * sl)
        accum(34, dyc * pm)
        dpm = dyc * scv
        for g in range(len(POOL_SIZES)):
            cs = slice(g * LANES, (g + 1) * LANES)
            dpm_g = dpm[:, cs].astype(BF16)
            dpw_ref[g] += lax.dot_general(pooled_s[:, cs].astype(BF16), dpm_g, TN_DIMS, preferred_element_type=F32)
            dpl_s[:, cs] = lax.dot_general(dpm_g, pw_r[g].astype(BF16), NT_DIMS, preferred_element_type=F32)
        lane_p = lax.broadcasted_iota(jnp.int32, (1, A_WIDTH), 1) // LANES
        pvec = jnp.left_shift(2, lane_p)
        cnt = jnp.minimum(row + 1, pvec).astype(F32)
        dpl = dpl_s[...]
        ext_p[0:T, :] = dpl / cnt
        for g, p in enumerate(POOL_SIZES):
            cs = slice(g * LANES, (g + 1) * LANES)
            win = ext_p[0:T, cs]
            for j in range(1, p):
                win = win + ext_p[j:j + T, cs]
            dp_ref[:, cs] = (win - dpl[:, cs]).astype(BF16)
        ext_p[T:T + H, :] = ext_p[0:H, :]
        dln = duh * (zh * sh) * _dsilu(ln, sln)
        accum(32, dln * yh)
        accum(33, dln)
        dyh = dln * lw_r[...]
        dc = rstd * (dyh - jnp.mean(dyh, axis=-1, keepdims=True) - yh * jnp.mean(dyh * yh, axis=-1, keepdims=True))
        accum(31, dc)
        ext_c[0:T, :] = dc
        _fill_shifted(ext_c, sh_c)
        base = H - (D_CONV - 1)
        dgl = dw_r[0:1, :] * _window(ext_c, sh_c, D_CONV - 1, T)
        accum(0, dc * _window(ext_g, sh_g, base, T))
        for kk in range(1, D_CONV):
            dgl = dgl + dw_r[kk:kk + 1, :] * _window(ext_c, sh_c, D_CONV - 1 - kk, T)
            accum(kk, dc * _window(ext_g, sh_g, base + kk, T))
        ext_c[T:T + H, :] = ext_c[0:H, :]
        dav, dgv = da_r[...], dg_r[...]
        sg = _sig(dgv)
        dp_ref[:, A_WIDTH:2 * A_WIDTH] = (dgl * sg).astype(BF16)
        dp_ref[:, 2 * A_WIDTH:3 * A_WIDTH] = (dgl * dav * sg * (1.0 - sg)).astype(BF16)
        dp_ref[:, 3 * A_WIDTH:4 * A_WIDTH] = (dul * (pm * scv) * _dsilu(zl, sl)).astype(BF16)
        dp_ref[:, 4 * A_WIDTH:5 * A_WIDTH] = (duh * (ln * sln) * _dsilu(zh, sh)).astype(BF16)

        @pl.when(step == nt - 1)
        def _():
            for r in range(N_SMALL_ODD):
                sm_ref[r:r + 1, :] = jnp.sum(acc[r * SUBLANES:(r + 1) * SUBLANES, :], axis=0, keepdims=True)

    ext = pltpu.VMEM((T + H, A_WIDTH), F32)
    shifted = pltpu.VMEM((SUBLANES - 1, T + H - SHIFT_ROWS_LESS, A_WIDTH), F32)
    tile = pltpu.VMEM((T, A_WIDTH), F32)
    return _call(
        body, "odd_mixer_bwd", (nt,),
        [pl.BlockSpec((T, D_MODEL), lambda s: (order(s), 0)), pl.BlockSpec((D_MODEL, D_MODEL), lambda s: (0, 0)),
         pl.BlockSpec((T, A_WIDTH), lambda s: (order(s), 0)),
         col(0), col(1), col(2), col(3), col(4), prev(0), prev(1), prev(2)] + wspecs,
        [pl.BlockSpec((T, ODD_IN), lambda s: (order(s), 0)),
         pl.BlockSpec((4, LANES, LANES), lambda s: (0, 0, 0)),
         pl.BlockSpec((N_SMALL_ODD, A_WIDTH), lambda s: (0, 0))],
        [jax.ShapeDtypeStruct((S, ODD_IN), BF16), jax.ShapeDtypeStruct((4, LANES, LANES), F32),
         jax.ShapeDtypeStruct((N_SMALL_ODD, A_WIDTH), F32)],
        scratch=[ext, ext, shifted, tile, tile, tile, ext, ext, shifted,
                 pltpu.VMEM((N_SMALL_ODD * SUBLANES, A_WIDTH), F32)],
        sem=("arbitrary",))(dy, w_out, conv, proj, proj, proj, proj, proj, proj, proj, proj,
                            pool_w, scale, dconv_w, dconv_b, ln_w, ln_b)


TILE_SEQ = 256
TILE_WG = 256
TILE_FIRST = 512
TILE_MM = 512


SMALL_PACK = "small_pack"
SMALL_PACK_W = 2 * LANES
LATE_WEIGHTS = ("e_w_out", "o_w_in", "o_w_out", SMALL_PACK)
ODD_MATS = ("o_w_in", "o_w_out")
EVEN_MATS = ("e_w_in", "e_w_out")


def _reduce_start(names, grads, grads16, cidx):
    recv = _swap_to_sibling(names, [grads16[n] for n in names], "swap_halves_" + names[0][0], True)
    both = [_add_half(cidx, grads[n], r, n) for n, r in zip(names, recv)]
    return [h for h, _ in both], [hb for _, hb in both]


def _local_step(x, pos, tgt, shards, p, unpack_small, cidx, bidx):
    T = TILE_SEQ
    freq = _freq_table()
    wq = jnp.tile(p["e_q_norm_w"], (1, LANES // HEAD_DIM))
    wk = jnp.tile(p["e_k_norm_w"], (1, LANES // HEAD_DIM))

    proj_e, ht_e, w_e_in, late = _inproj_gathering(x, p["e_norm_w"], shards["e_w_in"], bidx, "e_w_in", LATE_WEIGHTS,
                                                   [shards[n] for n in LATE_WEIGHTS], TILE_FIRST, "inproj_even")
    wb = dict(zip(LATE_WEIGHTS, late), e_w_in=w_e_in)
    p = dict(p, **unpack_small(wb[SMALL_PACK]))
    qkv = _qkv_prep(proj_e, pos, freq, wq, wk, T)
    qs, ks, vs, rope = qkv[0:3], qkv[3:6], qkv[6:9], qkv[9:12]
    os_, ls_ = [], []
    for g in range(3):
        o, l = _attn_fwd(qs[g], ks[g], vs[g], g)
        os_.append(o)
        ls_.append(l)
    x1, ut_e = _even_mixer_fwd(x, proj_e, os_, ls_, p["e_conv_w"], wb["e_w_out"], T)
    proj_o, ht_o = _inproj(x1, p["o_norm_w"], wb["o_w_in"], TILE_MM, 1280, "inproj_odd")
    odd_w = (p["o_pool_w"], p["o_pool_scale"], p["o_dconv_w"], p["o_dconv_b"], p["o_ln_w"], p["o_ln_b"])
    dy, lsum, ut_o, conv_o = _odd_mixer_fwd(x1, tgt, proj_o, *odd_w, wb["o_w_out"], T)

    g, g16 = {}, {}
    g["o_w_out"], g16["o_w_out"] = _mm_wgrad(ut_o, [dy], TILE_WG, "wgrad_o_out")
    dproj_o, g["o_pool_w"], small_o = _odd_mixer_bwd(dy, wb["o_w_out"], proj_o, conv_o, *odd_w, T)
    g["o_w_in"], g16["o_w_in"] = _mm_wgrad(ht_o, [dproj_o], TILE_WG, "wgrad_o_in")
    half_o, half_o16 = _reduce_start(ODD_MATS, g, g16, cidx)
    dx1, g["o_norm_w"], blocks_o = _mm_nt_rms([dproj_o], wb["o_w_in"], x1, p["o_norm_w"], dy, TILE_MM, "dx_odd",
                                              ODD_MATS, half_o16)
    g["o_dconv_w"] = small_o[0:D_CONV]
    g["o_dconv_b"] = small_o[31:32]
    g["o_ln_w"] = small_o[32:33]
    g["o_ln_b"] = small_o[33:34]
    g["o_pool_scale"] = small_o[34:35]

    g["e_w_out"], g16["e_w_out"] = _mm_wgrad(ut_e, [dx1], TILE_WG, "wgrad_e_out")
    dos, cgs, drest, g["e_conv_w"] = _even_mixer_bwd(dx1, wb["e_w_out"], proj_e, os_, ls_, p["e_conv_w"], T)
    dqs, dks, dvs = [], [], []
    for gi in range(3):
        dq, dk, dv = _attn_bwd(qs[gi], ks[gi], vs[gi], dos[gi], ls_[gi], cgs[gi], gi)
        dqs.append(dq)
        dks.append(dk)
        dvs.append(dv)
    dqkv, dnw = _qk_bwd(proj_e, dqs, dks, dvs, rope, wq, wk, T)
    g["e_q_norm_w"] = dnw[0:1, 0:HEAD_DIM]
    g["e_k_norm_w"] = dnw[1:2, 0:HEAD_DIM]
    pieces = [dqkv, drest]
    g["e_w_in"], g16["e_w_in"] = _mm_wgrad(ht_e, pieces, TILE_WG, "wgrad_e_in")
    half_e, half_e16 = _reduce_start(EVEN_MATS, g, g16, cidx)
    dx, g["e_norm_w"], blocks_e = _mm_nt_rms(pieces, wb["e_w_in"], x, p["e_norm_w"], dx1, TILE_MM, "dx_even",
                                             EVEN_MATS, half_e16)
    parts = {}
    for names, halves, blocks in ((ODD_MATS, half_o, blocks_o), (EVEN_MATS, half_e, blocks_e)):
        for n, h, r in zip(names, halves, blocks):
            parts[n] = _add_blocks(bidx, h, r, n)
    return lsum, dx, g, parts


BIG = ("e_w_in", "e_w_out", "o_w_in", "o_w_out")
SHARD_AXIS = {"e_w_in": 1, "e_w_out": 0, "o_w_in": 1, "o_w_out": 0, SMALL_PACK: 0}
N_CHIPS = 4


def _place():
    x, y, c = lax.axis_index("x"), lax.axis_index("y"), lax.axis_index("c")
    chips = [(1 - x, y), (x, 1 - y), (1 - x, 1 - y)]
    return x, y, c, chips


def _block_of(ref, name, block):
    rows, cols = ref.shape
    if SHARD_AXIS[name] == 1:
        cw = cols // N_CHIPS
        return ref.at[:, pl.ds(pl.multiple_of(block * cw, LANES), cw)]
    rw = rows // N_CHIPS
    return ref.at[pl.ds(pl.multiple_of(block * rw, rw), rw), :]


def _half_of(ref, name, half):
    rows, cols = ref.shape
    if SHARD_AXIS[name] == 1:
        return ref.at[pl.ds(pl.multiple_of(half * (rows // 2), rows // 2), rows // 2), :]
    return ref.at[:, pl.ds(pl.multiple_of(half * (cols // 2), LANES), cols // 2)]


def _sub(ref, name, block, half):
    rows, cols = ref.shape
    if SHARD_AXIS[name] == 1:
        cw, hr = cols // N_CHIPS, rows // 2
        return ref.at[pl.ds(pl.multiple_of(half * hr, hr), hr), pl.ds(pl.multiple_of(block * cw, LANES), cw)]
    rw, hc = rows // N_CHIPS, cols // 2
    return ref.at[pl.ds(pl.multiple_of(block * rw, rw), rw), pl.ds(pl.multiple_of(half * hc, LANES), hc)]


GATHER_COPIES = 7


class _Gather:
    def __init__(self, names, s_refs, f_refs, send, recv):
        self.names, self.s, self.f, self.send, self.recv = names, s_refs, f_refs, send, recv

    def _copy(self, k, src, dst, to):
        return pltpu.make_async_remote_copy(src_ref=src, dst_ref=dst, send_sem=self.send.at[k],
                                            recv_sem=self.recv.at[k], device_id=to, device_id_type=MESH)

    def _plan(self):
        x, y, c, chips = _place()
        me, sib = 2 * x + y, (x, y, 1 - c)
        first, relay_in, relay, last_in = [], [], [], []
        for wi, n in enumerate(self.names):
            k0 = wi * GATHER_COPIES
            s, f = self.s[wi], self.f[wi]
            own = _block_of(f, n, me)
            first.append(self._copy(k0 + 3, s, own, sib))
            last_in.append(self._copy(k0 + 3, s, own, sib))
            for j, (cx, cy) in enumerate(chips):
                first.append(self._copy(k0 + j, _half_of(s, n, c), _sub(f, n, me, c), (cx, cy, c)))
                mine = _sub(f, n, 2 * cx + cy, c)
                relay_in.append(self._copy(k0 + j, mine, mine, sib))
                relay.append(self._copy(k0 + 4 + j, mine, mine, sib))
                theirs = _sub(f, n, 2 * cx + cy, 1 - c)
                last_in.append(self._copy(k0 + 4 + j, theirs, theirs, sib))
        return first, relay_in, relay, last_in

    N_RELATIONS = 3

    def begin(self, relations=(0, 1, 2), sibling=True):
        first = self._plan()[0]
        for wi in range(len(self.names)):
            mine = first[wi * (1 + self.N_RELATIONS):(wi + 1) * (1 + self.N_RELATIONS)]
            if sibling:
                mine[0].start()
            for j in relations:
                mine[1 + j].start()

    def relay(self, relations=(0, 1, 2)):
        _, relay_in, relay, _ = self._plan()
        for wi in range(len(self.names)):
            for j in relations:
                relay_in[wi * self.N_RELATIONS + j].wait_recv()
                relay[wi * self.N_RELATIONS + j].start()

    def end(self):
        first, _, relay, last_in = self._plan()
        for cp in last_in:
            cp.wait_recv()
        for cp in first + relay:
            cp.wait_send()

    def wait_relayed(self, j):
        self._plan()[3][1 + j].wait_recv()

    def end_rest(self):
        first, _, relay, last_in = self._plan()
        last_in[0].wait_recv()
        for cp in first + relay:
            cp.wait_send()


def _full_shape(n, s):
    r, cdim = s.shape
    return jax.ShapeDtypeStruct((r, cdim * N_CHIPS) if SHARD_AXIS[n] == 1 else (r * N_CHIPS, cdim), s.dtype)


def _gather_sems(names):
    k = GATHER_COPIES * len(names)
    return [pltpu.SemaphoreType.DMA((k,)), pltpu.SemaphoreType.DMA((k,))]


def _scatter_copies(names, h_refs, r_refs, send, recv):
    _, _, c, chips = _place()
    cps = []
    for wi, n in enumerate(names):
        for j, (cx, cy) in enumerate(chips):
            cps.append(pltpu.make_async_remote_copy(
                src_ref=_block_of(h_refs[wi], n, 2 * cx + cy), dst_ref=r_refs[wi].at[j],
                send_sem=send.at[wi * 3 + j], recv_sem=recv.at[wi * 3 + j],
                device_id=(cx, cy, c), device_id_type=MESH))
    return cps


def _scatter_sems(names):
    return [pltpu.SemaphoreType.DMA((3 * len(names),)), pltpu.SemaphoreType.DMA((3 * len(names),))]


class _SmallSum:
    def __init__(self, p_ref, o_ref, sbuf, cbuf, send, recv):
        self.p, self.o, self.sbuf, self.cbuf, self.send, self.recv = p_ref, o_ref, sbuf, cbuf, send, recv

    def _copy(self, k, ref, to):
        return pltpu.make_async_remote_copy(src_ref=ref, dst_ref=ref, send_sem=self.send.at[k],
                                            recv_sem=self.recv.at[k], device_id=to, device_id_type=MESH)

    def _plan(self):
        x, y, c, chips = _place()
        me, sib = 2 * x + y, (x, y, 1 - c)
        d2d_out = self._copy(0, self.sbuf.at[c], sib)
        d2d_in = self._copy(0, self.sbuf.at[1 - c], sib)
        ici_out = [self._copy(1 + j, self.cbuf.at[me], (cx, cy, c)) for j, (cx, cy) in enumerate(chips)]
        ici_in = [self._copy(1 + j, self.cbuf.at[2 * cx + cy], (cx, cy, c)) for j, (cx, cy) in enumerate(chips)]
        return c, me, d2d_out, d2d_in, ici_out, ici_in

    def begin(self):
        c, _, d2d_out, _, _, _ = self._plan()
        self.sbuf[c] = self.p[...]
        d2d_out.start()

    def middle(self):
        _, me, _, d2d_in, ici_out, _ = self._plan()
        d2d_in.wait_recv()
        self.cbuf[me] = self.sbuf[0] + self.sbuf[1]
        for cp in ici_out:
            cp.start()

    def end(self):
        _, _, d2d_out, _, ici_out, ici_in = self._plan()
        for cp in ici_in:
            cp.wait_recv()
        self.o[...] = (self.cbuf[0] + self.cbuf[1]) + (self.cbuf[2] + self.cbuf[3])
        for cp in [d2d_out] + ici_out:
            cp.wait_send()


def _small_sum_scratch(R):
    return [pltpu.VMEM((2, R, LANES), F32), pltpu.VMEM((N_CHIPS, R, LANES), F32),
            pltpu.SemaphoreType.DMA((4,)), pltpu.SemaphoreType.DMA((4,))]


def _half_shape(shape, name):
    r, cdim = shape
    return (r // 2, cdim) if SHARD_AXIS[name] == 1 else (r, cdim // 2)


def _shard_shape(shape, name):
    r, cdim = shape
    return (r, cdim // N_CHIPS) if SHARD_AXIS[name] == 1 else (r // N_CHIPS, cdim)


def _swap_to_sibling(names, srcs, name, pick_half, small=None):
    nw = len(names)
    ns = 0 if small is None else 1
    vm = pl.BlockSpec(memory_space=pltpu.VMEM)

    def body(*refs):
        g_refs = refs[:nw]
        r_refs = refs[nw + ns:2 * nw + ns]
        send, recv = refs[2 * nw + 2 * ns:2 * nw + 2 * ns + 2]
        x, y, c, _ = _place()
        sib = (x, y, 1 - c)
        cps = []
        for wi, n in enumerate(names):
            src = _half_of(g_refs[wi], n, 1 - c) if pick_half else g_refs[wi]
            cp = pltpu.make_async_remote_copy(src_ref=src, dst_ref=r_refs[wi], send_sem=send.at[wi],
                                              recv_sem=recv.at[wi], device_id=sib, device_id_type=MESH)
            cp.start()
            cps.append(cp)
        if ns:
            total = _SmallSum(refs[nw], refs[2 * nw + ns], *refs[2 * nw + 2 * ns + 2:])
            total.begin()
            total.middle()
            total.end()
        for cp in cps:
            cp.wait()

    outs = [jax.ShapeDtypeStruct(_half_shape(g.shape, n) if pick_half else g.shape, g.dtype)
            for n, g in zip(names, srcs)]
    return pl.pallas_call(
        body, name=name, in_specs=[ANY] * nw + [vm] * ns, out_specs=[ANY] * nw + [vm] * ns,
        out_shape=outs + ([jax.ShapeDtypeStruct(small.shape, F32)] if ns else []),
        scratch_shapes=[pltpu.SemaphoreType.DMA((nw,)), pltpu.SemaphoreType.DMA((nw,))] +
        (_small_sum_scratch(small.shape[0]) if ns else []),
    )(*srcs, *([small] if ns else []))


def _add_half(cidx, g, r, name):
    rows, cols = r.shape
    tr = 256
    tc = cols if cols <= 1792 else (1792 if cols % 1792 == 0 else 1280)
    nr, nc = rows // tr, cols // tc

    def body(c_ref, g_ref, r_ref, o_ref, ob_ref):
        s = g_ref[...] + r_ref[...].astype(F32)
        o_ref[...] = s
        ob_ref[...] = s.astype(BF16)

    if SHARD_AXIS[name] == 1:
        gmap = lambda i, j, c_ref: (c_ref[0] * nr + i, j)
    else:
        gmap = lambda i, j, c_ref: (i, c_ref[0] * nc + j)
    same = lambda i, j, c_ref: (i, j)
    return pl.pallas_call(
        body, name="add_half_" + name,
        grid_spec=pltpu.PrefetchScalarGridSpec(
            num_scalar_prefetch=1, grid=(nr, nc),
            in_specs=[pl.BlockSpec((tr, tc), gmap), pl.BlockSpec((tr, tc), same)],
            out_specs=[pl.BlockSpec((tr, tc), same), pl.BlockSpec((tr, tc), same)]),
        out_shape=[jax.ShapeDtypeStruct(r.shape, F32), jax.ShapeDtypeStruct(r.shape, BF16)],
        compiler_params=pltpu.CompilerParams(dimension_semantics=("parallel", "parallel"), vmem_limit_bytes=VMEM_LIMIT),
    )(cidx, g, r)


def _add_blocks(bidx, h, r, name):
    _, rows, cols = r.shape
    tr = min(rows, 256)
    nr = rows // tr

    def body(b_ref, h_ref, r0, r1, r2, o_ref):
        o_ref[...] = ((h_ref[...] + r0[0].astype(F32)) + r1[0].astype(F32)) + r2[0].astype(F32)

    if SHARD_AXIS[name] == 1:
        hmap = lambda i, b_ref: (i, b_ref[0])
    else:
        hmap = lambda i, b_ref: (b_ref[0] * nr + i, 0)
    rspec = lambda j: pl.BlockSpec((1, tr, cols), lambda i, b_ref, j=j: (j, i, 0))
    return pl.pallas_call(
        body, name="add_blocks_" + name,
        grid_spec=pltpu.PrefetchScalarGridSpec(
            num_scalar_prefetch=1, grid=(nr,),
            in_specs=[pl.BlockSpec((tr, cols), hmap), rspec(0), rspec(1), rspec(2)],
            out_specs=pl.BlockSpec((tr, cols), lambda i, b_ref: (i, 0))),
        out_shape=jax.ShapeDtypeStruct((rows, cols), F32),
        compiler_params=pltpu.CompilerParams(dimension_semantics=("parallel",), vmem_limit_bytes=VMEM_LIMIT),
    )(bidx, h, r, r, r)


def _adam_math(w, g, m, v):
    c1 = 1.0 - ADAM_B1 ** ADAM_STEP
    c2 = 1.0 - ADAM_B2 ** ADAM_STEP
    nm = ADAM_B1 * m + (1.0 - ADAM_B1) * g
    nv = ADAM_B2 * v + (1.0 - ADAM_B2) * (g * g)
    delta = -ADAM_LR * ((nm / c1) / (jnp.sqrt(nv / c2) + ADAM_EPS) + ADAM_WD * w)
    return delta, nm, nv


def _adamw(w, g, m, v, name):
    def body(w_ref, g_ref, m_ref, v_ref, d_ref, nm_ref, nv_ref):
        d_ref[...], nm_ref[...], nv_ref[...] = _adam_math(w_ref[...], g_ref[...], m_ref[...], v_ref[...])

    spec = pl.BlockSpec(w.shape, lambda i: (0, 0))
    return _call(body, "adamw_" + name, (1,), [spec] * 4, [spec] * 3,
                 [jax.ShapeDtypeStruct(w.shape, F32)] * 3, sem=("arbitrary",))(w, g, m, v)


def _adamw_halves(cidx, w, mine, theirs, m, v, name):
    hr, hc = mine.shape
    tr = 128
    ni = hr // tr
    if SHARD_AXIS[name] == 1:
        wmap = lambda hh, i, c_ref: (hh * ni + i, 0)
    else:
        wmap = lambda hh, i, c_ref: (i, hh)
    hmap = lambda hh, i, c_ref: (i, 0)

    def body(c_ref, w_ref, a_ref, b_ref, m_ref, v_ref, g_ref, d_ref, nm_ref, nv_ref):
        g = jnp.where(pl.program_id(0) == c_ref[0], a_ref[...], b_ref[...])
        g_ref[...] = g
        d_ref[...], nm_ref[...], nv_ref[...] = _adam_math(w_ref[...], g, m_ref[...], v_ref[...])

    wspec = pl.BlockSpec((tr, hc), wmap)
    hspec = pl.BlockSpec((tr, hc), hmap)
    return pl.pallas_call(
        body, name="adamw_" + name,
        grid_spec=pltpu.PrefetchScalarGridSpec(
            num_scalar_prefetch=1, grid=(2, ni),
            in_specs=[wspec, hspec, hspec, wspec, wspec], out_specs=[wspec] * 4),
        out_shape=[jax.ShapeDtypeStruct(w.shape, F32)] * 4,
        compiler_params=pltpu.CompilerParams(dimension_semantics=("parallel", "parallel"), vmem_limit_bytes=VMEM_LIMIT),
    )(cidx, w, mine, theirs, m, v)


SMALL = ("e_norm_w", "e_q_norm_w", "e_k_norm_w", "e_conv_w", "o_norm_w", "o_pool_w", "o_pool_scale",
         "o_dconv_w", "o_dconv_b", "o_ln_w", "o_ln_b")
SMALL_SHARDED = ("e_conv_w", "o_norm_w", "o_pool_scale", "o_dconv_w", "o_dconv_b", "o_ln_w", "o_ln_b")
WEIGHTS = ("e_norm_w", "e_w_in", "e_q_norm_w", "e_k_norm_w", "e_conv_w", "e_w_out", "o_norm_w", "o_w_in",
           "o_pool_w", "o_pool_scale", "o_dconv_w", "o_dconv_b", "o_ln_w", "o_ln_b", "o_w_out")


def _pack(arrs):
    flat = jnp.concatenate([a.reshape(-1) for a in arrs])
    rows = -(-flat.shape[0] // (LANES * SUBLANES)) * SUBLANES
    flat = jnp.pad(flat, (0, rows * LANES - flat.shape[0]))
    return flat.reshape(rows, LANES)


def _unpack(packed, shapes):
    flat = packed.reshape(-1)
    out, off = [], 0
    for s in shapes:
        n = int(np.prod(s))
        out.append(flat[off:off + n].reshape(s))
        off += n
    return out


def _gather_last(a, block, width):
    return lax.dynamic_slice_in_dim(a, block * width, width, axis=a.ndim - 1)


def kernel(x, positions, e_norm_w, e_w_in, e_q_norm_w, e_k_norm_w, e_conv_w, e_w_out, o_norm_w, o_w_in, o_pool_w, o_pool_scale, o_dconv_w, o_dconv_b, o_ln_w, o_ln_b, o_w_out, loss_target, m_e_norm_w, m_e_w_in, m_e_q_norm_w, m_e_k_norm_w, m_e_conv_w, m_e_w_out, m_o_norm_w, m_o_w_in, m_o_pool_w, m_o_pool_scale, m_o_dconv_w, m_o_dconv_b, m_o_ln_w, m_o_ln_b, m_o_w_out, v_e_norm_w, v_e_w_in, v_e_q_norm_w, v_e_k_norm_w, v_e_conv_w, v_e_w_out, v_o_norm_w, v_o_w_in, v_o_pool_w, v_o_pool_scale, v_o_dconv_w, v_o_dconv_b, v_o_ln_w, v_o_ln_b, v_o_w_out):
    given = dict(e_norm_w=e_norm_w, e_w_in=e_w_in, e_q_norm_w=e_q_norm_w, e_k_norm_w=e_k_norm_w, e_conv_w=e_conv_w,
                 e_w_out=e_w_out, o_norm_w=o_norm_w, o_w_in=o_w_in, o_pool_w=o_pool_w, o_pool_scale=o_pool_scale,
                 o_dconv_w=o_dconv_w, o_dconv_b=o_dconv_b, o_ln_w=o_ln_w, o_ln_b=o_ln_b, o_w_out=o_w_out)
    mom = dict(e_norm_w=m_e_norm_w, e_w_in=m_e_w_in, e_q_norm_w=m_e_q_norm_w, e_k_norm_w=m_e_k_norm_w,
               e_conv_w=m_e_conv_w, e_w_out=m_e_w_out, o_norm_w=m_o_norm_w, o_w_in=m_o_w_in, o_pool_w=m_o_pool_w,
               o_pool_scale=m_o_pool_scale, o_dconv_w=m_o_dconv_w, o_dconv_b=m_o_dconv_b, o_ln_w=m_o_ln_w,
               o_ln_b=m_o_ln_b, o_w_out=m_o_w_out)
    var = dict(e_norm_w=v_e_norm_w, e_w_in=v_e_w_in, e_q_norm_w=v_e_q_norm_w, e_k_norm_w=v_e_k_norm_w,
               e_conv_w=v_e_conv_w, e_w_out=v_e_w_out, o_norm_w=v_o_norm_w, o_w_in=v_o_w_in, o_pool_w=v_o_pool_w,
               o_pool_scale=v_o_pool_scale, o_dconv_w=v_o_dconv_w, o_dconv_b=v_o_dconv_b, o_ln_w=v_o_ln_w,
               o_ln_b=v_o_ln_b, o_w_out=v_o_w_out)
    S = x.shape[1]
    mx, my, mc = lax.axis_index("x"), lax.axis_index("y"), lax.axis_index("c")
    chip = 2 * mx + my
    cidx = jnp.reshape(mc, (1,)).astype(jnp.int32)
    bidx = jnp.reshape(chip, (1,)).astype(jnp.int32)

    shards = {n: given[n][0].astype(BF16) for n in BIG}
    shard_sizes = [int(np.prod(given[n].shape)) for n in SMALL_SHARDED]
    flat = jnp.concatenate([given[n].reshape(-1) for n in SMALL_SHARDED])
    rows = -(-flat.shape[0] // (SMALL_PACK_W * SUBLANES)) * SUBLANES
    shards[SMALL_PACK] = jnp.pad(flat, (0, rows * SMALL_PACK_W - flat.shape[0])).reshape(rows, SMALL_PACK_W)

    def unpack_small(full):
        gathered = full.reshape(N_CHIPS, rows * SMALL_PACK_W)
        out, off = {}, 0
        for n, size in zip(SMALL_SHARDED, shard_sizes):
            sh = given[n].shape[1:]
            parts = gathered[:, off:off + size].reshape((N_CHIPS,) + sh)
            fullp = jnp.moveaxis(parts, 0, -2).reshape(sh[:-1] + (N_CHIPS * sh[-1],))
            out[n] = fullp.reshape(-1, fullp.shape[-1])
            off += size
        return out

    p = dict(e_norm_w=e_norm_w, e_q_norm_w=e_q_norm_w, e_k_norm_w=e_k_norm_w, o_pool_w=o_pool_w[0])
    lsum, dx, g, parts = _local_step(x[0], positions.reshape(S, 1), loss_target[0], shards, p, unpack_small,
                                     cidx, bidx)
    mine = [g[n] for n in SMALL] + [(0.5 / float(D_MODEL)) * jnp.sum(lsum, keepdims=True)]
    *theirs, tot = _swap_to_sibling(BIG, [parts[n] for n in BIG], "swap_reduced", False, small=_pack(mine))
    tot = _unpack(tot, [a.shape for a in mine])
    loss = tot[-1].reshape(())

    grads, delta, new_m, new_v = {}, {}, {}, {}
    for n, other in zip(BIG, theirs):
        sh = given[n].shape
        outs = _adamw_halves(cidx, given[n][0], parts[n], other, mom[n][0], var[n][0], n)
        grads[n], delta[n], new_m[n], new_v[n] = [a.reshape(sh) for a in outs]
    for n, gv in zip(SMALL, tot):
        if n in SMALL_SHARDED:
            gv = _gather_last(gv, chip, gv.shape[-1] // N_CHIPS)
        grads[n] = gv.reshape(given[n].shape)
    big_small = "o_pool_w"
    pw = [src[big_small].reshape(-1, LANES) for src in (given, grads, mom, var)]
    for dst, a in zip((delta, new_m, new_v), _adamw(*pw, "pool_w")):
        dst[big_small] = a.reshape(given[big_small].shape)
    tiny = tuple(n for n in SMALL if n != big_small)
    shapes = [given[n].shape for n in tiny]
    packed = [_pack([src[n] for n in tiny]) for src in (given, grads, mom, var)]
    for dst, pk in zip((delta, new_m, new_v), _adamw(*packed, "small")):
        for n, a in zip(tiny, _unpack(pk, shapes)):
            dst[n] = a
    return (loss, dx[None], *[grads[n] for n in WEIGHTS], *[delta[n] for n in WEIGHTS],
            *[new_m[n] for n in WEIGHTS], *[new_v[n] for n in WEIGHTS])
```

```python
import numpy as np
import jax
import jax.numpy as jnp
from jax import lax
from jax.experimental import pallas as pl
from jax.experimental.pallas import tpu as pltpu

F32 = jnp.float32
BF16 = jnp.bfloat16

D_MODEL = 1024
HEAD_DIM = 64
A_WIDTH = 512
A_HEADS = 8
A_GROUPS = ((128, 1), (512, 4), (2048, 16))
QBLK = 128
ROT_DIM = 16
ROPE_THETA = 500000.0
POOL_SIZES = (2, 4, 8, 16)
D_CONV = 31
SC_WIDTH = 3
EVEN_IN = 7168
ODD_IN = 2560
EPS = 1e-6
NEG = -1e30
ADAM_LR, ADAM_B1, ADAM_B2, ADAM_EPS, ADAM_WD, ADAM_STEP = 0.001, 0.9, 0.999, 1e-08, 0.01, 10

LANES = 128
SUBLANES = 8
HALO = 32
VMEM_LIMIT = 52 * 1024 * 1024
MESH = pl.DeviceIdType.MESH
ANY = pl.BlockSpec(memory_space=pl.ANY)

NT_DIMS = (((1,), (1,)), ((), ()))
TN_DIMS = (((0,), (0,)), ((), ()))


def _call(body, name, grid, in_specs, out_specs, out_shape, scratch=(), sem=None, aliases=None):
    return pl.pallas_call(
        body, name=name, grid=grid, in_specs=in_specs, out_specs=out_specs, out_shape=out_shape,
        scratch_shapes=list(scratch), input_output_aliases=aliases or {},
        compiler_params=pltpu.CompilerParams(dimension_semantics=sem, vmem_limit_bytes=VMEM_LIMIT))


def _sig(v):
    return jax.nn.sigmoid(v)


def _dsilu(v, s):
    return s * (1.0 + v * (1.0 - s))


def _out_projection(ut_ref, w_ref, lo, hi):
    acc = None
    for k, v in enumerate((lo, hi)):
        ut_ref[k * A_WIDTH:(k + 1) * A_WIDTH, :] = v.T.astype(BF16)
        part = jnp.dot(v.astype(BF16), w_ref[k * A_WIDTH:(k + 1) * A_WIDTH, :], preferred_element_type=F32)
        acc = part if acc is None else acc + part
    return acc


def _cs8(v):
    return v.reshape(v.shape[0] // SUBLANES, SUBLANES, v.shape[1]).sum(axis=0)


def _seg_mean():
    r = lax.broadcasted_iota(jnp.int32, (LANES, LANES), 0) // HEAD_DIM
    c = lax.broadcasted_iota(jnp.int32, (LANES, LANES), 1) // HEAD_DIM
    return jnp.where(r == c, 1.0 / HEAD_DIM, 0.0).astype(BF16)


def _segsum(v, ones):
    hi = v.astype(BF16)
    lo = (v - hi.astype(F32)).astype(BF16)
    return (jnp.dot(hi, ones, preferred_element_type=F32) + jnp.dot(lo, ones, preferred_element_type=F32))


def _head_rstd(v, seg_mean):
    return lax.rsqrt(jnp.dot((v * v).astype(BF16), seg_mean, preferred_element_type=F32) + EPS)


def _rope_tables(pos_ref, freq_ref):
    ang = pos_ref[...].astype(F32) * freq_ref[...]
    cosv, sinv = jnp.cos(ang), jnp.sin(ang)
    lm = lax.broadcasted_iota(jnp.int32, ang.shape, 1) % HEAD_DIM
    half = ROT_DIM // 2
    c = jnp.where(lm < ROT_DIM, cosv, 1.0)
    s1 = jnp.where((lm >= half) & (lm < ROT_DIM), sinv, 0.0)
    s2 = jnp.where(lm < half, -sinv, 0.0)
    return c, s1, s2


def _freq_table():
    half = ROT_DIM // 2
    inv = ROPE_THETA ** (-np.arange(half, dtype=np.float64) / half)
    lane = np.arange(LANES) % HEAD_DIM
    f = np.where(lane < ROT_DIM, inv[lane % half], 0.0)
    return jnp.asarray(f.reshape(1, LANES), F32)


def _load_once(hbm_ref, vmem_ref, sem):
    @pl.when(pl.program_id(0) == 0)
    def _():
        cp = pltpu.make_async_copy(hbm_ref, vmem_ref, sem)
        cp.start()
        cp.wait()


def _rms_rows(x_ref, nw_ref):
    xv = x_ref[...]
    ms = jnp.mean(xv * xv, axis=-1, keepdims=True)
    return xv * lax.rsqrt(ms + EPS) * nw_ref[...]


def _inproj(x, nw, w, tm, tn, name):
    S, N = x.shape[0], w.shape[1]

    def body(x_ref, nw_ref, w_hbm, o_ref, ht_ref, w_v, sem):
        _load_once(w_hbm, w_v, sem)
        h = _rms_rows(x_ref, nw_ref)
        ht_ref[...] = h.T.astype(BF16)
        hb = h.astype(BF16)
        for j in range(N // tn):
            o_ref[:, j * tn:(j + 1) * tn] = jnp.dot(hb, w_v[:, j * tn:(j + 1) * tn], preferred_element_type=F32)

    return _call(
        body, name, (S // tm,),
        [pl.BlockSpec((tm, D_MODEL), lambda i: (i, 0)),
         pl.BlockSpec((1, D_MODEL), lambda i: (0, 0)), ANY],
        [pl.BlockSpec((tm, N), lambda i: (i, 0)),
         pl.BlockSpec((D_MODEL, tm), lambda i: (0, i))],
        [jax.ShapeDtypeStruct((S, N), F32), jax.ShapeDtypeStruct((D_MODEL, S), BF16)],
        scratch=[pltpu.VMEM(w.shape, BF16), pltpu.SemaphoreType.DMA(())], sem=("arbitrary",))(x, nw, w)


def _inproj_gathering(x, nw, shard, bidx, first, late_names, late_shards, tm, name):
    S = x.shape[0]
    ni = S // tm
    K, cw = shard.shape
    nl = len(late_names)
    last = N_CHIPS - 1

    def body(b_ref, x_ref, nw_ref, s_hbm, *rest):
        ls_refs = rest[:nl]
        o_ref, ht_ref, f_hbm = rest[nl:nl + 3]
        lf_refs = rest[nl + 3:2 * nl + 3]
        hs, w_blk, lsem, send1, recv1, send2, recv2 = rest[2 * nl + 3:]
        j, i = pl.program_id(0), pl.program_id(1)
        g1 = _Gather((first,), (s_hbm,), (f_hbm,), send1, recv1)
        g2 = _Gather(late_names, ls_refs, lf_refs, send2, recv2)
        _, _, _, chips = _place()

        def load_block(src):
            cp = pltpu.make_async_copy(src, w_blk, lsem)
            cp.start()
            cp.wait()

        @pl.when((j == 0) & (i == 0))
        def _():
            g1.begin(relations=(0, 1))
            load_block(s_hbm)

        for r, (cx, cy) in enumerate(chips):
            @pl.when((j == r + 1) & (i == 0))
            def _(r=r, cx=cx, cy=cy):
                g1.wait_relayed(r)
                load_block(_block_of(f_hbm, first, 2 * cx + cy))

        @pl.when((j == 2) & (i == 0))
        def _():
            g1.relay(relations=(2,))
            g2.begin()

        pl.when((j == last) & (i == ni // 2))(g2.relay)

        rows = pl.ds(pl.multiple_of(i * tm, tm), tm)

        @pl.when(j == 0)
        def _():
            h = _rms_rows(x_ref, nw_ref)
            hs[rows, :] = h.astype(BF16)
            ht_ref[...] = h.T.astype(BF16)

        o_ref[...] = jnp.dot(hs[rows, :], w_blk[...], preferred_element_type=F32)

        @pl.when((j == 0) & (i == ni - 1))
        def _():
            g1.relay(relations=(0, 1))
            g1.begin(relations=(2,), sibling=False)

        @pl.when((j == last) & (i == ni - 1))
        def _():
            g1.end_rest()
            g2.end()

    def block_of_step(j, b_ref):
        return jnp.bitwise_xor(b_ref[0], jnp.bitwise_or(jnp.left_shift(jnp.bitwise_and(j, 1), 1), jnp.right_shift(j, 1)))

    outs = pl.pallas_call(
        body, name=name,
        grid_spec=pltpu.PrefetchScalarGridSpec(
            num_scalar_prefetch=1, grid=(N_CHIPS, ni),
            in_specs=[pl.BlockSpec((tm, D_MODEL), lambda j, i, b: (jnp.where(j == 0, i, 0), 0)),
                      pl.BlockSpec((1, D_MODEL), lambda j, i, b: (0, 0)), ANY] + [ANY] * nl,
            out_specs=[pl.BlockSpec((tm, cw), lambda j, i, b: (i, block_of_step(j, b))),
                       pl.BlockSpec((D_MODEL, tm), lambda j, i, b: (0, jnp.where(j == 0, i, ni - 1))),
                       ANY] + [ANY] * nl,
            scratch_shapes=[pltpu.VMEM((S, D_MODEL), BF16), pltpu.VMEM((K, cw), BF16), pltpu.SemaphoreType.DMA(())] +
            _gather_sems((first,)) + _gather_sems(late_names)),
        out_shape=[jax.ShapeDtypeStruct((S, cw * N_CHIPS), F32), jax.ShapeDtypeStruct((D_MODEL, S), BF16),
                   _full_shape(first, shard)] + [_full_shape(n, s) for n, s in zip(late_names, late_shards)],
        compiler_params=pltpu.CompilerParams(dimension_semantics=("arbitrary", "arbitrary"),
                                             vmem_limit_bytes=VMEM_LIMIT),
    )(bidx, x, nw, shard, *late_shards)
    return outs[0], outs[1], outs[2], list(outs[3:])


def _piece_blocks(pieces, tk, axis):
    starts, counts, s = [], [], 0
    for p in pieces:
        n = p.shape[axis] // tk
        starts.append(s)
        counts.append(n)
        s += n
    return starts, counts, s


def _mm_nt_rms(pieces, w, x, nw, dres, tm, name, scatter_names=(), scatter_halves=()):
    S = x.shape[0]
    npc = len(pieces)
    ni = S // tm
    ns = len(scatter_names)
    offs = np.cumsum([0] + [p.shape[1] for p in pieces]).tolist()

    def body(*refs):
        p_refs = refs[:npc]
        w_hbm, x_ref, nw_ref, dr_ref = refs[npc:npc + 4]
        h_refs = refs[npc + 4:npc + 4 + ns]
        dx_ref, dnw_ref = refs[npc + 4 + ns:npc + 6 + ns]
        r_refs = refs[npc + 6 + ns:npc + 6 + 2 * ns]
        w_v, sem, nacc = refs[npc + 6 + 2 * ns:npc + 9 + 2 * ns]
        i = pl.program_id(0)
        if ns:
            send, recv = refs[npc + 9 + 2 * ns:]

            @pl.when(i == 0)
            def _():
                for cp in _scatter_copies(scatter_names, h_refs, r_refs, send, recv):
                    cp.start()
        _load_once(w_hbm, w_v, sem)

        @pl.when(i == 0)
        def _():
            nacc[...] = jnp.zeros_like(nacc)

        dh = None
        for p in range(npc):
            part = lax.dot_general(p_refs[p][...].astype(BF16), w_v[:, offs[p]:offs[p + 1]], NT_DIMS,
                                   preferred_element_type=F32)
            dh = part if dh is None else dh + part
        xv = x_ref[...]
        rs = lax.rsqrt(jnp.mean(xv * xv, axis=-1, keepdims=True) + EPS)
        xh = xv * rs
        nacc[...] += _cs8(dh * xh)
        dxh = dh * nw_ref[...]
        dx_ref[...] = dr_ref[...] + rs * (dxh - xh * jnp.mean(dxh * xh, axis=-1, keepdims=True))

        @pl.when(i == ni - 1)
        def _():
            dnw_ref[...] = jnp.sum(nacc[...], axis=0, keepdims=True)
            if ns:
                for cp in _scatter_copies(scatter_names, h_refs, r_refs, send, recv):
                    cp.wait()

    row = pl.BlockSpec((tm, D_MODEL), lambda i: (i, 0))
    outs = _call(
        body, name, (ni,),
        [pl.BlockSpec((tm, p.shape[1]), lambda i: (i, 0)) for p in pieces] +
        [ANY, row, pl.BlockSpec((1, D_MODEL), lambda i: (0, 0)), row] + [ANY] * ns,
        [row, pl.BlockSpec((1, D_MODEL), lambda i: (0, 0))] + [ANY] * ns,
        [jax.ShapeDtypeStruct((S, D_MODEL), F32), jax.ShapeDtypeStruct((1, D_MODEL), F32)] +
        [jax.ShapeDtypeStruct((3,) + _shard_shape(h.shape, n), h.dtype) for n, h in zip(scatter_names, scatter_halves)],
        scratch=[pltpu.VMEM(w.shape, BF16), pltpu.SemaphoreType.DMA(()), pltpu.VMEM((SUBLANES, D_MODEL), F32)] +
        (_scatter_sems(scatter_names) if ns else []),
        sem=("arbitrary",))(*pieces, w, x, nw, dres, *scatter_halves)
    return outs[0], outs[1], list(outs[2:])


def _mm_wgrad(at, pieces, tn, name):
    M, S = at.shape
    starts, counts, nj = _piece_blocks(pieces, tn, 1)
    npc = len(pieces)

    def body(*refs):
        a_hbm = refs[0]
        p_refs = refs[1:1 + npc]
        o_ref, o16_ref, a_v, sem = refs[1 + npc:]
        j = pl.program_id(0)
        _load_once(a_hbm, a_v, sem)
        for p in range(npc):
            @pl.when((j >= starts[p]) & (j < starts[p] + counts[p]))
            def _(p=p):
                acc = jnp.dot(a_v[...], p_refs[p][...].astype(BF16), preferred_element_type=F32)
                o_ref[...] = acc
                o16_ref[...] = acc.astype(BF16)

    def pspec(p):
        return pl.BlockSpec((S, tn), lambda j: (0, jnp.clip(j - starts[p], 0, counts[p] - 1)))

    col = pl.BlockSpec((M, tn), lambda j: (0, j))
    return _call(
        body, name, (nj,),
        [ANY] + [pspec(p) for p in range(npc)], [col, col],
        [jax.ShapeDtypeStruct((M, nj * tn), F32), jax.ShapeDtypeStruct((M, nj * tn), BF16)],
        scratch=[pltpu.VMEM(at.shape, BF16), pltpu.SemaphoreType.DMA(())], sem=("arbitrary",))(at, *pieces)


def _stream_spec(d, T):
    return pl.BlockSpec((d, T // d, A_WIDTH), lambda i: (0, i, 0))


def _stream_shape(d, S, dtype):
    return jax.ShapeDtypeStruct((d, S // d, A_WIDTH), dtype)


N_CHUNK = A_WIDTH // LANES


def _to_tokens(ref, scr, d, T):
    if d == 1:
        return ref[0].astype(F32)
    for r in range(d):
        for ch in range(N_CHUNK):
            scr.at[ch][pl.ds(r, T // d, stride=d), :] = ref[r, :, ch * LANES:(ch + 1) * LANES].astype(F32)
    return _get(scr)


def _from_tokens(out_ref, scr, d, T):
    for r in range(d):
        for ch in range(N_CHUNK):
            out_ref[r, :, ch * LANES:(ch + 1) * LANES] = scr.at[ch][pl.ds(r, T // d, stride=d), :].astype(out_ref.dtype)


def _put(scr, val):
    for ch in range(N_CHUNK):
        scr[ch] = val[:, ch * LANES:(ch + 1) * LANES]


def _get(scr):
    return jnp.concatenate([scr[ch] for ch in range(N_CHUNK)], axis=1)


def _chunked(T):
    return pltpu.VMEM((N_CHUNK, T, LANES), F32)


def _compact_spec(d, T):
    return pl.BlockSpec((d, T // d, LANES), lambda i: (0, i, 0))


def _compact_shape(d, S):
    return jax.ShapeDtypeStruct((d, S // d, LANES), F32)


def _compact_to_tokens(ref, scr, d, T):
    if d == 1:
        return ref[0]
    for r in range(d):
        scr[pl.ds(r, T // d, stride=d), :] = ref[r]
    return scr[...]


def _compact_from_tokens(out_ref, scr, val, d, T):
    if d == 1:
        out_ref[0] = val
        return
    scr[...] = val
    for r in range(d):
        out_ref[r] = scr[pl.ds(r, T // d, stride=d), :]


def _head_expander():
    r = lax.broadcasted_iota(jnp.int32, (LANES, A_WIDTH), 0)
    c = lax.broadcasted_iota(jnp.int32, (LANES, A_WIDTH), 1) // HEAD_DIM
    return (r == c).astype(BF16)


def _head_reducer():
    r = lax.broadcasted_iota(jnp.int32, (A_WIDTH, LANES), 0) // HEAD_DIM
    c = lax.broadcasted_iota(jnp.int32, (A_WIDTH, LANES), 1)
    return (r == c).astype(BF16)


def _qkv_prep(proj, pos, freq, wq, wk, T):
    S = proj.shape[0]
    qk_w = 3 * A_WIDTH

    def body(q_ref, k_ref, v_ref, pos_ref, f_ref, wq_ref, wk_ref, *rest):
        outs, tabs, scr = rest[:9], rest[9:12], rest[12]
        seg_mean = _seg_mean()
        c, s1, s2 = _rope_tables(pos_ref, f_ref)
        for tab, val in zip(tabs, (c, s1, s2)):
            tab[...] = val
        for t, (src, w_ref) in enumerate(((q_ref, wq_ref), (k_ref, wk_ref), (v_ref, None))):
            for g in range(3):
                d = A_GROUPS[g][1]
                out = outs[3 * t + g]
                for ch in range(A_WIDTH // LANES):
                    cs = slice(ch * LANES, (ch + 1) * LANES)
                    v = src[:, g * A_WIDTH + ch * LANES: g * A_WIDTH + (ch + 1) * LANES]
                    if w_ref is not None:
                        y = v * _head_rstd(v, seg_mean) * w_ref[...]
                        v = y * c + pltpu.roll(y, 8, 1) * s1 + pltpu.roll(y, LANES - 8, 1) * s2
                    if d == 1:
                        out[0, :, cs] = v.astype(BF16)
                    else:
                        scr[ch] = v
                if d > 1:
                    _from_tokens(out, scr, d, T)

    ds_ = [A_GROUPS[g][1] for g in range(3)] * 3
    return _call(
        body, "qkv_prep", (S // T,),
        [pl.BlockSpec((T, qk_w), lambda i: (i, 0)), pl.BlockSpec((T, qk_w), lambda i: (i, 1)),
         pl.BlockSpec((T, qk_w), lambda i: (i, 2)),
         pl.BlockSpec((T, 1), lambda i: (i, 0)), pl.BlockSpec((1, LANES), lambda i: (0, 0)),
         pl.BlockSpec((1, LANES), lambda i: (0, 0)), pl.BlockSpec((1, LANES), lambda i: (0, 0))],
        [_stream_spec(d, T) for d in ds_] + [pl.BlockSpec((T, LANES), lambda i: (i, 0))] * 3,
        [_stream_shape(d, S, BF16) for d in ds_] + [jax.ShapeDtypeStruct((S, LANES), F32)] * 3,
        scratch=[_chunked(T)], sem=("parallel",))(proj, proj, proj, pos, freq, wq, wk)


def _attn_mask(i):
    qi = lax.broadcasted_iota(jnp.int32, (QBLK, 2 * QBLK), 0) + QBLK
    kj = lax.broadcasted_iota(jnp.int32, (QBLK, 2 * QBLK), 1)
    dist = qi - kj
    return (dist >= 0) & (dist <= QBLK) & ((i > 0) | (kj >= QBLK))


ATT_BLK = (None, QBLK, A_WIDTH)
ATT_CBLK = (None, QBLK, LANES)
FWD_BLOCKS = 2


def _first_head_lanes():
    return lax.broadcasted_iota(jnp.int32, (1, LANES), 1) < HEAD_DIM


def _split_heads(v, first):
    zero = jnp.zeros_like(v)
    return jnp.where(first, v, zero), jnp.where(first, zero, v)


def _attn_fwd(q, k, v, g):
    d, n, _ = q.shape
    nb = n // QBLK
    assert nb % FWD_BLOCKS == 0, (n, QBLK)
    rows2 = FWD_BLOCKS * QBLK

    def body(q_ref, kp_ref, kc_ref, vp_ref, vc_ref, o_ref, l_ref, s_scr, p_scr):
        i = pl.program_id(1)
        masks = (_attn_mask(i), _attn_mask(1))
        first = _first_head_lanes()

        def keys(prev_ref, cur_ref, b, ps):
            if b == 0:
                return jnp.concatenate([prev_ref[:, ps], cur_ref[0:QBLK, ps]], axis=0)
            return cur_ref[:, ps]

        for b in range(FWD_BLOCKS):
            rows = slice(b * QBLK, (b + 1) * QBLK)
            for pr in range(A_HEADS // 2):
                ps = slice(pr * LANES, (pr + 1) * LANES)
                kc = keys(kp_ref, kc_ref, b, ps)
                for e, qh in enumerate(_split_heads(q_ref[rows, ps], first)):
                    s_scr[b * A_HEADS + 2 * pr + e] = lax.dot_general(qh, kc, NT_DIMS, preferred_element_type=F32)
        lane = lax.broadcasted_iota(jnp.int32, (1, LANES), 1)
        for b in range(FWD_BLOCKS):
            lrow = jnp.zeros((QBLK, LANES), F32)
            for h in range(A_HEADS):
                s = jnp.where(masks[b], s_scr[b * A_HEADS + h] * (HEAD_DIM ** -0.5), NEG)
                m = jnp.max(s, axis=-1, keepdims=True)
                p = jnp.exp(s - m)
                den = jnp.sum(p, axis=-1, keepdims=True)
                p_scr[b * A_HEADS + h] = (p / den).astype(BF16)
                lrow = jnp.where(lane == h, m + jnp.log(den), lrow)
            l_ref[b * QBLK:(b + 1) * QBLK, :] = lrow
        for b in range(FWD_BLOCKS):
            for pr in range(A_HEADS // 2):
                ps = slice(pr * LANES, (pr + 1) * LANES)
                va, vb = _split_heads(keys(vp_ref, vc_ref, b, ps), first)
                o_ref[b * QBLK:(b + 1) * QBLK, ps] = (
                    jnp.dot(p_scr[b * A_HEADS + 2 * pr], va, preferred_element_type=F32) +
                    jnp.dot(p_scr[b * A_HEADS + 2 * pr + 1], vb, preferred_element_type=F32)).astype(BF16)

    prev = lambda r, i: (r, jnp.maximum(FWD_BLOCKS * i - 1, 0), 0)
    cur = lambda r, i: (r, i, 0)
    wide = (None, rows2, A_WIDTH)
    units = FWD_BLOCKS * A_HEADS
    return _call(
        body, "attn_fwd_g%d" % g, (d, nb // FWD_BLOCKS),
        [pl.BlockSpec(wide, cur), pl.BlockSpec(ATT_BLK, prev), pl.BlockSpec(wide, cur),
         pl.BlockSpec(ATT_BLK, prev), pl.BlockSpec(wide, cur)],
        [pl.BlockSpec(wide, cur), pl.BlockSpec((None, rows2, LANES), cur)],
        [jax.ShapeDtypeStruct((d, n, A_WIDTH), BF16), jax.ShapeDtypeStruct((d, n, LANES), F32)],
        scratch=[pltpu.VMEM((units, QBLK, 2 * QBLK), F32), pltpu.VMEM((units, QBLK, 2 * QBLK), BF16)],
        sem=("parallel", "parallel"))(q, k, k, v, v)


def _attn_bwd(q, k, v, do, lse, cg, g):
    d, n, _ = q.shape
    nb = n // QBLK
    scale = HEAD_DIM ** -0.5

    def body(q_ref, kp_ref, kc_ref, vp_ref, vc_ref, do_ref, l_ref, c_ref, dq_ref, dk_ref, dv_ref, ck, cv,
             s_scr, dp_scr, p_scr, ds_scr):
        i = pl.program_id(1)

        @pl.when(i == 0)
        def _():
            ck[...] = jnp.zeros_like(ck)
            cv[...] = jnp.zeros_like(cv)

        @pl.when(i < nb)
        def _():
            mask = _attn_mask(i)
            first = _first_head_lanes()
            for pr in range(A_HEADS // 2):
                ps = slice(pr * LANES, (pr + 1) * LANES)
                kc = jnp.concatenate([kp_ref[:, ps], kc_ref[:, ps]], axis=0)
                vc = jnp.concatenate([vp_ref[:, ps], vc_ref[:, ps]], axis=0)
                qs = _split_heads(q_ref[:, ps], first)
                dos = _split_heads(do_ref[:, ps], first)
                for e in range(2):
                    s_scr[2 * pr + e] = lax.dot_general(qs[e], kc, NT_DIMS, preferred_element_type=F32)
                    dp_scr[2 * pr + e] = lax.dot_general(dos[e], vc, NT_DIMS, preferred_element_type=F32)
            for h in range(A_HEADS):
                p = jnp.where(mask, jnp.exp(s_scr[h] * scale - l_ref[:, h:h + 1]), 0.0)
                p_scr[h] = p.astype(BF16)
                ds_scr[h] = (p * (dp_scr[h] + c_ref[:, h:h + 1]) * scale).astype(BF16)
            for pr in range(A_HEADS // 2):
                ps = slice(pr * LANES, (pr + 1) * LANES)
                ks = _split_heads(jnp.concatenate([kp_ref[:, ps], kc_ref[:, ps]], axis=0), first)
                qs = _split_heads(q_ref[:, ps], first)
                dos = _split_heads(do_ref[:, ps], first)
                dq = dkc = dvc = None
                for e in range(2):
                    ds = ds_scr[2 * pr + e]
                    a = jnp.dot(ds, ks[e], preferred_element_type=F32)
                    b = lax.dot_general(ds, qs[e], TN_DIMS, preferred_element_type=F32)
                    c = lax.dot_general(p_scr[2 * pr + e], dos[e], TN_DIMS, preferred_element_type=F32)
                    dq, dkc, dvc = (a, b, c) if e == 0 else (dq + a, dkc + b, dvc + c)
                dq_ref[:, ps] = dq.astype(BF16)
                dk_ref[:, ps] = (ck[:, ps] + dkc[:QBLK]).astype(BF16)
                dv_ref[:, ps] = (cv[:, ps] + dvc[:QBLK]).astype(BF16)
                ck[:, ps] = dkc[QBLK:]
                cv[:, ps] = dvc[QBLK:]

        @pl.when(i == nb)
        def _():
            dk_ref[...] = ck[...].astype(BF16)
            dv_ref[...] = cv[...].astype(BF16)

    qi = lambda i: jnp.minimum(i, nb - 1)
    cur = lambda r, i: (r, qi(i), 0)
    prev = lambda r, i: (r, jnp.maximum(qi(i) - 1, 0), 0)
    late = lambda r, i: (r, jnp.maximum(i - 1, 0), 0)
    return _call(
        body, "attn_bwd_g%d" % g, (d, nb + 1),
        [pl.BlockSpec(ATT_BLK, cur), pl.BlockSpec(ATT_BLK, prev), pl.BlockSpec(ATT_BLK, cur),
         pl.BlockSpec(ATT_BLK, prev), pl.BlockSpec(ATT_BLK, cur),
         pl.BlockSpec(ATT_BLK, cur), pl.BlockSpec(ATT_CBLK, cur), pl.BlockSpec(ATT_CBLK, cur)],
        [pl.BlockSpec(ATT_BLK, cur), pl.BlockSpec(ATT_BLK, late), pl.BlockSpec(ATT_BLK, late)],
        [jax.ShapeDtypeStruct((d, n, A_WIDTH), BF16)] * 3,
        scratch=[pltpu.VMEM((QBLK, A_WIDTH), F32), pltpu.VMEM((QBLK, A_WIDTH), F32),
                 pltpu.VMEM((A_HEADS, QBLK, 2 * QBLK), F32), pltpu.VMEM((A_HEADS, QBLK, 2 * QBLK), F32),
                 pltpu.VMEM((A_HEADS, QBLK, 2 * QBLK), BF16), pltpu.VMEM((A_HEADS, QBLK, 2 * QBLK), BF16)],
        sem=("parallel", "arbitrary"))(q, k, k, v, v, do, lse, cg)


def _merge_weights(l0, l1, l2):
    mx = jnp.maximum(jnp.maximum(l0, l1), l2)
    e0, e1, e2 = jnp.exp(l0 - mx), jnp.exp(l1 - mx), jnp.exp(l2 - mx)
    den = e0 + e1 + e2
    return e0 / den, e1 / den, e2 / den


def _even_specs(T, S):
    t8 = T // SUBLANES
    last8 = S // SUBLANES - 1
    col = lambda c: pl.BlockSpec((T, A_WIDTH), lambda i: (i, c))
    prev8 = lambda c: pl.BlockSpec((SUBLANES, A_WIDTH), lambda i: (jnp.maximum(i * t8 - 1, 0), c))
    next8 = lambda c: pl.BlockSpec((SUBLANES, A_WIDTH), lambda i: (jnp.minimum((i + 1) * t8, last8), c))
    return col, prev8, next8


GROUP_D = tuple(d for _, d in A_GROUPS)


def _even_mixer_fwd(x, proj, os_, ls_, conv_w, w_out, T):
    S = proj.shape[0]
    col, prev8, _ = _even_specs(T, S)
    H = SUBLANES

    def body(x_ref, w_ref, bg_r, cg_r, hb_r, zl_r, zh_r, cgp_r, hbp_r, o0, o1, o2, l0, l1, l2, cw_r,
             x1_ref, ut_ref, ext, cscr, *scr):
        i = pl.program_id(0)
        ls = [_compact_to_tokens(r, cscr, GROUP_D[g], T) for g, r in enumerate((l0, l1, l2))]
        expand = _head_expander()
        ws = [_segsum(w, expand) for w in _merge_weights(*ls)]
        oa = ws[0] * _to_tokens(o0, scr[0], GROUP_D[0], T)
        oa = oa + ws[1] * _to_tokens(o1, scr[1], GROUP_D[1], T)
        oa = oa + ws[2] * _to_tokens(o2, scr[2], GROUP_D[2], T)
        ext[0:H, :] = jnp.where(i == 0, 0.0, cgp_r[...] * hbp_r[...])
        ext[H:H + T, :] = cg_r[...] * hb_r[...]
        conv = cw_r[0:1, :] * ext[H - 2:H - 2 + T, :]
        for kk in range(1, SC_WIDTH):
            conv = conv + cw_r[kk:kk + 1, :] * ext[H - 2 + kk:H - 2 + kk + T, :]
        zl, zh = zl_r[...], zh_r[...]
        x1_ref[...] = x_ref[...] + _out_projection(ut_ref, w_ref, oa * (zl * _sig(zl)),
                                                   bg_r[...] * conv * (zh * _sig(zh)))

    streams = [_stream_spec(d, T) for d in GROUP_D]
    compacts = [_compact_spec(d, T) for d in GROUP_D]
    row = pl.BlockSpec((T, D_MODEL), lambda i: (i, 0))
    return _call(
        body, "even_mixer_fwd", (S // T,),
        [row, pl.BlockSpec((D_MODEL, D_MODEL), lambda i: (0, 0)),
         col(9), col(10), col(11), col(12), col(13), prev8(10), prev8(11)] + streams + compacts +
        [pl.BlockSpec((SC_WIDTH, A_WIDTH), lambda i: (0, 0))],
        [row, pl.BlockSpec((D_MODEL, T), lambda i: (0, i))],
        [jax.ShapeDtypeStruct((S, D_MODEL), F32), jax.ShapeDtypeStruct((D_MODEL, S), BF16)],
        scratch=[pltpu.VMEM((T + H, A_WIDTH), F32), pltpu.VMEM((T, LANES), F32)] + [_chunked(T)] * 3,
        sem=("parallel",))(
            x, w_out, proj, proj, proj, proj, proj, proj, proj, *os_, *ls_, conv_w)


def _even_mixer_bwd(dy, w_out, proj, os_, ls_, conv_w, T):
    S = proj.shape[0]
    nt = S // T
    col, prev8, next8 = _even_specs(T, S)
    H = SUBLANES
    t8 = T // SUBLANES
    last8 = S // SUBLANES - 1

    def body(dy_r, dyn_r, w_ref, bg_r, cg_r, hb_r, zl_r, zh_r, cgp_r, hbp_r, zhn_r, bgn_r,
             o0, o1, o2, l0, l1, l2, cw_r,
             do0, do1, do2, c0, c1, c2, dr_ref, dcw_ref, ext_t, ext_d, acc, cscr, s_a, s_b, s_c):
        i = pl.program_id(0)

        @pl.when(i == 0)
        def _():
            acc[...] = jnp.zeros_like(acc)

        zl, zh = zl_r[...], zh_r[...]
        sl, sh = _sig(zl), _sig(zh)
        du = lax.dot_general(dy_r[...].astype(BF16), w_ref[...], NT_DIMS, preferred_element_type=F32)
        dul, duh = du[:, 0:A_WIDTH], du[:, A_WIDTH:]
        dun = lax.dot_general(dyn_r[...].astype(BF16), w_ref[A_WIDTH:, :], NT_DIMS, preferred_element_type=F32)
        scr = (s_a, s_b, s_c)
        ls = [_compact_to_tokens(r, cscr, GROUP_D[g], T) for g, r in enumerate((l0, l1, l2))]
        wcs = _merge_weights(*ls)
        expand = _head_expander()
        ws = [_segsum(w, expand) for w in wcs]
        oa = ws[0] * _to_tokens(o0, scr[0], GROUP_D[0], T)
        oa = oa + ws[1] * _to_tokens(o1, scr[1], GROUP_D[1], T)
        oa = oa + ws[2] * _to_tokens(o2, scr[2], GROUP_D[2], T)
        doa = dul * (zl * sl)
        rsum = _segsum(doa * oa, _head_reducer())
        for g, (do_ref, c_ref) in enumerate(((do0, c0), (do1, c1), (do2, c2))):
            d = GROUP_D[g]
            _compact_from_tokens(c_ref, cscr, -wcs[g] * rsum, d, T)
            if d == 1:
                do_ref[0] = (ws[g] * doa).astype(BF16)
            else:
                _put(s_c, ws[g] * doa)
                _from_tokens(do_ref, s_c, d, T)
        cgv, hbv, bgv = cg_r[...], hb_r[...], bg_r[...]
        ext_t[0:H, :] = jnp.where(i == 0, 0.0, cgp_r[...] * hbp_r[...])
        ext_t[H:H + T, :] = cgv * hbv
        conv = cw_r[0:1, :] * ext_t[H - 2:H - 2 + T, :]
        for kk in range(1, SC_WIDTH):
            conv = conv + cw_r[kk:kk + 1, :] * ext_t[H - 2 + kk:H - 2 + kk + T, :]
        dyb = duh * (zh * sh)
        dconv = dyb * bgv
        zn = zhn_r[...]
        ext_d[0:T, :] = dconv
        ext_d[T:T + H, :] = jnp.where(i == nt - 1, 0.0, dun * (zn * _sig(zn)) * bgn_r[...])
        dt = cw_r[0:1, :] * ext_d[2:2 + T, :]
        for kk in range(1, SC_WIDTH):
            dt = dt + cw_r[kk:kk + 1, :] * ext_d[2 - kk:2 - kk + T, :]
        for kk in range(SC_WIDTH):
            acc[kk * SUBLANES:(kk + 1) * SUBLANES, :] += _cs8(dconv * ext_t[H - 2 + kk:H - 2 + kk + T, :])
        dr_ref[:, 0:A_WIDTH] = (dyb * conv).astype(BF16)
        dr_ref[:, A_WIDTH:2 * A_WIDTH] = (dt * hbv).astype(BF16)
        dr_ref[:, 2 * A_WIDTH:3 * A_WIDTH] = (dt * cgv).astype(BF16)
        dr_ref[:, 3 * A_WIDTH:4 * A_WIDTH] = (dul * oa * _dsilu(zl, sl)).astype(BF16)
        dr_ref[:, 4 * A_WIDTH:5 * A_WIDTH] = (duh * (bgv * conv) * _dsilu(zh, sh)).astype(BF16)

        @pl.when(i == nt - 1)
        def _():
            for kk in range(SC_WIDTH):
                dcw_ref[kk:kk + 1, :] = jnp.sum(acc[kk * SUBLANES:(kk + 1) * SUBLANES, :], axis=0, keepdims=True)

    streams = [_stream_spec(d, T) for d in GROUP_D]
    dynext = pl.BlockSpec((SUBLANES, D_MODEL), lambda i: (jnp.minimum((i + 1) * t8, last8), 0))
    compacts = [_compact_spec(d, T) for d in GROUP_D]
    outs = _call(
        body, "even_mixer_bwd", (nt,),
        [pl.BlockSpec((T, D_MODEL), lambda i: (i, 0)), dynext, pl.BlockSpec((D_MODEL, D_MODEL), lambda i: (0, 0)),
         col(9), col(10), col(11), col(12), col(13), prev8(10), prev8(11), next8(13), next8(9)] +
        streams + compacts + [pl.BlockSpec((SC_WIDTH, A_WIDTH), lambda i: (0, 0))],
        streams + compacts + [pl.BlockSpec((T, 5 * A_WIDTH), lambda i: (i, 0)),
                              pl.BlockSpec((SC_WIDTH, A_WIDTH), lambda i: (0, 0))],
        [_stream_shape(d, S, BF16) for d in GROUP_D] + [_compact_shape(d, S) for d in GROUP_D] +
        [jax.ShapeDtypeStruct((S, 5 * A_WIDTH), BF16), jax.ShapeDtypeStruct((SC_WIDTH, A_WIDTH), F32)],
        scratch=[pltpu.VMEM((T + H, A_WIDTH), F32), pltpu.VMEM((T + H, A_WIDTH), F32),
                 pltpu.VMEM((SC_WIDTH * SUBLANES, A_WIDTH), F32), pltpu.VMEM((T, LANES), F32)] +
                [_chunked(T)] * 3,
        sem=("arbitrary",))(dy, dy, w_out, proj, proj, proj, proj, proj, proj, proj, proj, proj, *os_, *ls_, conv_w)
    return outs[0:3], outs[3:6], outs[6], outs[7]


def _qk_bwd(proj, dqs, dks, dvs, rope, wq, wk, T):
    S = proj.shape[0]
    nt = S // T
    qk_w = 3 * A_WIDTH

    def body(q_ref, k_ref, dq0, dq1, dq2, dk0, dk1, dk2, dv0, dv1, dv2, c_ref, s1_ref, s2_ref, wq_ref, wk_ref,
             o_ref, dw_ref, acc, scr):
        i = pl.program_id(0)

        @pl.when(i == 0)
        def _():
            acc[...] = jnp.zeros_like(acc)
            dw_ref[...] = jnp.zeros_like(dw_ref)

        seg_mean = _seg_mean()
        c, s1, s2 = c_ref[...], s1_ref[...], s2_ref[...]
        for t, (src, w_ref, ds) in enumerate(((q_ref, wq_ref, (dq0, dq1, dq2)), (k_ref, wk_ref, (dk0, dk1, dk2)))):
            wv = w_ref[...]
            for g in range(3):
                d = GROUP_D[g]
                if d > 1:
                    _to_tokens(ds[g], scr, d, T)
                for ch in range(A_WIDTH // LANES):
                    cs = slice(g * A_WIDTH + ch * LANES, g * A_WIDTH + (ch + 1) * LANES)
                    lc = slice(ch * LANES, (ch + 1) * LANES)
                    v = src[:, cs]
                    dout = ds[g][0, :, lc].astype(F32) if d == 1 else scr[ch]
                    rs = lax.rsqrt(_segsum(v * v, seg_mean) + EPS)
                    xh = v * rs
                    dy = dout * c + pltpu.roll(dout * s1, LANES - 8, 1) + pltpu.roll(dout * s2, 8, 1)
                    acc[t * SUBLANES:(t + 1) * SUBLANES, :] += _cs8(dy * xh)
                    dxh = dy * wv
                    mean = _segsum(dxh * xh, seg_mean)
                    o_ref[:, t * qk_w + g * A_WIDTH + ch * LANES: t * qk_w + g * A_WIDTH + (ch + 1) * LANES] = (
                        rs * (dxh - xh * mean)).astype(BF16)
        for g, dv in enumerate((dv0, dv1, dv2)):
            d = GROUP_D[g]
            base = 2 * qk_w + g * A_WIDTH
            o_ref[:, base:base + A_WIDTH] = _to_tokens(dv, scr, d, T).astype(BF16)

        @pl.when(i == nt - 1)
        def _():
            for t in range(2):
                srow = jnp.sum(acc[t * SUBLANES:(t + 1) * SUBLANES, :], axis=0, keepdims=True)
                dw_ref[t:t + 1, :] = srow + pltpu.roll(srow, HEAD_DIM, 1)

    streams = [_stream_spec(d, T) for d in GROUP_D]
    return _call(
        body, "qk_bwd", (nt,),
        [pl.BlockSpec((T, qk_w), lambda i: (i, 0)), pl.BlockSpec((T, qk_w), lambda i: (i, 1))] + streams * 3 +
        [pl.BlockSpec((T, LANES), lambda i: (i, 0))] * 3 +
        [pl.BlockSpec((1, LANES), lambda i: (0, 0)), pl.BlockSpec((1, LANES), lambda i: (0, 0))],
        [pl.BlockSpec((T, 3 * qk_w), lambda i: (i, 0)), pl.BlockSpec((SUBLANES, LANES), lambda i: (0, 0))],
        [jax.ShapeDtypeStruct((S, 3 * qk_w), BF16), jax.ShapeDtypeStruct((SUBLANES, LANES), F32)],
        scratch=[pltpu.VMEM((2 * SUBLANES, LANES), F32), _chunked(T)], sem=("arbitrary",))(
            proj, proj, *dqs, *dks, *dvs, *rope, wq, wk)


N_SMALL_ODD = 40
SHIFT_ROWS_LESS = SUBLANES


def _fill_shifted(ext_ref, sh_ref):
    rows = ext_ref.shape[0] - SHIFT_ROWS_LESS
    for b in range(1, SUBLANES):
        sh_ref[b - 1] = ext_ref[b:b + rows, :]


def _window(ext_ref, sh_ref, off, T):
    a, b = divmod(off, SUBLANES)
    if b == 0:
        return ext_ref[off:off + T, :]
    return sh_ref[b - 1, a * SUBLANES:a * SUBLANES + T, :]


def _odd_pool_tile(i, uc_r, ucp_r, pw_r, ext_u, pooled_s, pm_s, T):
    H = HALO
    uc = uc_r[...]
    ext_u[0:H, :] = jnp.where(i == 0, 0.0, ucp_r[...])
    ext_u[H:H + T, :] = uc
    row = i * T + lax.broadcasted_iota(jnp.int32, (T, 1), 0)
    for g, p in enumerate(POOL_SIZES):
        cs = slice(g * LANES, (g + 1) * LANES)
        win = ext_u[H:H + T, cs]
        for j in range(1, p):
            win = win + ext_u[H - j:H - j + T, cs]
        cnt = jnp.minimum(row + 1, p).astype(F32)
        pooled = win / cnt - uc[:, cs]
        pooled_s[:, cs] = pooled
        pm_s[:, cs] = jnp.dot(pooled.astype(BF16), pw_r[g].astype(BF16), preferred_element_type=F32)
    return row


def _odd_glu_tile(i, da_r, dg_r, dap_r, dgp_r, ext_g, sh_g, T):
    H = HALO
    ext_g[0:H, :] = jnp.where(i == 0, 0.0, dap_r[...] * _sig(dgp_r[...]))
    ext_g[H:H + T, :] = da_r[...] * _sig(dg_r[...])
    _fill_shifted(ext_g, sh_g)


def _odd_specs(T, S, order):
    tb = T // HALO
    col = lambda c: pl.BlockSpec((T, A_WIDTH), lambda s: (order(s), c))
    prev = lambda c: pl.BlockSpec((HALO, A_WIDTH), lambda s: (jnp.maximum(order(s) * tb - 1, 0), c))
    const2 = lambda shape: pl.BlockSpec(shape, lambda s: (0, 0))
    weights = [pl.BlockSpec((4, LANES, LANES), lambda s: (0, 0, 0)), const2((1, A_WIDTH)),
               const2((D_CONV, A_WIDTH)), const2((1, A_WIDTH)), const2((1, A_WIDTH)), const2((1, A_WIDTH))]
    return col, prev, weights


def _odd_mixer_fwd(x, tgt, proj, pool_w, scale, dconv_w, dconv_b, ln_w, ln_b, w_out, T):
    S = proj.shape[0]
    nt = S // T
    col, prev, wspecs = _odd_specs(T, S, lambda s: s)
    H = HALO

    def body(x_ref, t_ref, w_ref, uc_r, da_r, dg_r, zl_r, zh_r, ucp_r, dap_r, dgp_r, pw_r, sc_r, dw_r, db_r,
             lw_r, lb_r, dy_ref, l_ref, ut_ref, cv_ref, ext_u, ext_g, sh_g, pooled_s, pm_s, lacc):
        i = pl.program_id(0)

        @pl.when(i == 0)
        def _():
            lacc[...] = jnp.zeros_like(lacc)

        _odd_pool_tile(i, uc_r, ucp_r, pw_r, ext_u, pooled_s, pm_s, T)
        _odd_glu_tile(i, da_r, dg_r, dap_r, dgp_r, ext_g, sh_g, T)
        base = H - (D_CONV - 1)
        conv = db_r[...] + dw_r[0:1, :] * _window(ext_g, sh_g, base, T)
        for kk in range(1, D_CONV):
            conv = conv + dw_r[kk:kk + 1, :] * _window(ext_g, sh_g, base + kk, T)
        cv_ref[...] = conv
        mu = jnp.mean(conv, axis=-1, keepdims=True)
        xc = conv - mu
        yh = xc * lax.rsqrt(jnp.mean(xc * xc, axis=-1, keepdims=True) + EPS)
        ln = yh * lw_r[...] + lb_r[...]
        zl, zh = zl_r[...], zh_r[...]
        y = x_ref[...] + _out_projection(ut_ref, w_ref, pm_s[...] * sc_r[...] * (zl * _sig(zl)),
                                         ln * _sig(ln) * (zh * _sig(zh)))
        diff = y - t_ref[...]
        dy_ref[...] = diff / float(D_MODEL)
        lacc[...] += _cs8(diff * diff)

        @pl.when(i == nt - 1)
        def _():
            l_ref[...] = jnp.sum(lacc[...], axis=0, keepdims=True)

    row = pl.BlockSpec((T, D_MODEL), lambda i: (i, 0))
    return _call(
        body, "odd_mixer_fwd", (nt,),
        [row, row, pl.BlockSpec((D_MODEL, D_MODEL), lambda i: (0, 0)),
         col(0), col(1), col(2), col(3), col(4), prev(0), prev(1), prev(2)] + wspecs,
        [row, pl.BlockSpec((1, D_MODEL), lambda i: (0, 0)), pl.BlockSpec((D_MODEL, T), lambda i: (0, i)),
         pl.BlockSpec((T, A_WIDTH), lambda i: (i, 0))],
        [jax.ShapeDtypeStruct((S, D_MODEL), F32), jax.ShapeDtypeStruct((1, D_MODEL), F32),
         jax.ShapeDtypeStruct((D_MODEL, S), BF16), jax.ShapeDtypeStruct((S, A_WIDTH), F32)],
        scratch=[pltpu.VMEM((T + H, A_WIDTH), F32), pltpu.VMEM((T + H, A_WIDTH), F32),
                 pltpu.VMEM((SUBLANES - 1, T + H - SHIFT_ROWS_LESS, A_WIDTH), F32),
                 pltpu.VMEM((T, A_WIDTH), F32), pltpu.VMEM((T, A_WIDTH), F32),
                 pltpu.VMEM((SUBLANES, D_MODEL), F32)],
        sem=("arbitrary",))(x, tgt, w_out, proj, proj, proj, proj, proj, proj, proj, proj,
                            pool_w, scale, dconv_w, dconv_b, ln_w, ln_b)


def _odd_mixer_bwd(dy, w_out, proj, conv, pool_w, scale, dconv_w, dconv_b, ln_w, ln_b, T):
    S = proj.shape[0]
    nt = S // T
    order = lambda s: nt - 1 - s
    col, prev, wspecs = _odd_specs(T, S, order)
    H = HALO

    def body(dy_r, w_ref, cv_r, uc_r, da_r, dg_r, zl_r, zh_r, ucp_r, dap_r, dgp_r, pw_r, sc_r, dw_r, db_r, lw_r, lb_r,
             dp_ref, dpw_ref, sm_ref, ext_u, ext_g, sh_g, pooled_s, pm_s, dpl_s, ext_p, ext_c, sh_c, acc):
        step = pl.program_id(0)
        i = nt - 1 - step

        @pl.when(step == 0)
        def _():
            ext_p[T:T + H, :] = jnp.zeros((H, A_WIDTH), F32)
            ext_c[T:T + H, :] = jnp.zeros((H, A_WIDTH), F32)
            acc[...] = jnp.zeros_like(acc)
            dpw_ref[...] = jnp.zeros_like(dpw_ref)

        def accum(r, v):
            acc[r * SUBLANES:(r + 1) * SUBLANES, :] += _cs8(v)

        row = _odd_pool_tile(i, uc_r, ucp_r, pw_r, ext_u, pooled_s, pm_s, T)
        _odd_glu_tile(i, da_r, dg_r, dap_r, dgp_r, ext_g, sh_g, T)
        conv = cv_r[...]
        mu = jnp.mean(conv, axis=-1, keepdims=True)
        xc = conv - mu
        rstd = lax.rsqrt(jnp.mean(xc * xc, axis=-1, keepdims=True) + EPS)
        yh = xc * rstd
        ln = yh * lw_r[...] + lb_r[...]
        sln = _sig(ln)
        zl, zh = zl_r[...], zh_r[...]
        sl, sh = _sig(zl), _sig(zh)
        du = lax.dot_general(dy_r[...].astype(BF16), w_ref[...], NT_DIMS, preferred_element_type=F32)
        dul, duh = du[:, 0:A_WIDTH], du[:, A_WIDTH:]
        pm = pm_s[...]
        scv = sc_r[...]
        dyc = dul * (zl * sl)
        accum(34, dyc * pm)
        dpm = dyc * scv
        for g in range(len(POOL_SIZES)):
            cs = slice(g * LANES, (g + 1) * LANES)
            dpm_g = dpm[:, cs].astype(BF16)
            dpw_ref[g] += lax.dot_general(pooled_s[:, cs].astype(BF16), dpm_g, TN_DIMS, preferred_element_type=F32)
            dpl_s[:, cs] = lax.dot_general(dpm_g, pw_r[g].astype(BF16), NT_DIMS, preferred_element_type=F32)
        lane_p = lax.broadcasted_iota(jnp.int32, (1, A_WIDTH), 1) // LANES
        pvec = jnp.left_shift(2, lane_p)
        cnt = jnp.minimum(row + 1, pvec).astype(F32)
        dpl = dpl_s[...]
        ext_p[0:T, :] = dpl / cnt
        for g, p in enumerate(POOL_SIZES):
            cs = slice(g * LANES, (g + 1) * LANES)
            win = ext_p[0:T, cs]
            for j in range(1, p):
                win = win + ext_p[j:j + T, cs]
            dp_ref[:, cs] = (win - dpl[:, cs]).astype(BF16)
        ext_p[T:T + H, :] = ext_p[0:H, :]
        dln = duh * (zh * sh) * _dsilu(ln, sln)
        accum(32, dln * yh)
        accum(33, dln)
        dyh = dln * lw_r[...]
        dc = rstd * (dyh - jnp.mean(dyh, axis=-1, keepdims=True) - yh * jnp.mean(dyh * yh, axis=-1, keepdims=True))
        accum(31, dc)
        ext_c[0:T, :] = dc
        _fill_shifted(ext_c, sh_c)
        base = H - (D_CONV - 1)
        dgl = dw_r[0:1, :] * _window(ext_c, sh_c, D_CONV - 1, T)
        accum(0, dc * _window(ext_g, sh_g, base, T))
        for kk in range(1, D_CONV):
            dgl = dgl + dw_r[kk:kk + 1, :] * _window(ext_c, sh_c, D_CONV - 1 - kk, T)
            accum(kk, dc * _window(ext_g, sh_g, base + kk, T))
        ext_c[T:T + H, :] = ext_c[0:H, :]
        dav, dgv = da_r[...], dg_r[...]
        sg = _sig(dgv)
        dp_ref[:, A_WIDTH:2 * A_WIDTH] = (dgl * sg).astype(BF16)
        dp_ref[:, 2 * A_WIDTH:3 * A_WIDTH] = (dgl * dav * sg * (1.0 - sg)).astype(BF16)
        dp_ref[:, 3 * A_WIDTH:4 * A_WIDTH] = (dul * (pm * scv) * _dsilu(zl, sl)).astype(BF16)
        dp_ref[:, 4 * A_WIDTH:5 * A_WIDTH] = (duh * (ln * sln) * _dsilu(zh, sh)).astype(BF16)

        @pl.when(step == nt - 1)
        def _():
            for r in range(N_SMALL_ODD):
                sm_ref[r:r + 1, :] = jnp.sum(acc[r * SUBLANES:(r + 1) * SUBLANES, :], axis=0, keepdims=True)

    ext = pltpu.VMEM((T + H, A_WIDTH), F32)
    shifted = pltpu.VMEM((SUBLANES - 1, T + H - SHIFT_ROWS_LESS, A_WIDTH), F32)
    tile = pltpu.VMEM((T, A_WIDTH), F32)
    return _call(
        body, "odd_mixer_bwd", (nt,),
        [pl.BlockSpec((T, D_MODEL), lambda s: (order(s), 0)), pl.BlockSpec((D_MODEL, D_MODEL), lambda s: (0, 0)),
         pl.BlockSpec((T, A_WIDTH), lambda s: (order(s), 0)),
         col(0), col(1), col(2), col(3), col(4), prev(0), prev(1), prev(2)] + wspecs,
        [pl.BlockSpec((T, ODD_IN), lambda s: (order(s), 0)),
         pl.BlockSpec((4, LANES, LANES), lambda s: (0, 0, 0)),
         pl.BlockSpec((N_SMALL_ODD, A_WIDTH), lambda s: (0, 0))],
        [jax.ShapeDtypeStruct((S, ODD_IN), BF16), jax.ShapeDtypeStruct((4, LANES, LANES), F32),
         jax.ShapeDtypeStruct((N_SMALL_ODD, A_WIDTH), F32)],
        scratch=[ext, ext, shifted, tile, tile, tile, ext, ext, shifted,
                 pltpu.VMEM((N_SMALL_ODD * SUBLANES, A_WIDTH), F32)],
        sem=("arbitrary",))(dy, w_out, conv, proj, proj, proj, proj, proj, proj, proj, proj,
                            pool_w, scale, dconv_w, dconv_b, ln_w, ln_b)


TILE_SEQ = 256
TILE_WG = 256
TILE_FIRST = 512
TILE_MM = 512


SMALL_PACK = "small_pack"
SMALL_PACK_W = 2 * LANES
LATE_WEIGHTS = ("e_w_out", "o_w_in", "o_w_out", SMALL_PACK)
ODD_MATS = ("o_w_in", "o_w_out")
EVEN_MATS = ("e_w_in", "e_w_out")


def _reduce_start(names, grads, grads16, cidx):
    recv = _swap_to_sibling(names, [grads16[n] for n in names], "swap_halves_" + names[0][0], True)
    both = [_add_half(cidx, grads[n], r, n) for n, r in zip(names, recv)]
    return [h for h, _ in both], [hb for _, hb in both]


def _local_step(x, pos, tgt, shards, p, unpack_small, cidx, bidx):
    T = TILE_SEQ
    freq = _freq_table()
    wq = jnp.tile(p["e_q_norm_w"], (1, LANES // HEAD_DIM))
    wk = jnp.tile(p["e_k_norm_w"], (1, LANES // HEAD_DIM))

    proj_e, ht_e, w_e_in, late = _inproj_gathering(x, p["e_norm_w"], shards["e_w_in"], bidx, "e_w_in", LATE_WEIGHTS,
                                                   [shards[n] for n in LATE_WEIGHTS], TILE_FIRST, "inproj_even")
    wb = dict(zip(LATE_WEIGHTS, late), e_w_in=w_e_in)
    p = dict(p, **unpack_small(wb[SMALL_PACK]))
    qkv = _qkv_prep(proj_e, pos, freq, wq, wk, T)
    qs, ks, vs, rope = qkv[0:3], qkv[3:6], qkv[6:9], qkv[9:12]
    os_, ls_ = [], []
    for g in range(3):
        o, l = _attn_fwd(qs[g], ks[g], vs[g], g)
        os_.append(o)
        ls_.append(l)
    x1, ut_e = _even_mixer_fwd(x, proj_e, os_, ls_, p["e_conv_w"], wb["e_w_out"], T)
    proj_o, ht_o = _inproj(x1, p["o_norm_w"], wb["o_w_in"], TILE_MM, 1280, "inproj_odd")
    odd_w = (p["o_pool_w"], p["o_pool_scale"], p["o_dconv_w"], p["o_dconv_b"], p["o_ln_w"], p["o_ln_b"])
    dy, lsum, ut_o, conv_o = _odd_mixer_fwd(x1, tgt, proj_o, *odd_w, wb["o_w_out"], T)

    g, g16 = {}, {}
    g["o_w_out"], g16["o_w_out"] = _mm_wgrad(ut_o, [dy], TILE_WG, "wgrad_o_out")
    dproj_o, g["o_pool_w"], small_o = _odd_mixer_bwd(dy, wb["o_w_out"], proj_o, conv_o, *odd_w, T)
    g["o_w_in"], g16["o_w_in"] = _mm_wgrad(ht_o, [dproj_o], TILE_WG, "wgrad_o_in")
    half_o, half_o16 = _reduce_start(ODD_MATS, g, g16, cidx)
    dx1, g["o_norm_w"], blocks_o = _mm_nt_rms([dproj_o], wb["o_w_in"], x1, p["o_norm_w"], dy, TILE_MM, "dx_odd",
                                              ODD_MATS, half_o16)
    g["o_dconv_w"] = small_o[0:D_CONV]
    g["o_dconv_b"] = small_o[31:32]
    g["o_ln_w"] = small_o[32:33]
    g["o_ln_b"] = small_o[33:34]
    g["o_pool_scale"] = small_o[34:35]

    g["e_w_out"], g16["e_w_out"] = _mm_wgrad(ut_e, [dx1], TILE_WG, "wgrad_e_out")
    dos, cgs, drest, g["e_conv_w"] = _even_mixer_bwd(dx1, wb["e_w_out"], proj_e, os_, ls_, p["e_conv_w"], T)
    dqs, dks, dvs = [], [], []
    for gi in range(3):
        dq, dk, dv = _attn_bwd(qs[gi], ks[gi], vs[gi], dos[gi], ls_[gi], cgs[gi], gi)
        dqs.append(dq)
        dks.append(dk)
        dvs.append(dv)
    dqkv, dnw = _qk_bwd(proj_e, dqs, dks, dvs, rope, wq, wk, T)
    g["e_q_norm_w"] = dnw[0:1, 0:HEAD_DIM]
    g["e_k_norm_w"] = dnw[1:2, 0:HEAD_DIM]
    pieces = [dqkv, drest]
    g["e_w_in"], g16["e_w_in"] = _mm_wgrad(ht_e, pieces, TILE_WG, "wgrad_e_in")
    half_e, half_e16 = _reduce_start(EVEN_MATS, g, g16, cidx)
    dx, g["e_norm_w"], blocks_e = _mm_nt_rms(pieces, wb["e_w_in"], x, p["e_norm_w"], dx1, TILE_MM, "dx_even",
                                             EVEN_MATS, half_e16)
    parts = {}
    for names, halves, blocks in ((ODD_MATS, half_o, blocks_o), (EVEN_MATS, half_e, blocks_e)):
        for n, h, r in zip(names, halves, blocks):
            parts[n] = _add_blocks(bidx, h, r, n)
    return lsum, dx, g, parts


BIG = ("e_w_in", "e_w_out", "o_w_in", "o_w_out")
SHARD_AXIS = {"e_w_in": 1, "e_w_out": 0, "o_w_in": 1, "o_w_out": 0, SMALL_PACK: 0}
N_CHIPS = 4


def _place():
    x, y, c = lax.axis_index("x"), lax.axis_index("y"), lax.axis_index("c")
    chips = [(1 - x, y), (x, 1 - y), (1 - x, 1 - y)]
    return x, y, c, chips


def _block_of(ref, name, block):
    rows, cols = ref.shape
    if SHARD_AXIS[name] == 1:
        cw = cols // N_CHIPS
        return ref.at[:, pl.ds(pl.multiple_of(block * cw, LANES), cw)]
    rw = rows // N_CHIPS
    return ref.at[pl.ds(pl.multiple_of(block * rw, rw), rw), :]


def _half_of(ref, name, half):
    rows, cols = ref.shape
    if SHARD_AXIS[name] == 1:
        return ref.at[pl.ds(pl.multiple_of(half * (rows // 2), rows // 2), rows // 2), :]
    return ref.at[:, pl.ds(pl.multiple_of(half * (cols // 2), LANES), cols // 2)]


def _sub(ref, name, block, half):
    rows, cols = ref.shape
    if SHARD_AXIS[name] == 1:
        cw, hr = cols // N_CHIPS, rows // 2
        return ref.at[pl.ds(pl.multiple_of(half * hr, hr), hr), pl.ds(pl.multiple_of(block * cw, LANES), cw)]
    rw, hc = rows // N_CHIPS, cols // 2
    return ref.at[pl.ds(pl.multiple_of(block * rw, rw), rw), pl.ds(pl.multiple_of(half * hc, LANES), hc)]


GATHER_COPIES = 7


class _Gather:
    def __init__(self, names, s_refs, f_refs, send, recv):
        self.names, self.s, self.f, self.send, self.recv = names, s_refs, f_refs, send, recv

    def _copy(self, k, src, dst, to):
        return pltpu.make_async_remote_copy(src_ref=src, dst_ref=dst, send_sem=self.send.at[k],
                                            recv_sem=self.recv.at[k], device_id=to, device_id_type=MESH)

    def _plan(self):
        x, y, c, chips = _place()
        me, sib = 2 * x + y, (x, y, 1 - c)
        first, relay_in, relay, last_in = [], [], [], []
        for wi, n in enumerate(self.names):
            k0 = wi * GATHER_COPIES
            s, f = self.s[wi], self.f[wi]
            own = _block_of(f, n, me)
            first.append(self._copy(k0 + 3, s, own, sib))
            last_in.append(self._copy(k0 + 3, s, own, sib))
            for j, (cx, cy) in enumerate(chips):
                first.append(self._copy(k0 + j, _half_of(s, n, c), _sub(f, n, me, c), (cx, cy, c)))
                mine = _sub(f, n, 2 * cx + cy, c)
                relay_in.append(self._copy(k0 + j, mine, mine, sib))
                relay.append(self._copy(k0 + 4 + j, mine, mine, sib))
                theirs = _sub(f, n, 2 * cx + cy, 1 - c)
                last_in.append(self._copy(k0 + 4 + j, theirs, theirs, sib))
        return first, relay_in, relay, last_in

    N_RELATIONS = 3

    def begin(self, relations=(0, 1, 2), sibling=True):
        first = self._plan()[0]
        for wi in range(len(self.names)):
            mine = first[wi * (1 + self.N_RELATIONS):(wi + 1) * (1 + self.N_RELATIONS)]
            if sibling:
                mine[0].start()
            for j in relations:
                mine[1 + j].start()

    def relay(self, relations=(0, 1, 2)):
        _, relay_in, relay, _ = self._plan()
        for wi in range(len(self.names)):
            for j in relations:
                relay_in[wi * self.N_RELATIONS + j].wait_recv()
                relay[wi * self.N_RELATIONS + j].start()

    def end(self):
        first, _, relay, last_in = self._plan()
        for cp in last_in:
            cp.wait_recv()
        for cp in first + relay:
            cp.wait_send()

    def wait_relayed(self, j):
        self._plan()[3][1 + j].wait_recv()

    def end_rest(self):
        first, _, relay, last_in = self._plan()
        last_in[0].wait_recv()
        for cp in first + relay:
            cp.wait_send()


def _full_shape(n, s):
    r, cdim = s.shape
    return jax.ShapeDtypeStruct((r, cdim * N_CHIPS) if SHARD_AXIS[n] == 1 else (r * N_CHIPS, cdim), s.dtype)


def _gather_sems(names):
    k = GATHER_COPIES * len(names)
    return [pltpu.SemaphoreType.DMA((k,)), pltpu.SemaphoreType.DMA((k,))]


def _scatter_copies(names, h_refs, r_refs, send, recv):
    _, _, c, chips = _place()
    cps = []
    for wi, n in enumerate(names):
        for j, (cx, cy) in enumerate(chips):
            cps.append(pltpu.make_async_remote_copy(
                src_ref=_block_of(h_refs[wi], n, 2 * cx + cy), dst_ref=r_refs[wi].at[j],
                send_sem=send.at[wi * 3 + j], recv_sem=recv.at[wi * 3 + j],
                device_id=(cx, cy, c), device_id_type=MESH))
    return cps


def _scatter_sems(names):
    return [pltpu.SemaphoreType.DMA((3 * len(names),)), pltpu.SemaphoreType.DMA((3 * len(names),))]


class _SmallSum:
    def __init__(self, p_ref, o_ref, sbuf, cbuf, send, recv):
        self.p, self.o, self.sbuf, self.cbuf, self.send, self.recv = p_ref, o_ref, sbuf, cbuf, send, recv

    def _copy(self, k, ref, to):
        return pltpu.make_async_remote_copy(src_ref=ref, dst_ref=ref, send_sem=self.send.at[k],
                                            recv_sem=self.recv.at[k], device_id=to, device_id_type=MESH)

    def _plan(self):
        x, y, c, chips = _place()
        me, sib = 2 * x + y, (x, y, 1 - c)
        d2d_out = self._copy(0, self.sbuf.at[c], sib)
        d2d_in = self._copy(0, self.sbuf.at[1 - c], sib)
        ici_out = [self._copy(1 + j, self.cbuf.at[me], (cx, cy, c)) for j, (cx, cy) in enumerate(chips)]
        ici_in = [self._copy(1 + j, self.cbuf.at[2 * cx + cy], (cx, cy, c)) for j, (cx, cy) in enumerate(chips)]
        return c, me, d2d_out, d2d_in, ici_out, ici_in

    def begin(self):
        c, _, d2d_out, _, _, _ = self._plan()
        self.sbuf[c] = self.p[...]
        d2d_out.start()

    def middle(self):
        _, me, _, d2d_in, ici_out, _ = self._plan()
        d2d_in.wait_recv()
        self.cbuf[me] = self.sbuf[0] + self.sbuf[1]
        for cp in ici_out:
            cp.start()

    def end(self):
        _, _, d2d_out, _, ici_out, ici_in = self._plan()
        for cp in ici_in:
            cp.wait_recv()
        self.o[...] = (self.cbuf[0] + self.cbuf[1]) + (self.cbuf[2] + self.cbuf[3])
        for cp in [d2d_out] + ici_out:
            cp.wait_send()


def _small_sum_scratch(R):
    return [pltpu.VMEM((2, R, LANES), F32), pltpu.VMEM((N_CHIPS, R, LANES), F32),
            pltpu.SemaphoreType.DMA((4,)), pltpu.SemaphoreType.DMA((4,))]


def _half_shape(shape, name):
    r, cdim = shape
    return (r // 2, cdim) if SHARD_AXIS[name] == 1 else (r, cdim // 2)


def _shard_shape(shape, name):
    r, cdim = shape
    return (r, cdim // N_CHIPS) if SHARD_AXIS[name] == 1 else (r // N_CHIPS, cdim)


def _swap_to_sibling(names, srcs, name, pick_half, small=None):
    nw = len(names)
    ns = 0 if small is None else 1
    vm = pl.BlockSpec(memory_space=pltpu.VMEM)

    def body(*refs):
        g_refs = refs[:nw]
        r_refs = refs[nw + ns:2 * nw + ns]
        send, recv = refs[2 * nw + 2 * ns:2 * nw + 2 * ns + 2]
        x, y, c, _ = _place()
        sib = (x, y, 1 - c)
        cps = []
        for wi, n in enumerate(names):
            src = _half_of(g_refs[wi], n, 1 - c) if pick_half else g_refs[wi]
            cp = pltpu.make_async_remote_copy(src_ref=src, dst_ref=r_refs[wi], send_sem=send.at[wi],
                                              recv_sem=recv.at[wi], device_id=sib, device_id_type=MESH)
            cp.start()
            cps.append(cp)
        if ns:
            total = _SmallSum(refs[nw], refs[2 * nw + ns], *refs[2 * nw + 2 * ns + 2:])
            total.begin()
            total.middle()
            total.end()
        for cp in cps:
            cp.wait()

    outs = [jax.ShapeDtypeStruct(_half_shape(g.shape, n) if pick_half else g.shape, g.dtype)
            for n, g in zip(names, srcs)]
    return pl.pallas_call(
        body, name=name, in_specs=[ANY] * nw + [vm] * ns, out_specs=[ANY] * nw + [vm] * ns,
        out_shape=outs + ([jax.ShapeDtypeStruct(small.shape, F32)] if ns else []),
        scratch_shapes=[pltpu.SemaphoreType.DMA((nw,)), pltpu.SemaphoreType.DMA((nw,))] +
        (_small_sum_scratch(small.shape[0]) if ns else []),
    )(*srcs, *([small] if ns else []))


def _add_half(cidx, g, r, name):
    rows, cols = r.shape
    tr = 256
    tc = cols if cols <= 1792 else (1792 if cols % 1792 == 0 else 1280)
    nr, nc = rows // tr, cols // tc

    def body(c_ref, g_ref, r_ref, o_ref, ob_ref):
        s = g_ref[...] + r_ref[...].astype(F32)
        o_ref[...] = s
        ob_ref[...] = s.astype(BF16)

    if SHARD_AXIS[name] == 1:
        gmap = lambda i, j, c_ref: (c_ref[0] * nr + i, j)
    else:
        gmap = lambda i, j, c_ref: (i, c_ref[0] * nc + j)
    same = lambda i, j, c_ref: (i, j)
    return pl.pallas_call(
        body, name="add_half_" + name,
        grid_spec=pltpu.PrefetchScalarGridSpec(
            num_scalar_prefetch=1, grid=(nr, nc),
            in_specs=[pl.BlockSpec((tr, tc), gmap), pl.BlockSpec((tr, tc), same)],
            out_specs=[pl.BlockSpec((tr, tc), same), pl.BlockSpec((tr, tc), same)]),
        out_shape=[jax.ShapeDtypeStruct(r.shape, F32), jax.ShapeDtypeStruct(r.shape, BF16)],
        compiler_params=pltpu.CompilerParams(dimension_semantics=("parallel", "parallel"), vmem_limit_bytes=VMEM_LIMIT),
    )(cidx, g, r)


def _add_blocks(bidx, h, r, name):
    _, rows, cols = r.shape
    tr = min(rows, 256)
    nr = rows // tr

    def body(b_ref, h_ref, r0, r1, r2, o_ref):
        o_ref[...] = ((h_ref[...] + r0[0].astype(F32)) + r1[0].astype(F32)) + r2[0].astype(F32)

    if SHARD_AXIS[name] == 1:
        hmap = lambda i, b_ref: (i, b_ref[0])
    else:
        hmap = lambda i, b_ref: (b_ref[0] * nr + i, 0)
    rspec = lambda j: pl.BlockSpec((1, tr, cols), lambda i, b_ref, j=j: (j, i, 0))
    return pl.pallas_call(
        body, name="add_blocks_" + name,
        grid_spec=pltpu.PrefetchScalarGridSpec(
            num_scalar_prefetch=1, grid=(nr,),
            in_specs=[pl.BlockSpec((tr, cols), hmap), rspec(0), rspec(1), rspec(2)],
            out_specs=pl.BlockSpec((tr, cols), lambda i, b_ref: (i, 0))),
        out_shape=jax.ShapeDtypeStruct((rows, cols), F32),
        compiler_params=pltpu.CompilerParams(dimension_semantics=("parallel",), vmem_limit_bytes=VMEM_LIMIT),
    )(bidx, h, r, r, r)


def _adam_math(w, g, m, v):
    c1 = 1.0 - ADAM_B1 ** ADAM_STEP
    c2 = 1.0 - ADAM_B2 ** ADAM_STEP
    nm = ADAM_B1 * m + (1.0 - ADAM_B1) * g
    nv = ADAM_B2 * v + (1.0 - ADAM_B2) * (g * g)
    delta = -ADAM_LR * ((nm / c1) / (jnp.sqrt(nv / c2) + ADAM_EPS) + ADAM_WD * w)
    return delta, nm, nv


def _adamw(w, g, m, v, name):
    def body(w_ref, g_ref, m_ref, v_ref, d_ref, nm_ref, nv_ref):
        d_ref[...], nm_ref[...], nv_ref[...] = _adam_math(w_ref[...], g_ref[...], m_ref[...], v_ref[...])

    spec = pl.BlockSpec(w.shape, lambda i: (0, 0))
    return _call(body, "adamw_" + name, (1,), [spec] * 4, [spec] * 3,
                 [jax.ShapeDtypeStruct(w.shape, F32)] * 3, sem=("arbitrary",))(w, g, m, v)


def _adamw_halves(cidx, w, mine, theirs, m, v, name):
    hr, hc = mine.shape
    tr = 128
    ni = hr // tr
    if SHARD_AXIS[name] == 1:
        wmap = lambda hh, i, c_ref: (hh * ni + i, 0)
    else:
        wmap = lambda hh, i, c_ref: (i, hh)
    hmap = lambda hh, i, c_ref: (i, 0)

    def body(c_ref, w_ref, a_ref, b_ref, m_ref, v_ref, g_ref, d_ref, nm_ref, nv_ref):
        g = jnp.where(pl.program_id(0) == c_ref[0], a_ref[...], b_ref[...])
        g_ref[...] = g
        d_ref[...], nm_ref[...], nv_ref[...] = _adam_math(w_ref[...], g, m_ref[...], v_ref[...])

    wspec = pl.BlockSpec((tr, hc), wmap)
    hspec = pl.BlockSpec((tr, hc), hmap)
    return pl.pallas_call(
        body, name="adamw_" + name,
        grid_spec=pltpu.PrefetchScalarGridSpec(
            num_scalar_prefetch=1, grid=(2, ni),
            in_specs=[wspec, hspec, hspec, wspec, wspec], out_specs=[wspec] * 4),
        out_shape=[jax.ShapeDtypeStruct(w.shape, F32)] * 4,
        compiler_params=pltpu.CompilerParams(dimension_semantics=("parallel", "parallel"), vmem_limit_bytes=VMEM_LIMIT),
    )(cidx, w, mine, theirs, m, v)


SMALL = ("e_norm_w", "e_q_norm_w", "e_k_norm_w", "e_conv_w", "o_norm_w", "o_pool_w", "o_pool_scale",
         "o_dconv_w", "o_dconv_b", "o_ln_w", "o_ln_b")
SMALL_SHARDED = ("e_conv_w", "o_norm_w", "o_pool_scale", "o_dconv_w", "o_dconv_b", "o_ln_w", "o_ln_b")
WEIGHTS = ("e_norm_w", "e_w_in", "e_q_norm_w", "e_k_norm_w", "e_conv_w", "e_w_out", "o_norm_w", "o_w_in",
           "o_pool_w", "o_pool_scale", "o_dconv_w", "o_dconv_b", "o_ln_w", "o_ln_b", "o_w_out")


def _pack(arrs):
    flat = jnp.concatenate([a.reshape(-1) for a in arrs])
    rows = -(-flat.shape[0] // (LANES * SUBLANES)) * SUBLANES
    flat = jnp.pad(flat, (0, rows * LANES - flat.shape[0]))
    return flat.reshape(rows, LANES)


def _unpack(packed, shapes):
    flat = packed.reshape(-1)
    out, off = [], 0
    for s in shapes:
        n = int(np.prod(s))
        out.append(flat[off:off + n].reshape(s))
        off += n
    return out


def _gather_last(a, block, width):
    return lax.dynamic_slice_in_dim(a, block * width, width, axis=a.ndim - 1)


def kernel(x, positions, e_norm_w, e_w_in, e_q_norm_w, e_k_norm_w, e_conv_w, e_w_out, o_norm_w, o_w_in, o_pool_w, o_pool_scale, o_dconv_w, o_dconv_b, o_ln_w, o_ln_b, o_w_out, loss_target, m_e_norm_w, m_e_w_in, m_e_q_norm_w, m_e_k_norm_w, m_e_conv_w, m_e_w_out, m_o_norm_w, m_o_w_in, m_o_pool_w, m_o_pool_scale, m_o_dconv_w, m_o_dconv_b, m_o_ln_w, m_o_ln_b, m_o_w_out, v_e_norm_w, v_e_w_in, v_e_q_norm_w, v_e_k_norm_w, v_e_conv_w, v_e_w_out, v_o_norm_w, v_o_w_in, v_o_pool_w, v_o_pool_scale, v_o_dconv_w, v_o_dconv_b, v_o_ln_w, v_o_ln_b, v_o_w_out):
    given = dict(e_norm_w=e_norm_w, e_w_in=e_w_in, e_q_norm_w=e_q_norm_w, e_k_norm_w=e_k_norm_w, e_conv_w=e_conv_w,
                 e_w_out=e_w_out, o_norm_w=o_norm_w, o_w_in=o_w_in, o_pool_w=o_pool_w, o_pool_scale=o_pool_scale,
                 o_dconv_w=o_dconv_w, o_dconv_b=o_dconv_b, o_ln_w=o_ln_w, o_ln_b=o_ln_b, o_w_out=o_w_out)
    mom = dict(e_norm_w=m_e_norm_w, e_w_in=m_e_w_in, e_q_norm_w=m_e_q_norm_w, e_k_norm_w=m_e_k_norm_w,
               e_conv_w=m_e_conv_w, e_w_out=m_e_w_out, o_norm_w=m_o_norm_w, o_w_in=m_o_w_in, o_pool_w=m_o_pool_w,
               o_pool_scale=m_o_pool_scale, o_dconv_w=m_o_dconv_w, o_dconv_b=m_o_dconv_b, o_ln_w=m_o_ln_w,
               o_ln_b=m_o_ln_b, o_w_out=m_o_w_out)
    var = dict(e_norm_w=v_e_norm_w, e_w_in=v_e_w_in, e_q_norm_w=v_e_q_norm_w, e_k_norm_w=v_e_k_norm_w,
               e_conv_w=v_e_conv_w, e_w_out=v_e_w_out, o_norm_w=v_o_norm_w, o_w_in=v_o_w_in, o_pool_w=v_o_pool_w,
               o_pool_scale=v_o_pool_scale, o_dconv_w=v_o_dconv_w, o_dconv_b=v_o_dconv_b, o_ln_w=v_o_ln_w,
               o_ln_b=v_o_ln_b, o_w_out=v_o_w_out)
    S = x.shape[1]
    mx, my, mc = lax.axis_index("x"), lax.axis_index("y"), lax.axis_index("c")
    chip = 2 * mx + my
    cidx = jnp.reshape(mc, (1,)).astype(jnp.int32)
    bidx = jnp.reshape(chip, (1,)).astype(jnp.int32)

    shards = {n: given[n][0].astype(BF16) for n in BIG}
    shard_sizes = [int(np.prod(given[n].shape)) for n in SMALL_SHARDED]
    flat = jnp.concatenate([given[n].reshape(-1) for n in SMALL_SHARDED])
    rows = -(-flat.shape[0] // (SMALL_PACK_W * SUBLANES)) * SUBLANES
    shards[SMALL_PACK] = jnp.pad(flat, (0, rows * SMALL_PACK_W - flat.shape[0])).reshape(rows, SMALL_PACK_W)

    def unpack_small(full):
        gathered = full.reshape(N_CHIPS, rows * SMALL_PACK_W)
        out, off = {}, 0
        for n, size in zip(SMALL_SHARDED, shard_sizes):
            sh = given[n].shape[1:]
            parts = gathered[:, off:off + size].reshape((N_CHIPS,) + sh)
            fullp = jnp.moveaxis(parts, 0, -2).reshape(sh[:-1] + (N_CHIPS * sh[-1],))
            out[n] = fullp.reshape(-1, fullp.shape[-1])
            off += size
        return out

    p = dict(e_norm_w=e_norm_w, e_q_norm_w=e_q_norm_w, e_k_norm_w=e_k_norm_w, o_pool_w=o_pool_w[0])
    lsum, dx, g, parts = _local_step(x[0], positions.reshape(S, 1), loss_target[0], shards, p, unpack_small,
                                     cidx, bidx)
    mine = [g[n] for n in SMALL] + [(0.5 / float(D_MODEL)) * jnp.sum(lsum, keepdims=True)]
    *theirs, tot = _swap_to_sibling(BIG, [parts[n] for n in BIG], "swap_reduced", False, small=_pack(mine))
    tot = _unpack(tot, [a.shape for a in mine])
    loss = tot[-1].reshape(())

    grads, delta, new_m, new_v = {}, {}, {}, {}
    for n, other in zip(BIG, theirs):
        sh = given[n].shape
        outs = _adamw_halves(cidx, given[n][0], parts[n], other, mom[n][0], var[n][0], n)
        grads[n], delta[n], new_m[n], new_v[n] = [a.reshape(sh) for a in outs]
    for n, gv in zip(SMALL, tot):
        if n in SMALL_SHARDED:
            gv = _gather_last(gv, chip, gv.shape[-1] // N_CHIPS)
        grads[n] = gv.reshape(given[n].shape)
    big_small = "o_pool_w"
    pw = [src[big_small].reshape(-1, LANES) for src in (given, grads, mom, var)]
    for dst, a in zip((delta, new_m, new_v), _adamw(*pw, "pool_w")):
        dst[big_small] = a.reshape(given[big_small].shape)
    tiny = tuple(n for n in SMALL if n != big_small)
    shapes = [given[n].shape for n in tiny]
    packed = [_pack([src[n] for n in tiny]) for src in (given, grads, mom, var)]
    for dst, pk in zip((delta, new_m, new_v), _adamw(*packed, "small")):
        for n, a in zip(tiny, _unpack(pk, shapes)):
            dst[n] = a
    return (loss, dx[None], *[grads[n] for n in WEIGHTS], *[delta[n] for n in WEIGHTS],
            *[new_m[n] for n in WEIGHTS], *[new_v[n] for n in WEIGHTS])
```

```python
import numpy as np
import jax
import jax.numpy as jnp
from jax import lax
from jax.experimental import pallas as pl
from jax.experimental.pallas import tpu as pltpu

F32 = jnp.float32
BF16 = jnp.bfloat16

D_MODEL = 1024
HEAD_DIM = 64
A_WIDTH = 512
A_HEADS = 8
A_GROUPS = ((128, 1), (512, 4), (2048, 16))
QBLK = 128
ROT_DIM = 16
ROPE_THETA = 500000.0
POOL_SIZES = (2, 4, 8, 16)
D_CONV = 31
SC_WIDTH = 3
EVEN_IN = 7168
ODD_IN = 2560
EPS = 1e-6
NEG = -1e30
ADAM_LR, ADAM_B1, ADAM_B2, ADAM_EPS, ADAM_WD, ADAM_STEP = 0.001, 0.9, 0.999, 1e-08, 0.01, 10

LANES = 128
SUBLANES = 8
HALO = 32
VMEM_LIMIT = 52 * 1024 * 1024
MESH = pl.DeviceIdType.MESH
ANY = pl.BlockSpec(memory_space=pl.ANY)

NT_DIMS = (((1,), (1,)), ((), ()))
TN_DIMS = (((0,), (0,)), ((), ()))


def _call(body, name, grid, in_specs, out_specs, out_shape, scratch=(), sem=None, aliases=None):
    return pl.pallas_call(
        body, name=name, grid=grid, in_specs=in_specs, out_specs=out_specs, out_shape=out_shape,
        scratch_shapes=list(scratch), input_output_aliases=aliases or {},
        compiler_params=pltpu.CompilerParams(dimension_semantics=sem, vmem_limit_bytes=VMEM_LIMIT))


def _sig(v):
    return jax.nn.sigmoid(v)


def _dsilu(v, s):
    return s * (1.0 + v * (1.0 - s))


def _out_projection(ut_ref, w_ref, lo, hi):
    acc = None
    for k, v in enumerate((lo, hi)):
        ut_ref[k * A_WIDTH:(k + 1) * A_WIDTH, :] = v.T.astype(BF16)
        part = jnp.dot(v.astype(BF16), w_ref[k * A_WIDTH:(k + 1) * A_WIDTH, :], preferred_element_type=F32)
        acc = part if acc is None else acc + part
    return acc


def _cs8(v):
    return v.reshape(v.shape[0] // SUBLANES, SUBLANES, v.shape[1]).sum(axis=0)


def _seg_mean():
    r = lax.broadcasted_iota(jnp.int32, (LANES, LANES), 0) // HEAD_DIM
    c = lax.broadcasted_iota(jnp.int32, (LANES, LANES), 1) // HEAD_DIM
    return jnp.where(r == c, 1.0 / HEAD_DIM, 0.0).astype(BF16)


def _segsum(v, ones):
    hi = v.astype(BF16)
    lo = (v - hi.astype(F32)).astype(BF16)
    return (jnp.dot(hi, ones, preferred_element_type=F32) + jnp.dot(lo, ones, preferred_element_type=F32))


def _head_rstd(v, seg_mean):
    return lax.rsqrt(jnp.dot((v * v).astype(BF16), seg_mean, preferred_element_type=F32) + EPS)


def _rope_tables(pos_ref, freq_ref):
    ang = pos_ref[...].astype(F32) * freq_ref[...]
    cosv, sinv = jnp.cos(ang), jnp.sin(ang)
    lm = lax.broadcasted_iota(jnp.int32, ang.shape, 1) % HEAD_DIM
    half = ROT_DIM // 2
    c = jnp.where(lm < ROT_DIM, cosv, 1.0)
    s1 = jnp.where((lm >= half) & (lm < ROT_DIM), sinv, 0.0)
    s2 = jnp.where(lm < half, -sinv, 0.0)
    return c, s1, s2


def _freq_table():
    half = ROT_DIM // 2
    inv = ROPE_THETA ** (-np.arange(half, dtype=np.float64) / half)
    lane = np.arange(LANES) % HEAD_DIM
    f = np.where(lane < ROT_DIM, inv[lane % half], 0.0)
    return jnp.asarray(f.reshape(1, LANES), F32)


def _load_once(hbm_ref, vmem_ref, sem):
    @pl.when(pl.program_id(0) == 0)
    def _():
        cp = pltpu.make_async_copy(hbm_ref, vmem_ref, sem)
        cp.start()
        cp.wait()


def _rms_rows(x_ref, nw_ref):
    xv = x_ref[...]
    ms = jnp.mean(xv * xv, axis=-1, keepdims=True)
    return xv * lax.rsqrt(ms + EPS) * nw_ref[...]


def _inproj(x, nw, w, tm, tn, name):
    S, N = x.shape[0], w.shape[1]

    def body(x_ref, nw_ref, w_hbm, o_ref, ht_ref, w_v, sem):
        _load_once(w_hbm, w_v, sem)
        h = _rms_rows(x_ref, nw_ref)
        ht_ref[...] = h.T.astype(BF16)
        hb = h.astype(BF16)
        for j in range(N // tn):
            o_ref[:, j * tn:(j + 1) * tn] = jnp.dot(hb, w_v[:, j * tn:(j + 1) * tn], preferred_element_type=F32)

    return _call(
        body, name, (S // tm,),
        [pl.BlockSpec((tm, D_MODEL), lambda i: (i, 0)),
         pl.BlockSpec((1, D_MODEL), lambda i: (0, 0)), ANY],
        [pl.BlockSpec((tm, N), lambda i: (i, 0)),
         pl.BlockSpec((D_MODEL, tm), lambda i: (0, i))],
        [jax.ShapeDtypeStruct((S, N), F32), jax.ShapeDtypeStruct((D_MODEL, S), BF16)],
        scratch=[pltpu.VMEM(w.shape, BF16), pltpu.SemaphoreType.DMA(())], sem=("arbitrary",))(x, nw, w)


def _inproj_gathering(x, nw, shard, bidx, first, late_names, late_shards, tm, name):
    S = x.shape[0]
    ni = S // tm
    K, cw = shard.shape
    nl = len(late_names)
    last = N_CHIPS - 1

    def body(b_ref, x_ref, nw_ref, s_hbm, *rest):
        ls_refs = rest[:nl]
        o_ref, ht_ref, f_hbm = rest[nl:nl + 3]
        lf_refs = rest[nl + 3:2 * nl + 3]
        hs, w_blk, lsem, send1, recv1, send2, recv2 = rest[2 * nl + 3:]
        j, i = pl.program_id(0), pl.program_id(1)
        g1 = _Gather((first,), (s_hbm,), (f_hbm,), send1, recv1)
        g2 = _Gather(late_names, ls_refs, lf_refs, send2, recv2)
        _, _, _, chips = _place()

        def load_block(src):
            cp = pltpu.make_async_copy(src, w_blk, lsem)
            cp.start()
            cp.wait()

        @pl.when((j == 0) & (i == 0))
        def _():
            g1.begin(relations=(0, 1))
            load_block(s_hbm)

        for r, (cx, cy) in enumerate(chips):
            @pl.when((j == r + 1) & (i == 0))
            def _(r=r, cx=cx, cy=cy):
                g1.wait_relayed(r)
                load_block(_block_of(f_hbm, first, 2 * cx + cy))

        @pl.when((j == 2) & (i == 0))
        def _():
            g1.relay(relations=(2,))
            g2.begin()

        pl.when((j == last) & (i == ni // 2))(g2.relay)

        rows = pl.ds(pl.multiple_of(i * tm, tm), tm)

        @pl.when(j == 0)
        def _():
            h = _rms_rows(x_ref, nw_ref)
            hs[rows, :] = h.astype(BF16)
            ht_ref[...] = h.T.astype(BF16)

        o_ref[...] = jnp.dot(hs[rows, :], w_blk[...], preferred_element_type=F32)

        @pl.when((j == 0) & (i == ni - 1))
        def _():
            g1.relay(relations=(0, 1))
            g1.begin(relations=(2,), sibling=False)

        @pl.when((j == last) & (i == ni - 1))
        def _():
            g1.end_rest()
            g2.end()

    def block_of_step(j, b_ref):
        return jnp.bitwise_xor(b_ref[0], jnp.bitwise_or(jnp.left_shift(jnp.bitwise_and(j, 1), 1), jnp.right_shift(j, 1)))

    outs = pl.pallas_call(
        body, name=name,
        grid_spec=pltpu.PrefetchScalarGridSpec(
            num_scalar_prefetch=1, grid=(N_CHIPS, ni),
            in_specs=[pl.BlockSpec((tm, D_MODEL), lambda j, i, b: (jnp.where(j == 0, i, 0), 0)),
                      pl.BlockSpec((1, D_MODEL), lambda j, i, b: (0, 0)), ANY] + [ANY] * nl,
            out_specs=[pl.BlockSpec((tm, cw), lambda j, i, b: (i, block_of_step(j, b))),
                       pl.BlockSpec((D_MODEL, tm), lambda j, i, b: (0, jnp.where(j == 0, i, ni - 1))),
                       ANY] + [ANY] * nl,
            scratch_shapes=[pltpu.VMEM((S, D_MODEL), BF16), pltpu.VMEM((K, cw), BF16), pltpu.SemaphoreType.DMA(())] +
            _gather_sems((first,)) + _gather_sems(late_names)),
        out_shape=[jax.ShapeDtypeStruct((S, cw * N_CHIPS), F32), jax.ShapeDtypeStruct((D_MODEL, S), BF16),
                   _full_shape(first, shard)] + [_full_shape(n, s) for n, s in zip(late_names, late_shards)],
        compiler_params=pltpu.CompilerParams(dimension_semantics=("arbitrary", "arbitrary"),
                                             vmem_limit_bytes=VMEM_LIMIT),
    )(bidx, x, nw, shard, *late_shards)
    return outs[0], outs[1], outs[2], list(outs[3:])


def _piece_blocks(pieces, tk, axis):
    starts, counts, s = [], [], 0
    for p in pieces:
        n = p.shape[axis] // tk
        starts.append(s)
        counts.append(n)
        s += n
    return starts, counts, s


def _mm_nt_rms(pieces, w, x, nw, dres, tm, name, scatter_names=(), scatter_halves=()):
    S = x.shape[0]
    npc = len(pieces)
    ni = S // tm
    ns = len(scatter_names)
    offs = np.cumsum([0] + [p.shape[1] for p in pieces]).tolist()

    def body(*refs):
        p_refs = refs[:npc]
        w_hbm, x_ref, nw_ref, dr_ref = refs[npc:npc + 4]
        h_refs = refs[npc + 4:npc + 4 + ns]
        dx_ref, dnw_ref = refs[npc + 4 + ns:npc + 6 + ns]
        r_refs = refs[npc + 6 + ns:npc + 6 + 2 * ns]
        w_v, sem, nacc = refs[npc + 6 + 2 * ns:npc + 9 + 2 * ns]
        i = pl.program_id(0)
        if ns:
            send, recv = refs[npc + 9 + 2 * ns:]

            @pl.when(i == 0)
            def _():
                for cp in _scatter_copies(scatter_names, h_refs, r_refs, send, recv):
                    cp.start()
        _load_once(w_hbm, w_v, sem)

        @pl.when(i == 0)
        def _():
            nacc[...] = jnp.zeros_like(nacc)

        dh = None
        for p in range(npc):
            part = lax.dot_general(p_refs[p][...].astype(BF16), w_v[:, offs[p]:offs[p + 1]], NT_DIMS,
                                   preferred_element_type=F32)
            dh = part if dh is None else dh + part
        xv = x_ref[...]
        rs = lax.rsqrt(jnp.mean(xv * xv, axis=-1, keepdims=True) + EPS)
        xh = xv * rs
        nacc[...] += _cs8(dh * xh)
        dxh = dh * nw_ref[...]
        dx_ref[...] = dr_ref[...] + rs * (dxh - xh * jnp.mean(dxh * xh, axis=-1, keepdims=True))

        @pl.when(i == ni - 1)
        def _():
            dnw_ref[...] = jnp.sum(nacc[...], axis=0, keepdims=True)
            if ns:
                for cp in _scatter_copies(scatter_names, h_refs, r_refs, send, recv):
                    cp.wait()

    row = pl.BlockSpec((tm, D_MODEL), lambda i: (i, 0))
    outs = _call(
        body, name, (ni,),
        [pl.BlockSpec((tm, p.shape[1]), lambda i: (i, 0)) for p in pieces] +
        [ANY, row, pl.BlockSpec((1, D_MODEL), lambda i: (0, 0)), row] + [ANY] * ns,
        [row, pl.BlockSpec((1, D_MODEL), lambda i: (0, 0))] + [ANY] * ns,
        [jax.ShapeDtypeStruct((S, D_MODEL), F32), jax.ShapeDtypeStruct((1, D_MODEL), F32)] +
        [jax.ShapeDtypeStruct((3,) + _shard_shape(h.shape, n), h.dtype) for n, h in zip(scatter_names, scatter_halves)],
        scratch=[pltpu.VMEM(w.shape, BF16), pltpu.SemaphoreType.DMA(()), pltpu.VMEM((SUBLANES, D_MODEL), F32)] +
        (_scatter_sems(scatter_names) if ns else []),
        sem=("arbitrary",))(*pieces, w, x, nw, dres, *scatter_halves)
    return outs[0], outs[1], list(outs[2:])


def _mm_wgrad(at, pieces, tn, name):
    M, S = at.shape
    starts, counts, nj = _piece_blocks(pieces, tn, 1)
    npc = len(pieces)

    def body(*refs):
        a_hbm = refs[0]
        p_refs = refs[1:1 + npc]
        o_ref, o16_ref, a_v, sem = refs[1 + npc:]
        j = pl.program_id(0)
        _load_once(a_hbm, a_v, sem)
        for p in range(npc):
            @pl.when((j >= starts[p]) & (j < starts[p] + counts[p]))
            def _(p=p):
                acc = jnp.dot(a_v[...], p_refs[p][...].astype(BF16), preferred_element_type=F32)
                o_ref[...] = acc
                o16_ref[...] = acc.astype(BF16)

    def pspec(p):
        return pl.BlockSpec((S, tn), lambda j: (0, jnp.clip(j - starts[p], 0, counts[p] - 1)))

    col = pl.BlockSpec((M, tn), lambda j: (0, j))
    return _call(
        body, name, (nj,),
        [ANY] + [pspec(p) for p in range(npc)], [col, col],
        [jax.ShapeDtypeStruct((M, nj * tn), F32), jax.ShapeDtypeStruct((M, nj * tn), BF16)],
        scratch=[pltpu.VMEM(at.shape, BF16), pltpu.SemaphoreType.DMA(())], sem=("arbitrary",))(at, *pieces)


def _stream_spec(d, T):
    return pl.BlockSpec((d, T // d, A_WIDTH), lambda i: (0, i, 0))


def _stream_shape(d, S, dtype):
    return jax.ShapeDtypeStruct((d, S // d, A_WIDTH), dtype)


N_CHUNK = A_WIDTH // LANES


def _to_tokens(ref, scr, d, T):
    if d == 1:
        return ref[0].astype(F32)
    for r in range(d):
        for ch in range(N_CHUNK):
            scr.at[ch][pl.ds(r, T // d, stride=d), :] = ref[r, :, ch * LANES:(ch + 1) * LANES].astype(F32)
    return _get(scr)


def _from_tokens(out_ref, scr, d, T):
    for r in range(d):
        for ch in range(N_CHUNK):
            out_ref[r, :, ch * LANES:(ch + 1) * LANES] = scr.at[ch][pl.ds(r, T // d, stride=d), :].astype(out_ref.dtype)


def _put(scr, val):
    for ch in range(N_CHUNK):
        scr[ch] = val[:, ch * LANES:(ch + 1) * LANES]


def _get(scr):
    return jnp.concatenate([scr[ch] for ch in range(N_CHUNK)], axis=1)


def _chunked(T):
    return pltpu.VMEM((N_CHUNK, T, LANES), F32)


def _compact_spec(d, T):
    return pl.BlockSpec((d, T // d, LANES), lambda i: (0, i, 0))


def _compact_shape(d, S):
    return jax.ShapeDtypeStruct((d, S // d, LANES), F32)


def _compact_to_tokens(ref, scr, d, T):
    if d == 1:
        return ref[0]
    for r in range(d):
        scr[pl.ds(r, T // d, stride=d), :] = ref[r]
    return scr[...]


def _compact_from_tokens(out_ref, scr, val, d, T):
    if d == 1:
        out_ref[0] = val
        return
    scr[...] = val
    for r in range(d):
        out_ref[r] = scr[pl.ds(r, T // d, stride=d), :]


def _head_expander():
    r = lax.broadcasted_iota(jnp.int32, (LANES, A_WIDTH), 0)
    c = lax.broadcasted_iota(jnp.int32, (LANES, A_WIDTH), 1) // HEAD_DIM
    return (r == c).astype(BF16)


def _head_reducer():
    r = lax.broadcasted_iota(jnp.int32, (A_WIDTH, LANES), 0) // HEAD_DIM
    c = lax.broadcasted_iota(jnp.int32, (A_WIDTH, LANES), 1)
    return (r == c).astype(BF16)


def _qkv_prep(proj, pos, freq, wq, wk, T):
    S = proj.shape[0]
    qk_w = 3 * A_WIDTH

    def body(q_ref, k_ref, v_ref, pos_ref, f_ref, wq_ref, wk_ref, *rest):
        outs, tabs, scr = rest[:9], rest[9:12], rest[12]
        seg_mean = _seg_mean()
        c, s1, s2 = _rope_tables(pos_ref, f_ref)
        for tab, val in zip(tabs, (c, s1, s2)):
            tab[...] = val
        for t, (src, w_ref) in enumerate(((q_ref, wq_ref), (k_ref, wk_ref), (v_ref, None))):
            for g in range(3):
                d = A_GROUPS[g][1]
                out = outs[3 * t + g]
                for ch in range(A_WIDTH // LANES):
                    cs = slice(ch * LANES, (ch + 1) * LANES)
                    v = src[:, g * A_WIDTH + ch * LANES: g * A_WIDTH + (ch + 1) * LANES]
                    if w_ref is not None:
                        y = v * _head_rstd(v, seg_mean) * w_ref[...]
                        v = y * c + pltpu.roll(y, 8, 1) * s1 + pltpu.roll(y, LANES - 8, 1) * s2
                    if d == 1:
                        out[0, :, cs] = v.astype(BF16)
                    else:
                        scr[ch] = v
                if d > 1:
                    _from_tokens(out, scr, d, T)

    ds_ = [A_GROUPS[g][1] for g in range(3)] * 3
    return _call(
        body, "qkv_prep", (S // T,),
        [pl.BlockSpec((T, qk_w), lambda i: (i, 0)), pl.BlockSpec((T, qk_w), lambda i: (i, 1)),
         pl.BlockSpec((T, qk_w), lambda i: (i, 2)),
         pl.BlockSpec((T, 1), lambda i: (i, 0)), pl.BlockSpec((1, LANES), lambda i: (0, 0)),
         pl.BlockSpec((1, LANES), lambda i: (0, 0)), pl.BlockSpec((1, LANES), lambda i: (0, 0))],
        [_stream_spec(d, T) for d in ds_] + [pl.BlockSpec((T, LANES), lambda i: (i, 0))] * 3,
        [_stream_shape(d, S, BF16) for d in ds_] + [jax.ShapeDtypeStruct((S, LANES), F32)] * 3,
        scratch=[_chunked(T)], sem=("parallel",))(proj, proj, proj, pos, freq, wq, wk)


def _attn_mask(i):
    qi = lax.broadcasted_iota(jnp.int32, (QBLK, 2 * QBLK), 0) + QBLK
    kj = lax.broadcasted_iota(jnp.int32, (QBLK, 2 * QBLK), 1)
    dist = qi - kj
    return (dist >= 0) & (dist <= QBLK) & ((i > 0) | (kj >= QBLK))


ATT_BLK = (None, QBLK, A_WIDTH)
ATT_CBLK = (None, QBLK, LANES)
FWD_BLOCKS = 4


def _first_head_lanes():
    return lax.broadcasted_iota(jnp.int32, (1, LANES), 1) < HEAD_DIM


def _split_heads(v, first):
    zero = jnp.zeros_like(v)
    return jnp.where(first, v, zero), jnp.where(first, zero, v)


def _attn_fwd(q, k, v, g):
    d, n, _ = q.shape
    nb = n // QBLK
    assert nb % FWD_BLOCKS == 0, (n, QBLK)
    rows2 = FWD_BLOCKS * QBLK

    def body(q_ref, kp_ref, kc_ref, vp_ref, vc_ref, o_ref, l_ref, s_scr, p_scr):
        i = pl.program_id(1)
        masks = (_attn_mask(i),) + (_attn_mask(1),) * (FWD_BLOCKS - 1)
        first = _first_head_lanes()

        def keys(prev_ref, cur_ref, b, ps):
            if b == 0:
                return jnp.concatenate([prev_ref[:, ps], cur_ref[0:QBLK, ps]], axis=0)
            return cur_ref[(b - 1) * QBLK:(b + 1) * QBLK, ps]

        for b in range(FWD_BLOCKS):
            rows = slice(b * QBLK, (b + 1) * QBLK)
            for pr in range(A_HEADS // 2):
                ps = slice(pr * LANES, (pr + 1) * LANES)
                kc = keys(kp_ref, kc_ref, b, ps)
                for e, qh in enumerate(_split_heads(q_ref[rows, ps], first)):
                    s_scr[b * A_HEADS + 2 * pr + e] = lax.dot_general(qh, kc, NT_DIMS, preferred_element_type=F32)
        lane = lax.broadcasted_iota(jnp.int32, (1, LANES), 1)
        for b in range(FWD_BLOCKS):
            lrow = jnp.zeros((QBLK, LANES), F32)
            for h in range(A_HEADS):
                s = jnp.where(masks[b], s_scr[b * A_HEADS + h] * (HEAD_DIM ** -0.5), NEG)
                m = jnp.max(s, axis=-1, keepdims=True)
                p = jnp.exp(s - m)
                den = jnp.sum(p, axis=-1, keepdims=True)
                p_scr[b * A_HEADS + h] = (p / den).astype(BF16)
                lrow = jnp.where(lane == h, m + jnp.log(den), lrow)
            l_ref[b * QBLK:(b + 1) * QBLK, :] = lrow
        for b in range(FWD_BLOCKS):
            for pr in range(A_HEADS // 2):
                ps = slice(pr * LANES, (pr + 1) * LANES)
                va, vb = _split_heads(keys(vp_ref, vc_ref, b, ps), first)
                o_ref[b * QBLK:(b + 1) * QBLK, ps] = (
                    jnp.dot(p_scr[b * A_HEADS + 2 * pr], va, preferred_element_type=F32) +
                    jnp.dot(p_scr[b * A_HEADS + 2 * pr + 1], vb, preferred_element_type=F32)).astype(BF16)

    prev = lambda r, i: (r, jnp.maximum(FWD_BLOCKS * i - 1, 0), 0)
    cur = lambda r, i: (r, i, 0)
    wide = (None, rows2, A_WIDTH)
    units = FWD_BLOCKS * A_HEADS
    return _call(
        body, "attn_fwd_g%d" % g, (d, nb // FWD_BLOCKS),
        [pl.BlockSpec(wide, cur), pl.BlockSpec(ATT_BLK, prev), pl.BlockSpec(wide, cur),
         pl.BlockSpec(ATT_BLK, prev), pl.BlockSpec(wide, cur)],
        [pl.BlockSpec(wide, cur), pl.BlockSpec((None, rows2, LANES), cur)],
        [jax.ShapeDtypeStruct((d, n, A_WIDTH), BF16), jax.ShapeDtypeStruct((d, n, LANES), F32)],
        scratch=[pltpu.VMEM((units, QBLK, 2 * QBLK), F32), pltpu.VMEM((units, QBLK, 2 * QBLK), BF16)],
        sem=("parallel", "parallel"))(q, k, k, v, v)


def _attn_bwd(q, k, v, do, lse, cg, g):
    d, n, _ = q.shape
    nb = n // QBLK
    scale = HEAD_DIM ** -0.5

    def body(q_ref, kp_ref, kc_ref, vp_ref, vc_ref, do_ref, l_ref, c_ref, dq_ref, dk_ref, dv_ref, ck, cv,
             s_scr, dp_scr, p_scr, ds_scr):
        i = pl.program_id(1)

        @pl.when(i == 0)
        def _():
            ck[...] = jnp.zeros_like(ck)
            cv[...] = jnp.zeros_like(cv)

        @pl.when(i < nb)
        def _():
            mask = _attn_mask(i)
            first = _first_head_lanes()
            for pr in range(A_HEADS // 2):
                ps = slice(pr * LANES, (pr + 1) * LANES)
                kc = jnp.concatenate([kp_ref[:, ps], kc_ref[:, ps]], axis=0)
                vc = jnp.concatenate([vp_ref[:, ps], vc_ref[:, ps]], axis=0)
                qs = _split_heads(q_ref[:, ps], first)
                dos = _split_heads(do_ref[:, ps], first)
                for e in range(2):
                    s_scr[2 * pr + e] = lax.dot_general(qs[e], kc, NT_DIMS, preferred_element_type=F32)
                    dp_scr[2 * pr + e] = lax.dot_general(dos[e], vc, NT_DIMS, preferred_element_type=F32)
            for h in range(A_HEADS):
                p = jnp.where(mask, jnp.exp(s_scr[h] * scale - l_ref[:, h:h + 1]), 0.0)
                p_scr[h] = p.astype(BF16)
                ds_scr[h] = (p * (dp_scr[h] + c_ref[:, h:h + 1]) * scale).astype(BF16)
            for pr in range(A_HEADS // 2):
                ps = slice(pr * LANES, (pr + 1) * LANES)
                ks = _split_heads(jnp.concatenate([kp_ref[:, ps], kc_ref[:, ps]], axis=0), first)
                qs = _split_heads(q_ref[:, ps], first)
                dos = _split_heads(do_ref[:, ps], first)
                dq = dkc = dvc = None
                for e in range(2):
                    ds = ds_scr[2 * pr + e]
                    a = jnp.dot(ds, ks[e], preferred_element_type=F32)
                    b = lax.dot_general(ds, qs[e], TN_DIMS, preferred_element_type=F32)
                    c = lax.dot_general(p_scr[2 * pr + e], dos[e], TN_DIMS, preferred_element_type=F32)
                    dq, dkc, dvc = (a, b, c) if e == 0 else (dq + a, dkc + b, dvc + c)
                dq_ref[:, ps] = dq.astype(BF16)
                dk_ref[:, ps] = (ck[:, ps] + dkc[:QBLK]).astype(BF16)
                dv_ref[:, ps] = (cv[:, ps] + dvc[:QBLK]).astype(BF16)
                ck[:, ps] = dkc[QBLK:]
                cv[:, ps] = dvc[QBLK:]

        @pl.when(i == nb)
        def _():
            dk_ref[...] = ck[...].astype(BF16)
            dv_ref[...] = cv[...].astype(BF16)

    qi = lambda i: jnp.minimum(i, nb - 1)
    cur = lambda r, i: (r, qi(i), 0)
    prev = lambda r, i: (r, jnp.maximum(qi(i) - 1, 0), 0)
    late = lambda r, i: (r, jnp.maximum(i - 1, 0), 0)
    return _call(
        body, "attn_bwd_g%d" % g, (d, nb + 1),
        [pl.BlockSpec(ATT_BLK, cur), pl.BlockSpec(ATT_BLK, prev), pl.BlockSpec(ATT_BLK, cur),
         pl.BlockSpec(ATT_BLK, prev), pl.BlockSpec(ATT_BLK, cur),
         pl.BlockSpec(ATT_BLK, cur), pl.BlockSpec(ATT_CBLK, cur), pl.BlockSpec(ATT_CBLK, cur)],
        [pl.BlockSpec(ATT_BLK, cur), pl.BlockSpec(ATT_BLK, late), pl.BlockSpec(ATT_BLK, late)],
        [jax.ShapeDtypeStruct((d, n, A_WIDTH), BF16)] * 3,
        scratch=[pltpu.VMEM((QBLK, A_WIDTH), F32), pltpu.VMEM((QBLK, A_WIDTH), F32),
                 pltpu.VMEM((A_HEADS, QBLK, 2 * QBLK), F32), pltpu.VMEM((A_HEADS, QBLK, 2 * QBLK), F32),
                 pltpu.VMEM((A_HEADS, QBLK, 2 * QBLK), BF16), pltpu.VMEM((A_HEADS, QBLK, 2 * QBLK), BF16)],
        sem=("parallel", "arbitrary"))(q, k, k, v, v, do, lse, cg)


def _merge_weights(l0, l1, l2):
    mx = jnp.maximum(jnp.maximum(l0, l1), l2)
    e0, e1, e2 = jnp.exp(l0 - mx), jnp.exp(l1 - mx), jnp.exp(l2 - mx)
    den = e0 + e1 + e2
    return e0 / den, e1 / den, e2 / den


def _even_specs(T, S):
    t8 = T // SUBLANES
    last8 = S // SUBLANES - 1
    col = lambda c: pl.BlockSpec((T, A_WIDTH), lambda i: (i, c))
    prev8 = lambda c: pl.BlockSpec((SUBLANES, A_WIDTH), lambda i: (jnp.maximum(i * t8 - 1, 0), c))
    next8 = lambda c: pl.BlockSpec((SUBLANES, A_WIDTH), lambda i: (jnp.minimum((i + 1) * t8, last8), c))
    return col, prev8, next8


GROUP_D = tuple(d for _, d in A_GROUPS)


def _even_mixer_fwd(x, proj, os_, ls_, conv_w, w_out, T):
    S = proj.shape[0]
    col, prev8, _ = _even_specs(T, S)
    H = SUBLANES

    def body(x_ref, w_ref, bg_r, cg_r, hb_r, zl_r, zh_r, cgp_r, hbp_r, o0, o1, o2, l0, l1, l2, cw_r,
             x1_ref, ut_ref, ext, cscr, *scr):
        i = pl.program_id(0)
        ls = [_compact_to_tokens(r, cscr, GROUP_D[g], T) for g, r in enumerate((l0, l1, l2))]
        expand = _head_expander()
        ws = [_segsum(w, expand) for w in _merge_weights(*ls)]
        oa = ws[0] * _to_tokens(o0, scr[0], GROUP_D[0], T)
        oa = oa + ws[1] * _to_tokens(o1, scr[1], GROUP_D[1], T)
        oa = oa + ws[2] * _to_tokens(o2, scr[2], GROUP_D[2], T)
        ext[0:H, :] = jnp.where(i == 0, 0.0, cgp_r[...] * hbp_r[...])
        ext[H:H + T, :] = cg_r[...] * hb_r[...]
        conv = cw_r[0:1, :] * ext[H - 2:H - 2 + T, :]
        for kk in range(1, SC_WIDTH):
            conv = conv + cw_r[kk:kk + 1, :] * ext[H - 2 + kk:H - 2 + kk + T, :]
        zl, zh = zl_r[...], zh_r[...]
        x1_ref[...] = x_ref[...] + _out_projection(ut_ref, w_ref, oa * (zl * _sig(zl)),
                                                   bg_r[...] * conv * (zh * _sig(zh)))

    streams = [_stream_spec(d, T) for d in GROUP_D]
    compacts = [_compact_spec(d, T) for d in GROUP_D]
    row = pl.BlockSpec((T, D_MODEL), lambda i: (i, 0))
    return _call(
        body, "even_mixer_fwd", (S // T,),
        [row, pl.BlockSpec((D_MODEL, D_MODEL), lambda i: (0, 0)),
         col(9), col(10), col(11), col(12), col(13), prev8(10), prev8(11)] + streams + compacts +
        [pl.BlockSpec((SC_WIDTH, A_WIDTH), lambda i: (0, 0))],
        [row, pl.BlockSpec((D_MODEL, T), lambda i: (0, i))],
        [jax.ShapeDtypeStruct((S, D_MODEL), F32), jax.ShapeDtypeStruct((D_MODEL, S), BF16)],
        scratch=[pltpu.VMEM((T + H, A_WIDTH), F32), pltpu.VMEM((T, LANES), F32)] + [_chunked(T)] * 3,
        sem=("parallel",))(
            x, w_out, proj, proj, proj, proj, proj, proj, proj, *os_, *ls_, conv_w)


def _even_mixer_bwd(dy, w_out, proj, os_, ls_, conv_w, T):
    S = proj.shape[0]
    nt = S // T
    col, prev8, next8 = _even_specs(T, S)
    H = SUBLANES
    t8 = T // SUBLANES
    last8 = S // SUBLANES - 1

    def body(dy_r, dyn_r, w_ref, bg_r, cg_r, hb_r, zl_r, zh_r, cgp_r, hbp_r, zhn_r, bgn_r,
             o0, o1, o2, l0, l1, l2, cw_r,
             do0, do1, do2, c0, c1, c2, dr_ref, dcw_ref, ext_t, ext_d, acc, cscr, s_a, s_b, s_c):
        i = pl.program_id(0)

        @pl.when(i == 0)
        def _():
            acc[...] = jnp.zeros_like(acc)

        zl, zh = zl_r[...], zh_r[...]
        sl, sh = _sig(zl), _sig(zh)
        du = lax.dot_general(dy_r[...].astype(BF16), w_ref[...], NT_DIMS, preferred_element_type=F32)
        dul, duh = du[:, 0:A_WIDTH], du[:, A_WIDTH:]
        dun = lax.dot_general(dyn_r[...].astype(BF16), w_ref[A_WIDTH:, :], NT_DIMS, preferred_element_type=F32)
        scr = (s_a, s_b, s_c)
        ls = [_compact_to_tokens(r, cscr, GROUP_D[g], T) for g, r in enumerate((l0, l1, l2))]
        wcs = _merge_weights(*ls)
        expand = _head_expander()
        ws = [_segsum(w, expand) for w in wcs]
        oa = ws[0] * _to_tokens(o0, scr[0], GROUP_D[0], T)
        oa = oa + ws[1] * _to_tokens(o1, scr[1], GROUP_D[1], T)
        oa = oa + ws[2] * _to_tokens(o2, scr[2], GROUP_D[2], T)
        doa = dul * (zl * sl)
        rsum = _segsum(doa * oa, _head_reducer())
        for g, (do_ref, c_ref) in enumerate(((do0, c0), (do1, c1), (do2, c2))):
            d = GROUP_D[g]
            _compact_from_tokens(c_ref, cscr, -wcs[g] * rsum, d, T)
            if d == 1:
                do_ref[0] = (ws[g] * doa).astype(BF16)
            else:
                _put(s_c, ws[g] * doa)
                _from_tokens(do_ref, s_c, d, T)
        cgv, hbv, bgv = cg_r[...], hb_r[...], bg_r[...]
        ext_t[0:H, :] = jnp.where(i == 0, 0.0, cgp_r[...] * hbp_r[...])
        ext_t[H:H + T, :] = cgv * hbv
        conv = cw_r[0:1, :] * ext_t[H - 2:H - 2 + T, :]
        for kk in range(1, SC_WIDTH):
            conv = conv + cw_r[kk:kk + 1, :] * ext_t[H - 2 + kk:H - 2 + kk + T, :]
        dyb = duh * (zh * sh)
        dconv = dyb * bgv
        zn = zhn_r[...]
        ext_d[0:T, :] = dconv
        ext_d[T:T + H, :] = jnp.where(i == nt - 1, 0.0, dun * (zn * _sig(zn)) * bgn_r[...])
        dt = cw_r[0:1, :] * ext_d[2:2 + T, :]
        for kk in range(1, SC_WIDTH):
            dt = dt + cw_r[kk:kk + 1, :] * ext_d[2 - kk:2 - kk + T, :]
        for kk in range(SC_WIDTH):
            acc[kk * SUBLANES:(kk + 1) * SUBLANES, :] += _cs8(dconv * ext_t[H - 2 + kk:H - 2 + kk + T, :])
        dr_ref[:, 0:A_WIDTH] = (dyb * conv).astype(BF16)
        dr_ref[:, A_WIDTH:2 * A_WIDTH] = (dt * hbv).astype(BF16)
        dr_ref[:, 2 * A_WIDTH:3 * A_WIDTH] = (dt * cgv).astype(BF16)
        dr_ref[:, 3 * A_WIDTH:4 * A_WIDTH] = (dul * oa * _dsilu(zl, sl)).astype(BF16)
        dr_ref[:, 4 * A_WIDTH:5 * A_WIDTH] = (duh * (bgv * conv) * _dsilu(zh, sh)).astype(BF16)

        @pl.when(i == nt - 1)
        def _():
            for kk in range(SC_WIDTH):
                dcw_ref[kk:kk + 1, :] = jnp.sum(acc[kk * SUBLANES:(kk + 1) * SUBLANES, :], axis=0, keepdims=True)

    streams = [_stream_spec(d, T) for d in GROUP_D]
    dynext = pl.BlockSpec((SUBLANES, D_MODEL), lambda i: (jnp.minimum((i + 1) * t8, last8), 0))
    compacts = [_compact_spec(d, T) for d in GROUP_D]
    outs = _call(
        body, "even_mixer_bwd", (nt,),
        [pl.BlockSpec((T, D_MODEL), lambda i: (i, 0)), dynext, pl.BlockSpec((D_MODEL, D_MODEL), lambda i: (0, 0)),
         col(9), col(10), col(11), col(12), col(13), prev8(10), prev8(11), next8(13), next8(9)] +
        streams + compacts + [pl.BlockSpec((SC_WIDTH, A_WIDTH), lambda i: (0, 0))],
        streams + compacts + [pl.BlockSpec((T, 5 * A_WIDTH), lambda i: (i, 0)),
                              pl.BlockSpec((SC_WIDTH, A_WIDTH), lambda i: (0, 0))],
        [_stream_shape(d, S, BF16) for d in GROUP_D] + [_compact_shape(d, S) for d in GROUP_D] +
        [jax.ShapeDtypeStruct((S, 5 * A_WIDTH), BF16), jax.ShapeDtypeStruct((SC_WIDTH, A_WIDTH), F32)],
        scratch=[pltpu.VMEM((T + H, A_WIDTH), F32), pltpu.VMEM((T + H, A_WIDTH), F32),
                 pltpu.VMEM((SC_WIDTH * SUBLANES, A_WIDTH), F32), pltpu.VMEM((T, LANES), F32)] +
                [_chunked(T)] * 3,
        sem=("arbitrary",))(dy, dy, w_out, proj, proj, proj, proj, proj, proj, proj, proj, proj, *os_, *ls_, conv_w)
    return outs[0:3], outs[3:6], outs[6], outs[7]


def _qk_bwd(proj, dqs, dks, dvs, rope, wq, wk, T):
    S = proj.shape[0]
    nt = S // T
    qk_w = 3 * A_WIDTH

    def body(q_ref, k_ref, dq0, dq1, dq2, dk0, dk1, dk2, dv0, dv1, dv2, c_ref, s1_ref, s2_ref, wq_ref, wk_ref,
             o_ref, dw_ref, acc, scr):
        i = pl.program_id(0)

        @pl.when(i == 0)
        def _():
            acc[...] = jnp.zeros_like(acc)
            dw_ref[...] = jnp.zeros_like(dw_ref)

        seg_mean = _seg_mean()
        c, s1, s2 = c_ref[...], s1_ref[...], s2_ref[...]
        for t, (src, w_ref, ds) in enumerate(((q_ref, wq_ref, (dq0, dq1, dq2)), (k_ref, wk_ref, (dk0, dk1, dk2)))):
            wv = w_ref[...]
            for g in range(3):
                d = GROUP_D[g]
                if d > 1:
                    _to_tokens(ds[g], scr, d, T)
                for ch in range(A_WIDTH // LANES):
                    cs = slice(g * A_WIDTH + ch * LANES, g * A_WIDTH + (ch + 1) * LANES)
                    lc = slice(ch * LANES, (ch + 1) * LANES)
                    v = src[:, cs]
                    dout = ds[g][0, :, lc].astype(F32) if d == 1 else scr[ch]
                    rs = lax.rsqrt(_segsum(v * v, seg_mean) + EPS)
                    xh = v * rs
                    dy = dout * c + pltpu.roll(dout * s1, LANES - 8, 1) + pltpu.roll(dout * s2, 8, 1)
                    acc[t * SUBLANES:(t + 1) * SUBLANES, :] += _cs8(dy * xh)
                    dxh = dy * wv
                    mean = _segsum(dxh * xh, seg_mean)
                    o_ref[:, t * qk_w + g * A_WIDTH + ch * LANES: t * qk_w + g * A_WIDTH + (ch + 1) * LANES] = (
                        rs * (dxh - xh * mean)).astype(BF16)
        for g, dv in enumerate((dv0, dv1, dv2)):
            d = GROUP_D[g]
            base = 2 * qk_w + g * A_WIDTH
            o_ref[:, base:base + A_WIDTH] = _to_tokens(dv, scr, d, T).astype(BF16)

        @pl.when(i == nt - 1)
        def _():
            for t in range(2):
                srow = jnp.sum(acc[t * SUBLANES:(t + 1) * SUBLANES, :], axis=0, keepdims=True)
                dw_ref[t:t + 1, :] = srow + pltpu.roll(srow, HEAD_DIM, 1)

    streams = [_stream_spec(d, T) for d in GROUP_D]
    return _call(
        body, "qk_bwd", (nt,),
        [pl.BlockSpec((T, qk_w), lambda i: (i, 0)), pl.BlockSpec((T, qk_w), lambda i: (i, 1))] + streams * 3 +
        [pl.BlockSpec((T, LANES), lambda i: (i, 0))] * 3 +
        [pl.BlockSpec((1, LANES), lambda i: (0, 0)), pl.BlockSpec((1, LANES), lambda i: (0, 0))],
        [pl.BlockSpec((T, 3 * qk_w), lambda i: (i, 0)), pl.BlockSpec((SUBLANES, LANES), lambda i: (0, 0))],
        [jax.ShapeDtypeStruct((S, 3 * qk_w), BF16), jax.ShapeDtypeStruct((SUBLANES, LANES), F32)],
        scratch=[pltpu.VMEM((2 * SUBLANES, LANES), F32), _chunked(T)], sem=("arbitrary",))(
            proj, proj, *dqs, *dks, *dvs, *rope, wq, wk)


N_SMALL_ODD = 40
SHIFT_ROWS_LESS = SUBLANES


def _fill_shifted(ext_ref, sh_ref):
    rows = ext_ref.shape[0] - SHIFT_ROWS_LESS
    for b in range(1, SUBLANES):
        sh_ref[b - 1] = ext_ref[b:b + rows, :]


def _window(ext_ref, sh_ref, off, T):
    a, b = divmod(off, SUBLANES)
    if b == 0:
        return ext_ref[off:off + T, :]
    return sh_ref[b - 1, a * SUBLANES:a * SUBLANES + T, :]


def _odd_pool_tile(i, uc_r, ucp_r, pw_r, ext_u, pooled_s, pm_s, T):
    H = HALO
    uc = uc_r[...]
    ext_u[0:H, :] = jnp.where(i == 0, 0.0, ucp_r[...])
    ext_u[H:H + T, :] = uc
    row = i * T + lax.broadcasted_iota(jnp.int32, (T, 1), 0)
    for g, p in enumerate(POOL_SIZES):
        cs = slice(g * LANES, (g + 1) * LANES)
        win = ext_u[H:H + T, cs]
        for j in range(1, p):
            win = win + ext_u[H - j:H - j + T, cs]
        cnt = jnp.minimum(row + 1, p).astype(F32)
        pooled = win / cnt - uc[:, cs]
        pooled_s[:, cs] = pooled
        pm_s[:, cs] = jnp.dot(pooled.astype(BF16), pw_r[g].astype(BF16), preferred_element_type=F32)
    return row


def _odd_glu_tile(i, da_r, dg_r, dap_r, dgp_r, ext_g, sh_g, T):
    H = HALO
    ext_g[0:H, :] = jnp.where(i == 0, 0.0, dap_r[...] * _sig(dgp_r[...]))
    ext_g[H:H + T, :] = da_r[...] * _sig(dg_r[...])
    _fill_shifted(ext_g, sh_g)


def _odd_specs(T, S, order):
    tb = T // HALO
    col = lambda c: pl.BlockSpec((T, A_WIDTH), lambda s: (order(s), c))
    prev = lambda c: pl.BlockSpec((HALO, A_WIDTH), lambda s: (jnp.maximum(order(s) * tb - 1, 0), c))
    const2 = lambda shape: pl.BlockSpec(shape, lambda s: (0, 0))
    weights = [pl.BlockSpec((4, LANES, LANES), lambda s: (0, 0, 0)), const2((1, A_WIDTH)),
               const2((D_CONV, A_WIDTH)), const2((1, A_WIDTH)), const2((1, A_WIDTH)), const2((1, A_WIDTH))]
    return col, prev, weights


def _odd_mixer_fwd(x, tgt, proj, pool_w, scale, dconv_w, dconv_b, ln_w, ln_b, w_out, T):
    S = proj.shape[0]
    nt = S // T
    col, prev, wspecs = _odd_specs(T, S, lambda s: s)
    H = HALO

    def body(x_ref, t_ref, w_ref, uc_r, da_r, dg_r, zl_r, zh_r, ucp_r, dap_r, dgp_r, pw_r, sc_r, dw_r, db_r,
             lw_r, lb_r, dy_ref, l_ref, ut_ref, cv_ref, ext_u, ext_g, sh_g, pooled_s, pm_s, lacc):
        i = pl.program_id(0)

        @pl.when(i == 0)
        def _():
            lacc[...] = jnp.zeros_like(lacc)

        _odd_pool_tile(i, uc_r, ucp_r, pw_r, ext_u, pooled_s, pm_s, T)
        _odd_glu_tile(i, da_r, dg_r, dap_r, dgp_r, ext_g, sh_g, T)
        base = H - (D_CONV - 1)
        conv = db_r[...] + dw_r[0:1, :] * _window(ext_g, sh_g, base, T)
        for kk in range(1, D_CONV):
            conv = conv + dw_r[kk:kk + 1, :] * _window(ext_g, sh_g, base + kk, T)
        cv_ref[...] = conv
        mu = jnp.mean(conv, axis=-1, keepdims=True)
        xc = conv - mu
        yh = xc * lax.rsqrt(jnp.mean(xc * xc, axis=-1, keepdims=True) + EPS)
        ln = yh * lw_r[...] + lb_r[...]
        zl, zh = zl_r[...], zh_r[...]
        y = x_ref[...] + _out_projection(ut_ref, w_ref, pm_s[...] * sc_r[...] * (zl * _sig(zl)),
                                         ln * _sig(ln) * (zh * _sig(zh)))
        diff = y - t_ref[...]
        dy_ref[...] = diff / float(D_MODEL)
        lacc[...] += _cs8(diff * diff)

        @pl.when(i == nt - 1)
        def _():
            l_ref[...] = jnp.sum(lacc[...], axis=0, keepdims=True)

    row = pl.BlockSpec((T, D_MODEL), lambda i: (i, 0))
    return _call(
        body, "odd_mixer_fwd", (nt,),
        [row, row, pl.BlockSpec((D_MODEL, D_MODEL), lambda i: (0, 0)),
         col(0), col(1), col(2), col(3), col(4), prev(0), prev(1), prev(2)] + wspecs,
        [row, pl.BlockSpec((1, D_MODEL), lambda i: (0, 0)), pl.BlockSpec((D_MODEL, T), lambda i: (0, i)),
         pl.BlockSpec((T, A_WIDTH), lambda i: (i, 0))],
        [jax.ShapeDtypeStruct((S, D_MODEL), F32), jax.ShapeDtypeStruct((1, D_MODEL), F32),
         jax.ShapeDtypeStruct((D_MODEL, S), BF16), jax.ShapeDtypeStruct((S, A_WIDTH), F32)],
        scratch=[pltpu.VMEM((T + H, A_WIDTH), F32), pltpu.VMEM((T + H, A_WIDTH), F32),
                 pltpu.VMEM((SUBLANES - 1, T + H - SHIFT_ROWS_LESS, A_WIDTH), F32),
                 pltpu.VMEM((T, A_WIDTH), F32), pltpu.VMEM((T, A_WIDTH), F32),
                 pltpu.VMEM((SUBLANES, D_MODEL), F32)],
        sem=("arbitrary",))(x, tgt, w_out, proj, proj, proj, proj, proj, proj, proj, proj,
                            pool_w, scale, dconv_w, dconv_b, ln_w, ln_b)


def _odd_mixer_bwd(dy, w_out, proj, conv, pool_w, scale, dconv_w, dconv_b, ln_w, ln_b, T):
    S = proj.shape[0]
    nt = S // T
    order = lambda s: nt - 1 - s
    col, prev, wspecs = _odd_specs(T, S, order)
    H = HALO

    def body(dy_r, w_ref, cv_r, uc_r, da_r, dg_r, zl_r, zh_r, ucp_r, dap_r, dgp_r, pw_r, sc_r, dw_r, db_r, lw_r, lb_r,
             dp_ref, dpw_ref, sm_ref, ext_u, ext_g, sh_g, pooled_s, pm_s, dpl_s, ext_p, ext_c, sh_c, acc):
        step = pl.program_id(0)
        i = nt - 1 - step

        @pl.when(step == 0)
        def _():
            ext_p[T:T + H, :] = jnp.zeros((H, A_WIDTH), F32)
            ext_c[T:T + H, :] = jnp.zeros((H, A_WIDTH), F32)
            acc[...] = jnp.zeros_like(acc)
            dpw_ref[...] = jnp.zeros_like(dpw_ref)

        def accum(r, v):
            acc[r * SUBLANES:(r + 1) * SUBLANES, :] += _cs8(v)

        row = _odd_pool_tile(i, uc_r, ucp_r, pw_r, ext_u, pooled_s, pm_s, T)
        _odd_glu_tile(i, da_r, dg_r, dap_r, dgp_r, ext_g, sh_g, T)
        conv = cv_r[...]
        mu = jnp.mean(conv, axis=-1, keepdims=True)
        xc = conv - mu
        rstd = lax.rsqrt(jnp.mean(xc * xc, axis=-1, keepdims=True) + EPS)
        yh = xc * rstd
        ln = yh * lw_r[...] + lb_r[...]
        sln = _sig(ln)
        zl, zh = zl_r[...], zh_r[...]
        sl, sh = _sig(zl), _sig(zh)
        du = lax.dot_general(dy_r[...].astype(BF16), w_ref[...], NT_DIMS, preferred_element_type=F32)
        dul, duh = du[:, 0:A_WIDTH], du[:, A_WIDTH:]
        pm = pm_s[...]
        scv = sc_r[...]
        dyc = dul * (zl * sl)
        accum(34, dyc * pm)
        dpm = dyc * scv
        for g in range(len(POOL_SIZES)):
            cs = slice(g * LANES, (g + 1) * LANES)
            dpm_g = dpm[:, cs].astype(BF16)
            dpw_ref[g] += lax.dot_general(pooled_s[:, cs].astype(BF16), dpm_g, TN_DIMS, preferred_element_type=F32)
            dpl_s[:, cs] = lax.dot_general(dpm_g, pw_r[g].astype(BF16), NT_DIMS, preferred_element_type=F32)
        lane_p = lax.broadcasted_iota(jnp.int32, (1, A_WIDTH), 1) // LANES
        pvec = jnp.left_shift(2, lane_p)
        cnt = jnp.minimum(row + 1, pvec).astype(F32)
        dpl = dpl_s[...]
        ext_p[0:T, :] = dpl / cnt
        for g, p in enumerate(POOL_SIZES):
            cs = slice(g * LANES, (g + 1) * LANES)
            win = ext_p[0:T, cs]
            for j in range(1, p):
                win = win + ext_p[j:j + T, cs]
            dp_ref[:, cs] = (win - dpl[:, cs]).astype(BF16)
        ext_p[T:T + H, :] = ext_p[0:H, :]
        dln = duh * (zh * sh) * _dsilu(ln, sln)
        accum(32, dln * yh)
        accum(33, dln)
        dyh = dln * lw_r[...]
        dc = rstd * (dyh - jnp.mean(dyh, axis=-1, keepdims=True) - yh * jnp.mean(dyh * yh, axis=-1, keepdims=True))
        accum(31, dc)
        ext_c[0:T, :] = dc
        _fill_shifted(ext_c, sh_c)
        base = H - (D_CONV - 1)
        dgl = dw_r[0:1, :] * _window(ext_c, sh_c, D_CONV - 1, T)
        accum(0, dc * _window(ext_g, sh_g, base, T))
        for kk in range(1, D_CONV):
            dgl = dgl + dw_r[kk:kk + 1, :] * _window(ext_c, sh_c, D_CONV - 1 - kk, T)
            accum(kk, dc * _window(ext_g, sh_g, base + kk, T))
        ext_c[T:T + H, :] = ext_c[0:H, :]
        dav, dgv = da_r[...], dg_r[...]
        sg = _sig(dgv)
        dp_ref[:, A_WIDTH:2 * A_WIDTH] = (dgl * sg).astype(BF16)
        dp_ref[:, 2 * A_WIDTH:3 * A_WIDTH] = (dgl * dav * sg * (1.0 - sg)).astype(BF16)
        dp_ref[:, 3 * A_WIDTH:4 * A_WIDTH] = (dul * (pm * scv) * _dsilu(zl, sl)).astype(BF16)
        dp_ref[:, 4 * A_WIDTH:5 * A_WIDTH] = (duh * (ln * sln) * _dsilu(zh, sh)).astype(BF16)

        @pl.when(step == nt - 1)
        def _():
            for r in range(N_SMALL_ODD):
                sm_ref[r:r + 1, :] = jnp.sum(acc[r * SUBLANES:(r + 1) * SUBLANES, :], axis=0, keepdims=True)

    ext = pltpu.VMEM((T + H, A_WIDTH), F32)
    shifted = pltpu.VMEM((SUBLANES - 1, T + H - SHIFT_ROWS_LESS, A_WIDTH), F32)
    tile = pltpu.VMEM((T, A_WIDTH), F32)
    return _call(
        body, "odd_mixer_bwd", (nt,),
        [pl.BlockSpec((T, D_MODEL), lambda s: (order(s), 0)), pl.BlockSpec((D_MODEL, D_MODEL), lambda s: (0, 0)),
         pl.BlockSpec((T, A_WIDTH), lambda s: (order(s), 0)),
         col(0), col(1), col(2), col(3), col(4), prev(0), prev(1), prev(2)] + wspecs,
        [pl.BlockSpec((T, ODD_IN), lambda s: (order(s), 0)),
         pl.BlockSpec((4, LANES, LANES), lambda s: (0, 0, 0)),
         pl.BlockSpec((N_SMALL_ODD, A_WIDTH), lambda s: (0, 0))],
        [jax.ShapeDtypeStruct((S, ODD_IN), BF16), jax.ShapeDtypeStruct((4, LANES, LANES), F32),
         jax.ShapeDtypeStruct((N_SMALL_ODD, A_WIDTH), F32)],
        scratch=[ext, ext, shifted, tile, tile, tile, ext, ext, shifted,
                 pltpu.VMEM((N_SMALL_ODD * SUBLANES, A_WIDTH), F32)],
        sem=("arbitrary",))(dy, w_out, conv, proj, proj, proj, proj, proj, proj, proj, proj,
                            pool_w, scale, dconv_w, dconv_b, ln_w, ln_b)


TILE_SEQ = 256
TILE_WG = 256
TILE_FIRST = 512
TILE_MM = 512


SMALL_PACK = "small_pack"
SMALL_PACK_W = 2 * LANES
LATE_WEIGHTS = ("e_w_out", "o_w_in", "o_w_out", SMALL_PACK)
ODD_MATS = ("o_w_in", "o_w_out")
EVEN_MATS = ("e_w_in", "e_w_out")


def _reduce_start(names, grads, grads16, cidx):
    recv = _swap_to_sibling(names, [grads16[n] for n in names], "swap_halves_" + names[0][0], True)
    both = [_add_half(cidx, grads[n], r, n) for n, r in zip(names, recv)]
    return [h for h, _ in both], [hb for _, hb in both]


def _local_step(x, pos, tgt, shards, p, unpack_small, cidx, bidx):
    T = TILE_SEQ
    freq = _freq_table()
    wq = jnp.tile(p["e_q_norm_w"], (1, LANES // HEAD_DIM))
    wk = jnp.tile(p["e_k_norm_w"], (1, LANES // HEAD_DIM))

    proj_e, ht_e, w_e_in, late = _inproj_gathering(x, p["e_norm_w"], shards["e_w_in"], bidx, "e_w_in", LATE_WEIGHTS,
                                                   [shards[n] for n in LATE_WEIGHTS], TILE_FIRST, "inproj_even")
    wb = dict(zip(LATE_WEIGHTS, late), e_w_in=w_e_in)
    p = dict(p, **unpack_small(wb[SMALL_PACK]))
    qkv = _qkv_prep(proj_e, pos, freq, wq, wk, T)
    qs, ks, vs, rope = qkv[0:3], qkv[3:6], qkv[6:9], qkv[9:12]
    os_, ls_ = [], []
    for g in range(3):
        o, l = _attn_fwd(qs[g], ks[g], vs[g], g)
        os_.append(o)
        ls_.append(l)
    x1, ut_e = _even_mixer_fwd(x, proj_e, os_, ls_, p["e_conv_w"], wb["e_w_out"], T)
    proj_o, ht_o = _inproj(x1, p["o_norm_w"], wb["o_w_in"], TILE_MM, 1280, "inproj_odd")
    odd_w = (p["o_pool_w"], p["o_pool_scale"], p["o_dconv_w"], p["o_dconv_b"], p["o_ln_w"], p["o_ln_b"])
    dy, lsum, ut_o, conv_o = _odd_mixer_fwd(x1, tgt, proj_o, *odd_w, wb["o_w_out"], T)

    g, g16 = {}, {}
    g["o_w_out"], g16["o_w_out"] = _mm_wgrad(ut_o, [dy], TILE_WG, "wgrad_o_out")
    dproj_o, g["o_pool_w"], small_o = _odd_mixer_bwd(dy, wb["o_w_out"], proj_o, conv_o, *odd_w, T)
    g["o_w_in"], g16["o_w_in"] = _mm_wgrad(ht_o, [dproj_o], TILE_WG, "wgrad_o_in")
    half_o, half_o16 = _reduce_start(ODD_MATS, g, g16, cidx)
    dx1, g["o_norm_w"], blocks_o = _mm_nt_rms([dproj_o], wb["o_w_in"], x1, p["o_norm_w"], dy, TILE_MM, "dx_odd",
                                              ODD_MATS, half_o16)
    g["o_dconv_w"] = small_o[0:D_CONV]
    g["o_dconv_b"] = small_o[31:32]
    g["o_ln_w"] = small_o[32:33]
    g["o_ln_b"] = small_o[33:34]
    g["o_pool_scale"] = small_o[34:35]

    g["e_w_out"], g16["e_w_out"] = _mm_wgrad(ut_e, [dx1], TILE_WG, "wgrad_e_out")
    dos, cgs, drest, g["e_conv_w"] = _even_mixer_bwd(dx1, wb["e_w_out"], proj_e, os_, ls_, p["e_conv_w"], T)
    dqs, dks, dvs = [], [], []
    for gi in range(3):
        dq, dk, dv = _attn_bwd(qs[gi], ks[gi], vs[gi], dos[gi], ls_[gi], cgs[gi], gi)
        dqs.append(dq)
        dks.append(dk)
        dvs.append(dv)
    dqkv, dnw = _qk_bwd(proj_e, dqs, dks, dvs, rope, wq, wk, T)
    g["e_q_norm_w"] = dnw[0:1, 0:HEAD_DIM]
    g["e_k_norm_w"] = dnw[1:2, 0:HEAD_DIM]
    pieces = [dqkv, drest]
    g["e_w_in"], g16["e_w_in"] = _mm_wgrad(ht_e, pieces, TILE_WG, "wgrad_e_in")
    half_e, half_e16 = _reduce_start(EVEN_MATS, g, g16, cidx)
    dx, g["e_norm_w"], blocks_e = _mm_nt_rms(pieces, wb["e_w_in"], x, p["e_norm_w"], dx1, TILE_MM, "dx_even",
                                             EVEN_MATS, half_e16)
    parts = {}
    for names, halves, blocks in ((ODD_MATS, half_o, blocks_o), (EVEN_MATS, half_e, blocks_e)):
        for n, h, r in zip(names, halves, blocks):
            parts[n] = _add_blocks(bidx, h, r, n)
    return lsum, dx, g, parts


BIG = ("e_w_in", "e_w_out", "o_w_in", "o_w_out")
SHARD_AXIS = {"e_w_in": 1, "e_w_out": 0, "o_w_in": 1, "o_w_out": 0, SMALL_PACK: 0}
N_CHIPS = 4


def _place():
    x, y, c = lax.axis_index("x"), lax.axis_index("y"), lax.axis_index("c")
    chips = [(1 - x, y), (x, 1 - y), (1 - x, 1 - y)]
    return x, y, c, chips


def _block_of(ref, name, block):
    rows, cols = ref.shape
    if SHARD_AXIS[name] == 1:
        cw = cols // N_CHIPS
        return ref.at[:, pl.ds(pl.multiple_of(block * cw, LANES), cw)]
    rw = rows // N_CHIPS
    return ref.at[pl.ds(pl.multiple_of(block * rw, rw), rw), :]


def _half_of(ref, name, half):
    rows, cols = ref.shape
    if SHARD_AXIS[name] == 1:
        return ref.at[pl.ds(pl.multiple_of(half * (rows // 2), rows // 2), rows // 2), :]
    return ref.at[:, pl.ds(pl.multiple_of(half * (cols // 2), LANES), cols // 2)]


def _sub(ref, name, block, half):
    rows, cols = ref.shape
    if SHARD_AXIS[name] == 1:
        cw, hr = cols // N_CHIPS, rows // 2
        return ref.at[pl.ds(pl.multiple_of(half * hr, hr), hr), pl.ds(pl.multiple_of(block * cw, LANES), cw)]
    rw, hc = rows // N_CHIPS, cols // 2
    return ref.at[pl.ds(pl.multiple_of(block * rw, rw), rw), pl.ds(pl.multiple_of(half * hc, LANES), hc)]


GATHER_COPIES = 7


class _Gather:
    def __init__(self, names, s_refs, f_refs, send, recv):
        self.names, self.s, self.f, self.send, self.recv = names, s_refs, f_refs, send, recv

    def _copy(self, k, src, dst, to):
        return pltpu.make_async_remote_copy(src_ref=src, dst_ref=dst, send_sem=self.send.at[k],
                                            recv_sem=self.recv.at[k], device_id=to, device_id_type=MESH)

    def _plan(self):
        x, y, c, chips = _place()
        me, sib = 2 * x + y, (x, y, 1 - c)
        first, relay_in, relay, last_in = [], [], [], []
        for wi, n in enumerate(self.names):
            k0 = wi * GATHER_COPIES
            s, f = self.s[wi], self.f[wi]
            own = _block_of(f, n, me)
            first.append(self._copy(k0 + 3, s, own, sib))
            last_in.append(self._copy(k0 + 3, s, own, sib))
            for j, (cx, cy) in enumerate(chips):
                first.append(self._copy(k0 + j, _half_of(s, n, c), _sub(f, n, me, c), (cx, cy, c)))
                mine = _sub(f, n, 2 * cx + cy, c)
                relay_in.append(self._copy(k0 + j, mine, mine, sib))
                relay.append(self._copy(k0 + 4 + j, mine, mine, sib))
                theirs = _sub(f, n, 2 * cx + cy, 1 - c)
                last_in.append(self._copy(k0 + 4 + j, theirs, theirs, sib))
        return first, relay_in, relay, last_in

    N_RELATIONS = 3

    def begin(self, relations=(0, 1, 2), sibling=True):
        first = self._plan()[0]
        for wi in range(len(self.names)):
            mine = first[wi * (1 + self.N_RELATIONS):(wi + 1) * (1 + self.N_RELATIONS)]
            if sibling:
                mine[0].start()
            for j in relations:
                mine[1 + j].start()

    def relay(self, relations=(0, 1, 2)):
        _, relay_in, relay, _ = self._plan()
        for wi in range(len(self.names)):
            for j in relations:
                relay_in[wi * self.N_RELATIONS + j].wait_recv()
                relay[wi * self.N_RELATIONS + j].start()

    def end(self):
        first, _, relay, last_in = self._plan()
        for cp in last_in:
            cp.wait_recv()
        for cp in first + relay:
            cp.wait_send()

    def wait_relayed(self, j):
        self._plan()[3][1 + j].wait_recv()

    def end_rest(self):
        first, _, relay, last_in = self._plan()
        last_in[0].wait_recv()
        for cp in first + relay:
            cp.wait_send()


def _full_shape(n, s):
    r, cdim = s.shape
    return jax.ShapeDtypeStruct((r, cdim * N_CHIPS) if SHARD_AXIS[n] == 1 else (r * N_CHIPS, cdim), s.dtype)


def _gather_sems(names):
    k = GATHER_COPIES * len(names)
    return [pltpu.SemaphoreType.DMA((k,)), pltpu.SemaphoreType.DMA((k,))]


def _scatter_copies(names, h_refs, r_refs, send, recv):
    _, _, c, chips = _place()
    cps = []
    for wi, n in enumerate(names):
        for j, (cx, cy) in enumerate(chips):
            cps.append(pltpu.make_async_remote_copy(
                src_ref=_block_of(h_refs[wi], n, 2 * cx + cy), dst_ref=r_refs[wi].at[j],
                send_sem=send.at[wi * 3 + j], recv_sem=recv.at[wi * 3 + j],
                device_id=(cx, cy, c), device_id_type=MESH))
    return cps


def _scatter_sems(names):
    return [pltpu.SemaphoreType.DMA((3 * len(names),)), pltpu.SemaphoreType.DMA((3 * len(names),))]


class _SmallSum:
    def __init__(self, p_ref, o_ref, sbuf, cbuf, send, recv):
        self.p, self.o, self.sbuf, self.cbuf, self.send, self.recv = p_ref, o_ref, sbuf, cbuf, send, recv

    def _copy(self, k, ref, to):
        return pltpu.make_async_remote_copy(src_ref=ref, dst_ref=ref, send_sem=self.send.at[k],
                                            recv_sem=self.recv.at[k], device_id=to, device_id_type=MESH)

    def _plan(self):
        x, y, c, chips = _place()
        me, sib = 2 * x + y, (x, y, 1 - c)
        d2d_out = self._copy(0, self.sbuf.at[c], sib)
        d2d_in = self._copy(0, self.sbuf.at[1 - c], sib)
        ici_out = [self._copy(1 + j, self.cbuf.at[me], (cx, cy, c)) for j, (cx, cy) in enumerate(chips)]
        ici_in = [self._copy(1 + j, self.cbuf.at[2 * cx + cy], (cx, cy, c)) for j, (cx, cy) in enumerate(chips)]
        return c, me, d2d_out, d2d_in, ici_out, ici_in

    def begin(self):
        c, _, d2d_out, _, _, _ = self._plan()
        self.sbuf[c] = self.p[...]
        d2d_out.start()

    def middle(self):
        _, me, _, d2d_in, ici_out, _ = self._plan()
        d2d_in.wait_recv()
        self.cbuf[me] = self.sbuf[0] + self.sbuf[1]
        for cp in ici_out:
            cp.start()

    def end(self):
        _, _, d2d_out, _, ici_out, ici_in = self._plan()
        for cp in ici_in:
            cp.wait_recv()
        self.o[...] = (self.cbuf[0] + self.cbuf[1]) + (self.cbuf[2] + self.cbuf[3])
        for cp in [d2d_out] + ici_out:
            cp.wait_send()


def _small_sum_scratch(R):
    return [pltpu.VMEM((2, R, LANES), F32), pltpu.VMEM((N_CHIPS, R, LANES), F32),
            pltpu.SemaphoreType.DMA((4,)), pltpu.SemaphoreType.DMA((4,))]


def _half_shape(shape, name):
    r, cdim = shape
    return (r // 2, cdim) if SHARD_AXIS[name] == 1 else (r, cdim // 2)


def _shard_shape(shape, name):
    r, cdim = shape
    return (r, cdim // N_CHIPS) if SHARD_AXIS[name] == 1 else (r // N_CHIPS, cdim)


def _swap_to_sibling(names, srcs, name, pick_half, small=None):
    nw = len(names)
    ns = 0 if small is None else 1
    vm = pl.BlockSpec(memory_space=pltpu.VMEM)

    def body(*refs):
        g_refs = refs[:nw]
        r_refs = refs[nw + ns:2 * nw + ns]
        send, recv = refs[2 * nw + 2 * ns:2 * nw + 2 * ns + 2]
        x, y, c, _ = _place()
        sib = (x, y, 1 - c)
        cps = []
        for wi, n in enumerate(names):
            src = _half_of(g_refs[wi], n, 1 - c) if pick_half else g_refs[wi]
            cp = pltpu.make_async_remote_copy(src_ref=src, dst_ref=r_refs[wi], send_sem=send.at[wi],
                                              recv_sem=recv.at[wi], device_id=sib, device_id_type=MESH)
            cp.start()
            cps.append(cp)
        if ns:
            total = _SmallSum(refs[nw], refs[2 * nw + ns], *refs[2 * nw + 2 * ns + 2:])
            total.begin()
            total.middle()
            total.end()
        for cp in cps:
            cp.wait()

    outs = [jax.ShapeDtypeStruct(_half_shape(g.shape, n) if pick_half else g.shape, g.dtype)
            for n, g in zip(names, srcs)]
    return pl.pallas_call(
        body, name=name, in_specs=[ANY] * nw + [vm] * ns, out_specs=[ANY] * nw + [vm] * ns,
        out_shape=outs + ([jax.ShapeDtypeStruct(small.shape, F32)] if ns else []),
        scratch_shapes=[pltpu.SemaphoreType.DMA((nw,)), pltpu.SemaphoreType.DMA((nw,))] +
        (_small_sum_scratch(small.shape[0]) if ns else []),
    )(*srcs, *([small] if ns else []))


def _add_half(cidx, g, r, name):
    rows, cols = r.shape
    tr = 256
    tc = cols if cols <= 1792 else (1792 if cols % 1792 == 0 else 1280)
    nr, nc = rows // tr, cols // tc

    def body(c_ref, g_ref, r_ref, o_ref, ob_ref):
        s = g_ref[...] + r_ref[...].astype(F32)
        o_ref[...] = s
        ob_ref[...] = s.astype(BF16)

    if SHARD_AXIS[name] == 1:
        gmap = lambda i, j, c_ref: (c_ref[0] * nr + i, j)
    else:
        gmap = lambda i, j, c_ref: (i, c_ref[0] * nc + j)
    same = lambda i, j, c_ref: (i, j)
    return pl.pallas_call(
        body, name="add_half_" + name,
        grid_spec=pltpu.PrefetchScalarGridSpec(
            num_scalar_prefetch=1, grid=(nr, nc),
            in_specs=[pl.BlockSpec((tr, tc), gmap), pl.BlockSpec((tr, tc), same)],
            out_specs=[pl.BlockSpec((tr, tc), same), pl.BlockSpec((tr, tc), same)]),
        out_shape=[jax.ShapeDtypeStruct(r.shape, F32), jax.ShapeDtypeStruct(r.shape, BF16)],
        compiler_params=pltpu.CompilerParams(dimension_semantics=("parallel", "parallel"), vmem_limit_bytes=VMEM_LIMIT),
    )(cidx, g, r)


def _add_blocks(bidx, h, r, name):
    _, rows, cols = r.shape
    tr = min(rows, 256)
    nr = rows // tr

    def body(b_ref, h_ref, r0, r1, r2, o_ref):
        o_ref[...] = ((h_ref[...] + r0[0].astype(F32)) + r1[0].astype(F32)) + r2[0].astype(F32)

    if SHARD_AXIS[name] == 1:
        hmap = lambda i, b_ref: (i, b_ref[0])
    else:
        hmap = lambda i, b_ref: (b_ref[0] * nr + i, 0)
    rspec = lambda j: pl.BlockSpec((1, tr, cols), lambda i, b_ref, j=j: (j, i, 0))
    return pl.pallas_call(
        body, name="add_blocks_" + name,
        grid_spec=pltpu.PrefetchScalarGridSpec(
            num_scalar_prefetch=1, grid=(nr,),
            in_specs=[pl.BlockSpec((tr, cols), hmap), rspec(0), rspec(1), rspec(2)],
            out_specs=pl.BlockSpec((tr, cols), lambda i, b_ref: (i, 0))),
        out_shape=jax.ShapeDtypeStruct((rows, cols), F32),
        compiler_params=pltpu.CompilerParams(dimension_semantics=("parallel",), vmem_limit_bytes=VMEM_LIMIT),
    )(bidx, h, r, r, r)


def _adam_math(w, g, m, v):
    c1 = 1.0 - ADAM_B1 ** ADAM_STEP
    c2 = 1.0 - ADAM_B2 ** ADAM_STEP
    nm = ADAM_B1 * m + (1.0 - ADAM_B1) * g
    nv = ADAM_B2 * v + (1.0 - ADAM_B2) * (g * g)
    delta = -ADAM_LR * ((nm / c1) / (jnp.sqrt(nv / c2) + ADAM_EPS) + ADAM_WD * w)
    return delta, nm, nv


def _adamw(w, g, m, v, name):
    def body(w_ref, g_ref, m_ref, v_ref, d_ref, nm_ref, nv_ref):
        d_ref[...], nm_ref[...], nv_ref[...] = _adam_math(w_ref[...], g_ref[...], m_ref[...], v_ref[...])

    spec = pl.BlockSpec(w.shape, lambda i: (0, 0))
    return _call(body, "adamw_" + name, (1,), [spec] * 4, [spec] * 3,
                 [jax.ShapeDtypeStruct(w.shape, F32)] * 3, sem=("arbitrary",))(w, g, m, v)


def _adamw_halves(cidx, w, mine, theirs, m, v, name):
    hr, hc = mine.shape
    tr = 128
    ni = hr // tr
    if SHARD_AXIS[name] == 1:
        wmap = lambda hh, i, c_ref: (hh * ni + i, 0)
    else:
        wmap = lambda hh, i, c_ref: (i, hh)
    hmap = lambda hh, i, c_ref: (i, 0)

    def body(c_ref, w_ref, a_ref, b_ref, m_ref, v_ref, g_ref, d_ref, nm_ref, nv_ref):
        g = jnp.where(pl.program_id(0) == c_ref[0], a_ref[...], b_ref[...])
        g_ref[...] = g
        d_ref[...], nm_ref[...], nv_ref[...] = _adam_math(w_ref[...], g, m_ref[...], v_ref[...])

    wspec = pl.BlockSpec((tr, hc), wmap)
    hspec = pl.BlockSpec((tr, hc), hmap)
    return pl.pallas_call(
        body, name="adamw_" + name,
        grid_spec=pltpu.PrefetchScalarGridSpec(
            num_scalar_prefetch=1, grid=(2, ni),
            in_specs=[wspec, hspec, hspec, wspec, wspec], out_specs=[wspec] * 4),
        out_shape=[jax.ShapeDtypeStruct(w.shape, F32)] * 4,
        compiler_params=pltpu.CompilerParams(dimension_semantics=("parallel", "parallel"), vmem_limit_bytes=VMEM_LIMIT),
    )(cidx, w, mine, theirs, m, v)


SMALL = ("e_norm_w", "e_q_norm_w", "e_k_norm_w", "e_conv_w", "o_norm_w", "o_pool_w", "o_pool_scale",
         "o_dconv_w", "o_dconv_b", "o_ln_w", "o_ln_b")
SMALL_SHARDED = ("e_conv_w", "o_norm_w", "o_pool_scale", "o_dconv_w", "o_dconv_b", "o_ln_w", "o_ln_b")
WEIGHTS = ("e_norm_w", "e_w_in", "e_q_norm_w", "e_k_norm_w", "e_conv_w", "e_w_out", "o_norm_w", "o_w_in",
           "o_pool_w", "o_pool_scale", "o_dconv_w", "o_dconv_b", "o_ln_w", "o_ln_b", "o_w_out")


def _pack(arrs):
    flat = jnp.concatenate([a.reshape(-1) for a in arrs])
    rows = -(-flat.shape[0] // (LANES * SUBLANES)) * SUBLANES
    flat = jnp.pad(flat, (0, rows * LANES - flat.shape[0]))
    return flat.reshape(rows, LANES)


def _unpack(packed, shapes):
    flat = packed.reshape(-1)
    out, off = [], 0
    for s in shapes:
        n = int(np.prod(s))
        out.append(flat[off:off + n].reshape(s))
        off += n
    return out


def _gather_last(a, block, width):
    return lax.dynamic_slice_in_dim(a, block * width, width, axis=a.ndim - 1)


def kernel(x, positions, e_norm_w, e_w_in, e_q_norm_w, e_k_norm_w, e_conv_w, e_w_out, o_norm_w, o_w_in, o_pool_w, o_pool_scale, o_dconv_w, o_dconv_b, o_ln_w, o_ln_b, o_w_out, loss_target, m_e_norm_w, m_e_w_in, m_e_q_norm_w, m_e_k_norm_w, m_e_conv_w, m_e_w_out, m_o_norm_w, m_o_w_in, m_o_pool_w, m_o_pool_scale, m_o_dconv_w, m_o_dconv_b, m_o_ln_w, m_o_ln_b, m_o_w_out, v_e_norm_w, v_e_w_in, v_e_q_norm_w, v_e_k_norm_w, v_e_conv_w, v_e_w_out, v_o_norm_w, v_o_w_in, v_o_pool_w, v_o_pool_scale, v_o_dconv_w, v_o_dconv_b, v_o_ln_w, v_o_ln_b, v_o_w_out):
    given = dict(e_norm_w=e_norm_w, e_w_in=e_w_in, e_q_norm_w=e_q_norm_w, e_k_norm_w=e_k_norm_w, e_conv_w=e_conv_w,
                 e_w_out=e_w_out, o_norm_w=o_norm_w, o_w_in=o_w_in, o_pool_w=o_pool_w, o_pool_scale=o_pool_scale,
                 o_dconv_w=o_dconv_w, o_dconv_b=o_dconv_b, o_ln_w=o_ln_w, o_ln_b=o_ln_b, o_w_out=o_w_out)
    mom = dict(e_norm_w=m_e_norm_w, e_w_in=m_e_w_in, e_q_norm_w=m_e_q_norm_w, e_k_norm_w=m_e_k_norm_w,
               e_conv_w=m_e_conv_w, e_w_out=m_e_w_out, o_norm_w=m_o_norm_w, o_w_in=m_o_w_in, o_pool_w=m_o_pool_w,
               o_pool_scale=m_o_pool_scale, o_dconv_w=m_o_dconv_w, o_dconv_b=m_o_dconv_b, o_ln_w=m_o_ln_w,
               o_ln_b=m_o_ln_b, o_w_out=m_o_w_out)
    var = dict(e_norm_w=v_e_norm_w, e_w_in=v_e_w_in, e_q_norm_w=v_e_q_norm_w, e_k_norm_w=v_e_k_norm_w,
               e_conv_w=v_e_conv_w, e_w_out=v_e_w_out, o_norm_w=v_o_norm_w, o_w_in=v_o_w_in, o_pool_w=v_o_pool_w,
               o_pool_scale=v_o_pool_scale, o_dconv_w=v_o_dconv_w, o_dconv_b=v_o_dconv_b, o_ln_w=v_o_ln_w,
               o_ln_b=v_o_ln_b, o_w_out=v_o_w_out)
    S = x.shape[1]
    mx, my, mc = lax.axis_index("x"), lax.axis_index("y"), lax.axis_index("c")
    chip = 2 * mx + my
    cidx = jnp.reshape(mc, (1,)).astype(jnp.int32)
    bidx = jnp.reshape(chip, (1,)).astype(jnp.int32)

    shards = {n: given[n][0].astype(BF16) for n in BIG}
    shard_sizes = [int(np.prod(given[n].shape)) for n in SMALL_SHARDED]
    flat = jnp.concatenate([given[n].reshape(-1) for n in SMALL_SHARDED])
    rows = -(-flat.shape[0] // (SMALL_PACK_W * SUBLANES)) * SUBLANES
    shards[SMALL_PACK] = jnp.pad(flat, (0, rows * SMALL_PACK_W - flat.shape[0])).reshape(rows, SMALL_PACK_W)

    def unpack_small(full):
        gathered = full.reshape(N_CHIPS, rows * SMALL_PACK_W)
        out, off = {}, 0
        for n, size in zip(SMALL_SHARDED, shard_sizes):
            sh = given[n].shape[1:]
            parts = gathered[:, off:off + size].reshape((N_CHIPS,) + sh)
            fullp = jnp.moveaxis(parts, 0, -2).reshape(sh[:-1] + (N_CHIPS * sh[-1],))
            out[n] = fullp.reshape(-1, fullp.shape[-1])
            off += size
        return out

    p = dict(e_norm_w=e_norm_w, e_q_norm_w=e_q_norm_w, e_k_norm_w=e_k_norm_w, o_pool_w=o_pool_w[0])
    lsum, dx, g, parts = _local_step(x[0], positions.reshape(S, 1), loss_target[0], shards, p, unpack_small,
                                     cidx, bidx)
    mine = [g[n] for n in SMALL] + [(0.5 / float(D_MODEL)) * jnp.sum(lsum, keepdims=True)]
    *theirs, tot = _swap_to_sibling(BIG, [parts[n] for n in BIG], "swap_reduced", False, small=_pack(mine))
    tot = _unpack(tot, [a.shape for a in mine])
    loss = tot[-1].reshape(())

    grads, delta, new_m, new_v = {}, {}, {}, {}
    for n, other in zip(BIG, theirs):
        sh = given[n].shape
        outs = _adamw_halves(cidx, given[n][0], parts[n], other, mom[n][0], var[n][0], n)
        grads[n], delta[n], new_m[n], new_v[n] = [a.reshape(sh) for a in outs]
    for n, gv in zip(SMALL, tot):
        if n in SMALL_SHARDED:
            gv = _gather_last(gv, chip, gv.shape[-1] // N_CHIPS)
        grads[n] = gv.reshape(given[n].shape)
    big_small = "o_pool_w"
    pw = [src[big_small].reshape(-1, LANES) for src in (given, grads, mom, var)]
    for dst, a in zip((delta, new_m, new_v), _adamw(*pw, "pool_w")):
        dst[big_small] = a.reshape(given[big_small].shape)
    tiny = tuple(n for n in SMALL if n != big_small)
    shapes = [given[n].shape for n in tiny]
    packed = [_pack([src[n] for n in tiny]) for src in (given, grads, mom, var)]
    for dst, pk in zip((delta, new_m, new_v), _adamw(*packed, "small")):
        for n, a in zip(tiny, _unpack(pk, shapes)):
            dst[n] = a
    return (loss, dx[None], *[grads[n] for n in WEIGHTS], *[delta[n] for n in WEIGHTS],
            *[new_m[n] for n in WEIGHTS], *[new_v[n] for n in WEIGHTS])
```

```python
import numpy as np
import jax
import jax.numpy as jnp
from jax import lax
from jax.experimental import pallas as pl
from jax.experimental.pallas import tpu as pltpu

F32 = jnp.float32
BF16 = jnp.bfloat16

D_MODEL = 1024
HEAD_DIM = 64
A_WIDTH = 512
A_HEADS = 8
A_GROUPS = ((128, 1), (512, 4), (2048, 16))
QBLK = 128
ROT_DIM = 16
ROPE_THETA = 500000.0
POOL_SIZES = (2, 4, 8, 16)
D_CONV = 31
SC_WIDTH = 3
EVEN_IN = 7168
ODD_IN = 2560
EPS = 1e-6
NEG = -1e30
ADAM_LR, ADAM_B1, ADAM_B2, ADAM_EPS, ADAM_WD, ADAM_STEP = 0.001, 0.9, 0.999, 1e-08, 0.01, 10

LANES = 128
SUBLANES = 8
HALO = 32
VMEM_LIMIT = 52 * 1024 * 1024
MESH = pl.DeviceIdType.MESH
ANY = pl.BlockSpec(memory_space=pl.ANY)

NT_DIMS = (((1,), (1,)), ((), ()))
TN_DIMS = (((0,), (0,)), ((), ()))


def _call(body, name, grid, in_specs, out_specs, out_shape, scratch=(), sem=None, aliases=None):
    return pl.pallas_call(
        body, name=name, grid=grid, in_specs=in_specs, out_specs=out_specs, out_shape=out_shape,
        scratch_shapes=list(scratch), input_output_aliases=aliases or {},
        compiler_params=pltpu.CompilerParams(dimension_semantics=sem, vmem_limit_bytes=VMEM_LIMIT))


def _sig(v):
    return jax.nn.sigmoid(v)


def _dsilu(v, s):
    return s * (1.0 + v * (1.0 - s))


def _out_projection(ut_ref, w_ref, lo, hi):
    acc = None
    for k, v in enumerate((lo, hi)):
        ut_ref[k * A_WIDTH:(k + 1) * A_WIDTH, :] = v.T.astype(BF16)
        part = jnp.dot(v.astype(BF16), w_ref[k * A_WIDTH:(k + 1) * A_WIDTH, :], preferred_element_type=F32)
        acc = part if acc is None else acc + part
    return acc


def _cs8(v):
    return v.reshape(v.shape[0] // SUBLANES, SUBLANES, v.shape[1]).sum(axis=0)


def _seg_mean():
    r = lax.broadcasted_iota(jnp.int32, (LANES, LANES), 0) // HEAD_DIM
    c = lax.broadcasted_iota(jnp.int32, (LANES, LANES), 1) // HEAD_DIM
    return jnp.where(r == c, 1.0 / HEAD_DIM, 0.0).astype(BF16)


def _segsum(v, ones):
    hi = v.astype(BF16)
    lo = (v - hi.astype(F32)).astype(BF16)
    return (jnp.dot(hi, ones, preferred_element_type=F32) + jnp.dot(lo, ones, preferred_element_type=F32))


def _head_rstd(v, seg_mean):
    return lax.rsqrt(jnp.dot((v * v).astype(BF16), seg_mean, preferred_element_type=F32) + EPS)


def _rope_tables(pos_ref, freq_ref):
    ang = pos_ref[...].astype(F32) * freq_ref[...]
    cosv, sinv = jnp.cos(ang), jnp.sin(ang)
    lm = lax.broadcasted_iota(jnp.int32, ang.shape, 1) % HEAD_DIM
    half = ROT_DIM // 2
    c = jnp.where(lm < ROT_DIM, cosv, 1.0)
    s1 = jnp.where((lm >= half) & (lm < ROT_DIM), sinv, 0.0)
    s2 = jnp.where(lm < half, -sinv, 0.0)
    return c, s1, s2


def _freq_table():
    half = ROT_DIM // 2
    inv = ROPE_THETA ** (-np.arange(half, dtype=np.float64) / half)
    lane = np.arange(LANES) % HEAD_DIM
    f = np.where(lane < ROT_DIM, inv[lane % half], 0.0)
    return jnp.asarray(f.reshape(1, LANES), F32)


def _load_once(hbm_ref, vmem_ref, sem):
    @pl.when(pl.program_id(0) == 0)
    def _():
        cp = pltpu.make_async_copy(hbm_ref, vmem_ref, sem)
        cp.start()
        cp.wait()


def _rms_rows(x_ref, nw_ref):
    xv = x_ref[...]
    ms = jnp.mean(xv * xv, axis=-1, keepdims=True)
    return xv * lax.rsqrt(ms + EPS) * nw_ref[...]


def _inproj(x, nw, w, tm, tn, name):
    S, N = x.shape[0], w.shape[1]

    def body(x_ref, nw_ref, w_hbm, o_ref, ht_ref, w_v, sem):
        _load_once(w_hbm, w_v, sem)
        h = _rms_rows(x_ref, nw_ref)
        ht_ref[...] = h.T.astype(BF16)
        hb = h.astype(BF16)
        for j in range(N // tn):
            o_ref[:, j * tn:(j + 1) * tn] = jnp.dot(hb, w_v[:, j * tn:(j + 1) * tn], preferred_element_type=F32)

    return _call(
        body, name, (S // tm,),
        [pl.BlockSpec((tm, D_MODEL), lambda i: (i, 0)),
         pl.BlockSpec((1, D_MODEL), lambda i: (0, 0)), ANY],
        [pl.BlockSpec((tm, N), lambda i: (i, 0)),
         pl.BlockSpec((D_MODEL, tm), lambda i: (0, i))],
        [jax.ShapeDtypeStruct((S, N), F32), jax.ShapeDtypeStruct((D_MODEL, S), BF16)],
        scratch=[pltpu.VMEM(w.shape, BF16), pltpu.SemaphoreType.DMA(())], sem=("arbitrary",))(x, nw, w)


def _inproj_gathering(x, nw, shard, bidx, first, late_names, late_shards, tm, name):
    S = x.shape[0]
    ni = S // tm
    K, cw = shard.shape
    nl = len(late_names)
    last = N_CHIPS - 1

    def body(b_ref, x_ref, nw_ref, s_hbm, *rest):
        ls_refs = rest[:nl]
        o_ref, ht_ref, f_hbm = rest[nl:nl + 3]
        lf_refs = rest[nl + 3:2 * nl + 3]
        hs, w_blk, lsem, send1, recv1, send2, recv2 = rest[2 * nl + 3:]
        j, i = pl.program_id(0), pl.program_id(1)
        g1 = _Gather((first,), (s_hbm,), (f_hbm,), send1, recv1)
        g2 = _Gather(late_names, ls_refs, lf_refs, send2, recv2)
        _, _, _, chips = _place()

        def load_block(src):
            cp = pltpu.make_async_copy(src, w_blk, lsem)
            cp.start()
            cp.wait()

        @pl.when((j == 0) & (i == 0))
        def _():
            g1.begin(relations=(0, 1))
            load_block(s_hbm)

        for r, (cx, cy) in enumerate(chips):
            @pl.when((j == r + 1) & (i == 0))
            def _(r=r, cx=cx, cy=cy):
                g1.wait_relayed(r)
                load_block(_block_of(f_hbm, first, 2 * cx + cy))

        @pl.when((j == 2) & (i == 0))
        def _():
            g1.relay(relations=(2,))
            g2.begin()

        pl.when((j == last) & (i == ni // 2))(g2.relay)

        rows = pl.ds(pl.multiple_of(i * tm, tm), tm)

        @pl.when(j == 0)
        def _():
            h = _rms_rows(x_ref, nw_ref)
            hs[rows, :] = h.astype(BF16)
            ht_ref[...] = h.T.astype(BF16)

        o_ref[...] = jnp.dot(hs[rows, :], w_blk[...], preferred_element_type=F32)

        @pl.when((j == 0) & (i == ni - 1))
        def _():
            g1.relay(relations=(0, 1))
            g1.begin(relations=(2,), sibling=False)

        @pl.when((j == last) & (i == ni - 1))
        def _():
            g1.end_rest()
            g2.end()

    def block_of_step(j, b_ref):
        return jnp.bitwise_xor(b_ref[0], jnp.bitwise_or(jnp.left_shift(jnp.bitwise_and(j, 1), 1), jnp.right_shift(j, 1)))

    outs = pl.pallas_call(
        body, name=name,
        grid_spec=pltpu.PrefetchScalarGridSpec(
            num_scalar_prefetch=1, grid=(N_CHIPS, ni),
            in_specs=[pl.BlockSpec((tm, D_MODEL), lambda j, i, b: (jnp.where(j == 0, i, 0), 0)),
                      pl.BlockSpec((1, D_MODEL), lambda j, i, b: (0, 0)), ANY] + [ANY] * nl,
            out_specs=[pl.BlockSpec((tm, cw), lambda j, i, b: (i, block_of_step(j, b))),
                       pl.BlockSpec((D_MODEL, tm), lambda j, i, b: (0, jnp.where(j == 0, i, ni - 1))),
                       ANY] + [ANY] * nl,
            scratch_shapes=[pltpu.VMEM((S, D_MODEL), BF16), pltpu.VMEM((K, cw), BF16), pltpu.SemaphoreType.DMA(())] +
            _gather_sems((first,)) + _gather_sems(late_names)),
        out_shape=[jax.ShapeDtypeStruct((S, cw * N_CHIPS), F32), jax.ShapeDtypeStruct((D_MODEL, S), BF16),
                   _full_shape(first, shard)] + [_full_shape(n, s) for n, s in zip(late_names, late_shards)],
        compiler_params=pltpu.CompilerParams(dimension_semantics=("arbitrary", "arbitrary"),
                                             vmem_limit_bytes=VMEM_LIMIT),
    )(bidx, x, nw, shard, *late_shards)
    return outs[0], outs[1], outs[2], list(outs[3:])


def _piece_blocks(pieces, tk, axis):
    starts, counts, s = [], [], 0
    for p in pieces:
        n = p.shape[axis] // tk
        starts.append(s)
        counts.append(n)
        s += n
    return starts, counts, s


def _mm_nt_rms(pieces, w, x, nw, dres, tm, name, scatter_names=(), scatter_halves=()):
    S = x.shape[0]
    npc = len(pieces)
    ni = S // tm
    ns = len(scatter_names)
    offs = np.cumsum([0] + [p.shape[1] for p in pieces]).tolist()

    def body(*refs):
        p_refs = refs[:npc]
        w_hbm, x_ref, nw_ref, dr_ref = refs[npc:npc + 4]
        h_refs = refs[npc + 4:npc + 4 + ns]
        dx_ref, dnw_ref = refs[npc + 4 + ns:npc + 6 + ns]
        r_refs = refs[npc + 6 + ns:npc + 6 + 2 * ns]
        w_v, sem, nacc = refs[npc + 6 + 2 * ns:npc + 9 + 2 * ns]
        i = pl.program_id(0)
        if ns:
            send, recv = refs[npc + 9 + 2 * ns:]

            @pl.when(i == 0)
            def _():
                for cp in _scatter_copies(scatter_names, h_refs, r_refs, send, recv):
                    cp.start()
        _load_once(w_hbm, w_v, sem)

        @pl.when(i == 0)
        def _():
            nacc[...] = jnp.zeros_like(nacc)

        dh = None
        for p in range(npc):
            part = lax.dot_general(p_refs[p][...].astype(BF16), w_v[:, offs[p]:offs[p + 1]], NT_DIMS,
                                   preferred_element_type=F32)
            dh = part if dh is None else dh + part
        xv = x_ref[...]
        rs = lax.rsqrt(jnp.mean(xv * xv, axis=-1, keepdims=True) + EPS)
        xh = xv * rs
        nacc[...] += _cs8(dh * xh)
        dxh = dh * nw_ref[...]
        dx_ref[...] = dr_ref[...] + rs * (dxh - xh * jnp.mean(dxh * xh, axis=-1, keepdims=True))

        @pl.when(i == ni - 1)
        def _():
            dnw_ref[...] = jnp.sum(nacc[...], axis=0, keepdims=True)
            if ns:
                for cp in _scatter_copies(scatter_names, h_refs, r_refs, send, recv):
                    cp.wait()

    row = pl.BlockSpec((tm, D_MODEL), lambda i: (i, 0))
    outs = _call(
        body, name, (ni,),
        [pl.BlockSpec((tm, p.shape[1]), lambda i: (i, 0)) for p in pieces] +
        [ANY, row, pl.BlockSpec((1, D_MODEL), lambda i: (0, 0)), row] + [ANY] * ns,
        [row, pl.BlockSpec((1, D_MODEL), lambda i: (0, 0))] + [ANY] * ns,
        [jax.ShapeDtypeStruct((S, D_MODEL), F32), jax.ShapeDtypeStruct((1, D_MODEL), F32)] +
        [jax.ShapeDtypeStruct((3,) + _shard_shape(h.shape, n), h.dtype) for n, h in zip(scatter_names, scatter_halves)],
        scratch=[pltpu.VMEM(w.shape, BF16), pltpu.SemaphoreType.DMA(()), pltpu.VMEM((SUBLANES, D_MODEL), F32)] +
        (_scatter_sems(scatter_names) if ns else []),
        sem=("arbitrary",))(*pieces, w, x, nw, dres, *scatter_halves)
    return outs[0], outs[1], list(outs[2:])


def _mm_wgrad(at, pieces, tn, name):
    M, S = at.shape
    starts, counts, nj = _piece_blocks(pieces, tn, 1)
    npc = len(pieces)

    def body(*refs):
        a_hbm = refs[0]
        p_refs = refs[1:1 + npc]
        o_ref, o16_ref, a_v, sem = refs[1 + npc:]
        j = pl.program_id(0)
        _load_once(a_hbm, a_v, sem)
        for p in range(npc):
            @pl.when((j >= starts[p]) & (j < starts[p] + counts[p]))
            def _(p=p):
                acc = jnp.dot(a_v[...], p_refs[p][...].astype(BF16), preferred_element_type=F32)
                o_ref[...] = acc
                o16_ref[...] = acc.astype(BF16)

    def pspec(p):
        return pl.BlockSpec((S, tn), lambda j: (0, jnp.clip(j - starts[p], 0, counts[p] - 1)))

    col = pl.BlockSpec((M, tn), lambda j: (0, j))
    return _call(
        body, name, (nj,),
        [ANY] + [pspec(p) for p in range(npc)], [col, col],
        [jax.ShapeDtypeStruct((M, nj * tn), F32), jax.ShapeDtypeStruct((M, nj * tn), BF16)],
        scratch=[pltpu.VMEM(at.shape, BF16), pltpu.SemaphoreType.DMA(())], sem=("arbitrary",))(at, *pieces)


def _stream_spec(d, T):
    return pl.BlockSpec((d, T // d, A_WIDTH), lambda i: (0, i, 0))


def _stream_shape(d, S, dtype):
    return jax.ShapeDtypeStruct((d, S // d, A_WIDTH), dtype)


N_CHUNK = A_WIDTH // LANES


def _to_tokens(ref, scr, d, T):
    if d == 1:
        return ref[0].astype(F32)
    for r in range(d):
        for ch in range(N_CHUNK):
            scr.at[ch][pl.ds(r, T // d, stride=d), :] = ref[r, :, ch * LANES:(ch + 1) * LANES].astype(F32)
    return _get(scr)


def _from_tokens(out_ref, scr, d, T):
    for r in range(d):
        for ch in range(N_CHUNK):
            out_ref[r, :, ch * LANES:(ch + 1) * LANES] = scr.at[ch][pl.ds(r, T // d, stride=d), :].astype(out_ref.dtype)


def _put(scr, val):
    for ch in range(N_CHUNK):
        scr[ch] = val[:, ch * LANES:(ch + 1) * LANES]


def _get(scr):
    return jnp.concatenate([scr[ch] for ch in range(N_CHUNK)], axis=1)


def _chunked(T):
    return pltpu.VMEM((N_CHUNK, T, LANES), F32)


def _compact_spec(d, T):
    return pl.BlockSpec((d, T // d, LANES), lambda i: (0, i, 0))


def _compact_shape(d, S):
    return jax.ShapeDtypeStruct((d, S // d, LANES), F32)


def _compact_to_tokens(ref, scr, d, T):
    if d == 1:
        return ref[0]
    for r in range(d):
        scr[pl.ds(r, T // d, stride=d), :] = ref[r]
    return scr[...]


def _compact_from_tokens(out_ref, scr, val, d, T):
    if d == 1:
        out_ref[0] = val
        return
    scr[...] = val
    for r in range(d):
        out_ref[r] = scr[pl.ds(r, T // d, stride=d), :]


def _head_expander():
    r = lax.broadcasted_iota(jnp.int32, (LANES, A_WIDTH), 0)
    c = lax.broadcasted_iota(jnp.int32, (LANES, A_WIDTH), 1) // HEAD_DIM
    return (r == c).astype(BF16)


def _head_reducer():
    r = lax.broadcasted_iota(jnp.int32, (A_WIDTH, LANES), 0) // HEAD_DIM
    c = lax.broadcasted_iota(jnp.int32, (A_WIDTH, LANES), 1)
    return (r == c).astype(BF16)


def _qkv_prep(proj, pos, freq, wq, wk, T):
    S = proj.shape[0]
    qk_w = 3 * A_WIDTH

    def body(q_ref, k_ref, v_ref, pos_ref, f_ref, wq_ref, wk_ref, *rest):
        outs, tabs, scr = rest[:9], rest[9:12], rest[12]
        seg_mean = _seg_mean()
        c, s1, s2 = _rope_tables(pos_ref, f_ref)
        for tab, val in zip(tabs, (c, s1, s2)):
            tab[...] = val
        for t, (src, w_ref) in enumerate(((q_ref, wq_ref), (k_ref, wk_ref), (v_ref, None))):
            for g in range(3):
                d = A_GROUPS[g][1]
                out = outs[3 * t + g]
                for ch in range(A_WIDTH // LANES):
                    cs = slice(ch * LANES, (ch + 1) * LANES)
                    v = src[:, g * A_WIDTH + ch * LANES: g * A_WIDTH + (ch + 1) * LANES]
                    if w_ref is not None:
                        y = v * _head_rstd(v, seg_mean) * w_ref[...]
                        v = y * c + pltpu.roll(y, 8, 1) * s1 + pltpu.roll(y, LANES - 8, 1) * s2
                    if d == 1:
                        out[0, :, cs] = v.astype(BF16)
                    else:
                        scr[ch] = v
                if d > 1:
                    _from_tokens(out, scr, d, T)

    ds_ = [A_GROUPS[g][1] for g in range(3)] * 3
    return _call(
        body, "qkv_prep", (S // T,),
        [pl.BlockSpec((T, qk_w), lambda i: (i, 0)), pl.BlockSpec((T, qk_w), lambda i: (i, 1)),
         pl.BlockSpec((T, qk_w), lambda i: (i, 2)),
         pl.BlockSpec((T, 1), lambda i: (i, 0)), pl.BlockSpec((1, LANES), lambda i: (0, 0)),
         pl.BlockSpec((1, LANES), lambda i: (0, 0)), pl.BlockSpec((1, LANES), lambda i: (0, 0))],
        [_stream_spec(d, T) for d in ds_] + [pl.BlockSpec((T, LANES), lambda i: (i, 0))] * 3,
        [_stream_shape(d, S, BF16) for d in ds_] + [jax.ShapeDtypeStruct((S, LANES), F32)] * 3,
        scratch=[_chunked(T)], sem=("parallel",))(proj, proj, proj, pos, freq, wq, wk)


def _attn_mask(i):
    qi = lax.broadcasted_iota(jnp.int32, (QBLK, 2 * QBLK), 0) + QBLK
    kj = lax.broadcasted_iota(jnp.int32, (QBLK, 2 * QBLK), 1)
    dist = qi - kj
    return (dist >= 0) & (dist <= QBLK) & ((i > 0) | (kj >= QBLK))


ATT_BLK = (None, QBLK, A_WIDTH)
ATT_CBLK = (None, QBLK, LANES)
FWD_BLOCKS = 4


def _first_head_lanes():
    return lax.broadcasted_iota(jnp.int32, (1, LANES), 1) < HEAD_DIM


def _split_heads(v, first):
    zero = jnp.zeros_like(v)
    return jnp.where(first, v, zero), jnp.where(first, zero, v)


def _attn_fwd(q, k, v, g):
    d, n, _ = q.shape
    nb = n // QBLK
    assert nb % FWD_BLOCKS == 0, (n, QBLK)
    rows2 = FWD_BLOCKS * QBLK

    def body(q_ref, kp_ref, kc_ref, vp_ref, vc_ref, o_ref, l_ref, s_scr, p_scr):
        i = pl.program_id(1)
        masks = (_attn_mask(i),) + (_attn_mask(1),) * (FWD_BLOCKS - 1)
        first = _first_head_lanes()

        def keys(prev_ref, cur_ref, b, ps):
            if b == 0:
                return jnp.concatenate([prev_ref[:, ps], cur_ref[0:QBLK, ps]], axis=0)
            return cur_ref[(b - 1) * QBLK:(b + 1) * QBLK, ps]

        for b in range(FWD_BLOCKS):
            rows = slice(b * QBLK, (b + 1) * QBLK)
            for pr in range(A_HEADS // 2):
                ps = slice(pr * LANES, (pr + 1) * LANES)
                kc = keys(kp_ref, kc_ref, b, ps)
                for e, qh in enumerate(_split_heads(q_ref[rows, ps], first)):
                    s_scr[b * A_HEADS + 2 * pr + e] = lax.dot_general(qh, kc, NT_DIMS, preferred_element_type=F32)
        lane = lax.broadcasted_iota(jnp.int32, (1, LANES), 1)
        for b in range(FWD_BLOCKS):
            lrow = jnp.zeros((QBLK, LANES), F32)
            for h in range(A_HEADS):
                s = jnp.where(masks[b], s_scr[b * A_HEADS + h] * (HEAD_DIM ** -0.5), NEG)
                m = jnp.max(s, axis=-1, keepdims=True)
                p = jnp.exp(s - m)
                den = jnp.sum(p, axis=-1, keepdims=True)
                p_scr[b * A_HEADS + h] = (p / den).astype(BF16)
                lrow = jnp.where(lane == h, m + jnp.log(den), lrow)
            l_ref[b * QBLK:(b + 1) * QBLK, :] = lrow
        for b in range(FWD_BLOCKS):
            for pr in range(A_HEADS // 2):
                ps = slice(pr * LANES, (pr + 1) * LANES)
                va, vb = _split_heads(keys(vp_ref, vc_ref, b, ps), first)
                o_ref[b * QBLK:(b + 1) * QBLK, ps] = (
                    jnp.dot(p_scr[b * A_HEADS + 2 * pr], va, preferred_element_type=F32) +
                    jnp.dot(p_scr[b * A_HEADS + 2 * pr + 1], vb, preferred_element_type=F32)).astype(BF16)

    prev = lambda r, i: (r, jnp.maximum(FWD_BLOCKS * i - 1, 0), 0)
    cur = lambda r, i: (r, i, 0)
    wide = (None, rows2, A_WIDTH)
    units = FWD_BLOCKS * A_HEADS
    return _call(
        body, "attn_fwd_g%d" % g, (d, nb // FWD_BLOCKS),
        [pl.BlockSpec(wide, cur), pl.BlockSpec(ATT_BLK, prev), pl.BlockSpec(wide, cur),
         pl.BlockSpec(ATT_BLK, prev), pl.BlockSpec(wide, cur)],
        [pl.BlockSpec(wide, cur), pl.BlockSpec((None, rows2, LANES), cur)],
        [jax.ShapeDtypeStruct((d, n, A_WIDTH), BF16), jax.ShapeDtypeStruct((d, n, LANES), F32)],
        scratch=[pltpu.VMEM((units, QBLK, 2 * QBLK), F32), pltpu.VMEM((units, QBLK, 2 * QBLK), BF16)],
        sem=("parallel", "parallel"))(q, k, k, v, v)


def _attn_bwd(q, k, v, do, lse, cg, g):
    d, n, _ = q.shape
    nb = n // QBLK
    scale = HEAD_DIM ** -0.5

    def body(q_ref, kp_ref, kc_ref, vp_ref, vc_ref, do_ref, l_ref, c_ref, dq_ref, dk_ref, dv_ref, ck, cv,
             s_scr, dp_scr, p_scr, ds_scr):
        i = pl.program_id(1)

        @pl.when(i == 0)
        def _():
            ck[...] = jnp.zeros_like(ck)
            cv[...] = jnp.zeros_like(cv)

        @pl.when(i < nb)
        def _():
            mask = _attn_mask(i)
            first = _first_head_lanes()
            for pr in range(A_HEADS // 2):
                ps = slice(pr * LANES, (pr + 1) * LANES)
                kc = jnp.concatenate([kp_ref[:, ps], kc_ref[:, ps]], axis=0)
                vc = jnp.concatenate([vp_ref[:, ps], vc_ref[:, ps]], axis=0)
                qs = _split_heads(q_ref[:, ps], first)
                dos = _split_heads(do_ref[:, ps], first)
                for e in range(2):
                    s_scr[2 * pr + e] = lax.dot_general(qs[e], kc, NT_DIMS, preferred_element_type=F32)
                    dp_scr[2 * pr + e] = lax.dot_general(dos[e], vc, NT_DIMS, preferred_element_type=F32)
            for h in range(A_HEADS):
                p = jnp.where(mask, jnp.exp(s_scr[h] * scale - l_ref[:, h:h + 1]), 0.0)
                p_scr[h] = p.astype(BF16)
                ds_scr[h] = (p * (dp_scr[h] + c_ref[:, h:h + 1]) * scale).astype(BF16)
            for pr in range(A_HEADS // 2):
                ps = slice(pr * LANES, (pr + 1) * LANES)
                ks = _split_heads(jnp.concatenate([kp_ref[:, ps], kc_ref[:, ps]], axis=0), first)
                qs = _split_heads(q_ref[:, ps], first)
                dos = _split_heads(do_ref[:, ps], first)
                dq = dkc = dvc = None
                for e in range(2):
                    ds = ds_scr[2 * pr + e]
                    a = jnp.dot(ds, ks[e], preferred_element_type=F32)
                    b = lax.dot_general(ds, qs[e], TN_DIMS, preferred_element_type=F32)
                    c = lax.dot_general(p_scr[2 * pr + e], dos[e], TN_DIMS, preferred_element_type=F32)
                    dq, dkc, dvc = (a, b, c) if e == 0 else (dq + a, dkc + b, dvc + c)
                dq_ref[:, ps] = dq.astype(BF16)
                dk_ref[:, ps] = (ck[:, ps] + dkc[:QBLK]).astype(BF16)
                dv_ref[:, ps] = (cv[:, ps] + dvc[:QBLK]).astype(BF16)
                ck[:, ps] = dkc[QBLK:]
                cv[:, ps] = dvc[QBLK:]

        @pl.when(i == nb)
        def _():
            dk_ref[...] = ck[...].astype(BF16)
            dv_ref[...] = cv[...].astype(BF16)

    qi = lambda i: jnp.minimum(i, nb - 1)
    cur = lambda r, i: (r, qi(i), 0)
    prev = lambda r, i: (r, jnp.maximum(qi(i) - 1, 0), 0)
    late = lambda r, i: (r, jnp.maximum(i - 1, 0), 0)
    return _call(
        body, "attn_bwd_g%d" % g, (d, nb + 1),
        [pl.BlockSpec(ATT_BLK, cur), pl.BlockSpec(ATT_BLK, prev), pl.BlockSpec(ATT_BLK, cur),
         pl.BlockSpec(ATT_BLK, prev), pl.BlockSpec(ATT_BLK, cur),
         pl.BlockSpec(ATT_BLK, cur), pl.BlockSpec(ATT_CBLK, cur), pl.BlockSpec(ATT_CBLK, cur)],
        [pl.BlockSpec(ATT_BLK, cur), pl.BlockSpec(ATT_BLK, late), pl.BlockSpec(ATT_BLK, late)],
        [jax.ShapeDtypeStruct((d, n, A_WIDTH), BF16)] * 3,
        scratch=[pltpu.VMEM((QBLK, A_WIDTH), F32), pltpu.VMEM((QBLK, A_WIDTH), F32),
                 pltpu.VMEM((A_HEADS, QBLK, 2 * QBLK), F32), pltpu.VMEM((A_HEADS, QBLK, 2 * QBLK), F32),
                 pltpu.VMEM((A_HEADS, QBLK, 2 * QBLK), BF16), pltpu.VMEM((A_HEADS, QBLK, 2 * QBLK), BF16)],
        sem=("parallel", "arbitrary"))(q, k, k, v, v, do, lse, cg)


def _merge_weights(l0, l1, l2):
    mx = jnp.maximum(jnp.maximum(l0, l1), l2)
    e0, e1, e2 = jnp.exp(l0 - mx), jnp.exp(l1 - mx), jnp.exp(l2 - mx)
    den = e0 + e1 + e2
    return e0 / den, e1 / den, e2 / den


def _even_specs(T, S):
    t8 = T // SUBLANES
    last8 = S // SUBLANES - 1
    col = lambda c: pl.BlockSpec((T, A_WIDTH), lambda i: (i, c))
    prev8 = lambda c: pl.BlockSpec((SUBLANES, A_WIDTH), lambda i: (jnp.maximum(i * t8 - 1, 0), c))
    next8 = lambda c: pl.BlockSpec((SUBLANES, A_WIDTH), lambda i: (jnp.minimum((i + 1) * t8, last8), c))
    return col, prev8, next8


GROUP_D = tuple(d for _, d in A_GROUPS)


def _even_mixer_fwd(x, proj, os_, ls_, conv_w, w_out, T):
    S = proj.shape[0]
    col, prev8, _ = _even_specs(T, S)
    H = SUBLANES

    def body(x_ref, w_ref, bg_r, cg_r, hb_r, zl_r, zh_r, cgp_r, hbp_r, o0, o1, o2, l0, l1, l2, cw_r,
             x1_ref, ut_ref, ext, cscr, *scr):
        i = pl.program_id(0)
        ls = [_compact_to_tokens(r, cscr, GROUP_D[g], T) for g, r in enumerate((l0, l1, l2))]
        expand = _head_expander()
        ws = [_segsum(w, expand) for w in _merge_weights(*ls)]
        oa = ws[0] * _to_tokens(o0, scr[0], GROUP_D[0], T)
        oa = oa + ws[1] * _to_tokens(o1, scr[1], GROUP_D[1], T)
        oa = oa + ws[2] * _to_tokens(o2, scr[2], GROUP_D[2], T)
        ext[0:H, :] = jnp.where(i == 0, 0.0, cgp_r[...] * hbp_r[...])
        ext[H:H + T, :] = cg_r[...] * hb_r[...]
        conv = cw_r[0:1, :] * ext[H - 2:H - 2 + T, :]
        for kk in range(1, SC_WIDTH):
            conv = conv + cw_r[kk:kk + 1, :] * ext[H - 2 + kk:H - 2 + kk + T, :]
        zl, zh = zl_r[...], zh_r[...]
        x1_ref[...] = x_ref[...] + _out_projection(ut_ref, w_ref, oa * (zl * _sig(zl)),
                                                   bg_r[...] * conv * (zh * _sig(zh)))

    streams = [_stream_spec(d, T) for d in GROUP_D]
    compacts = [_compact_spec(d, T) for d in GROUP_D]
    row = pl.BlockSpec((T, D_MODEL), lambda i: (i, 0))
    return _call(
        body, "even_mixer_fwd", (S // T,),
        [row, pl.BlockSpec((D_MODEL, D_MODEL), lambda i: (0, 0)),
         col(9), col(10), col(11), col(12), col(13), prev8(10), prev8(11)] + streams + compacts +
        [pl.BlockSpec((SC_WIDTH, A_WIDTH), lambda i: (0, 0))],
        [row, pl.BlockSpec((D_MODEL, T), lambda i: (0, i))],
        [jax.ShapeDtypeStruct((S, D_MODEL), F32), jax.ShapeDtypeStruct((D_MODEL, S), BF16)],
        scratch=[pltpu.VMEM((T + H, A_WIDTH), F32), pltpu.VMEM((T, LANES), F32)] + [_chunked(T)] * 3,
        sem=("parallel",))(
            x, w_out, proj, proj, proj, proj, proj, proj, proj, *os_, *ls_, conv_w)


def _even_mixer_bwd(dy, w_out, proj, os_, ls_, conv_w, T):
    S = proj.shape[0]
    nt = S // T
    col, prev8, next8 = _even_specs(T, S)
    H = SUBLANES
    t8 = T // SUBLANES
    last8 = S // SUBLANES - 1

    def body(dy_r, dyn_r, w_ref, bg_r, cg_r, hb_r, zl_r, zh_r, cgp_r, hbp_r, zhn_r, bgn_r,
             o0, o1, o2, l0, l1, l2, cw_r,
             do0, do1, do2, c0, c1, c2, dr_ref, dcw_ref, ext_t, ext_d, acc, cscr, s_a, s_b, s_c):
        i = pl.program_id(0)

        @pl.when(i == 0)
        def _():
            acc[...] = jnp.zeros_like(acc)

        zl, zh = zl_r[...], zh_r[...]
        sl, sh = _sig(zl), _sig(zh)
        du = lax.dot_general(dy_r[...].astype(BF16), w_ref[...], NT_DIMS, preferred_element_type=F32)
        dul, duh = du[:, 0:A_WIDTH], du[:, A_WIDTH:]
        dun = lax.dot_general(dyn_r[...].astype(BF16), w_ref[A_WIDTH:, :], NT_DIMS, preferred_element_type=F32)
        scr = (s_a, s_b, s_c)
        ls = [_compact_to_tokens(r, cscr, GROUP_D[g], T) for g, r in enumerate((l0, l1, l2))]
        wcs = _merge_weights(*ls)
        expand = _head_expander()
        ws = [_segsum(w, expand) for w in wcs]
        oa = ws[0] * _to_tokens(o0, scr[0], GROUP_D[0], T)
        oa = oa + ws[1] * _to_tokens(o1, scr[1], GROUP_D[1], T)
        oa = oa + ws[2] * _to_tokens(o2, scr[2], GROUP_D[2], T)
        doa = dul * (zl * sl)
        rsum = _segsum(doa * oa, _head_reducer())
        for g, (do_ref, c_ref) in enumerate(((do0, c0), (do1, c1), (do2, c2))):
            d = GROUP_D[g]
            _compact_from_tokens(c_ref, cscr, -wcs[g] * rsum, d, T)
            if d == 1:
                do_ref[0] = (ws[g] * doa).astype(BF16)
            else:
                _put(s_c, ws[g] * doa)
                _from_tokens(do_ref, s_c, d, T)
        cgv, hbv, bgv = cg_r[...], hb_r[...], bg_r[...]
        ext_t[0:H, :] = jnp.where(i == 0, 0.0, cgp_r[...] * hbp_r[...])
        ext_t[H:H + T, :] = cgv * hbv
        conv = cw_r[0:1, :] * ext_t[H - 2:H - 2 + T, :]
        for kk in range(1, SC_WIDTH):
            conv = conv + cw_r[kk:kk + 1, :] * ext_t[H - 2 + kk:H - 2 + kk + T, :]
        dyb = duh * (zh * sh)
        dconv = dyb * bgv
        zn = zhn_r[...]
        ext_d[0:T, :] = dconv
        ext_d[T:T + H, :] = jnp.where(i == nt - 1, 0.0, dun * (zn * _sig(zn)) * bgn_r[...])
        dt = cw_r[0:1, :] * ext_d[2:2 + T, :]
        for kk in range(1, SC_WIDTH):
            dt = dt + cw_r[kk:kk + 1, :] * ext_d[2 - kk:2 - kk + T, :]
        for kk in range(SC_WIDTH):
            acc[kk * SUBLANES:(kk + 1) * SUBLANES, :] += _cs8(dconv * ext_t[H - 2 + kk:H - 2 + kk + T, :])
        dr_ref[:, 0:A_WIDTH] = (dyb * conv).astype(BF16)
        dr_ref[:, A_WIDTH:2 * A_WIDTH] = (dt * hbv).astype(BF16)
        dr_ref[:, 2 * A_WIDTH:3 * A_WIDTH] = (dt * cgv).astype(BF16)
        dr_ref[:, 3 * A_WIDTH:4 * A_WIDTH] = (dul * oa * _dsilu(zl, sl)).astype(BF16)
        dr_ref[:, 4 * A_WIDTH:5 * A_WIDTH] = (duh * (bgv * conv) * _dsilu(zh, sh)).astype(BF16)

        @pl.when(i == nt - 1)
        def _():
            for kk in range(SC_WIDTH):
                dcw_ref[kk:kk + 1, :] = jnp.sum(acc[kk * SUBLANES:(kk + 1) * SUBLANES, :], axis=0, keepdims=True)

    streams = [_stream_spec(d, T) for d in GROUP_D]
    dynext = pl.BlockSpec((SUBLANES, D_MODEL), lambda i: (jnp.minimum((i + 1) * t8, last8), 0))
    compacts = [_compact_spec(d, T) for d in GROUP_D]
    outs = _call(
        body, "even_mixer_bwd", (nt,),
        [pl.BlockSpec((T, D_MODEL), lambda i: (i, 0)), dynext, pl.BlockSpec((D_MODEL, D_MODEL), lambda i: (0, 0)),
         col(9), col(10), col(11), col(12), col(13), prev8(10), prev8(11), next8(13), next8(9)] +
        streams + compacts + [pl.BlockSpec((SC_WIDTH, A_WIDTH), lambda i: (0, 0))],
        streams + compacts + [pl.BlockSpec((T, 5 * A_WIDTH), lambda i: (i, 0)),
                              pl.BlockSpec((SC_WIDTH, A_WIDTH), lambda i: (0, 0))],
        [_stream_shape(d, S, BF16) for d in GROUP_D] + [_compact_shape(d, S) for d in GROUP_D] +
        [jax.ShapeDtypeStruct((S, 5 * A_WIDTH), BF16), jax.ShapeDtypeStruct((SC_WIDTH, A_WIDTH), F32)],
        scratch=[pltpu.VMEM((T + H, A_WIDTH), F32), pltpu.VMEM((T + H, A_WIDTH), F32),
                 pltpu.VMEM((SC_WIDTH * SUBLANES, A_WIDTH), F32), pltpu.VMEM((T, LANES), F32)] +
                [_chunked(T)] * 3,
        sem=("arbitrary",))(dy, dy, w_out, proj, proj, proj, proj, proj, proj, proj, proj, proj, *os_, *ls_, conv_w)
    return outs[0:3], outs[3:6], outs[6], outs[7]


def _qk_bwd(proj, dqs, dks, dvs, rope, wq, wk, T):
    S = proj.shape[0]
    nt = S // T
    qk_w = 3 * A_WIDTH

    def body(q_ref, k_ref, dq0, dq1, dq2, dk0, dk1, dk2, dv0, dv1, dv2, c_ref, s1_ref, s2_ref, wq_ref, wk_ref,
             o_ref, dw_ref, acc, scr):
        i = pl.program_id(0)

        @pl.when(i == 0)
        def _():
            acc[...] = jnp.zeros_like(acc)
            dw_ref[...] = jnp.zeros_like(dw_ref)

        seg_mean = _seg_mean()
        c, s1, s2 = c_ref[...], s1_ref[...], s2_ref[...]
        for t, (src, w_ref, ds) in enumerate(((q_ref, wq_ref, (dq0, dq1, dq2)), (k_ref, wk_ref, (dk0, dk1, dk2)))):
            wv = w_ref[...]
            for g in range(3):
                d = GROUP_D[g]
                if d > 1:
                    _to_tokens(ds[g], scr, d, T)
                for ch in range(A_WIDTH // LANES):
                    cs = slice(g * A_WIDTH + ch * LANES, g * A_WIDTH + (ch + 1) * LANES)
                    lc = slice(ch * LANES, (ch + 1) * LANES)
                    v = src[:, cs]
                    dout = ds[g][0, :, lc].astype(F32) if d == 1 else scr[ch]
                    rs = lax.rsqrt(_segsum(v * v, seg_mean) + EPS)
                    xh = v * rs
                    dy = dout * c + pltpu.roll(dout * s1, LANES - 8, 1) + pltpu.roll(dout * s2, 8, 1)
                    acc[t * SUBLANES:(t + 1) * SUBLANES, :] += _cs8(dy * xh)
                    dxh = dy * wv
                    mean = _segsum(dxh * xh, seg_mean)
                    o_ref[:, t * qk_w + g * A_WIDTH + ch * LANES: t * qk_w + g * A_WIDTH + (ch + 1) * LANES] = (
                        rs * (dxh - xh * mean)).astype(BF16)
        for g, dv in enumerate((dv0, dv1, dv2)):
            d = GROUP_D[g]
            base = 2 * qk_w + g * A_WIDTH
            o_ref[:, base:base + A_WIDTH] = _to_tokens(dv, scr, d, T).astype(BF16)

        @pl.when(i == nt - 1)
        def _():
            for t in range(2):
                srow = jnp.sum(acc[t * SUBLANES:(t + 1) * SUBLANES, :], axis=0, keepdims=True)
                dw_ref[t:t + 1, :] = srow + pltpu.roll(srow, HEAD_DIM, 1)

    streams = [_stream_spec(d, T) for d in GROUP_D]
    return _call(
        body, "qk_bwd", (nt,),
        [pl.BlockSpec((T, qk_w), lambda i: (i, 0)), pl.BlockSpec((T, qk_w), lambda i: (i, 1))] + streams * 3 +
        [pl.BlockSpec((T, LANES), lambda i: (i, 0))] * 3 +
        [pl.BlockSpec((1, LANES), lambda i: (0, 0)), pl.BlockSpec((1, LANES), lambda i: (0, 0))],
        [pl.BlockSpec((T, 3 * qk_w), lambda i: (i, 0)), pl.BlockSpec((SUBLANES, LANES), lambda i: (0, 0))],
        [jax.ShapeDtypeStruct((S, 3 * qk_w), BF16), jax.ShapeDtypeStruct((SUBLANES, LANES), F32)],
        scratch=[pltpu.VMEM((2 * SUBLANES, LANES), F32), _chunked(T)], sem=("arbitrary",))(
            proj, proj, *dqs, *dks, *dvs, *rope, wq, wk)


N_SMALL_ODD = 40
SHIFT_ROWS_LESS = SUBLANES


def _fill_shifted(ext_ref, sh_ref):
    rows = ext_ref.shape[0] - SHIFT_ROWS_LESS
    for b in range(1, SUBLANES):
        sh_ref[b - 1] = ext_ref[b:b + rows, :]


def _window(ext_ref, sh_ref, off, T):
    a, b = divmod(off, SUBLANES)
    if b == 0:
        return ext_ref[off:off + T, :]
    return sh_ref[b - 1, a * SUBLANES:a * SUBLANES + T, :]


def _odd_pool_tile(i, uc_r, ucp_r, pw_r, ext_u, pooled_s, pm_s, T):
    H = HALO
    uc = uc_r[...]
    ext_u[0:H, :] = jnp.where(i == 0, 0.0, ucp_r[...])
    ext_u[H:H + T, :] = uc
    row = i * T + lax.broadcasted_iota(jnp.int32, (T, 1), 0)
    for g, p in enumerate(POOL_SIZES):
        cs = slice(g * LANES, (g + 1) * LANES)
        win = ext_u[H:H + T, cs]
        for j in range(1, p):
            win = win + ext_u[H - j:H - j + T, cs]
        cnt = jnp.minimum(row + 1, p).astype(F32)
        pooled = win / cnt - uc[:, cs]
        pooled_s[:, cs] = pooled
        pm_s[:, cs] = jnp.dot(pooled.astype(BF16), pw_r[g].astype(BF16), preferred_element_type=F32)
    return row


def _odd_glu_tile(i, da_r, dg_r, dap_r, dgp_r, ext_g, sh_g, T):
    H = HALO
    ext_g[0:H, :] = jnp.where(i == 0, 0.0, dap_r[...] * _sig(dgp_r[...]))
    ext_g[H:H + T, :] = da_r[...] * _sig(dg_r[...])
    _fill_shifted(ext_g, sh_g)


def _odd_specs(T, S, order):
    tb = T // HALO
    col = lambda c: pl.BlockSpec((T, A_WIDTH), lambda s: (order(s), c))
    prev = lambda c: pl.BlockSpec((HALO, A_WIDTH), lambda s: (jnp.maximum(order(s) * tb - 1, 0), c))
    const2 = lambda shape: pl.BlockSpec(shape, lambda s: (0, 0))
    weights = [pl.BlockSpec((4, LANES, LANES), lambda s: (0, 0, 0)), const2((1, A_WIDTH)),
               const2((D_CONV, A_WIDTH)), const2((1, A_WIDTH)), const2((1, A_WIDTH)), const2((1, A_WIDTH))]
    return col, prev, weights


def _odd_mixer_fwd(x, tgt, proj, pool_w, scale, dconv_w, dconv_b, ln_w, ln_b, w_out, T):
    S = proj.shape[0]
    nt = S // T
    col, prev, wspecs = _odd_specs(T, S, lambda s: s)
    H = HALO

    def body(x_ref, t_ref, w_ref, uc_r, da_r, dg_r, zl_r, zh_r, ucp_r, dap_r, dgp_r, pw_r, sc_r, dw_r, db_r,
             lw_r, lb_r, dy_ref, l_ref, ut_ref, cv_ref, ext_u, ext_g, sh_g, pooled_s, pm_s, lacc):
        i = pl.program_id(0)

        @pl.when(i == 0)
        def _():
            lacc[...] = jnp.zeros_like(lacc)

        _odd_pool_tile(i, uc_r, ucp_r, pw_r, ext_u, pooled_s, pm_s, T)
        _odd_glu_tile(i, da_r, dg_r, dap_r, dgp_r, ext_g, sh_g, T)
        base = H - (D_CONV - 1)
        conv = db_r[...] + dw_r[0:1, :] * _window(ext_g, sh_g, base, T)
        for kk in range(1, D_CONV):
            conv = conv + dw_r[kk:kk + 1, :] * _window(ext_g, sh_g, base + kk, T)
        cv_ref[...] = conv
        mu = jnp.mean(conv, axis=-1, keepdims=True)
        xc = conv - mu
        yh = xc * lax.rsqrt(jnp.mean(xc * xc, axis=-1, keepdims=True) + EPS)
        ln = yh * lw_r[...] + lb_r[...]
        zl, zh = zl_r[...], zh_r[...]
        y = x_ref[...] + _out_projection(ut_ref, w_ref, pm_s[...] * sc_r[...] * (zl * _sig(zl)),
                                         ln * _sig(ln) * (zh * _sig(zh)))
        diff = y - t_ref[...]
        dy_ref[...] = diff / float(D_MODEL)
        lacc[...] += _cs8(diff * diff)

        @pl.when(i == nt - 1)
        def _():
            l_ref[...] = jnp.sum(lacc[...], axis=0, keepdims=True)

    row = pl.BlockSpec((T, D_MODEL), lambda i: (i, 0))
    return _call(
        body, "odd_mixer_fwd", (nt,),
        [row, row, pl.BlockSpec((D_MODEL, D_MODEL), lambda i: (0, 0)),
         col(0), col(1), col(2), col(3), col(4), prev(0), prev(1), prev(2)] + wspecs,
        [row, pl.BlockSpec((1, D_MODEL), lambda i: (0, 0)), pl.BlockSpec((D_MODEL, T), lambda i: (0, i)),
         pl.BlockSpec((T, A_WIDTH), lambda i: (i, 0))],
        [jax.ShapeDtypeStruct((S, D_MODEL), F32), jax.ShapeDtypeStruct((1, D_MODEL), F32),
         jax.ShapeDtypeStruct((D_MODEL, S), BF16), jax.ShapeDtypeStruct((S, A_WIDTH), F32)],
        scratch=[pltpu.VMEM((T + H, A_WIDTH), F32), pltpu.VMEM((T + H, A_WIDTH), F32),
                 pltpu.VMEM((SUBLANES - 1, T + H - SHIFT_ROWS_LESS, A_WIDTH), F32),
                 pltpu.VMEM((T, A_WIDTH), F32), pltpu.VMEM((T, A_WIDTH), F32),
                 pltpu.VMEM((SUBLANES, D_MODEL), F32)],
        sem=("arbitrary",))(x, tgt, w_out, proj, proj, proj, proj, proj, proj, proj, proj,
                            pool_w, scale, dconv_w, dconv_b, ln_w, ln_b)


def _odd_mixer_bwd(dy, w_out, proj, conv, pool_w, scale, dconv_w, dconv_b, ln_w, ln_b, T):
    S = proj.shape[0]
    nt = S // T
    order = lambda s: nt - 1 - s
    col, prev, wspecs = _odd_specs(T, S, order)
    H = HALO

    def body(dy_r, w_ref, cv_r, uc_r, da_r, dg_r, zl_r, zh_r, ucp_r, dap_r, dgp_r, pw_r, sc_r, dw_r, db_r, lw_r, lb_r,
             dp_ref, dpw_ref, sm_ref, ext_u, ext_g, sh_g, pooled_s, pm_s, dpl_s, ext_p, ext_c, sh_c, acc):
        step = pl.program_id(0)
        i = nt - 1 - step

        @pl.when(step == 0)
        def _():
            ext_p[T:T + H, :] = jnp.zeros((H, A_WIDTH), F32)
            ext_c[T:T + H, :] = jnp.zeros((H, A_WIDTH), F32)
            acc[...] = jnp.zeros_like(acc)
            dpw_ref[...] = jnp.zeros_like(dpw_ref)

        def accum(r, v):
            acc[r * SUBLANES:(r + 1) * SUBLANES, :] += _cs8(v)

        row = _odd_pool_tile(i, uc_r, ucp_r, pw_r, ext_u, pooled_s, pm_s, T)
        _odd_glu_tile(i, da_r, dg_r, dap_r, dgp_r, ext_g, sh_g, T)
        conv = cv_r[...]
        mu = jnp.mean(conv, axis=-1, keepdims=True)
        xc = conv - mu
        rstd = lax.rsqrt(jnp.mean(xc * xc, axis=-1, keepdims=True) + EPS)
        yh = xc * rstd
        ln = yh * lw_r[...] + lb_r[...]
        sln = _sig(ln)
        zl, zh = zl_r[...], zh_r[...]
        sl, sh = _sig(zl), _sig(zh)
        du = lax.dot_general(dy_r[...].astype(BF16), w_ref[...], NT_DIMS, preferred_element_type=F32)
        dul, duh = du[:, 0:A_WIDTH], du[:, A_WIDTH:]
        pm = pm_s[...]
        scv = sc_r[...]
        dyc = dul * (zl * sl)
        accum(34, dyc * pm)
        dpm = dyc * scv
        for g in range(len(POOL_SIZES)):
            cs = slice(g * LANES, (g + 1) * LANES)
            dpm_g = dpm[:, cs].astype(BF16)
            dpw_ref[g] += lax.dot_general(pooled_s[:, cs].astype(BF16), dpm_g, TN_DIMS, preferred_element_type=F32)
            dpl_s[:, cs] = lax.dot_general(dpm_g, pw_r[g].astype(BF16), NT_DIMS, preferred_element_type=F32)
        lane_p = lax.broadcasted_iota(jnp.int32, (1, A_WIDTH), 1) // LANES
        pvec = jnp.left_shift(2, lane_p)
        cnt = jnp.minimum(row + 1, pvec).astype(F32)
        dpl = dpl_s[...]
        ext_p[0:T, :] = dpl / cnt
        for g, p in enumerate(POOL_SIZES):
            cs = slice(g * LANES, (g + 1) * LANES)
            win = ext_p[0:T, cs]
            for j in range(1, p):
                win = win + ext_p[j:j + T, cs]
            dp_ref[:, cs] = (win - dpl[:, cs]).astype(BF16)
        ext_p[T:T + H, :] = ext_p[0:H, :]
        dln = duh * (zh * sh) * _dsilu(ln, sln)
        accum(32, dln * yh)
        accum(33, dln)
        dyh = dln * lw_r[...]
        dc = rstd * (dyh - jnp.mean(dyh, axis=-1, keepdims=True) - yh * jnp.mean(dyh * yh, axis=-1, keepdims=True))
        accum(31, dc)
        ext_c[0:T, :] = dc
        _fill_shifted(ext_c, sh_c)
        base = H - (D_CONV - 1)
        dgl = dw_r[0:1, :] * _window(ext_c, sh_c, D_CONV - 1, T)
        accum(0, dc * _window(ext_g, sh_g, base, T))
        for kk in range(1, D_CONV):
            dgl = dgl + dw_r[kk:kk + 1, :] * _window(ext_c, sh_c, D_CONV - 1 - kk, T)
            accum(kk, dc * _window(ext_g, sh_g, base + kk, T))
        ext_c[T:T + H, :] = ext_c[0:H, :]
        dav, dgv = da_r[...], dg_r[...]
        sg = _sig(dgv)
        dp_ref[:, A_WIDTH:2 * A_WIDTH] = (dgl * sg).astype(BF16)
        dp_ref[:, 2 * A_WIDTH:3 * A_WIDTH] = (dgl * dav * sg * (1.0 - sg)).astype(BF16)
        dp_ref[:, 3 * A_WIDTH:4 * A_WIDTH] = (dul * (pm * scv) * _dsilu(zl, sl)).astype(BF16)
        dp_ref[:, 4 * A_WIDTH:5 * A_WIDTH] = (duh * (ln * sln) * _dsilu(zh, sh)).astype(BF16)

        @pl.when(step == nt - 1)
        def _():
            for r in range(N_SMALL_ODD):
                sm_ref[r:r + 1, :] = jnp.sum(acc[r * SUBLANES:(r + 1) * SUBLANES, :], axis=0, keepdims=True)

    ext = pltpu.VMEM((T + H, A_WIDTH), F32)
    shifted = pltpu.VMEM((SUBLANES - 1, T + H - SHIFT_ROWS_LESS, A_WIDTH), F32)
    tile = pltpu.VMEM((T, A_WIDTH), F32)
    return _call(
        body, "odd_mixer_bwd", (nt,),
        [pl.BlockSpec((T, D_MODEL), lambda s: (order(s), 0)), pl.BlockSpec((D_MODEL, D_MODEL), lambda s: (0, 0)),
         pl.BlockSpec((T, A_WIDTH), lambda s: (order(s), 0)),
         col(0), col(1), col(2), col(3), col(4), prev(0), prev(1), prev(2)] + wspecs,
        [pl.BlockSpec((T, ODD_IN), lambda s: (order(s), 0)),
         pl.BlockSpec((4, LANES, LANES), lambda s: (0, 0, 0)),
         pl.BlockSpec((N_SMALL_ODD, A_WIDTH), lambda s: (0, 0))],
        [jax.ShapeDtypeStruct((S, ODD_IN), BF16), jax.ShapeDtypeStruct((4, LANES, LANES), F32),
         jax.ShapeDtypeStruct((N_SMALL_ODD, A_WIDTH), F32)],
        scratch=[ext, ext, shifted, tile, tile, tile, ext, ext, shifted,
                 pltpu.VMEM((N_SMALL_ODD * SUBLANES, A_WIDTH), F32)],
        sem=("arbitrary",))(dy, w_out, conv, proj, proj, proj, proj, proj, proj, proj, proj,
                            pool_w, scale, dconv_w, dconv_b, ln_w, ln_b)


TILE_SEQ = 512
TILE_WG = 256
TILE_FIRST = 512
TILE_MM = 512


SMALL_PACK = "small_pack"
SMALL_PACK_W = 2 * LANES
LATE_WEIGHTS = ("e_w_out", "o_w_in", "o_w_out", SMALL_PACK)
ODD_MATS = ("o_w_in", "o_w_out")
EVEN_MATS = ("e_w_in", "e_w_out")


def _reduce_start(names, grads, grads16, cidx):
    recv = _swap_to_sibling(names, [grads16[n] for n in names], "swap_halves_" + names[0][0], True)
    both = [_add_half(cidx, grads[n], r, n) for n, r in zip(names, recv)]
    return [h for h, _ in both], [hb for _, hb in both]


def _local_step(x, pos, tgt, shards, p, unpack_small, cidx, bidx):
    T = TILE_SEQ
    freq = _freq_table()
    wq = jnp.tile(p["e_q_norm_w"], (1, LANES // HEAD_DIM))
    wk = jnp.tile(p["e_k_norm_w"], (1, LANES // HEAD_DIM))

    proj_e, ht_e, w_e_in, late = _inproj_gathering(x, p["e_norm_w"], shards["e_w_in"], bidx, "e_w_in", LATE_WEIGHTS,
                                                   [shards[n] for n in LATE_WEIGHTS], TILE_FIRST, "inproj_even")
    wb = dict(zip(LATE_WEIGHTS, late), e_w_in=w_e_in)
    p = dict(p, **unpack_small(wb[SMALL_PACK]))
    qkv = _qkv_prep(proj_e, pos, freq, wq, wk, T)
    qs, ks, vs, rope = qkv[0:3], qkv[3:6], qkv[6:9], qkv[9:12]
    os_, ls_ = [], []
    for g in range(3):
        o, l = _attn_fwd(qs[g], ks[g], vs[g], g)
        os_.append(o)
        ls_.append(l)
    x1, ut_e = _even_mixer_fwd(x, proj_e, os_, ls_, p["e_conv_w"], wb["e_w_out"], T)
    proj_o, ht_o = _inproj(x1, p["o_norm_w"], wb["o_w_in"], TILE_MM, 1280, "inproj_odd")
    odd_w = (p["o_pool_w"], p["o_pool_scale"], p["o_dconv_w"], p["o_dconv_b"], p["o_ln_w"], p["o_ln_b"])
    dy, lsum, ut_o, conv_o = _odd_mixer_fwd(x1, tgt, proj_o, *odd_w, wb["o_w_out"], T)

    g, g16 = {}, {}
    g["o_w_out"], g16["o_w_out"] = _mm_wgrad(ut_o, [dy], TILE_WG, "wgrad_o_out")
    dproj_o, g["o_pool_w"], small_o = _odd_mixer_bwd(dy, wb["o_w_out"], proj_o, conv_o, *odd_w, T)
    g["o_w_in"], g16["o_w_in"] = _mm_wgrad(ht_o, [dproj_o], TILE_WG, "wgrad_o_in")
    half_o, half_o16 = _reduce_start(ODD_MATS, g, g16, cidx)
    dx1, g["o_norm_w"], blocks_o = _mm_nt_rms([dproj_o], wb["o_w_in"], x1, p["o_norm_w"], dy, TILE_MM, "dx_odd",
                                              ODD_MATS, half_o16)
    g["o_dconv_w"] = small_o[0:D_CONV]
    g["o_dconv_b"] = small_o[31:32]
    g["o_ln_w"] = small_o[32:33]
    g["o_ln_b"] = small_o[33:34]
    g["o_pool_scale"] = small_o[34:35]

    g["e_w_out"], g16["e_w_out"] = _mm_wgrad(ut_e, [dx1], TILE_WG, "wgrad_e_out")
    dos, cgs, drest, g["e_conv_w"] = _even_mixer_bwd(dx1, wb["e_w_out"], proj_e, os_, ls_, p["e_conv_w"], T)
    dqs, dks, dvs = [], [], []
    for gi in range(3):
        dq, dk, dv = _attn_bwd(qs[gi], ks[gi], vs[gi], dos[gi], ls_[gi], cgs[gi], gi)
        dqs.append(dq)
        dks.append(dk)
        dvs.append(dv)
    dqkv, dnw = _qk_bwd(proj_e, dqs, dks, dvs, rope, wq, wk, T)
    g["e_q_norm_w"] = dnw[0:1, 0:HEAD_DIM]
    g["e_k_norm_w"] = dnw[1:2, 0:HEAD_DIM]
    pieces = [dqkv, drest]
    g["e_w_in"], g16["e_w_in"] = _mm_wgrad(ht_e, pieces, TILE_WG, "wgrad_e_in")
    half_e, half_e16 = _reduce_start(EVEN_MATS, g, g16, cidx)
    dx, g["e_norm_w"], blocks_e = _mm_nt_rms(pieces, wb["e_w_in"], x, p["e_norm_w"], dx1, TILE_MM, "dx_even",
                                             EVEN_MATS, half_e16)
    parts = {}
    for names, halves, blocks in ((ODD_MATS, half_o, blocks_o), (EVEN_MATS, half_e, blocks_e)):
        for n, h, r in zip(names, halves, blocks):
            parts[n] = _add_blocks(bidx, h, r, n)
    return lsum, dx, g, parts


BIG = ("e_w_in", "e_w_out", "o_w_in", "o_w_out")
SHARD_AXIS = {"e_w_in": 1, "e_w_out": 0, "o_w_in": 1, "o_w_out": 0, SMALL_PACK: 0}
N_CHIPS = 4


def _place():
    x, y, c = lax.axis_index("x"), lax.axis_index("y"), lax.axis_index("c")
    chips = [(1 - x, y), (x, 1 - y), (1 - x, 1 - y)]
    return x, y, c, chips


def _block_of(ref, name, block):
    rows, cols = ref.shape
    if SHARD_AXIS[name] == 1:
        cw = cols // N_CHIPS
        return ref.at[:, pl.ds(pl.multiple_of(block * cw, LANES), cw)]
    rw = rows // N_CHIPS
    return ref.at[pl.ds(pl.multiple_of(block * rw, rw), rw), :]


def _half_of(ref, name, half):
    rows, cols = ref.shape
    if SHARD_AXIS[name] == 1:
        return ref.at[pl.ds(pl.multiple_of(half * (rows // 2), rows // 2), rows // 2), :]
    return ref.at[:, pl.ds(pl.multiple_of(half * (cols // 2), LANES), cols // 2)]


def _sub(ref, name, block, half):
    rows, cols = ref.shape
    if SHARD_AXIS[name] == 1:
        cw, hr = cols // N_CHIPS, rows // 2
        return ref.at[pl.ds(pl.multiple_of(half * hr, hr), hr), pl.ds(pl.multiple_of(block * cw, LANES), cw)]
    rw, hc = rows // N_CHIPS, cols // 2
    return ref.at[pl.ds(pl.multiple_of(block * rw, rw), rw), pl.ds(pl.multiple_of(half * hc, LANES), hc)]


GATHER_COPIES = 7


class _Gather:
    def __init__(self, names, s_refs, f_refs, send, recv):
        self.names, self.s, self.f, self.send, self.recv = names, s_refs, f_refs, send, recv

    def _copy(self, k, src, dst, to):
        return pltpu.make_async_remote_copy(src_ref=src, dst_ref=dst, send_sem=self.send.at[k],
                                            recv_sem=self.recv.at[k], device_id=to, device_id_type=MESH)

    def _plan(self):
        x, y, c, chips = _place()
        me, sib = 2 * x + y, (x, y, 1 - c)
        first, relay_in, relay, last_in = [], [], [], []
        for wi, n in enumerate(self.names):
            k0 = wi * GATHER_COPIES
            s, f = self.s[wi], self.f[wi]
            own = _block_of(f, n, me)
            first.append(self._copy(k0 + 3, s, own, sib))
            last_in.append(self._copy(k0 + 3, s, own, sib))
            for j, (cx, cy) in enumerate(chips):
                first.append(self._copy(k0 + j, _half_of(s, n, c), _sub(f, n, me, c), (cx, cy, c)))
                mine = _sub(f, n, 2 * cx + cy, c)
                relay_in.append(self._copy(k0 + j, mine, mine, sib))
                relay.append(self._copy(k0 + 4 + j, mine, mine, sib))
                theirs = _sub(f, n, 2 * cx + cy, 1 - c)
                last_in.append(self._copy(k0 + 4 + j, theirs, theirs, sib))
        return first, relay_in, relay, last_in

    N_RELATIONS = 3

    def begin(self, relations=(0, 1, 2), sibling=True):
        first = self._plan()[0]
        for wi in range(len(self.names)):
            mine = first[wi * (1 + self.N_RELATIONS):(wi + 1) * (1 + self.N_RELATIONS)]
            if sibling:
                mine[0].start()
            for j in relations:
                mine[1 + j].start()

    def relay(self, relations=(0, 1, 2)):
        _, relay_in, relay, _ = self._plan()
        for wi in range(len(self.names)):
            for j in relations:
                relay_in[wi * self.N_RELATIONS + j].wait_recv()
                relay[wi * self.N_RELATIONS + j].start()

    def end(self):
        first, _, relay, last_in = self._plan()
        for cp in last_in:
            cp.wait_recv()
        for cp in first + relay:
            cp.wait_send()

    def wait_relayed(self, j):
        self._plan()[3][1 + j].wait_recv()

    def end_rest(self):
        first, _, relay, last_in = self._plan()
        last_in[0].wait_recv()
        for cp in first + relay:
            cp.wait_send()


def _full_shape(n, s):
    r, cdim = s.shape
    return jax.ShapeDtypeStruct((r, cdim * N_CHIPS) if SHARD_AXIS[n] == 1 else (r * N_CHIPS, cdim), s.dtype)


def _gather_sems(names):
    k = GATHER_COPIES * len(names)
    return [pltpu.SemaphoreType.DMA((k,)), pltpu.SemaphoreType.DMA((k,))]


def _scatter_copies(names, h_refs, r_refs, send, recv):
    _, _, c, chips = _place()
    cps = []
    for wi, n in enumerate(names):
        for j, (cx, cy) in enumerate(chips):
            cps.append(pltpu.make_async_remote_copy(
                src_ref=_block_of(h_refs[wi], n, 2 * cx + cy), dst_ref=r_refs[wi].at[j],
                send_sem=send.at[wi * 3 + j], recv_sem=recv.at[wi * 3 + j],
                device_id=(cx, cy, c), device_id_type=MESH))
    return cps


def _scatter_sems(names):
    return [pltpu.SemaphoreType.DMA((3 * len(names),)), pltpu.SemaphoreType.DMA((3 * len(names),))]


class _SmallSum:
    def __init__(self, p_ref, o_ref, sbuf, cbuf, send, recv):
        self.p, self.o, self.sbuf, self.cbuf, self.send, self.recv = p_ref, o_ref, sbuf, cbuf, send, recv

    def _copy(self, k, ref, to):
        return pltpu.make_async_remote_copy(src_ref=ref, dst_ref=ref, send_sem=self.send.at[k],
                                            recv_sem=self.recv.at[k], device_id=to, device_id_type=MESH)

    def _plan(self):
        x, y, c, chips = _place()
        me, sib = 2 * x + y, (x, y, 1 - c)
        d2d_out = self._copy(0, self.sbuf.at[c], sib)
        d2d_in = self._copy(0, self.sbuf.at[1 - c], sib)
        ici_out = [self._copy(1 + j, self.cbuf.at[me], (cx, cy, c)) for j, (cx, cy) in enumerate(chips)]
        ici_in = [self._copy(1 + j, self.cbuf.at[2 * cx + cy], (cx, cy, c)) for j, (cx, cy) in enumerate(chips)]
        return c, me, d2d_out, d2d_in, ici_out, ici_in

    def begin(self):
        c, _, d2d_out, _, _, _ = self._plan()
        self.sbuf[c] = self.p[...]
        d2d_out.start()

    def middle(self):
        _, me, _, d2d_in, ici_out, _ = self._plan()
        d2d_in.wait_recv()
        self.cbuf[me] = self.sbuf[0] + self.sbuf[1]
        for cp in ici_out:
            cp.start()

    def end(self):
        _, _, d2d_out, _, ici_out, ici_in = self._plan()
        for cp in ici_in:
            cp.wait_recv()
        self.o[...] = (self.cbuf[0] + self.cbuf[1]) + (self.cbuf[2] + self.cbuf[3])
        for cp in [d2d_out] + ici_out:
            cp.wait_send()


def _small_sum_scratch(R):
    return [pltpu.VMEM((2, R, LANES), F32), pltpu.VMEM((N_CHIPS, R, LANES), F32),
            pltpu.SemaphoreType.DMA((4,)), pltpu.SemaphoreType.DMA((4,))]


def _half_shape(shape, name):
    r, cdim = shape
    return (r // 2, cdim) if SHARD_AXIS[name] == 1 else (r, cdim // 2)


def _shard_shape(shape, name):
    r, cdim = shape
    return (r, cdim // N_CHIPS) if SHARD_AXIS[name] == 1 else (r // N_CHIPS, cdim)


def _swap_to_sibling(names, srcs, name, pick_half, small=None):
    nw = len(names)
    ns = 0 if small is None else 1
    vm = pl.BlockSpec(memory_space=pltpu.VMEM)

    def body(*refs):
        g_refs = refs[:nw]
        r_refs = refs[nw + ns:2 * nw + ns]
        send, recv = refs[2 * nw + 2 * ns:2 * nw + 2 * ns + 2]
        x, y, c, _ = _place()
        sib = (x, y, 1 - c)
        cps = []
        for wi, n in enumerate(names):
            src = _half_of(g_refs[wi], n, 1 - c) if pick_half else g_refs[wi]
            cp = pltpu.make_async_remote_copy(src_ref=src, dst_ref=r_refs[wi], send_sem=send.at[wi],
                                              recv_sem=recv.at[wi], device_id=sib, device_id_type=MESH)
            cp.start()
            cps.append(cp)
        if ns:
            total = _SmallSum(refs[nw], refs[2 * nw + ns], *refs[2 * nw + 2 * ns + 2:])
            total.begin()
            total.middle()
            total.end()
        for cp in cps:
            cp.wait()

    outs = [jax.ShapeDtypeStruct(_half_shape(g.shape, n) if pick_half else g.shape, g.dtype)
            for n, g in zip(names, srcs)]
    return pl.pallas_call(
        body, name=name, in_specs=[ANY] * nw + [vm] * ns, out_specs=[ANY] * nw + [vm] * ns,
        out_shape=outs + ([jax.ShapeDtypeStruct(small.shape, F32)] if ns else []),
        scratch_shapes=[pltpu.SemaphoreType.DMA((nw,)), pltpu.SemaphoreType.DMA((nw,))] +
        (_small_sum_scratch(small.shape[0]) if ns else []),
    )(*srcs, *([small] if ns else []))


def _add_half(cidx, g, r, name):
    rows, cols = r.shape
    tr = 256
    tc = cols if cols <= 1792 else (1792 if cols % 1792 == 0 else 1280)
    nr, nc = rows // tr, cols // tc

    def body(c_ref, g_ref, r_ref, o_ref, ob_ref):
        s = g_ref[...] + r_ref[...].astype(F32)
        o_ref[...] = s
        ob_ref[...] = s.astype(BF16)

    if SHARD_AXIS[name] == 1:
        gmap = lambda i, j, c_ref: (c_ref[0] * nr + i, j)
    else:
        gmap = lambda i, j, c_ref: (i, c_ref[0] * nc + j)
    same = lambda i, j, c_ref: (i, j)
    return pl.pallas_call(
        body, name="add_half_" + name,
        grid_spec=pltpu.PrefetchScalarGridSpec(
            num_scalar_prefetch=1, grid=(nr, nc),
            in_specs=[pl.BlockSpec((tr, tc), gmap), pl.BlockSpec((tr, tc), same)],
            out_specs=[pl.BlockSpec((tr, tc), same), pl.BlockSpec((tr, tc), same)]),
        out_shape=[jax.ShapeDtypeStruct(r.shape, F32), jax.ShapeDtypeStruct(r.shape, BF16)],
        compiler_params=pltpu.CompilerParams(dimension_semantics=("parallel", "parallel"), vmem_limit_bytes=VMEM_LIMIT),
    )(cidx, g, r)


def _add_blocks(bidx, h, r, name):
    _, rows, cols = r.shape
    tr = min(rows, 256)
    nr = rows // tr

    def body(b_ref, h_ref, r0, r1, r2, o_ref):
        o_ref[...] = ((h_ref[...] + r0[0].astype(F32)) + r1[0].astype(F32)) + r2[0].astype(F32)

    if SHARD_AXIS[name] == 1:
        hmap = lambda i, b_ref: (i, b_ref[0])
    else:
        hmap = lambda i, b_ref: (b_ref[0] * nr + i, 0)
    rspec = lambda j: pl.BlockSpec((1, tr, cols), lambda i, b_ref, j=j: (j, i, 0))
    return pl.pallas_call(
        body, name="add_blocks_" + name,
        grid_spec=pltpu.PrefetchScalarGridSpec(
            num_scalar_prefetch=1, grid=(nr,),
            in_specs=[pl.BlockSpec((tr, cols), hmap), rspec(0), rspec(1), rspec(2)],
            out_specs=pl.BlockSpec((tr, cols), lambda i, b_ref: (i, 0))),
        out_shape=jax.ShapeDtypeStruct((rows, cols), F32),
        compiler_params=pltpu.CompilerParams(dimension_semantics=("parallel",), vmem_limit_bytes=VMEM_LIMIT),
    )(bidx, h, r, r, r)


def _adam_math(w, g, m, v):
    c1 = 1.0 - ADAM_B1 ** ADAM_STEP
    c2 = 1.0 - ADAM_B2 ** ADAM_STEP
    nm = ADAM_B1 * m + (1.0 - ADAM_B1) * g
    nv = ADAM_B2 * v + (1.0 - ADAM_B2) * (g * g)
    delta = -ADAM_LR * ((nm / c1) / (jnp.sqrt(nv / c2) + ADAM_EPS) + ADAM_WD * w)
    return delta, nm, nv


def _adamw(w, g, m, v, name):
    def body(w_ref, g_ref, m_ref, v_ref, d_ref, nm_ref, nv_ref):
        d_ref[...], nm_ref[...], nv_ref[...] = _adam_math(w_ref[...], g_ref[...], m_ref[...], v_ref[...])

    spec = pl.BlockSpec(w.shape, lambda i: (0, 0))
    return _call(body, "adamw_" + name, (1,), [spec] * 4, [spec] * 3,
                 [jax.ShapeDtypeStruct(w.shape, F32)] * 3, sem=("arbitrary",))(w, g, m, v)


def _adamw_halves(cidx, w, mine, theirs, m, v, name):
    hr, hc = mine.shape
    tr = 128
    ni = hr // tr
    if SHARD_AXIS[name] == 1:
        wmap = lambda hh, i, c_ref: (hh * ni + i, 0)
    else:
        wmap = lambda hh, i, c_ref: (i, hh)
    hmap = lambda hh, i, c_ref: (i, 0)

    def body(c_ref, w_ref, a_ref, b_ref, m_ref, v_ref, g_ref, d_ref, nm_ref, nv_ref):
        g = jnp.where(pl.program_id(0) == c_ref[0], a_ref[...], b_ref[...])
        g_ref[...] = g
        d_ref[...], nm_ref[...], nv_ref[...] = _adam_math(w_ref[...], g, m_ref[...], v_ref[...])

    wspec = pl.BlockSpec((tr, hc), wmap)
    hspec = pl.BlockSpec((tr, hc), hmap)
    return pl.pallas_call(
        body, name="adamw_" + name,
        grid_spec=pltpu.PrefetchScalarGridSpec(
            num_scalar_prefetch=1, grid=(2, ni),
            in_specs=[wspec, hspec, hspec, wspec, wspec], out_specs=[wspec] * 4),
        out_shape=[jax.ShapeDtypeStruct(w.shape, F32)] * 4,
        compiler_params=pltpu.CompilerParams(dimension_semantics=("parallel", "parallel"), vmem_limit_bytes=VMEM_LIMIT),
    )(cidx, w, mine, theirs, m, v)


SMALL = ("e_norm_w", "e_q_norm_w", "e_k_norm_w", "e_conv_w", "o_norm_w", "o_pool_w", "o_pool_scale",
         "o_dconv_w", "o_dconv_b", "o_ln_w", "o_ln_b")
SMALL_SHARDED = ("e_conv_w", "o_norm_w", "o_pool_scale", "o_dconv_w", "o_dconv_b", "o_ln_w", "o_ln_b")
WEIGHTS = ("e_norm_w", "e_w_in", "e_q_norm_w", "e_k_norm_w", "e_conv_w", "e_w_out", "o_norm_w", "o_w_in",
           "o_pool_w", "o_pool_scale", "o_dconv_w", "o_dconv_b", "o_ln_w", "o_ln_b", "o_w_out")


def _pack(arrs):
    flat = jnp.concatenate([a.reshape(-1) for a in arrs])
    rows = -(-flat.shape[0] // (LANES * SUBLANES)) * SUBLANES
    flat = jnp.pad(flat, (0, rows * LANES - flat.shape[0]))
    return flat.reshape(rows, LANES)


def _unpack(packed, shapes):
    flat = packed.reshape(-1)
    out, off = [], 0
    for s in shapes:
        n = int(np.prod(s))
        out.append(flat[off:off + n].reshape(s))
        off += n
    return out


def _gather_last(a, block, width):
    return lax.dynamic_slice_in_dim(a, block * width, width, axis=a.ndim - 1)


def kernel(x, positions, e_norm_w, e_w_in, e_q_norm_w, e_k_norm_w, e_conv_w, e_w_out, o_norm_w, o_w_in, o_pool_w, o_pool_scale, o_dconv_w, o_dconv_b, o_ln_w, o_ln_b, o_w_out, loss_target, m_e_norm_w, m_e_w_in, m_e_q_norm_w, m_e_k_norm_w, m_e_conv_w, m_e_w_out, m_o_norm_w, m_o_w_in, m_o_pool_w, m_o_pool_scale, m_o_dconv_w, m_o_dconv_b, m_o_ln_w, m_o_ln_b, m_o_w_out, v_e_norm_w, v_e_w_in, v_e_q_norm_w, v_e_k_norm_w, v_e_conv_w, v_e_w_out, v_o_norm_w, v_o_w_in, v_o_pool_w, v_o_pool_scale, v_o_dconv_w, v_o_dconv_b, v_o_ln_w, v_o_ln_b, v_o_w_out):
    given = dict(e_norm_w=e_norm_w, e_w_in=e_w_in, e_q_norm_w=e_q_norm_w, e_k_norm_w=e_k_norm_w, e_conv_w=e_conv_w,
                 e_w_out=e_w_out, o_norm_w=o_norm_w, o_w_in=o_w_in, o_pool_w=o_pool_w, o_pool_scale=o_pool_scale,
                 o_dconv_w=o_dconv_w, o_dconv_b=o_dconv_b, o_ln_w=o_ln_w, o_ln_b=o_ln_b, o_w_out=o_w_out)
    mom = dict(e_norm_w=m_e_norm_w, e_w_in=m_e_w_in, e_q_norm_w=m_e_q_norm_w, e_k_norm_w=m_e_k_norm_w,
               e_conv_w=m_e_conv_w, e_w_out=m_e_w_out, o_norm_w=m_o_norm_w, o_w_in=m_o_w_in, o_pool_w=m_o_pool_w,
               o_pool_scale=m_o_pool_scale, o_dconv_w=m_o_dconv_w, o_dconv_b=m_o_dconv_b, o_ln_w=m_o_ln_w,
               o_ln_b=m_o_ln_b, o_w_out=m_o_w_out)
    var = dict(e_norm_w=v_e_norm_w, e_w_in=v_e_w_in, e_q_norm_w=v_e_q_norm_w, e_k_norm_w=v_e_k_norm_w,
               e_conv_w=v_e_conv_w, e_w_out=v_e_w_out, o_norm_w=v_o_norm_w, o_w_in=v_o_w_in, o_pool_w=v_o_pool_w,
               o_pool_scale=v_o_pool_scale, o_dconv_w=v_o_dconv_w, o_dconv_b=v_o_dconv_b, o_ln_w=v_o_ln_w,
               o_ln_b=v_o_ln_b, o_w_out=v_o_w_out)
    S = x.shape[1]
    mx, my, mc = lax.axis_index("x"), lax.axis_index("y"), lax.axis_index("c")
    chip = 2 * mx + my
    cidx = jnp.reshape(mc, (1,)).astype(jnp.int32)
    bidx = jnp.reshape(chip, (1,)).astype(jnp.int32)

    shards = {n: given[n][0].astype(BF16) for n in BIG}
    shard_sizes = [int(np.prod(given[n].shape)) for n in SMALL_SHARDED]
    flat = jnp.concatenate([given[n].reshape(-1) for n in SMALL_SHARDED])
    rows = -(-flat.shape[0] // (SMALL_PACK_W * SUBLANES)) * SUBLANES
    shards[SMALL_PACK] = jnp.pad(flat, (0, rows * SMALL_PACK_W - flat.shape[0])).reshape(rows, SMALL_PACK_W)

    def unpack_small(full):
        gathered = full.reshape(N_CHIPS, rows * SMALL_PACK_W)
        out, off = {}, 0
        for n, size in zip(SMALL_SHARDED, shard_sizes):
            sh = given[n].shape[1:]
            parts = gathered[:, off:off + size].reshape((N_CHIPS,) + sh)
            fullp = jnp.moveaxis(parts, 0, -2).reshape(sh[:-1] + (N_CHIPS * sh[-1],))
            out[n] = fullp.reshape(-1, fullp.shape[-1])
            off += size
        return out

    p = dict(e_norm_w=e_norm_w, e_q_norm_w=e_q_norm_w, e_k_norm_w=e_k_norm_w, o_pool_w=o_pool_w[0])
    lsum, dx, g, parts = _local_step(x[0], positions.reshape(S, 1), loss_target[0], shards, p, unpack_small,
                                     cidx, bidx)
    mine = [g[n] for n in SMALL] + [(0.5 / float(D_MODEL)) * jnp.sum(lsum, keepdims=True)]
    *theirs, tot = _swap_to_sibling(BIG, [parts[n] for n in BIG], "swap_reduced", False, small=_pack(mine))
    tot = _unpack(tot, [a.shape for a in mine])
    loss = tot[-1].reshape(())

    grads, delta, new_m, new_v = {}, {}, {}, {}
    for n, other in zip(BIG, theirs):
        sh = given[n].shape
        outs = _adamw_halves(cidx, given[n][0], parts[n], other, mom[n][0], var[n][0], n)
        grads[n], delta[n], new_m[n], new_v[n] = [a.reshape(sh) for a in outs]
    for n, gv in zip(SMALL, tot):
        if n in SMALL_SHARDED:
            gv = _gather_last(gv, chip, gv.shape[-1] // N_CHIPS)
        grads[n] = gv.reshape(given[n].shape)
    big_small = "o_pool_w"
    pw = [src[big_small].reshape(-1, LANES) for src in (given, grads, mom, var)]
    for dst, a in zip((delta, new_m, new_v), _adamw(*pw, "pool_w")):
        dst[big_small] = a.reshape(given[big_small].shape)
    tiny = tuple(n for n in SMALL if n != big_small)
    shapes = [given[n].shape for n in tiny]
    packed = [_pack([src[n] for n in tiny]) for src in (given, grads, mom, var)]
    for dst, pk in zip((delta, new_m, new_v), _adamw(*packed, "small")):
        for n, a in zip(tiny, _unpack(pk, shapes)):
            dst[n] = a
    return (loss, dx[None], *[grads[n] for n in WEIGHTS], *[delta[n] for n in WEIGHTS],
            *[new_m[n] for n in WEIGHTS], *[new_v[n] for n in WEIGHTS])
```

```python
import numpy as np
import jax
import jax.numpy as jnp
from jax import lax
from jax.experimental import pallas as pl
from jax.experimental.pallas import tpu as pltpu

F32 = jnp.float32
BF16 = jnp.bfloat16

D_MODEL = 1024
HEAD_DIM = 64
A_WIDTH = 512
A_HEADS = 8
A_GROUPS = ((128, 1), (512, 4), (2048, 16))
QBLK = 128
ROT_DIM = 16
ROPE_THETA = 500000.0
POOL_SIZES = (2, 4, 8, 16)
D_CONV = 31
SC_WIDTH = 3
EVEN_IN = 7168
ODD_IN = 2560
EPS = 1e-6
NEG = -1e30
ADAM_LR, ADAM_B1, ADAM_B2, ADAM_EPS, ADAM_WD, ADAM_STEP = 0.001, 0.9, 0.999, 1e-08, 0.01, 10

LANES = 128
SUBLANES = 8
HALO = 32
VMEM_LIMIT = 52 * 1024 * 1024
MESH = pl.DeviceIdType.MESH
ANY = pl.BlockSpec(memory_space=pl.ANY)

NT_DIMS = (((1,), (1,)), ((), ()))
TN_DIMS = (((0,), (0,)), ((), ()))


def _call(body, name, grid, in_specs, out_specs, out_shape, scratch=(), sem=None, aliases=None):
    return pl.pallas_call(
        body, name=name, grid=grid, in_specs=in_specs, out_specs=out_specs, out_shape=out_shape,
        scratch_shapes=list(scratch), input_output_aliases=aliases or {},
        compiler_params=pltpu.CompilerParams(dimension_semantics=sem, vmem_limit_bytes=VMEM_LIMIT))


def _sig(v):
    return jax.nn.sigmoid(v)


def _dsilu(v, s):
    return s * (1.0 + v * (1.0 - s))


def _out_projection(ut_ref, w_ref, lo, hi):
    acc = None
    for k, v in enumerate((lo, hi)):
        ut_ref[k * A_WIDTH:(k + 1) * A_WIDTH, :] = v.T.astype(BF16)
        part = jnp.dot(v.astype(BF16), w_ref[k * A_WIDTH:(k + 1) * A_WIDTH, :], preferred_element_type=F32)
        acc = part if acc is None else acc + part
    return acc


def _cs8(v):
    return v.reshape(v.shape[0] // SUBLANES, SUBLANES, v.shape[1]).sum(axis=0)


def _seg_mean():
    r = lax.broadcasted_iota(jnp.int32, (LANES, LANES), 0) // HEAD_DIM
    c = lax.broadcasted_iota(jnp.int32, (LANES, LANES), 1) // HEAD_DIM
    return jnp.where(r == c, 1.0 / HEAD_DIM, 0.0).astype(BF16)


def _segsum(v, ones):
    hi = v.astype(BF16)
    lo = (v - hi.astype(F32)).astype(BF16)
    return (jnp.dot(hi, ones, preferred_element_type=F32) + jnp.dot(lo, ones, preferred_element_type=F32))


def _head_rstd(v, seg_mean):
    return lax.rsqrt(jnp.dot((v * v).astype(BF16), seg_mean, preferred_element_type=F32) + EPS)


def _rope_tables(pos_ref, freq_ref):
    ang = pos_ref[...].astype(F32) * freq_ref[...]
    cosv, sinv = jnp.cos(ang), jnp.sin(ang)
    lm = lax.broadcasted_iota(jnp.int32, ang.shape, 1) % HEAD_DIM
    half = ROT_DIM // 2
    c = jnp.where(lm < ROT_DIM, cosv, 1.0)
    s1 = jnp.where((lm >= half) & (lm < ROT_DIM), sinv, 0.0)
    s2 = jnp.where(lm < half, -sinv, 0.0)
    return c, s1, s2


def _freq_table():
    half = ROT_DIM // 2
    inv = ROPE_THETA ** (-np.arange(half, dtype=np.float64) / half)
    lane = np.arange(LANES) % HEAD_DIM
    f = np.where(lane < ROT_DIM, inv[lane % half], 0.0)
    return jnp.asarray(f.reshape(1, LANES), F32)


def _load_once(hbm_ref, vmem_ref, sem):
    @pl.when(pl.program_id(0) == 0)
    def _():
        cp = pltpu.make_async_copy(hbm_ref, vmem_ref, sem)
        cp.start()
        cp.wait()


def _rms_rows(x_ref, nw_ref):
    xv = x_ref[...]
    ms = jnp.mean(xv * xv, axis=-1, keepdims=True)
    return xv * lax.rsqrt(ms + EPS) * nw_ref[...]


def _inproj(x, nw, w, tm, tn, name):
    S, N = x.shape[0], w.shape[1]

    def body(x_ref, nw_ref, w_hbm, o_ref, ht_ref, w_v, sem):
        _load_once(w_hbm, w_v, sem)
        h = _rms_rows(x_ref, nw_ref)
        ht_ref[...] = h.T.astype(BF16)
        hb = h.astype(BF16)
        for j in range(N // tn):
            o_ref[:, j * tn:(j + 1) * tn] = jnp.dot(hb, w_v[:, j * tn:(j + 1) * tn], preferred_element_type=F32)

    return _call(
        body, name, (S // tm,),
        [pl.BlockSpec((tm, D_MODEL), lambda i: (i, 0)),
         pl.BlockSpec((1, D_MODEL), lambda i: (0, 0)), ANY],
        [pl.BlockSpec((tm, N), lambda i: (i, 0)),
         pl.BlockSpec((D_MODEL, tm), lambda i: (0, i))],
        [jax.ShapeDtypeStruct((S, N), F32), jax.ShapeDtypeStruct((D_MODEL, S), BF16)],
        scratch=[pltpu.VMEM(w.shape, BF16), pltpu.SemaphoreType.DMA(())], sem=("arbitrary",))(x, nw, w)


def _inproj_gathering(x, nw, shard, bidx, first, late_names, late_shards, tm, name):
    S = x.shape[0]
    ni = S // tm
    K, cw = shard.shape
    nl = len(late_names)
    last = N_CHIPS - 1

    def body(b_ref, x_ref, nw_ref, s_hbm, *rest):
        ls_refs = rest[:nl]
        o_ref, ht_ref, f_hbm = rest[nl:nl + 3]
        lf_refs = rest[nl + 3:2 * nl + 3]
        hs, w_blk, lsem, send1, recv1, send2, recv2 = rest[2 * nl + 3:]
        j, i = pl.program_id(0), pl.program_id(1)
        g1 = _Gather((first,), (s_hbm,), (f_hbm,), send1, recv1)
        g2 = _Gather(late_names, ls_refs, lf_refs, send2, recv2)
        _, _, _, chips = _place()

        def load_block(src):
            cp = pltpu.make_async_copy(src, w_blk, lsem)
            cp.start()
            cp.wait()

        @pl.when((j == 0) & (i == 0))
        def _():
            g1.begin(relations=(0, 1))
            load_block(s_hbm)

        for r, (cx, cy) in enumerate(chips):
            @pl.when((j == r + 1) & (i == 0))
            def _(r=r, cx=cx, cy=cy):
                g1.wait_relayed(r)
                load_block(_block_of(f_hbm, first, 2 * cx + cy))

        @pl.when((j == 2) & (i == 0))
        def _():
            g1.relay(relations=(2,))
            g2.begin()

        pl.when((j == last) & (i == ni // 2))(g2.relay)

        rows = pl.ds(pl.multiple_of(i * tm, tm), tm)

        @pl.when(j == 0)
        def _():
            h = _rms_rows(x_ref, nw_ref)
            hs[rows, :] = h.astype(BF16)
            ht_ref[...] = h.T.astype(BF16)

        o_ref[...] = jnp.dot(hs[rows, :], w_blk[...], preferred_element_type=F32)

        @pl.when((j == 0) & (i == ni - 1))
        def _():
            g1.relay(relations=(0, 1))
            g1.begin(relations=(2,), sibling=False)

        @pl.when((j == last) & (i == ni - 1))
        def _():
            g1.end_rest()
            g2.end()

    def block_of_step(j, b_ref):
        return jnp.bitwise_xor(b_ref[0], jnp.bitwise_or(jnp.left_shift(jnp.bitwise_and(j, 1), 1), jnp.right_shift(j, 1)))

    outs = pl.pallas_call(
        body, name=name,
        grid_spec=pltpu.PrefetchScalarGridSpec(
            num_scalar_prefetch=1, grid=(N_CHIPS, ni),
            in_specs=[pl.BlockSpec((tm, D_MODEL), lambda j, i, b: (jnp.where(j == 0, i, 0), 0)),
                      pl.BlockSpec((1, D_MODEL), lambda j, i, b: (0, 0)), ANY] + [ANY] * nl,
            out_specs=[pl.BlockSpec((tm, cw), lambda j, i, b: (i, block_of_step(j, b))),
                       pl.BlockSpec((D_MODEL, tm), lambda j, i, b: (0, jnp.where(j == 0, i, ni - 1))),
                       ANY] + [ANY] * nl,
            scratch_shapes=[pltpu.VMEM((S, D_MODEL), BF16), pltpu.VMEM((K, cw), BF16), pltpu.SemaphoreType.DMA(())] +
            _gather_sems((first,)) + _gather_sems(late_names)),
        out_shape=[jax.ShapeDtypeStruct((S, cw * N_CHIPS), F32), jax.ShapeDtypeStruct((D_MODEL, S), BF16),
                   _full_shape(first, shard)] + [_full_shape(n, s) for n, s in zip(late_names, late_shards)],
        compiler_params=pltpu.CompilerParams(dimension_semantics=("arbitrary", "arbitrary"),
                                             vmem_limit_bytes=VMEM_LIMIT),
    )(bidx, x, nw, shard, *late_shards)
    return outs[0], outs[1], outs[2], list(outs[3:])


def _piece_blocks(pieces, tk, axis):
    starts, counts, s = [], [], 0
    for p in pieces:
        n = p.shape[axis] // tk
        starts.append(s)
        counts.append(n)
        s += n
    return starts, counts, s


def _mm_nt_rms(pieces, w, x, nw, dres, tm, name, scatter_names=(), scatter_halves=()):
    S = x.shape[0]
    npc = len(pieces)
    ni = S // tm
    ns = len(scatter_names)
    offs = np.cumsum([0] + [p.shape[1] for p in pieces]).tolist()

    def body(*refs):
        p_refs = refs[:npc]
        w_hbm, x_ref, nw_ref, dr_ref = refs[npc:npc + 4]
        h_refs = refs[npc + 4:npc + 4 + ns]
        dx_ref, dnw_ref = refs[npc + 4 + ns:npc + 6 + ns]
        r_refs = refs[npc + 6 + ns:npc + 6 + 2 * ns]
        w_v, sem, nacc = refs[npc + 6 + 2 * ns:npc + 9 + 2 * ns]
        i = pl.program_id(0)
        if ns:
            send, recv = refs[npc + 9 + 2 * ns:]

            @pl.when(i == 0)
            def _():
                for cp in _scatter_copies(scatter_names, h_refs, r_refs, send, recv):
                    cp.start()
        _load_once(w_hbm, w_v, sem)

        @pl.when(i == 0)
        def _():
            nacc[...] = jnp.zeros_like(nacc)

        dh = None
        for p in range(npc):
            part = lax.dot_general(p_refs[p][...].astype(BF16), w_v[:, offs[p]:offs[p + 1]], NT_DIMS,
                                   preferred_element_type=F32)
            dh = part if dh is None else dh + part
        xv = x_ref[...]
        rs = lax.rsqrt(jnp.mean(xv * xv, axis=-1, keepdims=True) + EPS)
        xh = xv * rs
        nacc[...] += _cs8(dh * xh)
        dxh = dh * nw_ref[...]
        dx_ref[...] = dr_ref[...] + rs * (dxh - xh * jnp.mean(dxh * xh, axis=-1, keepdims=True))

        @pl.when(i == ni - 1)
        def _():
            dnw_ref[...] = jnp.sum(nacc[...], axis=0, keepdims=True)
            if ns:
                for cp in _scatter_copies(scatter_names, h_refs, r_refs, send, recv):
                    cp.wait()

    row = pl.BlockSpec((tm, D_MODEL), lambda i: (i, 0))
    outs = _call(
        body, name, (ni,),
        [pl.BlockSpec((tm, p.shape[1]), lambda i: (i, 0)) for p in pieces] +
        [ANY, row, pl.BlockSpec((1, D_MODEL), lambda i: (0, 0)), row] + [ANY] * ns,
        [row, pl.BlockSpec((1, D_MODEL), lambda i: (0, 0))] + [ANY] * ns,
        [jax.ShapeDtypeStruct((S, D_MODEL), F32), jax.ShapeDtypeStruct((1, D_MODEL), F32)] +
        [jax.ShapeDtypeStruct((3,) + _shard_shape(h.shape, n), h.dtype) for n, h in zip(scatter_names, scatter_halves)],
        scratch=[pltpu.VMEM(w.shape, BF16), pltpu.SemaphoreType.DMA(()), pltpu.VMEM((SUBLANES, D_MODEL), F32)] +
        (_scatter_sems(scatter_names) if ns else []),
        sem=("arbitrary",))(*pieces, w, x, nw, dres, *scatter_halves)
    return outs[0], outs[1], list(outs[2:])


def _mm_wgrad(at, pieces, tn, name):
    M, S = at.shape
    starts, counts, nj = _piece_blocks(pieces, tn, 1)
    npc = len(pieces)

    def body(*refs):
        a_hbm = refs[0]
        p_refs = refs[1:1 + npc]
        o_ref, o16_ref, a_v, sem = refs[1 + npc:]
        j = pl.program_id(0)
        _load_once(a_hbm, a_v, sem)
        for p in range(npc):
            @pl.when((j >= starts[p]) & (j < starts[p] + counts[p]))
            def _(p=p):
                acc = jnp.dot(a_v[...], p_refs[p][...].astype(BF16), preferred_element_type=F32)
                o_ref[...] = acc
                o16_ref[...] = acc.astype(BF16)

    def pspec(p):
        return pl.BlockSpec((S, tn), lambda j: (0, jnp.clip(j - starts[p], 0, counts[p] - 1)))

    col = pl.BlockSpec((M, tn), lambda j: (0, j))
    return _call(
        body, name, (nj,),
        [ANY] + [pspec(p) for p in range(npc)], [col, col],
        [jax.ShapeDtypeStruct((M, nj * tn), F32), jax.ShapeDtypeStruct((M, nj * tn), BF16)],
        scratch=[pltpu.VMEM(at.shape, BF16), pltpu.SemaphoreType.DMA(())], sem=("arbitrary",))(at, *pieces)


def _stream_spec(d, T):
    return pl.BlockSpec((d, T // d, A_WIDTH), lambda i: (0, i, 0))


def _stream_shape(d, S, dtype):
    return jax.ShapeDtypeStruct((d, S // d, A_WIDTH), dtype)


N_CHUNK = A_WIDTH // LANES


def _to_tokens(ref, scr, d, T):
    if d == 1:
        return ref[0].astype(F32)
    for r in range(d):
        for ch in range(N_CHUNK):
            scr.at[ch][pl.ds(r, T // d, stride=d), :] = ref[r, :, ch * LANES:(ch + 1) * LANES].astype(F32)
    return _get(scr)


def _from_tokens(out_ref, scr, d, T):
    for r in range(d):
        for ch in range(N_CHUNK):
            out_ref[r, :, ch * LANES:(ch + 1) * LANES] = scr.at[ch][pl.ds(r, T // d, stride=d), :].astype(out_ref.dtype)


def _put(scr, val):
    for ch in range(N_CHUNK):
        scr[ch] = val[:, ch * LANES:(ch + 1) * LANES]


def _get(scr):
    return jnp.concatenate([scr[ch] for ch in range(N_CHUNK)], axis=1)


def _chunked(T):
    return pltpu.VMEM((N_CHUNK, T, LANES), F32)


def _compact_spec(d, T):
    return pl.BlockSpec((d, T // d, LANES), lambda i: (0, i, 0))


def _compact_shape(d, S):
    return jax.ShapeDtypeStruct((d, S // d, LANES), F32)


def _compact_to_tokens(ref, scr, d, T):
    if d == 1:
        return ref[0]
    for r in range(d):
        scr[pl.ds(r, T // d, stride=d), :] = ref[r]
    return scr[...]


def _compact_from_tokens(out_ref, scr, val, d, T):
    if d == 1:
        out_ref[0] = val
        return
    scr[...] = val
    for r in range(d):
        out_ref[r] = scr[pl.ds(r, T // d, stride=d), :]


def _head_expander():
    r = lax.broadcasted_iota(jnp.int32, (LANES, A_WIDTH), 0)
    c = lax.broadcasted_iota(jnp.int32, (LANES, A_WIDTH), 1) // HEAD_DIM
    return (r == c).astype(BF16)


def _head_reducer():
    r = lax.broadcasted_iota(jnp.int32, (A_WIDTH, LANES), 0) // HEAD_DIM
    c = lax.broadcasted_iota(jnp.int32, (A_WIDTH, LANES), 1)
    return (r == c).astype(BF16)


def _qkv_prep(proj, pos, freq, wq, wk, T):
    S = proj.shape[0]
    qk_w = 3 * A_WIDTH

    def body(q_ref, k_ref, v_ref, pos_ref, f_ref, wq_ref, wk_ref, *rest):
        outs, tabs, scr = rest[:9], rest[9:12], rest[12]
        seg_mean = _seg_mean()
        c, s1, s2 = _rope_tables(pos_ref, f_ref)
        for tab, val in zip(tabs, (c, s1, s2)):
            tab[...] = val
        for t, (src, w_ref) in enumerate(((q_ref, wq_ref), (k_ref, wk_ref), (v_ref, None))):
            for g in range(3):
                d = A_GROUPS[g][1]
                out = outs[3 * t + g]
                for ch in range(A_WIDTH // LANES):
                    cs = slice(ch * LANES, (ch + 1) * LANES)
                    v = src[:, g * A_WIDTH + ch * LANES: g * A_WIDTH + (ch + 1) * LANES]
                    if w_ref is not None:
                        y = v * _head_rstd(v, seg_mean) * w_ref[...]
                        v = y * c + pltpu.roll(y, 8, 1) * s1 + pltpu.roll(y, LANES - 8, 1) * s2
                    if d == 1:
                        out[0, :, cs] = v.astype(BF16)
                    else:
                        scr[ch] = v
                if d > 1:
                    _from_tokens(out, scr, d, T)

    ds_ = [A_GROUPS[g][1] for g in range(3)] * 3
    return _call(
        body, "qkv_prep", (S // T,),
        [pl.BlockSpec((T, qk_w), lambda i: (i, 0)), pl.BlockSpec((T, qk_w), lambda i: (i, 1)),
         pl.BlockSpec((T, qk_w), lambda i: (i, 2)),
         pl.BlockSpec((T, 1), lambda i: (i, 0)), pl.BlockSpec((1, LANES), lambda i: (0, 0)),
         pl.BlockSpec((1, LANES), lambda i: (0, 0)), pl.BlockSpec((1, LANES), lambda i: (0, 0))],
        [_stream_spec(d, T) for d in ds_] + [pl.BlockSpec((T, LANES), lambda i: (i, 0))] * 3,
        [_stream_shape(d, S, BF16) for d in ds_] + [jax.ShapeDtypeStruct((S, LANES), F32)] * 3,
        scratch=[_chunked(T)], sem=("parallel",))(proj, proj, proj, pos, freq, wq, wk)


def _attn_mask(i):
    qi = lax.broadcasted_iota(jnp.int32, (QBLK, 2 * QBLK), 0) + QBLK
    kj = lax.broadcasted_iota(jnp.int32, (QBLK, 2 * QBLK), 1)
    dist = qi - kj
    return (dist >= 0) & (dist <= QBLK) & ((i > 0) | (kj >= QBLK))


ATT_BLK = (None, QBLK, A_WIDTH)
ATT_CBLK = (None, QBLK, LANES)
FWD_BLOCKS = 4


def _first_head_lanes():
    return lax.broadcasted_iota(jnp.int32, (1, LANES), 1) < HEAD_DIM


def _split_heads(v, first):
    zero = jnp.zeros_like(v)
    return jnp.where(first, v, zero), jnp.where(first, zero, v)


def _attn_fwd(q, k, v, g):
    d, n, _ = q.shape
    nb = n // QBLK
    assert nb % FWD_BLOCKS == 0, (n, QBLK)
    rows2 = FWD_BLOCKS * QBLK

    def body(q_ref, kp_ref, kc_ref, vp_ref, vc_ref, o_ref, l_ref, s_scr, p_scr):
        i = pl.program_id(1)
        masks = (_attn_mask(i),) + (_attn_mask(1),) * (FWD_BLOCKS - 1)
        first = _first_head_lanes()

        def keys(prev_ref, cur_ref, b, ps):
            if b == 0:
                return jnp.concatenate([prev_ref[:, ps], cur_ref[0:QBLK, ps]], axis=0)
            return cur_ref[(b - 1) * QBLK:(b + 1) * QBLK, ps]

        for b in range(FWD_BLOCKS):
            rows = slice(b * QBLK, (b + 1) * QBLK)
            for pr in range(A_HEADS // 2):
                ps = slice(pr * LANES, (pr + 1) * LANES)
                kc = keys(kp_ref, kc_ref, b, ps)
                for e, qh in enumerate(_split_heads(q_ref[rows, ps], first)):
                    s_scr[b * A_HEADS + 2 * pr + e] = lax.dot_general(qh, kc, NT_DIMS, preferred_element_type=F32)
        lane = lax.broadcasted_iota(jnp.int32, (1, LANES), 1)
        for b in range(FWD_BLOCKS):
            lrow = jnp.zeros((QBLK, LANES), F32)
            for h in range(A_HEADS):
                s = jnp.where(masks[b], s_scr[b * A_HEADS + h] * (HEAD_DIM ** -0.5), NEG)
                m = jnp.max(s, axis=-1, keepdims=True)
                p = jnp.exp(s - m)
                den = jnp.sum(p, axis=-1, keepdims=True)
                p_scr[b * A_HEADS + h] = (p / den).astype(BF16)
                lrow = jnp.where(lane == h, m + jnp.log(den), lrow)
            l_ref[b * QBLK:(b + 1) * QBLK, :] = lrow
        for b in range(FWD_BLOCKS):
            for pr in range(A_HEADS // 2):
                ps = slice(pr * LANES, (pr + 1) * LANES)
                va, vb = _split_heads(keys(vp_ref, vc_ref, b, ps), first)
                o_ref[b * QBLK:(b + 1) * QBLK, ps] = (
                    jnp.dot(p_scr[b * A_HEADS + 2 * pr], va, preferred_element_type=F32) +
                    jnp.dot(p_scr[b * A_HEADS + 2 * pr + 1], vb, preferred_element_type=F32)).astype(BF16)

    prev = lambda r, i: (r, jnp.maximum(FWD_BLOCKS * i - 1, 0), 0)
    cur = lambda r, i: (r, i, 0)
    wide = (None, rows2, A_WIDTH)
    units = FWD_BLOCKS * A_HEADS
    return _call(
        body, "attn_fwd_g%d" % g, (d, nb // FWD_BLOCKS),
        [pl.BlockSpec(wide, cur), pl.BlockSpec(ATT_BLK, prev), pl.BlockSpec(wide, cur),
         pl.BlockSpec(ATT_BLK, prev), pl.BlockSpec(wide, cur)],
        [pl.BlockSpec(wide, cur), pl.BlockSpec((None, rows2, LANES), cur)],
        [jax.ShapeDtypeStruct((d, n, A_WIDTH), BF16), jax.ShapeDtypeStruct((d, n, LANES), F32)],
        scratch=[pltpu.VMEM((units, QBLK, 2 * QBLK), F32), pltpu.VMEM((units, QBLK, 2 * QBLK), BF16)],
        sem=("parallel", "parallel"))(q, k, k, v, v)


def _attn_bwd(q, k, v, do, lse, cg, g):
    d, n, _ = q.shape
    nb = n // QBLK
    scale = HEAD_DIM ** -0.5

    def body(q_ref, kp_ref, kc_ref, vp_ref, vc_ref, do_ref, l_ref, c_ref, dq_ref, dk_ref, dv_ref, ck, cv,
             s_scr, dp_scr, p_scr, ds_scr):
        i = pl.program_id(1)

        @pl.when(i == 0)
        def _():
            ck[...] = jnp.zeros_like(ck)
            cv[...] = jnp.zeros_like(cv)

        @pl.when(i < nb)
        def _():
            mask = _attn_mask(i)
            first = _first_head_lanes()
            for pr in range(A_HEADS // 2):
                ps = slice(pr * LANES, (pr + 1) * LANES)
                kc = jnp.concatenate([kp_ref[:, ps], kc_ref[:, ps]], axis=0)
                vc = jnp.concatenate([vp_ref[:, ps], vc_ref[:, ps]], axis=0)
                qs = _split_heads(q_ref[:, ps], first)
                dos = _split_heads(do_ref[:, ps], first)
                for e in range(2):
                    s_scr[2 * pr + e] = lax.dot_general(qs[e], kc, NT_DIMS, preferred_element_type=F32)
                    dp_scr[2 * pr + e] = lax.dot_general(dos[e], vc, NT_DIMS, preferred_element_type=F32)
            for h in range(A_HEADS):
                p = jnp.where(mask, jnp.exp(s_scr[h] * scale - l_ref[:, h:h + 1]), 0.0)
                p_scr[h] = p.astype(BF16)
                ds_scr[h] = (p * (dp_scr[h] + c_ref[:, h:h + 1]) * scale).astype(BF16)
            for pr in range(A_HEADS // 2):
                ps = slice(pr * LANES, (pr + 1) * LANES)
                ks = _split_heads(jnp.concatenate([kp_ref[:, ps], kc_ref[:, ps]], axis=0), first)
                qs = _split_heads(q_ref[:, ps], first)
                dos = _split_heads(do_ref[:, ps], first)
                dq = dkc = dvc = None
                for e in range(2):
                    ds = ds_scr[2 * pr + e]
                    a = jnp.dot(ds, ks[e], preferred_element_type=F32)
                    b = lax.dot_general(ds, qs[e], TN_DIMS, preferred_element_type=F32)
                    c = lax.dot_general(p_scr[2 * pr + e], dos[e], TN_DIMS, preferred_element_type=F32)
                    dq, dkc, dvc = (a, b, c) if e == 0 else (dq + a, dkc + b, dvc + c)
                dq_ref[:, ps] = dq.astype(BF16)
                dk_ref[:, ps] = (ck[:, ps] + dkc[:QBLK]).astype(BF16)
                dv_ref[:, ps] = (cv[:, ps] + dvc[:QBLK]).astype(BF16)
                ck[:, ps] = dkc[QBLK:]
                cv[:, ps] = dvc[QBLK:]

        @pl.when(i == nb)
        def _():
            dk_ref[...] = ck[...].astype(BF16)
            dv_ref[...] = cv[...].astype(BF16)

    qi = lambda i: jnp.minimum(i, nb - 1)
    cur = lambda r, i: (r, qi(i), 0)
    prev = lambda r, i: (r, jnp.maximum(qi(i) - 1, 0), 0)
    late = lambda r, i: (r, jnp.maximum(i - 1, 0), 0)
    return _call(
        body, "attn_bwd_g%d" % g, (d, nb + 1),
        [pl.BlockSpec(ATT_BLK, cur), pl.BlockSpec(ATT_BLK, prev), pl.BlockSpec(ATT_BLK, cur),
         pl.BlockSpec(ATT_BLK, prev), pl.BlockSpec(ATT_BLK, cur),
         pl.BlockSpec(ATT_BLK, cur), pl.BlockSpec(ATT_CBLK, cur), pl.BlockSpec(ATT_CBLK, cur)],
        [pl.BlockSpec(ATT_BLK, cur), pl.BlockSpec(ATT_BLK, late), pl.BlockSpec(ATT_BLK, late)],
        [jax.ShapeDtypeStruct((d, n, A_WIDTH), BF16)] * 3,
        scratch=[pltpu.VMEM((QBLK, A_WIDTH), F32), pltpu.VMEM((QBLK, A_WIDTH), F32),
                 pltpu.VMEM((A_HEADS, QBLK, 2 * QBLK), F32), pltpu.VMEM((A_HEADS, QBLK, 2 * QBLK), F32),
                 pltpu.VMEM((A_HEADS, QBLK, 2 * QBLK), BF16), pltpu.VMEM((A_HEADS, QBLK, 2 * QBLK), BF16)],
        sem=("parallel", "arbitrary"))(q, k, k, v, v, do, lse, cg)


def _merge_weights(l0, l1, l2):
    mx = jnp.maximum(jnp.maximum(l0, l1), l2)
    e0, e1, e2 = jnp.exp(l0 - mx), jnp.exp(l1 - mx), jnp.exp(l2 - mx)
    den = e0 + e1 + e2
    return e0 / den, e1 / den, e2 / den


def _even_specs(T, S):
    t8 = T // SUBLANES
    last8 = S // SUBLANES - 1
    col = lambda c: pl.BlockSpec((T, A_WIDTH), lambda i: (i, c))
    prev8 = lambda c: pl.BlockSpec((SUBLANES, A_WIDTH), lambda i: (jnp.maximum(i * t8 - 1, 0), c))
    next8 = lambda c: pl.BlockSpec((SUBLANES, A_WIDTH), lambda i: (jnp.minimum((i + 1) * t8, last8), c))
    return col, prev8, next8


GROUP_D = tuple(d for _, d in A_GROUPS)


def _even_mixer_fwd(x, proj, os_, ls_, conv_w, w_out, T):
    S = proj.shape[0]
    col, prev8, _ = _even_specs(T, S)
    H = SUBLANES

    def body(x_ref, w_ref, bg_r, cg_r, hb_r, zl_r, zh_r, cgp_r, hbp_r, o0, o1, o2, l0, l1, l2, cw_r,
             x1_ref, ut_ref, ext, cscr, *scr):
        i = pl.program_id(0)
        ls = [_compact_to_tokens(r, cscr, GROUP_D[g], T) for g, r in enumerate((l0, l1, l2))]
        expand = _head_expander()
        ws = [_segsum(w, expand) for w in _merge_weights(*ls)]
        oa = ws[0] * _to_tokens(o0, scr[0], GROUP_D[0], T)
        oa = oa + ws[1] * _to_tokens(o1, scr[1], GROUP_D[1], T)
        oa = oa + ws[2] * _to_tokens(o2, scr[2], GROUP_D[2], T)
        ext[0:H, :] = jnp.where(i == 0, 0.0, cgp_r[...] * hbp_r[...])
        ext[H:H + T, :] = cg_r[...] * hb_r[...]
        conv = cw_r[0:1, :] * ext[H - 2:H - 2 + T, :]
        for kk in range(1, SC_WIDTH):
            conv = conv + cw_r[kk:kk + 1, :] * ext[H - 2 + kk:H - 2 + kk + T, :]
        zl, zh = zl_r[...], zh_r[...]
        x1_ref[...] = x_ref[...] + _out_projection(ut_ref, w_ref, oa * (zl * _sig(zl)),
                                                   bg_r[...] * conv * (zh * _sig(zh)))

    streams = [_stream_spec(d, T) for d in GROUP_D]
    compacts = [_compact_spec(d, T) for d in GROUP_D]
    row = pl.BlockSpec((T, D_MODEL), lambda i: (i, 0))
    return _call(
        body, "even_mixer_fwd", (S // T,),
        [row, pl.BlockSpec((D_MODEL, D_MODEL), lambda i: (0, 0)),
         col(9), col(10), col(11), col(12), col(13), prev8(10), prev8(11)] + streams + compacts +
        [pl.BlockSpec((SC_WIDTH, A_WIDTH), lambda i: (0, 0))],
        [row, pl.BlockSpec((D_MODEL, T), lambda i: (0, i))],
        [jax.ShapeDtypeStruct((S, D_MODEL), F32), jax.ShapeDtypeStruct((D_MODEL, S), BF16)],
        scratch=[pltpu.VMEM((T + H, A_WIDTH), F32), pltpu.VMEM((T, LANES), F32)] + [_chunked(T)] * 3,
        sem=("parallel",))(
            x, w_out, proj, proj, proj, proj, proj, proj, proj, *os_, *ls_, conv_w)


def _even_mixer_bwd(dy, w_out, proj, os_, ls_, conv_w, T):
    S = proj.shape[0]
    nt = S // T
    col, prev8, next8 = _even_specs(T, S)
    H = SUBLANES
    t8 = T // SUBLANES
    last8 = S // SUBLANES - 1

    def body(dy_r, dyn_r, w_ref, bg_r, cg_r, hb_r, zl_r, zh_r, cgp_r, hbp_r, zhn_r, bgn_r,
             o0, o1, o2, l0, l1, l2, cw_r,
             do0, do1, do2, c0, c1, c2, dr_ref, dcw_ref, ext_t, ext_d, acc, cscr, s_a, s_b, s_c):
        i = pl.program_id(0)

        @pl.when(i == 0)
        def _():
            acc[...] = jnp.zeros_like(acc)

        zl, zh = zl_r[...], zh_r[...]
        sl, sh = _sig(zl), _sig(zh)
        du = lax.dot_general(dy_r[...].astype(BF16), w_ref[...], NT_DIMS, preferred_element_type=F32)
        dul, duh = du[:, 0:A_WIDTH], du[:, A_WIDTH:]
        dun = lax.dot_general(dyn_r[...].astype(BF16), w_ref[A_WIDTH:, :], NT_DIMS, preferred_element_type=F32)
        scr = (s_a, s_b, s_c)
        ls = [_compact_to_tokens(r, cscr, GROUP_D[g], T) for g, r in enumerate((l0, l1, l2))]
        wcs = _merge_weights(*ls)
        expand = _head_expander()
        ws = [_segsum(w, expand) for w in wcs]
        oa = ws[0] * _to_tokens(o0, scr[0], GROUP_D[0], T)
        oa = oa + ws[1] * _to_tokens(o1, scr[1], GROUP_D[1], T)
        oa = oa + ws[2] * _to_tokens(o2, scr[2], GROUP_D[2], T)
        doa = dul * (zl * sl)
        rsum = _segsum(doa * oa, _head_reducer())
        for g, (do_ref, c_ref) in enumerate(((do0, c0), (do1, c1), (do2, c2))):
            d = GROUP_D[g]
            _compact_from_tokens(c_ref, cscr, -wcs[g] * rsum, d, T)
            if d == 1:
                do_ref[0] = (ws[g] * doa).astype(BF16)
            else:
                _put(s_c, ws[g] * doa)
                _from_tokens(do_ref, s_c, d, T)
        cgv, hbv, bgv = cg_r[...], hb_r[...], bg_r[...]
        ext_t[0:H, :] = jnp.where(i == 0, 0.0, cgp_r[...] * hbp_r[...])
        ext_t[H:H + T, :] = cgv * hbv
        conv = cw_r[0:1, :] * ext_t[H - 2:H - 2 + T, :]
        for kk in range(1, SC_WIDTH):
            conv = conv + cw_r[kk:kk + 1, :] * ext_t[H - 2 + kk:H - 2 + kk + T, :]
        dyb = duh * (zh * sh)
        dconv = dyb * bgv
        zn = zhn_r[...]
        ext_d[0:T, :] = dconv
        ext_d[T:T + H, :] = jnp.where(i == nt - 1, 0.0, dun * (zn * _sig(zn)) * bgn_r[...])
        dt = cw_r[0:1, :] * ext_d[2:2 + T, :]
        for kk in range(1, SC_WIDTH):
            dt = dt + cw_r[kk:kk + 1, :] * ext_d[2 - kk:2 - kk + T, :]
        for kk in range(SC_WIDTH):
            acc[kk * SUBLANES:(kk + 1) * SUBLANES, :] += _cs8(dconv * ext_t[H - 2 + kk:H - 2 + kk + T, :])
        dr_ref[:, 0:A_WIDTH] = (dyb * conv).astype(BF16)
        dr_ref[:, A_WIDTH:2 * A_WIDTH] = (dt * hbv).astype(BF16)
        dr_ref[:, 2 * A_WIDTH:3 * A_WIDTH] = (dt * cgv).astype(BF16)
        dr_ref[:, 3 * A_WIDTH:4 * A_WIDTH] = (dul * oa * _dsilu(zl, sl)).astype(BF16)
        dr_ref[:, 4 * A_WIDTH:5 * A_WIDTH] = (duh * (bgv * conv) * _dsilu(zh, sh)).astype(BF16)

        @pl.when(i == nt - 1)
        def _():
            for kk in range(SC_WIDTH):
                dcw_ref[kk:kk + 1, :] = jnp.sum(acc[kk * SUBLANES:(kk + 1) * SUBLANES, :], axis=0, keepdims=True)

    streams = [_stream_spec(d, T) for d in GROUP_D]
    dynext = pl.BlockSpec((SUBLANES, D_MODEL), lambda i: (jnp.minimum((i + 1) * t8, last8), 0))
    compacts = [_compact_spec(d, T) for d in GROUP_D]
    outs = _call(
        body, "even_mixer_bwd", (nt,),
        [pl.BlockSpec((T, D_MODEL), lambda i: (i, 0)), dynext, pl.BlockSpec((D_MODEL, D_MODEL), lambda i: (0, 0)),
         col(9), col(10), col(11), col(12), col(13), prev8(10), prev8(11), next8(13), next8(9)] +
        streams + compacts + [pl.BlockSpec((SC_WIDTH, A_WIDTH), lambda i: (0, 0))],
        streams + compacts + [pl.BlockSpec((T, 5 * A_WIDTH), lambda i: (i, 0)),
                              pl.BlockSpec((SC_WIDTH, A_WIDTH), lambda i: (0, 0))],
        [_stream_shape(d, S, BF16) for d in GROUP_D] + [_compact_shape(d, S) for d in GROUP_D] +
        [jax.ShapeDtypeStruct((S, 5 * A_WIDTH), BF16), jax.ShapeDtypeStruct((SC_WIDTH, A_WIDTH), F32)],
        scratch=[pltpu.VMEM((T + H, A_WIDTH), F32), pltpu.VMEM((T + H, A_WIDTH), F32),
                 pltpu.VMEM((SC_WIDTH * SUBLANES, A_WIDTH), F32), pltpu.VMEM((T, LANES), F32)] +
                [_chunked(T)] * 3,
        sem=("arbitrary",))(dy, dy, w_out, proj, proj, proj, proj, proj, proj, proj, proj, proj, *os_, *ls_, conv_w)
    return outs[0:3], outs[3:6], outs[6], outs[7]


def _qk_bwd(proj, dqs, dks, dvs, rope, wq, wk, T):
    S = proj.shape[0]
    nt = S // T
    qk_w = 3 * A_WIDTH

    def body(q_ref, k_ref, dq0, dq1, dq2, dk0, dk1, dk2, dv0, dv1, dv2, c_ref, s1_ref, s2_ref, wq_ref, wk_ref,
             o_ref, dw_ref, acc, scr):
        i = pl.program_id(0)

        @pl.when(i == 0)
        def _():
            acc[...] = jnp.zeros_like(acc)
            dw_ref[...] = jnp.zeros_like(dw_ref)

        seg_mean = _seg_mean()
        c, s1, s2 = c_ref[...], s1_ref[...], s2_ref[...]
        for t, (src, w_ref, ds) in enumerate(((q_ref, wq_ref, (dq0, dq1, dq2)), (k_ref, wk_ref, (dk0, dk1, dk2)))):
            wv = w_ref[...]
            for g in range(3):
                d = GROUP_D[g]
                if d > 1:
                    _to_tokens(ds[g], scr, d, T)
                for ch in range(A_WIDTH // LANES):
                    cs = slice(g * A_WIDTH + ch * LANES, g * A_WIDTH + (ch + 1) * LANES)
                    lc = slice(ch * LANES, (ch + 1) * LANES)
                    v = src[:, cs]
                    dout = ds[g][0, :, lc].astype(F32) if d == 1 else scr[ch]
                    rs = lax.rsqrt(_segsum(v * v, seg_mean) + EPS)
                    xh = v * rs
                    dy = dout * c + pltpu.roll(dout * s1, LANES - 8, 1) + pltpu.roll(dout * s2, 8, 1)
                    acc[t * SUBLANES:(t + 1) * SUBLANES, :] += _cs8(dy * xh)
                    dxh = dy * wv
                    mean = _segsum(dxh * xh, seg_mean)
                    o_ref[:, t * qk_w + g * A_WIDTH + ch * LANES: t * qk_w + g * A_WIDTH + (ch + 1) * LANES] = (
                        rs * (dxh - xh * mean)).astype(BF16)
        for g, dv in enumerate((dv0, dv1, dv2)):
            d = GROUP_D[g]
            base = 2 * qk_w + g * A_WIDTH
            o_ref[:, base:base + A_WIDTH] = _to_tokens(dv, scr, d, T).astype(BF16)

        @pl.when(i == nt - 1)
        def _():
            for t in range(2):
                srow = jnp.sum(acc[t * SUBLANES:(t + 1) * SUBLANES, :], axis=0, keepdims=True)
                dw_ref[t:t + 1, :] = srow + pltpu.roll(srow, HEAD_DIM, 1)

    streams = [_stream_spec(d, T) for d in GROUP_D]
    return _call(
        body, "qk_bwd", (nt,),
        [pl.BlockSpec((T, qk_w), lambda i: (i, 0)), pl.BlockSpec((T, qk_w), lambda i: (i, 1))] + streams * 3 +
        [pl.BlockSpec((T, LANES), lambda i: (i, 0))] * 3 +
        [pl.BlockSpec((1, LANES), lambda i: (0, 0)), pl.BlockSpec((1, LANES), lambda i: (0, 0))],
        [pl.BlockSpec((T, 3 * qk_w), lambda i: (i, 0)), pl.BlockSpec((SUBLANES, LANES), lambda i: (0, 0))],
        [jax.ShapeDtypeStruct((S, 3 * qk_w), BF16), jax.ShapeDtypeStruct((SUBLANES, LANES), F32)],
        scratch=[pltpu.VMEM((2 * SUBLANES, LANES), F32), _chunked(T)], sem=("arbitrary",))(
            proj, proj, *dqs, *dks, *dvs, *rope, wq, wk)


N_SMALL_ODD = 40
SHIFT_ROWS_LESS = SUBLANES


def _fill_shifted(ext_ref, sh_ref):
    rows = ext_ref.shape[0] - SHIFT_ROWS_LESS
    for b in range(1, SUBLANES):
        sh_ref[b - 1] = ext_ref[b:b + rows, :]


def _window(ext_ref, sh_ref, off, T):
    a, b = divmod(off, SUBLANES)
    if b == 0:
        return ext_ref[off:off + T, :]
    return sh_ref[b - 1, a * SUBLANES:a * SUBLANES + T, :]


def _odd_pool_tile(i, uc_r, ucp_r, pw_r, ext_u, pooled_s, pm_s, T):
    H = HALO
    uc = uc_r[...]
    ext_u[0:H, :] = jnp.where(i == 0, 0.0, ucp_r[...])
    ext_u[H:H + T, :] = uc
    row = i * T + lax.broadcasted_iota(jnp.int32, (T, 1), 0)
    for g, p in enumerate(POOL_SIZES):
        cs = slice(g * LANES, (g + 1) * LANES)
        win = ext_u[H:H + T, cs]
        for j in range(1, p):
            win = win + ext_u[H - j:H - j + T, cs]
        cnt = jnp.minimum(row + 1, p).astype(F32)
        pooled = win / cnt - uc[:, cs]
        pooled_s[:, cs] = pooled
        pm_s[:, cs] = jnp.dot(pooled.astype(BF16), pw_r[g].astype(BF16), preferred_element_type=F32)
    return row


def _odd_glu_tile(i, da_r, dg_r, dap_r, dgp_r, ext_g, sh_g, T):
    H = HALO
    ext_g[0:H, :] = jnp.where(i == 0, 0.0, dap_r[...] * _sig(dgp_r[...]))
    ext_g[H:H + T, :] = da_r[...] * _sig(dg_r[...])
    _fill_shifted(ext_g, sh_g)


def _odd_specs(T, S, order):
    tb = T // HALO
    col = lambda c: pl.BlockSpec((T, A_WIDTH), lambda s: (order(s), c))
    prev = lambda c: pl.BlockSpec((HALO, A_WIDTH), lambda s: (jnp.maximum(order(s) * tb - 1, 0), c))
    const2 = lambda shape: pl.BlockSpec(shape, lambda s: (0, 0))
    weights = [pl.BlockSpec((4, LANES, LANES), lambda s: (0, 0, 0)), const2((1, A_WIDTH)),
               const2((D_CONV, A_WIDTH)), const2((1, A_WIDTH)), const2((1, A_WIDTH)), const2((1, A_WIDTH))]
    return col, prev, weights


def _odd_mixer_fwd(x, tgt, proj, pool_w, scale, dconv_w, dconv_b, ln_w, ln_b, w_out, T):
    S = proj.shape[0]
    nt = S // T
    col, prev, wspecs = _odd_specs(T, S, lambda s: s)
    H = HALO

    def body(x_ref, t_ref, w_ref, uc_r, da_r, dg_r, zl_r, zh_r, ucp_r, dap_r, dgp_r, pw_r, sc_r, dw_r, db_r,
             lw_r, lb_r, dy_ref, l_ref, ut_ref, cv_ref, ext_u, ext_g, sh_g, pooled_s, pm_s, lacc):
        i = pl.program_id(0)

        @pl.when(i == 0)
        def _():
            lacc[...] = jnp.zeros_like(lacc)

        _odd_pool_tile(i, uc_r, ucp_r, pw_r, ext_u, pooled_s, pm_s, T)
        _odd_glu_tile(i, da_r, dg_r, dap_r, dgp_r, ext_g, sh_g, T)
        base = H - (D_CONV - 1)
        conv = db_r[...] + dw_r[0:1, :] * _window(ext_g, sh_g, base, T)
        for kk in range(1, D_CONV):
            conv = conv + dw_r[kk:kk + 1, :] * _window(ext_g, sh_g, base + kk, T)
        cv_ref[...] = conv
        mu = jnp.mean(conv, axis=-1, keepdims=True)
        xc = conv - mu
        yh = xc * lax.rsqrt(jnp.mean(xc * xc, axis=-1, keepdims=True) + EPS)
        ln = yh * lw_r[...] + lb_r[...]
        zl, zh = zl_r[...], zh_r[...]
        y = x_ref[...] + _out_projection(ut_ref, w_ref, pm_s[...] * sc_r[...] * (zl * _sig(zl)),
                                         ln * _sig(ln) * (zh * _sig(zh)))
        diff = y - t_ref[...]
        dy_ref[...] = diff / float(D_MODEL)
        lacc[...] += _cs8(diff * diff)

        @pl.when(i == nt - 1)
        def _():
            l_ref[...] = jnp.sum(lacc[...], axis=0, keepdims=True)

    row = pl.BlockSpec((T, D_MODEL), lambda i: (i, 0))
    return _call(
        body, "odd_mixer_fwd", (nt,),
        [row, row, pl.BlockSpec((D_MODEL, D_MODEL), lambda i: (0, 0)),
         col(0), col(1), col(2), col(3), col(4), prev(0), prev(1), prev(2)] + wspecs,
        [row, pl.BlockSpec((1, D_MODEL), lambda i: (0, 0)), pl.BlockSpec((D_MODEL, T), lambda i: (0, i)),
         pl.BlockSpec((T, A_WIDTH), lambda i: (i, 0))],
        [jax.ShapeDtypeStruct((S, D_MODEL), F32), jax.ShapeDtypeStruct((1, D_MODEL), F32),
         jax.ShapeDtypeStruct((D_MODEL, S), BF16), jax.ShapeDtypeStruct((S, A_WIDTH), F32)],
        scratch=[pltpu.VMEM((T + H, A_WIDTH), F32), pltpu.VMEM((T + H, A_WIDTH), F32),
                 pltpu.VMEM((SUBLANES - 1, T + H - SHIFT_ROWS_LESS, A_WIDTH), F32),
                 pltpu.VMEM((T, A_WIDTH), F32), pltpu.VMEM((T, A_WIDTH), F32),
                 pltpu.VMEM((SUBLANES, D_MODEL), F32)],
        sem=("arbitrary",))(x, tgt, w_out, proj, proj, proj, proj, proj, proj, proj, proj,
                            pool_w, scale, dconv_w, dconv_b, ln_w, ln_b)


def _odd_mixer_bwd(dy, w_out, proj, conv, pool_w, scale, dconv_w, dconv_b, ln_w, ln_b, T):
    S = proj.shape[0]
    nt = S // T
    order = lambda s: nt - 1 - s
    col, prev, wspecs = _odd_specs(T, S, order)
    H = HALO

    def body(dy_r, w_ref, cv_r, uc_r, da_r, dg_r, zl_r, zh_r, ucp_r, dap_r, dgp_r, pw_r, sc_r, dw_r, db_r, lw_r, lb_r,
             dp_ref, dpw_ref, sm_ref, ext_u, ext_g, sh_g, pooled_s, pm_s, dpl_s, ext_p, ext_c, sh_c, acc):
        step = pl.program_id(0)
        i = nt - 1 - step

        @pl.when(step == 0)
        def _():
            ext_p[T:T + H, :] = jnp.zeros((H, A_WIDTH), F32)
            ext_c[T:T + H, :] = jnp.zeros((H, A_WIDTH), F32)
            acc[...] = jnp.zeros_like(acc)
            dpw_ref[...] = jnp.zeros_like(dpw_ref)

        def accum(r, v):
            acc[r * SUBLANES:(r + 1) * SUBLANES, :] += _cs8(v)

        row = _odd_pool_tile(i, uc_r, ucp_r, pw_r, ext_u, pooled_s, pm_s, T)
        _odd_glu_tile(i, da_r, dg_r, dap_r, dgp_r, ext_g, sh_g, T)
        conv = cv_r[...]
        mu = jnp.mean(conv, axis=-1, keepdims=True)
        xc = conv - mu
        rstd = lax.rsqrt(jnp.mean(xc * xc, axis=-1, keepdims=True) + EPS)
        yh = xc * rstd
        ln = yh * lw_r[...] + lb_r[...]
        sln = _sig(ln)
        zl, zh = zl_r[...], zh_r[...]
        sl, sh = _sig(zl), _sig(zh)
        du = lax.dot_general(dy_r[...].astype(BF16), w_ref[...], NT_DIMS, preferred_element_type=F32)
        dul, duh = du[:, 0:A_WIDTH], du[:, A_WIDTH:]
        pm = pm_s[...]
        scv = sc_r[...]
        dyc = dul * (zl * sl)
        accum(34, dyc * pm)
        dpm = dyc * scv
        for g in range(len(POOL_SIZES)):
            cs = slice(g * LANES, (g + 1) * LANES)
            dpm_g = dpm[:, cs].astype(BF16)
            dpw_ref[g] += lax.dot_general(pooled_s[:, cs].astype(BF16), dpm_g, TN_DIMS, preferred_element_type=F32)
            dpl_s[:, cs] = lax.dot_general(dpm_g, pw_r[g].astype(BF16), NT_DIMS, preferred_element_type=F32)
        lane_p = lax.broadcasted_iota(jnp.int32, (1, A_WIDTH), 1) // LANES
        pvec = jnp.left_shift(2, lane_p)
        cnt = jnp.minimum(row + 1, pvec).astype(F32)
        dpl = dpl_s[...]
        ext_p[0:T, :] = dpl / cnt
        for g, p in enumerate(POOL_SIZES):
            cs = slice(g * LANES, (g + 1) * LANES)
            win = ext_p[0:T, cs]
            for j in range(1, p):
                win = win + ext_p[j:j + T, cs]
            dp_ref[:, cs] = (win - dpl[:, cs]).astype(BF16)
        ext_p[T:T + H, :] = ext_p[0:H, :]
        dln = duh * (zh * sh) * _dsilu(ln, sln)
        accum(32, dln * yh)
        accum(33, dln)
        dyh = dln * lw_r[...]
        dc = rstd * (dyh - jnp.mean(dyh, axis=-1, keepdims=True) - yh * jnp.mean(dyh * yh, axis=-1, keepdims=True))
        accum(31, dc)
        ext_c[0:T, :] = dc
        _fill_shifted(ext_c, sh_c)
        base = H - (D_CONV - 1)
        dgl = dw_r[0:1, :] * _window(ext_c, sh_c, D_CONV - 1, T)
        accum(0, dc * _window(ext_g, sh_g, base, T))
        for kk in range(1, D_CONV):
            dgl = dgl + dw_r[kk:kk + 1, :] * _window(ext_c, sh_c, D_CONV - 1 - kk, T)
            accum(kk, dc * _window(ext_g, sh_g, base + kk, T))
        ext_c[T:T + H, :] = ext_c[0:H, :]
        dav, dgv = da_r[...], dg_r[...]
        sg = _sig(dgv)
        dp_ref[:, A_WIDTH:2 * A_WIDTH] = (dgl * sg).astype(BF16)
        dp_ref[:, 2 * A_WIDTH:3 * A_WIDTH] = (dgl * dav * sg * (1.0 - sg)).astype(BF16)
        dp_ref[:, 3 * A_WIDTH:4 * A_WIDTH] = (dul * (pm * scv) * _dsilu(zl, sl)).astype(BF16)
        dp_ref[:, 4 * A_WIDTH:5 * A_WIDTH] = (duh * (ln * sln) * _dsilu(zh, sh)).astype(BF16)

        @pl.when(step == nt - 1)
        def _():
            for r in range(N_SMALL_ODD):
                sm_ref[r:r + 1, :] = jnp.sum(acc[r * SUBLANES:(r + 1) * SUBLANES, :], axis=0, keepdims=True)

    ext = pltpu.VMEM((T + H, A_WIDTH), F32)
    shifted = pltpu.VMEM((SUBLANES - 1, T + H - SHIFT_ROWS_LESS, A_WIDTH), F32)
    tile = pltpu.VMEM((T, A_WIDTH), F32)
    return _call(
        body, "odd_mixer_bwd", (nt,),
        [pl.BlockSpec((T, D_MODEL), lambda s: (order(s), 0)), pl.BlockSpec((D_MODEL, D_MODEL), lambda s: (0, 0)),
         pl.BlockSpec((T, A_WIDTH), lambda s: (order(s), 0)),
         col(0), col(1), col(2), col(3), col(4), prev(0), prev(1), prev(2)] + wspecs,
        [pl.BlockSpec((T, ODD_IN), lambda s: (order(s), 0)),
         pl.BlockSpec((4, LANES, LANES), lambda s: (0, 0, 0)),
         pl.BlockSpec((N_SMALL_ODD, A_WIDTH), lambda s: (0, 0))],
        [jax.ShapeDtypeStruct((S, ODD_IN), BF16), jax.ShapeDtypeStruct((4, LANES, LANES), F32),
         jax.ShapeDtypeStruct((N_SMALL_ODD, A_WIDTH), F32)],
        scratch=[ext, ext, shifted, tile, tile, tile, ext, ext, shifted,
                 pltpu.VMEM((N_SMALL_ODD * SUBLANES, A_WIDTH), F32)],
        sem=("arbitrary",))(dy, w_out, conv, proj, proj, proj, proj, proj, proj, proj, proj,
                            pool_w, scale, dconv_w, dconv_b, ln_w, ln_b)


TILE_SEQ = 512
TILE_WG = 256
TILE_FIRST = 1024
TILE_MM = 512


SMALL_PACK = "small_pack"
SMALL_PACK_W = 2 * LANES
LATE_WEIGHTS = ("e_w_out", "o_w_in", "o_w_out", SMALL_PACK)
ODD_MATS = ("o_w_in", "o_w_out")
EVEN_MATS = ("e_w_in", "e_w_out")


def _reduce_start(names, grads, grads16, cidx):
    recv = _swap_to_sibling(names, [grads16[n] for n in names], "swap_halves_" + names[0][0], True)
    both = [_add_half(cidx, grads[n], r, n) for n, r in zip(names, recv)]
    return [h for h, _ in both], [hb for _, hb in both]


def _local_step(x, pos, tgt, shards, p, unpack_small, cidx, bidx):
    T = TILE_SEQ
    freq = _freq_table()
    wq = jnp.tile(p["e_q_norm_w"], (1, LANES // HEAD_DIM))
    wk = jnp.tile(p["e_k_norm_w"], (1, LANES // HEAD_DIM))

    proj_e, ht_e, w_e_in, late = _inproj_gathering(x, p["e_norm_w"], shards["e_w_in"], bidx, "e_w_in", LATE_WEIGHTS,
                                                   [shards[n] for n in LATE_WEIGHTS], TILE_FIRST, "inproj_even")
    wb = dict(zip(LATE_WEIGHTS, late), e_w_in=w_e_in)
    p = dict(p, **unpack_small(wb[SMALL_PACK]))
    qkv = _qkv_prep(proj_e, pos, freq, wq, wk, T)
    qs, ks, vs, rope = qkv[0:3], qkv[3:6], qkv[6:9], qkv[9:12]
    os_, ls_ = [], []
    for g in range(3):
        o, l = _attn_fwd(qs[g], ks[g], vs[g], g)
        os_.append(o)
        ls_.append(l)
    x1, ut_e = _even_mixer_fwd(x, proj_e, os_, ls_, p["e_conv_w"], wb["e_w_out"], T)
    proj_o, ht_o = _inproj(x1, p["o_norm_w"], wb["o_w_in"], TILE_MM, 1280, "inproj_odd")
    odd_w = (p["o_pool_w"], p["o_pool_scale"], p["o_dconv_w"], p["o_dconv_b"], p["o_ln_w"], p["o_ln_b"])
    dy, lsum, ut_o, conv_o = _odd_mixer_fwd(x1, tgt, proj_o, *odd_w, wb["o_w_out"], T)

    g, g16 = {}, {}
    g["o_w_out"], g16["o_w_out"] = _mm_wgrad(ut_o, [dy], TILE_WG, "wgrad_o_out")
    dproj_o, g["o_pool_w"], small_o = _odd_mixer_bwd(dy, wb["o_w_out"], proj_o, conv_o, *odd_w, T)
    g["o_w_in"], g16["o_w_in"] = _mm_wgrad(ht_o, [dproj_o], TILE_WG, "wgrad_o_in")
    half_o, half_o16 = _reduce_start(ODD_MATS, g, g16, cidx)
    dx1, g["o_norm_w"], blocks_o = _mm_nt_rms([dproj_o], wb["o_w_in"], x1, p["o_norm_w"], dy, TILE_MM, "dx_odd",
                                              ODD_MATS, half_o16)
    g["o_dconv_w"] = small_o[0:D_CONV]
    g["o_dconv_b"] = small_o[31:32]
    g["o_ln_w"] = small_o[32:33]
    g["o_ln_b"] = small_o[33:34]
    g["o_pool_scale"] = small_o[34:35]

    g["e_w_out"], g16["e_w_out"] = _mm_wgrad(ut_e, [dx1], TILE_WG, "wgrad_e_out")
    dos, cgs, drest, g["e_conv_w"] = _even_mixer_bwd(dx1, wb["e_w_out"], proj_e, os_, ls_, p["e_conv_w"], T)
    dqs, dks, dvs = [], [], []
    for gi in range(3):
        dq, dk, dv = _attn_bwd(qs[gi], ks[gi], vs[gi], dos[gi], ls_[gi], cgs[gi], gi)
        dqs.append(dq)
        dks.append(dk)
        dvs.append(dv)
    dqkv, dnw = _qk_bwd(proj_e, dqs, dks, dvs, rope, wq, wk, T)
    g["e_q_norm_w"] = dnw[0:1, 0:HEAD_DIM]
    g["e_k_norm_w"] = dnw[1:2, 0:HEAD_DIM]
    pieces = [dqkv, drest]
    g["e_w_in"], g16["e_w_in"] = _mm_wgrad(ht_e, pieces, TILE_WG, "wgrad_e_in")
    half_e, half_e16 = _reduce_start(EVEN_MATS, g, g16, cidx)
    dx, g["e_norm_w"], blocks_e = _mm_nt_rms(pieces, wb["e_w_in"], x, p["e_norm_w"], dx1, TILE_MM, "dx_even",
                                             EVEN_MATS, half_e16)
    parts = {}
    for names, halves, blocks in ((ODD_MATS, half_o, blocks_o), (EVEN_MATS, half_e, blocks_e)):
        for n, h, r in zip(names, halves, blocks):
            parts[n] = _add_blocks(bidx, h, r, n)
    return lsum, dx, g, parts


BIG = ("e_w_in", "e_w_out", "o_w_in", "o_w_out")
SHARD_AXIS = {"e_w_in": 1, "e_w_out": 0, "o_w_in": 1, "o_w_out": 0, SMALL_PACK: 0}
N_CHIPS = 4


def _place():
    x, y, c = lax.axis_index("x"), lax.axis_index("y"), lax.axis_index("c")
    chips = [(1 - x, y), (x, 1 - y), (1 - x, 1 - y)]
    return x, y, c, chips


def _block_of(ref, name, block):
    rows, cols = ref.shape
    if SHARD_AXIS[name] == 1:
        cw = cols // N_CHIPS
        return ref.at[:, pl.ds(pl.multiple_of(block * cw, LANES), cw)]
    rw = rows // N_CHIPS
    return ref.at[pl.ds(pl.multiple_of(block * rw, rw), rw), :]


def _half_of(ref, name, half):
    rows, cols = ref.shape
    if SHARD_AXIS[name] == 1:
        return ref.at[pl.ds(pl.multiple_of(half * (rows // 2), rows // 2), rows // 2), :]
    return ref.at[:, pl.ds(pl.multiple_of(half * (cols // 2), LANES), cols // 2)]


def _sub(ref, name, block, half):
    rows, cols = ref.shape
    if SHARD_AXIS[name] == 1:
        cw, hr = cols // N_CHIPS, rows // 2
        return ref.at[pl.ds(pl.multiple_of(half * hr, hr), hr), pl.ds(pl.multiple_of(block * cw, LANES), cw)]
    rw, hc = rows // N_CHIPS, cols // 2
    return ref.at[pl.ds(pl.multiple_of(block * rw, rw), rw), pl.ds(pl.multiple_of(half * hc, LANES), hc)]


GATHER_COPIES = 7


class _Gather:
    def __init__(self, names, s_refs, f_refs, send, recv):
        self.names, self.s, self.f, self.send, self.recv = names, s_refs, f_refs, send, recv

    def _copy(self, k, src, dst, to):
        return pltpu.make_async_remote_copy(src_ref=src, dst_ref=dst, send_sem=self.send.at[k],
                                            recv_sem=self.recv.at[k], device_id=to, device_id_type=MESH)

    def _plan(self):
        x, y, c, chips = _place()
        me, sib = 2 * x + y, (x, y, 1 - c)
        first, relay_in, relay, last_in = [], [], [], []
        for wi, n in enumerate(self.names):
            k0 = wi * GATHER_COPIES
            s, f = self.s[wi], self.f[wi]
            own = _block_of(f, n, me)
            first.append(self._copy(k0 + 3, s, own, sib))
            last_in.append(self._copy(k0 + 3, s, own, sib))
            for j, (cx, cy) in enumerate(chips):
                first.append(self._copy(k0 + j, _half_of(s, n, c), _sub(f, n, me, c), (cx, cy, c)))
                mine = _sub(f, n, 2 * cx + cy, c)
                relay_in.append(self._copy(k0 + j, mine, mine, sib))
                relay.append(self._copy(k0 + 4 + j, mine, mine, sib))
                theirs = _sub(f, n, 2 * cx + cy, 1 - c)
                last_in.append(self._copy(k0 + 4 + j, theirs, theirs, sib))
        return first, relay_in, relay, last_in

    N_RELATIONS = 3

    def begin(self, relations=(0, 1, 2), sibling=True):
        first = self._plan()[0]
        for wi in range(len(self.names)):
            mine = first[wi * (1 + self.N_RELATIONS):(wi + 1) * (1 + self.N_RELATIONS)]
            if sibling:
                mine[0].start()
            for j in relations:
                mine[1 + j].start()

    def relay(self, relations=(0, 1, 2)):
        _, relay_in, relay, _ = self._plan()
        for wi in range(len(self.names)):
            for j in relations:
                relay_in[wi * self.N_RELATIONS + j].wait_recv()
                relay[wi * self.N_RELATIONS + j].start()

    def end(self):
        first, _, relay, last_in = self._plan()
        for cp in last_in:
            cp.wait_recv()
        for cp in first + relay:
            cp.wait_send()

    def wait_relayed(self, j):
        self._plan()[3][1 + j].wait_recv()

    def end_rest(self):
        first, _, relay, last_in = self._plan()
        last_in[0].wait_recv()
        for cp in first + relay:
            cp.wait_send()


def _full_shape(n, s):
    r, cdim = s.shape
    return jax.ShapeDtypeStruct((r, cdim * N_CHIPS) if SHARD_AXIS[n] == 1 else (r * N_CHIPS, cdim), s.dtype)


def _gather_sems(names):
    k = GATHER_COPIES * len(names)
    return [pltpu.SemaphoreType.DMA((k,)), pltpu.SemaphoreType.DMA((k,))]


def _scatter_copies(names, h_refs, r_refs, send, recv):
    _, _, c, chips = _place()
    cps = []
    for wi, n in enumerate(names):
        for j, (cx, cy) in enumerate(chips):
            cps.append(pltpu.make_async_remote_copy(
                src_ref=_block_of(h_refs[wi], n, 2 * cx + cy), dst_ref=r_refs[wi].at[j],
                send_sem=send.at[wi * 3 + j], recv_sem=recv.at[wi * 3 + j],
                device_id=(cx, cy, c), device_id_type=MESH))
    return cps


def _scatter_sems(names):
    return [pltpu.SemaphoreType.DMA((3 * len(names),)), pltpu.SemaphoreType.DMA((3 * len(names),))]


class _SmallSum:
    def __init__(self, p_ref, o_ref, sbuf, cbuf, send, recv):
        self.p, self.o, self.sbuf, self.cbuf, self.send, self.recv = p_ref, o_ref, sbuf, cbuf, send, recv

    def _copy(self, k, ref, to):
        return pltpu.make_async_remote_copy(src_ref=ref, dst_ref=ref, send_sem=self.send.at[k],
                                            recv_sem=self.recv.at[k], device_id=to, device_id_type=MESH)

    def _plan(self):
        x, y, c, chips = _place()
        me, sib = 2 * x + y, (x, y, 1 - c)
        d2d_out = self._copy(0, self.sbuf.at[c], sib)
        d2d_in = self._copy(0, self.sbuf.at[1 - c], sib)
        ici_out = [self._copy(1 + j, self.cbuf.at[me], (cx, cy, c)) for j, (cx, cy) in enumerate(chips)]
        ici_in = [self._copy(1 + j, self.cbuf.at[2 * cx + cy], (cx, cy, c)) for j, (cx, cy) in enumerate(chips)]
        return c, me, d2d_out, d2d_in, ici_out, ici_in

    def begin(self):
        c, _, d2d_out, _, _, _ = self._plan()
        self.sbuf[c] = self.p[...]
        d2d_out.start()

    def middle(self):
        _, me, _, d2d_in, ici_out, _ = self._plan()
        d2d_in.wait_recv()
        self.cbuf[me] = self.sbuf[0] + self.sbuf[1]
        for cp in ici_out:
            cp.start()

    def end(self):
        _, _, d2d_out, _, ici_out, ici_in = self._plan()
        for cp in ici_in:
            cp.wait_recv()
        self.o[...] = (self.cbuf[0] + self.cbuf[1]) + (self.cbuf[2] + self.cbuf[3])
        for cp in [d2d_out] + ici_out:
            cp.wait_send()


def _small_sum_scratch(R):
    return [pltpu.VMEM((2, R, LANES), F32), pltpu.VMEM((N_CHIPS, R, LANES), F32),
            pltpu.SemaphoreType.DMA((4,)), pltpu.SemaphoreType.DMA((4,))]


def _half_shape(shape, name):
    r, cdim = shape
    return (r // 2, cdim) if SHARD_AXIS[name] == 1 else (r, cdim // 2)


def _shard_shape(shape, name):
    r, cdim = shape
    return (r, cdim // N_CHIPS) if SHARD_AXIS[name] == 1 else (r // N_CHIPS, cdim)


def _swap_to_sibling(names, srcs, name, pick_half, small=None):
    nw = len(names)
    ns = 0 if small is None else 1
    vm = pl.BlockSpec(memory_space=pltpu.VMEM)

    def body(*refs):
        g_refs = refs[:nw]
        r_refs = refs[nw + ns:2 * nw + ns]
        send, recv = refs[2 * nw + 2 * ns:2 * nw + 2 * ns + 2]
        x, y, c, _ = _place()
        sib = (x, y, 1 - c)
        cps = []
        for wi, n in enumerate(names):
            src = _half_of(g_refs[wi], n, 1 - c) if pick_half else g_refs[wi]
            cp = pltpu.make_async_remote_copy(src_ref=src, dst_ref=r_refs[wi], send_sem=send.at[wi],
                                              recv_sem=recv.at[wi], device_id=sib, device_id_type=MESH)
            cp.start()
            cps.append(cp)
        if ns:
            total = _SmallSum(refs[nw], refs[2 * nw + ns], *refs[2 * nw + 2 * ns + 2:])
            total.begin()
            total.middle()
            total.end()
        for cp in cps:
            cp.wait()

    outs = [jax.ShapeDtypeStruct(_half_shape(g.shape, n) if pick_half else g.shape, g.dtype)
            for n, g in zip(names, srcs)]
    return pl.pallas_call(
        body, name=name, in_specs=[ANY] * nw + [vm] * ns, out_specs=[ANY] * nw + [vm] * ns,
        out_shape=outs + ([jax.ShapeDtypeStruct(small.shape, F32)] if ns else []),
        scratch_shapes=[pltpu.SemaphoreType.DMA((nw,)), pltpu.SemaphoreType.DMA((nw,))] +
        (_small_sum_scratch(small.shape[0]) if ns else []),
    )(*srcs, *([small] if ns else []))


def _add_half(cidx, g, r, name):
    rows, cols = r.shape
    tr = 256
    tc = cols if cols <= 1792 else (1792 if cols % 1792 == 0 else 1280)
    nr, nc = rows // tr, cols // tc

    def body(c_ref, g_ref, r_ref, o_ref, ob_ref):
        s = g_ref[...] + r_ref[...].astype(F32)
        o_ref[...] = s
        ob_ref[...] = s.astype(BF16)

    if SHARD_AXIS[name] == 1:
        gmap = lambda i, j, c_ref: (c_ref[0] * nr + i, j)
    else:
        gmap = lambda i, j, c_ref: (i, c_ref[0] * nc + j)
    same = lambda i, j, c_ref: (i, j)
    return pl.pallas_call(
        body, name="add_half_" + name,
        grid_spec=pltpu.PrefetchScalarGridSpec(
            num_scalar_prefetch=1, grid=(nr, nc),
            in_specs=[pl.BlockSpec((tr, tc), gmap), pl.BlockSpec((tr, tc), same)],
            out_specs=[pl.BlockSpec((tr, tc), same), pl.BlockSpec((tr, tc), same)]),
        out_shape=[jax.ShapeDtypeStruct(r.shape, F32), jax.ShapeDtypeStruct(r.shape, BF16)],
        compiler_params=pltpu.CompilerParams(dimension_semantics=("parallel", "parallel"), vmem_limit_bytes=VMEM_LIMIT),
    )(cidx, g, r)


def _add_blocks(bidx, h, r, name):
    _, rows, cols = r.shape
    tr = min(rows, 256)
    nr = rows // tr

    def body(b_ref, h_ref, r0, r1, r2, o_ref):
        o_ref[...] = ((h_ref[...] + r0[0].astype(F32)) + r1[0].astype(F32)) + r2[0].astype(F32)

    if SHARD_AXIS[name] == 1:
        hmap = lambda i, b_ref: (i, b_ref[0])
    else:
        hmap = lambda i, b_ref: (b_ref[0] * nr + i, 0)
    rspec = lambda j: pl.BlockSpec((1, tr, cols), lambda i, b_ref, j=j: (j, i, 0))
    return pl.pallas_call(
        body, name="add_blocks_" + name,
        grid_spec=pltpu.PrefetchScalarGridSpec(
            num_scalar_prefetch=1, grid=(nr,),
            in_specs=[pl.BlockSpec((tr, cols), hmap), rspec(0), rspec(1), rspec(2)],
            out_specs=pl.BlockSpec((tr, cols), lambda i, b_ref: (i, 0))),
        out_shape=jax.ShapeDtypeStruct((rows, cols), F32),
        compiler_params=pltpu.CompilerParams(dimension_semantics=("parallel",), vmem_limit_bytes=VMEM_LIMIT),
    )(bidx, h, r, r, r)


def _adam_math(w, g, m, v):
    c1 = 1.0 - ADAM_B1 ** ADAM_STEP
    c2 = 1.0 - ADAM_B2 ** ADAM_STEP
    nm = ADAM_B1 * m + (1.0 - ADAM_B1) * g
    nv = ADAM_B2 * v + (1.0 - ADAM_B2) * (g * g)
    delta = -ADAM_LR * ((nm / c1) / (jnp.sqrt(nv / c2) + ADAM_EPS) + ADAM_WD * w)
    return delta, nm, nv


def _adamw(w, g, m, v, name):
    def body(w_ref, g_ref, m_ref, v_ref, d_ref, nm_ref, nv_ref):
        d_ref[...], nm_ref[...], nv_ref[...] = _adam_math(w_ref[...], g_ref[...], m_ref[...], v_ref[...])

    spec = pl.BlockSpec(w.shape, lambda i: (0, 0))
    return _call(body, "adamw_" + name, (1,), [spec] * 4, [spec] * 3,
                 [jax.ShapeDtypeStruct(w.shape, F32)] * 3, sem=("arbitrary",))(w, g, m, v)


def _adamw_halves(cidx, w, mine, theirs, m, v, name):
    hr, hc = mine.shape
    tr = 128
    ni = hr // tr
    if SHARD_AXIS[name] == 1:
        wmap = lambda hh, i, c_ref: (hh * ni + i, 0)
    else:
        wmap = lambda hh, i, c_ref: (i, hh)
    hmap = lambda hh, i, c_ref: (i, 0)

    def body(c_ref, w_ref, a_ref, b_ref, m_ref, v_ref, g_ref, d_ref, nm_ref, nv_ref):
        g = jnp.where(pl.program_id(0) == c_ref[0], a_ref[...], b_ref[...])
        g_ref[...] = g
        d_ref[...], nm_ref[...], nv_ref[...] = _adam_math(w_ref[...], g, m_ref[...], v_ref[...])

    wspec = pl.BlockSpec((tr, hc), wmap)
    hspec = pl.BlockSpec((tr, hc), hmap)
    return pl.pallas_call(
        body, name="adamw_" + name,
        grid_spec=pltpu.PrefetchScalarGridSpec(
            num_scalar_prefetch=1, grid=(2, ni),
            in_specs=[wspec, hspec, hspec, wspec, wspec], out_specs=[wspec] * 4),
        out_shape=[jax.ShapeDtypeStruct(w.shape, F32)] * 4,
        compiler_params=pltpu.CompilerParams(dimension_semantics=("parallel", "parallel"), vmem_limit_bytes=VMEM_LIMIT),
    )(cidx, w, mine, theirs, m, v)


SMALL = ("e_norm_w", "e_q_norm_w", "e_k_norm_w", "e_conv_w", "o_norm_w", "o_pool_w", "o_pool_scale",
         "o_dconv_w", "o_dconv_b", "o_ln_w", "o_ln_b")
SMALL_SHARDED = ("e_conv_w", "o_norm_w", "o_pool_scale", "o_dconv_w", "o_dconv_b", "o_ln_w", "o_ln_b")
WEIGHTS = ("e_norm_w", "e_w_in", "e_q_norm_w", "e_k_norm_w", "e_conv_w", "e_w_out", "o_norm_w", "o_w_in",
           "o_pool_w", "o_pool_scale", "o_dconv_w", "o_dconv_b", "o_ln_w", "o_ln_b", "o_w_out")


def _pack(arrs):
    flat = jnp.concatenate([a.reshape(-1) for a in arrs])
    rows = -(-flat.shape[0] // (LANES * SUBLANES)) * SUBLANES
    flat = jnp.pad(flat, (0, rows * LANES - flat.shape[0]))
    return flat.reshape(rows, LANES)


def _unpack(packed, shapes):
    flat = packed.reshape(-1)
    out, off = [], 0
    for s in shapes:
        n = int(np.prod(s))
        out.append(flat[off:off + n].reshape(s))
        off += n
    return out


def _gather_last(a, block, width):
    return lax.dynamic_slice_in_dim(a, block * width, width, axis=a.ndim - 1)


def kernel(x, positions, e_norm_w, e_w_in, e_q_norm_w, e_k_norm_w, e_conv_w, e_w_out, o_norm_w, o_w_in, o_pool_w, o_pool_scale, o_dconv_w, o_dconv_b, o_ln_w, o_ln_b, o_w_out, loss_target, m_e_norm_w, m_e_w_in, m_e_q_norm_w, m_e_k_norm_w, m_e_conv_w, m_e_w_out, m_o_norm_w, m_o_w_in, m_o_pool_w, m_o_pool_scale, m_o_dconv_w, m_o_dconv_b, m_o_ln_w, m_o_ln_b, m_o_w_out, v_e_norm_w, v_e_w_in, v_e_q_norm_w, v_e_k_norm_w, v_e_conv_w, v_e_w_out, v_o_norm_w, v_o_w_in, v_o_pool_w, v_o_pool_scale, v_o_dconv_w, v_o_dconv_b, v_o_ln_w, v_o_ln_b, v_o_w_out):
    given = dict(e_norm_w=e_norm_w, e_w_in=e_w_in, e_q_norm_w=e_q_norm_w, e_k_norm_w=e_k_norm_w, e_conv_w=e_conv_w,
                 e_w_out=e_w_out, o_norm_w=o_norm_w, o_w_in=o_w_in, o_pool_w=o_pool_w, o_pool_scale=o_pool_scale,
                 o_dconv_w=o_dconv_w, o_dconv_b=o_dconv_b, o_ln_w=o_ln_w, o_ln_b=o_ln_b, o_w_out=o_w_out)
    mom = dict(e_norm_w=m_e_norm_w, e_w_in=m_e_w_in, e_q_norm_w=m_e_q_norm_w, e_k_norm_w=m_e_k_norm_w,
               e_conv_w=m_e_conv_w, e_w_out=m_e_w_out, o_norm_w=m_o_norm_w, o_w_in=m_o_w_in, o_pool_w=m_o_pool_w,
               o_pool_scale=m_o_pool_scale, o_dconv_w=m_o_dconv_w, o_dconv_b=m_o_dconv_b, o_ln_w=m_o_ln_w,
               o_ln_b=m_o_ln_b, o_w_out=m_o_w_out)
    var = dict(e_norm_w=v_e_norm_w, e_w_in=v_e_w_in, e_q_norm_w=v_e_q_norm_w, e_k_norm_w=v_e_k_norm_w,
               e_conv_w=v_e_conv_w, e_w_out=v_e_w_out, o_norm_w=v_o_norm_w, o_w_in=v_o_w_in, o_pool_w=v_o_pool_w,
               o_pool_scale=v_o_pool_scale, o_dconv_w=v_o_dconv_w, o_dconv_b=v_o_dconv_b, o_ln_w=v_o_ln_w,
               o_ln_b=v_o_ln_b, o_w_out=v_o_w_out)
    S = x.shape[1]
    mx, my, mc = lax.axis_index("x"), lax.axis_index("y"), lax.axis_index("c")
    chip = 2 * mx + my
    cidx = jnp.reshape(mc, (1,)).astype(jnp.int32)
    bidx = jnp.reshape(chip, (1,)).astype(jnp.int32)

    shards = {n: given[n][0].astype(BF16) for n in BIG}
    shard_sizes = [int(np.prod(given[n].shape)) for n in SMALL_SHARDED]
    flat = jnp.concatenate([given[n].reshape(-1) for n in SMALL_SHARDED])
    rows = -(-flat.shape[0] // (SMALL_PACK_W * SUBLANES)) * SUBLANES
    shards[SMALL_PACK] = jnp.pad(flat, (0, rows * SMALL_PACK_W - flat.shape[0])).reshape(rows, SMALL_PACK_W)

    def unpack_small(full):
        gathered = full.reshape(N_CHIPS, rows * SMALL_PACK_W)
        out, off = {}, 0
        for n, size in zip(SMALL_SHARDED, shard_sizes):
            sh = given[n].shape[1:]
            parts = gathered[:, off:off + size].reshape((N_CHIPS,) + sh)
            fullp = jnp.moveaxis(parts, 0, -2).reshape(sh[:-1] + (N_CHIPS * sh[-1],))
            out[n] = fullp.reshape(-1, fullp.shape[-1])
            off += size
        return out

    p = dict(e_norm_w=e_norm_w, e_q_norm_w=e_q_norm_w, e_k_norm_w=e_k_norm_w, o_pool_w=o_pool_w[0])
    lsum, dx, g, parts = _local_step(x[0], positions.reshape(S, 1), loss_target[0], shards, p, unpack_small,
                                     cidx, bidx)
    mine = [g[n] for n in SMALL] + [(0.5 / float(D_MODEL)) * jnp.sum(lsum, keepdims=True)]
    *theirs, tot = _swap_to_sibling(BIG, [parts[n] for n in BIG], "swap_reduced", False, small=_pack(mine))
    tot = _unpack(tot, [a.shape for a in mine])
    loss = tot[-1].reshape(())

    grads, delta, new_m, new_v = {}, {}, {}, {}
    for n, other in zip(BIG, theirs):
        sh = given[n].shape
        outs = _adamw_halves(cidx, given[n][0], parts[n], other, mom[n][0], var[n][0], n)
        grads[n], delta[n], new_m[n], new_v[n] = [a.reshape(sh) for a in outs]
    for n, gv in zip(SMALL, tot):
        if n in SMALL_SHARDED:
            gv = _gather_last(gv, chip, gv.shape[-1] // N_CHIPS)
        grads[n] = gv.reshape(given[n].shape)
    big_small = "o_pool_w"
    pw = [src[big_small].reshape(-1, LANES) for src in (given, grads, mom, var)]
    for dst, a in zip((delta, new_m, new_v), _adamw(*pw, "pool_w")):
        dst[big_small] = a.reshape(given[big_small].shape)
    tiny = tuple(n for n in SMALL if n != big_small)
    shapes = [given[n].shape for n in tiny]
    packed = [_pack([src[n] for n in tiny]) for src in (given, grads, mom, var)]
    for dst, pk in zip((delta, new_m, new_v), _adamw(*packed, "small")):
        for n, a in zip(tiny, _unpack(pk, shapes)):
            dst[n] = a
    return (loss, dx[None], *[grads[n] for n in WEIGHTS], *[delta[n] for n in WEIGHTS],
            *[new_m[n] for n in WEIGHTS], *[new_v[n] for n in WEIGHTS])
```

```python
import numpy as np
import jax
import jax.numpy as jnp
from jax import lax
from jax.experimental import pallas as pl
from jax.experimental.pallas import tpu as pltpu

F32 = jnp.float32
BF16 = jnp.bfloat16

D_MODEL = 1024
HEAD_DIM = 64
A_WIDTH = 512
A_HEADS = 8
A_GROUPS = ((128, 1), (512, 4), (2048, 16))
QBLK = 128
ROT_DIM = 16
ROPE_THETA = 500000.0
POOL_SIZES = (2, 4, 8, 16)
D_CONV = 31
SC_WIDTH = 3
EVEN_IN = 7168
ODD_IN = 2560
EPS = 1e-6
NEG = -1e30
ADAM_LR, ADAM_B1, ADAM_B2, ADAM_EPS, ADAM_WD, ADAM_STEP = 0.001, 0.9, 0.999, 1e-08, 0.01, 10

LANES = 128
SUBLANES = 8
HALO = 32
VMEM_LIMIT = 52 * 1024 * 1024
MESH = pl.DeviceIdType.MESH
ANY = pl.BlockSpec(memory_space=pl.ANY)

NT_DIMS = (((1,), (1,)), ((), ()))
TN_DIMS = (((0,), (0,)), ((), ()))


def _call(body, name, grid, in_specs, out_specs, out_shape, scratch=(), sem=None, aliases=None):
    return pl.pallas_call(
        body, name=name, grid=grid, in_specs=in_specs, out_specs=out_specs, out_shape=out_shape,
        scratch_shapes=list(scratch), input_output_aliases=aliases or {},
        compiler_params=pltpu.CompilerParams(dimension_semantics=sem, vmem_limit_bytes=VMEM_LIMIT))


def _sig(v):
    return jax.nn.sigmoid(v)


def _dsilu(v, s):
    return s * (1.0 + v * (1.0 - s))


def _out_projection(ut_ref, w_ref, lo, hi):
    acc = None
    for k, v in enumerate((lo, hi)):
        ut_ref[k * A_WIDTH:(k + 1) * A_WIDTH, :] = v.T.astype(BF16)
        part = jnp.dot(v.astype(BF16), w_ref[k * A_WIDTH:(k + 1) * A_WIDTH, :], preferred_element_type=F32)
        acc = part if acc is None else acc + part
    return acc


def _cs8(v):
    return v.reshape(v.shape[0] // SUBLANES, SUBLANES, v.shape[1]).sum(axis=0)


def _seg_mean():
    r = lax.broadcasted_iota(jnp.int32, (LANES, LANES), 0) // HEAD_DIM
    c = lax.broadcasted_iota(jnp.int32, (LANES, LANES), 1) // HEAD_DIM
    return jnp.where(r == c, 1.0 / HEAD_DIM, 0.0).astype(BF16)


def _segsum(v, ones):
    hi = v.astype(BF16)
    lo = (v - hi.astype(F32)).astype(BF16)
    return (jnp.dot(hi, ones, preferred_element_type=F32) + jnp.dot(lo, ones, preferred_element_type=F32))


def _head_rstd(v, seg_mean):
    return lax.rsqrt(jnp.dot((v * v).astype(BF16), seg_mean, preferred_element_type=F32) + EPS)


def _rope_tables(pos_ref, freq_ref):
    ang = pos_ref[...].astype(F32) * freq_ref[...]
    cosv, sinv = jnp.cos(ang), jnp.sin(ang)
    lm = lax.broadcasted_iota(jnp.int32, ang.shape, 1) % HEAD_DIM
    half = ROT_DIM // 2
    c = jnp.where(lm < ROT_DIM, cosv, 1.0)
    s1 = jnp.where((lm >= half) & (lm < ROT_DIM), sinv, 0.0)
    s2 = jnp.where(lm < half, -sinv, 0.0)
    return c, s1, s2


def _freq_table():
    half = ROT_DIM // 2
    inv = ROPE_THETA ** (-np.arange(half, dtype=np.float64) / half)
    lane = np.arange(LANES) % HEAD_DIM
    f = np.where(lane < ROT_DIM, inv[lane % half], 0.0)
    return jnp.asarray(f.reshape(1, LANES), F32)


def _load_once(hbm_ref, vmem_ref, sem):
    @pl.when(pl.program_id(0) == 0)
    def _():
        cp = pltpu.make_async_copy(hbm_ref, vmem_ref, sem)
        cp.start()
        cp.wait()


def _rms_rows(x_ref, nw_ref):
    xv = x_ref[...]
    ms = jnp.mean(xv * xv, axis=-1, keepdims=True)
    return xv * lax.rsqrt(ms + EPS) * nw_ref[...]


def _inproj(x, nw, w, tm, tn, name):
    S, N = x.shape[0], w.shape[1]

    def body(x_ref, nw_ref, w_hbm, o_ref, ht_ref, w_v, sem):
        _load_once(w_hbm, w_v, sem)
        h = _rms_rows(x_ref, nw_ref)
        ht_ref[...] = h.T.astype(BF16)
        hb = h.astype(BF16)
        for j in range(N // tn):
            o_ref[:, j * tn:(j + 1) * tn] = jnp.dot(hb, w_v[:, j * tn:(j + 1) * tn], preferred_element_type=F32)

    return _call(
        body, name, (S // tm,),
        [pl.BlockSpec((tm, D_MODEL), lambda i: (i, 0)),
         pl.BlockSpec((1, D_MODEL), lambda i: (0, 0)), ANY],
        [pl.BlockSpec((tm, N), lambda i: (i, 0)),
         pl.BlockSpec((D_MODEL, tm), lambda i: (0, i))],
        [jax.ShapeDtypeStruct((S, N), F32), jax.ShapeDtypeStruct((D_MODEL, S), BF16)],
        scratch=[pltpu.VMEM(w.shape, BF16), pltpu.SemaphoreType.DMA(())], sem=("arbitrary",))(x, nw, w)


def _inproj_gathering(x, nw, shard, bidx, first, late_names, late_shards, tm, name):
    S = x.shape[0]
    ni = S // tm
    K, cw = shard.shape
    nl = len(late_names)
    last = N_CHIPS - 1

    def body(b_ref, x_ref, nw_ref, s_hbm, *rest):
        ls_refs = rest[:nl]
        o_ref, ht_ref, f_hbm = rest[nl:nl + 3]
        lf_refs = rest[nl + 3:2 * nl + 3]
        hs, w_blk, lsem, send1, recv1, send2, recv2 = rest[2 * nl + 3:]
        j, i = pl.program_id(0), pl.program_id(1)
        g1 = _Gather((first,), (s_hbm,), (f_hbm,), send1, recv1)
        g2 = _Gather(late_names, ls_refs, lf_refs, send2, recv2)
        _, _, _, chips = _place()

        def load_block(src):
            cp = pltpu.make_async_copy(src, w_blk, lsem)
            cp.start()
            cp.wait()

        @pl.when((j == 0) & (i == 0))
        def _():
            g1.begin(relations=(0, 1))
            load_block(s_hbm)

        for r, (cx, cy) in enumerate(chips):
            @pl.when((j == r + 1) & (i == 0))
            def _(r=r, cx=cx, cy=cy):
                g1.wait_relayed(r)
                load_block(_block_of(f_hbm, first, 2 * cx + cy))

        @pl.when((j == 2) & (i == 0))
        def _():
            g1.relay(relations=(2,))
            g2.begin()

        pl.when((j == last) & (i == ni // 2))(g2.relay)

        rows = pl.ds(pl.multiple_of(i * tm, tm), tm)

        @pl.when(j == 0)
        def _():
            h = _rms_rows(x_ref, nw_ref)
            hs[rows, :] = h.astype(BF16)
            ht_ref[...] = h.T.astype(BF16)

        o_ref[...] = jnp.dot(hs[rows, :], w_blk[...], preferred_element_type=F32)

        @pl.when((j == 0) & (i == ni - 1))
        def _():
            g1.relay(relations=(0, 1))
            g1.begin(relations=(2,), sibling=False)

        @pl.when((j == last) & (i == ni - 1))
        def _():
            g1.end_rest()
            g2.end()

    def block_of_step(j, b_ref):
        return jnp.bitwise_xor(b_ref[0], jnp.bitwise_or(jnp.left_shift(jnp.bitwise_and(j, 1), 1), jnp.right_shift(j, 1)))

    outs = pl.pallas_call(
        body, name=name,
        grid_spec=pltpu.PrefetchScalarGridSpec(
            num_scalar_prefetch=1, grid=(N_CHIPS, ni),
            in_specs=[pl.BlockSpec((tm, D_MODEL), lambda j, i, b: (jnp.where(j == 0, i, 0), 0)),
                      pl.BlockSpec((1, D_MODEL), lambda j, i, b: (0, 0)), ANY] + [ANY] * nl,
            out_specs=[pl.BlockSpec((tm, cw), lambda j, i, b: (i, block_of_step(j, b))),
                       pl.BlockSpec((D_MODEL, tm), lambda j, i, b: (0, jnp.where(j == 0, i, ni - 1))),
                       ANY] + [ANY] * nl,
            scratch_shapes=[pltpu.VMEM((S, D_MODEL), BF16), pltpu.VMEM((K, cw), BF16), pltpu.SemaphoreType.DMA(())] +
            _gather_sems((first,)) + _gather_sems(late_names)),
        out_shape=[jax.ShapeDtypeStruct((S, cw * N_CHIPS), F32), jax.ShapeDtypeStruct((D_MODEL, S), BF16),
                   _full_shape(first, shard)] + [_full_shape(n, s) for n, s in zip(late_names, late_shards)],
        compiler_params=pltpu.CompilerParams(dimension_semantics=("arbitrary", "arbitrary"),
                                             vmem_limit_bytes=VMEM_LIMIT),
    )(bidx, x, nw, shard, *late_shards)
    return outs[0], outs[1], outs[2], list(outs[3:])


def _piece_blocks(pieces, tk, axis):
    starts, counts, s = [], [], 0
    for p in pieces:
        n = p.shape[axis] // tk
        starts.append(s)
        counts.append(n)
        s += n
    return starts, counts, s


def _mm_nt_rms(pieces, w, x, nw, dres, tm, name, scatter_names=(), scatter_halves=()):
    S = x.shape[0]
    npc = len(pieces)
    ni = S // tm
    ns = len(scatter_names)
    offs = np.cumsum([0] + [p.shape[1] for p in pieces]).tolist()

    def body(*refs):
        p_refs = refs[:npc]
        w_hbm, x_ref, nw_ref, dr_ref = refs[npc:npc + 4]
        h_refs = refs[npc + 4:npc + 4 + ns]
        dx_ref, dnw_ref = refs[npc + 4 + ns:npc + 6 + ns]
        r_refs = refs[npc + 6 + ns:npc + 6 + 2 * ns]
        w_v, sem, nacc = refs[npc + 6 + 2 * ns:npc + 9 + 2 * ns]
        i = pl.program_id(0)
        if ns:
            send, recv = refs[npc + 9 + 2 * ns:]

            @pl.when(i == 0)
            def _():
                for cp in _scatter_copies(scatter_names, h_refs, r_refs, send, recv):
                    cp.start()
        _load_once(w_hbm, w_v, sem)

        @pl.when(i == 0)
        def _():
            nacc[...] = jnp.zeros_like(nacc)

        dh = None
        for p in range(npc):
            part = lax.dot_general(p_refs[p][...].astype(BF16), w_v[:, offs[p]:offs[p + 1]], NT_DIMS,
                                   preferred_element_type=F32)
            dh = part if dh is None else dh + part
        xv = x_ref[...]
        rs = lax.rsqrt(jnp.mean(xv * xv, axis=-1, keepdims=True) + EPS)
        xh = xv * rs
        nacc[...] += _cs8(dh * xh)
        dxh = dh * nw_ref[...]
        dx_ref[...] = dr_ref[...] + rs * (dxh - xh * jnp.mean(dxh * xh, axis=-1, keepdims=True))

        @pl.when(i == ni - 1)
        def _():
            dnw_ref[...] = jnp.sum(nacc[...], axis=0, keepdims=True)
            if ns:
                for cp in _scatter_copies(scatter_names, h_refs, r_refs, send, recv):
                    cp.wait()

    row = pl.BlockSpec((tm, D_MODEL), lambda i: (i, 0))
    outs = _call(
        body, name, (ni,),
        [pl.BlockSpec((tm, p.shape[1]), lambda i: (i, 0)) for p in pieces] +
        [ANY, row, pl.BlockSpec((1, D_MODEL), lambda i: (0, 0)), row] + [ANY] * ns,
        [row, pl.BlockSpec((1, D_MODEL), lambda i: (0, 0))] + [ANY] * ns,
        [jax.ShapeDtypeStruct((S, D_MODEL), F32), jax.ShapeDtypeStruct((1, D_MODEL), F32)] +
        [jax.ShapeDtypeStruct((3,) + _shard_shape(h.shape, n), h.dtype) for n, h in zip(scatter_names, scatter_halves)],
        scratch=[pltpu.VMEM(w.shape, BF16), pltpu.SemaphoreType.DMA(()), pltpu.VMEM((SUBLANES, D_MODEL), F32)] +
        (_scatter_sems(scatter_names) if ns else []),
        sem=("arbitrary",))(*pieces, w, x, nw, dres, *scatter_halves)
    return outs[0], outs[1], list(outs[2:])


def _mm_wgrad(at, pieces, tn, name):
    M, S = at.shape
    starts, counts, nj = _piece_blocks(pieces, tn, 1)
    npc = len(pieces)

    def body(*refs):
        a_hbm = refs[0]
        p_refs = refs[1:1 + npc]
        o_ref, o16_ref, a_v, sem = refs[1 + npc:]
        j = pl.program_id(0)
        _load_once(a_hbm, a_v, sem)
        for p in range(npc):
            @pl.when((j >= starts[p]) & (j < starts[p] + counts[p]))
            def _(p=p):
                acc = jnp.dot(a_v[...], p_refs[p][...].astype(BF16), preferred_element_type=F32)
                o_ref[...] = acc
                o16_ref[...] = acc.astype(BF16)

    def pspec(p):
        return pl.BlockSpec((S, tn), lambda j: (0, jnp.clip(j - starts[p], 0, counts[p] - 1)))

    col = pl.BlockSpec((M, tn), lambda j: (0, j))
    return _call(
        body, name, (nj,),
        [ANY] + [pspec(p) for p in range(npc)], [col, col],
        [jax.ShapeDtypeStruct((M, nj * tn), F32), jax.ShapeDtypeStruct((M, nj * tn), BF16)],
        scratch=[pltpu.VMEM(at.shape, BF16), pltpu.SemaphoreType.DMA(())], sem=("arbitrary",))(at, *pieces)


def _stream_spec(d, T):
    return pl.BlockSpec((d, T // d, A_WIDTH), lambda i: (0, i, 0))


def _stream_shape(d, S, dtype):
    return jax.ShapeDtypeStruct((d, S // d, A_WIDTH), dtype)


N_CHUNK = A_WIDTH // LANES


def _to_tokens(ref, scr, d, T):
    if d == 1:
        return ref[0].astype(F32)
    for r in range(d):
        for ch in range(N_CHUNK):
            scr.at[ch][pl.ds(r, T // d, stride=d), :] = ref[r, :, ch * LANES:(ch + 1) * LANES].astype(F32)
    return _get(scr)


def _from_tokens(out_ref, scr, d, T):
    for r in range(d):
        for ch in range(N_CHUNK):
            out_ref[r, :, ch * LANES:(ch + 1) * LANES] = scr.at[ch][pl.ds(r, T // d, stride=d), :].astype(out_ref.dtype)


def _put(scr, val):
    for ch in range(N_CHUNK):
        scr[ch] = val[:, ch * LANES:(ch + 1) * LANES]


def _get(scr):
    return jnp.concatenate([scr[ch] for ch in range(N_CHUNK)], axis=1)


def _chunked(T):
    return pltpu.VMEM((N_CHUNK, T, LANES), F32)


def _compact_spec(d, T):
    return pl.BlockSpec((d, T // d, LANES), lambda i: (0, i, 0))


def _compact_shape(d, S):
    return jax.ShapeDtypeStruct((d, S // d, LANES), F32)


def _compact_to_tokens(ref, scr, d, T):
    if d == 1:
        return ref[0]
    for r in range(d):
        scr[pl.ds(r, T // d, stride=d), :] = ref[r]
    return scr[...]


def _compact_from_tokens(out_ref, scr, val, d, T):
    if d == 1:
        out_ref[0] = val
        return
    scr[...] = val
    for r in range(d):
        out_ref[r] = scr[pl.ds(r, T // d, stride=d), :]


def _head_expander():
    r = lax.broadcasted_iota(jnp.int32, (LANES, A_WIDTH), 0)
    c = lax.broadcasted_iota(jnp.int32, (LANES, A_WIDTH), 1) // HEAD_DIM
    return (r == c).astype(BF16)


def _head_reducer():
    r = lax.broadcasted_iota(jnp.int32, (A_WIDTH, LANES), 0) // HEAD_DIM
    c = lax.broadcasted_iota(jnp.int32, (A_WIDTH, LANES), 1)
    return (r == c).astype(BF16)


def _qkv_prep(proj, pos, freq, wq, wk, T):
    S = proj.shape[0]
    qk_w = 3 * A_WIDTH

    def body(q_ref, k_ref, v_ref, pos_ref, f_ref, wq_ref, wk_ref, *rest):
        outs, tabs, scr = rest[:9], rest[9:12], rest[12]
        seg_mean = _seg_mean()
        c, s1, s2 = _rope_tables(pos_ref, f_ref)
        for tab, val in zip(tabs, (c, s1, s2)):
            tab[...] = val
        for t, (src, w_ref) in enumerate(((q_ref, wq_ref), (k_ref, wk_ref), (v_ref, None))):
            for g in range(3):
                d = A_GROUPS[g][1]
                out = outs[3 * t + g]
                for ch in range(A_WIDTH // LANES):
                    cs = slice(ch * LANES, (ch + 1) * LANES)
                    v = src[:, g * A_WIDTH + ch * LANES: g * A_WIDTH + (ch + 1) * LANES]
                    if w_ref is not None:
                        y = v * _head_rstd(v, seg_mean) * w_ref[...]
                        v = y * c + pltpu.roll(y, 8, 1) * s1 + pltpu.roll(y, LANES - 8, 1) * s2
                    if d == 1:
                        out[0, :, cs] = v.astype(BF16)
                    else:
                        scr[ch] = v
                if d > 1:
                    _from_tokens(out, scr, d, T)

    ds_ = [A_GROUPS[g][1] for g in range(3)] * 3
    return _call(
        body, "qkv_prep", (S // T,),
        [pl.BlockSpec((T, qk_w), lambda i: (i, 0)), pl.BlockSpec((T, qk_w), lambda i: (i, 1)),
         pl.BlockSpec((T, qk_w), lambda i: (i, 2)),
         pl.BlockSpec((T, 1), lambda i: (i, 0)), pl.BlockSpec((1, LANES), lambda i: (0, 0)),
         pl.BlockSpec((1, LANES), lambda i: (0, 0)), pl.BlockSpec((1, LANES), lambda i: (0, 0))],
        [_stream_spec(d, T) for d in ds_] + [pl.BlockSpec((T, LANES), lambda i: (i, 0))] * 3,
        [_stream_shape(d, S, BF16) for d in ds_] + [jax.ShapeDtypeStruct((S, LANES), F32)] * 3,
        scratch=[_chunked(T)], sem=("parallel",))(proj, proj, proj, pos, freq, wq, wk)


def _attn_mask(i):
    qi = lax.broadcasted_iota(jnp.int32, (QBLK, 2 * QBLK), 0) + QBLK
    kj = lax.broadcasted_iota(jnp.int32, (QBLK, 2 * QBLK), 1)
    dist = qi - kj
    return (dist >= 0) & (dist <= QBLK) & ((i > 0) | (kj >= QBLK))


ATT_BLK = (None, QBLK, A_WIDTH)
ATT_CBLK = (None, QBLK, LANES)
FWD_BLOCKS = 4


def _first_head_lanes():
    return lax.broadcasted_iota(jnp.int32, (1, LANES), 1) < HEAD_DIM


def _split_heads(v, first):
    zero = jnp.zeros_like(v)
    return jnp.where(first, v, zero), jnp.where(first, zero, v)


def _attn_fwd(q, k, v, g):
    d, n, _ = q.shape
    nb = n // QBLK
    assert nb % FWD_BLOCKS == 0, (n, QBLK)
    rows2 = FWD_BLOCKS * QBLK

    def body(q_ref, kp_ref, kc_ref, vp_ref, vc_ref, o_ref, l_ref, s_scr, p_scr):
        i = pl.program_id(1)
        masks = (_attn_mask(i),) + (_attn_mask(1),) * (FWD_BLOCKS - 1)
        first = _first_head_lanes()

        def keys(prev_ref, cur_ref, b, ps):
            if b == 0:
                return jnp.concatenate([prev_ref[:, ps], cur_ref[0:QBLK, ps]], axis=0)
            return cur_ref[(b - 1) * QBLK:(b + 1) * QBLK, ps]

        for b in range(FWD_BLOCKS):
            rows = slice(b * QBLK, (b + 1) * QBLK)
            for pr in range(A_HEADS // 2):
                ps = slice(pr * LANES, (pr + 1) * LANES)
                kc = keys(kp_ref, kc_ref, b, ps)
                for e, qh in enumerate(_split_heads(q_ref[rows, ps], first)):
                    s_scr[b * A_HEADS + 2 * pr + e] = lax.dot_general(qh, kc, NT_DIMS, preferred_element_type=F32)
        lane = lax.broadcasted_iota(jnp.int32, (1, LANES), 1)
        for b in range(FWD_BLOCKS):
            lrow = jnp.zeros((QBLK, LANES), F32)
            for h in range(A_HEADS):
                s = jnp.where(masks[b], s_scr[b * A_HEADS + h] * (HEAD_DIM ** -0.5), NEG)
                m = jnp.max(s, axis=-1, keepdims=True)
                p = jnp.exp(s - m)
                den = jnp.sum(p, axis=-1, keepdims=True)
                p_scr[b * A_HEADS + h] = (p / den).astype(BF16)
                lrow = jnp.where(lane == h, m + jnp.log(den), lrow)
            l_ref[b * QBLK:(b + 1) * QBLK, :] = lrow
        for b in range(FWD_BLOCKS):
            for pr in range(A_HEADS // 2):
                ps = slice(pr * LANES, (pr + 1) * LANES)
                va, vb = _split_heads(keys(vp_ref, vc_ref, b, ps), first)
                o_ref[b * QBLK:(b + 1) * QBLK, ps] = (
                    jnp.dot(p_scr[b * A_HEADS + 2 * pr], va, preferred_element_type=F32) +
                    jnp.dot(p_scr[b * A_HEADS + 2 * pr + 1], vb, preferred_element_type=F32)).astype(BF16)

    prev = lambda r, i: (r, jnp.maximum(FWD_BLOCKS * i - 1, 0), 0)
    cur = lambda r, i: (r, i, 0)
    wide = (None, rows2, A_WIDTH)
    units = FWD_BLOCKS * A_HEADS
    return _call(
        body, "attn_fwd_g%d" % g, (d, nb // FWD_BLOCKS),
        [pl.BlockSpec(wide, cur), pl.BlockSpec(ATT_BLK, prev), pl.BlockSpec(wide, cur),
         pl.BlockSpec(ATT_BLK, prev), pl.BlockSpec(wide, cur)],
        [pl.BlockSpec(wide, cur), pl.BlockSpec((None, rows2, LANES), cur)],
        [jax.ShapeDtypeStruct((d, n, A_WIDTH), BF16), jax.ShapeDtypeStruct((d, n, LANES), F32)],
        scratch=[pltpu.VMEM((units, QBLK, 2 * QBLK), F32), pltpu.VMEM((units, QBLK, 2 * QBLK), BF16)],
        sem=("parallel", "parallel"))(q, k, k, v, v)


def _attn_bwd(q, k, v, do, lse, cg, g):
    d, n, _ = q.shape
    nb = n // QBLK
    scale = HEAD_DIM ** -0.5

    def body(q_ref, kp_ref, kc_ref, vp_ref, vc_ref, do_ref, l_ref, c_ref, dq_ref, dk_ref, dv_ref, ck, cv,
             s_scr, dp_scr, p_scr, ds_scr):
        i = pl.program_id(1)

        @pl.when(i == 0)
        def _():
            ck[...] = jnp.zeros_like(ck)
            cv[...] = jnp.zeros_like(cv)

        @pl.when(i < nb)
        def _():
            mask = _attn_mask(i)
            first = _first_head_lanes()
            for pr in range(A_HEADS // 2):
                ps = slice(pr * LANES, (pr + 1) * LANES)
                kc = jnp.concatenate([kp_ref[:, ps], kc_ref[:, ps]], axis=0)
                vc = jnp.concatenate([vp_ref[:, ps], vc_ref[:, ps]], axis=0)
                qs = _split_heads(q_ref[:, ps], first)
                dos = _split_heads(do_ref[:, ps], first)
                for e in range(2):
                    s_scr[2 * pr + e] = lax.dot_general(qs[e], kc, NT_DIMS, preferred_element_type=F32)
                    dp_scr[2 * pr + e] = lax.dot_general(dos[e], vc, NT_DIMS, preferred_element_type=F32)
            for h in range(A_HEADS):
                p = jnp.where(mask, jnp.exp(s_scr[h] * scale - l_ref[:, h:h + 1]), 0.0)
                p_scr[h] = p.astype(BF16)
                ds_scr[h] = (p * (dp_scr[h] + c_ref[:, h:h + 1]) * scale).astype(BF16)
            for pr in range(A_HEADS // 2):
                ps = slice(pr * LANES, (pr + 1) * LANES)
                ks = _split_heads(jnp.concatenate([kp_ref[:, ps], kc_ref[:, ps]], axis=0), first)
                qs = _split_heads(q_ref[:, ps], first)
                dos = _split_heads(do_ref[:, ps], first)
                dq = dkc = dvc = None
                for e in range(2):
                    ds = ds_scr[2 * pr + e]
                    a = jnp.dot(ds, ks[e], preferred_element_type=F32)
                    b = lax.dot_general(ds, qs[e], TN_DIMS, preferred_element_type=F32)
                    c = lax.dot_general(p_scr[2 * pr + e], dos[e], TN_DIMS, preferred_element_type=F32)
                    dq, dkc, dvc = (a, b, c) if e == 0 else (dq + a, dkc + b, dvc + c)
                dq_ref[:, ps] = dq.astype(BF16)
                dk_ref[:, ps] = (ck[:, ps] + dkc[:QBLK]).astype(BF16)
                dv_ref[:, ps] = (cv[:, ps] + dvc[:QBLK]).astype(BF16)
                ck[:, ps] = dkc[QBLK:]
                cv[:, ps] = dvc[QBLK:]

        @pl.when(i == nb)
        def _():
            dk_ref[...] = ck[...].astype(BF16)
            dv_ref[...] = cv[...].astype(BF16)

    qi = lambda i: jnp.minimum(i, nb - 1)
    cur = lambda r, i: (r, qi(i), 0)
    prev = lambda r, i: (r, jnp.maximum(qi(i) - 1, 0), 0)
    late = lambda r, i: (r, jnp.maximum(i - 1, 0), 0)
    return _call(
        body, "attn_bwd_g%d" % g, (d, nb + 1),
        [pl.BlockSpec(ATT_BLK, cur), pl.BlockSpec(ATT_BLK, prev), pl.BlockSpec(ATT_BLK, cur),
         pl.BlockSpec(ATT_BLK, prev), pl.BlockSpec(ATT_BLK, cur),
         pl.BlockSpec(ATT_BLK, cur), pl.BlockSpec(ATT_CBLK, cur), pl.BlockSpec(ATT_CBLK, cur)],
        [pl.BlockSpec(ATT_BLK, cur), pl.BlockSpec(ATT_BLK, late), pl.BlockSpec(ATT_BLK, late)],
        [jax.ShapeDtypeStruct((d, n, A_WIDTH), BF16)] * 3,
        scratch=[pltpu.VMEM((QBLK, A_WIDTH), F32), pltpu.VMEM((QBLK, A_WIDTH), F32),
                 pltpu.VMEM((A_HEADS, QBLK, 2 * QBLK), F32), pltpu.VMEM((A_HEADS, QBLK, 2 * QBLK), F32),
                 pltpu.VMEM((A_HEADS, QBLK, 2 * QBLK), BF16), pltpu.VMEM((A_HEADS, QBLK, 2 * QBLK), BF16)],
        sem=("parallel", "arbitrary"))(q, k, k, v, v, do, lse, cg)


def _merge_weights(l0, l1, l2):
    mx = jnp.maximum(jnp.maximum(l0, l1), l2)
    e0, e1, e2 = jnp.exp(l0 - mx), jnp.exp(l1 - mx), jnp.exp(l2 - mx)
    den = e0 + e1 + e2
    return e0 / den, e1 / den, e2 / den


def _even_specs(T, S):
    t8 = T // SUBLANES
    last8 = S // SUBLANES - 1
    col = lambda c: pl.BlockSpec((T, A_WIDTH), lambda i: (i, c))
    prev8 = lambda c: pl.BlockSpec((SUBLANES, A_WIDTH), lambda i: (jnp.maximum(i * t8 - 1, 0), c))
    next8 = lambda c: pl.BlockSpec((SUBLANES, A_WIDTH), lambda i: (jnp.minimum((i + 1) * t8, last8), c))
    return col, prev8, next8


GROUP_D = tuple(d for _, d in A_GROUPS)


def _even_mixer_fwd(x, proj, os_, ls_, conv_w, w_out, T):
    S = proj.shape[0]
    col, prev8, _ = _even_specs(T, S)
    H = SUBLANES

    def body(x_ref, w_ref, bg_r, cg_r, hb_r, zl_r, zh_r, cgp_r, hbp_r, o0, o1, o2, l0, l1, l2, cw_r,
             x1_ref, ut_ref, ext, cscr, *scr):
        i = pl.program_id(0)
        ls = [_compact_to_tokens(r, cscr, GROUP_D[g], T) for g, r in enumerate((l0, l1, l2))]
        expand = _head_expander()
        ws = [_segsum(w, expand) for w in _merge_weights(*ls)]
        oa = ws[0] * _to_tokens(o0, scr[0], GROUP_D[0], T)
        oa = oa + ws[1] * _to_tokens(o1, scr[1], GROUP_D[1], T)
        oa = oa + ws[2] * _to_tokens(o2, scr[2], GROUP_D[2], T)
        ext[0:H, :] = jnp.where(i == 0, 0.0, cgp_r[...] * hbp_r[...])
        ext[H:H + T, :] = cg_r[...] * hb_r[...]
        conv = cw_r[0:1, :] * ext[H - 2:H - 2 + T, :]
        for kk in range(1, SC_WIDTH):
            conv = conv + cw_r[kk:kk + 1, :] * ext[H - 2 + kk:H - 2 + kk + T, :]
        zl, zh = zl_r[...], zh_r[...]
        x1_ref[...] = x_ref[...] + _out_projection(ut_ref, w_ref, oa * (zl * _sig(zl)),
                                                   bg_r[...] * conv * (zh * _sig(zh)))

    streams = [_stream_spec(d, T) for d in GROUP_D]
    compacts = [_compact_spec(d, T) for d in GROUP_D]
    row = pl.BlockSpec((T, D_MODEL), lambda i: (i, 0))
    return _call(
        body, "even_mixer_fwd", (S // T,),
        [row, pl.BlockSpec((D_MODEL, D_MODEL), lambda i: (0, 0)),
         col(9), col(10), col(11), col(12), col(13), prev8(10), prev8(11)] + streams + compacts +
        [pl.BlockSpec((SC_WIDTH, A_WIDTH), lambda i: (0, 0))],
        [row, pl.BlockSpec((D_MODEL, T), lambda i: (0, i))],
        [jax.ShapeDtypeStruct((S, D_MODEL), F32), jax.ShapeDtypeStruct((D_MODEL, S), BF16)],
        scratch=[pltpu.VMEM((T + H, A_WIDTH), F32), pltpu.VMEM((T, LANES), F32)] + [_chunked(T)] * 3,
        sem=("parallel",))(
            x, w_out, proj, proj, proj, proj, proj, proj, proj, *os_, *ls_, conv_w)


def _even_mixer_bwd(dy, w_out, proj, os_, ls_, conv_w, T):
    S = proj.shape[0]
    nt = S // T
    col, prev8, next8 = _even_specs(T, S)
    H = SUBLANES
    t8 = T // SUBLANES
    last8 = S // SUBLANES - 1

    def body(dy_r, dyn_r, w_ref, bg_r, cg_r, hb_r, zl_r, zh_r, cgp_r, hbp_r, zhn_r, bgn_r,
             o0, o1, o2, l0, l1, l2, cw_r,
             do0, do1, do2, c0, c1, c2, dr_ref, dcw_ref, ext_t, ext_d, acc, cscr, s_a, s_b, s_c):
        i = pl.program_id(0)

        @pl.when(i == 0)
        def _():
            acc[...] = jnp.zeros_like(acc)

        zl, zh = zl_r[...], zh_r[...]
        sl, sh = _sig(zl), _sig(zh)
        du = lax.dot_general(dy_r[...].astype(BF16), w_ref[...], NT_DIMS, preferred_element_type=F32)
        dul, duh = du[:, 0:A_WIDTH], du[:, A_WIDTH:]
        dun = lax.dot_general(dyn_r[...].astype(BF16), w_ref[A_WIDTH:, :], NT_DIMS, preferred_element_type=F32)
        scr = (s_a, s_b, s_c)
        ls = [_compact_to_tokens(r, cscr, GROUP_D[g], T) for g, r in enumerate((l0, l1, l2))]
        wcs = _merge_weights(*ls)
        expand = _head_expander()
        ws = [_segsum(w, expand) for w in wcs]
        oa = ws[0] * _to_tokens(o0, scr[0], GROUP_D[0], T)
        oa = oa + ws[1] * _to_tokens(o1, scr[1], GROUP_D[1], T)
        oa = oa + ws[2] * _to_tokens(o2, scr[2], GROUP_D[2], T)
        doa = dul * (zl * sl)
        rsum = _segsum(doa * oa, _head_reducer())
        for g, (do_ref, c_ref) in enumerate(((do0, c0), (do1, c1), (do2, c2))):
            d = GROUP_D[g]
            _compact_from_tokens(c_ref, cscr, -wcs[g] * rsum, d, T)
            if d == 1:
                do_ref[0] = (ws[g] * doa).astype(BF16)
            else:
                _put(s_c, ws[g] * doa)
                _from_tokens(do_ref, s_c, d, T)
        cgv, hbv, bgv = cg_r[...], hb_r[...], bg_r[...]
        ext_t[0:H, :] = jnp.where(i == 0, 0.0, cgp_r[...] * hbp_r[...])
        ext_t[H:H + T, :] = cgv * hbv
        conv = cw_r[0:1, :] * ext_t[H - 2:H - 2 + T, :]
        for kk in range(1, SC_WIDTH):
            conv = conv + cw_r[kk:kk + 1, :] * ext_t[H - 2 + kk:H - 2 + kk + T, :]
        dyb = duh * (zh * sh)
        dconv = dyb * bgv
        zn = zhn_r[...]
        ext_d[0:T, :] = dconv
        ext_d[T:T + H, :] = jnp.where(i == nt - 1, 0.0, dun * (zn * _sig(zn)) * bgn_r[...])
        dt = cw_r[0:1, :] * ext_d[2:2 + T, :]
        for kk in range(1, SC_WIDTH):
            dt = dt + cw_r[kk:kk + 1, :] * ext_d[2 - kk:2 - kk + T, :]
        for kk in range(SC_WIDTH):
            acc[kk * SUBLANES:(kk + 1) * SUBLANES, :] += _cs8(dconv * ext_t[H - 2 + kk:H - 2 + kk + T, :])
        dr_ref[:, 0:A_WIDTH] = (dyb * conv).astype(BF16)
        dr_ref[:, A_WIDTH:2 * A_WIDTH] = (dt * hbv).astype(BF16)
        dr_ref[:, 2 * A_WIDTH:3 * A_WIDTH] = (dt * cgv).astype(BF16)
        dr_ref[:, 3 * A_WIDTH:4 * A_WIDTH] = (dul * oa * _dsilu(zl, sl)).astype(BF16)
        dr_ref[:, 4 * A_WIDTH:5 * A_WIDTH] = (duh * (bgv * conv) * _dsilu(zh, sh)).astype(BF16)

        @pl.when(i == nt - 1)
        def _():
            for kk in range(SC_WIDTH):
                dcw_ref[kk:kk + 1, :] = jnp.sum(acc[kk * SUBLANES:(kk + 1) * SUBLANES, :], axis=0, keepdims=True)

    streams = [_stream_spec(d, T) for d in GROUP_D]
    dynext = pl.BlockSpec((SUBLANES, D_MODEL), lambda i: (jnp.minimum((i + 1) * t8, last8), 0))
    compacts = [_compact_spec(d, T) for d in GROUP_D]
    outs = _call(
        body, "even_mixer_bwd", (nt,),
        [pl.BlockSpec((T, D_MODEL), lambda i: (i, 0)), dynext, pl.BlockSpec((D_MODEL, D_MODEL), lambda i: (0, 0)),
         col(9), col(10), col(11), col(12), col(13), prev8(10), prev8(11), next8(13), next8(9)] +
        streams + compacts + [pl.BlockSpec((SC_WIDTH, A_WIDTH), lambda i: (0, 0))],
        streams + compacts + [pl.BlockSpec((T, 5 * A_WIDTH), lambda i: (i, 0)),
                              pl.BlockSpec((SC_WIDTH, A_WIDTH), lambda i: (0, 0))],
        [_stream_shape(d, S, BF16) for d in GROUP_D] + [_compact_shape(d, S) for d in GROUP_D] +
        [jax.ShapeDtypeStruct((S, 5 * A_WIDTH), BF16), jax.ShapeDtypeStruct((SC_WIDTH, A_WIDTH), F32)],
        scratch=[pltpu.VMEM((T + H, A_WIDTH), F32), pltpu.VMEM((T + H, A_WIDTH), F32),
                 pltpu.VMEM((SC_WIDTH * SUBLANES, A_WIDTH), F32), pltpu.VMEM((T, LANES), F32)] +
                [_chunked(T)] * 3,
        sem=("arbitrary",))(dy, dy, w_out, proj, proj, proj, proj, proj, proj, proj, proj, proj, *os_, *ls_, conv_w)
    return outs[0:3], outs[3:6], outs[6], outs[7]


def _qk_bwd(proj, dqs, dks, dvs, rope, wq, wk, T):
    S = proj.shape[0]
    nt = S // T
    qk_w = 3 * A_WIDTH

    def body(q_ref, k_ref, dq0, dq1, dq2, dk0, dk1, dk2, dv0, dv1, dv2, c_ref, s1_ref, s2_ref, wq_ref, wk_ref,
             o_ref, dw_ref, acc, scr):
        i = pl.program_id(0)

        @pl.when(i == 0)
        def _():
            acc[...] = jnp.zeros_like(acc)
            dw_ref[...] = jnp.zeros_like(dw_ref)

        seg_mean = _seg_mean()
        c, s1, s2 = c_ref[...], s1_ref[...], s2_ref[...]
        for t, (src, w_ref, ds) in enumerate(((q_ref, wq_ref, (dq0, dq1, dq2)), (k_ref, wk_ref, (dk0, dk1, dk2)))):
            wv = w_ref[...]
            for g in range(3):
                d = GROUP_D[g]
                if d > 1:
                    _to_tokens(ds[g], scr, d, T)
                for ch in range(A_WIDTH // LANES):
                    cs = slice(g * A_WIDTH + ch * LANES, g * A_WIDTH + (ch + 1) * LANES)
                    lc = slice(ch * LANES, (ch + 1) * LANES)
                    v = src[:, cs]
                    dout = ds[g][0, :, lc].astype(F32) if d == 1 else scr[ch]
                    rs = lax.rsqrt(_segsum(v * v, seg_mean) + EPS)
                    xh = v * rs
                    dy = dout * c + pltpu.roll(dout * s1, LANES - 8, 1) + pltpu.roll(dout * s2, 8, 1)
                    acc[t * SUBLANES:(t + 1) * SUBLANES, :] += _cs8(dy * xh)
                    dxh = dy * wv
                    mean = _segsum(dxh * xh, seg_mean)
                    o_ref[:, t * qk_w + g * A_WIDTH + ch * LANES: t * qk_w + g * A_WIDTH + (ch + 1) * LANES] = (
                        rs * (dxh - xh * mean)).astype(BF16)
        for g, dv in enumerate((dv0, dv1, dv2)):
            d = GROUP_D[g]
            base = 2 * qk_w + g * A_WIDTH
            o_ref[:, base:base + A_WIDTH] = _to_tokens(dv, scr, d, T).astype(BF16)

        @pl.when(i == nt - 1)
        def _():
            for t in range(2):
                srow = jnp.sum(acc[t * SUBLANES:(t + 1) * SUBLANES, :], axis=0, keepdims=True)
                dw_ref[t:t + 1, :] = srow + pltpu.roll(srow, HEAD_DIM, 1)

    streams = [_stream_spec(d, T) for d in GROUP_D]
    return _call(
        body, "qk_bwd", (nt,),
        [pl.BlockSpec((T, qk_w), lambda i: (i, 0)), pl.BlockSpec((T, qk_w), lambda i: (i, 1))] + streams * 3 +
        [pl.BlockSpec((T, LANES), lambda i: (i, 0))] * 3 +
        [pl.BlockSpec((1, LANES), lambda i: (0, 0)), pl.BlockSpec((1, LANES), lambda i: (0, 0))],
        [pl.BlockSpec((T, 3 * qk_w), lambda i: (i, 0)), pl.BlockSpec((SUBLANES, LANES), lambda i: (0, 0))],
        [jax.ShapeDtypeStruct((S, 3 * qk_w), BF16), jax.ShapeDtypeStruct((SUBLANES, LANES), F32)],
        scratch=[pltpu.VMEM((2 * SUBLANES, LANES), F32), _chunked(T)], sem=("arbitrary",))(
            proj, proj, *dqs, *dks, *dvs, *rope, wq, wk)


N_SMALL_ODD = 40
SHIFT_ROWS_LESS = SUBLANES


def _fill_shifted(ext_ref, sh_ref):
    rows = ext_ref.shape[0] - SHIFT_ROWS_LESS
    for b in range(1, SUBLANES):
        sh_ref[b - 1] = ext_ref[b:b + rows, :]


TAP_ROWS = 64


def _window(ext_ref, sh_ref, off, rows, r0=0):
    a, b = divmod(off, SUBLANES)
    if b == 0:
        return ext_ref[r0 + off:r0 + off + rows, :]
    return sh_ref[b - 1, r0 + a * SUBLANES:r0 + a * SUBLANES + rows, :]


def _odd_pool_tile(i, uc_r, ucp_r, pw_r, ext_u, pooled_s, pm_s, T):
    H = HALO
    uc = uc_r[...]
    ext_u[0:H, :] = jnp.where(i == 0, 0.0, ucp_r[...])
    ext_u[H:H + T, :] = uc
    row = i * T + lax.broadcasted_iota(jnp.int32, (T, 1), 0)
    for g, p in enumerate(POOL_SIZES):
        cs = slice(g * LANES, (g + 1) * LANES)
        win = ext_u[H:H + T, cs]
        for j in range(1, p):
            win = win + ext_u[H - j:H - j + T, cs]
        cnt = jnp.minimum(row + 1, p).astype(F32)
        pooled = win / cnt - uc[:, cs]
        pooled_s[:, cs] = pooled
        pm_s[:, cs] = jnp.dot(pooled.astype(BF16), pw_r[g].astype(BF16), preferred_element_type=F32)
    return row


def _odd_glu_tile(i, da_r, dg_r, dap_r, dgp_r, ext_g, sh_g, T):
    H = HALO
    ext_g[0:H, :] = jnp.where(i == 0, 0.0, dap_r[...] * _sig(dgp_r[...]))
    ext_g[H:H + T, :] = da_r[...] * _sig(dg_r[...])
    _fill_shifted(ext_g, sh_g)


def _odd_specs(T, S, order):
    tb = T // HALO
    col = lambda c: pl.BlockSpec((T, A_WIDTH), lambda s: (order(s), c))
    prev = lambda c: pl.BlockSpec((HALO, A_WIDTH), lambda s: (jnp.maximum(order(s) * tb - 1, 0), c))
    const2 = lambda shape: pl.BlockSpec(shape, lambda s: (0, 0))
    weights = [pl.BlockSpec((4, LANES, LANES), lambda s: (0, 0, 0)), const2((1, A_WIDTH)),
               const2((D_CONV, A_WIDTH)), const2((1, A_WIDTH)), const2((1, A_WIDTH)), const2((1, A_WIDTH))]
    return col, prev, weights


def _odd_mixer_fwd(x, tgt, proj, pool_w, scale, dconv_w, dconv_b, ln_w, ln_b, w_out, T):
    S = proj.shape[0]
    nt = S // T
    col, prev, wspecs = _odd_specs(T, S, lambda s: s)
    H = HALO

    def body(x_ref, t_ref, w_ref, uc_r, da_r, dg_r, zl_r, zh_r, ucp_r, dap_r, dgp_r, pw_r, sc_r, dw_r, db_r,
             lw_r, lb_r, dy_ref, l_ref, ut_ref, cv_ref, ext_u, ext_g, sh_g, pooled_s, pm_s, lacc):
        i = pl.program_id(0)

        @pl.when(i == 0)
        def _():
            lacc[...] = jnp.zeros_like(lacc)

        _odd_pool_tile(i, uc_r, ucp_r, pw_r, ext_u, pooled_s, pm_s, T)
        _odd_glu_tile(i, da_r, dg_r, dap_r, dgp_r, ext_g, sh_g, T)
        base = H - (D_CONV - 1)
        conv = db_r[...] + dw_r[0:1, :] * _window(ext_g, sh_g, base, T)
        for kk in range(1, D_CONV):
            conv = conv + dw_r[kk:kk + 1, :] * _window(ext_g, sh_g, base + kk, T)
        cv_ref[...] = conv
        mu = jnp.mean(conv, axis=-1, keepdims=True)
        xc = conv - mu
        yh = xc * lax.rsqrt(jnp.mean(xc * xc, axis=-1, keepdims=True) + EPS)
        ln = yh * lw_r[...] + lb_r[...]
        zl, zh = zl_r[...], zh_r[...]
        y = x_ref[...] + _out_projection(ut_ref, w_ref, pm_s[...] * sc_r[...] * (zl * _sig(zl)),
                                         ln * _sig(ln) * (zh * _sig(zh)))
        diff = y - t_ref[...]
        dy_ref[...] = diff / float(D_MODEL)
        lacc[...] += _cs8(diff * diff)

        @pl.when(i == nt - 1)
        def _():
            l_ref[...] = jnp.sum(lacc[...], axis=0, keepdims=True)

    row = pl.BlockSpec((T, D_MODEL), lambda i: (i, 0))
    return _call(
        body, "odd_mixer_fwd", (nt,),
        [row, row, pl.BlockSpec((D_MODEL, D_MODEL), lambda i: (0, 0)),
         col(0), col(1), col(2), col(3), col(4), prev(0), prev(1), prev(2)] + wspecs,
        [row, pl.BlockSpec((1, D_MODEL), lambda i: (0, 0)), pl.BlockSpec((D_MODEL, T), lambda i: (0, i)),
         pl.BlockSpec((T, A_WIDTH), lambda i: (i, 0))],
        [jax.ShapeDtypeStruct((S, D_MODEL), F32), jax.ShapeDtypeStruct((1, D_MODEL), F32),
         jax.ShapeDtypeStruct((D_MODEL, S), BF16), jax.ShapeDtypeStruct((S, A_WIDTH), F32)],
        scratch=[pltpu.VMEM((T + H, A_WIDTH), F32), pltpu.VMEM((T + H, A_WIDTH), F32),
                 pltpu.VMEM((SUBLANES - 1, T + H - SHIFT_ROWS_LESS, A_WIDTH), F32),
                 pltpu.VMEM((T, A_WIDTH), F32), pltpu.VMEM((T, A_WIDTH), F32),
                 pltpu.VMEM((SUBLANES, D_MODEL), F32)],
        sem=("arbitrary",))(x, tgt, w_out, proj, proj, proj, proj, proj, proj, proj, proj,
                            pool_w, scale, dconv_w, dconv_b, ln_w, ln_b)


def _odd_mixer_bwd(dy, w_out, proj, conv, pool_w, scale, dconv_w, dconv_b, ln_w, ln_b, T):
    S = proj.shape[0]
    nt = S // T
    order = lambda s: nt - 1 - s
    col, prev, wspecs = _odd_specs(T, S, order)
    H = HALO

    def body(dy_r, w_ref, cv_r, uc_r, da_r, dg_r, zl_r, zh_r, ucp_r, dap_r, dgp_r, pw_r, sc_r, dw_r, db_r, lw_r, lb_r,
             dp_ref, dpw_ref, sm_ref, ext_u, ext_g, sh_g, pooled_s, pm_s, dpl_s, ext_p, ext_c, sh_c, acc):
        step = pl.program_id(0)
        i = nt - 1 - step

        @pl.when(step == 0)
        def _():
            ext_p[T:T + H, :] = jnp.zeros((H, A_WIDTH), F32)
            ext_c[T:T + H, :] = jnp.zeros((H, A_WIDTH), F32)
            acc[...] = jnp.zeros_like(acc)
            dpw_ref[...] = jnp.zeros_like(dpw_ref)

        def accum(r, v):
            acc[r * SUBLANES:(r + 1) * SUBLANES, :] += _cs8(v)

        row = _odd_pool_tile(i, uc_r, ucp_r, pw_r, ext_u, pooled_s, pm_s, T)
        _odd_glu_tile(i, da_r, dg_r, dap_r, dgp_r, ext_g, sh_g, T)
        conv = cv_r[...]
        mu = jnp.mean(conv, axis=-1, keepdims=True)
        xc = conv - mu
        rstd = lax.rsqrt(jnp.mean(xc * xc, axis=-1, keepdims=True) + EPS)
        yh = xc * rstd
        ln = yh * lw_r[...] + lb_r[...]
        sln = _sig(ln)
        zl, zh = zl_r[...], zh_r[...]
        sl, sh = _sig(zl), _sig(zh)
        du = lax.dot_general(dy_r[...].astype(BF16), w_ref[...], NT_DIMS, preferred_element_type=F32)
        dul, duh = du[:, 0:A_WIDTH], du[:, A_WIDTH:]
        pm = pm_s[...]
        scv = sc_r[...]
        dyc = dul * (zl * sl)
        accum(34, dyc * pm)
        dpm = dyc * scv
        for g in range(len(POOL_SIZES)):
            cs = slice(g * LANES, (g + 1) * LANES)
            dpm_g = dpm[:, cs].astype(BF16)
            dpw_ref[g] += lax.dot_general(pooled_s[:, cs].astype(BF16), dpm_g, TN_DIMS, preferred_element_type=F32)
            dpl_s[:, cs] = lax.dot_general(dpm_g, pw_r[g].astype(BF16), NT_DIMS, preferred_element_type=F32)
        lane_p = lax.broadcasted_iota(jnp.int32, (1, A_WIDTH), 1) // LANES
        pvec = jnp.left_shift(2, lane_p)
        cnt = jnp.minimum(row + 1, pvec).astype(F32)
        dpl = dpl_s[...]
        ext_p[0:T, :] = dpl / cnt
        for g, p in enumerate(POOL_SIZES):
            cs = slice(g * LANES, (g + 1) * LANES)
            win = ext_p[0:T, cs]
            for j in range(1, p):
                win = win + ext_p[j:j + T, cs]
            dp_ref[:, cs] = (win - dpl[:, cs]).astype(BF16)
        ext_p[T:T + H, :] = ext_p[0:H, :]
        dln = duh * (zh * sh) * _dsilu(ln, sln)
        accum(32, dln * yh)
        accum(33, dln)
        dyh = dln * lw_r[...]
        dc = rstd * (dyh - jnp.mean(dyh, axis=-1, keepdims=True) - yh * jnp.mean(dyh * yh, axis=-1, keepdims=True))
        accum(31, dc)
        ext_c[0:T, :] = dc
        _fill_shifted(ext_c, sh_c)
        base = H - (D_CONV - 1)
        for r0 in range(0, T, TAP_ROWS):
            dc_c = ext_c[r0:r0 + TAP_ROWS, :]
            dgl_c = None
            for kk in range(D_CONV):
                term = dw_r[kk:kk + 1, :] * _window(ext_c, sh_c, D_CONV - 1 - kk, TAP_ROWS, r0)
                dgl_c = term if dgl_c is None else dgl_c + term
                accum(kk, dc_c * _window(ext_g, sh_g, base + kk, TAP_ROWS, r0))
            dpl_s[r0:r0 + TAP_ROWS, :] = dgl_c
        dgl = dpl_s[...]
        ext_c[T:T + H, :] = ext_c[0:H, :]
        dav, dgv = da_r[...], dg_r[...]
        sg = _sig(dgv)
        dp_ref[:, A_WIDTH:2 * A_WIDTH] = (dgl * sg).astype(BF16)
        dp_ref[:, 2 * A_WIDTH:3 * A_WIDTH] = (dgl * dav * sg * (1.0 - sg)).astype(BF16)
        dp_ref[:, 3 * A_WIDTH:4 * A_WIDTH] = (dul * (pm * scv) * _dsilu(zl, sl)).astype(BF16)
        dp_ref[:, 4 * A_WIDTH:5 * A_WIDTH] = (duh * (ln * sln) * _dsilu(zh, sh)).astype(BF16)

        @pl.when(step == nt - 1)
        def _():
            for r in range(N_SMALL_ODD):
                sm_ref[r:r + 1, :] = jnp.sum(acc[r * SUBLANES:(r + 1) * SUBLANES, :], axis=0, keepdims=True)

    ext = pltpu.VMEM((T + H, A_WIDTH), F32)
    shifted = pltpu.VMEM((SUBLANES - 1, T + H - SHIFT_ROWS_LESS, A_WIDTH), F32)
    tile = pltpu.VMEM((T, A_WIDTH), F32)
    return _call(
        body, "odd_mixer_bwd", (nt,),
        [pl.BlockSpec((T, D_MODEL), lambda s: (order(s), 0)), pl.BlockSpec((D_MODEL, D_MODEL), lambda s: (0, 0)),
         pl.BlockSpec((T, A_WIDTH), lambda s: (order(s), 0)),
         col(0), col(1), col(2), col(3), col(4), prev(0), prev(1), prev(2)] + wspecs,
        [pl.BlockSpec((T, ODD_IN), lambda s: (order(s), 0)),
         pl.BlockSpec((4, LANES, LANES), lambda s: (0, 0, 0)),
         pl.BlockSpec((N_SMALL_ODD, A_WIDTH), lambda s: (0, 0))],
        [jax.ShapeDtypeStruct((S, ODD_IN), BF16), jax.ShapeDtypeStruct((4, LANES, LANES), F32),
         jax.ShapeDtypeStruct((N_SMALL_ODD, A_WIDTH), F32)],
        scratch=[ext, ext, shifted, tile, tile, tile, ext, ext, shifted,
                 pltpu.VMEM((N_SMALL_ODD * SUBLANES, A_WIDTH), F32)],
        sem=("arbitrary",))(dy, w_out, conv, proj, proj, proj, proj, proj, proj, proj, proj,
                            pool_w, scale, dconv_w, dconv_b, ln_w, ln_b)


TILE_SEQ = 512
TILE_WG = 256
TILE_FIRST = 1024
TILE_MM = 512


SMALL_PACK = "small_pack"
SMALL_PACK_W = 2 * LANES
LATE_WEIGHTS = ("e_w_out", "o_w_in", "o_w_out", SMALL_PACK)
ODD_MATS = ("o_w_in", "o_w_out")
EVEN_MATS = ("e_w_in", "e_w_out")


def _reduce_start(names, grads, grads16, cidx):
    recv = _swap_to_sibling(names, [grads16[n] for n in names], "swap_halves_" + names[0][0], True)
    both = [_add_half(cidx, grads[n], r, n) for n, r in zip(names, recv)]
    return [h for h, _ in both], [hb for _, hb in both]


def _local_step(x, pos, tgt, shards, p, unpack_small, cidx, bidx):
    T = TILE_SEQ
    freq = _freq_table()
    wq = jnp.tile(p["e_q_norm_w"], (1, LANES // HEAD_DIM))
    wk = jnp.tile(p["e_k_norm_w"], (1, LANES // HEAD_DIM))

    proj_e, ht_e, w_e_in, late = _inproj_gathering(x, p["e_norm_w"], shards["e_w_in"], bidx, "e_w_in", LATE_WEIGHTS,
                                                   [shards[n] for n in LATE_WEIGHTS], TILE_FIRST, "inproj_even")
    wb = dict(zip(LATE_WEIGHTS, late), e_w_in=w_e_in)
    p = dict(p, **unpack_small(wb[SMALL_PACK]))
    qkv = _qkv_prep(proj_e, pos, freq, wq, wk, T)
    qs, ks, vs, rope = qkv[0:3], qkv[3:6], qkv[6:9], qkv[9:12]
    os_, ls_ = [], []
    for g in range(3):
        o, l = _attn_fwd(qs[g], ks[g], vs[g], g)
        os_.append(o)
        ls_.append(l)
    x1, ut_e = _even_mixer_fwd(x, proj_e, os_, ls_, p["e_conv_w"], wb["e_w_out"], T)
    proj_o, ht_o = _inproj(x1, p["o_norm_w"], wb["o_w_in"], TILE_MM, 1280, "inproj_odd")
    odd_w = (p["o_pool_w"], p["o_pool_scale"], p["o_dconv_w"], p["o_dconv_b"], p["o_ln_w"], p["o_ln_b"])
    dy, lsum, ut_o, conv_o = _odd_mixer_fwd(x1, tgt, proj_o, *odd_w, wb["o_w_out"], T)

    g, g16 = {}, {}
    g["o_w_out"], g16["o_w_out"] = _mm_wgrad(ut_o, [dy], TILE_WG, "wgrad_o_out")
    dproj_o, g["o_pool_w"], small_o = _odd_mixer_bwd(dy, wb["o_w_out"], proj_o, conv_o, *odd_w, T)
    g["o_w_in"], g16["o_w_in"] = _mm_wgrad(ht_o, [dproj_o], TILE_WG, "wgrad_o_in")
    half_o, half_o16 = _reduce_start(ODD_MATS, g, g16, cidx)
    dx1, g["o_norm_w"], blocks_o = _mm_nt_rms([dproj_o], wb["o_w_in"], x1, p["o_norm_w"], dy, TILE_MM, "dx_odd",
                                              ODD_MATS, half_o16)
    g["o_dconv_w"] = small_o[0:D_CONV]
    g["o_dconv_b"] = small_o[31:32]
    g["o_ln_w"] = small_o[32:33]
    g["o_ln_b"] = small_o[33:34]
    g["o_pool_scale"] = small_o[34:35]

    g["e_w_out"], g16["e_w_out"] = _mm_wgrad(ut_e, [dx1], TILE_WG, "wgrad_e_out")
    dos, cgs, drest, g["e_conv_w"] = _even_mixer_bwd(dx1, wb["e_w_out"], proj_e, os_, ls_, p["e_conv_w"], T)
    dqs, dks, dvs = [], [], []
    for gi in range(3):
        dq, dk, dv = _attn_bwd(qs[gi], ks[gi], vs[gi], dos[gi], ls_[gi], cgs[gi], gi)
        dqs.append(dq)
        dks.append(dk)
        dvs.append(dv)
    dqkv, dnw = _qk_bwd(proj_e, dqs, dks, dvs, rope, wq, wk, T)
    g["e_q_norm_w"] = dnw[0:1, 0:HEAD_DIM]
    g["e_k_norm_w"] = dnw[1:2, 0:HEAD_DIM]
    pieces = [dqkv, drest]
    g["e_w_in"], g16["e_w_in"] = _mm_wgrad(ht_e, pieces, TILE_WG, "wgrad_e_in")
    half_e, half_e16 = _reduce_start(EVEN_MATS, g, g16, cidx)
    dx, g["e_norm_w"], blocks_e = _mm_nt_rms(pieces, wb["e_w_in"], x, p["e_norm_w"], dx1, TILE_MM, "dx_even",
                                             EVEN_MATS, half_e16)
    parts = {}
    for names, halves, blocks in ((ODD_MATS, half_o, blocks_o), (EVEN_MATS, half_e, blocks_e)):
        for n, h, r in zip(names, halves, blocks):
            parts[n] = _add_blocks(bidx, h, r, n)
    return lsum, dx, g, parts


BIG = ("e_w_in", "e_w_out", "o_w_in", "o_w_out")
SHARD_AXIS = {"e_w_in": 1, "e_w_out": 0, "o_w_in": 1, "o_w_out": 0, SMALL_PACK: 0}
N_CHIPS = 4


def _place():
    x, y, c = lax.axis_index("x"), lax.axis_index("y"), lax.axis_index("c")
    chips = [(1 - x, y), (x, 1 - y), (1 - x, 1 - y)]
    return x, y, c, chips


def _block_of(ref, name, block):
    rows, cols = ref.shape
    if SHARD_AXIS[name] == 1:
        cw = cols // N_CHIPS
        return ref.at[:, pl.ds(pl.multiple_of(block * cw, LANES), cw)]
    rw = rows // N_CHIPS
    return ref.at[pl.ds(pl.multiple_of(block * rw, rw), rw), :]


def _half_of(ref, name, half):
    rows, cols = ref.shape
    if SHARD_AXIS[name] == 1:
        return ref.at[pl.ds(pl.multiple_of(half * (rows // 2), rows // 2), rows // 2), :]
    return ref.at[:, pl.ds(pl.multiple_of(half * (cols // 2), LANES), cols // 2)]


def _sub(ref, name, block, half):
    rows, cols = ref.shape
    if SHARD_AXIS[name] == 1:
        cw, hr = cols // N_CHIPS, rows // 2
        return ref.at[pl.ds(pl.multiple_of(half * hr, hr), hr), pl.ds(pl.multiple_of(block * cw, LANES), cw)]
    rw, hc = rows // N_CHIPS, cols // 2
    return ref.at[pl.ds(pl.multiple_of(block * rw, rw), rw), pl.ds(pl.multiple_of(half * hc, LANES), hc)]


GATHER_COPIES = 7


class _Gather:
    def __init__(self, names, s_refs, f_refs, send, recv):
        self.names, self.s, self.f, self.send, self.recv = names, s_refs, f_refs, send, recv

    def _copy(self, k, src, dst, to):
        return pltpu.make_async_remote_copy(src_ref=src, dst_ref=dst, send_sem=self.send.at[k],
                                            recv_sem=self.recv.at[k], device_id=to, device_id_type=MESH)

    def _plan(self):
        x, y, c, chips = _place()
        me, sib = 2 * x + y, (x, y, 1 - c)
        first, relay_in, relay, last_in = [], [], [], []
        for wi, n in enumerate(self.names):
            k0 = wi * GATHER_COPIES
            s, f = self.s[wi], self.f[wi]
            own = _block_of(f, n, me)
            first.append(self._copy(k0 + 3, s, own, sib))
            last_in.append(self._copy(k0 + 3, s, own, sib))
            for j, (cx, cy) in enumerate(chips):
                first.append(self._copy(k0 + j, _half_of(s, n, c), _sub(f, n, me, c), (cx, cy, c)))
                mine = _sub(f, n, 2 * cx + cy, c)
                relay_in.append(self._copy(k0 + j, mine, mine, sib))
                relay.append(self._copy(k0 + 4 + j, mine, mine, sib))
                theirs = _sub(f, n, 2 * cx + cy, 1 - c)
                last_in.append(self._copy(k0 + 4 + j, theirs, theirs, sib))
        return first, relay_in, relay, last_in

    N_RELATIONS = 3

    def begin(self, relations=(0, 1, 2), sibling=True):
        first = self._plan()[0]
        for wi in range(len(self.names)):
            mine = first[wi * (1 + self.N_RELATIONS):(wi + 1) * (1 + self.N_RELATIONS)]
            if sibling:
                mine[0].start()
            for j in relations:
                mine[1 + j].start()

    def relay(self, relations=(0, 1, 2)):
        _, relay_in, relay, _ = self._plan()
        for wi in range(len(self.names)):
            for j in relations:
                relay_in[wi * self.N_RELATIONS + j].wait_recv()
                relay[wi * self.N_RELATIONS + j].start()

    def end(self):
        first, _, relay, last_in = self._plan()
        for cp in last_in:
            cp.wait_recv()
        for cp in first + relay:
            cp.wait_send()

    def wait_relayed(self, j):
        self._plan()[3][1 + j].wait_recv()

    def end_rest(self):
        first, _, relay, last_in = self._plan()
        last_in[0].wait_recv()
        for cp in first + relay:
            cp.wait_send()


def _full_shape(n, s):
    r, cdim = s.shape
    return jax.ShapeDtypeStruct((r, cdim * N_CHIPS) if SHARD_AXIS[n] == 1 else (r * N_CHIPS, cdim), s.dtype)


def _gather_sems(names):
    k = GATHER_COPIES * len(names)
    return [pltpu.SemaphoreType.DMA((k,)), pltpu.SemaphoreType.DMA((k,))]


def _scatter_copies(names, h_refs, r_refs, send, recv):
    _, _, c, chips = _place()
    cps = []
    for wi, n in enumerate(names):
        for j, (cx, cy) in enumerate(chips):
            cps.append(pltpu.make_async_remote_copy(
                src_ref=_block_of(h_refs[wi], n, 2 * cx + cy), dst_ref=r_refs[wi].at[j],
                send_sem=send.at[wi * 3 + j], recv_sem=recv.at[wi * 3 + j],
                device_id=(cx, cy, c), device_id_type=MESH))
    return cps


def _scatter_sems(names):
    return [pltpu.SemaphoreType.DMA((3 * len(names),)), pltpu.SemaphoreType.DMA((3 * len(names),))]


class _SmallSum:
    def __init__(self, p_ref, o_ref, sbuf, cbuf, send, recv):
        self.p, self.o, self.sbuf, self.cbuf, self.send, self.recv = p_ref, o_ref, sbuf, cbuf, send, recv

    def _copy(self, k, ref, to):
        return pltpu.make_async_remote_copy(src_ref=ref, dst_ref=ref, send_sem=self.send.at[k],
                                            recv_sem=self.recv.at[k], device_id=to, device_id_type=MESH)

    def _plan(self):
        x, y, c, chips = _place()
        me, sib = 2 * x + y, (x, y, 1 - c)
        d2d_out = self._copy(0, self.sbuf.at[c], sib)
        d2d_in = self._copy(0, self.sbuf.at[1 - c], sib)
        ici_out = [self._copy(1 + j, self.cbuf.at[me], (cx, cy, c)) for j, (cx, cy) in enumerate(chips)]
        ici_in = [self._copy(1 + j, self.cbuf.at[2 * cx + cy], (cx, cy, c)) for j, (cx, cy) in enumerate(chips)]
        return c, me, d2d_out, d2d_in, ici_out, ici_in

    def begin(self):
        c, _, d2d_out, _, _, _ = self._plan()
        self.sbuf[c] = self.p[...]
        d2d_out.start()

    def middle(self):
        _, me, _, d2d_in, ici_out, _ = self._plan()
        d2d_in.wait_recv()
        self.cbuf[me] = self.sbuf[0] + self.sbuf[1]
        for cp in ici_out:
            cp.start()

    def end(self):
        _, _, d2d_out, _, ici_out, ici_in = self._plan()
        for cp in ici_in:
            cp.wait_recv()
        self.o[...] = (self.cbuf[0] + self.cbuf[1]) + (self.cbuf[2] + self.cbuf[3])
        for cp in [d2d_out] + ici_out:
            cp.wait_send()


def _small_sum_scratch(R):
    return [pltpu.VMEM((2, R, LANES), F32), pltpu.VMEM((N_CHIPS, R, LANES), F32),
            pltpu.SemaphoreType.DMA((4,)), pltpu.SemaphoreType.DMA((4,))]


def _half_shape(shape, name):
    r, cdim = shape
    return (r // 2, cdim) if SHARD_AXIS[name] == 1 else (r, cdim // 2)


def _shard_shape(shape, name):
    r, cdim = shape
    return (r, cdim // N_CHIPS) if SHARD_AXIS[name] == 1 else (r // N_CHIPS, cdim)


def _swap_to_sibling(names, srcs, name, pick_half, small=None):
    nw = len(names)
    ns = 0 if small is None else 1
    vm = pl.BlockSpec(memory_space=pltpu.VMEM)

    def body(*refs):
        g_refs = refs[:nw]
        r_refs = refs[nw + ns:2 * nw + ns]
        send, recv = refs[2 * nw + 2 * ns:2 * nw + 2 * ns + 2]
        x, y, c, _ = _place()
        sib = (x, y, 1 - c)
        cps = []
        for wi, n in enumerate(names):
            src = _half_of(g_refs[wi], n, 1 - c) if pick_half else g_refs[wi]
            cp = pltpu.make_async_remote_copy(src_ref=src, dst_ref=r_refs[wi], send_sem=send.at[wi],
                                              recv_sem=recv.at[wi], device_id=sib, device_id_type=MESH)
            cp.start()
            cps.append(cp)
        if ns:
            total = _SmallSum(refs[nw], refs[2 * nw + ns], *refs[2 * nw + 2 * ns + 2:])
            total.begin()
            total.middle()
            total.end()
        for cp in cps:
            cp.wait()

    outs = [jax.ShapeDtypeStruct(_half_shape(g.shape, n) if pick_half else g.shape, g.dtype)
            for n, g in zip(names, srcs)]
    return pl.pallas_call(
        body, name=name, in_specs=[ANY] * nw + [vm] * ns, out_specs=[ANY] * nw + [vm] * ns,
        out_shape=outs + ([jax.ShapeDtypeStruct(small.shape, F32)] if ns else []),
        scratch_shapes=[pltpu.SemaphoreType.DMA((nw,)), pltpu.SemaphoreType.DMA((nw,))] +
        (_small_sum_scratch(small.shape[0]) if ns else []),
    )(*srcs, *([small] if ns else []))


def _add_half(cidx, g, r, name):
    rows, cols = r.shape
    tr = 256
    tc = cols if cols <= 1792 else (1792 if cols % 1792 == 0 else 1280)
    nr, nc = rows // tr, cols // tc

    def body(c_ref, g_ref, r_ref, o_ref, ob_ref):
        s = g_ref[...] + r_ref[...].astype(F32)
        o_ref[...] = s
        ob_ref[...] = s.astype(BF16)

    if SHARD_AXIS[name] == 1:
        gmap = lambda i, j, c_ref: (c_ref[0] * nr + i, j)
    else:
        gmap = lambda i, j, c_ref: (i, c_ref[0] * nc + j)
    same = lambda i, j, c_ref: (i, j)
    return pl.pallas_call(
        body, name="add_half_" + name,
        grid_spec=pltpu.PrefetchScalarGridSpec(
            num_scalar_prefetch=1, grid=(nr, nc),
            in_specs=[pl.BlockSpec((tr, tc), gmap), pl.BlockSpec((tr, tc), same)],
            out_specs=[pl.BlockSpec((tr, tc), same), pl.BlockSpec((tr, tc), same)]),
        out_shape=[jax.ShapeDtypeStruct(r.shape, F32), jax.ShapeDtypeStruct(r.shape, BF16)],
        compiler_params=pltpu.CompilerParams(dimension_semantics=("parallel", "parallel"), vmem_limit_bytes=VMEM_LIMIT),
    )(cidx, g, r)


def _add_blocks(bidx, h, r, name):
    _, rows, cols = r.shape
    tr = min(rows, 256)
    nr = rows // tr

    def body(b_ref, h_ref, r0, r1, r2, o_ref):
        o_ref[...] = ((h_ref[...] + r0[0].astype(F32)) + r1[0].astype(F32)) + r2[0].astype(F32)

    if SHARD_AXIS[name] == 1:
        hmap = lambda i, b_ref: (i, b_ref[0])
    else:
        hmap = lambda i, b_ref: (b_ref[0] * nr + i, 0)
    rspec = lambda j: pl.BlockSpec((1, tr, cols), lambda i, b_ref, j=j: (j, i, 0))
    return pl.pallas_call(
        body, name="add_blocks_" + name,
        grid_spec=pltpu.PrefetchScalarGridSpec(
            num_scalar_prefetch=1, grid=(nr,),
            in_specs=[pl.BlockSpec((tr, cols), hmap), rspec(0), rspec(1), rspec(2)],
            out_specs=pl.BlockSpec((tr, cols), lambda i, b_ref: (i, 0))),
        out_shape=jax.ShapeDtypeStruct((rows, cols), F32),
        compiler_params=pltpu.CompilerParams(dimension_semantics=("parallel",), vmem_limit_bytes=VMEM_LIMIT),
    )(bidx, h, r, r, r)


def _adam_math(w, g, m, v):
    c1 = 1.0 - ADAM_B1 ** ADAM_STEP
    c2 = 1.0 - ADAM_B2 ** ADAM_STEP
    nm = ADAM_B1 * m + (1.0 - ADAM_B1) * g
    nv = ADAM_B2 * v + (1.0 - ADAM_B2) * (g * g)
    delta = -ADAM_LR * ((nm / c1) / (jnp.sqrt(nv / c2) + ADAM_EPS) + ADAM_WD * w)
    return delta, nm, nv


def _adamw(w, g, m, v, name):
    def body(w_ref, g_ref, m_ref, v_ref, d_ref, nm_ref, nv_ref):
        d_ref[...], nm_ref[...], nv_ref[...] = _adam_math(w_ref[...], g_ref[...], m_ref[...], v_ref[...])

    spec = pl.BlockSpec(w.shape, lambda i: (0, 0))
    return _call(body, "adamw_" + name, (1,), [spec] * 4, [spec] * 3,
                 [jax.ShapeDtypeStruct(w.shape, F32)] * 3, sem=("arbitrary",))(w, g, m, v)


def _adamw_halves(cidx, w, mine, theirs, m, v, name):
    hr, hc = mine.shape
    tr = 128
    ni = hr // tr
    if SHARD_AXIS[name] == 1:
        wmap = lambda hh, i, c_ref: (hh * ni + i, 0)
    else:
        wmap = lambda hh, i, c_ref: (i, hh)
    hmap = lambda hh, i, c_ref: (i, 0)

    def body(c_ref, w_ref, a_ref, b_ref, m_ref, v_ref, g_ref, d_ref, nm_ref, nv_ref):
        g = jnp.where(pl.program_id(0) == c_ref[0], a_ref[...], b_ref[...])
        g_ref[...] = g
        d_ref[...], nm_ref[...], nv_ref[...] = _adam_math(w_ref[...], g, m_ref[...], v_ref[...])

    wspec = pl.BlockSpec((tr, hc), wmap)
    hspec = pl.BlockSpec((tr, hc), hmap)
    return pl.pallas_call(
        body, name="adamw_" + name,
        grid_spec=pltpu.PrefetchScalarGridSpec(
            num_scalar_prefetch=1, grid=(2, ni),
            in_specs=[wspec, hspec, hspec, wspec, wspec], out_specs=[wspec] * 4),
        out_shape=[jax.ShapeDtypeStruct(w.shape, F32)] * 4,
        compiler_params=pltpu.CompilerParams(dimension_semantics=("parallel", "parallel"), vmem_limit_bytes=VMEM_LIMIT),
    )(cidx, w, mine, theirs, m, v)


SMALL = ("e_norm_w", "e_q_norm_w", "e_k_norm_w", "e_conv_w", "o_norm_w", "o_pool_w", "o_pool_scale",
         "o_dconv_w", "o_dconv_b", "o_ln_w", "o_ln_b")
SMALL_SHARDED = ("e_conv_w", "o_norm_w", "o_pool_scale", "o_dconv_w", "o_dconv_b", "o_ln_w", "o_ln_b")
WEIGHTS = ("e_norm_w", "e_w_in", "e_q_norm_w", "e_k_norm_w", "e_conv_w", "e_w_out", "o_norm_w", "o_w_in",
           "o_pool_w", "o_pool_scale", "o_dconv_w", "o_dconv_b", "o_ln_w", "o_ln_b", "o_w_out")


def _pack(arrs):
    flat = jnp.concatenate([a.reshape(-1) for a in arrs])
    rows = -(-flat.shape[0] // (LANES * SUBLANES)) * SUBLANES
    flat = jnp.pad(flat, (0, rows * LANES - flat.shape[0]))
    return flat.reshape(rows, LANES)


def _unpack(packed, shapes):
    flat = packed.reshape(-1)
    out, off = [], 0
    for s in shapes:
        n = int(np.prod(s))
        out.append(flat[off:off + n].reshape(s))
        off += n
    return out


def _gather_last(a, block, width):
    return lax.dynamic_slice_in_dim(a, block * width, width, axis=a.ndim - 1)


def kernel(x, positions, e_norm_w, e_w_in, e_q_norm_w, e_k_norm_w, e_conv_w, e_w_out, o_norm_w, o_w_in, o_pool_w, o_pool_scale, o_dconv_w, o_dconv_b, o_ln_w, o_ln_b, o_w_out, loss_target, m_e_norm_w, m_e_w_in, m_e_q_norm_w, m_e_k_norm_w, m_e_conv_w, m_e_w_out, m_o_norm_w, m_o_w_in, m_o_pool_w, m_o_pool_scale, m_o_dconv_w, m_o_dconv_b, m_o_ln_w, m_o_ln_b, m_o_w_out, v_e_norm_w, v_e_w_in, v_e_q_norm_w, v_e_k_norm_w, v_e_conv_w, v_e_w_out, v_o_norm_w, v_o_w_in, v_o_pool_w, v_o_pool_scale, v_o_dconv_w, v_o_dconv_b, v_o_ln_w, v_o_ln_b, v_o_w_out):
    given = dict(e_norm_w=e_norm_w, e_w_in=e_w_in, e_q_norm_w=e_q_norm_w, e_k_norm_w=e_k_norm_w, e_conv_w=e_conv_w,
                 e_w_out=e_w_out, o_norm_w=o_norm_w, o_w_in=o_w_in, o_pool_w=o_pool_w, o_pool_scale=o_pool_scale,
                 o_dconv_w=o_dconv_w, o_dconv_b=o_dconv_b, o_ln_w=o_ln_w, o_ln_b=o_ln_b, o_w_out=o_w_out)
    mom = dict(e_norm_w=m_e_norm_w, e_w_in=m_e_w_in, e_q_norm_w=m_e_q_norm_w, e_k_norm_w=m_e_k_norm_w,
               e_conv_w=m_e_conv_w, e_w_out=m_e_w_out, o_norm_w=m_o_norm_w, o_w_in=m_o_w_in, o_pool_w=m_o_pool_w,
               o_pool_scale=m_o_pool_scale, o_dconv_w=m_o_dconv_w, o_dconv_b=m_o_dconv_b, o_ln_w=m_o_ln_w,
               o_ln_b=m_o_ln_b, o_w_out=m_o_w_out)
    var = dict(e_norm_w=v_e_norm_w, e_w_in=v_e_w_in, e_q_norm_w=v_e_q_norm_w, e_k_norm_w=v_e_k_norm_w,
               e_conv_w=v_e_conv_w, e_w_out=v_e_w_out, o_norm_w=v_o_norm_w, o_w_in=v_o_w_in, o_pool_w=v_o_pool_w,
               o_pool_scale=v_o_pool_scale, o_dconv_w=v_o_dconv_w, o_dconv_b=v_o_dconv_b, o_ln_w=v_o_ln_w,
               o_ln_b=v_o_ln_b, o_w_out=v_o_w_out)
    S = x.shape[1]
    mx, my, mc = lax.axis_index("x"), lax.axis_index("y"), lax.axis_index("c")
    chip = 2 * mx + my
    cidx = jnp.reshape(mc, (1,)).astype(jnp.int32)
    bidx = jnp.reshape(chip, (1,)).astype(jnp.int32)

    shards = {n: given[n][0].astype(BF16) for n in BIG}
    shard_sizes = [int(np.prod(given[n].shape)) for n in SMALL_SHARDED]
    flat = jnp.concatenate([given[n].reshape(-1) for n in SMALL_SHARDED])
    rows = -(-flat.shape[0] // (SMALL_PACK_W * SUBLANES)) * SUBLANES
    shards[SMALL_PACK] = jnp.pad(flat, (0, rows * SMALL_PACK_W - flat.shape[0])).reshape(rows, SMALL_PACK_W)

    def unpack_small(full):
        gathered = full.reshape(N_CHIPS, rows * SMALL_PACK_W)
        out, off = {}, 0
        for n, size in zip(SMALL_SHARDED, shard_sizes):
            sh = given[n].shape[1:]
            parts = gathered[:, off:off + size].reshape((N_CHIPS,) + sh)
            fullp = jnp.moveaxis(parts, 0, -2).reshape(sh[:-1] + (N_CHIPS * sh[-1],))
            out[n] = fullp.reshape(-1, fullp.shape[-1])
            off += size
        return out

    p = dict(e_norm_w=e_norm_w, e_q_norm_w=e_q_norm_w, e_k_norm_w=e_k_norm_w, o_pool_w=o_pool_w[0])
    lsum, dx, g, parts = _local_step(x[0], positions.reshape(S, 1), loss_target[0], shards, p, unpack_small,
                                     cidx, bidx)
    mine = [g[n] for n in SMALL] + [(0.5 / float(D_MODEL)) * jnp.sum(lsum, keepdims=True)]
    *theirs, tot = _swap_to_sibling(BIG, [parts[n] for n in BIG], "swap_reduced", False, small=_pack(mine))
    tot = _unpack(tot, [a.shape for a in mine])
    loss = tot[-1].reshape(())

    grads, delta, new_m, new_v = {}, {}, {}, {}
    for n, other in zip(BIG, theirs):
        sh = given[n].shape
        outs = _adamw_halves(cidx, given[n][0], parts[n], other, mom[n][0], var[n][0], n)
        grads[n], delta[n], new_m[n], new_v[n] = [a.reshape(sh) for a in outs]
    for n, gv in zip(SMALL, tot):
        if n in SMALL_SHARDED:
            gv = _gather_last(gv, chip, gv.shape[-1] // N_CHIPS)
        grads[n] = gv.reshape(given[n].shape)
    big_small = "o_pool_w"
    pw = [src[big_small].reshape(-1, LANES) for src in (given, grads, mom, var)]
    for dst, a in zip((delta, new_m, new_v), _adamw(*pw, "pool_w")):
        dst[big_small] = a.reshape(given[big_small].shape)
    tiny = tuple(n for n in SMALL if n != big_small)
    shapes = [given[n].shape for n in tiny]
    packed = [_pack([src[n] for n in tiny]) for src in (given, grads, mom, var)]
    for dst, pk in zip((delta, new_m, new_v), _adamw(*packed, "small")):
        for n, a in zip(tiny, _unpack(pk, shapes)):
            dst[n] = a
    return (loss, dx[None], *[grads[n] for n in WEIGHTS], *[delta[n] for n in WEIGHTS],
            *[new_m[n] for n in WEIGHTS], *[new_v[n] for n in WEIGHTS])
```
